```python
import jax, jax.numpy as jnp
from jax import lax
import numpy as np

D_MODEL = 1024
BATCH = 8
SEQ = 4096
DEPTH = 4

N_MIXERS = 2
N_ATTN_LAYERS = (DEPTH + 1) // 2
N_HGRN_LAYERS = DEPTH // 2
HEAD_DIM = 64
N_Q_HEADS = D_MODEL // HEAD_DIM
N_KV_HEADS = N_Q_HEADS // 4
Q_PER_KV = N_Q_HEADS // N_KV_HEADS
WINDOW = 128
ATTN_BLOCK = WINDOW
ATTN_IN = (N_Q_HEADS + 2 * N_KV_HEADS) * HEAD_DIM
HG_EXPAND = 128
HG_HEADS = D_MODEL // HG_EXPAND
HG_K = HG_EXPAND
HG_V = D_MODEL // HG_HEADS
HG_FDIM = HG_HEADS * HG_K
HG_IDIM = HG_HEADS * HG_V
HG_IN = 2 * HG_FDIM + 2 * HG_IDIM
HG_CHUNK = 64
D_FF = 2816
CONV_W = 3
EPS = 1e-6

kernel_name = 'hybrid_swa_sink_alibi_hgrn2_convffn'


def rmsnorm(x, g):
    xf = x.astype(jnp.float32)
    y = xf * lax.rsqrt(jnp.mean(xf * xf, axis=-1, keepdims=True) + EPS)
    return (y * g.astype(jnp.float32)).astype(x.dtype)


def alibi_slopes():
    h = jnp.arange(1, N_Q_HEADS + 1, dtype=jnp.float32)
    return jnp.exp2(-8.0 * h / N_Q_HEADS)


def sliding_window_attention(h, w_in, w_out, sinks):
    B, S, _ = h.shape
    nblk = S // ATTN_BLOCK
    proj = h @ w_in
    q, k, v = jnp.split(proj, [N_Q_HEADS * HEAD_DIM, (N_Q_HEADS + N_KV_HEADS) * HEAD_DIM], axis=-1)
    q = q.reshape(B, nblk, ATTN_BLOCK, N_KV_HEADS, Q_PER_KV, HEAD_DIM)
    k = k.reshape(B, S, N_KV_HEADS, HEAD_DIM)
    v = v.reshape(B, S, N_KV_HEADS, HEAD_DIM)
    pad = jnp.zeros((B, ATTN_BLOCK, N_KV_HEADS, HEAD_DIM), k.dtype)
    kb = jnp.concatenate([pad, k], axis=1).reshape(B, nblk + 1, ATTN_BLOCK, N_KV_HEADS, HEAD_DIM)
    vb = jnp.concatenate([pad, v], axis=1).reshape(B, nblk + 1, ATTN_BLOCK, N_KV_HEADS, HEAD_DIM)
    kw = jnp.concatenate([kb[:, :-1], kb[:, 1:]], axis=2)
    vw = jnp.concatenate([vb[:, :-1], vb[:, 1:]], axis=2)
    scale = HEAD_DIM ** -0.5
    scores = jnp.einsum('bnqhgd,bnkhd->bnhgqk', q, kw).astype(jnp.float32) * scale
    qi = jnp.arange(ATTN_BLOCK)[:, None]
    ki = jnp.arange(2 * ATTN_BLOCK)[None, :]
    dist = qi + ATTN_BLOCK - ki
    in_window = (dist >= 0) & (dist < WINDOW)
    key_pos = (jnp.arange(nblk) * ATTN_BLOCK - ATTN_BLOCK)[:, None, None] + ki[None]
    valid = in_window[None] & (key_pos >= 0)
    slopes = alibi_slopes().reshape(N_KV_HEADS, Q_PER_KV)
    bias = -slopes[:, :, None, None] * dist.astype(jnp.float32)
    scores = jnp.where(valid[None, :, None, None], scores + bias, -jnp.inf)
    sink = jnp.broadcast_to(sinks.astype(jnp.float32).reshape(1, 1, N_KV_HEADS, Q_PER_KV, 1, 1),
                            scores.shape[:-1] + (1,))
    p = jax.nn.softmax(jnp.concatenate([scores, sink], axis=-1), axis=-1)[..., :-1]
    o = jnp.einsum('bnhgqk,bnkhd->bnqhgd', p.astype(vw.dtype), vw)
    o = o.reshape(B, S, N_Q_HEADS * HEAD_DIM)
    return o @ w_out


def hgrn2(h, w_in, w_out, norm_g, lb):
    B, S, _ = h.shape
    nc = S // HG_CHUNK
    proj = h @ w_in
    q, f, i, g = jnp.split(proj, [HG_FDIM, 2 * HG_FDIM, 2 * HG_FDIM + HG_IDIM], axis=-1)
    q = jax.nn.silu(q.astype(jnp.float32))
    f = f.astype(jnp.float32)
    lb = lb.astype(jnp.float32)
    log_f = jnp.logaddexp(jnp.log(lb), jnp.log1p(-lb) + jax.nn.log_sigmoid(f))
    k = (1.0 - lb) * jax.nn.sigmoid(-f)

    def to_chunks(t, d):
        return t.reshape(B, nc, HG_CHUNK, HG_HEADS, d).transpose(1, 0, 3, 2, 4)

    qc = to_chunks(q, HG_K)
    kc = to_chunks(k, HG_K)
    gc = to_chunks(log_f, HG_K)
    vc = to_chunks(i.astype(jnp.float32), HG_V)
    causal = jnp.tril(jnp.ones((HG_CHUNK, HG_CHUNK), bool))

    def step(state, inp):
        qt, kt, gt, vt = inp
        b = jnp.cumsum(gt, axis=2)
        rel = b[:, :, :, None, :] - b[:, :, None, :, :]
        decay = jnp.exp(jnp.where(causal[:, :, None], rel, -jnp.inf))
        a = jnp.einsum('bhtk,bhsk,bhtsk->bhts', qt, kt, decay)
        o = jnp.einsum('bhts,bhsv->bhtv', a, vt) + jnp.einsum('bhtk,bhkv->bhtv', qt * jnp.exp(b), state)
        b_last = b[:, :, -1:, :]
        new_state = jnp.exp(b_last[:, :, 0, :])[..., None] * state + \
            jnp.einsum('bhsk,bhsv->bhkv', kt * jnp.exp(b_last - b), vt)
        return new_state, o

    state0 = jnp.zeros((B, HG_HEADS, HG_K, HG_V), jnp.float32)
    _, o = lax.scan(step, state0, (qc, kc, gc, vc))
    o = o.transpose(1, 0, 3, 2, 4).reshape(B, S, HG_HEADS, HG_V)
    gate = g.reshape(B, S, HG_HEADS, HG_V).astype(jnp.float32)
    o = rmsnorm(o, norm_g) * jax.nn.silu(gate)
    return o.reshape(B, S, HG_IDIM).astype(h.dtype) @ w_out


def conv_ffn(h, w_up, conv_w, conv_b, w_down):
    S = h.shape[1]
    u = h @ w_up
    up = jnp.pad(u, ((0, 0), (CONV_W - 1, 0), (0, 0)))
    c = conv_b + conv_w[0] * up[:, 0:S]
    for j in range(1, CONV_W):
        c = c + conv_w[j] * up[:, j:j + S]
    gate, val = jnp.split(c, 2, axis=-1)
    return (jax.nn.silu(gate) * val) @ w_down


def _fwd_setup_inputs(seed: int = 0) -> dict:
    key = jax.random.key(seed)
    ks = jax.random.split(key, 16)
    f32 = jnp.float32
    nrm = lambda k, shape, s: jax.random.normal(k, shape, f32) * s
    return {
        'x': nrm(ks[0], (BATCH, SEQ, D_MODEL), 1.0),
        'norm_mix': 1.0 + nrm(ks[1], (DEPTH, D_MODEL), 0.02),
        'norm_ffn': 1.0 + nrm(ks[2], (DEPTH, D_MODEL), 0.02),
        'norm_final': 1.0 + nrm(ks[3], (D_MODEL,), 0.02),
        'attn_w_in': nrm(ks[4], (N_ATTN_LAYERS, D_MODEL, ATTN_IN), D_MODEL ** -0.5),
        'attn_w_out': nrm(ks[5], (N_ATTN_LAYERS, N_Q_HEADS * HEAD_DIM, D_MODEL), (N_Q_HEADS * HEAD_DIM) ** -0.5),
        'attn_sinks': nrm(ks[6], (N_ATTN_LAYERS, N_Q_HEADS), 1.0),
        'hgrn_w_in': nrm(ks[7], (N_HGRN_LAYERS, D_MODEL, HG_IN), D_MODEL ** -0.5),
        'hgrn_w_out': nrm(ks[8], (N_HGRN_LAYERS, HG_IDIM, D_MODEL), HG_IDIM ** -0.5),
        'hgrn_norm': 1.0 + nrm(ks[9], (N_HGRN_LAYERS, HG_V), 0.02),
        'hgrn_lb_logits': 1.0 + nrm(ks[10], (DEPTH, HG_FDIM), 0.1),
        'ffn_w_up': nrm(ks[11], (DEPTH, D_MODEL, 2 * D_FF), D_MODEL ** -0.5),
        'ffn_conv_w': nrm(ks[12], (DEPTH, CONV_W, 2 * D_FF), CONV_W ** -0.5),
        'ffn_conv_b': nrm(ks[13], (DEPTH, 2 * D_FF), 0.01),
        'ffn_w_down': nrm(ks[14], (DEPTH, D_FF, D_MODEL), D_FF ** -0.5),
    }


def _fwd_reference(x, norm_mix, norm_ffn, norm_final, attn_w_in, attn_w_out, attn_sinks,
              hgrn_w_in, hgrn_w_out, hgrn_norm, hgrn_lb_logits,
              ffn_w_up, ffn_conv_w, ffn_conv_b, ffn_w_down):
    s = jax.nn.softmax(hgrn_lb_logits.astype(jnp.float32), axis=0)
    lower_bounds = jnp.cumsum(s, axis=0) - s[0]
    h = x
    for layer in range(DEPTH):
        idx = layer // N_MIXERS
        hn = rmsnorm(h, norm_mix[layer])
        if layer % N_MIXERS == 0:
            h = h + sliding_window_attention(hn, attn_w_in[idx], attn_w_out[idx], attn_sinks[idx])
        else:
            h = h + hgrn2(hn, hgrn_w_in[idx], hgrn_w_out[idx], hgrn_norm[idx], lower_bounds[layer])
        h = h + conv_ffn(rmsnorm(h, norm_ffn[layer]), ffn_w_up[layer], ffn_conv_w[layer],
                         ffn_conv_b[layer], ffn_w_down[layer])
    return rmsnorm(h, norm_final)


import jax as _jax
import jax.numpy as _jnp

TWIN_FORMAT = 'train_step'
FWD_PARAMS = ['x', 'norm_mix', 'norm_ffn', 'norm_final', 'attn_w_in', 'attn_w_out', 'attn_sinks', 'hgrn_w_in', 'hgrn_w_out', 'hgrn_norm', 'hgrn_lb_logits', 'ffn_w_up', 'ffn_conv_w', 'ffn_conv_b', 'ffn_w_down']
TWIN_WEIGHTS = ['norm_mix', 'norm_ffn', 'norm_final', 'attn_w_in', 'attn_w_out', 'attn_sinks', 'hgrn_w_in', 'hgrn_w_out', 'hgrn_norm', 'hgrn_lb_logits', 'ffn_w_up', 'ffn_conv_w', 'ffn_conv_b', 'ffn_w_down']
TWIN_DIFF_INPUT = 'x'
TWIN_INPUTS = ['x', 'norm_mix', 'norm_ffn', 'norm_final', 'attn_w_in', 'attn_w_out', 'attn_sinks', 'hgrn_w_in', 'hgrn_w_out', 'hgrn_norm', 'hgrn_lb_logits', 'ffn_w_up', 'ffn_conv_w', 'ffn_conv_b', 'ffn_w_down', 'loss_target', 'm_norm_mix', 'm_norm_ffn', 'm_norm_final', 'm_attn_w_in', 'm_attn_w_out', 'm_attn_sinks', 'm_hgrn_w_in', 'm_hgrn_w_out', 'm_hgrn_norm', 'm_hgrn_lb_logits', 'm_ffn_w_up', 'm_ffn_conv_w', 'm_ffn_conv_b', 'm_ffn_w_down', 'v_norm_mix', 'v_norm_ffn', 'v_norm_final', 'v_attn_w_in', 'v_attn_w_out', 'v_attn_sinks', 'v_hgrn_w_in', 'v_hgrn_w_out', 'v_hgrn_norm', 'v_hgrn_lb_logits', 'v_ffn_w_up', 'v_ffn_conv_w', 'v_ffn_conv_b', 'v_ffn_w_down']
TWIN_OUTPUTS = ['loss', 'grad_x', 'grad_norm_mix', 'grad_norm_ffn', 'grad_norm_final', 'grad_attn_w_in', 'grad_attn_w_out', 'grad_attn_sinks', 'grad_hgrn_w_in', 'grad_hgrn_w_out', 'grad_hgrn_norm', 'grad_hgrn_lb_logits', 'grad_ffn_w_up', 'grad_ffn_conv_w', 'grad_ffn_conv_b', 'grad_ffn_w_down', 'delta_norm_mix', 'delta_norm_ffn', 'delta_norm_final', 'delta_attn_w_in', 'delta_attn_w_out', 'delta_attn_sinks', 'delta_hgrn_w_in', 'delta_hgrn_w_out', 'delta_hgrn_norm', 'delta_hgrn_lb_logits', 'delta_ffn_w_up', 'delta_ffn_conv_w', 'delta_ffn_conv_b', 'delta_ffn_w_down', 'new_m_norm_mix', 'new_m_norm_ffn', 'new_m_norm_final', 'new_m_attn_w_in', 'new_m_attn_w_out', 'new_m_attn_sinks', 'new_m_hgrn_w_in', 'new_m_hgrn_w_out', 'new_m_hgrn_norm', 'new_m_hgrn_lb_logits', 'new_m_ffn_w_up', 'new_m_ffn_conv_w', 'new_m_ffn_conv_b', 'new_m_ffn_w_down', 'new_v_norm_mix', 'new_v_norm_ffn', 'new_v_norm_final', 'new_v_attn_w_in', 'new_v_attn_w_out', 'new_v_attn_sinks', 'new_v_hgrn_w_in', 'new_v_hgrn_w_out', 'new_v_hgrn_norm', 'new_v_hgrn_lb_logits', 'new_v_ffn_w_up', 'new_v_ffn_conv_w', 'new_v_ffn_conv_b', 'new_v_ffn_w_down']
TWIN_LEAF_KINDS = {'loss': 'loss', 'grad_x': 'grad_x', 'grad_norm_mix': 'grad_w', 'grad_norm_ffn': 'grad_w', 'grad_norm_final': 'grad_w', 'grad_attn_w_in': 'grad_w', 'grad_attn_w_out': 'grad_w', 'grad_attn_sinks': 'grad_w', 'grad_hgrn_w_in': 'grad_w', 'grad_hgrn_w_out': 'grad_w', 'grad_hgrn_norm': 'grad_w', 'grad_hgrn_lb_logits': 'grad_w', 'grad_ffn_w_up': 'grad_w', 'grad_ffn_conv_w': 'grad_w', 'grad_ffn_conv_b': 'grad_w', 'grad_ffn_w_down': 'grad_w', 'delta_norm_mix': 'delta_w', 'delta_norm_ffn': 'delta_w', 'delta_norm_final': 'delta_w', 'delta_attn_w_in': 'delta_w', 'delta_attn_w_out': 'delta_w', 'delta_attn_sinks': 'delta_w', 'delta_hgrn_w_in': 'delta_w', 'delta_hgrn_w_out': 'delta_w', 'delta_hgrn_norm': 'delta_w', 'delta_hgrn_lb_logits': 'delta_w', 'delta_ffn_w_up': 'delta_w', 'delta_ffn_conv_w': 'delta_w', 'delta_ffn_conv_b': 'delta_w', 'delta_ffn_w_down': 'delta_w', 'new_m_norm_mix': 'new_m', 'new_m_norm_ffn': 'new_m', 'new_m_norm_final': 'new_m', 'new_m_attn_w_in': 'new_m', 'new_m_attn_w_out': 'new_m', 'new_m_attn_sinks': 'new_m', 'new_m_hgrn_w_in': 'new_m', 'new_m_hgrn_w_out': 'new_m', 'new_m_hgrn_norm': 'new_m', 'new_m_hgrn_lb_logits': 'new_m', 'new_m_ffn_w_up': 'new_m', 'new_m_ffn_conv_w': 'new_m', 'new_m_ffn_conv_b': 'new_m', 'new_m_ffn_w_down': 'new_m', 'new_v_norm_mix': 'new_v', 'new_v_norm_ffn': 'new_v', 'new_v_norm_final': 'new_v', 'new_v_attn_w_in': 'new_v', 'new_v_attn_w_out': 'new_v', 'new_v_attn_sinks': 'new_v', 'new_v_hgrn_w_in': 'new_v', 'new_v_hgrn_w_out': 'new_v', 'new_v_hgrn_norm': 'new_v', 'new_v_hgrn_lb_logits': 'new_v', 'new_v_ffn_w_up': 'new_v', 'new_v_ffn_conv_w': 'new_v', 'new_v_ffn_conv_b': 'new_v', 'new_v_ffn_w_down': 'new_v'}


def _forward(args):
    return _fwd_reference(*[args[k] for k in FWD_PARAMS])


def _output_shape():
    out = _jax.eval_shape(lambda: _forward(_fwd_setup_inputs(0)))
    return out.shape, out.dtype

N_MICROBATCH = 1
ADAM_LR = 0.001
ADAM_B1 = 0.9
ADAM_B2 = 0.999
ADAM_EPS = 1e-08
ADAM_WD = 0.01
ADAM_STEP = 10
PER_EXAMPLE_BATCH_AXIS = {'x': 0, 'loss_target': 0}
SHARED_INPUTS = []
_WEIGHT_DTYPES = {'norm_mix': _jnp.float32, 'norm_ffn': _jnp.float32, 'norm_final': _jnp.float32, 'attn_w_in': _jnp.float32, 'attn_w_out': _jnp.float32, 'attn_sinks': _jnp.float32, 'hgrn_w_in': _jnp.float32, 'hgrn_w_out': _jnp.float32, 'hgrn_norm': _jnp.float32, 'hgrn_lb_logits': _jnp.float32, 'ffn_w_up': _jnp.float32, 'ffn_conv_w': _jnp.float32, 'ffn_conv_b': _jnp.float32, 'ffn_w_down': _jnp.float32}
MOMENT_SCALE = {'norm_mix': 1.153035e-01, 'norm_ffn': 1.335980e-01, 'norm_final': 3.200964e+01, 'attn_w_in': 8.392627e-02, 'attn_w_out': 7.322207e-02, 'attn_sinks': 8.893869e-02, 'hgrn_w_in': 6.296379e-02, 'hgrn_w_out': 8.745687e-02, 'hgrn_norm': 2.891961e-01, 'hgrn_lb_logits': 5.397449e-03, 'ffn_w_up': 5.711338e-02, 'ffn_conv_w': 5.743518e-02, 'ffn_conv_b': 5.862525e-02, 'ffn_w_down': 9.347344e-02}


def _to_microbatches(a, axis):
    t = _jnp.moveaxis(a, axis, 0)
    t = t.reshape((N_MICROBATCH, t.shape[0] // N_MICROBATCH) + t.shape[1:])
    return _jnp.moveaxis(t, 1, axis + 1)


def setup_inputs(seed: int = 0) -> dict:
    inp = _fwd_setup_inputs(seed)
    key = _jax.random.fold_in(_jax.random.key(seed), 7919)
    shape, _ = _output_shape()
    out = dict(inp)
    out["loss_target"] = _jax.random.normal(_jax.random.fold_in(key, 0), shape, _jnp.float32)
    for i, name in enumerate(TWIN_WEIGHTS):
        w = inp[name].astype(_jnp.float32)
        if MOMENT_SCALE is None:
            s = _jnp.sqrt(_jnp.mean(_jnp.square(w)) + 1e-30)
        else:
            s = MOMENT_SCALE[name]
        km, kv = _jax.random.split(_jax.random.fold_in(key, i + 1))
        out[name] = w
        out["m_" + name] = s * _jax.random.normal(km, w.shape, _jnp.float32)
        out["v_" + name] = (s * s) * _jax.random.uniform(kv, w.shape, _jnp.float32, 0.5, 1.5)
    if N_MICROBATCH > 1:
        for name, axis in PER_EXAMPLE_BATCH_AXIS.items():
            out[name] = _to_microbatches(out[name], axis)
    return {'x': out['x'], 'norm_mix': out['norm_mix'], 'norm_ffn': out['norm_ffn'], 'norm_final': out['norm_final'], 'attn_w_in': out['attn_w_in'], 'attn_w_out': out['attn_w_out'], 'attn_sinks': out['attn_sinks'], 'hgrn_w_in': out['hgrn_w_in'], 'hgrn_w_out': out['hgrn_w_out'], 'hgrn_norm': out['hgrn_norm'], 'hgrn_lb_logits': out['hgrn_lb_logits'], 'ffn_w_up': out['ffn_w_up'], 'ffn_conv_w': out['ffn_conv_w'], 'ffn_conv_b': out['ffn_conv_b'], 'ffn_w_down': out['ffn_w_down'], 'loss_target': out['loss_target'], 'm_norm_mix': out['m_norm_mix'], 'm_norm_ffn': out['m_norm_ffn'], 'm_norm_final': out['m_norm_final'], 'm_attn_w_in': out['m_attn_w_in'], 'm_attn_w_out': out['m_attn_w_out'], 'm_attn_sinks': out['m_attn_sinks'], 'm_hgrn_w_in': out['m_hgrn_w_in'], 'm_hgrn_w_out': out['m_hgrn_w_out'], 'm_hgrn_norm': out['m_hgrn_norm'], 'm_hgrn_lb_logits': out['m_hgrn_lb_logits'], 'm_ffn_w_up': out['m_ffn_w_up'], 'm_ffn_conv_w': out['m_ffn_conv_w'], 'm_ffn_conv_b': out['m_ffn_conv_b'], 'm_ffn_w_down': out['m_ffn_w_down'], 'v_norm_mix': out['v_norm_mix'], 'v_norm_ffn': out['v_norm_ffn'], 'v_norm_final': out['v_norm_final'], 'v_attn_w_in': out['v_attn_w_in'], 'v_attn_w_out': out['v_attn_w_out'], 'v_attn_sinks': out['v_attn_sinks'], 'v_hgrn_w_in': out['v_hgrn_w_in'], 'v_hgrn_w_out': out['v_hgrn_w_out'], 'v_hgrn_norm': out['v_hgrn_norm'], 'v_hgrn_lb_logits': out['v_hgrn_lb_logits'], 'v_ffn_w_up': out['v_ffn_w_up'], 'v_ffn_conv_w': out['v_ffn_conv_w'], 'v_ffn_conv_b': out['v_ffn_conv_b'], 'v_ffn_w_down': out['v_ffn_w_down']}


def _loss(weights, diff, rest, loss_target):
    with _jax.named_scope("forward"):
        args = {**rest, TWIN_DIFF_INPUT: diff, **{k: w.astype(_WEIGHT_DTYPES[k]) for k, w in weights.items()}}
        y = _forward(args)
    with _jax.named_scope("loss_head"):
        err = _jnp.square(y.astype(_jnp.float32) - loss_target)
        return 0.5 * _jnp.sum(_jnp.mean(err, axis=-1)) if err.ndim else 0.5 * err


def _adamw(w, g, m, v):
    m = ADAM_B1 * m + (1.0 - ADAM_B1) * g
    v = ADAM_B2 * v + (1.0 - ADAM_B2) * _jnp.square(g)
    m_hat = m / (1.0 - ADAM_B1 ** ADAM_STEP)
    v_hat = v / (1.0 - ADAM_B2 ** ADAM_STEP)
    delta = -ADAM_LR * (m_hat / (_jnp.sqrt(v_hat) + ADAM_EPS) + ADAM_WD * w)
    return delta, m, v


def reference(x, norm_mix, norm_ffn, norm_final, attn_w_in, attn_w_out, attn_sinks, hgrn_w_in, hgrn_w_out, hgrn_norm, hgrn_lb_logits, ffn_w_up, ffn_conv_w, ffn_conv_b, ffn_w_down, loss_target, m_norm_mix, m_norm_ffn, m_norm_final, m_attn_w_in, m_attn_w_out, m_attn_sinks, m_hgrn_w_in, m_hgrn_w_out, m_hgrn_norm, m_hgrn_lb_logits, m_ffn_w_up, m_ffn_conv_w, m_ffn_conv_b, m_ffn_w_down, v_norm_mix, v_norm_ffn, v_norm_final, v_attn_w_in, v_attn_w_out, v_attn_sinks, v_hgrn_w_in, v_hgrn_w_out, v_hgrn_norm, v_hgrn_lb_logits, v_ffn_w_up, v_ffn_conv_w, v_ffn_conv_b, v_ffn_w_down):
    given = dict(x=x, norm_mix=norm_mix, norm_ffn=norm_ffn, norm_final=norm_final, attn_w_in=attn_w_in, attn_w_out=attn_w_out, attn_sinks=attn_sinks, hgrn_w_in=hgrn_w_in, hgrn_w_out=hgrn_w_out, hgrn_norm=hgrn_norm, hgrn_lb_logits=hgrn_lb_logits, ffn_w_up=ffn_w_up, ffn_conv_w=ffn_conv_w, ffn_conv_b=ffn_conv_b, ffn_w_down=ffn_w_down, loss_target=loss_target, m_norm_mix=m_norm_mix, m_norm_ffn=m_norm_ffn, m_norm_final=m_norm_final, m_attn_w_in=m_attn_w_in, m_attn_w_out=m_attn_w_out, m_attn_sinks=m_attn_sinks, m_hgrn_w_in=m_hgrn_w_in, m_hgrn_w_out=m_hgrn_w_out, m_hgrn_norm=m_hgrn_norm, m_hgrn_lb_logits=m_hgrn_lb_logits, m_ffn_w_up=m_ffn_w_up, m_ffn_conv_w=m_ffn_conv_w, m_ffn_conv_b=m_ffn_conv_b, m_ffn_w_down=m_ffn_w_down, v_norm_mix=v_norm_mix, v_norm_ffn=v_norm_ffn, v_norm_final=v_norm_final, v_attn_w_in=v_attn_w_in, v_attn_w_out=v_attn_w_out, v_attn_sinks=v_attn_sinks, v_hgrn_w_in=v_hgrn_w_in, v_hgrn_w_out=v_hgrn_w_out, v_hgrn_norm=v_hgrn_norm, v_hgrn_lb_logits=v_hgrn_lb_logits, v_ffn_w_up=v_ffn_w_up, v_ffn_conv_w=v_ffn_conv_w, v_ffn_conv_b=v_ffn_conv_b, v_ffn_w_down=v_ffn_w_down)
    weights = {n: given[n] for n in TWIN_WEIGHTS}
    shared = {n: given[n] for n in SHARED_INPUTS}
    per_example = {n: given[n] for n in ['x']}
    grad_fn = _jax.value_and_grad(_loss, argnums=(0, 1))

    def one_microbatch(ex, loss_target):
        ex = dict(ex)
        diff = ex.pop(TWIN_DIFF_INPUT)
        return grad_fn(weights, diff, {**shared, **ex}, loss_target)

    if N_MICROBATCH == 1:
        loss, (grad_w, grad_x) = one_microbatch(per_example, given["loss_target"])
    else:
        def body(carry, xs):
            loss_sum, grad_sum = carry
            l_k, (gw_k, gx_k) = one_microbatch(xs[0], xs[1])
            with _jax.named_scope("update"):
                return (loss_sum + l_k, _jax.tree.map(_jnp.add, grad_sum, gw_k)), gx_k

        init = (_jnp.zeros((), _jnp.float32), _jax.tree.map(_jnp.zeros_like, weights))
        (loss, grad_w), grad_x = _jax.lax.scan(body, init, (per_example, given["loss_target"]))
    with _jax.named_scope("update"):
        delta_w, new_m, new_v = {}, {}, {}
        for n in TWIN_WEIGHTS:
            delta_w[n], new_m[n], new_v[n] = _adamw(weights[n], grad_w[n], given["m_" + n], given["v_" + n])
    return (loss, grad_x, *[grad_w[n] for n in TWIN_WEIGHTS], *[delta_w[n] for n in TWIN_WEIGHTS],
            *[new_m[n] for n in TWIN_WEIGHTS], *[new_v[n] for n in TWIN_WEIGHTS])
```

```python
import numpy as np
import jax
import jax.numpy as jnp
from jax import lax
from jax.experimental import pallas as pl
from jax.experimental.pallas import tpu as pltpu

F32 = jnp.float32
BF16 = jnp.bfloat16

D_MODEL = 1024
DEPTH = 4
HEAD_DIM = 64
N_Q_HEADS = 16
N_KV_HEADS = 4
Q_PER_KV = 4
ATTN_BLOCK = 128
ATTN_IN = 1536
HG_HEADS = 8
HG_K = 128
HG_CHUNK = 64
HG_IN = 4096
D_FF = 2816
EPS = 1e-6
N_DEV = 8
FF_SLOT = 2 * D_FF // N_DEV
HG_SLOT = HG_IN // N_DEV
HG_LEVELS = 6

ADAM_LR = 0.001
ADAM_B1 = 0.9
ADAM_B2 = 0.999
ADAM_EPS = 1e-08
ADAM_WD = 0.01
ADAM_STEP = 10

VMEM_LIMIT = 56 * 1024 * 1024
NEG_BIG = -1e30

NN = (((1,), (0,)), ((), ()))
NT = (((1,), (1,)), ((), ()))
TN = (((0,), (0,)), ((), ()))


def _bdot(a, b, dn):
    return lax.dot_general(a.astype(BF16), b.astype(BF16), dn, preferred_element_type=F32)


def _sds(shape, dtype):
    return jax.ShapeDtypeStruct(tuple(shape), dtype)


def _params(sem):
    return pltpu.CompilerParams(dimension_semantics=sem, vmem_limit_bytes=VMEM_LIMIT)


def _matmul(name, a, b, *, dn, grid, a_spec, b_spec, o_spec, out_shape, acc_shape=None, res=None, res_spec=None):
    nk = grid[2]

    def body(*refs):
        a_ref, b_ref = refs[0], refs[1]
        r_ref = refs[2] if res is not None else None
        o_ref = refs[3] if res is not None else refs[2]

        def prod():
            return _bdot(a_ref[...], b_ref[...], dn)

        def finish(v):
            if r_ref is not None:
                v = v + r_ref[...]
            o_ref[...] = v.astype(o_ref.dtype)

        if nk == 1:
            finish(prod())
        else:
            acc = refs[-1]
            k = pl.program_id(2)

            @pl.when(k == 0)
            def _():
                acc[...] = prod()

            @pl.when(k > 0)
            def _():
                acc[...] += prod()

            @pl.when(k == nk - 1)
            def _():
                finish(acc[...])

    in_specs = [a_spec, b_spec] + ([res_spec] if res is not None else [])
    args = (a, b) + ((res,) if res is not None else ())
    scratch = [] if nk == 1 else [pltpu.VMEM(acc_shape, F32)]
    return pl.pallas_call(
        body, name=name, grid=grid, in_specs=in_specs, out_specs=o_spec, out_shape=out_shape,
        scratch_shapes=scratch, compiler_params=_params(("parallel", "parallel", "arbitrary")),
    )(*args)


def _tile(n, t):
    return min(n, t)


def _proj_rows(name, hn, wt, l, out_dtype):
    S, N = hn.shape[0], wt.shape[1]
    tm, tn = _tile(S, 512), 512
    return _matmul(
        name, hn, wt, dn=NT, grid=(S // tm, N // tn, 1),
        a_spec=pl.BlockSpec((tm, D_MODEL), lambda i, j, k: (i, 0)),
        b_spec=pl.BlockSpec((None, tn, D_MODEL), lambda i, j, k: (l, j, 0)),
        o_spec=pl.BlockSpec((tm, tn), lambda i, j, k: (i, j)),
        out_shape=_sds((S, N), out_dtype))


def _proj_slots(name, hn, w, l):
    S, r = hn.shape[0], w.shape[3]
    tm = _tile(S, 512)
    return _matmul(
        name, hn, w, dn=NN, grid=(N_DEV, S // tm, 1),
        a_spec=pl.BlockSpec((tm, D_MODEL), lambda j, i, k: (i, 0)),
        b_spec=pl.BlockSpec((None, None, D_MODEL, r), lambda j, i, k: (l, j, 0, 0)),
        o_spec=pl.BlockSpec((None, tm, r), lambda j, i, k: (j, i, 0)),
        out_shape=_sds((N_DEV, S, r), F32))


def _out_proj(name, o, w, l, h):
    S, K = o.shape
    tm = _tile(S, 512)
    return _matmul(
        name, o, w, dn=NN, grid=(S // tm, 1, 1),
        a_spec=pl.BlockSpec((tm, K), lambda i, j, k: (i, 0)),
        b_spec=pl.BlockSpec((None, K, D_MODEL), lambda i, j, k: (l, 0, 0)),
        o_spec=pl.BlockSpec((tm, D_MODEL), lambda i, j, k: (i, 0)),
        out_shape=_sds((S, D_MODEL), F32), res=h,
        res_spec=pl.BlockSpec((tm, D_MODEL), lambda i, j, k: (i, 0)))


def _down_proj(name, a, w, l, h):
    nj, S, r = a.shape
    tm = _tile(S, 512)
    return _matmul(
        name, a, w, dn=NN, grid=(S // tm, 1, nj),
        a_spec=pl.BlockSpec((None, tm, r), lambda i, j, k: (k, i, 0)),
        b_spec=pl.BlockSpec((None, None, r, D_MODEL), lambda i, j, k: (l, k, 0, 0)),
        o_spec=pl.BlockSpec((tm, D_MODEL), lambda i, j, k: (i, 0)),
        out_shape=_sds((S, D_MODEL), F32), acc_shape=(tm, D_MODEL), res=h,
        res_spec=pl.BlockSpec((tm, D_MODEL), lambda i, j, k: (i, 0)))


def _dgrad_down(name, dh, w, l):
    S = dh.shape[0]
    nj, r = w.shape[1], w.shape[2]
    tm = _tile(S, 512)
    return _matmul(
        name, dh, w, dn=NT, grid=(nj, S // tm, 1),
        a_spec=pl.BlockSpec((tm, D_MODEL), lambda j, i, k: (i, 0)),
        b_spec=pl.BlockSpec((None, None, r, D_MODEL), lambda j, i, k: (l, j, 0, 0)),
        o_spec=pl.BlockSpec((None, tm, r), lambda j, i, k: (j, i, 0)),
        out_shape=_sds((nj, S, r), BF16))


def _wgrad_down(name, a, dh):
    nj, S, r = a.shape
    tk = _tile(S, 512)
    return _matmul(
        name, a, dh, dn=TN, grid=(nj, 1, S // tk),
        a_spec=pl.BlockSpec((None, tk, r), lambda s, j, k: (s, k, 0)),
        b_spec=pl.BlockSpec((tk, D_MODEL), lambda s, j, k: (k, 0)),
        o_spec=pl.BlockSpec((None, r, D_MODEL), lambda s, j, k: (s, 0, 0)),
        out_shape=_sds((nj, r, D_MODEL), BF16), acc_shape=(r, D_MODEL))


def _dgrad_slots(name, dz, w, l):
    nj, S, r = dz.shape
    tm = _tile(S, 512)
    return _matmul(
        name, dz, w, dn=NT, grid=(S // tm, 1, nj),
        a_spec=pl.BlockSpec((None, tm, r), lambda i, j, k: (k, i, 0)),
        b_spec=pl.BlockSpec((None, None, D_MODEL, r), lambda i, j, k: (l, k, 0, 0)),
        o_spec=pl.BlockSpec((tm, D_MODEL), lambda i, j, k: (i, 0)),
        out_shape=_sds((S, D_MODEL), F32), acc_shape=(tm, D_MODEL))


def _wgrad_slots(name, hn, dz):
    nj, S, r = dz.shape
    tk = _tile(S, 512)
    return _matmul(
        name, hn, dz, dn=TN, grid=(nj, 1, S // tk),
        a_spec=pl.BlockSpec((tk, D_MODEL), lambda s, j, k: (k, 0)),
        b_spec=pl.BlockSpec((None, tk, r), lambda s, j, k: (s, k, 0)),
        o_spec=pl.BlockSpec((None, D_MODEL, r), lambda s, j, k: (s, 0, 0)),
        out_shape=_sds((nj, D_MODEL, r), BF16), acc_shape=(D_MODEL, r))


def _dgrad_out(name, dh, w, l, out_dtype):
    S, K = dh.shape[0], w.shape[1]
    tm = _tile(S, 512)
    return _matmul(
        name, dh, w, dn=NT, grid=(S // tm, 1, 1),
        a_spec=pl.BlockSpec((tm, D_MODEL), lambda i, j, k: (i, 0)),
        b_spec=pl.BlockSpec((None, K, D_MODEL), lambda i, j, k: (l, 0, 0)),
        o_spec=pl.BlockSpec((tm, K), lambda i, j, k: (i, 0)),
        out_shape=_sds((S, K), out_dtype))


def _wgrad_rows(name, a, b):
    S, K = a.shape
    tk = _tile(S, 512)
    return _matmul(
        name, a, b, dn=TN, grid=(1, 1, S // tk),
        a_spec=pl.BlockSpec((tk, K), lambda i, j, k: (k, 0)),
        b_spec=pl.BlockSpec((tk, D_MODEL), lambda i, j, k: (k, 0)),
        o_spec=pl.BlockSpec((K, D_MODEL), lambda i, j, k: (0, 0)),
        out_shape=_sds((K, D_MODEL), BF16), acc_shape=(K, D_MODEL))


def _dgrad_rows(name, dz, wt, l):
    S, N = dz.shape
    tm = _tile(S, 512)
    return _matmul(
        name, dz, wt, dn=NN, grid=(S // tm, 1, 1),
        a_spec=pl.BlockSpec((tm, N), lambda i, j, k: (i, 0)),
        b_spec=pl.BlockSpec((None, N, D_MODEL), lambda i, j, k: (l, 0, 0)),
        o_spec=pl.BlockSpec((tm, D_MODEL), lambda i, j, k: (i, 0)),
        out_shape=_sds((S, D_MODEL), F32))


def _rmsnorm_fwd(name, h, g):
    S = h.shape[0]
    tm = _tile(S, 512)

    def body(h_ref, g_ref, o_ref):
        x = h_ref[...]
        r = lax.rsqrt(jnp.mean(x * x, axis=-1, keepdims=True) + EPS)
        o_ref[...] = (x * r * g_ref[...]).astype(o_ref.dtype)

    row = pl.BlockSpec((tm, D_MODEL), lambda i: (i, 0))
    return pl.pallas_call(
        body, name=name, grid=(S // tm,), in_specs=[row, pl.BlockSpec((1, D_MODEL), lambda i: (0, 0))],
        out_specs=row, out_shape=_sds((S, D_MODEL), BF16), compiler_params=_params(("parallel",)),
    )(h, g)


def _rmsnorm_bwd(name, h, g, dy, dres):
    S = h.shape[0]
    tm = _tile(S, 512)

    def body(h_ref, g_ref, dy_ref, dres_ref, dh_ref, dg_ref):
        x = h_ref[...]
        r = lax.rsqrt(jnp.mean(x * x, axis=-1, keepdims=True) + EPS)
        xh = x * r
        dyf = dy_ref[...].astype(F32)
        dyg = dyf * g_ref[...]
        dh_ref[...] = dres_ref[...] + r * (dyg - xh * jnp.mean(dyg * xh, axis=-1, keepdims=True))
        part = jnp.sum(dyf * xh, axis=0, keepdims=True)

        @pl.when(pl.program_id(0) == 0)
        def _():
            dg_ref[...] = part

        @pl.when(pl.program_id(0) > 0)
        def _():
            dg_ref[...] += part

    row = pl.BlockSpec((tm, D_MODEL), lambda i: (i, 0))
    vec = pl.BlockSpec((1, D_MODEL), lambda i: (0, 0))
    return pl.pallas_call(
        body, name=name, grid=(S // tm,), in_specs=[row, vec, row, row], out_specs=[row, vec],
        out_shape=[_sds((S, D_MODEL), F32), _sds((1, D_MODEL), F32)], compiler_params=_params(("arbitrary",)),
    )(h, g, dy, dres)


def _loss_head(name, h, g, target):
    S = h.shape[0]
    tm = _tile(S, 512)

    def body(h_ref, g_ref, t_ref, dh_ref, dg_ref, ls_ref):
        x = h_ref[...]
        r = lax.rsqrt(jnp.mean(x * x, axis=-1, keepdims=True) + EPS)
        xh = x * r
        diff = xh * g_ref[...] - t_ref[...]
        dyf = diff * (1.0 / D_MODEL)
        dyg = dyf * g_ref[...]
        dh_ref[...] = r * (dyg - xh * jnp.mean(dyg * xh, axis=-1, keepdims=True))
        part = jnp.sum(dyf * xh, axis=0, keepdims=True)
        lpart = jnp.sum(diff * diff, axis=0, keepdims=True) * (0.5 / D_MODEL)

        @pl.when(pl.program_id(0) == 0)
        def _():
            dg_ref[...] = part
            ls_ref[...] = lpart

        @pl.when(pl.program_id(0) > 0)
        def _():
            dg_ref[...] += part
            ls_ref[...] += lpart

    row = pl.BlockSpec((tm, D_MODEL), lambda i: (i, 0))
    vec = pl.BlockSpec((1, D_MODEL), lambda i: (0, 0))
    return pl.pallas_call(
        body, name=name, grid=(S // tm,), in_specs=[row, vec, row], out_specs=[row, vec, vec],
        out_shape=[_sds((S, D_MODEL), F32), _sds((1, D_MODEL), F32), _sds((1, D_MODEL), F32)],
        compiler_params=_params(("arbitrary",)),
    )(h, g, target)


ATTN_SCALE = HEAD_DIM ** -0.5
ALIBI_SLOPES = [2.0 ** (-8.0 * (h + 1) / N_Q_HEADS) for h in range(N_Q_HEADS)]
K_COL = N_Q_HEADS * HEAD_DIM
KV_COLS = N_KV_HEADS * HEAD_DIM
V_COL = K_COL + KV_COLS


def _attn_masks(n):
    qi = lax.broadcasted_iota(jnp.int32, (ATTN_BLOCK, ATTN_BLOCK), 0)
    ki = lax.broadcasted_iota(jnp.int32, (ATTN_BLOCK, ATTN_BLOCK), 1)
    dist_c = (qi - ki).astype(F32)
    return dist_c + float(ATTN_BLOCK), dist_c, (ki > qi) & (n > 0), qi >= ki


def _attn_probs(q, kp, kc, sink, slope, masks):
    dist_p, dist_c, valid_p, valid_c = masks
    sp = jnp.where(valid_p, _bdot(q, kp, NT) * ATTN_SCALE - slope * dist_p, NEG_BIG)
    sc = jnp.where(valid_c, _bdot(q, kc, NT) * ATTN_SCALE - slope * dist_c, NEG_BIG)
    m = jnp.maximum(jnp.maximum(jnp.max(sp, axis=-1, keepdims=True), jnp.max(sc, axis=-1, keepdims=True)), sink)
    ep, ec, es = jnp.exp(sp - m), jnp.exp(sc - m), jnp.exp(sink - m)
    inv = 1.0 / (jnp.sum(ep, axis=-1, keepdims=True) + jnp.sum(ec, axis=-1, keepdims=True) + es)
    return ep * inv, ec * inv, es * inv


def _attn_specs(nblk):
    last = nblk - 1
    kcol, vcol = K_COL // KV_COLS, V_COL // KV_COLS
    return [
        pl.BlockSpec((ATTN_BLOCK, K_COL), lambda n: (jnp.minimum(n, last), 0)),
        pl.BlockSpec((ATTN_BLOCK, KV_COLS), lambda n: (jnp.minimum(n, last), kcol)),
        pl.BlockSpec((ATTN_BLOCK, KV_COLS), lambda n: (jnp.maximum(jnp.minimum(n, last) - 1, 0), kcol)),
        pl.BlockSpec((ATTN_BLOCK, KV_COLS), lambda n: (jnp.minimum(n, last), vcol)),
        pl.BlockSpec((ATTN_BLOCK, KV_COLS), lambda n: (jnp.maximum(jnp.minimum(n, last) - 1, 0), vcol)),
    ]


def _attn_fwd(name, proj, sinks):
    S = proj.shape[0]
    nblk = S // ATTN_BLOCK

    def body(q_ref, kc_ref, kp_ref, vc_ref, vp_ref, sk_ref, o_ref):
        masks = _attn_masks(pl.program_id(0))
        for m in range(N_KV_HEADS):
            ks = slice(HEAD_DIM * m, HEAD_DIM * (m + 1))
            kp, kc, vp, vc = kp_ref[:, ks], kc_ref[:, ks], vp_ref[:, ks], vc_ref[:, ks]
            for g in range(Q_PER_KV):
                hh = Q_PER_KV * m + g
                qs = slice(HEAD_DIM * hh, HEAD_DIM * (hh + 1))
                pp, pc, _ = _attn_probs(q_ref[:, qs], kp, kc, sk_ref[0, hh], ALIBI_SLOPES[hh], masks)
                o_ref[:, qs] = (_bdot(pp, vp, NN) + _bdot(pc, vc, NN)).astype(o_ref.dtype)

    return pl.pallas_call(
        body, name=name, grid=(nblk,),
        in_specs=_attn_specs(nblk) + [pl.BlockSpec(memory_space=pltpu.SMEM)],
        out_specs=pl.BlockSpec((ATTN_BLOCK, K_COL), lambda n: (n, 0)),
        out_shape=_sds((S, K_COL), BF16), compiler_params=_params(("parallel",)),
    )(proj, proj, proj, proj, proj, sinks)


def _attn_bwd(name, proj, sinks, do):
    S = proj.shape[0]
    nblk = S // ATTN_BLOCK

    def body(q_ref, kc_ref, kp_ref, vc_ref, vp_ref, do_ref, sk_ref, dz_ref, ds_ref, carry, cur, padd):
        n = pl.program_id(0)

        @pl.when(n == 0)
        def _():
            carry[...] = jnp.zeros_like(carry)
            ds_ref[...] = jnp.zeros_like(ds_ref)

        @pl.when(n < nblk)
        def _():
            masks = _attn_masks(n)
            lane = lax.broadcasted_iota(jnp.int32, (1, 128), 1)
            dsv = jnp.zeros((1, 128), F32)
            for m in range(N_KV_HEADS):
                ks = slice(HEAD_DIM * m, HEAD_DIM * (m + 1))
                kp, kc, vp, vc = kp_ref[:, ks], kc_ref[:, ks], vp_ref[:, ks], vc_ref[:, ks]
                dkp = dkc = dvp = dvc = jnp.zeros((ATTN_BLOCK, HEAD_DIM), F32)
                for g in range(Q_PER_KV):
                    hh = Q_PER_KV * m + g
                    qs = slice(HEAD_DIM * hh, HEAD_DIM * (hh + 1))
                    q, dout = q_ref[:, qs], do_ref[:, qs]
                    pp, pc, ps = _attn_probs(q, kp, kc, sk_ref[0, hh], ALIBI_SLOPES[hh], masks)
                    dpp, dpc = _bdot(dout, vp, NT), _bdot(dout, vc, NT)
                    delta = jnp.sum(pp * dpp, axis=-1, keepdims=True) + jnp.sum(pc * dpc, axis=-1, keepdims=True)
                    dsp, dsc = pp * (dpp - delta), pc * (dpc - delta)
                    dsv = dsv + jnp.where(lane == hh, -jnp.sum(ps * delta, axis=0, keepdims=True), 0.0)
                    cur[:, qs] = (_bdot(dsp, kp, NN) + _bdot(dsc, kc, NN)) * ATTN_SCALE
                    dkp = dkp + _bdot(dsp, q, TN) * ATTN_SCALE
                    dkc = dkc + _bdot(dsc, q, TN) * ATTN_SCALE
                    dvp = dvp + _bdot(pp, dout, TN)
                    dvc = dvc + _bdot(pc, dout, TN)
                cur[:, K_COL + HEAD_DIM * m:K_COL + HEAD_DIM * (m + 1)] = dkc
                cur[:, V_COL + HEAD_DIM * m:V_COL + HEAD_DIM * (m + 1)] = dvc
                padd[:, ks] = dkp
                padd[:, KV_COLS + HEAD_DIM * m:KV_COLS + HEAD_DIM * (m + 1)] = dvp
            ds_ref[...] += dsv
            dz_ref[:, :K_COL] = carry[:, :K_COL].astype(dz_ref.dtype)
            dz_ref[:, K_COL:] = (carry[:, K_COL:] + padd[...]).astype(dz_ref.dtype)
            carry[...] = cur[...]

        @pl.when(n == nblk)
        def _():
            dz_ref[...] = carry[...].astype(dz_ref.dtype)

    return pl.pallas_call(
        body, name=name, grid=(nblk + 1,),
        in_specs=_attn_specs(nblk) + [
            pl.BlockSpec((ATTN_BLOCK, K_COL), lambda n: (jnp.minimum(n, nblk - 1), 0)),
            pl.BlockSpec(memory_space=pltpu.SMEM)],
        out_specs=[pl.BlockSpec((ATTN_BLOCK, ATTN_IN), lambda n: (jnp.maximum(n - 1, 0), 0)),
                   pl.BlockSpec((1, 128), lambda n: (0, 0))],
        out_shape=[_sds((S, ATTN_IN), BF16), _sds((1, 128), F32)],
        scratch_shapes=[pltpu.VMEM((ATTN_BLOCK, ATTN_IN), F32), pltpu.VMEM((ATTN_BLOCK, ATTN_IN), F32),
                        pltpu.VMEM((ATTN_BLOCK, 2 * KV_COLS), F32)],
        compiler_params=_params(("arbitrary",)),
    )(proj, proj, proj, proj, proj, do, sinks)


def _hg_consts():
    C = HG_CHUNK
    tri = np.tril(np.ones((C, C)))
    t = np.arange(C)
    rows, masks = [tri], []
    for lvl in range(HG_LEVELS):
        n = C >> (lvl + 1)
        sel = np.zeros((C, C))
        sel[t, (t // (2 * n)) * (2 * n) + n - 1] = 1.0
        rows.append(sel @ tri)
        tt, ss = t[:, None], t[None, :]
        masks.append((tt // (2 * n) == ss // (2 * n)) & ((tt // n) % 2 == 1) & ((ss // n) % 2 == 0))
    masks.append(np.eye(C, dtype=bool))
    stk = np.concatenate(rows, axis=0)
    return jnp.asarray(stk, BF16), jnp.asarray(np.stack(masks), F32)


def _sigmoid(x):
    return 1.0 / (1.0 + jnp.exp(-x))


def _split3(x):
    hi = x.astype(BF16)
    r1 = x - hi.astype(F32)
    mid = r1.astype(BF16)
    return hi, mid, (r1 - mid.astype(F32)).astype(BF16)


def _dot01(m01, x, dn):
    return sum(lax.dot_general(m01, p, dn, preferred_element_type=F32) for p in _split3(x))


def _hg_common(z_ref, lb_ref, stk_ref):
    qr, fr = z_ref[0], z_ref[1]
    lb = lb_ref[...]
    sq, sg, sgn = _sigmoid(qr), _sigmoid(fr), _sigmoid(-fr)
    ft = lb + (1.0 - lb) * sg
    lf = jnp.log(ft)
    bb = _dot01(stk_ref[...], lf, NN)
    b = bb[0:HG_CHUNK]
    diffs = [b - bb[HG_CHUNK * (l + 1):HG_CHUNK * (l + 2)] for l in range(HG_LEVELS)]
    ws = [jnp.exp(-jnp.abs(d)) for d in diffs]
    blast = b[HG_CHUNK - 1:HG_CHUNK]
    return dict(qr=qr, fr=fr, lb=lb, sq=sq, sg=sg, sgn=sgn, ft=ft, q=qr * sq, kk=(1.0 - lb) * sgn, b=b, diffs=diffs,
                ws=ws, eb=jnp.exp(b), ed=jnp.exp(blast - b), elast=jnp.exp(blast))


def _hg_intra(qh, kh, ws, msk_ref, sl):
    a = msk_ref[HG_LEVELS] * _bdot(qh, kh, NT)
    qls, kls = [], []
    for l in range(HG_LEVELS):
        w = ws[l][:, sl]
        qls.append((qh * w).astype(BF16))
        kls.append((kh * w).astype(BF16))
        a = a + msk_ref[l] * _bdot(qls[l], kls[l], NT)
    return a, qls, kls


def _hg_fwd(name, z, lb, ng):
    S = z.shape[2]
    nc = S // HG_CHUNK
    stk, msk = _hg_consts()

    def body(z_ref, lb_ref, ng_ref, stk_ref, msk_ref, og_ref, st_ref, state):
        @pl.when(pl.program_id(1) == 0)
        def _():
            state[...] = jnp.zeros_like(state)

        cm = _hg_common(z_ref, lb_ref, stk_ref)
        v, gt = z_ref[2], z_ref[3]
        kd = cm["kk"] * cm["ed"]
        for hh in range(4):
            sl = slice(HG_K * hh, HG_K * (hh + 1))
            st = state[hh]
            st_ref[hh] = st
            qh, kh, vh = cm["q"][:, sl], cm["kk"][:, sl], v[:, sl]
            a, _, _ = _hg_intra(qh, kh, cm["ws"], msk_ref, sl)
            o = _bdot(a, vh, NN) + _bdot(qh * cm["eb"][:, sl], st, NT)
            state[hh] = cm["elast"][:, sl] * st + _bdot(vh, kd[:, sl], TN)
            r = lax.rsqrt(jnp.mean(o * o, axis=-1, keepdims=True) + EPS)
            gh = gt[:, sl]
            og_ref[:, sl] = (o * r * ng_ref[...] * (gh * _sigmoid(gh))).astype(og_ref.dtype)

    return pl.pallas_call(
        body, name=name, grid=(2, nc),
        in_specs=[pl.BlockSpec((4, None, HG_CHUNK, HG_SLOT), lambda g, c: (0, g, c, 0)),
                  pl.BlockSpec((1, HG_SLOT), lambda g, c: (0, g)),
                  pl.BlockSpec((1, HG_K), lambda g, c: (0, 0)),
                  pl.BlockSpec(stk.shape, lambda g, c: (0, 0)),
                  pl.BlockSpec(msk.shape, lambda g, c: (0, 0, 0))],
        out_specs=[pl.BlockSpec((HG_CHUNK, HG_SLOT), lambda g, c: (c, g)),
                   pl.BlockSpec((None, 4, HG_K, HG_K), lambda g, c: (c, g, 0, 0))],
        out_shape=[_sds((S, D_MODEL), BF16), _sds((nc, HG_HEADS, HG_K, HG_K), F32)],
        scratch_shapes=[pltpu.VMEM((4, HG_K, HG_K), F32)],
        compiler_params=_params(("parallel", "arbitrary")),
    )(z, lb, ng, stk, msk)


def _hg_bwd(name, z, lb, ng, states, dog):
    S = z.shape[2]
    nc = S // HG_CHUNK
    stk, msk = _hg_consts()

    def body(z_ref, lb_ref, ng_ref, stk_ref, msk_ref, st_ref, dog_ref, dz_ref, dlb_ref, dng_ref, dstate):
        @pl.when(pl.program_id(1) == 0)
        def _():
            dstate[...] = jnp.zeros_like(dstate)
            dlb_ref[...] = jnp.zeros_like(dlb_ref)
            dng_ref[...] = jnp.zeros_like(dng_ref)

        cm = _hg_common(z_ref, lb_ref, stk_ref)
        v, gt = z_ref[2], z_ref[3]
        ng = ng_ref[...]
        kd = cm["kk"] * cm["ed"]
        row = lax.broadcasted_iota(jnp.int32, (HG_CHUNK, 1), 0)
        dng = jnp.zeros((1, HG_K), F32)
        dq_h, dkk_h, db_h, dv_h, dgt_h = [], [], [], [], []
        dr_h = [[] for _ in range(HG_LEVELS)]
        for hh in range(4):
            sl = slice(HG_K * hh, HG_K * (hh + 1))
            st, dst = st_ref[hh], dstate[hh]
            qh, kh, vh, ebh, edh, kdh = cm["q"][:, sl], cm["kk"][:, sl], v[:, sl], cm["eb"][:, sl], cm["ed"][:, sl], kd[:, sl]
            elh = cm["elast"][:, sl]
            a, qls, kls = _hg_intra(qh, kh, cm["ws"], msk_ref, sl)
            qe = qh * ebh
            o = _bdot(a, vh, NN) + _bdot(qe, st, NT)
            r = lax.rsqrt(jnp.mean(o * o, axis=-1, keepdims=True) + EPS)
            xh = o * r
            gh = gt[:, sl]
            sgg = _sigmoid(gh)
            dog = dog_ref[:, sl].astype(F32)
            dy = dog * (gh * sgg)
            dgt_h.append(dog * (xh * ng) * (sgg * (1.0 + gh * (1.0 - sgg))))
            dng = dng + jnp.sum(dy * xh, axis=0, keepdims=True)
            dyg = dy * ng
            do = r * (dyg - xh * jnp.mean(dyg * xh, axis=-1, keepdims=True))
            da = _bdot(do, vh, NT)
            dv_h.append(_bdot(a, do, TN) + _bdot(kdh, dst, NT))
            dkd = _bdot(vh, dst, NN)
            delast = jnp.sum(st * dst, axis=0, keepdims=True)
            dqe = _bdot(do, st, NN)
            dstate[hh] = elh * dst + _bdot(do, qe, TN)
            gk = dkd * kdh
            dblast = jnp.sum(gk, axis=0, keepdims=True) + delast * elh
            db = dqe * qe - gk + jnp.where(row == HG_CHUNK - 1, dblast, 0.0)
            dp = (msk_ref[HG_LEVELS] * da).astype(BF16)
            dq = dqe * ebh + _bdot(dp, kh, NN)
            dkk = dkd * edh + _bdot(dp, qh, TN)
            for l in range(HG_LEVELS):
                dp = (msk_ref[l] * da).astype(BF16)
                dql, dkl = _bdot(dp, kls[l], NN), _bdot(dp, qls[l], TN)
                w = cm["ws"][l][:, sl]
                dq = dq + dql * w
                dkk = dkk + dkl * w
                half = jnp.where(((row >> (HG_LEVELS - 1 - l)) & 1) == 1, 1.0, -1.0)
                dd = half * w * (dql * qh + dkl * kh)
                db = db + dd
                dr_h[l].append(-dd)
            dq_h.append(dq)
            dkk_h.append(dkk)
            db_h.append(db)
        cat = lambda xs: jnp.concatenate(xs, axis=1)
        cot = jnp.concatenate([cat(db_h)] + [cat(dr_h[l]) for l in range(HG_LEVELS)], axis=0)
        dlf = _dot01(stk_ref[...], cot, TN)
        dq, dkk = cat(dq_h), cat(dkk_h)
        dft = dlf / cm["ft"]
        one_lb = 1.0 - cm["lb"]
        dz_ref[0] = dq * (cm["sq"] * (1.0 + cm["qr"] * (1.0 - cm["sq"])))
        dz_ref[1] = (dft - dkk) * one_lb * cm["sg"] * cm["sgn"]
        dz_ref[2] = cat(dv_h)
        dz_ref[3] = cat(dgt_h)
        dlb_ref[...] += jnp.sum((dft - dkk) * cm["sgn"], axis=0, keepdims=True)
        dng_ref[...] += dng

    rev = lambda c: nc - 1 - c
    return pl.pallas_call(
        body, name=name, grid=(2, nc),
        in_specs=[pl.BlockSpec((4, None, HG_CHUNK, HG_SLOT), lambda g, c: (0, g, rev(c), 0)),
                  pl.BlockSpec((1, HG_SLOT), lambda g, c: (0, g)),
                  pl.BlockSpec((1, HG_K), lambda g, c: (0, 0)),
                  pl.BlockSpec(stk.shape, lambda g, c: (0, 0)),
                  pl.BlockSpec(msk.shape, lambda g, c: (0, 0, 0)),
                  pl.BlockSpec((None, 4, HG_K, HG_K), lambda g, c: (rev(c), g, 0, 0)),
                  pl.BlockSpec((HG_CHUNK, HG_SLOT), lambda g, c: (rev(c), g))],
        out_specs=[pl.BlockSpec((4, None, HG_CHUNK, HG_SLOT), lambda g, c: (0, g, rev(c), 0)),
                   pl.BlockSpec((1, HG_SLOT), lambda g, c: (0, g)),
                   pl.BlockSpec((None, 1, HG_K), lambda g, c: (g, 0, 0))],
        out_shape=[_sds(z.shape, F32), _sds((1, 2 * HG_SLOT), F32), _sds((2, 1, HG_K), F32)],
        scratch_shapes=[pltpu.VMEM((4, HG_K, HG_K), F32)],
        compiler_params=_params(("parallel", "arbitrary")),
    )(z, lb, ng, stk, msk, states, dog)


def _lb_fwd(name, logits):
    def body(l_ref, o_ref):
        x = l_ref[...]
        e = jnp.exp(x - jnp.max(x, axis=0, keepdims=True))
        s = e / jnp.sum(e, axis=0, keepdims=True)
        o_ref[0:1, :] = s[1:2]
        o_ref[1:2, :] = s[1:2] + s[2:3] + s[3:4]

    return pl.pallas_call(body, name=name, out_shape=_sds((2, logits.shape[1]), F32))(logits)


def _lb_bwd(name, logits, dlb):
    def body(l_ref, d_ref, o_ref):
        x = l_ref[...]
        e = jnp.exp(x - jnp.max(x, axis=0, keepdims=True))
        s = e / jnp.sum(e, axis=0, keepdims=True)
        d1, d3 = d_ref[0:1, :], d_ref[1:2, :]
        ds = [jnp.zeros_like(d1), d1 + d3, d3, d3]
        dot = sum(ds[r] * s[r:r + 1] for r in range(1, DEPTH))
        for r in range(DEPTH):
            o_ref[r:r + 1, :] = s[r:r + 1] * (ds[r] - dot)

    return pl.pallas_call(body, name=name, out_shape=_sds(logits.shape, F32))(logits, dlb)


def _conv_taps(buf, p, w, b, tm):
    return b + w[0:1] * buf[p, 6:6 + tm] + w[1:2] * buf[p, 7:7 + tm] + w[2:3] * buf[p, 8:8 + tm]


def _convgate_fwd(name, u, cw, cb):
    S = u.shape[2]
    tm = _tile(S, 512)

    def body(u_ref, w_ref, b_ref, a_ref, buf):
        @pl.when(pl.program_id(1) == 0)
        def _():
            buf[:, 0:8, :] = jnp.zeros((2, 8, FF_SLOT), F32)

        buf[:, 8:8 + tm, :] = u_ref[...]
        cg = _conv_taps(buf, 0, w_ref[0], b_ref[0], tm)
        cv = _conv_taps(buf, 1, w_ref[1], b_ref[1], tm)
        a_ref[...] = (cg * _sigmoid(cg) * cv).astype(a_ref.dtype)
        buf[:, 0:8, :] = buf[:, tm:tm + 8, :]

    return pl.pallas_call(
        body, name=name, grid=(4, S // tm),
        in_specs=[pl.BlockSpec((2, None, tm, FF_SLOT), lambda j, t: (0, j, t, 0)),
                  pl.BlockSpec((2, None, 3, FF_SLOT), lambda j, t: (0, j, 0, 0)),
                  pl.BlockSpec((2, None, 1, FF_SLOT), lambda j, t: (0, j, 0, 0))],
        out_specs=pl.BlockSpec((None, tm, FF_SLOT), lambda j, t: (j, t, 0)),
        out_shape=_sds((4, S, FF_SLOT), BF16),
        scratch_shapes=[pltpu.VMEM((2, tm + 8, FF_SLOT), F32)],
        compiler_params=_params(("parallel", "arbitrary")),
    )(u, cw, cb)


def _convgate_bwd(name, u, cw, cb, da):
    S = u.shape[2]
    tm = _tile(S, 512)
    nt = S // tm

    def body(u_ref, uh_ref, w_ref, b_ref, da_ref, du_ref, dw_ref, db_ref, buf, dbuf):
        t = pl.program_id(1)

        @pl.when(t == 0)
        def _():
            dbuf[:, tm:tm + 8, :] = jnp.zeros((2, 8, FF_SLOT), F32)
            dw_ref[...] = jnp.zeros_like(dw_ref)
            db_ref[...] = jnp.zeros_like(db_ref)

        buf[:, 0:8, :] = jnp.where(t < nt - 1, uh_ref[...], 0.0)
        buf[:, 8:8 + tm, :] = u_ref[...]
        cg = _conv_taps(buf, 0, w_ref[0], b_ref[0], tm)
        cv = _conv_taps(buf, 1, w_ref[1], b_ref[1], tm)
        sg = _sigmoid(cg)
        dav = da_ref[...].astype(F32)
        dc = [dav * cv * (sg * (1.0 + cg * (1.0 - sg))), dav * cg * sg]
        for p in range(2):
            dbuf[p, 0:tm, :] = dc[p]
            w = w_ref[p]
            du_ref[p] = w[2:3] * dbuf[p, 0:tm] + w[1:2] * dbuf[p, 1:tm + 1] + w[0:1] * dbuf[p, 2:tm + 2]
            dbuf[p, tm:tm + 8, :] = dc[p][0:8]
            for j in range(3):
                dw_ref[p, j:j + 1, :] += jnp.sum(dc[p] * buf[p, 6 + j:6 + j + tm], axis=0, keepdims=True)
            db_ref[p] += jnp.sum(dc[p], axis=0, keepdims=True)

    rev = lambda t: nt - 1 - t
    return pl.pallas_call(
        body, name=name, grid=(4, nt),
        in_specs=[pl.BlockSpec((2, None, tm, FF_SLOT), lambda j, t: (0, j, rev(t), 0)),
                  pl.BlockSpec((2, None, 8, FF_SLOT), lambda j, t: (0, j, jnp.maximum(rev(t) * (tm // 8) - 1, 0), 0)),
                  pl.BlockSpec((2, None, 3, FF_SLOT), lambda j, t: (0, j, 0, 0)),
                  pl.BlockSpec((2, None, 1, FF_SLOT), lambda j, t: (0, j, 0, 0)),
                  pl.BlockSpec((None, tm, FF_SLOT), lambda j, t: (j, rev(t), 0))],
        out_specs=[pl.BlockSpec((2, None, tm, FF_SLOT), lambda j, t: (0, j, rev(t), 0)),
                   pl.BlockSpec((2, None, 3, FF_SLOT), lambda j, t: (0, j, 0, 0)),
                   pl.BlockSpec((2, None, 1, FF_SLOT), lambda j, t: (0, j, 0, 0))],
        out_shape=[_sds(u.shape, F32), _sds(cw.shape, F32), _sds(cb.shape, F32)],
        scratch_shapes=[pltpu.VMEM((2, tm + 8, FF_SLOT), F32), pltpu.VMEM((2, tm + 8, FF_SLOT), F32)],
        compiler_params=_params(("parallel", "arbitrary")),
    )(u, u, cw, cb, da)


def _row_tile(R):
    for t in (256, 128):
        if R % t == 0:
            return t
    return R


def _sum_sources(name, gsrc):
    n, R, C = gsrc.shape
    tr = _row_tile(R)

    def body(g_ref, o_ref):
        g = g_ref[0].astype(F32)
        for s in range(1, n):
            g = g + g_ref[s].astype(F32)
        o_ref[...] = g

    return pl.pallas_call(
        body, name=name, grid=(R // tr,), in_specs=[pl.BlockSpec((n, tr, C), lambda i: (0, i, 0))],
        out_specs=pl.BlockSpec((tr, C), lambda i: (i, 0)), out_shape=_sds((R, C), F32),
        compiler_params=_params(("parallel",)),
    )(gsrc)


def _adamw(name, gsrc, w, m, v):
    n, R, C = gsrc.shape
    tr = _row_tile(R)

    def body(g_ref, w_ref, m_ref, v_ref, go_ref, d_ref, mo_ref, vo_ref):
        g = g_ref[0].astype(F32)
        for s in range(1, n):
            g = g + g_ref[s].astype(F32)
        m2 = ADAM_B1 * m_ref[...] + (1.0 - ADAM_B1) * g
        v2 = ADAM_B2 * v_ref[...] + (1.0 - ADAM_B2) * (g * g)
        m_hat = m2 / (1.0 - ADAM_B1 ** ADAM_STEP)
        v_hat = v2 / (1.0 - ADAM_B2 ** ADAM_STEP)
        go_ref[...] = g
        d_ref[...] = -ADAM_LR * (m_hat / (jnp.sqrt(v_hat) + ADAM_EPS) + ADAM_WD * w_ref[...])
        mo_ref[...] = m2
        vo_ref[...] = v2

    blk = pl.BlockSpec((tr, C), lambda i: (i, 0))
    return pl.pallas_call(
        body, name=name, grid=(R // tr,), in_specs=[pl.BlockSpec((n, tr, C), lambda i: (0, i, 0)), blk, blk, blk],
        out_specs=[blk] * 4, out_shape=[_sds((R, C), F32)] * 4, compiler_params=_params(("parallel",)),
    )(gsrc, w, m, v)


MESH = pl.DeviceIdType.MESH
HBM_SPEC = pl.BlockSpec(memory_space=pltpu.HBM)
N_PEERS = N_DEV - 1


def _mesh_place():
    x, y, c = lax.axis_index("x"), lax.axis_index("y"), lax.axis_index("c")
    peers = []
    for p in range(1, N_DEV):
        px = 1 - x if p & 4 else x
        py = 1 - y if p & 2 else y
        pc = 1 - c if p & 1 else c
        peers.append(((px, py, pc), 4 * px + 2 * py + pc))
    return 4 * x + 2 * y + c, peers


def _gather_shards(name, shards):
    n = len(shards)

    def body(*refs):
        src, dst = refs[:n], refs[n:2 * n]
        send, recv, loc = refs[2 * n:]
        me, peers = _mesh_place()
        own = [pltpu.make_async_copy(src[k], dst[k].at[:, me], loc.at[k]) for k in range(n)]
        for cp in own:
            cp.start()
        sends = []
        for k in range(n):
            for p, (dev, _) in enumerate(peers):
                cp = pltpu.make_async_remote_copy(src_ref=src[k], dst_ref=dst[k].at[:, me], send_sem=send.at[k, p],
                                                  recv_sem=recv.at[k, p], device_id=dev, device_id_type=MESH)
                cp.start()
                sends.append(cp)
        for k in range(n):
            for p, (dev, idx) in enumerate(peers):
                pltpu.make_async_remote_copy(src_ref=src[k], dst_ref=dst[k].at[:, idx], send_sem=send.at[k, p],
                                             recv_sem=recv.at[k, p], device_id=dev, device_id_type=MESH).wait_recv()
        for cp in sends:
            cp.wait_send()
        for cp in own:
            cp.wait()

    out_shape = [pltpu.HBM((s.shape[0], N_DEV) + s.shape[1:], s.dtype) for s in shards]
    return pl.pallas_call(
        body, name=name, in_specs=[HBM_SPEC] * n, out_specs=[HBM_SPEC] * n, out_shape=out_shape,
        scratch_shapes=[pltpu.SemaphoreType.DMA((n, N_PEERS)), pltpu.SemaphoreType.DMA((n, N_PEERS)),
                        pltpu.SemaphoreType.DMA((n,))],
    )(*shards)


def _scatter_grads(name, kinds):
    n = len(kinds)
    flat = [g for layers in kinds for g in layers]
    first = [sum(len(layers) for layers in kinds[:k]) for k in range(n)]

    def body(*refs):
        src, dst = refs[:len(flat)], refs[len(flat):len(flat) + n]
        send, recv, loc = refs[len(flat) + n:]
        me, peers = _mesh_place()
        own = []
        for k in range(n):
            for l in range(len(kinds[k])):
                g = src[first[k] + l]
                cp = pltpu.make_async_copy(g.at[me], dst[k].at[me, l], loc.at[first[k] + l])
                cp.start()
                own.append(cp)
                for p, (dev, idx) in enumerate(peers):
                    pltpu.make_async_remote_copy(src_ref=g.at[idx], dst_ref=dst[k].at[me, l], send_sem=send.at[k, p],
                                                 recv_sem=recv.at[k, p], device_id=dev, device_id_type=MESH).start()
        for k in range(n):
            for p, (dev, idx) in enumerate(peers):
                pltpu.make_async_remote_copy(src_ref=dst[k].at[me], dst_ref=dst[k].at[idx], send_sem=send.at[k, p],
                                             recv_sem=recv.at[k, p], device_id=dev, device_id_type=MESH).wait()
        for cp in own:
            cp.wait()

    out_shape = [pltpu.HBM((N_DEV, len(layers)) + layers[0].shape[1:], layers[0].dtype) for layers in kinds]
    return pl.pallas_call(
        body, name=name, in_specs=[HBM_SPEC] * len(flat), out_specs=[HBM_SPEC] * n, out_shape=out_shape,
        scratch_shapes=[pltpu.SemaphoreType.DMA((n, N_PEERS)), pltpu.SemaphoreType.DMA((n, N_PEERS)),
                        pltpu.SemaphoreType.DMA((len(flat),))],
    )(*flat)


def _allsum_rows(name, part):
    R, C = part.shape

    def body(p_ref, o_ref, gath, send, recv):
        me, peers = _mesh_place()
        gath[me] = p_ref[...]
        sends = []
        for p, (dev, _) in enumerate(peers):
            cp = pltpu.make_async_remote_copy(src_ref=p_ref, dst_ref=gath.at[me], send_sem=send.at[p], recv_sem=recv.at[p],
                                              device_id=dev, device_id_type=MESH)
            cp.start()
            sends.append(cp)
        for p, (dev, idx) in enumerate(peers):
            pltpu.make_async_remote_copy(src_ref=p_ref, dst_ref=gath.at[idx], send_sem=send.at[p], recv_sem=recv.at[p],
                                         device_id=dev, device_id_type=MESH).wait_recv()
        for cp in sends:
            cp.wait_send()
        tot = gath[0]
        for j in range(1, N_DEV):
            tot = tot + gath[j]
        o_ref[...] = tot

    vm = pl.BlockSpec(memory_space=pltpu.VMEM)
    return pl.pallas_call(
        body, name=name, in_specs=[vm], out_specs=vm, out_shape=_sds((R, C), F32),
        scratch_shapes=[pltpu.VMEM((N_DEV, R, C), F32), pltpu.SemaphoreType.DMA((N_PEERS,)),
                        pltpu.SemaphoreType.DMA((N_PEERS,))],
        compiler_params=pltpu.CompilerParams(vmem_limit_bytes=VMEM_LIMIT),
    )(part)


def _rows(a, width=D_MODEL):
    flat = a.reshape(-1)
    pad = (-flat.shape[0]) % width
    if pad:
        flat = jnp.concatenate([flat, jnp.zeros((pad,), flat.dtype)])
    return flat.reshape(-1, width)


def _pack_rows(parts):
    rows = jnp.concatenate([_rows(p) for p in parts], axis=0)
    pad = (-rows.shape[0]) % 8
    if pad:
        rows = jnp.concatenate([rows, jnp.zeros((pad, D_MODEL), rows.dtype)], axis=0)
    return rows


def _unpack_rows(rows, shapes):
    out, at = [], 0
    for s in shapes:
        size = int(np.prod(s))
        n = -(-size // D_MODEL)
        out.append(rows[at:at + n].reshape(-1)[:size].reshape(s))
        at += n
    return out


def kernel(x, norm_mix, norm_ffn, norm_final, attn_w_in, attn_w_out, attn_sinks, hgrn_w_in, hgrn_w_out, hgrn_norm, hgrn_lb_logits, ffn_w_up, ffn_conv_w, ffn_conv_b, ffn_w_down, loss_target, m_norm_mix, m_norm_ffn, m_norm_final, m_attn_w_in, m_attn_w_out, m_attn_sinks, m_hgrn_w_in, m_hgrn_w_out, m_hgrn_norm, m_hgrn_lb_logits, m_ffn_w_up, m_ffn_conv_w, m_ffn_conv_b, m_ffn_w_down, v_norm_mix, v_norm_ffn, v_norm_final, v_attn_w_in, v_attn_w_out, v_attn_sinks, v_hgrn_w_in, v_hgrn_w_out, v_hgrn_norm, v_hgrn_lb_logits, v_ffn_w_up, v_ffn_conv_w, v_ffn_conv_b, v_ffn_w_down):
    S = x.shape[1]
    n_attn, n_hgrn = attn_w_in.shape[0], hgrn_w_in.shape[0]
    me = 4 * lax.axis_index("x") + 2 * lax.axis_index("y") + lax.axis_index("c")

    gathered = _gather_shards("gather_weights", [
        attn_w_in.transpose(0, 2, 1).astype(BF16), attn_w_out.astype(BF16), hgrn_w_in.astype(BF16),
        hgrn_w_out.astype(BF16), ffn_w_up.astype(BF16), ffn_w_down.astype(BF16), ffn_conv_w])
    wa_in = gathered[0].reshape(n_attn, ATTN_IN, D_MODEL)
    wa_out = gathered[1].reshape(n_attn, D_MODEL, D_MODEL)
    wh_in = gathered[2]
    wh_out = gathered[3].reshape(n_hgrn, D_MODEL, D_MODEL)
    wf_up = gathered[4]
    wf_down = gathered[5].reshape(DEPTH, 4, FF_SLOT, D_MODEL)
    conv_w = gathered[6].reshape(DEPTH, 2, 4, 3, FF_SLOT)
    conv_b = ffn_conv_b.reshape(DEPTH, 2, 4, 1, FF_SLOT)
    lb = _lb_fwd("lb_fwd", hgrn_lb_logits)

    saved = []
    h = x[0]
    for l in range(DEPTH):
        i = l // 2
        hn = _rmsnorm_fwd(f"norm_mix_fwd{l}", h, norm_mix[l:l + 1])
        if l % 2 == 0:
            proj = _proj_rows(f"attn_proj{i}", hn, wa_in, i, BF16)
            o = _attn_fwd(f"attn_fwd{i}", proj, attn_sinks[i:i + 1])
            h2 = _out_proj(f"attn_out{i}", o, wa_out, i, h)
            mix = (proj, o)
        else:
            z = _proj_slots(f"hgrn_proj{i}", hn, wh_in, i).reshape(4, 2, S, HG_SLOT)
            o, states = _hg_fwd(f"hgrn_fwd{i}", z, lb[i:i + 1], hgrn_norm[i:i + 1])
            h2 = _out_proj(f"hgrn_out{i}", o, wh_out, i, h)
            mix = (z, o, states)
        hn2 = _rmsnorm_fwd(f"norm_ffn_fwd{l}", h2, norm_ffn[l:l + 1])
        u = _proj_slots(f"ffn_up{l}", hn2, wf_up, l).reshape(2, 4, S, FF_SLOT)
        a = _convgate_fwd(f"ffn_gate{l}", u, conv_w[l], conv_b[l])
        h3 = _down_proj(f"ffn_down{l}", a, wf_down, l, h2)
        saved.append((h, hn, mix, h2, hn2, u, a))
        h = h3
    dh, d_norm_final, loss_rows = _loss_head("loss_head", h, norm_final[None], loss_target[0])

    g_attn_in, g_attn_out, g_hgrn_in, g_hgrn_out = [None] * n_attn, [None] * n_attn, [None] * n_hgrn, [None] * n_hgrn
    g_up, g_down, d_conv_w, d_conv_b = [None] * DEPTH, [None] * DEPTH, [None] * DEPTH, [None] * DEPTH
    d_norm_mix, d_norm_ffn = [None] * DEPTH, [None] * DEPTH
    d_sinks, d_lb, d_hgrn_norm = [None] * n_attn, [None] * n_hgrn, [None] * n_hgrn
    for l in reversed(range(DEPTH)):
        i = l // 2
        h0, hn, mix, h2, hn2, u, a = saved[l]
        da = _dgrad_down(f"ffn_down_dgrad{l}", dh, wf_down, l)
        g_down[l] = _wgrad_down(f"ffn_down_wgrad{l}", a, dh).reshape(N_DEV, D_FF // N_DEV, D_MODEL)
        du, d_conv_w[l], d_conv_b[l] = _convgate_bwd(f"ffn_gate_bwd{l}", u, conv_w[l], conv_b[l], da)
        du = du.reshape(N_DEV, S, FF_SLOT)
        dhn2 = _dgrad_slots(f"ffn_up_dgrad{l}", du, wf_up, l)
        g_up[l] = _wgrad_slots(f"ffn_up_wgrad{l}", hn2, du)
        dh, d_norm_ffn[l] = _rmsnorm_bwd(f"norm_ffn_bwd{l}", h2, norm_ffn[l:l + 1], dhn2, dh)
        if l % 2 == 0:
            proj, o = mix
            do = _dgrad_out(f"attn_out_dgrad{i}", dh, wa_out, i, BF16)
            g_attn_out[i] = _wgrad_rows(f"attn_out_wgrad{i}", o, dh).reshape(N_DEV, D_MODEL // N_DEV, D_MODEL)
            dproj, d_sinks[i] = _attn_bwd(f"attn_bwd{i}", proj, attn_sinks[i:i + 1], do)
            dhn = _dgrad_rows(f"attn_proj_dgrad{i}", dproj, wa_in, i)
            g_attn_in[i] = _wgrad_rows(f"attn_proj_wgrad{i}", dproj, hn).reshape(N_DEV, ATTN_IN // N_DEV, D_MODEL)
        else:
            z, o, states = mix
            dog = _dgrad_out(f"hgrn_out_dgrad{i}", dh, wh_out, i, F32)
            g_hgrn_out[i] = _wgrad_rows(f"hgrn_out_wgrad{i}", o, dh).reshape(N_DEV, D_MODEL // N_DEV, D_MODEL)
            dz, d_lb[i], dng = _hg_bwd(f"hgrn_bwd{i}", z, lb[i:i + 1], hgrn_norm[i:i + 1], states, dog)
            d_hgrn_norm[i] = dng[0] + dng[1]
            dz = dz.reshape(N_DEV, S, HG_SLOT)
            dhn = _dgrad_slots(f"hgrn_proj_dgrad{i}", dz, wh_in, i)
            g_hgrn_in[i] = _wgrad_slots(f"hgrn_proj_wgrad{i}", hn, dz)
        dh, d_norm_mix[l] = _rmsnorm_bwd(f"norm_mix_bwd{l}", h0, norm_mix[l:l + 1], dhn, dh)
    grad_x = dh[None]

    r_attn_in, r_attn_out, r_hgrn_in, r_hgrn_out, r_up, r_down = _scatter_grads(
        "scatter_grads", [g_attn_in, g_attn_out, g_hgrn_in, g_hgrn_out, g_up, g_down])
    small_shapes = [(DEPTH, D_MODEL), (DEPTH, D_MODEL), (1, D_MODEL), (1, D_MODEL), (n_hgrn, D_MODEL), (n_attn, 128),
                    (n_hgrn, HG_K), (DEPTH, 2 * D_FF), (DEPTH, N_DEV, 3, FF_SLOT)]
    total = _allsum_rows("allsum_small", _pack_rows([
        jnp.concatenate(d_norm_mix), jnp.concatenate(d_norm_ffn), d_norm_final, loss_rows, jnp.concatenate(d_lb),
        jnp.concatenate(d_sinks), jnp.concatenate(d_hgrn_norm), jnp.stack(d_conv_b), jnp.stack(d_conv_w)]))
    (g_norm_mix, g_norm_ffn, g_norm_final, loss_sum, g_lb, g_sinks, g_hgrn_norm, g_conv_b, g_conv_w_all) = _unpack_rows(
        total, small_shapes)
    loss = jnp.sum(loss_sum)
    g_norm_final = g_norm_final[0]
    g_sinks = g_sinks[:, :N_Q_HEADS]
    g_lb_logits = _lb_bwd("lb_bwd", hgrn_lb_logits, g_lb)
    g_conv_w = lax.dynamic_index_in_dim(g_conv_w_all, me, axis=1, keepdims=False)

    def update(name, gsrc, w, m, v):
        outs = _adamw(name, gsrc.reshape(gsrc.shape[0], -1, w.shape[-1]), *[t.reshape(-1, w.shape[-1]) for t in (w, m, v)])
        return [t.reshape(w.shape) for t in outs]

    g_t = _sum_sources("sum_attn_in", r_attn_in.reshape(N_DEV, -1, D_MODEL)).reshape(n_attn, ATTN_IN // N_DEV, D_MODEL)
    big = {
        "attn_w_in": update("adamw_attn_in", g_t.transpose(0, 2, 1)[None], attn_w_in, m_attn_w_in, v_attn_w_in),
        "attn_w_out": update("adamw_attn_out", r_attn_out, attn_w_out, m_attn_w_out, v_attn_w_out),
        "hgrn_w_in": update("adamw_hgrn_in", r_hgrn_in, hgrn_w_in, m_hgrn_w_in, v_hgrn_w_in),
        "hgrn_w_out": update("adamw_hgrn_out", r_hgrn_out, hgrn_w_out, m_hgrn_w_out, v_hgrn_w_out),
        "ffn_w_up": update("adamw_ffn_up", r_up, ffn_w_up, m_ffn_w_up, v_ffn_w_up),
        "ffn_w_down": update("adamw_ffn_down", r_down, ffn_w_down, m_ffn_w_down, v_ffn_w_down),
        "ffn_conv_w": update("adamw_conv_w", g_conv_w[None], ffn_conv_w, m_ffn_conv_w, v_ffn_conv_w),
    }
    small_w = [norm_mix, norm_ffn, norm_final, attn_sinks, hgrn_norm, hgrn_lb_logits, ffn_conv_b]
    small_m = [m_norm_mix, m_norm_ffn, m_norm_final, m_attn_sinks, m_hgrn_norm, m_hgrn_lb_logits, m_ffn_conv_b]
    small_v = [v_norm_mix, v_norm_ffn, v_norm_final, v_attn_sinks, v_hgrn_norm, v_hgrn_lb_logits, v_ffn_conv_b]
    small_g = [g_norm_mix, g_norm_ffn, g_norm_final, g_sinks, g_hgrn_norm, g_lb_logits, g_conv_b]
    outs = _adamw("adamw_small", _pack_rows(small_g)[None], _pack_rows(small_w), _pack_rows(small_m), _pack_rows(small_v))
    shapes = [w.shape for w in small_w]
    small = {n: [t[j] for t in [_unpack_rows(o, shapes) for o in outs]]
             for j, n in enumerate(["norm_mix", "norm_ffn", "norm_final", "attn_sinks", "hgrn_norm", "hgrn_lb_logits", "ffn_conv_b"])}
    order = ["norm_mix", "norm_ffn", "norm_final", "attn_w_in", "attn_w_out", "attn_sinks", "hgrn_w_in", "hgrn_w_out",
             "hgrn_norm", "hgrn_lb_logits", "ffn_w_up", "ffn_conv_w", "ffn_conv_b", "ffn_w_down"]
    res = {**big, **small}
    return (loss, grad_x, *[res[n][0] for n in order], *[res[n][1] for n in order], *[res[n][2] for n in order],
            *[res[n][3] for n in order])
```

```python
import numpy as np
import jax
import jax.numpy as jnp
from jax import lax
from jax.experimental import pallas as pl
from jax.experimental.pallas import tpu as pltpu

F32 = jnp.float32
BF16 = jnp.bfloat16

D_MODEL = 1024
DEPTH = 4
HEAD_DIM = 64
N_Q_HEADS = 16
N_KV_HEADS = 4
Q_PER_KV = 4
ATTN_BLOCK = 128
ATTN_IN = 1536
HG_HEADS = 8
HG_K = 128
HG_CHUNK = 64
HG_IN = 4096
D_FF = 2816
EPS = 1e-6
N_DEV = 8
FF_SLOT = 2 * D_FF // N_DEV
HG_SLOT = HG_IN // N_DEV
HG_LEVELS = 6

ADAM_LR = 0.001
ADAM_B1 = 0.9
ADAM_B2 = 0.999
ADAM_EPS = 1e-08
ADAM_WD = 0.01
ADAM_STEP = 10

VMEM_LIMIT = 56 * 1024 * 1024
NEG_BIG = -1e30

NN = (((1,), (0,)), ((), ()))
NT = (((1,), (1,)), ((), ()))
TN = (((0,), (0,)), ((), ()))


def _bdot(a, b, dn):
    return lax.dot_general(a.astype(BF16), b.astype(BF16), dn, preferred_element_type=F32)


def _sds(shape, dtype):
    return jax.ShapeDtypeStruct(tuple(shape), dtype)


def _params(sem):
    return pltpu.CompilerParams(dimension_semantics=sem, vmem_limit_bytes=VMEM_LIMIT)


DEP_SHAPE = (8, 128)


def _dep_spec(rank):
    return pl.BlockSpec(DEP_SHAPE, lambda *_: (0, 0))


def _matmul(name, a, b, *, dn, grid, a_spec, b_spec, o_spec, out_shape, acc_shape=None, res=None, res_spec=None, dep=None):
    nk = grid[2]
    n_in = 2 + (res is not None) + (dep is not None)

    def body(*refs):
        a_ref, b_ref = refs[0], refs[1]
        r_ref = refs[2] if res is not None else None
        o_ref = refs[n_in]

        def prod():
            return _bdot(a_ref[...], b_ref[...], dn)

        def finish(v):
            if r_ref is not None:
                v = v + r_ref[...]
            o_ref[...] = v.astype(o_ref.dtype)

        if nk == 1:
            finish(prod())
        else:
            acc = refs[-1]
            k = pl.program_id(2)

            @pl.when(k == 0)
            def _():
                acc[...] = prod()

            @pl.when(k > 0)
            def _():
                acc[...] += prod()

            @pl.when(k == nk - 1)
            def _():
                finish(acc[...])

    in_specs = [a_spec, b_spec] + ([res_spec] if res is not None else []) + ([_dep_spec(3)] if dep is not None else [])
    args = (a, b) + ((res,) if res is not None else ()) + ((dep,) if dep is not None else ())
    scratch = [] if nk == 1 else [pltpu.VMEM(acc_shape, F32)]
    return pl.pallas_call(
        body, name=name, grid=grid, in_specs=in_specs, out_specs=o_spec, out_shape=out_shape,
        scratch_shapes=scratch, compiler_params=_params(("parallel", "parallel", "arbitrary")),
    )(*args)


def _tile(n, t):
    return min(n, t)


def _proj_rows(name, hn, wt, l, out_dtype):
    S, N = hn.shape[0], wt.shape[1]
    tm, tn = _tile(S, 512), 512
    return _matmul(
        name, hn, wt, dn=NT, grid=(S // tm, N // tn, 1),
        a_spec=pl.BlockSpec((tm, D_MODEL), lambda i, j, k: (i, 0)),
        b_spec=pl.BlockSpec((None, tn, D_MODEL), lambda i, j, k: (l, j, 0)),
        o_spec=pl.BlockSpec((tm, tn), lambda i, j, k: (i, j)),
        out_shape=_sds((S, N), out_dtype))


def _proj_slots(name, hn, w, l):
    S, r = hn.shape[0], w.shape[3]
    tm = _tile(S, 512)
    return _matmul(
        name, hn, w, dn=NN, grid=(N_DEV, S // tm, 1),
        a_spec=pl.BlockSpec((tm, D_MODEL), lambda j, i, k: (i, 0)),
        b_spec=pl.BlockSpec((None, None, D_MODEL, r), lambda j, i, k: (l, j, 0, 0)),
        o_spec=pl.BlockSpec((None, tm, r), lambda j, i, k: (j, i, 0)),
        out_shape=_sds((N_DEV, S, r), F32))


def _out_proj(name, o, w, l, h):
    S, K = o.shape
    tm = _tile(S, 512)
    return _matmul(
        name, o, w, dn=NN, grid=(S // tm, 1, 1),
        a_spec=pl.BlockSpec((tm, K), lambda i, j, k: (i, 0)),
        b_spec=pl.BlockSpec((None, K, D_MODEL), lambda i, j, k: (l, 0, 0)),
        o_spec=pl.BlockSpec((tm, D_MODEL), lambda i, j, k: (i, 0)),
        out_shape=_sds((S, D_MODEL), F32), res=h,
        res_spec=pl.BlockSpec((tm, D_MODEL), lambda i, j, k: (i, 0)))


def _down_proj(name, a, w, l, h):
    nj, S, r = a.shape
    tm = _tile(S, 512)
    return _matmul(
        name, a, w, dn=NN, grid=(S // tm, 1, nj),
        a_spec=pl.BlockSpec((None, tm, r), lambda i, j, k: (k, i, 0)),
        b_spec=pl.BlockSpec((None, None, r, D_MODEL), lambda i, j, k: (l, k, 0, 0)),
        o_spec=pl.BlockSpec((tm, D_MODEL), lambda i, j, k: (i, 0)),
        out_shape=_sds((S, D_MODEL), F32), acc_shape=(tm, D_MODEL), res=h,
        res_spec=pl.BlockSpec((tm, D_MODEL), lambda i, j, k: (i, 0)))


def _dgrad_down(name, dh, w, l, dep=None):
    S = dh.shape[0]
    nj, r = w.shape[1], w.shape[2]
    tm = _tile(S, 512)
    return _matmul(
        name, dh, w, dn=NT, grid=(nj, S // tm, 1),
        a_spec=pl.BlockSpec((tm, D_MODEL), lambda j, i, k: (i, 0)),
        b_spec=pl.BlockSpec((None, None, r, D_MODEL), lambda j, i, k: (l, j, 0, 0)),
        o_spec=pl.BlockSpec((None, tm, r), lambda j, i, k: (j, i, 0)),
        out_shape=_sds((nj, S, r), BF16), dep=dep)


def _wgrad_down(name, a, dh):
    nj, S, r = a.shape
    tk = _tile(S, 512)
    return _matmul(
        name, a, dh, dn=TN, grid=(nj, 1, S // tk),
        a_spec=pl.BlockSpec((None, tk, r), lambda s, j, k: (s, k, 0)),
        b_spec=pl.BlockSpec((tk, D_MODEL), lambda s, j, k: (k, 0)),
        o_spec=pl.BlockSpec((None, r, D_MODEL), lambda s, j, k: (s, 0, 0)),
        out_shape=_sds((nj, r, D_MODEL), BF16), acc_shape=(r, D_MODEL))


def _dgrad_slots(name, dz, w, l):
    nj, S, r = dz.shape
    tm = _tile(S, 512)
    return _matmul(
        name, dz, w, dn=NT, grid=(S // tm, 1, nj),
        a_spec=pl.BlockSpec((None, tm, r), lambda i, j, k: (k, i, 0)),
        b_spec=pl.BlockSpec((None, None, D_MODEL, r), lambda i, j, k: (l, k, 0, 0)),
        o_spec=pl.BlockSpec((tm, D_MODEL), lambda i, j, k: (i, 0)),
        out_shape=_sds((S, D_MODEL), F32), acc_shape=(tm, D_MODEL))


def _wgrad_slots(name, hn, dz):
    nj, S, r = dz.shape
    tk = _tile(S, 512)
    return _matmul(
        name, hn, dz, dn=TN, grid=(nj, 1, S // tk),
        a_spec=pl.BlockSpec((tk, D_MODEL), lambda s, j, k: (k, 0)),
        b_spec=pl.BlockSpec((None, tk, r), lambda s, j, k: (s, k, 0)),
        o_spec=pl.BlockSpec((None, D_MODEL, r), lambda s, j, k: (s, 0, 0)),
        out_shape=_sds((nj, D_MODEL, r), BF16), acc_shape=(D_MODEL, r))


def _dgrad_out(name, dh, w, l, out_dtype, dep=None):
    S, K = dh.shape[0], w.shape[1]
    tm = _tile(S, 512)
    return _matmul(
        name, dh, w, dn=NT, grid=(S // tm, 1, 1),
        a_spec=pl.BlockSpec((tm, D_MODEL), lambda i, j, k: (i, 0)),
        b_spec=pl.BlockSpec((None, K, D_MODEL), lambda i, j, k: (l, 0, 0)),
        o_spec=pl.BlockSpec((tm, K), lambda i, j, k: (i, 0)),
        out_shape=_sds((S, K), out_dtype), dep=dep)


def _wgrad_rows(name, a, b):
    S, K = a.shape
    tk = _tile(S, 512)
    return _matmul(
        name, a, b, dn=TN, grid=(1, 1, S // tk),
        a_spec=pl.BlockSpec((tk, K), lambda i, j, k: (k, 0)),
        b_spec=pl.BlockSpec((tk, D_MODEL), lambda i, j, k: (k, 0)),
        o_spec=pl.BlockSpec((K, D_MODEL), lambda i, j, k: (0, 0)),
        out_shape=_sds((K, D_MODEL), BF16), acc_shape=(K, D_MODEL))


def _dgrad_rows(name, dz, wt, l):
    S, N = dz.shape
    tm = _tile(S, 512)
    return _matmul(
        name, dz, wt, dn=NN, grid=(S // tm, 1, 1),
        a_spec=pl.BlockSpec((tm, N), lambda i, j, k: (i, 0)),
        b_spec=pl.BlockSpec((None, N, D_MODEL), lambda i, j, k: (l, 0, 0)),
        o_spec=pl.BlockSpec((tm, D_MODEL), lambda i, j, k: (i, 0)),
        out_shape=_sds((S, D_MODEL), F32))


def _rmsnorm_fwd(name, h, g, dep=None):
    S = h.shape[0]
    tm = _tile(S, 512)

    def body(h_ref, g_ref, *rest):
        o_ref = rest[-1]
        x = h_ref[...]
        r = lax.rsqrt(jnp.mean(x * x, axis=-1, keepdims=True) + EPS)
        o_ref[...] = (x * r * g_ref[...]).astype(o_ref.dtype)

    row = pl.BlockSpec((tm, D_MODEL), lambda i: (i, 0))
    deps = () if dep is None else (dep,)
    return pl.pallas_call(
        body, name=name, grid=(S // tm,),
        in_specs=[row, pl.BlockSpec((1, D_MODEL), lambda i: (0, 0))] + [_dep_spec(1)] * len(deps),
        out_specs=row, out_shape=_sds((S, D_MODEL), BF16), compiler_params=_params(("parallel",)),
    )(h, g, *deps)


def _rmsnorm_bwd(name, h, g, dy, dres):
    S = h.shape[0]
    tm = _tile(S, 512)

    def body(h_ref, g_ref, dy_ref, dres_ref, dh_ref, dg_ref):
        x = h_ref[...]
        r = lax.rsqrt(jnp.mean(x * x, axis=-1, keepdims=True) + EPS)
        xh = x * r
        dyf = dy_ref[...].astype(F32)
        dyg = dyf * g_ref[...]
        dh_ref[...] = dres_ref[...] + r * (dyg - xh * jnp.mean(dyg * xh, axis=-1, keepdims=True))
        part = jnp.sum(dyf * xh, axis=0, keepdims=True)

        @pl.when(pl.program_id(0) == 0)
        def _():
            dg_ref[...] = part

        @pl.when(pl.program_id(0) > 0)
        def _():
            dg_ref[...] += part

    row = pl.BlockSpec((tm, D_MODEL), lambda i: (i, 0))
    vec = pl.BlockSpec((1, D_MODEL), lambda i: (0, 0))
    return pl.pallas_call(
        body, name=name, grid=(S // tm,), in_specs=[row, vec, row, row], out_specs=[row, vec],
        out_shape=[_sds((S, D_MODEL), F32), _sds((1, D_MODEL), F32)], compiler_params=_params(("arbitrary",)),
    )(h, g, dy, dres)


def _loss_head(name, h, g, target):
    S = h.shape[0]
    tm = _tile(S, 512)

    def body(h_ref, g_ref, t_ref, dh_ref, dg_ref, ls_ref):
        x = h_ref[...]
        r = lax.rsqrt(jnp.mean(x * x, axis=-1, keepdims=True) + EPS)
        xh = x * r
        diff = xh * g_ref[...] - t_ref[...]
        dyf = diff * (1.0 / D_MODEL)
        dyg = dyf * g_ref[...]
        dh_ref[...] = r * (dyg - xh * jnp.mean(dyg * xh, axis=-1, keepdims=True))
        part = jnp.sum(dyf * xh, axis=0, keepdims=True)
        lpart = jnp.sum(diff * diff, axis=0, keepdims=True) * (0.5 / D_MODEL)

        @pl.when(pl.program_id(0) == 0)
        def _():
            dg_ref[...] = part
            ls_ref[...] = lpart

        @pl.when(pl.program_id(0) > 0)
        def _():
            dg_ref[...] += part
            ls_ref[...] += lpart

    row = pl.BlockSpec((tm, D_MODEL), lambda i: (i, 0))
    vec = pl.BlockSpec((1, D_MODEL), lambda i: (0, 0))
    return pl.pallas_call(
        body, name=name, grid=(S // tm,), in_specs=[row, vec, row], out_specs=[row, vec, vec],
        out_shape=[_sds((S, D_MODEL), F32), _sds((1, D_MODEL), F32), _sds((1, D_MODEL), F32)],
        compiler_params=_params(("arbitrary",)),
    )(h, g, target)


ATTN_SCALE = HEAD_DIM ** -0.5
ALIBI_SLOPES = [2.0 ** (-8.0 * (h + 1) / N_Q_HEADS) for h in range(N_Q_HEADS)]
K_COL = N_Q_HEADS * HEAD_DIM
KV_COLS = N_KV_HEADS * HEAD_DIM
V_COL = K_COL + KV_COLS


def _attn_masks(n):
    qi = lax.broadcasted_iota(jnp.int32, (ATTN_BLOCK, ATTN_BLOCK), 0)
    ki = lax.broadcasted_iota(jnp.int32, (ATTN_BLOCK, ATTN_BLOCK), 1)
    dist_c = (qi - ki).astype(F32)
    return dist_c + float(ATTN_BLOCK), dist_c, (ki > qi) & (n > 0), qi >= ki


def _attn_probs(q, kp, kc, sink, slope, masks):
    dist_p, dist_c, valid_p, valid_c = masks
    sp = jnp.where(valid_p, _bdot(q, kp, NT) * ATTN_SCALE - slope * dist_p, NEG_BIG)
    sc = jnp.where(valid_c, _bdot(q, kc, NT) * ATTN_SCALE - slope * dist_c, NEG_BIG)
    m = jnp.maximum(jnp.maximum(jnp.max(sp, axis=-1, keepdims=True), jnp.max(sc, axis=-1, keepdims=True)), sink)
    ep, ec, es = jnp.exp(sp - m), jnp.exp(sc - m), jnp.exp(sink - m)
    inv = 1.0 / (jnp.sum(ep, axis=-1, keepdims=True) + jnp.sum(ec, axis=-1, keepdims=True) + es)
    return ep * inv, ec * inv, es * inv


def _attn_specs(nblk):
    last = nblk - 1
    kcol, vcol = K_COL // KV_COLS, V_COL // KV_COLS
    return [
        pl.BlockSpec((ATTN_BLOCK, K_COL), lambda n: (jnp.minimum(n, last), 0)),
        pl.BlockSpec((ATTN_BLOCK, KV_COLS), lambda n: (jnp.minimum(n, last), kcol)),
        pl.BlockSpec((ATTN_BLOCK, KV_COLS), lambda n: (jnp.maximum(jnp.minimum(n, last) - 1, 0), kcol)),
        pl.BlockSpec((ATTN_BLOCK, KV_COLS), lambda n: (jnp.minimum(n, last), vcol)),
        pl.BlockSpec((ATTN_BLOCK, KV_COLS), lambda n: (jnp.maximum(jnp.minimum(n, last) - 1, 0), vcol)),
    ]


def _attn_fwd(name, proj, sinks):
    S = proj.shape[0]
    nblk = S // ATTN_BLOCK

    def body(q_ref, kc_ref, kp_ref, vc_ref, vp_ref, sk_ref, o_ref):
        masks = _attn_masks(pl.program_id(0))
        for m in range(N_KV_HEADS):
            ks = slice(HEAD_DIM * m, HEAD_DIM * (m + 1))
            kp, kc, vp, vc = kp_ref[:, ks], kc_ref[:, ks], vp_ref[:, ks], vc_ref[:, ks]
            for g in range(Q_PER_KV):
                hh = Q_PER_KV * m + g
                qs = slice(HEAD_DIM * hh, HEAD_DIM * (hh + 1))
                pp, pc, _ = _attn_probs(q_ref[:, qs], kp, kc, sk_ref[0, hh], ALIBI_SLOPES[hh], masks)
                o_ref[:, qs] = (_bdot(pp, vp, NN) + _bdot(pc, vc, NN)).astype(o_ref.dtype)

    return pl.pallas_call(
        body, name=name, grid=(nblk,),
        in_specs=_attn_specs(nblk) + [pl.BlockSpec(memory_space=pltpu.SMEM)],
        out_specs=pl.BlockSpec((ATTN_BLOCK, K_COL), lambda n: (n, 0)),
        out_shape=_sds((S, K_COL), BF16), compiler_params=_params(("parallel",)),
    )(proj, proj, proj, proj, proj, sinks)


def _attn_bwd(name, proj, sinks, do):
    S = proj.shape[0]
    nblk = S // ATTN_BLOCK

    def body(q_ref, kc_ref, kp_ref, vc_ref, vp_ref, do_ref, sk_ref, dz_ref, ds_ref, carry, cur, padd):
        n = pl.program_id(0)

        @pl.when(n == 0)
        def _():
            carry[...] = jnp.zeros_like(carry)
            ds_ref[...] = jnp.zeros_like(ds_ref)

        @pl.when(n < nblk)
        def _():
            masks = _attn_masks(n)
            lane = lax.broadcasted_iota(jnp.int32, (1, 128), 1)
            dsv = jnp.zeros((1, 128), F32)
            for m in range(N_KV_HEADS):
                ks = slice(HEAD_DIM * m, HEAD_DIM * (m + 1))
                kp, kc, vp, vc = kp_ref[:, ks], kc_ref[:, ks], vp_ref[:, ks], vc_ref[:, ks]
                dkp = dkc = dvp = dvc = jnp.zeros((ATTN_BLOCK, HEAD_DIM), F32)
                for g in range(Q_PER_KV):
                    hh = Q_PER_KV * m + g
                    qs = slice(HEAD_DIM * hh, HEAD_DIM * (hh + 1))
                    q, dout = q_ref[:, qs], do_ref[:, qs]
                    pp, pc, ps = _attn_probs(q, kp, kc, sk_ref[0, hh], ALIBI_SLOPES[hh], masks)
                    dpp, dpc = _bdot(dout, vp, NT), _bdot(dout, vc, NT)
                    delta = jnp.sum(pp * dpp, axis=-1, keepdims=True) + jnp.sum(pc * dpc, axis=-1, keepdims=True)
                    dsp, dsc = pp * (dpp - delta), pc * (dpc - delta)
                    dsv = dsv + jnp.where(lane == hh, -jnp.sum(ps * delta, axis=0, keepdims=True), 0.0)
                    cur[:, qs] = (_bdot(dsp, kp, NN) + _bdot(dsc, kc, NN)) * ATTN_SCALE
                    dkp = dkp + _bdot(dsp, q, TN) * ATTN_SCALE
                    dkc = dkc + _bdot(dsc, q, TN) * ATTN_SCALE
                    dvp = dvp + _bdot(pp, dout, TN)
                    dvc = dvc + _bdot(pc, dout, TN)
                cur[:, K_COL + HEAD_DIM * m:K_COL + HEAD_DIM * (m + 1)] = dkc
                cur[:, V_COL + HEAD_DIM * m:V_COL + HEAD_DIM * (m + 1)] = dvc
                padd[:, ks] = dkp
                padd[:, KV_COLS + HEAD_DIM * m:KV_COLS + HEAD_DIM * (m + 1)] = dvp
            ds_ref[...] += dsv
            dz_ref[:, :K_COL] = carry[:, :K_COL].astype(dz_ref.dtype)
            dz_ref[:, K_COL:] = (carry[:, K_COL:] + padd[...]).astype(dz_ref.dtype)
            carry[...] = cur[...]

        @pl.when(n == nblk)
        def _():
            dz_ref[...] = carry[...].astype(dz_ref.dtype)

    return pl.pallas_call(
        body, name=name, grid=(nblk + 1,),
        in_specs=_attn_specs(nblk) + [
            pl.BlockSpec((ATTN_BLOCK, K_COL), lambda n: (jnp.minimum(n, nblk - 1), 0)),
            pl.BlockSpec(memory_space=pltpu.SMEM)],
        out_specs=[pl.BlockSpec((ATTN_BLOCK, ATTN_IN), lambda n: (jnp.maximum(n - 1, 0), 0)),
                   pl.BlockSpec((1, 128), lambda n: (0, 0))],
        out_shape=[_sds((S, ATTN_IN), BF16), _sds((1, 128), F32)],
        scratch_shapes=[pltpu.VMEM((ATTN_BLOCK, ATTN_IN), F32), pltpu.VMEM((ATTN_BLOCK, ATTN_IN), F32),
                        pltpu.VMEM((ATTN_BLOCK, 2 * KV_COLS), F32)],
        compiler_params=_params(("arbitrary",)),
    )(proj, proj, proj, proj, proj, do, sinks)


def _hg_consts():
    C = HG_CHUNK
    tri = np.tril(np.ones((C, C)))
    t = np.arange(C)
    rows, masks = [tri], []
    for lvl in range(HG_LEVELS):
        n = C >> (lvl + 1)
        sel = np.zeros((C, C))
        sel[t, (t // (2 * n)) * (2 * n) + n - 1] = 1.0
        rows.append(sel @ tri)
        tt, ss = t[:, None], t[None, :]
        masks.append((tt // (2 * n) == ss // (2 * n)) & ((tt // n) % 2 == 1) & ((ss // n) % 2 == 0))
    masks.append(np.eye(C, dtype=bool))
    stk = np.concatenate(rows, axis=0)
    return jnp.asarray(stk, BF16), jnp.asarray(np.stack(masks), F32)


def _sigmoid(x):
    return 1.0 / (1.0 + jnp.exp(-x))


def _split3(x):
    hi = x.astype(BF16)
    r1 = x - hi.astype(F32)
    mid = r1.astype(BF16)
    return hi, mid, (r1 - mid.astype(F32)).astype(BF16)


def _dot01(m01, x, dn):
    return sum(lax.dot_general(m01, p, dn, preferred_element_type=F32) for p in _split3(x))


def _hg_common(z_ref, lb_ref, stk_ref):
    qr, fr = z_ref[0], z_ref[1]
    lb = lb_ref[...]
    sq, sg, sgn = _sigmoid(qr), _sigmoid(fr), _sigmoid(-fr)
    ft = lb + (1.0 - lb) * sg
    lf = jnp.log(ft)
    bb = _dot01(stk_ref[...], lf, NN)
    b = bb[0:HG_CHUNK]
    diffs = [b - bb[HG_CHUNK * (l + 1):HG_CHUNK * (l + 2)] for l in range(HG_LEVELS)]
    ws = [jnp.exp(-jnp.abs(d)) for d in diffs]
    blast = b[HG_CHUNK - 1:HG_CHUNK]
    return dict(qr=qr, fr=fr, lb=lb, sq=sq, sg=sg, sgn=sgn, ft=ft, q=qr * sq, kk=(1.0 - lb) * sgn, b=b, diffs=diffs,
                ws=ws, eb=jnp.exp(b), ed=jnp.exp(blast - b), elast=jnp.exp(blast))


def _hg_intra(qh, kh, ws, msk_ref, sl):
    a = msk_ref[HG_LEVELS] * _bdot(qh, kh, NT)
    qls, kls = [], []
    for l in range(HG_LEVELS):
        w = ws[l][:, sl]
        qls.append((qh * w).astype(BF16))
        kls.append((kh * w).astype(BF16))
        a = a + msk_ref[l] * _bdot(qls[l], kls[l], NT)
    return a, qls, kls


def _hg_fwd(name, z, lb, ng):
    S = z.shape[2]
    nc = S // HG_CHUNK
    stk, msk = _hg_consts()

    def body(z_ref, lb_ref, ng_ref, stk_ref, msk_ref, og_ref, st_ref, state):
        @pl.when(pl.program_id(1) == 0)
        def _():
            state[...] = jnp.zeros_like(state)

        cm = _hg_common(z_ref, lb_ref, stk_ref)
        v, gt = z_ref[2], z_ref[3]
        kd = cm["kk"] * cm["ed"]
        for hh in range(4):
            sl = slice(HG_K * hh, HG_K * (hh + 1))
            st = state[hh]
            st_ref[hh] = st
            qh, kh, vh = cm["q"][:, sl], cm["kk"][:, sl], v[:, sl]
            a, _, _ = _hg_intra(qh, kh, cm["ws"], msk_ref, sl)
            o = _bdot(a, vh, NN) + _bdot(qh * cm["eb"][:, sl], st, NT)
            state[hh] = cm["elast"][:, sl] * st + _bdot(vh, kd[:, sl], TN)
            r = lax.rsqrt(jnp.mean(o * o, axis=-1, keepdims=True) + EPS)
            gh = gt[:, sl]
            og_ref[:, sl] = (o * r * ng_ref[...] * (gh * _sigmoid(gh))).astype(og_ref.dtype)

    return pl.pallas_call(
        body, name=name, grid=(2, nc),
        in_specs=[pl.BlockSpec((4, None, HG_CHUNK, HG_SLOT), lambda g, c: (0, g, c, 0)),
                  pl.BlockSpec((1, HG_SLOT), lambda g, c: (0, g)),
                  pl.BlockSpec((1, HG_K), lambda g, c: (0, 0)),
                  pl.BlockSpec(stk.shape, lambda g, c: (0, 0)),
                  pl.BlockSpec(msk.shape, lambda g, c: (0, 0, 0))],
        out_specs=[pl.BlockSpec((HG_CHUNK, HG_SLOT), lambda g, c: (c, g)),
                   pl.BlockSpec((None, 4, HG_K, HG_K), lambda g, c: (c, g, 0, 0))],
        out_shape=[_sds((S, D_MODEL), BF16), _sds((nc, HG_HEADS, HG_K, HG_K), F32)],
        scratch_shapes=[pltpu.VMEM((4, HG_K, HG_K), F32)],
        compiler_params=_params(("parallel", "arbitrary")),
    )(z, lb, ng, stk, msk)


def _hg_bwd(name, z, lb, ng, states, dog):
    S = z.shape[2]
    nc = S // HG_CHUNK
    stk, msk = _hg_consts()

    def body(z_ref, lb_ref, ng_ref, stk_ref, msk_ref, st_ref, dog_ref, dz_ref, dlb_ref, dng_ref, dstate):
        @pl.when(pl.program_id(1) == 0)
        def _():
            dstate[...] = jnp.zeros_like(dstate)
            dlb_ref[...] = jnp.zeros_like(dlb_ref)
            dng_ref[...] = jnp.zeros_like(dng_ref)

        cm = _hg_common(z_ref, lb_ref, stk_ref)
        v, gt = z_ref[2], z_ref[3]
        ng = ng_ref[...]
        kd = cm["kk"] * cm["ed"]
        row = lax.broadcasted_iota(jnp.int32, (HG_CHUNK, 1), 0)
        dng = jnp.zeros((1, HG_K), F32)
        dq_h, dkk_h, db_h, dv_h, dgt_h = [], [], [], [], []
        dr_h = [[] for _ in range(HG_LEVELS)]
        for hh in range(4):
            sl = slice(HG_K * hh, HG_K * (hh + 1))
            st, dst = st_ref[hh], dstate[hh]
            qh, kh, vh, ebh, edh, kdh = cm["q"][:, sl], cm["kk"][:, sl], v[:, sl], cm["eb"][:, sl], cm["ed"][:, sl], kd[:, sl]
            elh = cm["elast"][:, sl]
            a, qls, kls = _hg_intra(qh, kh, cm["ws"], msk_ref, sl)
            qe = qh * ebh
            o = _bdot(a, vh, NN) + _bdot(qe, st, NT)
            r = lax.rsqrt(jnp.mean(o * o, axis=-1, keepdims=True) + EPS)
            xh = o * r
            gh = gt[:, sl]
            sgg = _sigmoid(gh)
            dog = dog_ref[:, sl].astype(F32)
            dy = dog * (gh * sgg)
            dgt_h.append(dog * (xh * ng) * (sgg * (1.0 + gh * (1.0 - sgg))))
            dng = dng + jnp.sum(dy * xh, axis=0, keepdims=True)
            dyg = dy * ng
            do = r * (dyg - xh * jnp.mean(dyg * xh, axis=-1, keepdims=True))
            da = _bdot(do, vh, NT)
            dv_h.append(_bdot(a, do, TN) + _bdot(kdh, dst, NT))
            dkd = _bdot(vh, dst, NN)
            delast = jnp.sum(st * dst, axis=0, keepdims=True)
            dqe = _bdot(do, st, NN)
            dstate[hh] = elh * dst + _bdot(do, qe, TN)
            gk = dkd * kdh
            dblast = jnp.sum(gk, axis=0, keepdims=True) + delast * elh
            db = dqe * qe - gk + jnp.where(row == HG_CHUNK - 1, dblast, 0.0)
            dp = (msk_ref[HG_LEVELS] * da).astype(BF16)
            dq = dqe * ebh + _bdot(dp, kh, NN)
            dkk = dkd * edh + _bdot(dp, qh, TN)
            for l in range(HG_LEVELS):
                dp = (msk_ref[l] * da).astype(BF16)
                dql, dkl = _bdot(dp, kls[l], NN), _bdot(dp, qls[l], TN)
                w = cm["ws"][l][:, sl]
                dq = dq + dql * w
                dkk = dkk + dkl * w
                half = jnp.where(((row >> (HG_LEVELS - 1 - l)) & 1) == 1, 1.0, -1.0)
                dd = half * w * (dql * qh + dkl * kh)
                db = db + dd
                dr_h[l].append(-dd)
            dq_h.append(dq)
            dkk_h.append(dkk)
            db_h.append(db)
        cat = lambda xs: jnp.concatenate(xs, axis=1)
        cot = jnp.concatenate([cat(db_h)] + [cat(dr_h[l]) for l in range(HG_LEVELS)], axis=0)
        dlf = _dot01(stk_ref[...], cot, TN)
        dq, dkk = cat(dq_h), cat(dkk_h)
        dft = dlf / cm["ft"]
        one_lb = 1.0 - cm["lb"]
        dz_ref[0] = dq * (cm["sq"] * (1.0 + cm["qr"] * (1.0 - cm["sq"])))
        dz_ref[1] = (dft - dkk) * one_lb * cm["sg"] * cm["sgn"]
        dz_ref[2] = cat(dv_h)
        dz_ref[3] = cat(dgt_h)
        dlb_ref[...] += jnp.sum((dft - dkk) * cm["sgn"], axis=0, keepdims=True)
        dng_ref[...] += dng

    rev = lambda c: nc - 1 - c
    return pl.pallas_call(
        body, name=name, grid=(2, nc),
        in_specs=[pl.BlockSpec((4, None, HG_CHUNK, HG_SLOT), lambda g, c: (0, g, rev(c), 0)),
                  pl.BlockSpec((1, HG_SLOT), lambda g, c: (0, g)),
                  pl.BlockSpec((1, HG_K), lambda g, c: (0, 0)),
                  pl.BlockSpec(stk.shape, lambda g, c: (0, 0)),
                  pl.BlockSpec(msk.shape, lambda g, c: (0, 0, 0)),
                  pl.BlockSpec((None, 4, HG_K, HG_K), lambda g, c: (rev(c), g, 0, 0)),
                  pl.BlockSpec((HG_CHUNK, HG_SLOT), lambda g, c: (rev(c), g))],
        out_specs=[pl.BlockSpec((4, None, HG_CHUNK, HG_SLOT), lambda g, c: (0, g, rev(c), 0)),
                   pl.BlockSpec((1, HG_SLOT), lambda g, c: (0, g)),
                   pl.BlockSpec((None, 1, HG_K), lambda g, c: (g, 0, 0))],
        out_shape=[_sds(z.shape, F32), _sds((1, 2 * HG_SLOT), F32), _sds((2, 1, HG_K), F32)],
        scratch_shapes=[pltpu.VMEM((4, HG_K, HG_K), F32)],
        compiler_params=_params(("parallel", "arbitrary")),
    )(z, lb, ng, stk, msk, states, dog)


def _lb_fwd(name, logits):
    def body(l_ref, o_ref):
        x = l_ref[...]
        e = jnp.exp(x - jnp.max(x, axis=0, keepdims=True))
        s = e / jnp.sum(e, axis=0, keepdims=True)
        o_ref[0:1, :] = s[1:2]
        o_ref[1:2, :] = s[1:2] + s[2:3] + s[3:4]

    return pl.pallas_call(body, name=name, out_shape=_sds((2, logits.shape[1]), F32))(logits)


def _lb_bwd(name, logits, dlb):
    def body(l_ref, d_ref, o_ref):
        x = l_ref[...]
        e = jnp.exp(x - jnp.max(x, axis=0, keepdims=True))
        s = e / jnp.sum(e, axis=0, keepdims=True)
        d1, d3 = d_ref[0:1, :], d_ref[1:2, :]
        ds = [jnp.zeros_like(d1), d1 + d3, d3, d3]
        dot = sum(ds[r] * s[r:r + 1] for r in range(1, DEPTH))
        for r in range(DEPTH):
            o_ref[r:r + 1, :] = s[r:r + 1] * (ds[r] - dot)

    return pl.pallas_call(body, name=name, out_shape=_sds(logits.shape, F32))(logits, dlb)


def _conv_taps(buf, p, w, b, tm):
    return b + w[0:1] * buf[p, 6:6 + tm] + w[1:2] * buf[p, 7:7 + tm] + w[2:3] * buf[p, 8:8 + tm]


def _convgate_fwd(name, u, cw, cb):
    S = u.shape[2]
    tm = _tile(S, 512)

    def body(u_ref, w_ref, b_ref, a_ref, buf):
        @pl.when(pl.program_id(1) == 0)
        def _():
            buf[:, 0:8, :] = jnp.zeros((2, 8, FF_SLOT), F32)

        buf[:, 8:8 + tm, :] = u_ref[...]
        cg = _conv_taps(buf, 0, w_ref[0], b_ref[0], tm)
        cv = _conv_taps(buf, 1, w_ref[1], b_ref[1], tm)
        a_ref[...] = (cg * _sigmoid(cg) * cv).astype(a_ref.dtype)
        buf[:, 0:8, :] = buf[:, tm:tm + 8, :]

    return pl.pallas_call(
        body, name=name, grid=(4, S // tm),
        in_specs=[pl.BlockSpec((2, None, tm, FF_SLOT), lambda j, t: (0, j, t, 0)),
                  pl.BlockSpec((2, None, 3, FF_SLOT), lambda j, t: (0, j, 0, 0)),
                  pl.BlockSpec((2, None, 1, FF_SLOT), lambda j, t: (0, j, 0, 0))],
        out_specs=pl.BlockSpec((None, tm, FF_SLOT), lambda j, t: (j, t, 0)),
        out_shape=_sds((4, S, FF_SLOT), BF16),
        scratch_shapes=[pltpu.VMEM((2, tm + 8, FF_SLOT), F32)],
        compiler_params=_params(("parallel", "arbitrary")),
    )(u, cw, cb)


def _convgate_bwd(name, u, cw, cb, da):
    S = u.shape[2]
    tm = _tile(S, 512)
    nt = S // tm

    def body(u_ref, uh_ref, w_ref, b_ref, da_ref, du_ref, dw_ref, db_ref, buf, dbuf):
        t = pl.program_id(1)

        @pl.when(t == 0)
        def _():
            dbuf[:, tm:tm + 8, :] = jnp.zeros((2, 8, FF_SLOT), F32)
            dw_ref[...] = jnp.zeros_like(dw_ref)
            db_ref[...] = jnp.zeros_like(db_ref)

        buf[:, 0:8, :] = jnp.where(t < nt - 1, uh_ref[...], 0.0)
        buf[:, 8:8 + tm, :] = u_ref[...]
        cg = _conv_taps(buf, 0, w_ref[0], b_ref[0], tm)
        cv = _conv_taps(buf, 1, w_ref[1], b_ref[1], tm)
        sg = _sigmoid(cg)
        dav = da_ref[...].astype(F32)
        dc = [dav * cv * (sg * (1.0 + cg * (1.0 - sg))), dav * cg * sg]
        for p in range(2):
            dbuf[p, 0:tm, :] = dc[p]
            w = w_ref[p]
            du_ref[p] = w[2:3] * dbuf[p, 0:tm] + w[1:2] * dbuf[p, 1:tm + 1] + w[0:1] * dbuf[p, 2:tm + 2]
            dbuf[p, tm:tm + 8, :] = dc[p][0:8]
            for j in range(3):
                dw_ref[p, j:j + 1, :] += jnp.sum(dc[p] * buf[p, 6 + j:6 + j + tm], axis=0, keepdims=True)
            db_ref[p] += jnp.sum(dc[p], axis=0, keepdims=True)

    rev = lambda t: nt - 1 - t
    return pl.pallas_call(
        body, name=name, grid=(4, nt),
        in_specs=[pl.BlockSpec((2, None, tm, FF_SLOT), lambda j, t: (0, j, rev(t), 0)),
                  pl.BlockSpec((2, None, 8, FF_SLOT), lambda j, t: (0, j, jnp.maximum(rev(t) * (tm // 8) - 1, 0), 0)),
                  pl.BlockSpec((2, None, 3, FF_SLOT), lambda j, t: (0, j, 0, 0)),
                  pl.BlockSpec((2, None, 1, FF_SLOT), lambda j, t: (0, j, 0, 0)),
                  pl.BlockSpec((None, tm, FF_SLOT), lambda j, t: (j, rev(t), 0))],
        out_specs=[pl.BlockSpec((2, None, tm, FF_SLOT), lambda j, t: (0, j, rev(t), 0)),
                   pl.BlockSpec((2, None, 3, FF_SLOT), lambda j, t: (0, j, 0, 0)),
                   pl.BlockSpec((2, None, 1, FF_SLOT), lambda j, t: (0, j, 0, 0))],
        out_shape=[_sds(u.shape, F32), _sds(cw.shape, F32), _sds(cb.shape, F32)],
        scratch_shapes=[pltpu.VMEM((2, tm + 8, FF_SLOT), F32), pltpu.VMEM((2, tm + 8, FF_SLOT), F32)],
        compiler_params=_params(("parallel", "arbitrary")),
    )(u, u, cw, cb, da)


def _row_tile(R):
    for t in range(256, 15, -16):
        if R % t == 0:
            return t
    return R


def _sum_sources(name, gsrc):
    n, R, C = gsrc.shape
    tr = _row_tile(R)

    def body(g_ref, o_ref):
        g = g_ref[0].astype(F32)
        for s in range(1, n):
            g = g + g_ref[s].astype(F32)
        o_ref[...] = g

    return pl.pallas_call(
        body, name=name, grid=(R // tr,), in_specs=[pl.BlockSpec((n, tr, C), lambda i: (0, i, 0))],
        out_specs=pl.BlockSpec((tr, C), lambda i: (i, 0)), out_shape=_sds((R, C), F32),
        compiler_params=_params(("parallel",)),
    )(gsrc)


def _adamw(name, gsrcs, w, m, v):
    L = len(gsrcs)
    n, A, C = gsrcs[0].shape
    tr = _row_tile(A)

    def body(*refs):
        g_refs = refs[:L]
        w_ref, m_ref, v_ref, go_ref, d_ref, mo_ref, vo_ref = refs[L:]
        for k in range(L):
            @pl.when(pl.program_id(0) == k)
            def _(k=k):
                g = g_refs[k][0].astype(F32)
                for s in range(1, n):
                    g = g + g_refs[k][s].astype(F32)
                m2 = ADAM_B1 * m_ref[...] + (1.0 - ADAM_B1) * g
                v2 = ADAM_B2 * v_ref[...] + (1.0 - ADAM_B2) * (g * g)
                m_hat = m2 / (1.0 - ADAM_B1 ** ADAM_STEP)
                v_hat = v2 / (1.0 - ADAM_B2 ** ADAM_STEP)
                go_ref[...] = g
                d_ref[...] = -ADAM_LR * (m_hat / (jnp.sqrt(v_hat) + ADAM_EPS) + ADAM_WD * w_ref[...])
                mo_ref[...] = m2
                vo_ref[...] = v2

    g_specs = [pl.BlockSpec((n, tr, C), lambda l, i, k=k: (0, jnp.where(l == k, i, 0), 0)) for k in range(L)]
    blk = pl.BlockSpec((None, tr, C), lambda l, i: (l, i, 0))
    return pl.pallas_call(
        body, name=name, grid=(L, A // tr), in_specs=g_specs + [blk, blk, blk],
        out_specs=[blk] * 4, out_shape=[_sds((L, A, C), F32)] * 4, compiler_params=_params(("parallel", "parallel")),
    )(*gsrcs, w, m, v)


MESH = pl.DeviceIdType.MESH
HBM_SPEC = pl.BlockSpec(memory_space=pltpu.HBM)
N_PEERS = N_DEV - 1


def _mesh_place():
    x, y, c = lax.axis_index("x"), lax.axis_index("y"), lax.axis_index("c")
    peers = []
    for p in range(1, N_DEV):
        px = 1 - x if p & 4 else x
        py = 1 - y if p & 2 else y
        pc = 1 - c if p & 1 else c
        peers.append(((px, py, pc), 4 * px + 2 * py + pc))
    return 4 * x + 2 * y + c, peers


SEM_SPEC = pl.BlockSpec(memory_space=pltpu.SEMAPHORE)
ANY_SPEC = pl.BlockSpec(memory_space=pl.ANY)
EFFECT = pltpu.SideEffectType.DATAFLOW_SIDE_EFFECTING


def _exchange_refs(scatter, src, land, send, recv, k, p, dev, idx, me):
    return pltpu.make_async_remote_copy(src_ref=src[k].at[idx] if scatter else src[k], dst_ref=land[k].at[me],
                                        send_sem=send.at[k * N_PEERS + p], recv_sem=recv.at[k * N_PEERS + p], device_id=dev,
                                        device_id_type=MESH)


def _exchange_start(name, srcs, scatter, gate):
    n = len(srcs)
    lands = [lax.empty(s.shape if scatter else (N_DEV,) + s.shape, s.dtype) for s in srcs]

    def body(*refs):
        src, land = refs[:n], refs[n:2 * n]
        send, recv = refs[2 * n + 1], refs[2 * n + 2]
        token = refs[-1]
        me, peers = _mesh_place()
        for k in range(n):
            pltpu.sync_copy(src[k].at[me] if scatter else src[k], land[k].at[me])
            for p, (dev, idx) in enumerate(peers):
                _exchange_refs(scatter, src, land, send, recv, k, p, dev, idx, me).start()
        token[...] = jnp.zeros_like(token)

    hbm = lambda a: pltpu.HBM(a.shape, a.dtype)
    outs = pl.pallas_call(
        body, name=name,
        out_shape=(pltpu.SemaphoreType.DMA((n * N_PEERS,)), pltpu.SemaphoreType.DMA((n * N_PEERS,)),
                   *[hbm(s) for s in srcs], *[hbm(s) for s in lands], _sds(DEP_SHAPE, F32)),
        in_specs=[HBM_SPEC] * (2 * n) + [ANY_SPEC],
        out_specs=(SEM_SPEC, SEM_SPEC, *[HBM_SPEC] * (2 * n), pl.BlockSpec(memory_space=pltpu.VMEM)),
        input_output_aliases={j: 2 + j for j in range(2 * n)},
        compiler_params=pltpu.CompilerParams(has_side_effects=EFFECT),
    )(*[pltpu.with_memory_space_constraint(s, pltpu.HBM) for s in srcs],
      *[pltpu.with_memory_space_constraint(s, pltpu.HBM) for s in lands], gate)
    return outs[0], outs[1], list(outs[2:2 + n]), list(outs[2 + n:2 + 2 * n]), outs[-1]


def _exchange_wait(name, started, scatter, after):
    send, recv, srcs, lands, _ = started
    n = len(srcs)

    def body(*refs):
        src, land = refs[:n], refs[n:2 * n]
        send, recv = refs[2 * n], refs[2 * n + 1]
        me, peers = _mesh_place()
        for k in range(n):
            for p, (dev, idx) in enumerate(peers):
                cp = pltpu.make_async_remote_copy(src_ref=src[k].at[idx] if scatter else src[k], dst_ref=land[k].at[idx],
                                                  send_sem=send.at[k * N_PEERS + p], recv_sem=recv.at[k * N_PEERS + p], device_id=dev,
                                                  device_id_type=MESH)
                cp.wait_send()
                cp.wait_recv()

    hbm = lambda a: pltpu.HBM(a.shape, a.dtype)
    outs = pl.pallas_call(
        body, name=name, out_shape=(*[hbm(s) for s in srcs], *[hbm(s) for s in lands]),
        in_specs=[HBM_SPEC] * (2 * n) + [SEM_SPEC, SEM_SPEC, ANY_SPEC], out_specs=tuple([HBM_SPEC] * (2 * n)),
        input_output_aliases={j: j for j in range(2 * n)},
        compiler_params=pltpu.CompilerParams(has_side_effects=EFFECT),
    )(*srcs, *lands, send, recv, after)
    return list(outs[n:])


def _allsum_rows(name, part):
    R, C = part.shape

    def body(p_ref, o_ref, gath, send, recv):
        me, peers = _mesh_place()
        gath[me] = p_ref[...]
        sends = []
        for p, (dev, _) in enumerate(peers):
            cp = pltpu.make_async_remote_copy(src_ref=p_ref, dst_ref=gath.at[me], send_sem=send.at[p], recv_sem=recv.at[p],
                                              device_id=dev, device_id_type=MESH)
            cp.start()
            sends.append(cp)
        for p, (dev, idx) in enumerate(peers):
            pltpu.make_async_remote_copy(src_ref=p_ref, dst_ref=gath.at[idx], send_sem=send.at[p], recv_sem=recv.at[p],
                                         device_id=dev, device_id_type=MESH).wait_recv()
        for cp in sends:
            cp.wait_send()
        tot = gath[0]
        for j in range(1, N_DEV):
            tot = tot + gath[j]
        o_ref[...] = tot

    vm = pl.BlockSpec(memory_space=pltpu.VMEM)
    return pl.pallas_call(
        body, name=name, in_specs=[vm], out_specs=vm, out_shape=_sds((R, C), F32),
        scratch_shapes=[pltpu.VMEM((N_DEV, R, C), F32), pltpu.SemaphoreType.DMA((N_PEERS,)),
                        pltpu.SemaphoreType.DMA((N_PEERS,))],
        compiler_params=pltpu.CompilerParams(vmem_limit_bytes=VMEM_LIMIT),
    )(part)


def _rows(a, width=D_MODEL):
    flat = a.reshape(-1)
    return jnp.pad(flat, (0, (-flat.shape[0]) % width)).reshape(-1, width)


def _pack_rows(parts):
    blocks = []
    for p in parts:
        r = _rows(p)
        blocks.append(jnp.pad(r, ((0, (-r.shape[0]) % 8), (0, 0))))
    return jnp.concatenate(blocks, axis=0)


def _unpack_rows(rows, shapes):
    out, at = [], 0
    for s in shapes:
        size = int(np.prod(s))
        n = -(-size // D_MODEL)
        out.append(rows[at:at + n].reshape(-1)[:size].reshape(s))
        at += -(-n // 8) * 8
    return out


def kernel(x, norm_mix, norm_ffn, norm_final, attn_w_in, attn_w_out, attn_sinks, hgrn_w_in, hgrn_w_out, hgrn_norm, hgrn_lb_logits, ffn_w_up, ffn_conv_w, ffn_conv_b, ffn_w_down, loss_target, m_norm_mix, m_norm_ffn, m_norm_final, m_attn_w_in, m_attn_w_out, m_attn_sinks, m_hgrn_w_in, m_hgrn_w_out, m_hgrn_norm, m_hgrn_lb_logits, m_ffn_w_up, m_ffn_conv_w, m_ffn_conv_b, m_ffn_w_down, v_norm_mix, v_norm_ffn, v_norm_final, v_attn_w_in, v_attn_w_out, v_attn_sinks, v_hgrn_w_in, v_hgrn_w_out, v_hgrn_norm, v_hgrn_lb_logits, v_ffn_w_up, v_ffn_conv_w, v_ffn_conv_b, v_ffn_w_down):
    S = x.shape[1]
    n_attn, n_hgrn = attn_w_in.shape[0], hgrn_w_in.shape[0]
    me = 4 * lax.axis_index("x") + 2 * lax.axis_index("y") + lax.axis_index("c")

    wa_in_t, wa_out_b = attn_w_in.transpose(0, 2, 1).astype(BF16), attn_w_out.astype(BF16)
    wh_in_b, wh_out_b = hgrn_w_in.astype(BF16), hgrn_w_out.astype(BF16)
    wf_up_b, wf_down_b = ffn_w_up.astype(BF16), ffn_w_down.astype(BF16)
    conv_b = ffn_conv_b.reshape(DEPTH, 2, 4, 1, FF_SLOT)
    lb = _lb_fwd("lb_fwd", hgrn_lb_logits)

    def unit_shards(l, part):
        if part == "ffn":
            return [wf_up_b[l], wf_down_b[l], ffn_conv_w[l]]
        return [wa_in_t[l // 2], wa_out_b[l // 2]] if l % 2 == 0 else [wh_in_b[l // 2], wh_out_b[l // 2]]

    def unit_weights(l, part, w):
        if part == "ffn":
            return w[0][None], w[1].reshape(1, 4, FF_SLOT, D_MODEL), w[2].reshape(2, 4, 3, FF_SLOT)
        if l % 2 == 0:
            return w[0].reshape(1, ATTN_IN, D_MODEL), w[1].reshape(1, D_MODEL, D_MODEL)
        return w[0][None], w[1].reshape(1, D_MODEL, D_MODEL)

    units = [(l, part) for l in range(DEPTH) for part in ("mix", "ffn")]
    started = _exchange_start("gather_start0", unit_shards(*units[0]), False, norm_final)
    arrived = _exchange_wait("gather_wait0", started, False, started[4])
    weights, saved = {}, [dict() for _ in range(DEPTH)]
    h = x[0]
    for n, (l, part) in enumerate(units):
        i, sv = l // 2, saved[l]
        weights[l, part] = w = unit_weights(l, part, arrived)
        started = dep = None
        if n + 1 < len(units):
            started = _exchange_start(f"gather_start{n + 1}", unit_shards(*units[n + 1]), False, arrived[0])
            dep = started[4]
        if part == "mix":
            sv["h"] = h
            sv["hn"] = hn = _rmsnorm_fwd(f"norm_mix_fwd{l}", h, norm_mix[l:l + 1], dep)
            if l % 2 == 0:
                sv["proj"] = _proj_rows(f"attn_proj{i}", hn, w[0], 0, BF16)
                sv["o"] = _attn_fwd(f"attn_fwd{i}", sv["proj"], attn_sinks[i:i + 1])
                h = _out_proj(f"attn_out{i}", sv["o"], w[1], 0, h)
            else:
                sv["z"] = _proj_slots(f"hgrn_proj{i}", hn, w[0], 0).reshape(4, 2, S, HG_SLOT)
                sv["o"], sv["states"] = _hg_fwd(f"hgrn_fwd{i}", sv["z"], lb[i:i + 1], hgrn_norm[i:i + 1])
                h = _out_proj(f"hgrn_out{i}", sv["o"], w[1], 0, h)
        else:
            sv["h2"] = h
            sv["hn2"] = _rmsnorm_fwd(f"norm_ffn_fwd{l}", h, norm_ffn[l:l + 1], dep)
            sv["u"] = _proj_slots(f"ffn_up{l}", sv["hn2"], w[0], 0).reshape(2, 4, S, FF_SLOT)
            sv["a"] = _convgate_fwd(f"ffn_gate{l}", sv["u"], w[2], conv_b[l])
            h = _down_proj(f"ffn_down{l}", sv["a"], w[1], 0, h)
        if started is not None:
            arrived = _exchange_wait(f"gather_wait{n + 1}", started, False, h)
    dh, d_norm_final, loss_rows = _loss_head("loss_head", h, norm_final[None], loss_target[0])

    d_conv_w, d_conv_b, d_norm_mix, d_norm_ffn = [None] * DEPTH, [None] * DEPTH, [None] * DEPTH, [None] * DEPTH
    d_sinks, d_lb, d_hgrn_norm = [None] * n_attn, [None] * n_hgrn, [None] * n_hgrn
    received, started, before = {}, None, None
    for l, part in reversed(units):
        i, sv, w = l // 2, saved[l], weights[l, part]
        dep = None if started is None else started[4]
        if part == "ffn":
            da = _dgrad_down(f"ffn_down_dgrad{l}", dh, w[1], 0, dep)
            g_down = _wgrad_down(f"ffn_down_wgrad{l}", sv["a"], dh).reshape(N_DEV, D_FF // N_DEV, D_MODEL)
            du, d_conv_w[l], d_conv_b[l] = _convgate_bwd(f"ffn_gate_bwd{l}", sv["u"], w[2], conv_b[l], da)
            du = du.reshape(N_DEV, S, FF_SLOT)
            dhn2 = _dgrad_slots(f"ffn_up_dgrad{l}", du, w[0], 0)
            grads = [_wgrad_slots(f"ffn_up_wgrad{l}", sv["hn2"], du), g_down]
            dh, d_norm_ffn[l] = _rmsnorm_bwd(f"norm_ffn_bwd{l}", sv["h2"], norm_ffn[l:l + 1], dhn2, dh)
        else:
            if l % 2 == 0:
                do = _dgrad_out(f"attn_out_dgrad{i}", dh, w[1], 0, BF16, dep)
                g_out = _wgrad_rows(f"attn_out_wgrad{i}", sv["o"], dh)
                dproj, d_sinks[i] = _attn_bwd(f"attn_bwd{i}", sv["proj"], attn_sinks[i:i + 1], do)
                dhn = _dgrad_rows(f"attn_proj_dgrad{i}", dproj, w[0], 0)
                g_in = _wgrad_rows(f"attn_proj_wgrad{i}", dproj, sv["hn"]).reshape(N_DEV, ATTN_IN // N_DEV, D_MODEL)
            else:
                dog = _dgrad_out(f"hgrn_out_dgrad{i}", dh, w[1], 0, F32, dep)
                g_out = _wgrad_rows(f"hgrn_out_wgrad{i}", sv["o"], dh)
                dz, d_lb[i], dng = _hg_bwd(f"hgrn_bwd{i}", sv["z"], lb[i:i + 1], hgrn_norm[i:i + 1], sv["states"], dog)
                d_hgrn_norm[i] = dng[0] + dng[1]
                dz = dz.reshape(N_DEV, S, HG_SLOT)
                dhn = _dgrad_slots(f"hgrn_proj_dgrad{i}", dz, w[0], 0)
                g_in = _wgrad_slots(f"hgrn_proj_wgrad{i}", sv["hn"], dz)
            grads = [g_in, g_out.reshape(N_DEV, D_MODEL // N_DEV, D_MODEL)]
            dh, d_norm_mix[l] = _rmsnorm_bwd(f"norm_mix_bwd{l}", sv["h"], norm_mix[l:l + 1], dhn, dh)
        gate = dh
        if started is not None:
            received[before] = _exchange_wait(f"scatter_wait_{before[1]}{before[0]}", started, True, dh)
            gate = received[before][0]
        started, before = _exchange_start(f"scatter_start_{part}{l}", grads, True, gate), (l, part)
    received[before] = _exchange_wait(f"scatter_wait_{before[1]}{before[0]}", started, True, started[4])
    grad_x = dh[None]

    small_shapes = [(DEPTH, D_MODEL), (DEPTH, D_MODEL), (1, D_MODEL), (1, D_MODEL), (n_hgrn, D_MODEL), (n_attn, 128),
                    (n_hgrn, HG_K), (DEPTH, 2 * D_FF), (DEPTH, N_DEV, 3, FF_SLOT)]
    total = _allsum_rows("allsum_small", _pack_rows([
        jnp.concatenate(d_norm_mix), jnp.concatenate(d_norm_ffn), d_norm_final, loss_rows, jnp.concatenate(d_lb),
        jnp.concatenate(d_sinks), jnp.concatenate(d_hgrn_norm), jnp.stack(d_conv_b), jnp.stack(d_conv_w)]))
    (g_norm_mix, g_norm_ffn, g_norm_final, loss_sum, g_lb, g_sinks, g_hgrn_norm, g_conv_b, g_conv_w_all) = _unpack_rows(
        total, small_shapes)
    loss = jnp.sum(loss_sum)
    g_norm_final = g_norm_final[0]
    g_sinks = g_sinks[:, :N_Q_HEADS]
    g_lb_logits = _lb_bwd("lb_bwd", hgrn_lb_logits, g_lb)
    g_conv_w = lax.dynamic_index_in_dim(g_conv_w_all, me, axis=1, keepdims=False)

    attn_layers, hgrn_layers = range(0, DEPTH, 2), range(1, DEPTH, 2)
    g_attn_in_t = [_sum_sources(f"sum_attn_in{l // 2}", received[l, "mix"][0]).T[None] for l in attn_layers]
    big = {
        "attn_w_in": _adamw("adamw_attn_in", g_attn_in_t, attn_w_in, m_attn_w_in, v_attn_w_in),
        "attn_w_out": _adamw("adamw_attn_out", [received[l, "mix"][1] for l in attn_layers], attn_w_out, m_attn_w_out, v_attn_w_out),
        "hgrn_w_in": _adamw("adamw_hgrn_in", [received[l, "mix"][0] for l in hgrn_layers], hgrn_w_in, m_hgrn_w_in, v_hgrn_w_in),
        "hgrn_w_out": _adamw("adamw_hgrn_out", [received[l, "mix"][1] for l in hgrn_layers], hgrn_w_out, m_hgrn_w_out, v_hgrn_w_out),
        "ffn_w_up": _adamw("adamw_ffn_up", [received[l, "ffn"][0] for l in range(DEPTH)], ffn_w_up, m_ffn_w_up, v_ffn_w_up),
        "ffn_w_down": _adamw("adamw_ffn_down", [received[l, "ffn"][1] for l in range(DEPTH)], ffn_w_down, m_ffn_w_down, v_ffn_w_down),
        "ffn_conv_w": _adamw("adamw_conv_w", [g_conv_w[l][None] for l in range(DEPTH)], ffn_conv_w, m_ffn_conv_w, v_ffn_conv_w),
    }
    small_w = [norm_mix, norm_ffn, norm_final, attn_sinks, hgrn_norm, hgrn_lb_logits, ffn_conv_b]
    small_m = [m_norm_mix, m_norm_ffn, m_norm_final, m_attn_sinks, m_hgrn_norm, m_hgrn_lb_logits, m_ffn_conv_b]
    small_v = [v_norm_mix, v_norm_ffn, v_norm_final, v_attn_sinks, v_hgrn_norm, v_hgrn_lb_logits, v_ffn_conv_b]
    small_g = [g_norm_mix, g_norm_ffn, g_norm_final, g_sinks, g_hgrn_norm, g_lb_logits, g_conv_b]
    outs = _adamw("adamw_small", [_pack_rows(small_g)[None]], *[_pack_rows(t)[None] for t in (small_w, small_m, small_v)])
    outs = [o[0] for o in outs]
    shapes = [w.shape for w in small_w]
    small = {n: [t[j] for t in [_unpack_rows(o, shapes) for o in outs]]
             for j, n in enumerate(["norm_mix", "norm_ffn", "norm_final", "attn_sinks", "hgrn_norm", "hgrn_lb_logits", "ffn_conv_b"])}
    order = ["norm_mix", "norm_ffn", "norm_final", "attn_w_in", "attn_w_out", "attn_sinks", "hgrn_w_in", "hgrn_w_out",
             "hgrn_norm", "hgrn_lb_logits", "ffn_w_up", "ffn_conv_w", "ffn_conv_b", "ffn_w_down"]
    res = {**big, **small}
    return (loss, grad_x, *[res[n][0] for n in order], *[res[n][1] for n in order], *[res[n][2] for n in order],
            *[res[n][3] for n in order])
```

```python
import numpy as np
import jax
import jax.numpy as jnp
from jax import lax
from jax.experimental import pallas as pl
from jax.experimental.pallas import tpu as pltpu

F32 = jnp.float32
BF16 = jnp.bfloat16

D_MODEL = 1024
DEPTH = 4
HEAD_DIM = 64
N_Q_HEADS = 16
N_KV_HEADS = 4
Q_PER_KV = 4
ATTN_BLOCK = 128
ATTN_IN = 1536
HG_HEADS = 8
HG_K = 128
HG_CHUNK = 64
HG_IN = 4096
D_FF = 2816
EPS = 1e-6
N_DEV = 8
FF_SLOT = 2 * D_FF // N_DEV
HG_SLOT = HG_IN // N_DEV
HG_LEVELS = 6

ADAM_LR = 0.001
ADAM_B1 = 0.9
ADAM_B2 = 0.999
ADAM_EPS = 1e-08
ADAM_WD = 0.01
ADAM_STEP = 10

VMEM_LIMIT = 56 * 1024 * 1024
NEG_BIG = -1e30

NN = (((1,), (0,)), ((), ()))
NT = (((1,), (1,)), ((), ()))
TN = (((0,), (0,)), ((), ()))


def _bdot(a, b, dn):
    return lax.dot_general(a.astype(BF16), b.astype(BF16), dn, preferred_element_type=F32)


def _sds(shape, dtype):
    return jax.ShapeDtypeStruct(tuple(shape), dtype)


def _params(sem):
    return pltpu.CompilerParams(dimension_semantics=sem, vmem_limit_bytes=VMEM_LIMIT)


DEP_SHAPE = (8, 128)


def _dep_spec(rank):
    return pl.BlockSpec(DEP_SHAPE, lambda *_: (0, 0))


def _matmul(name, a, b, *, dn, grid, a_spec, b_spec, o_spec, out_shape, acc_shape=None, res=None, res_spec=None, dep=None):
    nk = grid[2]
    n_in = 2 + (res is not None) + (dep is not None)

    def body(*refs):
        a_ref, b_ref = refs[0], refs[1]
        r_ref = refs[2] if res is not None else None
        o_ref = refs[n_in]

        def prod():
            return _bdot(a_ref[...], b_ref[...], dn)

        def finish(v):
            if r_ref is not None:
                v = v + r_ref[...]
            o_ref[...] = v.astype(o_ref.dtype)

        if nk == 1:
            finish(prod())
        else:
            acc = refs[-1]
            k = pl.program_id(2)

            @pl.when(k == 0)
            def _():
                acc[...] = prod()

            @pl.when(k > 0)
            def _():
                acc[...] += prod()

            @pl.when(k == nk - 1)
            def _():
                finish(acc[...])

    in_specs = [a_spec, b_spec] + ([res_spec] if res is not None else []) + ([_dep_spec(3)] if dep is not None else [])
    args = (a, b) + ((res,) if res is not None else ()) + ((dep,) if dep is not None else ())
    scratch = [] if nk == 1 else [pltpu.VMEM(acc_shape, F32)]
    return pl.pallas_call(
        body, name=name, grid=grid, in_specs=in_specs, out_specs=o_spec, out_shape=out_shape,
        scratch_shapes=scratch, compiler_params=_params(("parallel", "parallel", "arbitrary")),
    )(*args)


def _tile(n, t):
    return min(n, t)


def _proj_rows(name, hn, wt, l, out_dtype):
    S, N = hn.shape[0], wt.shape[1]
    tm, tn = _tile(S, 512), 512
    return _matmul(
        name, hn, wt, dn=NT, grid=(S // tm, N // tn, 1),
        a_spec=pl.BlockSpec((tm, D_MODEL), lambda i, j, k: (i, 0)),
        b_spec=pl.BlockSpec((None, tn, D_MODEL), lambda i, j, k: (l, j, 0)),
        o_spec=pl.BlockSpec((tm, tn), lambda i, j, k: (i, j)),
        out_shape=_sds((S, N), out_dtype))


def _proj_slots(name, hn, w, l):
    S, r = hn.shape[0], w.shape[3]
    tm = _tile(S, 512)
    return _matmul(
        name, hn, w, dn=NN, grid=(N_DEV, S // tm, 1),
        a_spec=pl.BlockSpec((tm, D_MODEL), lambda j, i, k: (i, 0)),
        b_spec=pl.BlockSpec((None, None, D_MODEL, r), lambda j, i, k: (l, j, 0, 0)),
        o_spec=pl.BlockSpec((None, tm, r), lambda j, i, k: (j, i, 0)),
        out_shape=_sds((N_DEV, S, r), F32))


def _out_proj(name, o, w, l, h):
    S, K = o.shape
    tm = _tile(S, 512)
    return _matmul(
        name, o, w, dn=NN, grid=(S // tm, 1, 1),
        a_spec=pl.BlockSpec((tm, K), lambda i, j, k: (i, 0)),
        b_spec=pl.BlockSpec((None, K, D_MODEL), lambda i, j, k: (l, 0, 0)),
        o_spec=pl.BlockSpec((tm, D_MODEL), lambda i, j, k: (i, 0)),
        out_shape=_sds((S, D_MODEL), F32), res=h,
        res_spec=pl.BlockSpec((tm, D_MODEL), lambda i, j, k: (i, 0)))


def _down_proj(name, a, w, l, h):
    nj, S, r = a.shape
    tm = _tile(S, 512)
    return _matmul(
        name, a, w, dn=NN, grid=(S // tm, 1, nj),
        a_spec=pl.BlockSpec((None, tm, r), lambda i, j, k: (k, i, 0)),
        b_spec=pl.BlockSpec((None, None, r, D_MODEL), lambda i, j, k: (l, k, 0, 0)),
        o_spec=pl.BlockSpec((tm, D_MODEL), lambda i, j, k: (i, 0)),
        out_shape=_sds((S, D_MODEL), F32), acc_shape=(tm, D_MODEL), res=h,
        res_spec=pl.BlockSpec((tm, D_MODEL), lambda i, j, k: (i, 0)))


def _dgrad_down(name, dh, w, l, dep=None):
    S = dh.shape[0]
    nj, r = w.shape[1], w.shape[2]
    tm = _tile(S, 512)
    return _matmul(
        name, dh, w, dn=NT, grid=(nj, S // tm, 1),
        a_spec=pl.BlockSpec((tm, D_MODEL), lambda j, i, k: (i, 0)),
        b_spec=pl.BlockSpec((None, None, r, D_MODEL), lambda j, i, k: (l, j, 0, 0)),
        o_spec=pl.BlockSpec((None, tm, r), lambda j, i, k: (j, i, 0)),
        out_shape=_sds((nj, S, r), BF16), dep=dep)


def _wgrad_down(name, a, dh):
    nj, S, r = a.shape
    tk = _tile(S, 512)
    return _matmul(
        name, a, dh, dn=TN, grid=(nj, 1, S // tk),
        a_spec=pl.BlockSpec((None, tk, r), lambda s, j, k: (s, k, 0)),
        b_spec=pl.BlockSpec((tk, D_MODEL), lambda s, j, k: (k, 0)),
        o_spec=pl.BlockSpec((None, r, D_MODEL), lambda s, j, k: (s, 0, 0)),
        out_shape=_sds((nj, r, D_MODEL), BF16), acc_shape=(r, D_MODEL))


def _dgrad_slots(name, dz, w, l):
    nj, S, r = dz.shape
    tm = _tile(S, 512)
    return _matmul(
        name, dz, w, dn=NT, grid=(S // tm, 1, nj),
        a_spec=pl.BlockSpec((None, tm, r), lambda i, j, k: (k, i, 0)),
        b_spec=pl.BlockSpec((None, None, D_MODEL, r), lambda i, j, k: (l, k, 0, 0)),
        o_spec=pl.BlockSpec((tm, D_MODEL), lambda i, j, k: (i, 0)),
        out_shape=_sds((S, D_MODEL), F32), acc_shape=(tm, D_MODEL))


def _wgrad_slots(name, hn, dz):
    nj, S, r = dz.shape
    tk = _tile(S, 512)
    return _matmul(
        name, hn, dz, dn=TN, grid=(nj, 1, S // tk),
        a_spec=pl.BlockSpec((tk, D_MODEL), lambda s, j, k: (k, 0)),
        b_spec=pl.BlockSpec((None, tk, r), lambda s, j, k: (s, k, 0)),
        o_spec=pl.BlockSpec((None, D_MODEL, r), lambda s, j, k: (s, 0, 0)),
        out_shape=_sds((nj, D_MODEL, r), BF16), acc_shape=(D_MODEL, r))


def _dgrad_out(name, dh, w, l, out_dtype, dep=None):
    S, K = dh.shape[0], w.shape[1]
    tm = _tile(S, 512)
    return _matmul(
        name, dh, w, dn=NT, grid=(S // tm, 1, 1),
        a_spec=pl.BlockSpec((tm, D_MODEL), lambda i, j, k: (i, 0)),
        b_spec=pl.BlockSpec((None, K, D_MODEL), lambda i, j, k: (l, 0, 0)),
        o_spec=pl.BlockSpec((tm, K), lambda i, j, k: (i, 0)),
        out_shape=_sds((S, K), out_dtype), dep=dep)


def _wgrad_rows(name, a, b):
    S, K = a.shape
    tk = _tile(S, 512)
    return _matmul(
        name, a, b, dn=TN, grid=(1, 1, S // tk),
        a_spec=pl.BlockSpec((tk, K), lambda i, j, k: (k, 0)),
        b_spec=pl.BlockSpec((tk, D_MODEL), lambda i, j, k: (k, 0)),
        o_spec=pl.BlockSpec((K, D_MODEL), lambda i, j, k: (0, 0)),
        out_shape=_sds((K, D_MODEL), BF16), acc_shape=(K, D_MODEL))


def _dgrad_rows(name, dz, wt, l):
    S, N = dz.shape
    tm = _tile(S, 512)
    return _matmul(
        name, dz, wt, dn=NN, grid=(S // tm, 1, 1),
        a_spec=pl.BlockSpec((tm, N), lambda i, j, k: (i, 0)),
        b_spec=pl.BlockSpec((None, N, D_MODEL), lambda i, j, k: (l, 0, 0)),
        o_spec=pl.BlockSpec((tm, D_MODEL), lambda i, j, k: (i, 0)),
        out_shape=_sds((S, D_MODEL), F32))


def _rmsnorm_fwd(name, h, g, dep=None):
    S = h.shape[0]
    tm = _tile(S, 512)

    def body(h_ref, g_ref, *rest):
        o_ref = rest[-1]
        x = h_ref[...]
        r = lax.rsqrt(jnp.mean(x * x, axis=-1, keepdims=True) + EPS)
        o_ref[...] = (x * r * g_ref[...]).astype(o_ref.dtype)

    row = pl.BlockSpec((tm, D_MODEL), lambda i: (i, 0))
    deps = () if dep is None else (dep,)
    return pl.pallas_call(
        body, name=name, grid=(S // tm,),
        in_specs=[row, pl.BlockSpec((1, D_MODEL), lambda i: (0, 0))] + [_dep_spec(1)] * len(deps),
        out_specs=row, out_shape=_sds((S, D_MODEL), BF16), compiler_params=_params(("parallel",)),
    )(h, g, *deps)


def _rmsnorm_bwd(name, h, g, dy, dres):
    S = h.shape[0]
    tm = _tile(S, 512)

    def body(h_ref, g_ref, dy_ref, dres_ref, dh_ref, dg_ref):
        x = h_ref[...]
        r = lax.rsqrt(jnp.mean(x * x, axis=-1, keepdims=True) + EPS)
        xh = x * r
        dyf = dy_ref[...].astype(F32)
        dyg = dyf * g_ref[...]
        dh_ref[...] = dres_ref[...] + r * (dyg - xh * jnp.mean(dyg * xh, axis=-1, keepdims=True))
        part = jnp.sum(dyf * xh, axis=0, keepdims=True)

        @pl.when(pl.program_id(0) == 0)
        def _():
            dg_ref[...] = part

        @pl.when(pl.program_id(0) > 0)
        def _():
            dg_ref[...] += part

    row = pl.BlockSpec((tm, D_MODEL), lambda i: (i, 0))
    vec = pl.BlockSpec((1, D_MODEL), lambda i: (0, 0))
    return pl.pallas_call(
        body, name=name, grid=(S // tm,), in_specs=[row, vec, row, row], out_specs=[row, vec],
        out_shape=[_sds((S, D_MODEL), F32), _sds((1, D_MODEL), F32)], compiler_params=_params(("arbitrary",)),
    )(h, g, dy, dres)


def _loss_head(name, h, g, target):
    S = h.shape[0]
    tm = _tile(S, 512)

    def body(h_ref, g_ref, t_ref, dh_ref, dg_ref, ls_ref):
        x = h_ref[...]
        r = lax.rsqrt(jnp.mean(x * x, axis=-1, keepdims=True) + EPS)
        xh = x * r
        diff = xh * g_ref[...] - t_ref[...]
        dyf = diff * (1.0 / D_MODEL)
        dyg = dyf * g_ref[...]
        dh_ref[...] = r * (dyg - xh * jnp.mean(dyg * xh, axis=-1, keepdims=True))
        part = jnp.sum(dyf * xh, axis=0, keepdims=True)
        lpart = jnp.sum(diff * diff, axis=0, keepdims=True) * (0.5 / D_MODEL)

        @pl.when(pl.program_id(0) == 0)
        def _():
            dg_ref[...] = part
            ls_ref[...] = lpart

        @pl.when(pl.program_id(0) > 0)
        def _():
            dg_ref[...] += part
            ls_ref[...] += lpart

    row = pl.BlockSpec((tm, D_MODEL), lambda i: (i, 0))
    vec = pl.BlockSpec((1, D_MODEL), lambda i: (0, 0))
    return pl.pallas_call(
        body, name=name, grid=(S // tm,), in_specs=[row, vec, row], out_specs=[row, vec, vec],
        out_shape=[_sds((S, D_MODEL), F32), _sds((1, D_MODEL), F32), _sds((1, D_MODEL), F32)],
        compiler_params=_params(("arbitrary",)),
    )(h, g, target)


ATTN_SCALE = HEAD_DIM ** -0.5
ALIBI_SLOPES = [2.0 ** (-8.0 * (h + 1) / N_Q_HEADS) for h in range(N_Q_HEADS)]
K_COL = N_Q_HEADS * HEAD_DIM
KV_COLS = N_KV_HEADS * HEAD_DIM
V_COL = K_COL + KV_COLS


def _attn_masks(n):
    qi = lax.broadcasted_iota(jnp.int32, (ATTN_BLOCK, ATTN_BLOCK), 0)
    ki = lax.broadcasted_iota(jnp.int32, (ATTN_BLOCK, ATTN_BLOCK), 1)
    dist_c = (qi - ki).astype(F32)
    return dist_c + float(ATTN_BLOCK), dist_c, (ki > qi) & (n > 0), qi >= ki


def _attn_probs(q, kp, kc, sink, slope, masks):
    dist_p, dist_c, valid_p, valid_c = masks
    sp = jnp.where(valid_p, _bdot(q, kp, NT) * ATTN_SCALE - slope * dist_p, NEG_BIG)
    sc = jnp.where(valid_c, _bdot(q, kc, NT) * ATTN_SCALE - slope * dist_c, NEG_BIG)
    m = jnp.maximum(jnp.maximum(jnp.max(sp, axis=-1, keepdims=True), jnp.max(sc, axis=-1, keepdims=True)), sink)
    ep, ec, es = jnp.exp(sp - m), jnp.exp(sc - m), jnp.exp(sink - m)
    inv = 1.0 / (jnp.sum(ep, axis=-1, keepdims=True) + jnp.sum(ec, axis=-1, keepdims=True) + es)
    return ep * inv, ec * inv, es * inv


def _attn_specs(nblk):
    last = nblk - 1
    kcol, vcol = K_COL // KV_COLS, V_COL // KV_COLS
    return [
        pl.BlockSpec((ATTN_BLOCK, K_COL), lambda n: (jnp.minimum(n, last), 0)),
        pl.BlockSpec((ATTN_BLOCK, KV_COLS), lambda n: (jnp.minimum(n, last), kcol)),
        pl.BlockSpec((ATTN_BLOCK, KV_COLS), lambda n: (jnp.maximum(jnp.minimum(n, last) - 1, 0), kcol)),
        pl.BlockSpec((ATTN_BLOCK, KV_COLS), lambda n: (jnp.minimum(n, last), vcol)),
        pl.BlockSpec((ATTN_BLOCK, KV_COLS), lambda n: (jnp.maximum(jnp.minimum(n, last) - 1, 0), vcol)),
    ]


def _attn_fwd(name, proj, sinks):
    S = proj.shape[0]
    nblk = S // ATTN_BLOCK

    def body(q_ref, kc_ref, kp_ref, vc_ref, vp_ref, sk_ref, o_ref):
        masks = _attn_masks(pl.program_id(0))
        for m in range(N_KV_HEADS):
            ks = slice(HEAD_DIM * m, HEAD_DIM * (m + 1))
            kp, kc, vp, vc = kp_ref[:, ks], kc_ref[:, ks], vp_ref[:, ks], vc_ref[:, ks]
            for g in range(Q_PER_KV):
                hh = Q_PER_KV * m + g
                qs = slice(HEAD_DIM * hh, HEAD_DIM * (hh + 1))
                pp, pc, _ = _attn_probs(q_ref[:, qs], kp, kc, sk_ref[0, hh], ALIBI_SLOPES[hh], masks)
                o_ref[:, qs] = (_bdot(pp, vp, NN) + _bdot(pc, vc, NN)).astype(o_ref.dtype)

    return pl.pallas_call(
        body, name=name, grid=(nblk,),
        in_specs=_attn_specs(nblk) + [pl.BlockSpec(memory_space=pltpu.SMEM)],
        out_specs=pl.BlockSpec((ATTN_BLOCK, K_COL), lambda n: (n, 0)),
        out_shape=_sds((S, K_COL), BF16), compiler_params=_params(("parallel",)),
    )(proj, proj, proj, proj, proj, sinks)


def _attn_bwd(name, proj, sinks, do):
    S = proj.shape[0]
    nblk = S // ATTN_BLOCK

    def body(q_ref, kc_ref, kp_ref, vc_ref, vp_ref, do_ref, sk_ref, dz_ref, ds_ref, carry, cur, padd):
        n = pl.program_id(0)

        @pl.when(n == 0)
        def _():
            carry[...] = jnp.zeros_like(carry)
            ds_ref[...] = jnp.zeros_like(ds_ref)

        @pl.when(n < nblk)
        def _():
            masks = _attn_masks(n)
            lane = lax.broadcasted_iota(jnp.int32, (1, 128), 1)
            dsv = jnp.zeros((1, 128), F32)
            for m in range(N_KV_HEADS):
                ks = slice(HEAD_DIM * m, HEAD_DIM * (m + 1))
                kp, kc, vp, vc = kp_ref[:, ks], kc_ref[:, ks], vp_ref[:, ks], vc_ref[:, ks]
                dkp = dkc = dvp = dvc = jnp.zeros((ATTN_BLOCK, HEAD_DIM), F32)
                for g in range(Q_PER_KV):
                    hh = Q_PER_KV * m + g
                    qs = slice(HEAD_DIM * hh, HEAD_DIM * (hh + 1))
                    q, dout = q_ref[:, qs], do_ref[:, qs]
                    pp, pc, ps = _attn_probs(q, kp, kc, sk_ref[0, hh], ALIBI_SLOPES[hh], masks)
                    dpp, dpc = _bdot(dout, vp, NT), _bdot(dout, vc, NT)
                    delta = jnp.sum(pp * dpp, axis=-1, keepdims=True) + jnp.sum(pc * dpc, axis=-1, keepdims=True)
                    dsp, dsc = pp * (dpp - delta), pc * (dpc - delta)
                    dsv = dsv + jnp.where(lane == hh, -jnp.sum(ps * delta, axis=0, keepdims=True), 0.0)
                    cur[:, qs] = (_bdot(dsp, kp, NN) + _bdot(dsc, kc, NN)) * ATTN_SCALE
                    dkp = dkp + _bdot(dsp, q, TN) * ATTN_SCALE
                    dkc = dkc + _bdot(dsc, q, TN) * ATTN_SCALE
                    dvp = dvp + _bdot(pp, dout, TN)
                    dvc = dvc + _bdot(pc, dout, TN)
                cur[:, K_COL + HEAD_DIM * m:K_COL + HEAD_DIM * (m + 1)] = dkc
                cur[:, V_COL + HEAD_DIM * m:V_COL + HEAD_DIM * (m + 1)] = dvc
                padd[:, ks] = dkp
                padd[:, KV_COLS + HEAD_DIM * m:KV_COLS + HEAD_DIM * (m + 1)] = dvp
            ds_ref[...] += dsv
            dz_ref[:, :K_COL] = carry[:, :K_COL].astype(dz_ref.dtype)
            dz_ref[:, K_COL:] = (carry[:, K_COL:] + padd[...]).astype(dz_ref.dtype)
            carry[...] = cur[...]

        @pl.when(n == nblk)
        def _():
            dz_ref[...] = carry[...].astype(dz_ref.dtype)

    return pl.pallas_call(
        body, name=name, grid=(nblk + 1,),
        in_specs=_attn_specs(nblk) + [
            pl.BlockSpec((ATTN_BLOCK, K_COL), lambda n: (jnp.minimum(n, nblk - 1), 0)),
            pl.BlockSpec(memory_space=pltpu.SMEM)],
        out_specs=[pl.BlockSpec((ATTN_BLOCK, ATTN_IN), lambda n: (jnp.maximum(n - 1, 0), 0)),
                   pl.BlockSpec((1, 128), lambda n: (0, 0))],
        out_shape=[_sds((S, ATTN_IN), BF16), _sds((1, 128), F32)],
        scratch_shapes=[pltpu.VMEM((ATTN_BLOCK, ATTN_IN), F32), pltpu.VMEM((ATTN_BLOCK, ATTN_IN), F32),
                        pltpu.VMEM((ATTN_BLOCK, 2 * KV_COLS), F32)],
        compiler_params=_params(("arbitrary",)),
    )(proj, proj, proj, proj, proj, do, sinks)


def _hg_consts():
    C = HG_CHUNK
    tri = np.tril(np.ones((C, C)))
    t = np.arange(C)
    rows, masks = [tri], []
    for lvl in range(HG_LEVELS):
        n = C >> (lvl + 1)
        sel = np.zeros((C, C))
        sel[t, (t // (2 * n)) * (2 * n) + n - 1] = 1.0
        rows.append(sel @ tri)
        tt, ss = t[:, None], t[None, :]
        masks.append((tt // (2 * n) == ss // (2 * n)) & ((tt // n) % 2 == 1) & ((ss // n) % 2 == 0))
    masks.append(np.eye(C, dtype=bool))
    stk = np.concatenate(rows, axis=0)
    return jnp.asarray(stk, BF16), jnp.asarray(np.stack(masks), F32)


def _sigmoid(x):
    return 1.0 / (1.0 + jnp.exp(-x))


def _split3(x):
    hi = x.astype(BF16)
    r1 = x - hi.astype(F32)
    mid = r1.astype(BF16)
    return hi, mid, (r1 - mid.astype(F32)).astype(BF16)


def _dot01(m01, x, dn):
    return sum(lax.dot_general(m01, p, dn, preferred_element_type=F32) for p in _split3(x))


def _hg_common(z_ref, lb_ref, stk_ref):
    qr, fr = z_ref[0], z_ref[1]
    lb = lb_ref[...]
    sq, sg, sgn = _sigmoid(qr), _sigmoid(fr), _sigmoid(-fr)
    ft = lb + (1.0 - lb) * sg
    lf = jnp.log(ft)
    bb = _dot01(stk_ref[...], lf, NN)
    b = bb[0:HG_CHUNK]
    diffs = [b - bb[HG_CHUNK * (l + 1):HG_CHUNK * (l + 2)] for l in range(HG_LEVELS)]
    ws = [jnp.exp(-jnp.abs(d)) for d in diffs]
    blast = b[HG_CHUNK - 1:HG_CHUNK]
    return dict(qr=qr, fr=fr, lb=lb, sq=sq, sg=sg, sgn=sgn, ft=ft, q=qr * sq, kk=(1.0 - lb) * sgn, b=b, diffs=diffs,
                ws=ws, eb=jnp.exp(b), ed=jnp.exp(blast - b), elast=jnp.exp(blast))


def _hg_intra(qh, kh, ws, msk_ref, sl):
    a = msk_ref[HG_LEVELS] * _bdot(qh, kh, NT)
    qls, kls = [], []
    for l in range(HG_LEVELS):
        w = ws[l][:, sl]
        qls.append((qh * w).astype(BF16))
        kls.append((kh * w).astype(BF16))
        a = a + msk_ref[l] * _bdot(qls[l], kls[l], NT)
    return a, qls, kls


def _hg_fwd(name, z, lb, ng):
    S = z.shape[2]
    nc = S // HG_CHUNK
    stk, msk = _hg_consts()

    def body(z_ref, lb_ref, ng_ref, stk_ref, msk_ref, og_ref, st_ref, state):
        @pl.when(pl.program_id(1) == 0)
        def _():
            state[...] = jnp.zeros_like(state)

        cm = _hg_common(z_ref, lb_ref, stk_ref)
        v, gt = z_ref[2], z_ref[3]
        kd = cm["kk"] * cm["ed"]
        for hh in range(4):
            sl = slice(HG_K * hh, HG_K * (hh + 1))
            st = state[hh]
            st_ref[hh] = st
            qh, kh, vh = cm["q"][:, sl], cm["kk"][:, sl], v[:, sl]
            a, _, _ = _hg_intra(qh, kh, cm["ws"], msk_ref, sl)
            o = _bdot(a, vh, NN) + _bdot(qh * cm["eb"][:, sl], st, NT)
            state[hh] = cm["elast"][:, sl] * st + _bdot(vh, kd[:, sl], TN)
            r = lax.rsqrt(jnp.mean(o * o, axis=-1, keepdims=True) + EPS)
            gh = gt[:, sl]
            og_ref[:, sl] = (o * r * ng_ref[...] * (gh * _sigmoid(gh))).astype(og_ref.dtype)

    return pl.pallas_call(
        body, name=name, grid=(2, nc),
        in_specs=[pl.BlockSpec((4, None, HG_CHUNK, HG_SLOT), lambda g, c: (0, g, c, 0)),
                  pl.BlockSpec((1, HG_SLOT), lambda g, c: (0, g)),
                  pl.BlockSpec((1, HG_K), lambda g, c: (0, 0)),
                  pl.BlockSpec(stk.shape, lambda g, c: (0, 0)),
                  pl.BlockSpec(msk.shape, lambda g, c: (0, 0, 0))],
        out_specs=[pl.BlockSpec((HG_CHUNK, HG_SLOT), lambda g, c: (c, g)),
                   pl.BlockSpec((None, 4, HG_K, HG_K), lambda g, c: (c, g, 0, 0))],
        out_shape=[_sds((S, D_MODEL), BF16), _sds((nc, HG_HEADS, HG_K, HG_K), F32)],
        scratch_shapes=[pltpu.VMEM((4, HG_K, HG_K), F32)],
        compiler_params=_params(("parallel", "arbitrary")),
    )(z, lb, ng, stk, msk)


def _hg_bwd(name, z, lb, ng, states, dog):
    S = z.shape[2]
    nc = S // HG_CHUNK
    stk, msk = _hg_consts()

    def body(z_ref, lb_ref, ng_ref, stk_ref, msk_ref, st_ref, dog_ref, dz_ref, dlb_ref, dng_ref, dstate):
        @pl.when(pl.program_id(1) == 0)
        def _():
            dstate[...] = jnp.zeros_like(dstate)
            dlb_ref[...] = jnp.zeros_like(dlb_ref)
            dng_ref[...] = jnp.zeros_like(dng_ref)

        cm = _hg_common(z_ref, lb_ref, stk_ref)
        v, gt = z_ref[2], z_ref[3]
        ng = ng_ref[...]
        kd = cm["kk"] * cm["ed"]
        row = lax.broadcasted_iota(jnp.int32, (HG_CHUNK, 1), 0)
        dng = jnp.zeros((1, HG_K), F32)
        dq_h, dkk_h, db_h, dv_h, dgt_h = [], [], [], [], []
        dr_h = [[] for _ in range(HG_LEVELS)]
        for hh in range(4):
            sl = slice(HG_K * hh, HG_K * (hh + 1))
            st, dst = st_ref[hh], dstate[hh]
            qh, kh, vh, ebh, edh, kdh = cm["q"][:, sl], cm["kk"][:, sl], v[:, sl], cm["eb"][:, sl], cm["ed"][:, sl], kd[:, sl]
            elh = cm["elast"][:, sl]
            a, qls, kls = _hg_intra(qh, kh, cm["ws"], msk_ref, sl)
            qe = qh * ebh
            o = _bdot(a, vh, NN) + _bdot(qe, st, NT)
            r = lax.rsqrt(jnp.mean(o * o, axis=-1, keepdims=True) + EPS)
            xh = o * r
            gh = gt[:, sl]
            sgg = _sigmoid(gh)
            dog = dog_ref[:, sl].astype(F32)
            dy = dog * (gh * sgg)
            dgt_h.append(dog * (xh * ng) * (sgg * (1.0 + gh * (1.0 - sgg))))
            dng = dng + jnp.sum(dy * xh, axis=0, keepdims=True)
            dyg = dy * ng
            do = r * (dyg - xh * jnp.mean(dyg * xh, axis=-1, keepdims=True))
            da = _bdot(do, vh, NT)
            dv_h.append(_bdot(a, do, TN) + _bdot(kdh, dst, NT))
            dkd = _bdot(vh, dst, NN)
            delast = jnp.sum(st * dst, axis=0, keepdims=True)
            dqe = _bdot(do, st, NN)
            dstate[hh] = elh * dst + _bdot(do, qe, TN)
            gk = dkd * kdh
            dblast = jnp.sum(gk, axis=0, keepdims=True) + delast * elh
            db = dqe * qe - gk + jnp.where(row == HG_CHUNK - 1, dblast, 0.0)
            dp = (msk_ref[HG_LEVELS] * da).astype(BF16)
            dq = dqe * ebh + _bdot(dp, kh, NN)
            dkk = dkd * edh + _bdot(dp, qh, TN)
            for l in range(HG_LEVELS):
                dp = (msk_ref[l] * da).astype(BF16)
                dql, dkl = _bdot(dp, kls[l], NN), _bdot(dp, qls[l], TN)
                w = cm["ws"][l][:, sl]
                dq = dq + dql * w
                dkk = dkk + dkl * w
                half = jnp.where(((row >> (HG_LEVELS - 1 - l)) & 1) == 1, 1.0, -1.0)
                dd = half * w * (dql * qh + dkl * kh)
                db = db + dd
                dr_h[l].append(-dd)
            dq_h.append(dq)
            dkk_h.append(dkk)
            db_h.append(db)
        cat = lambda xs: jnp.concatenate(xs, axis=1)
        cot = jnp.concatenate([cat(db_h)] + [cat(dr_h[l]) for l in range(HG_LEVELS)], axis=0)
        dlf = _dot01(stk_ref[...], cot, TN)
        dq, dkk = cat(dq_h), cat(dkk_h)
        dft = dlf / cm["ft"]
        one_lb = 1.0 - cm["lb"]
        dz_ref[0] = dq * (cm["sq"] * (1.0 + cm["qr"] * (1.0 - cm["sq"])))
        dz_ref[1] = (dft - dkk) * one_lb * cm["sg"] * cm["sgn"]
        dz_ref[2] = cat(dv_h)
        dz_ref[3] = cat(dgt_h)
        dlb_ref[...] += jnp.sum((dft - dkk) * cm["sgn"], axis=0, keepdims=True)
        dng_ref[...] += dng

    rev = lambda c: nc - 1 - c
    return pl.pallas_call(
        body, name=name, grid=(2, nc),
        in_specs=[pl.BlockSpec((4, None, HG_CHUNK, HG_SLOT), lambda g, c: (0, g, rev(c), 0)),
                  pl.BlockSpec((1, HG_SLOT), lambda g, c: (0, g)),
                  pl.BlockSpec((1, HG_K), lambda g, c: (0, 0)),
                  pl.BlockSpec(stk.shape, lambda g, c: (0, 0)),
                  pl.BlockSpec(msk.shape, lambda g, c: (0, 0, 0)),
                  pl.BlockSpec((None, 4, HG_K, HG_K), lambda g, c: (rev(c), g, 0, 0)),
                  pl.BlockSpec((HG_CHUNK, HG_SLOT), lambda g, c: (rev(c), g))],
        out_specs=[pl.BlockSpec((4, None, HG_CHUNK, HG_SLOT), lambda g, c: (0, g, rev(c), 0)),
                   pl.BlockSpec((1, HG_SLOT), lambda g, c: (0, g)),
                   pl.BlockSpec((None, 1, HG_K), lambda g, c: (g, 0, 0))],
        out_shape=[_sds(z.shape, F32), _sds((1, 2 * HG_SLOT), F32), _sds((2, 1, HG_K), F32)],
        scratch_shapes=[pltpu.VMEM((4, HG_K, HG_K), F32)],
        compiler_params=_params(("parallel", "arbitrary")),
    )(z, lb, ng, stk, msk, states, dog)


def _lb_fwd(name, logits):
    def body(l_ref, o_ref):
        x = l_ref[...]
        e = jnp.exp(x - jnp.max(x, axis=0, keepdims=True))
        s = e / jnp.sum(e, axis=0, keepdims=True)
        o_ref[0:1, :] = s[1:2]
        o_ref[1:2, :] = s[1:2] + s[2:3] + s[3:4]

    return pl.pallas_call(body, name=name, out_shape=_sds((2, logits.shape[1]), F32))(logits)


def _lb_bwd(name, logits, dlb):
    def body(l_ref, d_ref, o_ref):
        x = l_ref[...]
        e = jnp.exp(x - jnp.max(x, axis=0, keepdims=True))
        s = e / jnp.sum(e, axis=0, keepdims=True)
        d1, d3 = d_ref[0:1, :], d_ref[1:2, :]
        ds = [jnp.zeros_like(d1), d1 + d3, d3, d3]
        dot = sum(ds[r] * s[r:r + 1] for r in range(1, DEPTH))
        for r in range(DEPTH):
            o_ref[r:r + 1, :] = s[r:r + 1] * (ds[r] - dot)

    return pl.pallas_call(body, name=name, out_shape=_sds(logits.shape, F32))(logits, dlb)


def _conv_taps(buf, p, w, b, tm):
    return b + w[0:1] * buf[p, 6:6 + tm] + w[1:2] * buf[p, 7:7 + tm] + w[2:3] * buf[p, 8:8 + tm]


def _convgate_fwd(name, u, cw, cb):
    S = u.shape[2]
    tm = _tile(S, 512)

    def body(u_ref, w_ref, b_ref, a_ref, buf):
        @pl.when(pl.program_id(1) == 0)
        def _():
            buf[:, 0:8, :] = jnp.zeros((2, 8, FF_SLOT), F32)

        buf[:, 8:8 + tm, :] = u_ref[...]
        cg = _conv_taps(buf, 0, w_ref[0], b_ref[0], tm)
        cv = _conv_taps(buf, 1, w_ref[1], b_ref[1], tm)
        a_ref[...] = (cg * _sigmoid(cg) * cv).astype(a_ref.dtype)
        buf[:, 0:8, :] = buf[:, tm:tm + 8, :]

    return pl.pallas_call(
        body, name=name, grid=(4, S // tm),
        in_specs=[pl.BlockSpec((2, None, tm, FF_SLOT), lambda j, t: (0, j, t, 0)),
                  pl.BlockSpec((2, None, 3, FF_SLOT), lambda j, t: (0, j, 0, 0)),
                  pl.BlockSpec((2, None, 1, FF_SLOT), lambda j, t: (0, j, 0, 0))],
        out_specs=pl.BlockSpec((None, tm, FF_SLOT), lambda j, t: (j, t, 0)),
        out_shape=_sds((4, S, FF_SLOT), BF16),
        scratch_shapes=[pltpu.VMEM((2, tm + 8, FF_SLOT), F32)],
        compiler_params=_params(("parallel", "arbitrary")),
    )(u, cw, cb)


def _convgate_bwd(name, u, cw, cb, da):
    S = u.shape[2]
    tm = _tile(S, 512)
    nt = S // tm

    def body(u_ref, uh_ref, w_ref, b_ref, da_ref, du_ref, dw_ref, db_ref, buf, dbuf):
        t = pl.program_id(1)

        @pl.when(t == 0)
        def _():
            dbuf[:, tm:tm + 8, :] = jnp.zeros((2, 8, FF_SLOT), F32)
            dw_ref[...] = jnp.zeros_like(dw_ref)
            db_ref[...] = jnp.zeros_like(db_ref)

        buf[:, 0:8, :] = jnp.where(t < nt - 1, uh_ref[...], 0.0)
        buf[:, 8:8 + tm, :] = u_ref[...]
        cg = _conv_taps(buf, 0, w_ref[0], b_ref[0], tm)
        cv = _conv_taps(buf, 1, w_ref[1], b_ref[1], tm)
        sg = _sigmoid(cg)
        dav = da_ref[...].astype(F32)
        dc = [dav * cv * (sg * (1.0 + cg * (1.0 - sg))), dav * cg * sg]
        for p in range(2):
            dbuf[p, 0:tm, :] = dc[p]
            w = w_ref[p]
            du_ref[p] = w[2:3] * dbuf[p, 0:tm] + w[1:2] * dbuf[p, 1:tm + 1] + w[0:1] * dbuf[p, 2:tm + 2]
            dbuf[p, tm:tm + 8, :] = dc[p][0:8]
            for j in range(3):
                dw_ref[p, j:j + 1, :] += jnp.sum(dc[p] * buf[p, 6 + j:6 + j + tm], axis=0, keepdims=True)
            db_ref[p] += jnp.sum(dc[p], axis=0, keepdims=True)

    rev = lambda t: nt - 1 - t
    return pl.pallas_call(
        body, name=name, grid=(4, nt),
        in_specs=[pl.BlockSpec((2, None, tm, FF_SLOT), lambda j, t: (0, j, rev(t), 0)),
                  pl.BlockSpec((2, None, 8, FF_SLOT), lambda j, t: (0, j, jnp.maximum(rev(t) * (tm // 8) - 1, 0), 0)),
                  pl.BlockSpec((2, None, 3, FF_SLOT), lambda j, t: (0, j, 0, 0)),
                  pl.BlockSpec((2, None, 1, FF_SLOT), lambda j, t: (0, j, 0, 0)),
                  pl.BlockSpec((None, tm, FF_SLOT), lambda j, t: (j, rev(t), 0))],
        out_specs=[pl.BlockSpec((2, None, tm, FF_SLOT), lambda j, t: (0, j, rev(t), 0)),
                   pl.BlockSpec((2, None, 3, FF_SLOT), lambda j, t: (0, j, 0, 0)),
                   pl.BlockSpec((2, None, 1, FF_SLOT), lambda j, t: (0, j, 0, 0))],
        out_shape=[_sds(u.shape, F32), _sds(cw.shape, F32), _sds(cb.shape, F32)],
        scratch_shapes=[pltpu.VMEM((2, tm + 8, FF_SLOT), F32), pltpu.VMEM((2, tm + 8, FF_SLOT), F32)],
        compiler_params=_params(("parallel", "arbitrary")),
    )(u, u, cw, cb, da)


def _row_tile(R):
    for t in range(256, 15, -16):
        if R % t == 0:
            return t
    return R


def _sum_sources(name, gsrc):
    n, R, C = gsrc.shape
    tr = _row_tile(R)

    def body(g_ref, o_ref):
        g = g_ref[0].astype(F32)
        for s in range(1, n):
            g = g + g_ref[s].astype(F32)
        o_ref[...] = g

    return pl.pallas_call(
        body, name=name, grid=(R // tr,), in_specs=[pl.BlockSpec((n, tr, C), lambda i: (0, i, 0))],
        out_specs=pl.BlockSpec((tr, C), lambda i: (i, 0)), out_shape=_sds((R, C), F32),
        compiler_params=_params(("parallel",)),
    )(gsrc)


def _adamw(name, gsrcs, w, m, v):
    L = len(gsrcs)
    n, A, C = gsrcs[0].shape
    tr = _row_tile(A)

    def body(*refs):
        g_refs = refs[:L]
        w_ref, m_ref, v_ref, go_ref, d_ref, mo_ref, vo_ref = refs[L:]
        for k in range(L):
            @pl.when(pl.program_id(0) == k)
            def _(k=k):
                g = g_refs[k][0].astype(F32)
                for s in range(1, n):
                    g = g + g_refs[k][s].astype(F32)
                m2 = ADAM_B1 * m_ref[...] + (1.0 - ADAM_B1) * g
                v2 = ADAM_B2 * v_ref[...] + (1.0 - ADAM_B2) * (g * g)
                m_hat = m2 / (1.0 - ADAM_B1 ** ADAM_STEP)
                v_hat = v2 / (1.0 - ADAM_B2 ** ADAM_STEP)
                go_ref[...] = g
                d_ref[...] = -ADAM_LR * (m_hat / (jnp.sqrt(v_hat) + ADAM_EPS) + ADAM_WD * w_ref[...])
                mo_ref[...] = m2
                vo_ref[...] = v2

    g_specs = [pl.BlockSpec((n, tr, C), lambda l, i, k=k: (0, jnp.where(l == k, i, 0), 0)) for k in range(L)]
    blk = pl.BlockSpec((None, tr, C), lambda l, i: (l, i, 0))
    return pl.pallas_call(
        body, name=name, grid=(L, A // tr), in_specs=g_specs + [blk, blk, blk],
        out_specs=[blk] * 4, out_shape=[_sds((L, A, C), F32)] * 4, compiler_params=_params(("parallel", "parallel")),
    )(*gsrcs, w, m, v)


MESH = pl.DeviceIdType.MESH
HBM_SPEC = pl.BlockSpec(memory_space=pltpu.HBM)
N_PEERS = N_DEV - 1


def _mesh_place():
    x, y, c = lax.axis_index("x"), lax.axis_index("y"), lax.axis_index("c")
    peers = []
    for p in range(1, N_DEV):
        px = 1 - x if p & 4 else x
        py = 1 - y if p & 2 else y
        pc = 1 - c if p & 1 else c
        peers.append(((px, py, pc), 4 * px + 2 * py + pc))
    return 4 * x + 2 * y + c, peers


SEM_SPEC = pl.BlockSpec(memory_space=pltpu.SEMAPHORE)
ANY_SPEC = pl.BlockSpec(memory_space=pl.ANY)
EFFECT = pltpu.SideEffectType.DATAFLOW_SIDE_EFFECTING


def _exchange_refs(scatter, src, land, send, recv, k, p, dev, idx, me):
    return pltpu.make_async_remote_copy(src_ref=src[k].at[idx] if scatter else src[k], dst_ref=land[k].at[me],
                                        send_sem=send.at[k * N_PEERS + p], recv_sem=recv.at[k * N_PEERS + p], device_id=dev,
                                        device_id_type=MESH)


def _exchange_start(name, srcs, scatter, gate):
    n = len(srcs)
    lands = [lax.empty(s.shape if scatter else (N_DEV,) + s.shape, s.dtype) for s in srcs]

    def body(*refs):
        src, land = refs[:n], refs[n:2 * n]
        send, recv, own = refs[2 * n + 1:2 * n + 4]
        token = refs[-1]
        me, peers = _mesh_place()
        for k in range(n):
            pltpu.make_async_copy(src[k].at[me] if scatter else src[k], land[k].at[me], own.at[k]).start()
            for p, (dev, idx) in enumerate(peers):
                _exchange_refs(scatter, src, land, send, recv, k, p, dev, idx, me).start()
        token[...] = jnp.zeros_like(token)

    hbm = lambda a: pltpu.HBM(a.shape, a.dtype)
    outs = pl.pallas_call(
        body, name=name,
        out_shape=(pltpu.SemaphoreType.DMA((n * N_PEERS,)), pltpu.SemaphoreType.DMA((n * N_PEERS,)),
                   pltpu.SemaphoreType.DMA((n,)), *[hbm(s) for s in srcs], *[hbm(s) for s in lands], _sds(DEP_SHAPE, F32)),
        in_specs=[HBM_SPEC] * (2 * n) + [ANY_SPEC],
        out_specs=(SEM_SPEC, SEM_SPEC, SEM_SPEC, *[HBM_SPEC] * (2 * n), pl.BlockSpec(memory_space=pltpu.VMEM)),
        input_output_aliases={j: 3 + j for j in range(2 * n)},
        compiler_params=pltpu.CompilerParams(has_side_effects=EFFECT),
    )(*[pltpu.with_memory_space_constraint(s, pltpu.HBM) for s in srcs],
      *[pltpu.with_memory_space_constraint(s, pltpu.HBM) for s in lands], gate)
    return outs[:3], None, list(outs[3:3 + n]), list(outs[3 + n:3 + 2 * n]), outs[-1]


def _exchange_wait(name, started, scatter, after):
    (send, recv, own), _, srcs, lands, _ = started
    n = len(srcs)

    def body(*refs):
        src, land = refs[:n], refs[n:2 * n]
        send, recv, own = refs[2 * n:2 * n + 3]
        me, peers = _mesh_place()
        for k in range(n):
            pltpu.make_async_copy(src[k].at[me] if scatter else src[k], land[k].at[me], own.at[k]).wait()
            for p, (dev, idx) in enumerate(peers):
                cp = pltpu.make_async_remote_copy(src_ref=src[k].at[idx] if scatter else src[k], dst_ref=land[k].at[idx],
                                                  send_sem=send.at[k * N_PEERS + p], recv_sem=recv.at[k * N_PEERS + p], device_id=dev,
                                                  device_id_type=MESH)
                cp.wait_send()
                cp.wait_recv()

    hbm = lambda a: pltpu.HBM(a.shape, a.dtype)
    outs = pl.pallas_call(
        body, name=name, out_shape=(*[hbm(s) for s in srcs], *[hbm(s) for s in lands]),
        in_specs=[HBM_SPEC] * (2 * n) + [SEM_SPEC, SEM_SPEC, SEM_SPEC, ANY_SPEC], out_specs=tuple([HBM_SPEC] * (2 * n)),
        input_output_aliases={j: j for j in range(2 * n)},
        compiler_params=pltpu.CompilerParams(has_side_effects=EFFECT),
    )(*srcs, *lands, send, recv, own, after)
    return list(outs[n:])


def _allsum_rows(name, part):
    R, C = part.shape

    def body(p_ref, o_ref, gath, send, recv):
        me, peers = _mesh_place()
        gath[me] = p_ref[...]
        sends = []
        for p, (dev, _) in enumerate(peers):
            cp = pltpu.make_async_remote_copy(src_ref=p_ref, dst_ref=gath.at[me], send_sem=send.at[p], recv_sem=recv.at[p],
                                              device_id=dev, device_id_type=MESH)
            cp.start()
            sends.append(cp)
        for p, (dev, idx) in enumerate(peers):
            pltpu.make_async_remote_copy(src_ref=p_ref, dst_ref=gath.at[idx], send_sem=send.at[p], recv_sem=recv.at[p],
                                         device_id=dev, device_id_type=MESH).wait_recv()
        for cp in sends:
            cp.wait_send()
        tot = gath[0]
        for j in range(1, N_DEV):
            tot = tot + gath[j]
        o_ref[...] = tot

    vm = pl.BlockSpec(memory_space=pltpu.VMEM)
    return pl.pallas_call(
        body, name=name, in_specs=[vm], out_specs=vm, out_shape=_sds((R, C), F32),
        scratch_shapes=[pltpu.VMEM((N_DEV, R, C), F32), pltpu.SemaphoreType.DMA((N_PEERS,)),
                        pltpu.SemaphoreType.DMA((N_PEERS,))],
        compiler_params=pltpu.CompilerParams(vmem_limit_bytes=VMEM_LIMIT),
    )(part)


def _rows(a, width=D_MODEL):
    flat = a.reshape(-1)
    return jnp.pad(flat, (0, (-flat.shape[0]) % width)).reshape(-1, width)


def _pack_rows(parts):
    blocks = []
    for p in parts:
        r = _rows(p)
        blocks.append(jnp.pad(r, ((0, (-r.shape[0]) % 8), (0, 0))))
    return jnp.concatenate(blocks, axis=0)


def _unpack_rows(rows, shapes):
    out, at = [], 0
    for s in shapes:
        size = int(np.prod(s))
        n = -(-size // D_MODEL)
        out.append(rows[at:at + n].reshape(-1)[:size].reshape(s))
        at += -(-n // 8) * 8
    return out


def kernel(x, norm_mix, norm_ffn, norm_final, attn_w_in, attn_w_out, attn_sinks, hgrn_w_in, hgrn_w_out, hgrn_norm, hgrn_lb_logits, ffn_w_up, ffn_conv_w, ffn_conv_b, ffn_w_down, loss_target, m_norm_mix, m_norm_ffn, m_norm_final, m_attn_w_in, m_attn_w_out, m_attn_sinks, m_hgrn_w_in, m_hgrn_w_out, m_hgrn_norm, m_hgrn_lb_logits, m_ffn_w_up, m_ffn_conv_w, m_ffn_conv_b, m_ffn_w_down, v_norm_mix, v_norm_ffn, v_norm_final, v_attn_w_in, v_attn_w_out, v_attn_sinks, v_hgrn_w_in, v_hgrn_w_out, v_hgrn_norm, v_hgrn_lb_logits, v_ffn_w_up, v_ffn_conv_w, v_ffn_conv_b, v_ffn_w_down):
    S = x.shape[1]
    n_attn, n_hgrn = attn_w_in.shape[0], hgrn_w_in.shape[0]
    me = 4 * lax.axis_index("x") + 2 * lax.axis_index("y") + lax.axis_index("c")

    wa_in_t, wa_out_b = attn_w_in.transpose(0, 2, 1).astype(BF16), attn_w_out.astype(BF16)
    wh_in_b, wh_out_b = hgrn_w_in.astype(BF16), hgrn_w_out.astype(BF16)
    wf_up_b, wf_down_b = ffn_w_up.astype(BF16), ffn_w_down.astype(BF16)
    conv_b = ffn_conv_b.reshape(DEPTH, 2, 4, 1, FF_SLOT)
    lb = _lb_fwd("lb_fwd", hgrn_lb_logits)

    def unit_shards(l, part):
        if part == "ffn":
            return [wf_up_b[l], wf_down_b[l], ffn_conv_w[l]]
        return [wa_in_t[l // 2], wa_out_b[l // 2]] if l % 2 == 0 else [wh_in_b[l // 2], wh_out_b[l // 2]]

    def unit_weights(l, part, w):
        if part == "ffn":
            return w[0][None], w[1].reshape(1, 4, FF_SLOT, D_MODEL), w[2].reshape(2, 4, 3, FF_SLOT)
        if l % 2 == 0:
            return w[0].reshape(1, ATTN_IN, D_MODEL), w[1].reshape(1, D_MODEL, D_MODEL)
        return w[0][None], w[1].reshape(1, D_MODEL, D_MODEL)

    units = [(l, part) for l in range(DEPTH) for part in ("mix", "ffn")]
    started = _exchange_start("gather_start0", unit_shards(*units[0]), False, norm_final)
    arrived = _exchange_wait("gather_wait0", started, False, started[4])
    weights, saved = {}, [dict() for _ in range(DEPTH)]
    h = x[0]
    for n, (l, part) in enumerate(units):
        i, sv = l // 2, saved[l]
        weights[l, part] = w = unit_weights(l, part, arrived)
        started = dep = None
        if n + 1 < len(units):
            started = _exchange_start(f"gather_start{n + 1}", unit_shards(*units[n + 1]), False, arrived[0])
            dep = started[4]
        if part == "mix":
            sv["h"] = h
            sv["hn"] = hn = _rmsnorm_fwd(f"norm_mix_fwd{l}", h, norm_mix[l:l + 1], dep)
            if l % 2 == 0:
                sv["proj"] = _proj_rows(f"attn_proj{i}", hn, w[0], 0, BF16)
                sv["o"] = _attn_fwd(f"attn_fwd{i}", sv["proj"], attn_sinks[i:i + 1])
                h = _out_proj(f"attn_out{i}", sv["o"], w[1], 0, h)
            else:
                sv["z"] = _proj_slots(f"hgrn_proj{i}", hn, w[0], 0).reshape(4, 2, S, HG_SLOT)
                sv["o"], sv["states"] = _hg_fwd(f"hgrn_fwd{i}", sv["z"], lb[i:i + 1], hgrn_norm[i:i + 1])
                h = _out_proj(f"hgrn_out{i}", sv["o"], w[1], 0, h)
        else:
            sv["h2"] = h
            sv["hn2"] = _rmsnorm_fwd(f"norm_ffn_fwd{l}", h, norm_ffn[l:l + 1], dep)
            sv["u"] = _proj_slots(f"ffn_up{l}", sv["hn2"], w[0], 0).reshape(2, 4, S, FF_SLOT)
            sv["a"] = _convgate_fwd(f"ffn_gate{l}", sv["u"], w[2], conv_b[l])
            h = _down_proj(f"ffn_down{l}", sv["a"], w[1], 0, h)
        if started is not None:
            arrived = _exchange_wait(f"gather_wait{n + 1}", started, False, h)
    dh, d_norm_final, loss_rows = _loss_head("loss_head", h, norm_final[None], loss_target[0])

    d_conv_w, d_conv_b, d_norm_mix, d_norm_ffn = [None] * DEPTH, [None] * DEPTH, [None] * DEPTH, [None] * DEPTH
    d_sinks, d_lb, d_hgrn_norm = [None] * n_attn, [None] * n_hgrn, [None] * n_hgrn
    received, started, before = {}, None, None
    for l, part in reversed(units):
        i, sv, w = l // 2, saved[l], weights[l, part]
        dep = None if started is None else started[4]
        if part == "ffn":
            da = _dgrad_down(f"ffn_down_dgrad{l}", dh, w[1], 0, dep)
            g_down = _wgrad_down(f"ffn_down_wgrad{l}", sv["a"], dh).reshape(N_DEV, D_FF // N_DEV, D_MODEL)
            du, d_conv_w[l], d_conv_b[l] = _convgate_bwd(f"ffn_gate_bwd{l}", sv["u"], w[2], conv_b[l], da)
            du = du.reshape(N_DEV, S, FF_SLOT)
            dhn2 = _dgrad_slots(f"ffn_up_dgrad{l}", du, w[0], 0)
            grads = [_wgrad_slots(f"ffn_up_wgrad{l}", sv["hn2"], du), g_down]
            dh, d_norm_ffn[l] = _rmsnorm_bwd(f"norm_ffn_bwd{l}", sv["h2"], norm_ffn[l:l + 1], dhn2, dh)
        else:
            if l % 2 == 0:
                do = _dgrad_out(f"attn_out_dgrad{i}", dh, w[1], 0, BF16, dep)
                g_out = _wgrad_rows(f"attn_out_wgrad{i}", sv["o"], dh)
                dproj, d_sinks[i] = _attn_bwd(f"attn_bwd{i}", sv["proj"], attn_sinks[i:i + 1], do)
                dhn = _dgrad_rows(f"attn_proj_dgrad{i}", dproj, w[0], 0)
                g_in = _wgrad_rows(f"attn_proj_wgrad{i}", dproj, sv["hn"]).reshape(N_DEV, ATTN_IN // N_DEV, D_MODEL)
            else:
                dog = _dgrad_out(f"hgrn_out_dgrad{i}", dh, w[1], 0, F32, dep)
                g_out = _wgrad_rows(f"hgrn_out_wgrad{i}", sv["o"], dh)
                dz, d_lb[i], dng = _hg_bwd(f"hgrn_bwd{i}", sv["z"], lb[i:i + 1], hgrn_norm[i:i + 1], sv["states"], dog)
                d_hgrn_norm[i] = dng[0] + dng[1]
                dz = dz.reshape(N_DEV, S, HG_SLOT)
                dhn = _dgrad_slots(f"hgrn_proj_dgrad{i}", dz, w[0], 0)
                g_in = _wgrad_slots(f"hgrn_proj_wgrad{i}", sv["hn"], dz)
            grads = [g_in, g_out.reshape(N_DEV, D_MODEL // N_DEV, D_MODEL)]
            dh, d_norm_mix[l] = _rmsnorm_bwd(f"norm_mix_bwd{l}", sv["h"], norm_mix[l:l + 1], dhn, dh)
        gate = dh
        if started is not None:
            received[before] = _exchange_wait(f"scatter_wait_{before[1]}{before[0]}", started, True, dh)
            gate = received[before][0]
        started, before = _exchange_start(f"scatter_start_{part}{l}", grads, True, gate), (l, part)
    received[before] = _exchange_wait(f"scatter_wait_{before[1]}{before[0]}", started, True, started[4])
    grad_x = dh[None]

    small_shapes = [(DEPTH, D_MODEL), (DEPTH, D_MODEL), (1, D_MODEL), (1, D_MODEL), (n_hgrn, D_MODEL), (n_attn, 128),
                    (n_hgrn, HG_K), (DEPTH, 2 * D_FF), (DEPTH, N_DEV, 3, FF_SLOT)]
    total = _allsum_rows("allsum_small", _pack_rows([
        jnp.concatenate(d_norm_mix), jnp.concatenate(d_norm_ffn), d_norm_final, loss_rows, jnp.concatenate(d_lb),
        jnp.concatenate(d_sinks), jnp.concatenate(d_hgrn_norm), jnp.stack(d_conv_b), jnp.stack(d_conv_w)]))
    (g_norm_mix, g_norm_ffn, g_norm_final, loss_sum, g_lb, g_sinks, g_hgrn_norm, g_conv_b, g_conv_w_all) = _unpack_rows(
        total, small_shapes)
    loss = jnp.sum(loss_sum)
    g_norm_final = g_norm_final[0]
    g_sinks = g_sinks[:, :N_Q_HEADS]
    g_lb_logits = _lb_bwd("lb_bwd", hgrn_lb_logits, g_lb)
    g_conv_w = lax.dynamic_index_in_dim(g_conv_w_all, me, axis=1, keepdims=False)

    attn_layers, hgrn_layers = range(0, DEPTH, 2), range(1, DEPTH, 2)
    g_attn_in_t = [_sum_sources(f"sum_attn_in{l // 2}", received[l, "mix"][0]).T[None] for l in attn_layers]
    big = {
        "attn_w_in": _adamw("adamw_attn_in", g_attn_in_t, attn_w_in, m_attn_w_in, v_attn_w_in),
        "attn_w_out": _adamw("adamw_attn_out", [received[l, "mix"][1] for l in attn_layers], attn_w_out, m_attn_w_out, v_attn_w_out),
        "hgrn_w_in": _adamw("adamw_hgrn_in", [received[l, "mix"][0] for l in hgrn_layers], hgrn_w_in, m_hgrn_w_in, v_hgrn_w_in),
        "hgrn_w_out": _adamw("adamw_hgrn_out", [received[l, "mix"][1] for l in hgrn_layers], hgrn_w_out, m_hgrn_w_out, v_hgrn_w_out),
        "ffn_w_up": _adamw("adamw_ffn_up", [received[l, "ffn"][0] for l in range(DEPTH)], ffn_w_up, m_ffn_w_up, v_ffn_w_up),
        "ffn_w_down": _adamw("adamw_ffn_down", [received[l, "ffn"][1] for l in range(DEPTH)], ffn_w_down, m_ffn_w_down, v_ffn_w_down),
        "ffn_conv_w": _adamw("adamw_conv_w", [g_conv_w[l][None] for l in range(DEPTH)], ffn_conv_w, m_ffn_conv_w, v_ffn_conv_w),
    }
    small_w = [norm_mix, norm_ffn, norm_final, attn_sinks, hgrn_norm, hgrn_lb_logits, ffn_conv_b]
    small_m = [m_norm_mix, m_norm_ffn, m_norm_final, m_attn_sinks, m_hgrn_norm, m_hgrn_lb_logits, m_ffn_conv_b]
    small_v = [v_norm_mix, v_norm_ffn, v_norm_final, v_attn_sinks, v_hgrn_norm, v_hgrn_lb_logits, v_ffn_conv_b]
    small_g = [g_norm_mix, g_norm_ffn, g_norm_final, g_sinks, g_hgrn_norm, g_lb_logits, g_conv_b]
    outs = _adamw("adamw_small", [_pack_rows(small_g)[None]], *[_pack_rows(t)[None] for t in (small_w, small_m, small_v)])
    outs = [o[0] for o in outs]
    shapes = [w.shape for w in small_w]
    small = {n: [t[j] for t in [_unpack_rows(o, shapes) for o in outs]]
             for j, n in enumerate(["norm_mix", "norm_ffn", "norm_final", "attn_sinks", "hgrn_norm", "hgrn_lb_logits", "ffn_conv_b"])}
    order = ["norm_mix", "norm_ffn", "norm_final", "attn_w_in", "attn_w_out", "attn_sinks", "hgrn_w_in", "hgrn_w_out",
             "hgrn_norm", "hgrn_lb_logits", "ffn_w_up", "ffn_conv_w", "ffn_conv_b", "ffn_w_down"]
    res = {**big, **small}
    return (loss, grad_x, *[res[n][0] for n in order], *[res[n][1] for n in order], *[res[n][2] for n in order],
            *[res[n][3] for n in order])
```

```python
import numpy as np
import jax
import jax.numpy as jnp
from jax import lax
from jax.experimental import pallas as pl
from jax.experimental.pallas import tpu as pltpu

F32 = jnp.float32
BF16 = jnp.bfloat16

D_MODEL = 1024
DEPTH = 4
HEAD_DIM = 64
N_Q_HEADS = 16
N_KV_HEADS = 4
Q_PER_KV = 4
ATTN_BLOCK = 128
ATTN_IN = 1536
HG_HEADS = 8
HG_K = 128
HG_CHUNK = 64
HG_IN = 4096
D_FF = 2816
EPS = 1e-6
N_DEV = 8
FF_SLOT = 2 * D_FF // N_DEV
HG_SLOT = HG_IN // N_DEV
HG_LEVELS = 6

ADAM_LR = 0.001
ADAM_B1 = 0.9
ADAM_B2 = 0.999
ADAM_EPS = 1e-08
ADAM_WD = 0.01
ADAM_STEP = 10

VMEM_LIMIT = 56 * 1024 * 1024
ROW_TILE = 1024
WIDE_ROW_TILE = 2048
NEG_BIG = -1e30

NN = (((1,), (0,)), ((), ()))
NT = (((1,), (1,)), ((), ()))
TN = (((0,), (0,)), ((), ()))


def _bdot(a, b, dn):
    return lax.dot_general(a.astype(BF16), b.astype(BF16), dn, preferred_element_type=F32)


def _sds(shape, dtype):
    return jax.ShapeDtypeStruct(tuple(shape), dtype)


def _params(sem):
    return pltpu.CompilerParams(dimension_semantics=sem, vmem_limit_bytes=VMEM_LIMIT)


DEP_SHAPE = (8, 128)


def _dep_spec(rank):
    return pl.BlockSpec(DEP_SHAPE, lambda *_: (0, 0))


def _matmul(name, a, b, *, dn, grid, a_spec, b_spec, o_spec, out_shape, acc_shape=None, res=None, res_spec=None, dep=None):
    nk = grid[2]
    n_in = 2 + (res is not None) + (dep is not None)

    def body(*refs):
        a_ref, b_ref = refs[0], refs[1]
        r_ref = refs[2] if res is not None else None
        o_ref = refs[n_in]

        def prod():
            return _bdot(a_ref[...], b_ref[...], dn)

        def finish(v):
            if r_ref is not None:
                v = v + r_ref[...]
            o_ref[...] = v.astype(o_ref.dtype)

        if nk == 1:
            finish(prod())
        else:
            acc = refs[-1]
            k = pl.program_id(2)

            @pl.when(k == 0)
            def _():
                acc[...] = prod()

            @pl.when(k > 0)
            def _():
                acc[...] += prod()

            @pl.when(k == nk - 1)
            def _():
                finish(acc[...])

    in_specs = [a_spec, b_spec] + ([res_spec] if res is not None else []) + ([_dep_spec(3)] if dep is not None else [])
    args = (a, b) + ((res,) if res is not None else ()) + ((dep,) if dep is not None else ())
    scratch = [] if nk == 1 else [pltpu.VMEM(acc_shape, F32)]
    return pl.pallas_call(
        body, name=name, grid=grid, in_specs=in_specs, out_specs=o_spec, out_shape=out_shape,
        scratch_shapes=scratch, compiler_params=_params(("parallel", "parallel", "arbitrary")),
    )(*args)


def _tile(n, t):
    return min(n, t)


def _proj_rows(name, hn, wt, l, out_dtype):
    S, N = hn.shape[0], wt.shape[1]
    tm, tn = _tile(S, ROW_TILE), 512
    return _matmul(
        name, hn, wt, dn=NT, grid=(S // tm, N // tn, 1),
        a_spec=pl.BlockSpec((tm, D_MODEL), lambda i, j, k: (i, 0)),
        b_spec=pl.BlockSpec((None, tn, D_MODEL), lambda i, j, k: (l, j, 0)),
        o_spec=pl.BlockSpec((tm, tn), lambda i, j, k: (i, j)),
        out_shape=_sds((S, N), out_dtype))


def _proj_slots(name, hn, w, l):
    S, r = hn.shape[0], w.shape[3]
    tm = _tile(S, WIDE_ROW_TILE)
    return _matmul(
        name, hn, w, dn=NN, grid=(N_DEV, S // tm, 1),
        a_spec=pl.BlockSpec((tm, D_MODEL), lambda j, i, k: (i, 0)),
        b_spec=pl.BlockSpec((None, None, D_MODEL, r), lambda j, i, k: (l, j, 0, 0)),
        o_spec=pl.BlockSpec((None, tm, r), lambda j, i, k: (j, i, 0)),
        out_shape=_sds((N_DEV, S, r), F32))


def _out_proj(name, o, w, l, h):
    S, K = o.shape
    tm = _tile(S, ROW_TILE)
    return _matmul(
        name, o, w, dn=NN, grid=(S // tm, 1, 1),
        a_spec=pl.BlockSpec((tm, K), lambda i, j, k: (i, 0)),
        b_spec=pl.BlockSpec((None, K, D_MODEL), lambda i, j, k: (l, 0, 0)),
        o_spec=pl.BlockSpec((tm, D_MODEL), lambda i, j, k: (i, 0)),
        out_shape=_sds((S, D_MODEL), F32), res=h,
        res_spec=pl.BlockSpec((tm, D_MODEL), lambda i, j, k: (i, 0)))


def _down_proj(name, a, w, l, h):
    nj, S, r = a.shape
    tm = _tile(S, ROW_TILE)
    return _matmul(
        name, a, w, dn=NN, grid=(S // tm, 1, nj),
        a_spec=pl.BlockSpec((None, tm, r), lambda i, j, k: (k, i, 0)),
        b_spec=pl.BlockSpec((None, None, r, D_MODEL), lambda i, j, k: (l, k, 0, 0)),
        o_spec=pl.BlockSpec((tm, D_MODEL), lambda i, j, k: (i, 0)),
        out_shape=_sds((S, D_MODEL), F32), acc_shape=(tm, D_MODEL), res=h,
        res_spec=pl.BlockSpec((tm, D_MODEL), lambda i, j, k: (i, 0)))


def _dgrad_down(name, dh, w, l, dep=None):
    S = dh.shape[0]
    nj, r = w.shape[1], w.shape[2]
    tm = _tile(S, ROW_TILE)
    return _matmul(
        name, dh, w, dn=NT, grid=(nj, S // tm, 1),
        a_spec=pl.BlockSpec((tm, D_MODEL), lambda j, i, k: (i, 0)),
        b_spec=pl.BlockSpec((None, None, r, D_MODEL), lambda j, i, k: (l, j, 0, 0)),
        o_spec=pl.BlockSpec((None, tm, r), lambda j, i, k: (j, i, 0)),
        out_shape=_sds((nj, S, r), BF16), dep=dep)


def _wgrad_down(name, a, dh):
    nj, S, r = a.shape
    tk = _tile(S, ROW_TILE)
    return _matmul(
        name, a, dh, dn=TN, grid=(nj, 1, S // tk),
        a_spec=pl.BlockSpec((None, tk, r), lambda s, j, k: (s, k, 0)),
        b_spec=pl.BlockSpec((tk, D_MODEL), lambda s, j, k: (k, 0)),
        o_spec=pl.BlockSpec((None, r, D_MODEL), lambda s, j, k: (s, 0, 0)),
        out_shape=_sds((nj, r, D_MODEL), BF16), acc_shape=(r, D_MODEL))


def _dgrad_slots(name, dz, w, l):
    nj, S, r = dz.shape
    tm = _tile(S, ROW_TILE)
    return _matmul(
        name, dz, w, dn=NT, grid=(S // tm, 1, nj),
        a_spec=pl.BlockSpec((None, tm, r), lambda i, j, k: (k, i, 0)),
        b_spec=pl.BlockSpec((None, None, D_MODEL, r), lambda i, j, k: (l, k, 0, 0)),
        o_spec=pl.BlockSpec((tm, D_MODEL), lambda i, j, k: (i, 0)),
        out_shape=_sds((S, D_MODEL), F32), acc_shape=(tm, D_MODEL))


def _wgrad_slots(name, hn, dz):
    nj, S, r = dz.shape
    tk = _tile(S, ROW_TILE)
    return _matmul(
        name, hn, dz, dn=TN, grid=(nj, 1, S // tk),
        a_spec=pl.BlockSpec((tk, D_MODEL), lambda s, j, k: (k, 0)),
        b_spec=pl.BlockSpec((None, tk, r), lambda s, j, k: (s, k, 0)),
        o_spec=pl.BlockSpec((None, D_MODEL, r), lambda s, j, k: (s, 0, 0)),
        out_shape=_sds((nj, D_MODEL, r), BF16), acc_shape=(D_MODEL, r))


def _dgrad_out(name, dh, w, l, out_dtype, dep=None):
    S, K = dh.shape[0], w.shape[1]
    tm = _tile(S, ROW_TILE)
    return _matmul(
        name, dh, w, dn=NT, grid=(S // tm, 1, 1),
        a_spec=pl.BlockSpec((tm, D_MODEL), lambda i, j, k: (i, 0)),
        b_spec=pl.BlockSpec((None, K, D_MODEL), lambda i, j, k: (l, 0, 0)),
        o_spec=pl.BlockSpec((tm, K), lambda i, j, k: (i, 0)),
        out_shape=_sds((S, K), out_dtype), dep=dep)


def _wgrad_rows(name, a, b):
    S, K = a.shape
    tk = _tile(S, ROW_TILE)
    return _matmul(
        name, a, b, dn=TN, grid=(1, 1, S // tk),
        a_spec=pl.BlockSpec((tk, K), lambda i, j, k: (k, 0)),
        b_spec=pl.BlockSpec((tk, D_MODEL), lambda i, j, k: (k, 0)),
        o_spec=pl.BlockSpec((K, D_MODEL), lambda i, j, k: (0, 0)),
        out_shape=_sds((K, D_MODEL), BF16), acc_shape=(K, D_MODEL))


def _dgrad_rows(name, dz, wt, l):
    S, N = dz.shape
    tm = _tile(S, ROW_TILE)
    return _matmul(
        name, dz, wt, dn=NN, grid=(S // tm, 1, 1),
        a_spec=pl.BlockSpec((tm, N), lambda i, j, k: (i, 0)),
        b_spec=pl.BlockSpec((None, N, D_MODEL), lambda i, j, k: (l, 0, 0)),
        o_spec=pl.BlockSpec((tm, D_MODEL), lambda i, j, k: (i, 0)),
        out_shape=_sds((S, D_MODEL), F32))


def _rmsnorm_fwd(name, h, g, dep=None):
    S = h.shape[0]
    tm = _tile(S, ROW_TILE)

    def body(h_ref, g_ref, *rest):
        o_ref = rest[-1]
        x = h_ref[...]
        r = lax.rsqrt(jnp.mean(x * x, axis=-1, keepdims=True) + EPS)
        o_ref[...] = (x * r * g_ref[...]).astype(o_ref.dtype)

    row = pl.BlockSpec((tm, D_MODEL), lambda i: (i, 0))
    deps = () if dep is None else (dep,)
    return pl.pallas_call(
        body, name=name, grid=(S // tm,),
        in_specs=[row, pl.BlockSpec((1, D_MODEL), lambda i: (0, 0))] + [_dep_spec(1)] * len(deps),
        out_specs=row, out_shape=_sds((S, D_MODEL), BF16), compiler_params=_params(("parallel",)),
    )(h, g, *deps)


def _rmsnorm_bwd(name, h, g, dy, dres):
    S = h.shape[0]
    tm = _tile(S, ROW_TILE)

    def body(h_ref, g_ref, dy_ref, dres_ref, dh_ref, dg_ref):
        x = h_ref[...]
        r = lax.rsqrt(jnp.mean(x * x, axis=-1, keepdims=True) + EPS)
        xh = x * r
        dyf = dy_ref[...].astype(F32)
        dyg = dyf * g_ref[...]
        dh_ref[...] = dres_ref[...] + r * (dyg - xh * jnp.mean(dyg * xh, axis=-1, keepdims=True))
        part = jnp.sum(dyf * xh, axis=0, keepdims=True)

        @pl.when(pl.program_id(0) == 0)
        def _():
            dg_ref[...] = part

        @pl.when(pl.program_id(0) > 0)
        def _():
            dg_ref[...] += part

    row = pl.BlockSpec((tm, D_MODEL), lambda i: (i, 0))
    vec = pl.BlockSpec((1, D_MODEL), lambda i: (0, 0))
    return pl.pallas_call(
        body, name=name, grid=(S // tm,), in_specs=[row, vec, row, row], out_specs=[row, vec],
        out_shape=[_sds((S, D_MODEL), F32), _sds((1, D_MODEL), F32)], compiler_params=_params(("arbitrary",)),
    )(h, g, dy, dres)


def _loss_head(name, h, g, target):
    S = h.shape[0]
    tm = _tile(S, ROW_TILE)

    def body(h_ref, g_ref, t_ref, dh_ref, dg_ref, ls_ref):
        x = h_ref[...]
        r = lax.rsqrt(jnp.mean(x * x, axis=-1, keepdims=True) + EPS)
        xh = x * r
        diff = xh * g_ref[...] - t_ref[...]
        dyf = diff * (1.0 / D_MODEL)
        dyg = dyf * g_ref[...]
        dh_ref[...] = r * (dyg - xh * jnp.mean(dyg * xh, axis=-1, keepdims=True))
        part = jnp.sum(dyf * xh, axis=0, keepdims=True)
        lpart = jnp.sum(diff * diff, axis=0, keepdims=True) * (0.5 / D_MODEL)

        @pl.when(pl.program_id(0) == 0)
        def _():
            dg_ref[...] = part
            ls_ref[...] = lpart

        @pl.when(pl.program_id(0) > 0)
        def _():
            dg_ref[...] += part
            ls_ref[...] += lpart

    row = pl.BlockSpec((tm, D_MODEL), lambda i: (i, 0))
    vec = pl.BlockSpec((1, D_MODEL), lambda i: (0, 0))
    return pl.pallas_call(
        body, name=name, grid=(S // tm,), in_specs=[row, vec, row], out_specs=[row, vec, vec],
        out_shape=[_sds((S, D_MODEL), F32), _sds((1, D_MODEL), F32), _sds((1, D_MODEL), F32)],
        compiler_params=_params(("arbitrary",)),
    )(h, g, target)


ATTN_SCALE = HEAD_DIM ** -0.5
ALIBI_SLOPES = [2.0 ** (-8.0 * (h + 1) / N_Q_HEADS) for h in range(N_Q_HEADS)]
K_COL = N_Q_HEADS * HEAD_DIM
KV_COLS = N_KV_HEADS * HEAD_DIM
V_COL = K_COL + KV_COLS


def _attn_masks(n):
    qi = lax.broadcasted_iota(jnp.int32, (ATTN_BLOCK, ATTN_BLOCK), 0)
    ki = lax.broadcasted_iota(jnp.int32, (ATTN_BLOCK, ATTN_BLOCK), 1)
    dist_c = (qi - ki).astype(F32)
    return dist_c + float(ATTN_BLOCK), dist_c, (ki > qi) & (n > 0), qi >= ki


def _attn_probs(q, kp, kc, sink, slope, masks):
    dist_p, dist_c, valid_p, valid_c = masks
    sp = jnp.where(valid_p, _bdot(q, kp, NT) * ATTN_SCALE - slope * dist_p, NEG_BIG)
    sc = jnp.where(valid_c, _bdot(q, kc, NT) * ATTN_SCALE - slope * dist_c, NEG_BIG)
    m = jnp.maximum(jnp.maximum(jnp.max(sp, axis=-1, keepdims=True), jnp.max(sc, axis=-1, keepdims=True)), sink)
    ep, ec, es = jnp.exp(sp - m), jnp.exp(sc - m), jnp.exp(sink - m)
    inv = 1.0 / (jnp.sum(ep, axis=-1, keepdims=True) + jnp.sum(ec, axis=-1, keepdims=True) + es)
    return ep * inv, ec * inv, es * inv


def _attn_specs(nblk):
    last = nblk - 1
    kcol, vcol = K_COL // KV_COLS, V_COL // KV_COLS
    return [
        pl.BlockSpec((ATTN_BLOCK, K_COL), lambda n: (jnp.minimum(n, last), 0)),
        pl.BlockSpec((ATTN_BLOCK, KV_COLS), lambda n: (jnp.minimum(n, last), kcol)),
        pl.BlockSpec((ATTN_BLOCK, KV_COLS), lambda n: (jnp.maximum(jnp.minimum(n, last) - 1, 0), kcol)),
        pl.BlockSpec((ATTN_BLOCK, KV_COLS), lambda n: (jnp.minimum(n, last), vcol)),
        pl.BlockSpec((ATTN_BLOCK, KV_COLS), lambda n: (jnp.maximum(jnp.minimum(n, last) - 1, 0), vcol)),
    ]


def _attn_fwd(name, proj, sinks):
    S = proj.shape[0]
    nblk = S // ATTN_BLOCK

    def body(q_ref, kc_ref, kp_ref, vc_ref, vp_ref, sk_ref, o_ref):
        masks = _attn_masks(pl.program_id(0))
        for m in range(N_KV_HEADS):
            ks = slice(HEAD_DIM * m, HEAD_DIM * (m + 1))
            kp, kc, vp, vc = kp_ref[:, ks], kc_ref[:, ks], vp_ref[:, ks], vc_ref[:, ks]
            for g in range(Q_PER_KV):
                hh = Q_PER_KV * m + g
                qs = slice(HEAD_DIM * hh, HEAD_DIM * (hh + 1))
                pp, pc, _ = _attn_probs(q_ref[:, qs], kp, kc, sk_ref[0, hh], ALIBI_SLOPES[hh], masks)
                o_ref[:, qs] = (_bdot(pp, vp, NN) + _bdot(pc, vc, NN)).astype(o_ref.dtype)

    return pl.pallas_call(
        body, name=name, grid=(nblk,),
        in_specs=_attn_specs(nblk) + [pl.BlockSpec(memory_space=pltpu.SMEM)],
        out_specs=pl.BlockSpec((ATTN_BLOCK, K_COL), lambda n: (n, 0)),
        out_shape=_sds((S, K_COL), BF16), compiler_params=_params(("parallel",)),
    )(proj, proj, proj, proj, proj, sinks)


def _attn_bwd(name, proj, sinks, do):
    S = proj.shape[0]
    nblk = S // ATTN_BLOCK

    def body(q_ref, kc_ref, kp_ref, vc_ref, vp_ref, do_ref, sk_ref, dz_ref, ds_ref, carry, cur, padd):
        n = pl.program_id(0)

        @pl.when(n == 0)
        def _():
            carry[...] = jnp.zeros_like(carry)
            ds_ref[...] = jnp.zeros_like(ds_ref)

        @pl.when(n < nblk)
        def _():
            masks = _attn_masks(n)
            lane = lax.broadcasted_iota(jnp.int32, (1, 128), 1)
            dsv = jnp.zeros((1, 128), F32)
            for m in range(N_KV_HEADS):
                ks = slice(HEAD_DIM * m, HEAD_DIM * (m + 1))
                kp, kc, vp, vc = kp_ref[:, ks], kc_ref[:, ks], vp_ref[:, ks], vc_ref[:, ks]
                dkp = dkc = dvp = dvc = jnp.zeros((ATTN_BLOCK, HEAD_DIM), F32)
                for g in range(Q_PER_KV):
                    hh = Q_PER_KV * m + g
                    qs = slice(HEAD_DIM * hh, HEAD_DIM * (hh + 1))
                    q, dout = q_ref[:, qs], do_ref[:, qs]
                    pp, pc, ps = _attn_probs(q, kp, kc, sk_ref[0, hh], ALIBI_SLOPES[hh], masks)
                    dpp, dpc = _bdot(dout, vp, NT), _bdot(dout, vc, NT)
                    delta = jnp.sum(pp * dpp, axis=-1, keepdims=True) + jnp.sum(pc * dpc, axis=-1, keepdims=True)
                    dsp, dsc = pp * (dpp - delta), pc * (dpc - delta)
                    dsv = dsv + jnp.where(lane == hh, -jnp.sum(ps * delta, axis=0, keepdims=True), 0.0)
                    cur[:, qs] = (_bdot(dsp, kp, NN) + _bdot(dsc, kc, NN)) * ATTN_SCALE
                    dkp = dkp + _bdot(dsp, q, TN) * ATTN_SCALE
                    dkc = dkc + _bdot(dsc, q, TN) * ATTN_SCALE
                    dvp = dvp + _bdot(pp, dout, TN)
                    dvc = dvc + _bdot(pc, dout, TN)
                cur[:, K_COL + HEAD_DIM * m:K_COL + HEAD_DIM * (m + 1)] = dkc
                cur[:, V_COL + HEAD_DIM * m:V_COL + HEAD_DIM * (m + 1)] = dvc
                padd[:, ks] = dkp
                padd[:, KV_COLS + HEAD_DIM * m:KV_COLS + HEAD_DIM * (m + 1)] = dvp
            ds_ref[...] += dsv
            dz_ref[:, :K_COL] = carry[:, :K_COL].astype(dz_ref.dtype)
            dz_ref[:, K_COL:] = (carry[:, K_COL:] + padd[...]).astype(dz_ref.dtype)
            carry[...] = cur[...]

        @pl.when(n == nblk)
        def _():
            dz_ref[...] = carry[...].astype(dz_ref.dtype)

    return pl.pallas_call(
        body, name=name, grid=(nblk + 1,),
        in_specs=_attn_specs(nblk) + [
            pl.BlockSpec((ATTN_BLOCK, K_COL), lambda n: (jnp.minimum(n, nblk - 1), 0)),
            pl.BlockSpec(memory_space=pltpu.SMEM)],
        out_specs=[pl.BlockSpec((ATTN_BLOCK, ATTN_IN), lambda n: (jnp.maximum(n - 1, 0), 0)),
                   pl.BlockSpec((1, 128), lambda n: (0, 0))],
        out_shape=[_sds((S, ATTN_IN), BF16), _sds((1, 128), F32)],
        scratch_shapes=[pltpu.VMEM((ATTN_BLOCK, ATTN_IN), F32), pltpu.VMEM((ATTN_BLOCK, ATTN_IN), F32),
                        pltpu.VMEM((ATTN_BLOCK, 2 * KV_COLS), F32)],
        compiler_params=_params(("arbitrary",)),
    )(proj, proj, proj, proj, proj, do, sinks)


def _hg_consts():
    C = HG_CHUNK
    tri = np.tril(np.ones((C, C)))
    t = np.arange(C)
    rows, masks = [tri], []
    for lvl in range(HG_LEVELS):
        n = C >> (lvl + 1)
        sel = np.zeros((C, C))
        sel[t, (t // (2 * n)) * (2 * n) + n - 1] = 1.0
        rows.append(sel @ tri)
        tt, ss = t[:, None], t[None, :]
        masks.append((tt // (2 * n) == ss // (2 * n)) & ((tt // n) % 2 == 1) & ((ss // n) % 2 == 0))
    masks.append(np.eye(C, dtype=bool))
    stk = np.concatenate(rows, axis=0)
    return jnp.asarray(stk, BF16), jnp.asarray(np.stack(masks), F32)


def _sigmoid(x):
    return 1.0 / (1.0 + jnp.exp(-x))


def _split3(x):
    hi = x.astype(BF16)
    r1 = x - hi.astype(F32)
    mid = r1.astype(BF16)
    return hi, mid, (r1 - mid.astype(F32)).astype(BF16)


def _dot01(m01, x, dn):
    return sum(lax.dot_general(m01, p, dn, preferred_element_type=F32) for p in _split3(x))


def _hg_common(z_ref, lb_ref, stk_ref):
    qr, fr = z_ref[0], z_ref[1]
    lb = lb_ref[...]
    sq, sg, sgn = _sigmoid(qr), _sigmoid(fr), _sigmoid(-fr)
    ft = lb + (1.0 - lb) * sg
    lf = jnp.log(ft)
    bb = _dot01(stk_ref[...], lf, NN)
    b = bb[0:HG_CHUNK]
    diffs = [b - bb[HG_CHUNK * (l + 1):HG_CHUNK * (l + 2)] for l in range(HG_LEVELS)]
    ws = [jnp.exp(-jnp.abs(d)) for d in diffs]
    blast = b[HG_CHUNK - 1:HG_CHUNK]
    return dict(qr=qr, fr=fr, lb=lb, sq=sq, sg=sg, sgn=sgn, ft=ft, q=qr * sq, kk=(1.0 - lb) * sgn, b=b, diffs=diffs,
                ws=ws, eb=jnp.exp(b), ed=jnp.exp(blast - b), elast=jnp.exp(blast))


def _hg_intra(qh, kh, ws, msk_ref, sl):
    a = msk_ref[HG_LEVELS] * _bdot(qh, kh, NT)
    qls, kls = [], []
    for l in range(HG_LEVELS):
        w = ws[l][:, sl]
        qls.append((qh * w).astype(BF16))
        kls.append((kh * w).astype(BF16))
        a = a + msk_ref[l] * _bdot(qls[l], kls[l], NT)
    return a, qls, kls


def _hg_fwd(name, z, lb, ng):
    S = z.shape[2]
    nc = S // HG_CHUNK
    stk, msk = _hg_consts()

    def body(z_ref, lb_ref, ng_ref, stk_ref, msk_ref, og_ref, st_ref, state):
        @pl.when(pl.program_id(1) == 0)
        def _():
            state[...] = jnp.zeros_like(state)

        cm = _hg_common(z_ref, lb_ref, stk_ref)
        v, gt = z_ref[2], z_ref[3]
        kd = cm["kk"] * cm["ed"]
        for hh in range(4):
            sl = slice(HG_K * hh, HG_K * (hh + 1))
            st = state[hh]
            st_ref[hh] = st
            qh, kh, vh = cm["q"][:, sl], cm["kk"][:, sl], v[:, sl]
            a, _, _ = _hg_intra(qh, kh, cm["ws"], msk_ref, sl)
            o = _bdot(a, vh, NN) + _bdot(qh * cm["eb"][:, sl], st, NT)
            state[hh] = cm["elast"][:, sl] * st + _bdot(vh, kd[:, sl], TN)
            r = lax.rsqrt(jnp.mean(o * o, axis=-1, keepdims=True) + EPS)
            gh = gt[:, sl]
            og_ref[:, sl] = (o * r * ng_ref[...] * (gh * _sigmoid(gh))).astype(og_ref.dtype)

    return pl.pallas_call(
        body, name=name, grid=(2, nc),
        in_specs=[pl.BlockSpec((4, None, HG_CHUNK, HG_SLOT), lambda g, c: (0, g, c, 0)),
                  pl.BlockSpec((1, HG_SLOT), lambda g, c: (0, g)),
                  pl.BlockSpec((1, HG_K), lambda g, c: (0, 0)),
                  pl.BlockSpec(stk.shape, lambda g, c: (0, 0)),
                  pl.BlockSpec(msk.shape, lambda g, c: (0, 0, 0))],
        out_specs=[pl.BlockSpec((HG_CHUNK, HG_SLOT), lambda g, c: (c, g)),
                   pl.BlockSpec((None, 4, HG_K, HG_K), lambda g, c: (c, g, 0, 0))],
        out_shape=[_sds((S, D_MODEL), BF16), _sds((nc, HG_HEADS, HG_K, HG_K), F32)],
        scratch_shapes=[pltpu.VMEM((4, HG_K, HG_K), F32)],
        compiler_params=_params(("parallel", "arbitrary")),
    )(z, lb, ng, stk, msk)


def _hg_bwd(name, z, lb, ng, states, dog):
    S = z.shape[2]
    nc = S // HG_CHUNK
    stk, msk = _hg_consts()

    def body(z_ref, lb_ref, ng_ref, stk_ref, msk_ref, st_ref, dog_ref, dz_ref, dlb_ref, dng_ref, dstate):
        @pl.when(pl.program_id(1) == 0)
        def _():
            dstate[...] = jnp.zeros_like(dstate)
            dlb_ref[...] = jnp.zeros_like(dlb_ref)
            dng_ref[...] = jnp.zeros_like(dng_ref)

        cm = _hg_common(z_ref, lb_ref, stk_ref)
        v, gt = z_ref[2], z_ref[3]
        ng = ng_ref[...]
        kd = cm["kk"] * cm["ed"]
        row = lax.broadcasted_iota(jnp.int32, (HG_CHUNK, 1), 0)
        dng = jnp.zeros((1, HG_K), F32)
        dq_h, dkk_h, db_h, dv_h, dgt_h = [], [], [], [], []
        dr_h = [[] for _ in range(HG_LEVELS)]
        for hh in range(4):
            sl = slice(HG_K * hh, HG_K * (hh + 1))
            st, dst = st_ref[hh], dstate[hh]
            qh, kh, vh, ebh, edh, kdh = cm["q"][:, sl], cm["kk"][:, sl], v[:, sl], cm["eb"][:, sl], cm["ed"][:, sl], kd[:, sl]
            elh = cm["elast"][:, sl]
            a, qls, kls = _hg_intra(qh, kh, cm["ws"], msk_ref, sl)
            qe = qh * ebh
            o = _bdot(a, vh, NN) + _bdot(qe, st, NT)
            r = lax.rsqrt(jnp.mean(o * o, axis=-1, keepdims=True) + EPS)
            xh = o * r
            gh = gt[:, sl]
            sgg = _sigmoid(gh)
            dog = dog_ref[:, sl].astype(F32)
            dy = dog * (gh * sgg)
            dgt_h.append(dog * (xh * ng) * (sgg * (1.0 + gh * (1.0 - sgg))))
            dng = dng + jnp.sum(dy * xh, axis=0, keepdims=True)
            dyg = dy * ng
            do = r * (dyg - xh * jnp.mean(dyg * xh, axis=-1, keepdims=True))
            da = _bdot(do, vh, NT)
            dv_h.append(_bdot(a, do, TN) + _bdot(kdh, dst, NT))
            dkd = _bdot(vh, dst, NN)
            delast = jnp.sum(st * dst, axis=0, keepdims=True)
            dqe = _bdot(do, st, NN)
            dstate[hh] = elh * dst + _bdot(do, qe, TN)
            gk = dkd * kdh
            dblast = jnp.sum(gk, axis=0, keepdims=True) + delast * elh
            db = dqe * qe - gk + jnp.where(row == HG_CHUNK - 1, dblast, 0.0)
            dp = (msk_ref[HG_LEVELS] * da).astype(BF16)
            dq = dqe * ebh + _bdot(dp, kh, NN)
            dkk = dkd * edh + _bdot(dp, qh, TN)
            for l in range(HG_LEVELS):
                dp = (msk_ref[l] * da).astype(BF16)
                dql, dkl = _bdot(dp, kls[l], NN), _bdot(dp, qls[l], TN)
                w = cm["ws"][l][:, sl]
                dq = dq + dql * w
                dkk = dkk + dkl * w
                half = jnp.where(((row >> (HG_LEVELS - 1 - l)) & 1) == 1, 1.0, -1.0)
                dd = half * w * (dql * qh + dkl * kh)
                db = db + dd
                dr_h[l].append(-dd)
            dq_h.append(dq)
            dkk_h.append(dkk)
            db_h.append(db)
        cat = lambda xs: jnp.concatenate(xs, axis=1)
        cot = jnp.concatenate([cat(db_h)] + [cat(dr_h[l]) for l in range(HG_LEVELS)], axis=0)
        dlf = _dot01(stk_ref[...], cot, TN)
        dq, dkk = cat(dq_h), cat(dkk_h)
        dft = dlf / cm["ft"]
        one_lb = 1.0 - cm["lb"]
        dz_ref[0] = dq * (cm["sq"] * (1.0 + cm["qr"] * (1.0 - cm["sq"])))
        dz_ref[1] = (dft - dkk) * one_lb * cm["sg"] * cm["sgn"]
        dz_ref[2] = cat(dv_h)
        dz_ref[3] = cat(dgt_h)
        dlb_ref[...] += jnp.sum((dft - dkk) * cm["sgn"], axis=0, keepdims=True)
        dng_ref[...] += dng

    rev = lambda c: nc - 1 - c
    return pl.pallas_call(
        body, name=name, grid=(2, nc),
        in_specs=[pl.BlockSpec((4, None, HG_CHUNK, HG_SLOT), lambda g, c: (0, g, rev(c), 0)),
                  pl.BlockSpec((1, HG_SLOT), lambda g, c: (0, g)),
                  pl.BlockSpec((1, HG_K), lambda g, c: (0, 0)),
                  pl.BlockSpec(stk.shape, lambda g, c: (0, 0)),
                  pl.BlockSpec(msk.shape, lambda g, c: (0, 0, 0)),
                  pl.BlockSpec((None, 4, HG_K, HG_K), lambda g, c: (rev(c), g, 0, 0)),
                  pl.BlockSpec((HG_CHUNK, HG_SLOT), lambda g, c: (rev(c), g))],
        out_specs=[pl.BlockSpec((4, None, HG_CHUNK, HG_SLOT), lambda g, c: (0, g, rev(c), 0)),
                   pl.BlockSpec((1, HG_SLOT), lambda g, c: (0, g)),
                   pl.BlockSpec((None, 1, HG_K), lambda g, c: (g, 0, 0))],
        out_shape=[_sds(z.shape, F32), _sds((1, 2 * HG_SLOT), F32), _sds((2, 1, HG_K), F32)],
        scratch_shapes=[pltpu.VMEM((4, HG_K, HG_K), F32)],
        compiler_params=_params(("parallel", "arbitrary")),
    )(z, lb, ng, stk, msk, states, dog)


def _lb_fwd(name, logits):
    def body(l_ref, o_ref):
        x = l_ref[...]
        e = jnp.exp(x - jnp.max(x, axis=0, keepdims=True))
        s = e / jnp.sum(e, axis=0, keepdims=True)
        o_ref[0:1, :] = s[1:2]
        o_ref[1:2, :] = s[1:2] + s[2:3] + s[3:4]

    return pl.pallas_call(body, name=name, out_shape=_sds((2, logits.shape[1]), F32))(logits)


def _lb_bwd(name, logits, dlb):
    def body(l_ref, d_ref, o_ref):
        x = l_ref[...]
        e = jnp.exp(x - jnp.max(x, axis=0, keepdims=True))
        s = e / jnp.sum(e, axis=0, keepdims=True)
        d1, d3 = d_ref[0:1, :], d_ref[1:2, :]
        ds = [jnp.zeros_like(d1), d1 + d3, d3, d3]
        dot = sum(ds[r] * s[r:r + 1] for r in range(1, DEPTH))
        for r in range(DEPTH):
            o_ref[r:r + 1, :] = s[r:r + 1] * (ds[r] - dot)

    return pl.pallas_call(body, name=name, out_shape=_sds(logits.shape, F32))(logits, dlb)


SUB = 8


def _rows_down(x, prev, k):
    row = lax.broadcasted_iota(jnp.int32, x.shape, 0)
    return jnp.where(row >= k, pltpu.roll(x, k, 0), pltpu.roll(prev, k, 0))


def _rows_up(x, nxt, k):
    row = lax.broadcasted_iota(jnp.int32, x.shape, 0)
    return jnp.where(row < SUB - k, pltpu.roll(x, SUB - k, 0), pltpu.roll(nxt, SUB - k, 0))


def _conv_block(w_ref, b_ref, p, x, prev):
    x0, x1 = _rows_down(x, prev, 2), _rows_down(x, prev, 1)
    return b_ref[p] + w_ref[p, 0:1, :] * x0 + w_ref[p, 1:2, :] * x1 + w_ref[p, 2:3, :] * x, x0, x1


def _convgate_fwd(name, u, cw, cb):
    S = u.shape[2]
    tm = _tile(S, ROW_TILE)

    def body(u_ref, w_ref, b_ref, a_ref, halo):
        @pl.when(pl.program_id(1) == 0)
        def _():
            halo[...] = jnp.zeros_like(halo)

        def step(r, prev):
            pg, pv = prev
            out = []
            for s in range(2):
                rows = pl.ds(pl.multiple_of(r * 2 * SUB + s * SUB, SUB), SUB)
                xg, xv = u_ref[0, rows, :], u_ref[1, rows, :]
                cg = _conv_block(w_ref, b_ref, 0, xg, pg)[0]
                cv = _conv_block(w_ref, b_ref, 1, xv, pv)[0]
                out.append(cg * _sigmoid(cg) * cv)
                pg, pv = xg, xv
            a_ref[pl.ds(pl.multiple_of(r * 2 * SUB, 2 * SUB), 2 * SUB), :] = jnp.concatenate(out, axis=0).astype(a_ref.dtype)
            return pg, pv

        pg, pv = lax.fori_loop(0, tm // (2 * SUB), step, (halo[0], halo[1]))
        halo[0] = pg
        halo[1] = pv

    return pl.pallas_call(
        body, name=name, grid=(4, S // tm),
        in_specs=[pl.BlockSpec((2, None, tm, FF_SLOT), lambda j, t: (0, j, t, 0)),
                  pl.BlockSpec((2, None, 3, FF_SLOT), lambda j, t: (0, j, 0, 0)),
                  pl.BlockSpec((2, None, 1, FF_SLOT), lambda j, t: (0, j, 0, 0))],
        out_specs=pl.BlockSpec((None, tm, FF_SLOT), lambda j, t: (j, t, 0)),
        out_shape=_sds((4, S, FF_SLOT), BF16),
        scratch_shapes=[pltpu.VMEM((2, SUB, FF_SLOT), F32)],
        compiler_params=_params(("parallel", "arbitrary")),
    )(u, cw, cb)


def _convgate_bwd(name, u, cw, cb, da):
    S = u.shape[2]
    tm = _tile(S, ROW_TILE)
    nt = S // tm

    def body(u_ref, uh_ref, w_ref, b_ref, da_ref, du_ref, dw_ref, db_ref, after, first, acc):
        t = pl.program_id(1)

        @pl.when(t == 0)
        def _():
            after[...] = jnp.zeros_like(after)
            acc[...] = jnp.zeros_like(acc)

        def du_block(p, d, nxt):
            return w_ref[p, 2:3, :] * d + w_ref[p, 1:2, :] * _rows_up(d, nxt, 1) + w_ref[p, 0:1, :] * _rows_up(d, nxt, 2)

        def step(r, carry):
            pg, pv, dg_last, dv_last = carry
            dav = da_ref[pl.ds(pl.multiple_of(r * 2 * SUB, 2 * SUB), 2 * SUB), :].astype(F32)
            for s in range(2):
                at = r * 2 * SUB + s * SUB
                rows = pl.ds(pl.multiple_of(at, SUB), SUB)
                xg, xv = u_ref[0, rows, :], u_ref[1, rows, :]
                cg, x0g, x1g = _conv_block(w_ref, b_ref, 0, xg, pg)
                cv, x0v, x1v = _conv_block(w_ref, b_ref, 1, xv, pv)
                sg = _sigmoid(cg)
                dab = dav[s * SUB:(s + 1) * SUB]
                dg = dab * cv * (sg * (1.0 + cg * (1.0 - sg)))
                dv = dab * cg * sg
                for p, d, taps in ((0, dg, (x0g, x1g, xg)), (1, dv, (x0v, x1v, xv))):
                    for j in range(3):
                        acc[p, j] += d * taps[j]
                    acc[p, 3] += d
                if s == 0:
                    @pl.when(r == 0)
                    def _():
                        first[0] = dg
                        first[1] = dv

                    @pl.when(r > 0)
                    def _():
                        before = pl.ds(pl.multiple_of(at - SUB, SUB), SUB)
                        du_ref[0, before, :] = du_block(0, dg_last, dg)
                        du_ref[1, before, :] = du_block(1, dv_last, dv)
                else:
                    before = pl.ds(pl.multiple_of(at - SUB, SUB), SUB)
                    du_ref[0, before, :] = du_block(0, dg_last, dg)
                    du_ref[1, before, :] = du_block(1, dv_last, dv)
                pg, pv, dg_last, dv_last = xg, xv, dg, dv
            return pg, pv, dg_last, dv_last

        halo = jnp.where(t < nt - 1, uh_ref[...], 0.0)
        zero = jnp.zeros((SUB, FF_SLOT), F32)
        _, _, dg_last, dv_last = lax.fori_loop(0, tm // (2 * SUB), step, (halo[0], halo[1], zero, zero))
        du_ref[0, tm - SUB:tm, :] = du_block(0, dg_last, after[0])
        du_ref[1, tm - SUB:tm, :] = du_block(1, dv_last, after[1])
        after[...] = first[...]
        for p in range(2):
            for j in range(3):
                dw_ref[p, j:j + 1, :] = jnp.sum(acc[p, j], axis=0, keepdims=True)
            db_ref[p] = jnp.sum(acc[p, 3], axis=0, keepdims=True)

    rev = lambda t: nt - 1 - t
    return pl.pallas_call(
        body, name=name, grid=(4, nt),
        in_specs=[pl.BlockSpec((2, None, tm, FF_SLOT), lambda j, t: (0, j, rev(t), 0)),
                  pl.BlockSpec((2, None, SUB, FF_SLOT), lambda j, t: (0, j, jnp.maximum(rev(t) * (tm // SUB) - 1, 0), 0)),
                  pl.BlockSpec((2, None, 3, FF_SLOT), lambda j, t: (0, j, 0, 0)),
                  pl.BlockSpec((2, None, 1, FF_SLOT), lambda j, t: (0, j, 0, 0)),
                  pl.BlockSpec((None, tm, FF_SLOT), lambda j, t: (j, rev(t), 0))],
        out_specs=[pl.BlockSpec((2, None, tm, FF_SLOT), lambda j, t: (0, j, rev(t), 0)),
                   pl.BlockSpec((2, None, 3, FF_SLOT), lambda j, t: (0, j, 0, 0)),
                   pl.BlockSpec((2, None, 1, FF_SLOT), lambda j, t: (0, j, 0, 0))],
        out_shape=[_sds(u.shape, F32), _sds(cw.shape, F32), _sds(cb.shape, F32)],
        scratch_shapes=[pltpu.VMEM((2, SUB, FF_SLOT), F32), pltpu.VMEM((2, SUB, FF_SLOT), F32),
                        pltpu.VMEM((2, 4, SUB, FF_SLOT), F32)],
        compiler_params=_params(("parallel", "arbitrary")),
    )(u, u, cw, cb, da)


def _row_tile(R):
    for t in range(256, 15, -16):
        if R % t == 0:
            return t
    return R


def _sum_sources(name, gsrc):
    n, R, C = gsrc.shape
    tr = _row_tile(R)

    def body(g_ref, o_ref):
        g = g_ref[0].astype(F32)
        for s in range(1, n):
            g = g + g_ref[s].astype(F32)
        o_ref[...] = g

    return pl.pallas_call(
        body, name=name, grid=(R // tr,), in_specs=[pl.BlockSpec((n, tr, C), lambda i: (0, i, 0))],
        out_specs=pl.BlockSpec((tr, C), lambda i: (i, 0)), out_shape=_sds((R, C), F32),
        compiler_params=_params(("parallel",)),
    )(gsrc)


def _adamw(name, gsrcs, w, m, v):
    L = len(gsrcs)
    n, A, C = gsrcs[0].shape
    tr = _row_tile(A)

    def body(*refs):
        g_refs = refs[:L]
        w_ref, m_ref, v_ref, go_ref, d_ref, mo_ref, vo_ref = refs[L:]
        for k in range(L):
            @pl.when(pl.program_id(0) == k)
            def _(k=k):
                g = g_refs[k][0].astype(F32)
                for s in range(1, n):
                    g = g + g_refs[k][s].astype(F32)
                m2 = ADAM_B1 * m_ref[...] + (1.0 - ADAM_B1) * g
                v2 = ADAM_B2 * v_ref[...] + (1.0 - ADAM_B2) * (g * g)
                m_hat = m2 / (1.0 - ADAM_B1 ** ADAM_STEP)
                v_hat = v2 / (1.0 - ADAM_B2 ** ADAM_STEP)
                go_ref[...] = g
                d_ref[...] = -ADAM_LR * (m_hat / (jnp.sqrt(v_hat) + ADAM_EPS) + ADAM_WD * w_ref[...])
                mo_ref[...] = m2
                vo_ref[...] = v2

    g_specs = [pl.BlockSpec((n, tr, C), lambda l, i, k=k: (0, jnp.where(l == k, i, 0), 0)) for k in range(L)]
    blk = pl.BlockSpec((None, tr, C), lambda l, i: (l, i, 0))
    return pl.pallas_call(
        body, name=name, grid=(L, A // tr), in_specs=g_specs + [blk, blk, blk],
        out_specs=[blk] * 4, out_shape=[_sds((L, A, C), F32)] * 4, compiler_params=_params(("parallel", "parallel")),
    )(*gsrcs, w, m, v)


MESH = pl.DeviceIdType.MESH
HBM_SPEC = pl.BlockSpec(memory_space=pltpu.HBM)
N_PEERS = N_DEV - 1


def _mesh_place():
    x, y, c = lax.axis_index("x"), lax.axis_index("y"), lax.axis_index("c")
    peers = []
    for p in range(1, N_DEV):
        px = 1 - x if p & 4 else x
        py = 1 - y if p & 2 else y
        pc = 1 - c if p & 1 else c
        peers.append(((px, py, pc), 4 * px + 2 * py + pc))
    return 4 * x + 2 * y + c, peers


SEM_SPEC = pl.BlockSpec(memory_space=pltpu.SEMAPHORE)
ANY_SPEC = pl.BlockSpec(memory_space=pl.ANY)
EFFECT = pltpu.SideEffectType.DATAFLOW_SIDE_EFFECTING


def _exchange_refs(scatter, src, land, send, recv, k, p, dev, idx, me):
    return pltpu.make_async_remote_copy(src_ref=src[k].at[idx] if scatter else src[k], dst_ref=land[k].at[me],
                                        send_sem=send.at[k * N_PEERS + p], recv_sem=recv.at[k * N_PEERS + p], device_id=dev,
                                        device_id_type=MESH)


def _exchange_start(name, srcs, scatter, gate):
    n = len(srcs)
    lands = [lax.empty(s.shape if scatter else (N_DEV,) + s.shape, s.dtype) for s in srcs]

    def body(*refs):
        src, land = refs[:n], refs[n:2 * n]
        send, recv, own = refs[2 * n + 1:2 * n + 4]
        token = refs[-1]
        me, peers = _mesh_place()
        for k in range(n):
            pltpu.make_async_copy(src[k].at[me] if scatter else src[k], land[k].at[me], own.at[k]).start()
            for p, (dev, idx) in enumerate(peers):
                _exchange_refs(scatter, src, land, send, recv, k, p, dev, idx, me).start()
        token[...] = jnp.zeros_like(token)

    hbm = lambda a: pltpu.HBM(a.shape, a.dtype)
    outs = pl.pallas_call(
        body, name=name,
        out_shape=(pltpu.SemaphoreType.DMA((n * N_PEERS,)), pltpu.SemaphoreType.DMA((n * N_PEERS,)),
                   pltpu.SemaphoreType.DMA((n,)), *[hbm(s) for s in srcs], *[hbm(s) for s in lands], _sds(DEP_SHAPE, F32)),
        in_specs=[HBM_SPEC] * (2 * n) + [ANY_SPEC],
        out_specs=(SEM_SPEC, SEM_SPEC, SEM_SPEC, *[HBM_SPEC] * (2 * n), pl.BlockSpec(memory_space=pltpu.VMEM)),
        input_output_aliases={j: 3 + j for j in range(2 * n)},
        compiler_params=pltpu.CompilerParams(has_side_effects=EFFECT),
    )(*[pltpu.with_memory_space_constraint(s, pltpu.HBM) for s in srcs],
      *[pltpu.with_memory_space_constraint(s, pltpu.HBM) for s in lands], gate)
    return outs[:3], None, list(outs[3:3 + n]), list(outs[3 + n:3 + 2 * n]), outs[-1]


def _exchange_wait(name, started, scatter, after):
    (send, recv, own), _, srcs, lands, _ = started
    n = len(srcs)

    def body(*refs):
        src, land = refs[:n], refs[n:2 * n]
        send, recv, own = refs[2 * n:2 * n + 3]
        me, peers = _mesh_place()
        for k in range(n):
            pltpu.make_async_copy(src[k].at[me] if scatter else src[k], land[k].at[me], own.at[k]).wait()
            for p, (dev, idx) in enumerate(peers):
                cp = pltpu.make_async_remote_copy(src_ref=src[k].at[idx] if scatter else src[k], dst_ref=land[k].at[idx],
                                                  send_sem=send.at[k * N_PEERS + p], recv_sem=recv.at[k * N_PEERS + p], device_id=dev,
                                                  device_id_type=MESH)
                cp.wait_send()
                cp.wait_recv()

    hbm = lambda a: pltpu.HBM(a.shape, a.dtype)
    outs = pl.pallas_call(
        body, name=name, out_shape=(*[hbm(s) for s in srcs], *[hbm(s) for s in lands]),
        in_specs=[HBM_SPEC] * (2 * n) + [SEM_SPEC, SEM_SPEC, SEM_SPEC, ANY_SPEC], out_specs=tuple([HBM_SPEC] * (2 * n)),
        input_output_aliases={j: j for j in range(2 * n)},
        compiler_params=pltpu.CompilerParams(has_side_effects=EFFECT),
    )(*srcs, *lands, send, recv, own, after)
    return list(outs[n:])


def _allsum_rows(name, part):
    R, C = part.shape

    def body(p_ref, o_ref, gath, send, recv):
        me, peers = _mesh_place()
        gath[me] = p_ref[...]
        sends = []
        for p, (dev, _) in enumerate(peers):
            cp = pltpu.make_async_remote_copy(src_ref=p_ref, dst_ref=gath.at[me], send_sem=send.at[p], recv_sem=recv.at[p],
                                              device_id=dev, device_id_type=MESH)
            cp.start()
            sends.append(cp)
        for p, (dev, idx) in enumerate(peers):
            pltpu.make_async_remote_copy(src_ref=p_ref, dst_ref=gath.at[idx], send_sem=send.at[p], recv_sem=recv.at[p],
                                         device_id=dev, device_id_type=MESH).wait_recv()
        for cp in sends:
            cp.wait_send()
        tot = gath[0]
        for j in range(1, N_DEV):
            tot = tot + gath[j]
        o_ref[...] = tot

    vm = pl.BlockSpec(memory_space=pltpu.VMEM)
    return pl.pallas_call(
        body, name=name, in_specs=[vm], out_specs=vm, out_shape=_sds((R, C), F32),
        scratch_shapes=[pltpu.VMEM((N_DEV, R, C), F32), pltpu.SemaphoreType.DMA((N_PEERS,)),
                        pltpu.SemaphoreType.DMA((N_PEERS,))],
        compiler_params=pltpu.CompilerParams(vmem_limit_bytes=VMEM_LIMIT),
    )(part)


def _rows(a, width=D_MODEL):
    flat = a.reshape(-1)
    return jnp.pad(flat, (0, (-flat.shape[0]) % width)).reshape(-1, width)


def _pack_rows(parts):
    blocks = []
    for p in parts:
        r = _rows(p)
        blocks.append(jnp.pad(r, ((0, (-r.shape[0]) % 8), (0, 0))))
    return jnp.concatenate(blocks, axis=0)


def _unpack_rows(rows, shapes):
    out, at = [], 0
    for s in shapes:
        size = int(np.prod(s))
        n = -(-size // D_MODEL)
        out.append(rows[at:at + n].reshape(-1)[:size].reshape(s))
        at += -(-n // 8) * 8
    return out


def kernel(x, norm_mix, norm_ffn, norm_final, attn_w_in, attn_w_out, attn_sinks, hgrn_w_in, hgrn_w_out, hgrn_norm, hgrn_lb_logits, ffn_w_up, ffn_conv_w, ffn_conv_b, ffn_w_down, loss_target, m_norm_mix, m_norm_ffn, m_norm_final, m_attn_w_in, m_attn_w_out, m_attn_sinks, m_hgrn_w_in, m_hgrn_w_out, m_hgrn_norm, m_hgrn_lb_logits, m_ffn_w_up, m_ffn_conv_w, m_ffn_conv_b, m_ffn_w_down, v_norm_mix, v_norm_ffn, v_norm_final, v_attn_w_in, v_attn_w_out, v_attn_sinks, v_hgrn_w_in, v_hgrn_w_out, v_hgrn_norm, v_hgrn_lb_logits, v_ffn_w_up, v_ffn_conv_w, v_ffn_conv_b, v_ffn_w_down):
    S = x.shape[1]
    n_attn, n_hgrn = attn_w_in.shape[0], hgrn_w_in.shape[0]
    me = 4 * lax.axis_index("x") + 2 * lax.axis_index("y") + lax.axis_index("c")

    wa_in_t, wa_out_b = attn_w_in.transpose(0, 2, 1).astype(BF16), attn_w_out.astype(BF16)
    wh_in_b, wh_out_b = hgrn_w_in.astype(BF16), hgrn_w_out.astype(BF16)
    wf_up_b, wf_down_b = ffn_w_up.astype(BF16), ffn_w_down.astype(BF16)
    conv_b = ffn_conv_b.reshape(DEPTH, 2, 4, 1, FF_SLOT)
    lb = _lb_fwd("lb_fwd", hgrn_lb_logits)

    def unit_shards(l, part):
        if part == "ffn":
            return [wf_up_b[l], wf_down_b[l], ffn_conv_w[l]]
        return [wa_in_t[l // 2], wa_out_b[l // 2]] if l % 2 == 0 else [wh_in_b[l // 2], wh_out_b[l // 2]]

    def unit_weights(l, part, w):
        if part == "ffn":
            return w[0][None], w[1].reshape(1, 4, FF_SLOT, D_MODEL), w[2].reshape(2, 4, 3, FF_SLOT)
        if l % 2 == 0:
            return w[0].reshape(1, ATTN_IN, D_MODEL), w[1].reshape(1, D_MODEL, D_MODEL)
        return w[0][None], w[1].reshape(1, D_MODEL, D_MODEL)

    units = [(l, part) for l in range(DEPTH) for part in ("mix", "ffn")]
    started = _exchange_start("gather_start0", unit_shards(*units[0]), False, norm_final)
    arrived = _exchange_wait("gather_wait0", started, False, started[4])
    weights, saved = {}, [dict() for _ in range(DEPTH)]
    h = x[0]
    for n, (l, part) in enumerate(units):
        i, sv = l // 2, saved[l]
        weights[l, part] = w = unit_weights(l, part, arrived)
        started = dep = None
        if n + 1 < len(units):
            started = _exchange_start(f"gather_start{n + 1}", unit_shards(*units[n + 1]), False, arrived[0])
            dep = started[4]
        if part == "mix":
            sv["h"] = h
            sv["hn"] = hn = _rmsnorm_fwd(f"norm_mix_fwd{l}", h, norm_mix[l:l + 1], dep)
            if l % 2 == 0:
                sv["proj"] = _proj_rows(f"attn_proj{i}", hn, w[0], 0, BF16)
                sv["o"] = _attn_fwd(f"attn_fwd{i}", sv["proj"], attn_sinks[i:i + 1])
                h = _out_proj(f"attn_out{i}", sv["o"], w[1], 0, h)
            else:
                sv["z"] = _proj_slots(f"hgrn_proj{i}", hn, w[0], 0).reshape(4, 2, S, HG_SLOT)
                sv["o"], sv["states"] = _hg_fwd(f"hgrn_fwd{i}", sv["z"], lb[i:i + 1], hgrn_norm[i:i + 1])
                h = _out_proj(f"hgrn_out{i}", sv["o"], w[1], 0, h)
        else:
            sv["h2"] = h
            sv["hn2"] = _rmsnorm_fwd(f"norm_ffn_fwd{l}", h, norm_ffn[l:l + 1], dep)
            sv["u"] = _proj_slots(f"ffn_up{l}", sv["hn2"], w[0], 0).reshape(2, 4, S, FF_SLOT)
            sv["a"] = _convgate_fwd(f"ffn_gate{l}", sv["u"], w[2], conv_b[l])
            h = _down_proj(f"ffn_down{l}", sv["a"], w[1], 0, h)
        if started is not None:
            arrived = _exchange_wait(f"gather_wait{n + 1}", started, False, h)
    dh, d_norm_final, loss_rows = _loss_head("loss_head", h, norm_final[None], loss_target[0])

    d_conv_w, d_conv_b, d_norm_mix, d_norm_ffn = [None] * DEPTH, [None] * DEPTH, [None] * DEPTH, [None] * DEPTH
    d_sinks, d_lb, d_hgrn_norm = [None] * n_attn, [None] * n_hgrn, [None] * n_hgrn
    received, started, before = {}, None, None
    for l, part in reversed(units):
        i, sv, w = l // 2, saved[l], weights[l, part]
        dep = None if started is None else started[4]
        if part == "ffn":
            da = _dgrad_down(f"ffn_down_dgrad{l}", dh, w[1], 0, dep)
            g_down = _wgrad_down(f"ffn_down_wgrad{l}", sv["a"], dh).reshape(N_DEV, D_FF // N_DEV, D_MODEL)
            du, d_conv_w[l], d_conv_b[l] = _convgate_bwd(f"ffn_gate_bwd{l}", sv["u"], w[2], conv_b[l], da)
            du = du.reshape(N_DEV, S, FF_SLOT)
            dhn2 = _dgrad_slots(f"ffn_up_dgrad{l}", du, w[0], 0)
            grads = [_wgrad_slots(f"ffn_up_wgrad{l}", sv["hn2"], du), g_down]
            dh, d_norm_ffn[l] = _rmsnorm_bwd(f"norm_ffn_bwd{l}", sv["h2"], norm_ffn[l:l + 1], dhn2, dh)
        else:
            if l % 2 == 0:
                do = _dgrad_out(f"attn_out_dgrad{i}", dh, w[1], 0, BF16, dep)
                g_out = _wgrad_rows(f"attn_out_wgrad{i}", sv["o"], dh)
                dproj, d_sinks[i] = _attn_bwd(f"attn_bwd{i}", sv["proj"], attn_sinks[i:i + 1], do)
                dhn = _dgrad_rows(f"attn_proj_dgrad{i}", dproj, w[0], 0)
                g_in = _wgrad_rows(f"attn_proj_wgrad{i}", dproj, sv["hn"]).reshape(N_DEV, ATTN_IN // N_DEV, D_MODEL)
            else:
                dog = _dgrad_out(f"hgrn_out_dgrad{i}", dh, w[1], 0, F32, dep)
                g_out = _wgrad_rows(f"hgrn_out_wgrad{i}", sv["o"], dh)
                dz, d_lb[i], dng = _hg_bwd(f"hgrn_bwd{i}", sv["z"], lb[i:i + 1], hgrn_norm[i:i + 1], sv["states"], dog)
                d_hgrn_norm[i] = dng[0] + dng[1]
                dz = dz.reshape(N_DEV, S, HG_SLOT)
                dhn = _dgrad_slots(f"hgrn_proj_dgrad{i}", dz, w[0], 0)
                g_in = _wgrad_slots(f"hgrn_proj_wgrad{i}", sv["hn"], dz)
            grads = [g_in, g_out.reshape(N_DEV, D_MODEL // N_DEV, D_MODEL)]
            dh, d_norm_mix[l] = _rmsnorm_bwd(f"norm_mix_bwd{l}", sv["h"], norm_mix[l:l + 1], dhn, dh)
        gate = dh
        if started is not None:
            received[before] = _exchange_wait(f"scatter_wait_{before[1]}{before[0]}", started, True, dh)
            gate = received[before][0]
        started, before = _exchange_start(f"scatter_start_{part}{l}", grads, True, gate), (l, part)
    received[before] = _exchange_wait(f"scatter_wait_{before[1]}{before[0]}", started, True, started[4])
    grad_x = dh[None]

    small_shapes = [(DEPTH, D_MODEL), (DEPTH, D_MODEL), (1, D_MODEL), (1, D_MODEL), (n_hgrn, D_MODEL), (n_attn, 128),
                    (n_hgrn, HG_K), (DEPTH, 2 * D_FF), (DEPTH, N_DEV, 3, FF_SLOT)]
    total = _allsum_rows("allsum_small", _pack_rows([
        jnp.concatenate(d_norm_mix), jnp.concatenate(d_norm_ffn), d_norm_final, loss_rows, jnp.concatenate(d_lb),
        jnp.concatenate(d_sinks), jnp.concatenate(d_hgrn_norm), jnp.stack(d_conv_b), jnp.stack(d_conv_w)]))
    (g_norm_mix, g_norm_ffn, g_norm_final, loss_sum, g_lb, g_sinks, g_hgrn_norm, g_conv_b, g_conv_w_all) = _unpack_rows(
        total, small_shapes)
    loss = jnp.sum(loss_sum)
    g_norm_final = g_norm_final[0]
    g_sinks = g_sinks[:, :N_Q_HEADS]
    g_lb_logits = _lb_bwd("lb_bwd", hgrn_lb_logits, g_lb)
    g_conv_w = lax.dynamic_index_in_dim(g_conv_w_all, me, axis=1, keepdims=False)

    attn_layers, hgrn_layers = range(0, DEPTH, 2), range(1, DEPTH, 2)
    g_attn_in_t = [_sum_sources(f"sum_attn_in{l // 2}", received[l, "mix"][0]).T[None] for l in attn_layers]
    big = {
        "attn_w_in": _adamw("adamw_attn_in", g_attn_in_t, attn_w_in, m_attn_w_in, v_attn_w_in),
        "attn_w_out": _adamw("adamw_attn_out", [received[l, "mix"][1] for l in attn_layers], attn_w_out, m_attn_w_out, v_attn_w_out),
        "hgrn_w_in": _adamw("adamw_hgrn_in", [received[l, "mix"][0] for l in hgrn_layers], hgrn_w_in, m_hgrn_w_in, v_hgrn_w_in),
        "hgrn_w_out": _adamw("adamw_hgrn_out", [received[l, "mix"][1] for l in hgrn_layers], hgrn_w_out, m_hgrn_w_out, v_hgrn_w_out),
        "ffn_w_up": _adamw("adamw_ffn_up", [received[l, "ffn"][0] for l in range(DEPTH)], ffn_w_up, m_ffn_w_up, v_ffn_w_up),
        "ffn_w_down": _adamw("adamw_ffn_down", [received[l, "ffn"][1] for l in range(DEPTH)], ffn_w_down, m_ffn_w_down, v_ffn_w_down),
        "ffn_conv_w": _adamw("adamw_conv_w", [g_conv_w[l][None] for l in range(DEPTH)], ffn_conv_w, m_ffn_conv_w, v_ffn_conv_w),
    }
    small_w = [norm_mix, norm_ffn, norm_final, attn_sinks, hgrn_norm, hgrn_lb_logits, ffn_conv_b]
    small_m = [m_norm_mix, m_norm_ffn, m_norm_final, m_attn_sinks, m_hgrn_norm, m_hgrn_lb_logits, m_ffn_conv_b]
    small_v = [v_norm_mix, v_norm_ffn, v_norm_final, v_attn_sinks, v_hgrn_norm, v_hgrn_lb_logits, v_ffn_conv_b]
    small_g = [g_norm_mix, g_norm_ffn, g_norm_final, g_sinks, g_hgrn_norm, g_lb_logits, g_conv_b]
    outs = _adamw("adamw_small", [_pack_rows(small_g)[None]], *[_pack_rows(t)[None] for t in (small_w, small_m, small_v)])
    outs = [o[0] for o in outs]
    shapes = [w.shape for w in small_w]
    small = {n: [t[j] for t in [_unpack_rows(o, shapes) for o in outs]]
             for j, n in enumerate(["norm_mix", "norm_ffn", "norm_final", "attn_sinks", "hgrn_norm", "hgrn_lb_logits", "ffn_conv_b"])}
    order = ["norm_mix", "norm_ffn", "norm_final", "attn_w_in", "attn_w_out", "attn_sinks", "hgrn_w_in", "hgrn_w_out",
             "hgrn_norm", "hgrn_lb_logits", "ffn_w_up", "ffn_conv_w", "ffn_conv_b", "ffn_w_down"]
    res = {**big, **small}
    return (loss, grad_x, *[res[n][0] for n in order], *[res[n][1] for n in order], *[res[n][2] for n in order],
            *[res[n][3] for n in order])
```

```python
import numpy as np
import jax
import jax.numpy as jnp
from jax import lax
from jax.experimental import pallas as pl
from jax.experimental.pallas import tpu as pltpu

F32 = jnp.float32
BF16 = jnp.bfloat16

D_MODEL = 1024
DEPTH = 4
HEAD_DIM = 64
N_Q_HEADS = 16
N_KV_HEADS = 4
Q_PER_KV = 4
ATTN_BLOCK = 128
ATTN_IN = 1536
HG_HEADS = 8
HG_K = 128
HG_CHUNK = 64
HG_IN = 4096
D_FF = 2816
EPS = 1e-6
N_DEV = 8
FF_SLOT = 2 * D_FF // N_DEV
HG_SLOT = HG_IN // N_DEV
HG_LEVELS = 6

ADAM_LR = 0.001
ADAM_B1 = 0.9
ADAM_B2 = 0.999
ADAM_EPS = 1e-08
ADAM_WD = 0.01
ADAM_STEP = 10

VMEM_LIMIT = 56 * 1024 * 1024
ROW_TILE = 1024
WIDE_ROW_TILE = 2048
NEG_BIG = -1e30

NN = (((1,), (0,)), ((), ()))
NT = (((1,), (1,)), ((), ()))
TN = (((0,), (0,)), ((), ()))


def _bdot(a, b, dn):
    return lax.dot_general(a.astype(BF16), b.astype(BF16), dn, preferred_element_type=F32)


def _sds(shape, dtype):
    return jax.ShapeDtypeStruct(tuple(shape), dtype)


def _params(sem):
    return pltpu.CompilerParams(dimension_semantics=sem, vmem_limit_bytes=VMEM_LIMIT)


DEP_SHAPE = (8, 128)


def _dep_spec(rank):
    return pl.BlockSpec(DEP_SHAPE, lambda *_: (0, 0))


def _matmul(name, a, b, *, dn, grid, a_spec, b_spec, o_spec, out_shape, acc_shape=None, res=None, res_spec=None, dep=None):
    nk = grid[2]
    n_in = 2 + (res is not None) + (dep is not None)

    def body(*refs):
        a_ref, b_ref = refs[0], refs[1]
        r_ref = refs[2] if res is not None else None
        o_ref = refs[n_in]

        def prod():
            return _bdot(a_ref[...], b_ref[...], dn)

        def finish(v):
            if r_ref is not None:
                v = v + r_ref[...]
            o_ref[...] = v.astype(o_ref.dtype)

        if nk == 1:
            finish(prod())
        else:
            acc = refs[-1]
            k = pl.program_id(2)

            @pl.when(k == 0)
            def _():
                acc[...] = prod()

            @pl.when(k > 0)
            def _():
                acc[...] += prod()

            @pl.when(k == nk - 1)
            def _():
                finish(acc[...])

    in_specs = [a_spec, b_spec] + ([res_spec] if res is not None else []) + ([_dep_spec(3)] if dep is not None else [])
    args = (a, b) + ((res,) if res is not None else ()) + ((dep,) if dep is not None else ())
    scratch = [] if nk == 1 else [pltpu.VMEM(acc_shape, F32)]
    return pl.pallas_call(
        body, name=name, grid=grid, in_specs=in_specs, out_specs=o_spec, out_shape=out_shape,
        scratch_shapes=scratch, compiler_params=_params(("parallel", "parallel", "arbitrary")),
    )(*args)


def _tile(n, t):
    return min(n, t)


def _proj_rows(name, hn, wt, l, out_dtype):
    S, N = hn.shape[0], wt.shape[1]
    tm, tn = _tile(S, ROW_TILE), 512
    return _matmul(
        name, hn, wt, dn=NT, grid=(S // tm, N // tn, 1),
        a_spec=pl.BlockSpec((tm, D_MODEL), lambda i, j, k: (i, 0)),
        b_spec=pl.BlockSpec((None, tn, D_MODEL), lambda i, j, k: (l, j, 0)),
        o_spec=pl.BlockSpec((tm, tn), lambda i, j, k: (i, j)),
        out_shape=_sds((S, N), out_dtype))


def _proj_slots(name, hn, w, l):
    S, r = hn.shape[0], w.shape[3]
    tm = _tile(S, WIDE_ROW_TILE)
    return _matmul(
        name, hn, w, dn=NN, grid=(N_DEV, S // tm, 1),
        a_spec=pl.BlockSpec((tm, D_MODEL), lambda j, i, k: (i, 0)),
        b_spec=pl.BlockSpec((None, None, D_MODEL, r), lambda j, i, k: (l, j, 0, 0)),
        o_spec=pl.BlockSpec((None, tm, r), lambda j, i, k: (j, i, 0)),
        out_shape=_sds((N_DEV, S, r), F32))


def _out_proj(name, o, w, l, h):
    S, K = o.shape
    tm = _tile(S, ROW_TILE)
    return _matmul(
        name, o, w, dn=NN, grid=(S // tm, 1, 1),
        a_spec=pl.BlockSpec((tm, K), lambda i, j, k: (i, 0)),
        b_spec=pl.BlockSpec((None, K, D_MODEL), lambda i, j, k: (l, 0, 0)),
        o_spec=pl.BlockSpec((tm, D_MODEL), lambda i, j, k: (i, 0)),
        out_shape=_sds((S, D_MODEL), F32), res=h,
        res_spec=pl.BlockSpec((tm, D_MODEL), lambda i, j, k: (i, 0)))


def _down_proj(name, a, w, l, h):
    nj, S, r = a.shape
    tm = _tile(S, ROW_TILE)
    return _matmul(
        name, a, w, dn=NN, grid=(S // tm, 1, nj),
        a_spec=pl.BlockSpec((None, tm, r), lambda i, j, k: (k, i, 0)),
        b_spec=pl.BlockSpec((None, None, r, D_MODEL), lambda i, j, k: (l, k, 0, 0)),
        o_spec=pl.BlockSpec((tm, D_MODEL), lambda i, j, k: (i, 0)),
        out_shape=_sds((S, D_MODEL), F32), acc_shape=(tm, D_MODEL), res=h,
        res_spec=pl.BlockSpec((tm, D_MODEL), lambda i, j, k: (i, 0)))


def _dgrad_down(name, dh, w, l, dep=None):
    S = dh.shape[0]
    nj, r = w.shape[1], w.shape[2]
    tm = _tile(S, ROW_TILE)
    return _matmul(
        name, dh, w, dn=NT, grid=(nj, S // tm, 1),
        a_spec=pl.BlockSpec((tm, D_MODEL), lambda j, i, k: (i, 0)),
        b_spec=pl.BlockSpec((None, None, r, D_MODEL), lambda j, i, k: (l, j, 0, 0)),
        o_spec=pl.BlockSpec((None, tm, r), lambda j, i, k: (j, i, 0)),
        out_shape=_sds((nj, S, r), BF16), dep=dep)


def _wgrad_down(name, a, dh):
    nj, S, r = a.shape
    tk = _tile(S, ROW_TILE)
    return _matmul(
        name, a, dh, dn=TN, grid=(nj, 1, S // tk),
        a_spec=pl.BlockSpec((None, tk, r), lambda s, j, k: (s, k, 0)),
        b_spec=pl.BlockSpec((tk, D_MODEL), lambda s, j, k: (k, 0)),
        o_spec=pl.BlockSpec((None, r, D_MODEL), lambda s, j, k: (s, 0, 0)),
        out_shape=_sds((nj, r, D_MODEL), BF16), acc_shape=(r, D_MODEL))


def _dgrad_slots(name, dz, w, l):
    nj, S, r = dz.shape
    tm = _tile(S, ROW_TILE)
    return _matmul(
        name, dz, w, dn=NT, grid=(S // tm, 1, nj),
        a_spec=pl.BlockSpec((None, tm, r), lambda i, j, k: (k, i, 0)),
        b_spec=pl.BlockSpec((None, None, D_MODEL, r), lambda i, j, k: (l, k, 0, 0)),
        o_spec=pl.BlockSpec((tm, D_MODEL), lambda i, j, k: (i, 0)),
        out_shape=_sds((S, D_MODEL), F32), acc_shape=(tm, D_MODEL))


def _wgrad_slots(name, hn, dz):
    nj, S, r = dz.shape
    tk = _tile(S, ROW_TILE)
    return _matmul(
        name, hn, dz, dn=TN, grid=(nj, 1, S // tk),
        a_spec=pl.BlockSpec((tk, D_MODEL), lambda s, j, k: (k, 0)),
        b_spec=pl.BlockSpec((None, tk, r), lambda s, j, k: (s, k, 0)),
        o_spec=pl.BlockSpec((None, D_MODEL, r), lambda s, j, k: (s, 0, 0)),
        out_shape=_sds((nj, D_MODEL, r), BF16), acc_shape=(D_MODEL, r))


def _dgrad_out(name, dh, w, l, out_dtype, dep=None):
    S, K = dh.shape[0], w.shape[1]
    tm = _tile(S, ROW_TILE)
    return _matmul(
        name, dh, w, dn=NT, grid=(S // tm, 1, 1),
        a_spec=pl.BlockSpec((tm, D_MODEL), lambda i, j, k: (i, 0)),
        b_spec=pl.BlockSpec((None, K, D_MODEL), lambda i, j, k: (l, 0, 0)),
        o_spec=pl.BlockSpec((tm, K), lambda i, j, k: (i, 0)),
        out_shape=_sds((S, K), out_dtype), dep=dep)


def _wgrad_rows(name, a, b):
    S, K = a.shape
    tk = _tile(S, ROW_TILE)
    return _matmul(
        name, a, b, dn=TN, grid=(1, 1, S // tk),
        a_spec=pl.BlockSpec((tk, K), lambda i, j, k: (k, 0)),
        b_spec=pl.BlockSpec((tk, D_MODEL), lambda i, j, k: (k, 0)),
        o_spec=pl.BlockSpec((K, D_MODEL), lambda i, j, k: (0, 0)),
        out_shape=_sds((K, D_MODEL), BF16), acc_shape=(K, D_MODEL))


def _dgrad_rows(name, dz, wt, l):
    S, N = dz.shape
    tm = _tile(S, ROW_TILE)
    return _matmul(
        name, dz, wt, dn=NN, grid=(S // tm, 1, 1),
        a_spec=pl.BlockSpec((tm, N), lambda i, j, k: (i, 0)),
        b_spec=pl.BlockSpec((None, N, D_MODEL), lambda i, j, k: (l, 0, 0)),
        o_spec=pl.BlockSpec((tm, D_MODEL), lambda i, j, k: (i, 0)),
        out_shape=_sds((S, D_MODEL), F32))


def _rmsnorm_fwd(name, h, g, dep=None):
    S = h.shape[0]
    tm = _tile(S, ROW_TILE)

    def body(h_ref, g_ref, *rest):
        o_ref = rest[-1]
        x = h_ref[...]
        r = lax.rsqrt(jnp.mean(x * x, axis=-1, keepdims=True) + EPS)
        o_ref[...] = (x * r * g_ref[...]).astype(o_ref.dtype)

    row = pl.BlockSpec((tm, D_MODEL), lambda i: (i, 0))
    deps = () if dep is None else (dep,)
    return pl.pallas_call(
        body, name=name, grid=(S // tm,),
        in_specs=[row, pl.BlockSpec((1, D_MODEL), lambda i: (0, 0))] + [_dep_spec(1)] * len(deps),
        out_specs=row, out_shape=_sds((S, D_MODEL), BF16), compiler_params=_params(("parallel",)),
    )(h, g, *deps)


def _rmsnorm_bwd(name, h, g, dy, dres):
    S = h.shape[0]
    tm = _tile(S, ROW_TILE)

    def body(h_ref, g_ref, dy_ref, dres_ref, dh_ref, dg_ref):
        x = h_ref[...]
        r = lax.rsqrt(jnp.mean(x * x, axis=-1, keepdims=True) + EPS)
        xh = x * r
        dyf = dy_ref[...].astype(F32)
        dyg = dyf * g_ref[...]
        dh_ref[...] = dres_ref[...] + r * (dyg - xh * jnp.mean(dyg * xh, axis=-1, keepdims=True))
        part = jnp.sum(dyf * xh, axis=0, keepdims=True)

        @pl.when(pl.program_id(0) == 0)
        def _():
            dg_ref[...] = part

        @pl.when(pl.program_id(0) > 0)
        def _():
            dg_ref[...] += part

    row = pl.BlockSpec((tm, D_MODEL), lambda i: (i, 0))
    vec = pl.BlockSpec((1, D_MODEL), lambda i: (0, 0))
    return pl.pallas_call(
        body, name=name, grid=(S // tm,), in_specs=[row, vec, row, row], out_specs=[row, vec],
        out_shape=[_sds((S, D_MODEL), F32), _sds((1, D_MODEL), F32)], compiler_params=_params(("arbitrary",)),
    )(h, g, dy, dres)


def _loss_head(name, h, g, target):
    S = h.shape[0]
    tm = _tile(S, ROW_TILE)

    def body(h_ref, g_ref, t_ref, dh_ref, dg_ref, ls_ref):
        x = h_ref[...]
        r = lax.rsqrt(jnp.mean(x * x, axis=-1, keepdims=True) + EPS)
        xh = x * r
        diff = xh * g_ref[...] - t_ref[...]
        dyf = diff * (1.0 / D_MODEL)
        dyg = dyf * g_ref[...]
        dh_ref[...] = r * (dyg - xh * jnp.mean(dyg * xh, axis=-1, keepdims=True))
        part = jnp.sum(dyf * xh, axis=0, keepdims=True)
        lpart = jnp.sum(diff * diff, axis=0, keepdims=True) * (0.5 / D_MODEL)

        @pl.when(pl.program_id(0) == 0)
        def _():
            dg_ref[...] = part
            ls_ref[...] = lpart

        @pl.when(pl.program_id(0) > 0)
        def _():
            dg_ref[...] += part
            ls_ref[...] += lpart

    row = pl.BlockSpec((tm, D_MODEL), lambda i: (i, 0))
    vec = pl.BlockSpec((1, D_MODEL), lambda i: (0, 0))
    return pl.pallas_call(
        body, name=name, grid=(S // tm,), in_specs=[row, vec, row], out_specs=[row, vec, vec],
        out_shape=[_sds((S, D_MODEL), F32), _sds((1, D_MODEL), F32), _sds((1, D_MODEL), F32)],
        compiler_params=_params(("arbitrary",)),
    )(h, g, target)


ATTN_SCALE = HEAD_DIM ** -0.5
ALIBI_SLOPES = [2.0 ** (-8.0 * (h + 1) / N_Q_HEADS) for h in range(N_Q_HEADS)]
K_COL = N_Q_HEADS * HEAD_DIM
KV_COLS = N_KV_HEADS * HEAD_DIM
V_COL = K_COL + KV_COLS


def _attn_masks(n):
    qi = lax.broadcasted_iota(jnp.int32, (ATTN_BLOCK, ATTN_BLOCK), 0)
    ki = lax.broadcasted_iota(jnp.int32, (ATTN_BLOCK, ATTN_BLOCK), 1)
    dist_c = (qi - ki).astype(F32)
    return dist_c + float(ATTN_BLOCK), dist_c, (ki > qi) & (n > 0), qi >= ki


def _attn_probs(raw_p, raw_c, sink, slope, masks):
    dist_p, dist_c, valid_p, valid_c = masks
    sp = jnp.where(valid_p, raw_p * ATTN_SCALE - slope * dist_p, NEG_BIG)
    sc = jnp.where(valid_c, raw_c * ATTN_SCALE - slope * dist_c, NEG_BIG)
    m = jnp.maximum(jnp.maximum(jnp.max(sp, axis=-1, keepdims=True), jnp.max(sc, axis=-1, keepdims=True)), sink)
    ep, ec, es = jnp.exp(sp - m), jnp.exp(sc - m), jnp.exp(sink - m)
    inv = 1.0 / (jnp.sum(ep, axis=-1, keepdims=True) + jnp.sum(ec, axis=-1, keepdims=True) + es)
    return ep * inv, ec * inv, es * inv


def _group_rows(ref, m):
    return jnp.concatenate([ref[:, HEAD_DIM * (Q_PER_KV * m + g):HEAD_DIM * (Q_PER_KV * m + g + 1)]
                            for g in range(Q_PER_KV)], axis=0)


def _head_rows(x, g):
    return x[ATTN_BLOCK * g:ATTN_BLOCK * (g + 1)]


def _attn_specs(nblk):
    last = nblk - 1
    kcol, vcol = K_COL // KV_COLS, V_COL // KV_COLS
    return [
        pl.BlockSpec((ATTN_BLOCK, K_COL), lambda n: (jnp.minimum(n, last), 0)),
        pl.BlockSpec((ATTN_BLOCK, KV_COLS), lambda n: (jnp.minimum(n, last), kcol)),
        pl.BlockSpec((ATTN_BLOCK, KV_COLS), lambda n: (jnp.maximum(jnp.minimum(n, last) - 1, 0), kcol)),
        pl.BlockSpec((ATTN_BLOCK, KV_COLS), lambda n: (jnp.minimum(n, last), vcol)),
        pl.BlockSpec((ATTN_BLOCK, KV_COLS), lambda n: (jnp.maximum(jnp.minimum(n, last) - 1, 0), vcol)),
    ]


def _attn_fwd(name, proj, sinks):
    S = proj.shape[0]
    nblk = S // ATTN_BLOCK

    def body(q_ref, kc_ref, kp_ref, vc_ref, vp_ref, sk_ref, o_ref):
        masks = _attn_masks(pl.program_id(0))
        for m in range(N_KV_HEADS):
            ks = slice(HEAD_DIM * m, HEAD_DIM * (m + 1))
            kp, kc, vp, vc = kp_ref[:, ks], kc_ref[:, ks], vp_ref[:, ks], vc_ref[:, ks]
            q4 = _group_rows(q_ref, m)
            raw_p, raw_c = _bdot(q4, kp, NT), _bdot(q4, kc, NT)
            pps, pcs = [], []
            for g in range(Q_PER_KV):
                hh = Q_PER_KV * m + g
                pp, pc, _ = _attn_probs(_head_rows(raw_p, g), _head_rows(raw_c, g), sk_ref[0, hh], ALIBI_SLOPES[hh], masks)
                pps.append(pp.astype(BF16))
                pcs.append(pc.astype(BF16))
            o4 = _bdot(jnp.concatenate(pps, axis=0), vp, NN) + _bdot(jnp.concatenate(pcs, axis=0), vc, NN)
            for g in range(Q_PER_KV):
                hh = Q_PER_KV * m + g
                o_ref[:, HEAD_DIM * hh:HEAD_DIM * (hh + 1)] = _head_rows(o4, g).astype(o_ref.dtype)

    return pl.pallas_call(
        body, name=name, grid=(nblk,),
        in_specs=_attn_specs(nblk) + [pl.BlockSpec(memory_space=pltpu.SMEM)],
        out_specs=pl.BlockSpec((ATTN_BLOCK, K_COL), lambda n: (n, 0)),
        out_shape=_sds((S, K_COL), BF16), compiler_params=_params(("parallel",)),
    )(proj, proj, proj, proj, proj, sinks)


def _attn_bwd(name, proj, sinks, do):
    S = proj.shape[0]
    nblk = S // ATTN_BLOCK

    def body(q_ref, kc_ref, kp_ref, vc_ref, vp_ref, do_ref, sk_ref, dz_ref, ds_ref, carry, cur, padd):
        n = pl.program_id(0)

        @pl.when(n == 0)
        def _():
            carry[...] = jnp.zeros_like(carry)
            ds_ref[...] = jnp.zeros_like(ds_ref)

        @pl.when(n < nblk)
        def _():
            masks = _attn_masks(n)
            lane = lax.broadcasted_iota(jnp.int32, (1, 128), 1)
            dsv = jnp.zeros((1, 128), F32)
            for m in range(N_KV_HEADS):
                ks = slice(HEAD_DIM * m, HEAD_DIM * (m + 1))
                kp, kc, vp, vc = kp_ref[:, ks], kc_ref[:, ks], vp_ref[:, ks], vc_ref[:, ks]
                q4, do4 = _group_rows(q_ref, m), _group_rows(do_ref, m)
                raw_p, raw_c = _bdot(q4, kp, NT), _bdot(q4, kc, NT)
                dpp4, dpc4 = _bdot(do4, vp, NT), _bdot(do4, vc, NT)
                pps, pcs, dsps, dscs = [], [], [], []
                for g in range(Q_PER_KV):
                    hh = Q_PER_KV * m + g
                    pp, pc, ps = _attn_probs(_head_rows(raw_p, g), _head_rows(raw_c, g), sk_ref[0, hh], ALIBI_SLOPES[hh], masks)
                    dpp, dpc = _head_rows(dpp4, g), _head_rows(dpc4, g)
                    delta = jnp.sum(pp * dpp, axis=-1, keepdims=True) + jnp.sum(pc * dpc, axis=-1, keepdims=True)
                    dsv = dsv + jnp.where(lane == hh, -jnp.sum(ps * delta, axis=0, keepdims=True), 0.0)
                    pps.append(pp.astype(BF16))
                    pcs.append(pc.astype(BF16))
                    dsps.append((pp * (dpp - delta)).astype(BF16))
                    dscs.append((pc * (dpc - delta)).astype(BF16))
                pp4, pc4 = jnp.concatenate(pps, axis=0), jnp.concatenate(pcs, axis=0)
                dsp4, dsc4 = jnp.concatenate(dsps, axis=0), jnp.concatenate(dscs, axis=0)
                dq4 = (_bdot(dsp4, kp, NN) + _bdot(dsc4, kc, NN)) * ATTN_SCALE
                for g in range(Q_PER_KV):
                    hh = Q_PER_KV * m + g
                    cur[:, HEAD_DIM * hh:HEAD_DIM * (hh + 1)] = _head_rows(dq4, g)
                cur[:, K_COL + HEAD_DIM * m:K_COL + HEAD_DIM * (m + 1)] = _bdot(dsc4, q4, TN) * ATTN_SCALE
                cur[:, V_COL + HEAD_DIM * m:V_COL + HEAD_DIM * (m + 1)] = _bdot(pc4, do4, TN)
                padd[:, ks] = _bdot(dsp4, q4, TN) * ATTN_SCALE
                padd[:, KV_COLS + HEAD_DIM * m:KV_COLS + HEAD_DIM * (m + 1)] = _bdot(pp4, do4, TN)
            ds_ref[...] += dsv
            dz_ref[:, :K_COL] = carry[:, :K_COL].astype(dz_ref.dtype)
            dz_ref[:, K_COL:] = (carry[:, K_COL:] + padd[...]).astype(dz_ref.dtype)
            carry[...] = cur[...]

        @pl.when(n == nblk)
        def _():
            dz_ref[...] = carry[...].astype(dz_ref.dtype)

    return pl.pallas_call(
        body, name=name, grid=(nblk + 1,),
        in_specs=_attn_specs(nblk) + [
            pl.BlockSpec((ATTN_BLOCK, K_COL), lambda n: (jnp.minimum(n, nblk - 1), 0)),
            pl.BlockSpec(memory_space=pltpu.SMEM)],
        out_specs=[pl.BlockSpec((ATTN_BLOCK, ATTN_IN), lambda n: (jnp.maximum(n - 1, 0), 0)),
                   pl.BlockSpec((1, 128), lambda n: (0, 0))],
        out_shape=[_sds((S, ATTN_IN), BF16), _sds((1, 128), F32)],
        scratch_shapes=[pltpu.VMEM((ATTN_BLOCK, ATTN_IN), F32), pltpu.VMEM((ATTN_BLOCK, ATTN_IN), F32),
                        pltpu.VMEM((ATTN_BLOCK, 2 * KV_COLS), F32)],
        compiler_params=_params(("arbitrary",)),
    )(proj, proj, proj, proj, proj, do, sinks)


def _hg_consts():
    C = HG_CHUNK
    tri = np.tril(np.ones((C, C)))
    t = np.arange(C)
    rows, masks = [tri], []
    for lvl in range(HG_LEVELS):
        n = C >> (lvl + 1)
        sel = np.zeros((C, C))
        sel[t, (t // (2 * n)) * (2 * n) + n - 1] = 1.0
        rows.append(sel @ tri)
        tt, ss = t[:, None], t[None, :]
        masks.append((tt // (2 * n) == ss // (2 * n)) & ((tt // n) % 2 == 1) & ((ss // n) % 2 == 0))
    masks.append(np.eye(C, dtype=bool))
    stk = np.concatenate(rows, axis=0)
    return jnp.asarray(stk, BF16), jnp.asarray(np.stack(masks), F32)


def _sigmoid(x):
    return 1.0 / (1.0 + jnp.exp(-x))


def _split3(x):
    hi = x.astype(BF16)
    r1 = x - hi.astype(F32)
    mid = r1.astype(BF16)
    return hi, mid, (r1 - mid.astype(F32)).astype(BF16)


def _dot01(m01, x, dn):
    return sum(lax.dot_general(m01, p, dn, preferred_element_type=F32) for p in _split3(x))


def _hg_common(z_ref, lb_ref, stk_ref):
    qr, fr = z_ref[0], z_ref[1]
    lb = lb_ref[...]
    sq, sg, sgn = _sigmoid(qr), _sigmoid(fr), _sigmoid(-fr)
    ft = lb + (1.0 - lb) * sg
    lf = jnp.log(ft)
    bb = _dot01(stk_ref[...], lf, NN)
    b = bb[0:HG_CHUNK]
    diffs = [b - bb[HG_CHUNK * (l + 1):HG_CHUNK * (l + 2)] for l in range(HG_LEVELS)]
    ws = [jnp.exp(-jnp.abs(d)) for d in diffs]
    blast = b[HG_CHUNK - 1:HG_CHUNK]
    return dict(qr=qr, fr=fr, lb=lb, sq=sq, sg=sg, sgn=sgn, ft=ft, q=qr * sq, kk=(1.0 - lb) * sgn, b=b, diffs=diffs,
                ws=ws, eb=jnp.exp(b), ed=jnp.exp(blast - b), elast=jnp.exp(blast))


def _hg_intra(qh, kh, ws, msk_ref, sl):
    a = msk_ref[HG_LEVELS] * _bdot(qh, kh, NT)
    qls, kls = [], []
    for l in range(HG_LEVELS):
        w = ws[l][:, sl]
        qls.append((qh * w).astype(BF16))
        kls.append((kh * w).astype(BF16))
        a = a + msk_ref[l] * _bdot(qls[l], kls[l], NT)
    return a, qls, kls


def _hg_fwd(name, z, lb, ng):
    S = z.shape[2]
    nc = S // HG_CHUNK
    stk, msk = _hg_consts()

    def body(z_ref, lb_ref, ng_ref, stk_ref, msk_ref, og_ref, st_ref, state):
        @pl.when(pl.program_id(1) == 0)
        def _():
            state[...] = jnp.zeros_like(state)

        cm = _hg_common(z_ref, lb_ref, stk_ref)
        v, gt = z_ref[2], z_ref[3]
        kd = cm["kk"] * cm["ed"]
        for hh in range(4):
            sl = slice(HG_K * hh, HG_K * (hh + 1))
            st = state[hh]
            st_ref[hh] = st
            qh, kh, vh = cm["q"][:, sl], cm["kk"][:, sl], v[:, sl]
            a, _, _ = _hg_intra(qh, kh, cm["ws"], msk_ref, sl)
            o = _bdot(a, vh, NN) + _bdot(qh * cm["eb"][:, sl], st, NT)
            state[hh] = cm["elast"][:, sl] * st + _bdot(vh, kd[:, sl], TN)
            r = lax.rsqrt(jnp.mean(o * o, axis=-1, keepdims=True) + EPS)
            gh = gt[:, sl]
            og_ref[:, sl] = (o * r * ng_ref[...] * (gh * _sigmoid(gh))).astype(og_ref.dtype)

    return pl.pallas_call(
        body, name=name, grid=(2, nc),
        in_specs=[pl.BlockSpec((4, None, HG_CHUNK, HG_SLOT), lambda g, c: (0, g, c, 0)),
                  pl.BlockSpec((1, HG_SLOT), lambda g, c: (0, g)),
                  pl.BlockSpec((1, HG_K), lambda g, c: (0, 0)),
                  pl.BlockSpec(stk.shape, lambda g, c: (0, 0)),
                  pl.BlockSpec(msk.shape, lambda g, c: (0, 0, 0))],
        out_specs=[pl.BlockSpec((HG_CHUNK, HG_SLOT), lambda g, c: (c, g)),
                   pl.BlockSpec((None, 4, HG_K, HG_K), lambda g, c: (c, g, 0, 0))],
        out_shape=[_sds((S, D_MODEL), BF16), _sds((nc, HG_HEADS, HG_K, HG_K), F32)],
        scratch_shapes=[pltpu.VMEM((4, HG_K, HG_K), F32)],
        compiler_params=_params(("parallel", "arbitrary")),
    )(z, lb, ng, stk, msk)


def _hg_bwd(name, z, lb, ng, states, dog):
    S = z.shape[2]
    nc = S // HG_CHUNK
    stk, msk = _hg_consts()

    def body(z_ref, lb_ref, ng_ref, stk_ref, msk_ref, st_ref, dog_ref, dz_ref, dlb_ref, dng_ref, dstate):
        @pl.when(pl.program_id(1) == 0)
        def _():
            dstate[...] = jnp.zeros_like(dstate)
            dlb_ref[...] = jnp.zeros_like(dlb_ref)
            dng_ref[...] = jnp.zeros_like(dng_ref)

        cm = _hg_common(z_ref, lb_ref, stk_ref)
        v, gt = z_ref[2], z_ref[3]
        ng = ng_ref[...]
        kd = cm["kk"] * cm["ed"]
        row = lax.broadcasted_iota(jnp.int32, (HG_CHUNK, 1), 0)
        dng = jnp.zeros((1, HG_K), F32)
        dq_h, dkk_h, db_h, dv_h, dgt_h = [], [], [], [], []
        dr_h = [[] for _ in range(HG_LEVELS)]
        for hh in range(4):
            sl = slice(HG_K * hh, HG_K * (hh + 1))
            st, dst = st_ref[hh], dstate[hh]
            qh, kh, vh, ebh, edh, kdh = cm["q"][:, sl], cm["kk"][:, sl], v[:, sl], cm["eb"][:, sl], cm["ed"][:, sl], kd[:, sl]
            elh = cm["elast"][:, sl]
            a, qls, kls = _hg_intra(qh, kh, cm["ws"], msk_ref, sl)
            qe = qh * ebh
            o = _bdot(a, vh, NN) + _bdot(qe, st, NT)
            r = lax.rsqrt(jnp.mean(o * o, axis=-1, keepdims=True) + EPS)
            xh = o * r
            gh = gt[:, sl]
            sgg = _sigmoid(gh)
            dog = dog_ref[:, sl].astype(F32)
            dy = dog * (gh * sgg)
            dgt_h.append(dog * (xh * ng) * (sgg * (1.0 + gh * (1.0 - sgg))))
            dng = dng + jnp.sum(dy * xh, axis=0, keepdims=True)
            dyg = dy * ng
            do = r * (dyg - xh * jnp.mean(dyg * xh, axis=-1, keepdims=True))
            da = _bdot(do, vh, NT)
            dv_h.append(_bdot(a, do, TN) + _bdot(kdh, dst, NT))
            dkd = _bdot(vh, dst, NN)
            delast = jnp.sum(st * dst, axis=0, keepdims=True)
            dqe = _bdot(do, st, NN)
            dstate[hh] = elh * dst + _bdot(do, qe, TN)
            gk = dkd * kdh
            dblast = jnp.sum(gk, axis=0, keepdims=True) + delast * elh
            db = dqe * qe - gk + jnp.where(row == HG_CHUNK - 1, dblast, 0.0)
            dp = (msk_ref[HG_LEVELS] * da).astype(BF16)
            dq = dqe * ebh + _bdot(dp, kh, NN)
            dkk = dkd * edh + _bdot(dp, qh, TN)
            for l in range(HG_LEVELS):
                dp = (msk_ref[l] * da).astype(BF16)
                dql, dkl = _bdot(dp, kls[l], NN), _bdot(dp, qls[l], TN)
                w = cm["ws"][l][:, sl]
                dq = dq + dql * w
                dkk = dkk + dkl * w
                half = jnp.where(((row >> (HG_LEVELS - 1 - l)) & 1) == 1, 1.0, -1.0)
                dd = half * w * (dql * qh + dkl * kh)
                db = db + dd
                dr_h[l].append(-dd)
            dq_h.append(dq)
            dkk_h.append(dkk)
            db_h.append(db)
        cat = lambda xs: jnp.concatenate(xs, axis=1)
        cot = jnp.concatenate([cat(db_h)] + [cat(dr_h[l]) for l in range(HG_LEVELS)], axis=0)
        dlf = _dot01(stk_ref[...], cot, TN)
        dq, dkk = cat(dq_h), cat(dkk_h)
        dft = dlf / cm["ft"]
        one_lb = 1.0 - cm["lb"]
        dz_ref[0] = (dq * (cm["sq"] * (1.0 + cm["qr"] * (1.0 - cm["sq"])))).astype(dz_ref.dtype)
        dz_ref[1] = ((dft - dkk) * one_lb * cm["sg"] * cm["sgn"]).astype(dz_ref.dtype)
        dz_ref[2] = cat(dv_h).astype(dz_ref.dtype)
        dz_ref[3] = cat(dgt_h).astype(dz_ref.dtype)
        dlb_ref[...] += jnp.sum((dft - dkk) * cm["sgn"], axis=0, keepdims=True)
        dng_ref[...] += dng

    rev = lambda c: nc - 1 - c
    return pl.pallas_call(
        body, name=name, grid=(2, nc),
        in_specs=[pl.BlockSpec((4, None, HG_CHUNK, HG_SLOT), lambda g, c: (0, g, rev(c), 0)),
                  pl.BlockSpec((1, HG_SLOT), lambda g, c: (0, g)),
                  pl.BlockSpec((1, HG_K), lambda g, c: (0, 0)),
                  pl.BlockSpec(stk.shape, lambda g, c: (0, 0)),
                  pl.BlockSpec(msk.shape, lambda g, c: (0, 0, 0)),
                  pl.BlockSpec((None, 4, HG_K, HG_K), lambda g, c: (rev(c), g, 0, 0)),
                  pl.BlockSpec((HG_CHUNK, HG_SLOT), lambda g, c: (rev(c), g))],
        out_specs=[pl.BlockSpec((4, None, HG_CHUNK, HG_SLOT), lambda g, c: (0, g, rev(c), 0)),
                   pl.BlockSpec((1, HG_SLOT), lambda g, c: (0, g)),
                   pl.BlockSpec((None, 1, HG_K), lambda g, c: (g, 0, 0))],
        out_shape=[_sds(z.shape, BF16), _sds((1, 2 * HG_SLOT), F32), _sds((2, 1, HG_K), F32)],
        scratch_shapes=[pltpu.VMEM((4, HG_K, HG_K), F32)],
        compiler_params=_params(("parallel", "arbitrary")),
    )(z, lb, ng, stk, msk, states, dog)


def _lb_fwd(name, logits):
    def body(l_ref, o_ref):
        x = l_ref[...]
        e = jnp.exp(x - jnp.max(x, axis=0, keepdims=True))
        s = e / jnp.sum(e, axis=0, keepdims=True)
        o_ref[0:1, :] = s[1:2]
        o_ref[1:2, :] = s[1:2] + s[2:3] + s[3:4]

    return pl.pallas_call(body, name=name, out_shape=_sds((2, logits.shape[1]), F32))(logits)


def _lb_bwd(name, logits, dlb):
    def body(l_ref, d_ref, o_ref):
        x = l_ref[...]
        e = jnp.exp(x - jnp.max(x, axis=0, keepdims=True))
        s = e / jnp.sum(e, axis=0, keepdims=True)
        d1, d3 = d_ref[0:1, :], d_ref[1:2, :]
        ds = [jnp.zeros_like(d1), d1 + d3, d3, d3]
        dot = sum(ds[r] * s[r:r + 1] for r in range(1, DEPTH))
        for r in range(DEPTH):
            o_ref[r:r + 1, :] = s[r:r + 1] * (ds[r] - dot)

    return pl.pallas_call(body, name=name, out_shape=_sds(logits.shape, F32))(logits, dlb)


SUB = 8


def _rows_down(x, prev, k):
    row = lax.broadcasted_iota(jnp.int32, x.shape, 0)
    return jnp.where(row >= k, pltpu.roll(x, k, 0), pltpu.roll(prev, k, 0))


def _rows_up(x, nxt, k):
    row = lax.broadcasted_iota(jnp.int32, x.shape, 0)
    return jnp.where(row < SUB - k, pltpu.roll(x, SUB - k, 0), pltpu.roll(nxt, SUB - k, 0))


def _conv_block(w_ref, b_ref, p, x, prev):
    x0, x1 = _rows_down(x, prev, 2), _rows_down(x, prev, 1)
    return b_ref[p] + w_ref[p, 0:1, :] * x0 + w_ref[p, 1:2, :] * x1 + w_ref[p, 2:3, :] * x, x0, x1


def _convgate_fwd(name, u, cw, cb):
    S = u.shape[2]
    tm = _tile(S, ROW_TILE)

    def body(u_ref, w_ref, b_ref, a_ref, halo):
        @pl.when(pl.program_id(1) == 0)
        def _():
            halo[...] = jnp.zeros_like(halo)

        def step(r, prev):
            pg, pv = prev
            out = []
            for s in range(2):
                rows = pl.ds(pl.multiple_of(r * 2 * SUB + s * SUB, SUB), SUB)
                xg, xv = u_ref[0, rows, :], u_ref[1, rows, :]
                cg = _conv_block(w_ref, b_ref, 0, xg, pg)[0]
                cv = _conv_block(w_ref, b_ref, 1, xv, pv)[0]
                out.append(cg * _sigmoid(cg) * cv)
                pg, pv = xg, xv
            a_ref[pl.ds(pl.multiple_of(r * 2 * SUB, 2 * SUB), 2 * SUB), :] = jnp.concatenate(out, axis=0).astype(a_ref.dtype)
            return pg, pv

        pg, pv = lax.fori_loop(0, tm // (2 * SUB), step, (halo[0], halo[1]))
        halo[0] = pg
        halo[1] = pv

    return pl.pallas_call(
        body, name=name, grid=(4, S // tm),
        in_specs=[pl.BlockSpec((2, None, tm, FF_SLOT), lambda j, t: (0, j, t, 0)),
                  pl.BlockSpec((2, None, 3, FF_SLOT), lambda j, t: (0, j, 0, 0)),
                  pl.BlockSpec((2, None, 1, FF_SLOT), lambda j, t: (0, j, 0, 0))],
        out_specs=pl.BlockSpec((None, tm, FF_SLOT), lambda j, t: (j, t, 0)),
        out_shape=_sds((4, S, FF_SLOT), BF16),
        scratch_shapes=[pltpu.VMEM((2, SUB, FF_SLOT), F32)],
        compiler_params=_params(("parallel", "arbitrary")),
    )(u, cw, cb)


def _convgate_bwd(name, u, cw, cb, da):
    S = u.shape[2]
    tm = _tile(S, ROW_TILE)
    nt = S // tm

    def body(u_ref, uh_ref, w_ref, b_ref, da_ref, du_out, dw_ref, db_ref, after, first, acc, du_ref):
        t = pl.program_id(1)

        @pl.when(t == 0)
        def _():
            after[...] = jnp.zeros_like(after)
            acc[...] = jnp.zeros_like(acc)

        def du_block(p, d, nxt):
            return w_ref[p, 2:3, :] * d + w_ref[p, 1:2, :] * _rows_up(d, nxt, 1) + w_ref[p, 0:1, :] * _rows_up(d, nxt, 2)

        def step(r, carry):
            pg, pv, dg_last, dv_last = carry
            dav = da_ref[pl.ds(pl.multiple_of(r * 2 * SUB, 2 * SUB), 2 * SUB), :].astype(F32)
            for s in range(2):
                at = r * 2 * SUB + s * SUB
                rows = pl.ds(pl.multiple_of(at, SUB), SUB)
                xg, xv = u_ref[0, rows, :], u_ref[1, rows, :]
                cg, x0g, x1g = _conv_block(w_ref, b_ref, 0, xg, pg)
                cv, x0v, x1v = _conv_block(w_ref, b_ref, 1, xv, pv)
                sg = _sigmoid(cg)
                dab = dav[s * SUB:(s + 1) * SUB]
                dg = dab * cv * (sg * (1.0 + cg * (1.0 - sg)))
                dv = dab * cg * sg
                for p, d, taps in ((0, dg, (x0g, x1g, xg)), (1, dv, (x0v, x1v, xv))):
                    for j in range(3):
                        acc[p, j] += d * taps[j]
                    acc[p, 3] += d
                if s == 0:
                    @pl.when(r == 0)
                    def _():
                        first[0] = dg
                        first[1] = dv

                    @pl.when(r > 0)
                    def _():
                        before = pl.ds(pl.multiple_of(at - SUB, SUB), SUB)
                        du_ref[0, before, :] = du_block(0, dg_last, dg)
                        du_ref[1, before, :] = du_block(1, dv_last, dv)
                else:
                    before = pl.ds(pl.multiple_of(at - SUB, SUB), SUB)
                    du_ref[0, before, :] = du_block(0, dg_last, dg)
                    du_ref[1, before, :] = du_block(1, dv_last, dv)
                pg, pv, dg_last, dv_last = xg, xv, dg, dv
            return pg, pv, dg_last, dv_last

        halo = jnp.where(t < nt - 1, uh_ref[...], 0.0)
        zero = jnp.zeros((SUB, FF_SLOT), F32)
        _, _, dg_last, dv_last = lax.fori_loop(0, tm // (2 * SUB), step, (halo[0], halo[1], zero, zero))
        du_ref[0, tm - SUB:tm, :] = du_block(0, dg_last, after[0])
        du_ref[1, tm - SUB:tm, :] = du_block(1, dv_last, after[1])
        du_out[...] = du_ref[...].astype(du_out.dtype)
        after[...] = first[...]
        for p in range(2):
            for j in range(3):
                dw_ref[p, j:j + 1, :] = jnp.sum(acc[p, j], axis=0, keepdims=True)
            db_ref[p] = jnp.sum(acc[p, 3], axis=0, keepdims=True)

    rev = lambda t: nt - 1 - t
    return pl.pallas_call(
        body, name=name, grid=(4, nt),
        in_specs=[pl.BlockSpec((2, None, tm, FF_SLOT), lambda j, t: (0, j, rev(t), 0)),
                  pl.BlockSpec((2, None, SUB, FF_SLOT), lambda j, t: (0, j, jnp.maximum(rev(t) * (tm // SUB) - 1, 0), 0)),
                  pl.BlockSpec((2, None, 3, FF_SLOT), lambda j, t: (0, j, 0, 0)),
                  pl.BlockSpec((2, None, 1, FF_SLOT), lambda j, t: (0, j, 0, 0)),
                  pl.BlockSpec((None, tm, FF_SLOT), lambda j, t: (j, rev(t), 0))],
        out_specs=[pl.BlockSpec((2, None, tm, FF_SLOT), lambda j, t: (0, j, rev(t), 0)),
                   pl.BlockSpec((2, None, 3, FF_SLOT), lambda j, t: (0, j, 0, 0)),
                   pl.BlockSpec((2, None, 1, FF_SLOT), lambda j, t: (0, j, 0, 0))],
        out_shape=[_sds(u.shape, BF16), _sds(cw.shape, F32), _sds(cb.shape, F32)],
        scratch_shapes=[pltpu.VMEM((2, SUB, FF_SLOT), F32), pltpu.VMEM((2, SUB, FF_SLOT), F32),
                        pltpu.VMEM((2, 4, SUB, FF_SLOT), F32), pltpu.VMEM((2, tm, FF_SLOT), F32)],
        compiler_params=_params(("parallel", "arbitrary")),
    )(u, u, cw, cb, da)


def _row_tile(R):
    for t in range(256, 15, -16):
        if R % t == 0:
            return t
    return R


def _sum_sources(name, gsrc):
    n, R, C = gsrc.shape
    tr = _row_tile(R)

    def body(g_ref, o_ref):
        g = g_ref[0].astype(F32)
        for s in range(1, n):
            g = g + g_ref[s].astype(F32)
        o_ref[...] = g

    return pl.pallas_call(
        body, name=name, grid=(R // tr,), in_specs=[pl.BlockSpec((n, tr, C), lambda i: (0, i, 0))],
        out_specs=pl.BlockSpec((tr, C), lambda i: (i, 0)), out_shape=_sds((R, C), F32),
        compiler_params=_params(("parallel",)),
    )(gsrc)


def _adamw(name, gsrcs, w, m, v):
    L = len(gsrcs)
    n, A, C = gsrcs[0].shape
    tr = _row_tile(A)

    def body(*refs):
        g_refs = refs[:L]
        w_ref, m_ref, v_ref, go_ref, d_ref, mo_ref, vo_ref = refs[L:]
        for k in range(L):
            @pl.when(pl.program_id(0) == k)
            def _(k=k):
                g = g_refs[k][0].astype(F32)
                for s in range(1, n):
                    g = g + g_refs[k][s].astype(F32)
                m2 = ADAM_B1 * m_ref[...] + (1.0 - ADAM_B1) * g
                v2 = ADAM_B2 * v_ref[...] + (1.0 - ADAM_B2) * (g * g)
                m_hat = m2 / (1.0 - ADAM_B1 ** ADAM_STEP)
                v_hat = v2 / (1.0 - ADAM_B2 ** ADAM_STEP)
                go_ref[...] = g
                d_ref[...] = -ADAM_LR * (m_hat / (jnp.sqrt(v_hat) + ADAM_EPS) + ADAM_WD * w_ref[...])
                mo_ref[...] = m2
                vo_ref[...] = v2

    g_specs = [pl.BlockSpec((n, tr, C), lambda l, i, k=k: (0, jnp.where(l == k, i, 0), 0)) for k in range(L)]
    blk = pl.BlockSpec((None, tr, C), lambda l, i: (l, i, 0))
    return pl.pallas_call(
        body, name=name, grid=(L, A // tr), in_specs=g_specs + [blk, blk, blk],
        out_specs=[blk] * 4, out_shape=[_sds((L, A, C), F32)] * 4, compiler_params=_params(("parallel", "parallel")),
    )(*gsrcs, w, m, v)


MESH = pl.DeviceIdType.MESH
HBM_SPEC = pl.BlockSpec(memory_space=pltpu.HBM)
N_PEERS = N_DEV - 1


def _mesh_place():
    x, y, c = lax.axis_index("x"), lax.axis_index("y"), lax.axis_index("c")
    peers = []
    for p in range(1, N_DEV):
        px = 1 - x if p & 4 else x
        py = 1 - y if p & 2 else y
        pc = 1 - c if p & 1 else c
        peers.append(((px, py, pc), 4 * px + 2 * py + pc))
    return 4 * x + 2 * y + c, peers


SEM_SPEC = pl.BlockSpec(memory_space=pltpu.SEMAPHORE)
ANY_SPEC = pl.BlockSpec(memory_space=pl.ANY)
EFFECT = pltpu.SideEffectType.DATAFLOW_SIDE_EFFECTING


def _exchange_refs(scatter, src, land, send, recv, k, p, dev, idx, me):
    return pltpu.make_async_remote_copy(src_ref=src[k].at[idx] if scatter else src[k], dst_ref=land[k].at[me],
                                        send_sem=send.at[k * N_PEERS + p], recv_sem=recv.at[k * N_PEERS + p], device_id=dev,
                                        device_id_type=MESH)


def _exchange_start(name, srcs, scatter, gate):
    n = len(srcs)
    lands = [lax.empty(s.shape if scatter else (N_DEV,) + s.shape, s.dtype) for s in srcs]

    def body(*refs):
        src, land = refs[:n], refs[n:2 * n]
        send, recv, own = refs[2 * n + 1:2 * n + 4]
        token = refs[-1]
        me, peers = _mesh_place()
        for k in range(n):
            pltpu.make_async_copy(src[k].at[me] if scatter else src[k], land[k].at[me], own.at[k]).start()
            for p, (dev, idx) in enumerate(peers):
                _exchange_refs(scatter, src, land, send, recv, k, p, dev, idx, me).start()
        token[...] = jnp.zeros_like(token)

    hbm = lambda a: pltpu.HBM(a.shape, a.dtype)
    outs = pl.pallas_call(
        body, name=name,
        out_shape=(pltpu.SemaphoreType.DMA((n * N_PEERS,)), pltpu.SemaphoreType.DMA((n * N_PEERS,)),
                   pltpu.SemaphoreType.DMA((n,)), *[hbm(s) for s in srcs], *[hbm(s) for s in lands], _sds(DEP_SHAPE, F32)),
        in_specs=[HBM_SPEC] * (2 * n) + [ANY_SPEC],
        out_specs=(SEM_SPEC, SEM_SPEC, SEM_SPEC, *[HBM_SPEC] * (2 * n), pl.BlockSpec(memory_space=pltpu.VMEM)),
        input_output_aliases={j: 3 + j for j in range(2 * n)},
        compiler_params=pltpu.CompilerParams(has_side_effects=EFFECT),
    )(*[pltpu.with_memory_space_constraint(s, pltpu.HBM) for s in srcs],
      *[pltpu.with_memory_space_constraint(s, pltpu.HBM) for s in lands], gate)
    return outs[:3], None, list(outs[3:3 + n]), list(outs[3 + n:3 + 2 * n]), outs[-1]


def _exchange_wait(name, started, scatter, after):
    (send, recv, own), _, srcs, lands, _ = started
    n = len(srcs)

    def body(*refs):
        src, land = refs[:n], refs[n:2 * n]
        send, recv, own = refs[2 * n:2 * n + 3]
        me, peers = _mesh_place()
        for k in range(n):
            pltpu.make_async_copy(src[k].at[me] if scatter else src[k], land[k].at[me], own.at[k]).wait()
            for p, (dev, idx) in enumerate(peers):
                cp = pltpu.make_async_remote_copy(src_ref=src[k].at[idx] if scatter else src[k], dst_ref=land[k].at[idx],
                                                  send_sem=send.at[k * N_PEERS + p], recv_sem=recv.at[k * N_PEERS + p], device_id=dev,
                                                  device_id_type=MESH)
                cp.wait_send()
                cp.wait_recv()

    hbm = lambda a: pltpu.HBM(a.shape, a.dtype)
    outs = pl.pallas_call(
        body, name=name, out_shape=(*[hbm(s) for s in srcs], *[hbm(s) for s in lands]),
        in_specs=[HBM_SPEC] * (2 * n) + [SEM_SPEC, SEM_SPEC, SEM_SPEC, ANY_SPEC], out_specs=tuple([HBM_SPEC] * (2 * n)),
        input_output_aliases={j: j for j in range(2 * n)},
        compiler_params=pltpu.CompilerParams(has_side_effects=EFFECT),
    )(*srcs, *lands, send, recv, own, after)
    return list(outs[n:])


def _allsum_rows(name, part):
    R, C = part.shape

    def body(p_ref, o_ref, gath, send, recv):
        me, peers = _mesh_place()
        gath[me] = p_ref[...]
        sends = []
        for p, (dev, _) in enumerate(peers):
            cp = pltpu.make_async_remote_copy(src_ref=p_ref, dst_ref=gath.at[me], send_sem=send.at[p], recv_sem=recv.at[p],
                                              device_id=dev, device_id_type=MESH)
            cp.start()
            sends.append(cp)
        for p, (dev, idx) in enumerate(peers):
            pltpu.make_async_remote_copy(src_ref=p_ref, dst_ref=gath.at[idx], send_sem=send.at[p], recv_sem=recv.at[p],
                                         device_id=dev, device_id_type=MESH).wait_recv()
        for cp in sends:
            cp.wait_send()
        tot = gath[0]
        for j in range(1, N_DEV):
            tot = tot + gath[j]
        o_ref[...] = tot

    vm = pl.BlockSpec(memory_space=pltpu.VMEM)
    return pl.pallas_call(
        body, name=name, in_specs=[vm], out_specs=vm, out_shape=_sds((R, C), F32),
        scratch_shapes=[pltpu.VMEM((N_DEV, R, C), F32), pltpu.SemaphoreType.DMA((N_PEERS,)),
                        pltpu.SemaphoreType.DMA((N_PEERS,))],
        compiler_params=pltpu.CompilerParams(vmem_limit_bytes=VMEM_LIMIT),
    )(part)


def _rows(a, width=D_MODEL):
    flat = a.reshape(-1)
    return jnp.pad(flat, (0, (-flat.shape[0]) % width)).reshape(-1, width)


def _pack_rows(parts):
    blocks = []
    for p in parts:
        r = _rows(p)
        blocks.append(jnp.pad(r, ((0, (-r.shape[0]) % 8), (0, 0))))
    return jnp.concatenate(blocks, axis=0)


def _unpack_rows(rows, shapes):
    out, at = [], 0
    for s in shapes:
        size = int(np.prod(s))
        n = -(-size // D_MODEL)
        out.append(rows[at:at + n].reshape(-1)[:size].reshape(s))
        at += -(-n // 8) * 8
    return out


def kernel(x, norm_mix, norm_ffn, norm_final, attn_w_in, attn_w_out, attn_sinks, hgrn_w_in, hgrn_w_out, hgrn_norm, hgrn_lb_logits, ffn_w_up, ffn_conv_w, ffn_conv_b, ffn_w_down, loss_target, m_norm_mix, m_norm_ffn, m_norm_final, m_attn_w_in, m_attn_w_out, m_attn_sinks, m_hgrn_w_in, m_hgrn_w_out, m_hgrn_norm, m_hgrn_lb_logits, m_ffn_w_up, m_ffn_conv_w, m_ffn_conv_b, m_ffn_w_down, v_norm_mix, v_norm_ffn, v_norm_final, v_attn_w_in, v_attn_w_out, v_attn_sinks, v_hgrn_w_in, v_hgrn_w_out, v_hgrn_norm, v_hgrn_lb_logits, v_ffn_w_up, v_ffn_conv_w, v_ffn_conv_b, v_ffn_w_down):
    S = x.shape[1]
    n_attn, n_hgrn = attn_w_in.shape[0], hgrn_w_in.shape[0]
    me = 4 * lax.axis_index("x") + 2 * lax.axis_index("y") + lax.axis_index("c")

    wa_in_t, wa_out_b = attn_w_in.transpose(0, 2, 1).astype(BF16), attn_w_out.astype(BF16)
    wh_in_b, wh_out_b = hgrn_w_in.astype(BF16), hgrn_w_out.astype(BF16)
    wf_up_b, wf_down_b = ffn_w_up.astype(BF16), ffn_w_down.astype(BF16)
    conv_b = ffn_conv_b.reshape(DEPTH, 2, 4, 1, FF_SLOT)
    lb = _lb_fwd("lb_fwd", hgrn_lb_logits)

    def unit_shards(l, part):
        if part == "ffn":
            return [wf_up_b[l], wf_down_b[l], ffn_conv_w[l]]
        return [wa_in_t[l // 2], wa_out_b[l // 2]] if l % 2 == 0 else [wh_in_b[l // 2], wh_out_b[l // 2]]

    def unit_weights(l, part, w):
        if part == "ffn":
            return w[0][None], w[1].reshape(1, 4, FF_SLOT, D_MODEL), w[2].reshape(2, 4, 3, FF_SLOT)
        if l % 2 == 0:
            return w[0].reshape(1, ATTN_IN, D_MODEL), w[1].reshape(1, D_MODEL, D_MODEL)
        return w[0][None], w[1].reshape(1, D_MODEL, D_MODEL)

    units = [(l, part) for l in range(DEPTH) for part in ("mix", "ffn")]
    started = _exchange_start("gather_start0", unit_shards(*units[0]), False, norm_final)
    arrived = _exchange_wait("gather_wait0", started, False, started[4])
    weights, saved = {}, [dict() for _ in range(DEPTH)]
    h = x[0]
    for n, (l, part) in enumerate(units):
        i, sv = l // 2, saved[l]
        weights[l, part] = w = unit_weights(l, part, arrived)
        started = dep = None
        if n + 1 < len(units):
            started = _exchange_start(f"gather_start{n + 1}", unit_shards(*units[n + 1]), False, arrived[0])
            dep = started[4]
        if part == "mix":
            sv["h"] = h
            sv["hn"] = hn = _rmsnorm_fwd(f"norm_mix_fwd{l}", h, norm_mix[l:l + 1], dep)
            if l % 2 == 0:
                sv["proj"] = _proj_rows(f"attn_proj{i}", hn, w[0], 0, BF16)
                sv["o"] = _attn_fwd(f"attn_fwd{i}", sv["proj"], attn_sinks[i:i + 1])
                h = _out_proj(f"attn_out{i}", sv["o"], w[1], 0, h)
            else:
                sv["z"] = _proj_slots(f"hgrn_proj{i}", hn, w[0], 0).reshape(4, 2, S, HG_SLOT)
                sv["o"], sv["states"] = _hg_fwd(f"hgrn_fwd{i}", sv["z"], lb[i:i + 1], hgrn_norm[i:i + 1])
                h = _out_proj(f"hgrn_out{i}", sv["o"], w[1], 0, h)
        else:
            sv["h2"] = h
            sv["hn2"] = _rmsnorm_fwd(f"norm_ffn_fwd{l}", h, norm_ffn[l:l + 1], dep)
            sv["u"] = _proj_slots(f"ffn_up{l}", sv["hn2"], w[0], 0).reshape(2, 4, S, FF_SLOT)
            sv["a"] = _convgate_fwd(f"ffn_gate{l}", sv["u"], w[2], conv_b[l])
            h = _down_proj(f"ffn_down{l}", sv["a"], w[1], 0, h)
        if started is not None:
            arrived = _exchange_wait(f"gather_wait{n + 1}", started, False, h)
    dh, d_norm_final, loss_rows = _loss_head("loss_head", h, norm_final[None], loss_target[0])

    d_conv_w, d_conv_b, d_norm_mix, d_norm_ffn = [None] * DEPTH, [None] * DEPTH, [None] * DEPTH, [None] * DEPTH
    d_sinks, d_lb, d_hgrn_norm = [None] * n_attn, [None] * n_hgrn, [None] * n_hgrn
    received, started, before = {}, None, None
    for l, part in reversed(units):
        i, sv, w = l // 2, saved[l], weights[l, part]
        dep = None if started is None else started[4]
        if part == "ffn":
            da = _dgrad_down(f"ffn_down_dgrad{l}", dh, w[1], 0, dep)
            g_down = _wgrad_down(f"ffn_down_wgrad{l}", sv["a"], dh).reshape(N_DEV, D_FF // N_DEV, D_MODEL)
            du, d_conv_w[l], d_conv_b[l] = _convgate_bwd(f"ffn_gate_bwd{l}", sv["u"], w[2], conv_b[l], da)
            du = du.reshape(N_DEV, S, FF_SLOT)
            dhn2 = _dgrad_slots(f"ffn_up_dgrad{l}", du, w[0], 0)
            grads = [_wgrad_slots(f"ffn_up_wgrad{l}", sv["hn2"], du), g_down]
            dh, d_norm_ffn[l] = _rmsnorm_bwd(f"norm_ffn_bwd{l}", sv["h2"], norm_ffn[l:l + 1], dhn2, dh)
        else:
            if l % 2 == 0:
                do = _dgrad_out(f"attn_out_dgrad{i}", dh, w[1], 0, BF16, dep)
                g_out = _wgrad_rows(f"attn_out_wgrad{i}", sv["o"], dh)
                dproj, d_sinks[i] = _attn_bwd(f"attn_bwd{i}", sv["proj"], attn_sinks[i:i + 1], do)
                dhn = _dgrad_rows(f"attn_proj_dgrad{i}", dproj, w[0], 0)
                g_in = _wgrad_rows(f"attn_proj_wgrad{i}", dproj, sv["hn"]).reshape(N_DEV, ATTN_IN // N_DEV, D_MODEL)
            else:
                dog = _dgrad_out(f"hgrn_out_dgrad{i}", dh, w[1], 0, F32, dep)
                g_out = _wgrad_rows(f"hgrn_out_wgrad{i}", sv["o"], dh)
                dz, d_lb[i], dng = _hg_bwd(f"hgrn_bwd{i}", sv["z"], lb[i:i + 1], hgrn_norm[i:i + 1], sv["states"], dog)
                d_hgrn_norm[i] = dng[0] + dng[1]
                dz = dz.reshape(N_DEV, S, HG_SLOT)
                dhn = _dgrad_slots(f"hgrn_proj_dgrad{i}", dz, w[0], 0)
                g_in = _wgrad_slots(f"hgrn_proj_wgrad{i}", sv["hn"], dz)
            grads = [g_in, g_out.reshape(N_DEV, D_MODEL // N_DEV, D_MODEL)]
            dh, d_norm_mix[l] = _rmsnorm_bwd(f"norm_mix_bwd{l}", sv["h"], norm_mix[l:l + 1], dhn, dh)
        gate = dh
        if started is not None:
            received[before] = _exchange_wait(f"scatter_wait_{before[1]}{before[0]}", started, True, dh)
            gate = received[before][0]
        started, before = _exchange_start(f"scatter_start_{part}{l}", grads, True, gate), (l, part)
    received[before] = _exchange_wait(f"scatter_wait_{before[1]}{before[0]}", started, True, started[4])
    grad_x = dh[None]

    small_shapes = [(DEPTH, D_MODEL), (DEPTH, D_MODEL), (1, D_MODEL), (1, D_MODEL), (n_hgrn, D_MODEL), (n_attn, 128),
                    (n_hgrn, HG_K), (DEPTH, 2 * D_FF), (DEPTH, N_DEV, 3, FF_SLOT)]
    total = _allsum_rows("allsum_small", _pack_rows([
        jnp.concatenate(d_norm_mix), jnp.concatenate(d_norm_ffn), d_norm_final, loss_rows, jnp.concatenate(d_lb),
        jnp.concatenate(d_sinks), jnp.concatenate(d_hgrn_norm), jnp.stack(d_conv_b), jnp.stack(d_conv_w)]))
    (g_norm_mix, g_norm_ffn, g_norm_final, loss_sum, g_lb, g_sinks, g_hgrn_norm, g_conv_b, g_conv_w_all) = _unpack_rows(
        total, small_shapes)
    loss = jnp.sum(loss_sum)
    g_norm_final = g_norm_final[0]
    g_sinks = g_sinks[:, :N_Q_HEADS]
    g_lb_logits = _lb_bwd("lb_bwd", hgrn_lb_logits, g_lb)
    g_conv_w = lax.dynamic_index_in_dim(g_conv_w_all, me, axis=1, keepdims=False)

    attn_layers, hgrn_layers = range(0, DEPTH, 2), range(1, DEPTH, 2)
    g_attn_in_t = [_sum_sources(f"sum_attn_in{l // 2}", received[l, "mix"][0]).T[None] for l in attn_layers]
    big = {
        "attn_w_in": _adamw("adamw_attn_in", g_attn_in_t, attn_w_in, m_attn_w_in, v_attn_w_in),
        "attn_w_out": _adamw("adamw_attn_out", [received[l, "mix"][1] for l in attn_layers], attn_w_out, m_attn_w_out, v_attn_w_out),
        "hgrn_w_in": _adamw("adamw_hgrn_in", [received[l, "mix"][0] for l in hgrn_layers], hgrn_w_in, m_hgrn_w_in, v_hgrn_w_in),
        "hgrn_w_out": _adamw("adamw_hgrn_out", [received[l, "mix"][1] for l in hgrn_layers], hgrn_w_out, m_hgrn_w_out, v_hgrn_w_out),
        "ffn_w_up": _adamw("adamw_ffn_up", [received[l, "ffn"][0] for l in range(DEPTH)], ffn_w_up, m_ffn_w_up, v_ffn_w_up),
        "ffn_w_down": _adamw("adamw_ffn_down", [received[l, "ffn"][1] for l in range(DEPTH)], ffn_w_down, m_ffn_w_down, v_ffn_w_down),
        "ffn_conv_w": _adamw("adamw_conv_w", [g_conv_w[l][None] for l in range(DEPTH)], ffn_conv_w, m_ffn_conv_w, v_ffn_conv_w),
    }
    small_w = [norm_mix, norm_ffn, norm_final, attn_sinks, hgrn_norm, hgrn_lb_logits, ffn_conv_b]
    small_m = [m_norm_mix, m_norm_ffn, m_norm_final, m_attn_sinks, m_hgrn_norm, m_hgrn_lb_logits, m_ffn_conv_b]
    small_v = [v_norm_mix, v_norm_ffn, v_norm_final, v_attn_sinks, v_hgrn_norm, v_hgrn_lb_logits, v_ffn_conv_b]
    small_g = [g_norm_mix, g_norm_ffn, g_norm_final, g_sinks, g_hgrn_norm, g_lb_logits, g_conv_b]
    outs = _adamw("adamw_small", [_pack_rows(small_g)[None]], *[_pack_rows(t)[None] for t in (small_w, small_m, small_v)])
    outs = [o[0] for o in outs]
    shapes = [w.shape for w in small_w]
    small = {n: [t[j] for t in [_unpack_rows(o, shapes) for o in outs]]
             for j, n in enumerate(["norm_mix", "norm_ffn", "norm_final", "attn_sinks", "hgrn_norm", "hgrn_lb_logits", "ffn_conv_b"])}
    order = ["norm_mix", "norm_ffn", "norm_final", "attn_w_in", "attn_w_out", "attn_sinks", "hgrn_w_in", "hgrn_w_out",
             "hgrn_norm", "hgrn_lb_logits", "ffn_w_up", "ffn_conv_w", "ffn_conv_b", "ffn_w_down"]
    res = {**big, **small}
    return (loss, grad_x, *[res[n][0] for n in order], *[res[n][1] for n in order], *[res[n][2] for n in order],
            *[res[n][3] for n in order])
```

```python
import numpy as np
import jax
import jax.numpy as jnp
from jax import lax
from jax.experimental import pallas as pl
from jax.experimental.pallas import tpu as pltpu

F32 = jnp.float32
BF16 = jnp.bfloat16

D_MODEL = 1024
DEPTH = 4
HEAD_DIM = 64
N_Q_HEADS = 16
N_KV_HEADS = 4
Q_PER_KV = 4
ATTN_BLOCK = 128
ATTN_IN = 1536
HG_HEADS = 8
HG_K = 128
HG_CHUNK = 64
HG_IN = 4096
D_FF = 2816
EPS = 1e-6
N_DEV = 8
FF_SLOT = 2 * D_FF // N_DEV
HG_SLOT = HG_IN // N_DEV
HG_LEVELS = 6

ADAM_LR = 0.001
ADAM_B1 = 0.9
ADAM_B2 = 0.999
ADAM_EPS = 1e-08
ADAM_WD = 0.01
ADAM_STEP = 10

VMEM_LIMIT = 56 * 1024 * 1024
ROW_TILE = 1024
WIDE_ROW_TILE = 2048
NEG_BIG = -1e30

NN = (((1,), (0,)), ((), ()))
NT = (((1,), (1,)), ((), ()))
TN = (((0,), (0,)), ((), ()))


def _bdot(a, b, dn):
    return lax.dot_general(a.astype(BF16), b.astype(BF16), dn, preferred_element_type=F32)


def _sds(shape, dtype):
    return jax.ShapeDtypeStruct(tuple(shape), dtype)


def _params(sem):
    return pltpu.CompilerParams(dimension_semantics=sem, vmem_limit_bytes=VMEM_LIMIT)


DEP_SHAPE = (8, 128)


def _dep_spec(rank):
    return pl.BlockSpec(DEP_SHAPE, lambda *_: (0, 0))


def _matmul(name, a, b, *, dn, grid, a_spec, b_spec, o_spec, out_shape, acc_shape=None, extra=(), extra_specs=(),
            finish=None, dep=None, sem=("parallel", "parallel", "arbitrary")):
    nk = grid[2]
    many = isinstance(out_shape, (list, tuple))
    n_in = 2 + len(extra) + (dep is not None)
    n_out = len(out_shape) if many else 1

    def body(*refs):
        a_ref, b_ref = refs[0], refs[1]
        outs = refs[n_in:n_in + n_out]

        def prod():
            return _bdot(a_ref[...], b_ref[...], dn)

        def done(v):
            if finish is None:
                outs[0][...] = v.astype(outs[0].dtype)
            else:
                finish(v, refs[2:2 + len(extra)], outs)

        if nk == 1:
            done(prod())
        else:
            acc = refs[-1]
            k = pl.program_id(2)

            @pl.when(k == 0)
            def _():
                acc[...] = prod()

            @pl.when(k > 0)
            def _():
                acc[...] += prod()

            @pl.when(k == nk - 1)
            def _():
                done(acc[...])

    in_specs = [a_spec, b_spec, *extra_specs] + ([_dep_spec(3)] if dep is not None else [])
    args = (a, b, *extra) + ((dep,) if dep is not None else ())
    scratch = [] if nk == 1 else [pltpu.VMEM(acc_shape, F32)]
    return pl.pallas_call(
        body, name=name, grid=grid, in_specs=in_specs, out_specs=o_spec, out_shape=out_shape,
        scratch_shapes=scratch, compiler_params=_params(sem),
    )(*args)


def _rms(x):
    return lax.rsqrt(jnp.mean(x * x, axis=-1, keepdims=True) + EPS)


def _residual_finish(v, ex, outs):
    h = v + ex[0][...]
    outs[0][...] = h
    if len(ex) > 1:
        outs[1][...] = (h * _rms(h) * ex[1][...]).astype(outs[1].dtype)


def _norm_bwd_finish(v, ex, outs):
    x = ex[0][...]
    r = _rms(x)
    xh = x * r
    dyg = v * ex[1][...]
    outs[0][...] = ex[2][...] + r * (dyg - xh * jnp.mean(dyg * xh, axis=-1, keepdims=True))
    part = jnp.sum(v * xh, axis=0, keepdims=True)

    @pl.when(pl.program_id(0) == 0)
    def _():
        outs[1][...] = part

    @pl.when(pl.program_id(0) > 0)
    def _():
        outs[1][...] += part


def _row_io(tm, norm_g):
    row = pl.BlockSpec((tm, D_MODEL), lambda i, j, k: (i, 0))
    vec = pl.BlockSpec((1, D_MODEL), lambda i, j, k: (0, 0))
    if norm_g is None:
        return (row,), row, lambda S: _sds((S, D_MODEL), F32)
    return (row, vec), [row, row], lambda S: [_sds((S, D_MODEL), F32), _sds((S, D_MODEL), BF16)]


def _tile(n, t):
    return min(n, t)


def _proj_rows(name, hn, wt, l, out_dtype, dep=None):
    S, N = hn.shape[0], wt.shape[1]
    tm, tn = _tile(S, ROW_TILE), 512
    return _matmul(
        name, hn, wt, dn=NT, grid=(S // tm, N // tn, 1),
        a_spec=pl.BlockSpec((tm, D_MODEL), lambda i, j, k: (i, 0)),
        b_spec=pl.BlockSpec((None, tn, D_MODEL), lambda i, j, k: (l, j, 0)),
        o_spec=pl.BlockSpec((tm, tn), lambda i, j, k: (i, j)),
        out_shape=_sds((S, N), out_dtype), dep=dep)


def _slot_weight(w, transposed):
    if transposed:
        return w.shape[2], (None, None, w.shape[2], D_MODEL), NT, NN
    return w.shape[3], (None, None, D_MODEL, w.shape[3]), NN, NT


def _proj_slots(name, hn, w, l, transposed=False, dep=None):
    S = hn.shape[0]
    r, blk, dn, _ = _slot_weight(w, transposed)
    tm = _tile(S, WIDE_ROW_TILE)
    return _matmul(
        name, hn, w, dn=dn, grid=(N_DEV, S // tm, 1),
        a_spec=pl.BlockSpec((tm, D_MODEL), lambda j, i, k: (i, 0)),
        b_spec=pl.BlockSpec(blk, lambda j, i, k: (l, j, 0, 0)),
        o_spec=pl.BlockSpec((None, tm, r), lambda j, i, k: (j, i, 0)),
        out_shape=_sds((N_DEV, S, r), F32), dep=dep)


def _out_proj(name, o, w, l, h, norm_g=None):
    S, K = o.shape
    tm = _tile(S, ROW_TILE)
    extra_specs, o_spec, out_shape = _row_io(tm, norm_g)
    return _matmul(
        name, o, w, dn=NN, grid=(S // tm, 1, 1),
        a_spec=pl.BlockSpec((tm, K), lambda i, j, k: (i, 0)),
        b_spec=pl.BlockSpec((None, K, D_MODEL), lambda i, j, k: (l, 0, 0)),
        o_spec=o_spec, out_shape=out_shape(S), extra=(h,) if norm_g is None else (h, norm_g),
        extra_specs=extra_specs, finish=_residual_finish)


def _down_proj(name, a, w, l, h, norm_g=None):
    nj, S, r = a.shape
    tm = _tile(S, ROW_TILE)
    extra_specs, o_spec, out_shape = _row_io(tm, norm_g)
    return _matmul(
        name, a, w, dn=NN, grid=(S // tm, 1, nj),
        a_spec=pl.BlockSpec((None, tm, r), lambda i, j, k: (k, i, 0)),
        b_spec=pl.BlockSpec((None, None, r, D_MODEL), lambda i, j, k: (l, k, 0, 0)),
        o_spec=o_spec, out_shape=out_shape(S), acc_shape=(tm, D_MODEL),
        extra=(h,) if norm_g is None else (h, norm_g), extra_specs=extra_specs, finish=_residual_finish)


def _dgrad_down(name, dh, w, l, dep=None):
    S = dh.shape[0]
    nj, r = w.shape[1], w.shape[2]
    tm = _tile(S, ROW_TILE)
    return _matmul(
        name, dh, w, dn=NT, grid=(nj, S // tm, 1),
        a_spec=pl.BlockSpec((tm, D_MODEL), lambda j, i, k: (i, 0)),
        b_spec=pl.BlockSpec((None, None, r, D_MODEL), lambda j, i, k: (l, j, 0, 0)),
        o_spec=pl.BlockSpec((None, tm, r), lambda j, i, k: (j, i, 0)),
        out_shape=_sds((nj, S, r), BF16), dep=dep)


def _wgrad_down(name, a, dh):
    nj, S, r = a.shape
    tk = _tile(S, ROW_TILE)
    return _matmul(
        name, a, dh, dn=TN, grid=(nj, 1, S // tk),
        a_spec=pl.BlockSpec((None, tk, r), lambda s, j, k: (s, k, 0)),
        b_spec=pl.BlockSpec((tk, D_MODEL), lambda s, j, k: (k, 0)),
        o_spec=pl.BlockSpec((None, r, D_MODEL), lambda s, j, k: (s, 0, 0)),
        out_shape=_sds((nj, r, D_MODEL), BF16), acc_shape=(r, D_MODEL))


def _norm_bwd_io(tm, S):
    row = pl.BlockSpec((tm, D_MODEL), lambda i, j, k: (i, 0))
    vec = pl.BlockSpec((1, D_MODEL), lambda i, j, k: (0, 0))
    return dict(extra_specs=(row, vec, row), o_spec=[row, vec], out_shape=[_sds((S, D_MODEL), F32), _sds((1, D_MODEL), F32)],
                finish=_norm_bwd_finish, sem=("arbitrary", "arbitrary", "arbitrary"))


def _dgrad_slots(name, dz, w, l, norm, transposed=False):
    nj, S, r = dz.shape
    _, blk, _, dn = _slot_weight(w, transposed)
    tm = _tile(S, ROW_TILE)
    return _matmul(
        name, dz, w, dn=dn, grid=(S // tm, 1, nj),
        a_spec=pl.BlockSpec((None, tm, r), lambda i, j, k: (k, i, 0)),
        b_spec=pl.BlockSpec(blk, lambda i, j, k: (l, k, 0, 0)),
        acc_shape=(tm, D_MODEL), extra=norm, **_norm_bwd_io(tm, S))


def _wgrad_slots(name, hn, dz, transposed=False):
    nj, S, r = dz.shape
    tk = _tile(S, ROW_TILE)
    hn_spec = pl.BlockSpec((tk, D_MODEL), lambda s, j, k: (k, 0))
    dz_spec = pl.BlockSpec((None, tk, r), lambda s, j, k: (s, k, 0))
    if transposed:
        return _matmul(
            name, dz, hn, dn=TN, grid=(nj, 1, S // tk), a_spec=dz_spec, b_spec=hn_spec,
            o_spec=pl.BlockSpec((None, r, D_MODEL), lambda s, j, k: (s, 0, 0)),
            out_shape=_sds((nj, r, D_MODEL), BF16), acc_shape=(r, D_MODEL))
    return _matmul(
        name, hn, dz, dn=TN, grid=(nj, 1, S // tk), a_spec=hn_spec, b_spec=dz_spec,
        o_spec=pl.BlockSpec((None, D_MODEL, r), lambda s, j, k: (s, 0, 0)),
        out_shape=_sds((nj, D_MODEL, r), BF16), acc_shape=(D_MODEL, r))


def _dgrad_out(name, dh, w, l, out_dtype, dep=None):
    S, K = dh.shape[0], w.shape[1]
    tm = _tile(S, ROW_TILE)
    return _matmul(
        name, dh, w, dn=NT, grid=(S // tm, 1, 1),
        a_spec=pl.BlockSpec((tm, D_MODEL), lambda i, j, k: (i, 0)),
        b_spec=pl.BlockSpec((None, K, D_MODEL), lambda i, j, k: (l, 0, 0)),
        o_spec=pl.BlockSpec((tm, K), lambda i, j, k: (i, 0)),
        out_shape=_sds((S, K), out_dtype), dep=dep)


def _wgrad_rows(name, a, b):
    S, K = a.shape
    tk = _tile(S, ROW_TILE)
    return _matmul(
        name, a, b, dn=TN, grid=(1, 1, S // tk),
        a_spec=pl.BlockSpec((tk, K), lambda i, j, k: (k, 0)),
        b_spec=pl.BlockSpec((tk, D_MODEL), lambda i, j, k: (k, 0)),
        o_spec=pl.BlockSpec((K, D_MODEL), lambda i, j, k: (0, 0)),
        out_shape=_sds((K, D_MODEL), BF16), acc_shape=(K, D_MODEL))


def _dgrad_rows(name, dz, wt, l, norm):
    S, N = dz.shape
    tm = _tile(S, ROW_TILE)
    return _matmul(
        name, dz, wt, dn=NN, grid=(S // tm, 1, 1),
        a_spec=pl.BlockSpec((tm, N), lambda i, j, k: (i, 0)),
        b_spec=pl.BlockSpec((None, N, D_MODEL), lambda i, j, k: (l, 0, 0)),
        extra=norm, **_norm_bwd_io(tm, S))


def _rmsnorm_fwd(name, h, g):
    S = h.shape[0]
    tm = _tile(S, ROW_TILE)

    def body(h_ref, g_ref, o_ref):
        x = h_ref[...]
        o_ref[...] = (x * _rms(x) * g_ref[...]).astype(o_ref.dtype)

    row = pl.BlockSpec((tm, D_MODEL), lambda i: (i, 0))
    return pl.pallas_call(
        body, name=name, grid=(S // tm,), in_specs=[row, pl.BlockSpec((1, D_MODEL), lambda i: (0, 0))],
        out_specs=row, out_shape=_sds((S, D_MODEL), BF16), compiler_params=_params(("parallel",)),
    )(h, g)


def _loss_head(name, h, g, target):
    S = h.shape[0]
    tm = _tile(S, ROW_TILE)

    def body(h_ref, g_ref, t_ref, dh_ref, dg_ref, ls_ref):
        x = h_ref[...]
        r = lax.rsqrt(jnp.mean(x * x, axis=-1, keepdims=True) + EPS)
        xh = x * r
        diff = xh * g_ref[...] - t_ref[...]
        dyf = diff * (1.0 / D_MODEL)
        dyg = dyf * g_ref[...]
        dh_ref[...] = r * (dyg - xh * jnp.mean(dyg * xh, axis=-1, keepdims=True))
        part = jnp.sum(dyf * xh, axis=0, keepdims=True)
        lpart = jnp.sum(diff * diff, axis=0, keepdims=True) * (0.5 / D_MODEL)

        @pl.when(pl.program_id(0) == 0)
        def _():
            dg_ref[...] = part
            ls_ref[...] = lpart

        @pl.when(pl.program_id(0) > 0)
        def _():
            dg_ref[...] += part
            ls_ref[...] += lpart

    row = pl.BlockSpec((tm, D_MODEL), lambda i: (i, 0))
    vec = pl.BlockSpec((1, D_MODEL), lambda i: (0, 0))
    return pl.pallas_call(
        body, name=name, grid=(S // tm,), in_specs=[row, vec, row], out_specs=[row, vec, vec],
        out_shape=[_sds((S, D_MODEL), F32), _sds((1, D_MODEL), F32), _sds((1, D_MODEL), F32)],
        compiler_params=_params(("arbitrary",)),
    )(h, g, target)


ATTN_SCALE = HEAD_DIM ** -0.5
ALIBI_SLOPES = [2.0 ** (-8.0 * (h + 1) / N_Q_HEADS) for h in range(N_Q_HEADS)]
K_COL = N_Q_HEADS * HEAD_DIM
KV_COLS = N_KV_HEADS * HEAD_DIM
V_COL = K_COL + KV_COLS


def _attn_masks(n):
    qi = lax.broadcasted_iota(jnp.int32, (ATTN_BLOCK, ATTN_BLOCK), 0)
    ki = lax.broadcasted_iota(jnp.int32, (ATTN_BLOCK, ATTN_BLOCK), 1)
    dist_c = (qi - ki).astype(F32)
    return dist_c + float(ATTN_BLOCK), dist_c, (ki > qi) & (n > 0), qi >= ki


def _attn_probs(raw_p, raw_c, sink, slope, masks):
    dist_p, dist_c, valid_p, valid_c = masks
    sp = jnp.where(valid_p, raw_p * ATTN_SCALE - slope * dist_p, NEG_BIG)
    sc = jnp.where(valid_c, raw_c * ATTN_SCALE - slope * dist_c, NEG_BIG)
    m = jnp.maximum(jnp.maximum(jnp.max(sp, axis=-1, keepdims=True), jnp.max(sc, axis=-1, keepdims=True)), sink)
    ep, ec, es = jnp.exp(sp - m), jnp.exp(sc - m), jnp.exp(sink - m)
    inv = 1.0 / (jnp.sum(ep, axis=-1, keepdims=True) + jnp.sum(ec, axis=-1, keepdims=True) + es)
    return ep * inv, ec * inv, es * inv


def _group_rows(ref, m):
    return jnp.concatenate([ref[:, HEAD_DIM * (Q_PER_KV * m + g):HEAD_DIM * (Q_PER_KV * m + g + 1)]
                            for g in range(Q_PER_KV)], axis=0)


def _head_rows(x, g):
    return x[ATTN_BLOCK * g:ATTN_BLOCK * (g + 1)]


def _attn_specs(nblk):
    last = nblk - 1
    kcol, vcol = K_COL // KV_COLS, V_COL // KV_COLS
    return [
        pl.BlockSpec((ATTN_BLOCK, K_COL), lambda n: (jnp.minimum(n, last), 0)),
        pl.BlockSpec((ATTN_BLOCK, KV_COLS), lambda n: (jnp.minimum(n, last), kcol)),
        pl.BlockSpec((ATTN_BLOCK, KV_COLS), lambda n: (jnp.maximum(jnp.minimum(n, last) - 1, 0), kcol)),
        pl.BlockSpec((ATTN_BLOCK, KV_COLS), lambda n: (jnp.minimum(n, last), vcol)),
        pl.BlockSpec((ATTN_BLOCK, KV_COLS), lambda n: (jnp.maximum(jnp.minimum(n, last) - 1, 0), vcol)),
    ]


def _attn_fwd(name, proj, sinks):
    S = proj.shape[0]
    nblk = S // ATTN_BLOCK

    def body(q_ref, kc_ref, kp_ref, vc_ref, vp_ref, sk_ref, o_ref):
        masks = _attn_masks(pl.program_id(0))
        for m in range(N_KV_HEADS):
            ks = slice(HEAD_DIM * m, HEAD_DIM * (m + 1))
            kp, kc, vp, vc = kp_ref[:, ks], kc_ref[:, ks], vp_ref[:, ks], vc_ref[:, ks]
            q4 = _group_rows(q_ref, m)
            raw_p, raw_c = _bdot(q4, kp, NT), _bdot(q4, kc, NT)
            pps, pcs = [], []
            for g in range(Q_PER_KV):
                hh = Q_PER_KV * m + g
                pp, pc, _ = _attn_probs(_head_rows(raw_p, g), _head_rows(raw_c, g), sk_ref[0, hh], ALIBI_SLOPES[hh], masks)
                pps.append(pp.astype(BF16))
                pcs.append(pc.astype(BF16))
            o4 = _bdot(jnp.concatenate(pps, axis=0), vp, NN) + _bdot(jnp.concatenate(pcs, axis=0), vc, NN)
            for g in range(Q_PER_KV):
                hh = Q_PER_KV * m + g
                o_ref[:, HEAD_DIM * hh:HEAD_DIM * (hh + 1)] = _head_rows(o4, g).astype(o_ref.dtype)

    return pl.pallas_call(
        body, name=name, grid=(nblk,),
        in_specs=_attn_specs(nblk) + [pl.BlockSpec(memory_space=pltpu.SMEM)],
        out_specs=pl.BlockSpec((ATTN_BLOCK, K_COL), lambda n: (n, 0)),
        out_shape=_sds((S, K_COL), BF16), compiler_params=_params(("parallel",)),
    )(proj, proj, proj, proj, proj, sinks)


def _attn_bwd(name, proj, sinks, do):
    S = proj.shape[0]
    nblk = S // ATTN_BLOCK

    def body(q_ref, kc_ref, kp_ref, vc_ref, vp_ref, do_ref, sk_ref, dz_ref, ds_ref, carry, cur, padd):
        n = pl.program_id(0)

        @pl.when(n == 0)
        def _():
            carry[...] = jnp.zeros_like(carry)
            ds_ref[...] = jnp.zeros_like(ds_ref)

        @pl.when(n < nblk)
        def _():
            masks = _attn_masks(n)
            lane = lax.broadcasted_iota(jnp.int32, (1, 128), 1)
            dsv = jnp.zeros((1, 128), F32)
            for m in range(N_KV_HEADS):
                ks = slice(HEAD_DIM * m, HEAD_DIM * (m + 1))
                kp, kc, vp, vc = kp_ref[:, ks], kc_ref[:, ks], vp_ref[:, ks], vc_ref[:, ks]
                q4, do4 = _group_rows(q_ref, m), _group_rows(do_ref, m)
                raw_p, raw_c = _bdot(q4, kp, NT), _bdot(q4, kc, NT)
                dpp4, dpc4 = _bdot(do4, vp, NT), _bdot(do4, vc, NT)
                pps, pcs, dsps, dscs = [], [], [], []
                for g in range(Q_PER_KV):
                    hh = Q_PER_KV * m + g
                    pp, pc, ps = _attn_probs(_head_rows(raw_p, g), _head_rows(raw_c, g), sk_ref[0, hh], ALIBI_SLOPES[hh], masks)
                    dpp, dpc = _head_rows(dpp4, g), _head_rows(dpc4, g)
                    delta = jnp.sum(pp * dpp, axis=-1, keepdims=True) + jnp.sum(pc * dpc, axis=-1, keepdims=True)
                    dsv = dsv + jnp.where(lane == hh, -jnp.sum(ps * delta, axis=0, keepdims=True), 0.0)
                    pps.append(pp.astype(BF16))
                    pcs.append(pc.astype(BF16))
                    dsps.append((pp * (dpp - delta)).astype(BF16))
                    dscs.append((pc * (dpc - delta)).astype(BF16))
                pp4, pc4 = jnp.concatenate(pps, axis=0), jnp.concatenate(pcs, axis=0)
                dsp4, dsc4 = jnp.concatenate(dsps, axis=0), jnp.concatenate(dscs, axis=0)
                dq4 = (_bdot(dsp4, kp, NN) + _bdot(dsc4, kc, NN)) * ATTN_SCALE
                for g in range(Q_PER_KV):
                    hh = Q_PER_KV * m + g
                    cur[:, HEAD_DIM * hh:HEAD_DIM * (hh + 1)] = _head_rows(dq4, g)
                cur[:, K_COL + HEAD_DIM * m:K_COL + HEAD_DIM * (m + 1)] = _bdot(dsc4, q4, TN) * ATTN_SCALE
                cur[:, V_COL + HEAD_DIM * m:V_COL + HEAD_DIM * (m + 1)] = _bdot(pc4, do4, TN)
                padd[:, ks] = _bdot(dsp4, q4, TN) * ATTN_SCALE
                padd[:, KV_COLS + HEAD_DIM * m:KV_COLS + HEAD_DIM * (m + 1)] = _bdot(pp4, do4, TN)
            ds_ref[...] += dsv
            dz_ref[:, :K_COL] = carry[:, :K_COL].astype(dz_ref.dtype)
            dz_ref[:, K_COL:] = (carry[:, K_COL:] + padd[...]).astype(dz_ref.dtype)
            carry[...] = cur[...]

        @pl.when(n == nblk)
        def _():
            dz_ref[...] = carry[...].astype(dz_ref.dtype)

    return pl.pallas_call(
        body, name=name, grid=(nblk + 1,),
        in_specs=_attn_specs(nblk) + [
            pl.BlockSpec((ATTN_BLOCK, K_COL), lambda n: (jnp.minimum(n, nblk - 1), 0)),
            pl.BlockSpec(memory_space=pltpu.SMEM)],
        out_specs=[pl.BlockSpec((ATTN_BLOCK, ATTN_IN), lambda n: (jnp.maximum(n - 1, 0), 0)),
                   pl.BlockSpec((1, 128), lambda n: (0, 0))],
        out_shape=[_sds((S, ATTN_IN), BF16), _sds((1, 128), F32)],
        scratch_shapes=[pltpu.VMEM((ATTN_BLOCK, ATTN_IN), F32), pltpu.VMEM((ATTN_BLOCK, ATTN_IN), F32),
                        pltpu.VMEM((ATTN_BLOCK, 2 * KV_COLS), F32)],
        compiler_params=_params(("arbitrary",)),
    )(proj, proj, proj, proj, proj, do, sinks)


def _hg_consts():
    C = HG_CHUNK
    tri = np.tril(np.ones((C, C)))
    t = np.arange(C)
    rows, masks = [tri], []
    for lvl in range(HG_LEVELS):
        n = C >> (lvl + 1)
        sel = np.zeros((C, C))
        sel[t, (t // (2 * n)) * (2 * n) + n - 1] = 1.0
        rows.append(sel @ tri)
        tt, ss = t[:, None], t[None, :]
        masks.append((tt // (2 * n) == ss // (2 * n)) & ((tt // n) % 2 == 1) & ((ss // n) % 2 == 0))
    masks.append(np.eye(C, dtype=bool))
    stk = np.concatenate(rows, axis=0)
    return jnp.asarray(stk, BF16), jnp.asarray(np.stack(masks), F32)


def _sigmoid(x):
    return 1.0 / (1.0 + jnp.exp(-x))


def _split3(x):
    hi = x.astype(BF16)
    r1 = x - hi.astype(F32)
    mid = r1.astype(BF16)
    return hi, mid, (r1 - mid.astype(F32)).astype(BF16)


def _dot01(m01, x, dn):
    return sum(lax.dot_general(m01, p, dn, preferred_element_type=F32) for p in _split3(x))


def _ref_rows(b, n):
    C = b.shape[1]
    if 2 * n >= 8:
        b3 = b.reshape(HG_CHUNK // (2 * n), 2 * n, C)
        return jnp.broadcast_to(b3[:, n - 1:n, :], b3.shape).reshape(HG_CHUNK, C)
    pos = lax.broadcasted_iota(jnp.int32, b.shape, 0) % (2 * n)
    out = b
    for p in range(2 * n):
        if p != n - 1:
            out = jnp.where(pos == p, pltpu.roll(b, (p - (n - 1)) % HG_CHUNK, 0), out)
    return out


def _hg_common(z_ref, lb_ref, stk_ref):
    qr, fr = z_ref[0], z_ref[1]
    lb = lb_ref[...]
    sq, sg, sgn = _sigmoid(qr), _sigmoid(fr), _sigmoid(-fr)
    ft = lb + (1.0 - lb) * sg
    b = _dot01(stk_ref[0:HG_CHUNK, :], jnp.log(ft), NN)
    ws = [jnp.exp(-jnp.abs(b - _ref_rows(b, HG_CHUNK >> (l + 1)))) for l in range(HG_LEVELS)]
    blast = b[HG_CHUNK - 1:HG_CHUNK]
    return dict(qr=qr, fr=fr, lb=lb, sq=sq, sg=sg, sgn=sgn, ft=ft, q=qr * sq, kk=(1.0 - lb) * sgn, b=b,
                ws=ws, eb=jnp.exp(b), ed=jnp.exp(blast - b), elast=jnp.exp(blast))


def _hg_intra(qh, kh, ws, msk_ref, sl):
    a = msk_ref[HG_LEVELS] * _bdot(qh, kh, NT)
    qls, kls = [], []
    for l in range(HG_LEVELS):
        w = ws[l][:, sl]
        qls.append((qh * w).astype(BF16))
        kls.append((kh * w).astype(BF16))
        a = a + msk_ref[l] * _bdot(qls[l], kls[l], NT)
    return a, qls, kls


def _hg_fwd(name, z, lb, ng):
    S = z.shape[2]
    nc = S // HG_CHUNK
    stk, msk = _hg_consts()

    def body(z_ref, lb_ref, ng_ref, stk_ref, msk_ref, og_ref, st_ref, state):
        @pl.when(pl.program_id(1) == 0)
        def _():
            state[...] = jnp.zeros_like(state)

        cm = _hg_common(z_ref, lb_ref, stk_ref)
        v, gt = z_ref[2], z_ref[3]
        kd = cm["kk"] * cm["ed"]
        for hh in range(4):
            sl = slice(HG_K * hh, HG_K * (hh + 1))
            st = state[hh]
            st_ref[hh] = st
            qh, kh, vh = cm["q"][:, sl], cm["kk"][:, sl], v[:, sl]
            a, _, _ = _hg_intra(qh, kh, cm["ws"], msk_ref, sl)
            o = _bdot(a, vh, NN) + _bdot(qh * cm["eb"][:, sl], st, NT)
            state[hh] = cm["elast"][:, sl] * st + _bdot(vh, kd[:, sl], TN)
            r = lax.rsqrt(jnp.mean(o * o, axis=-1, keepdims=True) + EPS)
            gh = gt[:, sl]
            og_ref[:, sl] = (o * r * ng_ref[...] * (gh * _sigmoid(gh))).astype(og_ref.dtype)

    return pl.pallas_call(
        body, name=name, grid=(2, nc),
        in_specs=[pl.BlockSpec((4, None, HG_CHUNK, HG_SLOT), lambda g, c: (0, g, c, 0)),
                  pl.BlockSpec((1, HG_SLOT), lambda g, c: (0, g)),
                  pl.BlockSpec((1, HG_K), lambda g, c: (0, 0)),
                  pl.BlockSpec(stk.shape, lambda g, c: (0, 0)),
                  pl.BlockSpec(msk.shape, lambda g, c: (0, 0, 0))],
        out_specs=[pl.BlockSpec((HG_CHUNK, HG_SLOT), lambda g, c: (c, g)),
                   pl.BlockSpec((None, 4, HG_K, HG_K), lambda g, c: (c, g, 0, 0))],
        out_shape=[_sds((S, D_MODEL), BF16), _sds((nc, HG_HEADS, HG_K, HG_K), F32)],
        scratch_shapes=[pltpu.VMEM((4, HG_K, HG_K), F32)],
        compiler_params=_params(("parallel", "arbitrary")),
    )(z, lb, ng, stk, msk)


def _hg_bwd(name, z, lb, ng, states, dog):
    S = z.shape[2]
    nc = S // HG_CHUNK
    stk, msk = _hg_consts()

    def body(z_ref, lb_ref, ng_ref, stk_ref, msk_ref, st_ref, dog_ref, dz_ref, dlb_ref, dng_ref, dstate):
        @pl.when(pl.program_id(1) == 0)
        def _():
            dstate[...] = jnp.zeros_like(dstate)
            dlb_ref[...] = jnp.zeros_like(dlb_ref)
            dng_ref[...] = jnp.zeros_like(dng_ref)

        cm = _hg_common(z_ref, lb_ref, stk_ref)
        v, gt = z_ref[2], z_ref[3]
        ng = ng_ref[...]
        kd = cm["kk"] * cm["ed"]
        row = lax.broadcasted_iota(jnp.int32, (HG_CHUNK, 1), 0)
        dng = jnp.zeros((1, HG_K), F32)
        dq_h, dkk_h, db_h, dv_h, dgt_h = [], [], [], [], []
        dr_h = [[] for _ in range(HG_LEVELS)]
        for hh in range(4):
            sl = slice(HG_K * hh, HG_K * (hh + 1))
            st, dst = st_ref[hh], dstate[hh]
            qh, kh, vh, ebh, edh, kdh = cm["q"][:, sl], cm["kk"][:, sl], v[:, sl], cm["eb"][:, sl], cm["ed"][:, sl], kd[:, sl]
            elh = cm["elast"][:, sl]
            a, qls, kls = _hg_intra(qh, kh, cm["ws"], msk_ref, sl)
            qe = qh * ebh
            o = _bdot(a, vh, NN) + _bdot(qe, st, NT)
            r = lax.rsqrt(jnp.mean(o * o, axis=-1, keepdims=True) + EPS)
            xh = o * r
            gh = gt[:, sl]
            sgg = _sigmoid(gh)
            dog = dog_ref[:, sl].astype(F32)
            dy = dog * (gh * sgg)
            dgt_h.append(dog * (xh * ng) * (sgg * (1.0 + gh * (1.0 - sgg))))
            dng = dng + jnp.sum(dy * xh, axis=0, keepdims=True)
            dyg = dy * ng
            do = r * (dyg - xh * jnp.mean(dyg * xh, axis=-1, keepdims=True))
            da = _bdot(do, vh, NT)
            dv_h.append(_bdot(a, do, TN) + _bdot(kdh, dst, NT))
            dkd = _bdot(vh, dst, NN)
            delast = jnp.sum(st * dst, axis=0, keepdims=True)
            dqe = _bdot(do, st, NN)
            dstate[hh] = elh * dst + _bdot(do, qe, TN)
            gk = dkd * kdh
            dblast = jnp.sum(gk, axis=0, keepdims=True) + delast * elh
            db = dqe * qe - gk + jnp.where(row == HG_CHUNK - 1, dblast, 0.0)
            dp = (msk_ref[HG_LEVELS] * da).astype(BF16)
            dq = dqe * ebh + _bdot(dp, kh, NN)
            dkk = dkd * edh + _bdot(dp, qh, TN)
            for l in range(HG_LEVELS):
                dp = (msk_ref[l] * da).astype(BF16)
                dql, dkl = _bdot(dp, kls[l], NN), _bdot(dp, qls[l], TN)
                w = cm["ws"][l][:, sl]
                dq = dq + dql * w
                dkk = dkk + dkl * w
                half = jnp.where(((row >> (HG_LEVELS - 1 - l)) & 1) == 1, 1.0, -1.0)
                dd = half * w * (dql * qh + dkl * kh)
                db = db + dd
                dr_h[l].append(-dd)
            dq_h.append(dq)
            dkk_h.append(dkk)
            db_h.append(db)
        cat = lambda xs: jnp.concatenate(xs, axis=1)
        cot = jnp.concatenate([cat(db_h)] + [cat(dr_h[l]) for l in range(HG_LEVELS)], axis=0)
        dlf = _dot01(stk_ref[...], cot, TN)
        dq, dkk = cat(dq_h), cat(dkk_h)
        dft = dlf / cm["ft"]
        one_lb = 1.0 - cm["lb"]
        dz_ref[0] = (dq * (cm["sq"] * (1.0 + cm["qr"] * (1.0 - cm["sq"])))).astype(dz_ref.dtype)
        dz_ref[1] = ((dft - dkk) * one_lb * cm["sg"] * cm["sgn"]).astype(dz_ref.dtype)
        dz_ref[2] = cat(dv_h).astype(dz_ref.dtype)
        dz_ref[3] = cat(dgt_h).astype(dz_ref.dtype)
        dlb_ref[...] += jnp.sum((dft - dkk) * cm["sgn"], axis=0, keepdims=True)
        dng_ref[...] += dng

    rev = lambda c: nc - 1 - c
    return pl.pallas_call(
        body, name=name, grid=(2, nc),
        in_specs=[pl.BlockSpec((4, None, HG_CHUNK, HG_SLOT), lambda g, c: (0, g, rev(c), 0)),
                  pl.BlockSpec((1, HG_SLOT), lambda g, c: (0, g)),
                  pl.BlockSpec((1, HG_K), lambda g, c: (0, 0)),
                  pl.BlockSpec(stk.shape, lambda g, c: (0, 0)),
                  pl.BlockSpec(msk.shape, lambda g, c: (0, 0, 0)),
                  pl.BlockSpec((None, 4, HG_K, HG_K), lambda g, c: (rev(c), g, 0, 0)),
                  pl.BlockSpec((HG_CHUNK, HG_SLOT), lambda g, c: (rev(c), g))],
        out_specs=[pl.BlockSpec((4, None, HG_CHUNK, HG_SLOT), lambda g, c: (0, g, rev(c), 0)),
                   pl.BlockSpec((1, HG_SLOT), lambda g, c: (0, g)),
                   pl.BlockSpec((None, 1, HG_K), lambda g, c: (g, 0, 0))],
        out_shape=[_sds(z.shape, BF16), _sds((1, 2 * HG_SLOT), F32), _sds((2, 1, HG_K), F32)],
        scratch_shapes=[pltpu.VMEM((4, HG_K, HG_K), F32)],
        compiler_params=_params(("parallel", "arbitrary")),
    )(z, lb, ng, stk, msk, states, dog)


def _lb_fwd(name, logits):
    def body(l_ref, o_ref):
        x = l_ref[...]
        e = jnp.exp(x - jnp.max(x, axis=0, keepdims=True))
        s = e / jnp.sum(e, axis=0, keepdims=True)
        o_ref[0:1, :] = s[1:2]
        o_ref[1:2, :] = s[1:2] + s[2:3] + s[3:4]

    return pl.pallas_call(body, name=name, out_shape=_sds((2, logits.shape[1]), F32))(logits)


def _lb_bwd(name, logits, dlb):
    def body(l_ref, d_ref, o_ref):
        x = l_ref[...]
        e = jnp.exp(x - jnp.max(x, axis=0, keepdims=True))
        s = e / jnp.sum(e, axis=0, keepdims=True)
        d1, d3 = d_ref[0:1, :], d_ref[1:2, :]
        ds = [jnp.zeros_like(d1), d1 + d3, d3, d3]
        dot = sum(ds[r] * s[r:r + 1] for r in range(1, DEPTH))
        for r in range(DEPTH):
            o_ref[r:r + 1, :] = s[r:r + 1] * (ds[r] - dot)

    return pl.pallas_call(body, name=name, out_shape=_sds(logits.shape, F32))(logits, dlb)


SUB = 8


def _rows_down(x, prev, k):
    row = lax.broadcasted_iota(jnp.int32, x.shape, 0)
    return jnp.where(row >= k, pltpu.roll(x, k, 0), pltpu.roll(prev, k, 0))


def _rows_up(x, nxt, k):
    row = lax.broadcasted_iota(jnp.int32, x.shape, 0)
    return jnp.where(row < SUB - k, pltpu.roll(x, SUB - k, 0), pltpu.roll(nxt, SUB - k, 0))


def _conv_block(w_ref, b_ref, p, x, prev):
    x0, x1 = _rows_down(x, prev, 2), _rows_down(x, prev, 1)
    return b_ref[p] + w_ref[p, 0:1, :] * x0 + w_ref[p, 1:2, :] * x1 + w_ref[p, 2:3, :] * x, x0, x1


def _convgate_fwd(name, u, cw, cb):
    S = u.shape[2]
    tm = _tile(S, ROW_TILE)

    def body(u_ref, w_ref, b_ref, a_ref, halo):
        @pl.when(pl.program_id(1) == 0)
        def _():
            halo[...] = jnp.zeros_like(halo)

        def step(r, prev):
            pg, pv = prev
            out = []
            for s in range(2):
                rows = pl.ds(pl.multiple_of(r * 2 * SUB + s * SUB, SUB), SUB)
                xg, xv = u_ref[0, rows, :], u_ref[1, rows, :]
                cg = _conv_block(w_ref, b_ref, 0, xg, pg)[0]
                cv = _conv_block(w_ref, b_ref, 1, xv, pv)[0]
                out.append(cg * _sigmoid(cg) * cv)
                pg, pv = xg, xv
            a_ref[pl.ds(pl.multiple_of(r * 2 * SUB, 2 * SUB), 2 * SUB), :] = jnp.concatenate(out, axis=0).astype(a_ref.dtype)
            return pg, pv

        pg, pv = lax.fori_loop(0, tm // (2 * SUB), step, (halo[0], halo[1]))
        halo[0] = pg
        halo[1] = pv

    return pl.pallas_call(
        body, name=name, grid=(4, S // tm),
        in_specs=[pl.BlockSpec((2, None, tm, FF_SLOT), lambda j, t: (0, j, t, 0)),
                  pl.BlockSpec((2, None, 3, FF_SLOT), lambda j, t: (0, j, 0, 0)),
                  pl.BlockSpec((2, None, 1, FF_SLOT), lambda j, t: (0, j, 0, 0))],
        out_specs=pl.BlockSpec((None, tm, FF_SLOT), lambda j, t: (j, t, 0)),
        out_shape=_sds((4, S, FF_SLOT), BF16),
        scratch_shapes=[pltpu.VMEM((2, SUB, FF_SLOT), F32)],
        compiler_params=_params(("parallel", "arbitrary")),
    )(u, cw, cb)


def _convgate_bwd(name, u, cw, cb, da):
    S = u.shape[2]
    tm = _tile(S, ROW_TILE)
    nt = S // tm

    def body(u_ref, uh_ref, w_ref, b_ref, da_ref, du_out, dw_ref, db_ref, after, first, acc, du_ref):
        t = pl.program_id(1)

        @pl.when(t == 0)
        def _():
            after[...] = jnp.zeros_like(after)
            acc[...] = jnp.zeros_like(acc)

        def du_block(p, d, nxt):
            return w_ref[p, 2:3, :] * d + w_ref[p, 1:2, :] * _rows_up(d, nxt, 1) + w_ref[p, 0:1, :] * _rows_up(d, nxt, 2)

        def step(r, carry):
            pg, pv, dg_last, dv_last = carry
            dav = da_ref[pl.ds(pl.multiple_of(r * 2 * SUB, 2 * SUB), 2 * SUB), :].astype(F32)
            for s in range(2):
                at = r * 2 * SUB + s * SUB
                rows = pl.ds(pl.multiple_of(at, SUB), SUB)
                xg, xv = u_ref[0, rows, :], u_ref[1, rows, :]
                cg, x0g, x1g = _conv_block(w_ref, b_ref, 0, xg, pg)
                cv, x0v, x1v = _conv_block(w_ref, b_ref, 1, xv, pv)
                sg = _sigmoid(cg)
                dab = dav[s * SUB:(s + 1) * SUB]
                dg = dab * cv * (sg * (1.0 + cg * (1.0 - sg)))
                dv = dab * cg * sg
                for p, d, taps in ((0, dg, (x0g, x1g, xg)), (1, dv, (x0v, x1v, xv))):
                    for j in range(3):
                        acc[p, j] += d * taps[j]
                    acc[p, 3] += d
                if s == 0:
                    @pl.when(r == 0)
                    def _():
                        first[0] = dg
                        first[1] = dv

                    @pl.when(r > 0)
                    def _():
                        before = pl.ds(pl.multiple_of(at - SUB, SUB), SUB)
                        du_ref[0, before, :] = du_block(0, dg_last, dg)
                        du_ref[1, before, :] = du_block(1, dv_last, dv)
                else:
                    before = pl.ds(pl.multiple_of(at - SUB, SUB), SUB)
                    du_ref[0, before, :] = du_block(0, dg_last, dg)
                    du_ref[1, before, :] = du_block(1, dv_last, dv)
                pg, pv, dg_last, dv_last = xg, xv, dg, dv
            return pg, pv, dg_last, dv_last

        halo = jnp.where(t < nt - 1, uh_ref[...], 0.0)
        zero = jnp.zeros((SUB, FF_SLOT), F32)
        _, _, dg_last, dv_last = lax.fori_loop(0, tm // (2 * SUB), step, (halo[0], halo[1], zero, zero))
        du_ref[0, tm - SUB:tm, :] = du_block(0, dg_last, after[0])
        du_ref[1, tm - SUB:tm, :] = du_block(1, dv_last, after[1])
        du_out[...] = du_ref[...].astype(du_out.dtype)
        after[...] = first[...]
        for p in range(2):
            for j in range(3):
                dw_ref[p, j:j + 1, :] = jnp.sum(acc[p, j], axis=0, keepdims=True)
            db_ref[p] = jnp.sum(acc[p, 3], axis=0, keepdims=True)

    rev = lambda t: nt - 1 - t
    return pl.pallas_call(
        body, name=name, grid=(4, nt),
        in_specs=[pl.BlockSpec((2, None, tm, FF_SLOT), lambda j, t: (0, j, rev(t), 0)),
                  pl.BlockSpec((2, None, SUB, FF_SLOT), lambda j, t: (0, j, jnp.maximum(rev(t) * (tm // SUB) - 1, 0), 0)),
                  pl.BlockSpec((2, None, 3, FF_SLOT), lambda j, t: (0, j, 0, 0)),
                  pl.BlockSpec((2, None, 1, FF_SLOT), lambda j, t: (0, j, 0, 0)),
                  pl.BlockSpec((None, tm, FF_SLOT), lambda j, t: (j, rev(t), 0))],
        out_specs=[pl.BlockSpec((2, None, tm, FF_SLOT), lambda j, t: (0, j, rev(t), 0)),
                   pl.BlockSpec((2, None, 3, FF_SLOT), lambda j, t: (0, j, 0, 0)),
                   pl.BlockSpec((2, None, 1, FF_SLOT), lambda j, t: (0, j, 0, 0))],
        out_shape=[_sds(u.shape, BF16), _sds(cw.shape, F32), _sds(cb.shape, F32)],
        scratch_shapes=[pltpu.VMEM((2, SUB, FF_SLOT), F32), pltpu.VMEM((2, SUB, FF_SLOT), F32),
                        pltpu.VMEM((2, 4, SUB, FF_SLOT), F32), pltpu.VMEM((2, tm, FF_SLOT), F32)],
        compiler_params=_params(("parallel", "arbitrary")),
    )(u, u, cw, cb, da)


def _row_tile(R):
    for t in range(256, 15, -16):
        if R % t == 0:
            return t
    return R


def _adamw(name, gsrcs, w, m, v):
    L = len(gsrcs)
    n, A, C = gsrcs[0].shape
    tr = _row_tile(A)

    def body(*refs):
        g_refs = refs[:L]
        w_ref, m_ref, v_ref, go_ref, d_ref, mo_ref, vo_ref = refs[L:]
        for k in range(L):
            @pl.when(pl.program_id(0) == k)
            def _(k=k):
                g = g_refs[k][0].astype(F32)
                for s in range(1, n):
                    g = g + g_refs[k][s].astype(F32)
                m2 = ADAM_B1 * m_ref[...] + (1.0 - ADAM_B1) * g
                v2 = ADAM_B2 * v_ref[...] + (1.0 - ADAM_B2) * (g * g)
                m_hat = m2 / (1.0 - ADAM_B1 ** ADAM_STEP)
                v_hat = v2 / (1.0 - ADAM_B2 ** ADAM_STEP)
                go_ref[...] = g
                d_ref[...] = -ADAM_LR * (m_hat / (jnp.sqrt(v_hat) + ADAM_EPS) + ADAM_WD * w_ref[...])
                mo_ref[...] = m2
                vo_ref[...] = v2

    g_specs = [pl.BlockSpec((n, tr, C), lambda l, i, k=k: (0, jnp.where(l == k, i, 0), 0)) for k in range(L)]
    blk = pl.BlockSpec((None, tr, C), lambda l, i: (l, i, 0))
    return pl.pallas_call(
        body, name=name, grid=(L, A // tr), in_specs=g_specs + [blk, blk, blk],
        out_specs=[blk] * 4, out_shape=[_sds((L, A, C), F32)] * 4, compiler_params=_params(("parallel", "parallel")),
    )(*gsrcs, w, m, v)


MESH = pl.DeviceIdType.MESH
HBM_SPEC = pl.BlockSpec(memory_space=pltpu.HBM)
N_PEERS = N_DEV - 1


def _mesh_place():
    x, y, c = lax.axis_index("x"), lax.axis_index("y"), lax.axis_index("c")
    peers = []
    for p in range(1, N_DEV):
        px = 1 - x if p & 4 else x
        py = 1 - y if p & 2 else y
        pc = 1 - c if p & 1 else c
        peers.append(((px, py, pc), 4 * px + 2 * py + pc))
    return 4 * x + 2 * y + c, peers


SEM_SPEC = pl.BlockSpec(memory_space=pltpu.SEMAPHORE)
ANY_SPEC = pl.BlockSpec(memory_space=pl.ANY)
EFFECT = pltpu.SideEffectType.DATAFLOW_SIDE_EFFECTING


def _exchange_refs(scatter, src, land, send, recv, k, p, dev, idx, me):
    return pltpu.make_async_remote_copy(src_ref=src[k].at[idx] if scatter else src[k], dst_ref=land[k].at[me],
                                        send_sem=send.at[k * N_PEERS + p], recv_sem=recv.at[k * N_PEERS + p], device_id=dev,
                                        device_id_type=MESH)


def _exchange_start(name, srcs, scatter, gate):
    n = len(srcs)
    lands = [lax.empty(s.shape if scatter else (N_DEV,) + s.shape, s.dtype) for s in srcs]

    def body(*refs):
        src, land = refs[:n], refs[n:2 * n]
        send, recv, own = refs[2 * n + 1:2 * n + 4]
        token = refs[-1]
        me, peers = _mesh_place()
        for k in range(n):
            pltpu.make_async_copy(src[k].at[me] if scatter else src[k], land[k].at[me], own.at[k]).start()
            for p, (dev, idx) in enumerate(peers):
                _exchange_refs(scatter, src, land, send, recv, k, p, dev, idx, me).start()
        token[...] = jnp.zeros_like(token)

    hbm = lambda a: pltpu.HBM(a.shape, a.dtype)
    outs = pl.pallas_call(
        body, name=name,
        out_shape=(pltpu.SemaphoreType.DMA((n * N_PEERS,)), pltpu.SemaphoreType.DMA((n * N_PEERS,)),
                   pltpu.SemaphoreType.DMA((n,)), *[hbm(s) for s in srcs], *[hbm(s) for s in lands], _sds(DEP_SHAPE, F32)),
        in_specs=[HBM_SPEC] * (2 * n) + [ANY_SPEC],
        out_specs=(SEM_SPEC, SEM_SPEC, SEM_SPEC, *[HBM_SPEC] * (2 * n), pl.BlockSpec(memory_space=pltpu.VMEM)),
        input_output_aliases={j: 3 + j for j in range(2 * n)},
        compiler_params=pltpu.CompilerParams(has_side_effects=EFFECT),
    )(*[pltpu.with_memory_space_constraint(s, pltpu.HBM) for s in srcs],
      *[pltpu.with_memory_space_constraint(s, pltpu.HBM) for s in lands], gate)
    return outs[:3], None, list(outs[3:3 + n]), list(outs[3 + n:3 + 2 * n]), outs[-1]


def _exchange_wait(name, started, scatter, after):
    (send, recv, own), _, srcs, lands, _ = started
    n = len(srcs)

    def body(*refs):
        src, land = refs[:n], refs[n:2 * n]
        send, recv, own = refs[2 * n:2 * n + 3]
        me, peers = _mesh_place()
        for k in range(n):
            pltpu.make_async_copy(src[k].at[me] if scatter else src[k], land[k].at[me], own.at[k]).wait()
            for p, (dev, idx) in enumerate(peers):
                cp = pltpu.make_async_remote_copy(src_ref=src[k].at[idx] if scatter else src[k], dst_ref=land[k].at[idx],
                                                  send_sem=send.at[k * N_PEERS + p], recv_sem=recv.at[k * N_PEERS + p], device_id=dev,
                                                  device_id_type=MESH)
                cp.wait_send()
                cp.wait_recv()

    hbm = lambda a: pltpu.HBM(a.shape, a.dtype)
    outs = pl.pallas_call(
        body, name=name, out_shape=(*[hbm(s) for s in srcs], *[hbm(s) for s in lands]),
        in_specs=[HBM_SPEC] * (2 * n) + [SEM_SPEC, SEM_SPEC, SEM_SPEC, ANY_SPEC], out_specs=tuple([HBM_SPEC] * (2 * n)),
        input_output_aliases={j: j for j in range(2 * n)},
        compiler_params=pltpu.CompilerParams(has_side_effects=EFFECT),
    )(*srcs, *lands, send, recv, own, after)
    return list(outs[n:])


def _allsum_rows(name, part):
    R, C = part.shape

    def body(p_ref, o_ref, gath, send, recv):
        me, peers = _mesh_place()
        gath[me] = p_ref[...]
        sends = []
        for p, (dev, _) in enumerate(peers):
            cp = pltpu.make_async_remote_copy(src_ref=p_ref, dst_ref=gath.at[me], send_sem=send.at[p], recv_sem=recv.at[p],
                                              device_id=dev, device_id_type=MESH)
            cp.start()
            sends.append(cp)
        for p, (dev, idx) in enumerate(peers):
            pltpu.make_async_remote_copy(src_ref=p_ref, dst_ref=gath.at[idx], send_sem=send.at[p], recv_sem=recv.at[p],
                                         device_id=dev, device_id_type=MESH).wait_recv()
        for cp in sends:
            cp.wait_send()
        tot = gath[0]
        for j in range(1, N_DEV):
            tot = tot + gath[j]
        o_ref[...] = tot

    vm = pl.BlockSpec(memory_space=pltpu.VMEM)
    return pl.pallas_call(
        body, name=name, in_specs=[vm], out_specs=vm, out_shape=_sds((R, C), F32),
        scratch_shapes=[pltpu.VMEM((N_DEV, R, C), F32), pltpu.SemaphoreType.DMA((N_PEERS,)),
                        pltpu.SemaphoreType.DMA((N_PEERS,))],
        compiler_params=pltpu.CompilerParams(vmem_limit_bytes=VMEM_LIMIT),
    )(part)


def _rows(a, width=D_MODEL):
    flat = a.reshape(-1)
    return jnp.pad(flat, (0, (-flat.shape[0]) % width)).reshape(-1, width)


def _pack_rows(parts):
    blocks = []
    for p in parts:
        r = _rows(p)
        blocks.append(jnp.pad(r, ((0, (-r.shape[0]) % 8), (0, 0))))
    return jnp.concatenate(blocks, axis=0)


def _unpack_rows(rows, shapes):
    out, at = [], 0
    for s in shapes:
        size = int(np.prod(s))
        n = -(-size // D_MODEL)
        out.append(rows[at:at + n].reshape(-1)[:size].reshape(s))
        at += -(-n // 8) * 8
    return out


def kernel(x, norm_mix, norm_ffn, norm_final, attn_w_in, attn_w_out, attn_sinks, hgrn_w_in, hgrn_w_out, hgrn_norm, hgrn_lb_logits, ffn_w_up, ffn_conv_w, ffn_conv_b, ffn_w_down, loss_target, m_norm_mix, m_norm_ffn, m_norm_final, m_attn_w_in, m_attn_w_out, m_attn_sinks, m_hgrn_w_in, m_hgrn_w_out, m_hgrn_norm, m_hgrn_lb_logits, m_ffn_w_up, m_ffn_conv_w, m_ffn_conv_b, m_ffn_w_down, v_norm_mix, v_norm_ffn, v_norm_final, v_attn_w_in, v_attn_w_out, v_attn_sinks, v_hgrn_w_in, v_hgrn_w_out, v_hgrn_norm, v_hgrn_lb_logits, v_ffn_w_up, v_ffn_conv_w, v_ffn_conv_b, v_ffn_w_down):
    S = x.shape[1]
    n_attn, n_hgrn = attn_w_in.shape[0], hgrn_w_in.shape[0]
    me = 4 * lax.axis_index("x") + 2 * lax.axis_index("y") + lax.axis_index("c")

    wa_in_t, wa_out_b = attn_w_in.transpose(0, 2, 1).astype(BF16), attn_w_out.astype(BF16)
    wh_in_b, wh_out_b = hgrn_w_in.astype(BF16), hgrn_w_out.astype(BF16)
    wf_up_b, wf_down_b = ffn_w_up.transpose(0, 2, 1).astype(BF16), ffn_w_down.astype(BF16)
    conv_b = ffn_conv_b.reshape(DEPTH, 2, 4, 1, FF_SLOT)
    lb = _lb_fwd("lb_fwd", hgrn_lb_logits)

    def unit_shards(l, part):
        if part == "ffn":
            return [wf_up_b[l], wf_down_b[l], ffn_conv_w[l]]
        return [wa_in_t[l // 2], wa_out_b[l // 2]] if l % 2 == 0 else [wh_in_b[l // 2], wh_out_b[l // 2]]

    def unit_weights(l, part, w):
        if part == "ffn":
            return w[0][None], w[1].reshape(1, 4, FF_SLOT, D_MODEL), w[2].reshape(2, 4, 3, FF_SLOT)
        if l % 2 == 0:
            return w[0].reshape(1, ATTN_IN, D_MODEL), w[1].reshape(1, D_MODEL, D_MODEL)
        return w[0][None], w[1].reshape(1, D_MODEL, D_MODEL)

    units = [(l, part) for l in range(DEPTH) for part in ("mix", "ffn")]
    gathers = [_exchange_start("gather_start0", unit_shards(*units[0]), False, norm_final)]
    gathers.append(_exchange_start("gather_start1", unit_shards(*units[1]), False, gathers[0][4]))
    arrived = _exchange_wait("gather_wait0", gathers[0], False, gathers[1][4])
    weights, saved = {}, [dict() for _ in range(DEPTH)]
    h = x[0]
    hn = _rmsnorm_fwd("norm_mix_fwd0", h, norm_mix[0:1])
    for n, (l, part) in enumerate(units):
        i, sv = l // 2, saved[l]
        weights[l, part] = w = unit_weights(l, part, arrived)
        dep = None
        if n + 2 < len(units):
            gathers.append(_exchange_start(f"gather_start{n + 2}", unit_shards(*units[n + 2]), False, arrived[0]))
            dep = gathers[n + 2][4]
        if part == "mix":
            sv["h"], sv["hn"] = h, hn
            if l % 2 == 0:
                sv["proj"] = _proj_rows(f"attn_proj{i}", hn, w[0], 0, BF16, dep)
                sv["o"] = _attn_fwd(f"attn_fwd{i}", sv["proj"], attn_sinks[i:i + 1])
                h, hn = _out_proj(f"attn_out{i}", sv["o"], w[1], 0, h, norm_ffn[l:l + 1])
            else:
                sv["z"] = _proj_slots(f"hgrn_proj{i}", hn, w[0], 0, dep=dep).reshape(4, 2, S, HG_SLOT)
                sv["o"], sv["states"] = _hg_fwd(f"hgrn_fwd{i}", sv["z"], lb[i:i + 1], hgrn_norm[i:i + 1])
                h, hn = _out_proj(f"hgrn_out{i}", sv["o"], w[1], 0, h, norm_ffn[l:l + 1])
        else:
            sv["h2"], sv["hn2"] = h, hn
            sv["u"] = _proj_slots(f"ffn_up{l}", hn, w[0], 0, True, dep).reshape(2, 4, S, FF_SLOT)
            sv["a"] = _convgate_fwd(f"ffn_gate{l}", sv["u"], w[2], conv_b[l])
            if l + 1 < DEPTH:
                h, hn = _down_proj(f"ffn_down{l}", sv["a"], w[1], 0, h, norm_mix[l + 1:l + 2])
            else:
                h = _down_proj(f"ffn_down{l}", sv["a"], w[1], 0, h)
        if n + 1 < len(units):
            arrived = _exchange_wait(f"gather_wait{n + 1}", gathers[n + 1], False, h)
    dh, d_norm_final, loss_rows = _loss_head("loss_head", h, norm_final[None], loss_target[0])

    d_conv_w, d_conv_b, d_norm_mix, d_norm_ffn = [None] * DEPTH, [None] * DEPTH, [None] * DEPTH, [None] * DEPTH
    d_sinks, d_lb, d_hgrn_norm = [None] * n_attn, [None] * n_hgrn, [None] * n_hgrn
    received, started, before = {}, None, None
    for l, part in reversed(units):
        i, sv, w = l // 2, saved[l], weights[l, part]
        dep = None if started is None else started[4]
        if part == "ffn":
            da = _dgrad_down(f"ffn_down_dgrad{l}", dh, w[1], 0, dep)
            g_down = _wgrad_down(f"ffn_down_wgrad{l}", sv["a"], dh).reshape(N_DEV, D_FF // N_DEV, D_MODEL)
            du, d_conv_w[l], d_conv_b[l] = _convgate_bwd(f"ffn_gate_bwd{l}", sv["u"], w[2], conv_b[l], da)
            du = du.reshape(N_DEV, S, FF_SLOT)
            grads = [_wgrad_slots(f"ffn_up_wgrad{l}", sv["hn2"], du, True), g_down]
            dh, d_norm_ffn[l] = _dgrad_slots(f"ffn_up_dgrad{l}", du, w[0], 0, (sv["h2"], norm_ffn[l:l + 1], dh), True)
        else:
            if l % 2 == 0:
                do = _dgrad_out(f"attn_out_dgrad{i}", dh, w[1], 0, BF16, dep)
                g_out = _wgrad_rows(f"attn_out_wgrad{i}", sv["o"], dh)
                dproj, d_sinks[i] = _attn_bwd(f"attn_bwd{i}", sv["proj"], attn_sinks[i:i + 1], do)
                g_in = _wgrad_rows(f"attn_proj_wgrad{i}", dproj, sv["hn"]).reshape(N_DEV, ATTN_IN // N_DEV, D_MODEL)
                dh_new = _dgrad_rows(f"attn_proj_dgrad{i}", dproj, w[0], 0, (sv["h"], norm_mix[l:l + 1], dh))
            else:
                dog = _dgrad_out(f"hgrn_out_dgrad{i}", dh, w[1], 0, F32, dep)
                g_out = _wgrad_rows(f"hgrn_out_wgrad{i}", sv["o"], dh)
                dz, d_lb[i], dng = _hg_bwd(f"hgrn_bwd{i}", sv["z"], lb[i:i + 1], hgrn_norm[i:i + 1], sv["states"], dog)
                d_hgrn_norm[i] = dng[0] + dng[1]
                dz = dz.reshape(N_DEV, S, HG_SLOT)
                g_in = _wgrad_slots(f"hgrn_proj_wgrad{i}", sv["hn"], dz)
                dh_new = _dgrad_slots(f"hgrn_proj_dgrad{i}", dz, w[0], 0, (sv["h"], norm_mix[l:l + 1], dh))
            grads = [g_in, g_out.reshape(N_DEV, D_MODEL // N_DEV, D_MODEL)]
            dh, d_norm_mix[l] = dh_new
        gate = dh
        if started is not None:
            received[before] = _exchange_wait(f"scatter_wait_{before[1]}{before[0]}", started, True, dh)
            gate = received[before][0]
        started, before = _exchange_start(f"scatter_start_{part}{l}", grads, True, gate), (l, part)
    received[before] = _exchange_wait(f"scatter_wait_{before[1]}{before[0]}", started, True, started[4])
    grad_x = dh[None]

    small_shapes = [(DEPTH, D_MODEL), (DEPTH, D_MODEL), (1, D_MODEL), (1, D_MODEL), (n_hgrn, D_MODEL), (n_attn, 128),
                    (n_hgrn, HG_K), (DEPTH, 2 * D_FF), (DEPTH, N_DEV, 3, FF_SLOT)]
    total = _allsum_rows("allsum_small", _pack_rows([
        jnp.concatenate(d_norm_mix), jnp.concatenate(d_norm_ffn), d_norm_final, loss_rows, jnp.concatenate(d_lb),
        jnp.concatenate(d_sinks), jnp.concatenate(d_hgrn_norm), jnp.stack(d_conv_b), jnp.stack(d_conv_w)]))
    (g_norm_mix, g_norm_ffn, g_norm_final, loss_sum, g_lb, g_sinks, g_hgrn_norm, g_conv_b, g_conv_w_all) = _unpack_rows(
        total, small_shapes)
    loss = jnp.sum(loss_sum)
    g_norm_final = g_norm_final[0]
    g_sinks = g_sinks[:, :N_Q_HEADS]
    g_lb_logits = _lb_bwd("lb_bwd", hgrn_lb_logits, g_lb)
    g_conv_w = lax.dynamic_index_in_dim(g_conv_w_all, me, axis=1, keepdims=False)

    attn_layers, hgrn_layers = range(0, DEPTH, 2), range(1, DEPTH, 2)
    def transposed_update(name, gsrcs, w, m, v):
        outs = _adamw(name, gsrcs, *[t.transpose(0, 2, 1) for t in (w, m, v)])
        return [o.transpose(0, 2, 1) for o in outs]

    big = {
        "attn_w_in": transposed_update("adamw_attn_in", [received[l, "mix"][0] for l in attn_layers], attn_w_in,
                                       m_attn_w_in, v_attn_w_in),
        "attn_w_out": _adamw("adamw_attn_out", [received[l, "mix"][1] for l in attn_layers], attn_w_out, m_attn_w_out, v_attn_w_out),
        "hgrn_w_in": _adamw("adamw_hgrn_in", [received[l, "mix"][0] for l in hgrn_layers], hgrn_w_in, m_hgrn_w_in, v_hgrn_w_in),
        "hgrn_w_out": _adamw("adamw_hgrn_out", [received[l, "mix"][1] for l in hgrn_layers], hgrn_w_out, m_hgrn_w_out, v_hgrn_w_out),
        "ffn_w_up": transposed_update("adamw_ffn_up", [received[l, "ffn"][0] for l in range(DEPTH)], ffn_w_up, m_ffn_w_up,
                                      v_ffn_w_up),
        "ffn_w_down": _adamw("adamw_ffn_down", [received[l, "ffn"][1] for l in range(DEPTH)], ffn_w_down, m_ffn_w_down, v_ffn_w_down),
        "ffn_conv_w": _adamw("adamw_conv_w", [g_conv_w[l][None] for l in range(DEPTH)], ffn_conv_w, m_ffn_conv_w, v_ffn_conv_w),
    }
    small_w = [norm_mix, norm_ffn, norm_final, attn_sinks, hgrn_norm, hgrn_lb_logits, ffn_conv_b]
    small_m = [m_norm_mix, m_norm_ffn, m_norm_final, m_attn_sinks, m_hgrn_norm, m_hgrn_lb_logits, m_ffn_conv_b]
    small_v = [v_norm_mix, v_norm_ffn, v_norm_final, v_attn_sinks, v_hgrn_norm, v_hgrn_lb_logits, v_ffn_conv_b]
    small_g = [g_norm_mix, g_norm_ffn, g_norm_final, g_sinks, g_hgrn_norm, g_lb_logits, g_conv_b]
    outs = _adamw("adamw_small", [_pack_rows(small_g)[None]], *[_pack_rows(t)[None] for t in (small_w, small_m, small_v)])
    outs = [o[0] for o in outs]
    shapes = [w.shape for w in small_w]
    small = {n: [t[j] for t in [_unpack_rows(o, shapes) for o in outs]]
             for j, n in enumerate(["norm_mix", "norm_ffn", "norm_final", "attn_sinks", "hgrn_norm", "hgrn_lb_logits", "ffn_conv_b"])}
    order = ["norm_mix", "norm_ffn", "norm_final", "attn_w_in", "attn_w_out", "attn_sinks", "hgrn_w_in", "hgrn_w_out",
             "hgrn_norm", "hgrn_lb_logits", "ffn_w_up", "ffn_conv_w", "ffn_conv_b", "ffn_w_down"]
    res = {**big, **small}
    return (loss, grad_x, *[res[n][0] for n in order], *[res[n][1] for n in order], *[res[n][2] for n in order],
            *[res[n][3] for n in order])
```

```python
import numpy as np
import jax
import jax.numpy as jnp
from jax import lax
from jax.experimental import pallas as pl
from jax.experimental.pallas import tpu as pltpu

F32 = jnp.float32
BF16 = jnp.bfloat16

D_MODEL = 1024
DEPTH = 4
HEAD_DIM = 64
N_Q_HEADS = 16
N_KV_HEADS = 4
Q_PER_KV = 4
ATTN_BLOCK = 128
ATTN_IN = 1536
HG_HEADS = 8
HG_K = 128
HG_CHUNK = 64
HG_IN = 4096
D_FF = 2816
EPS = 1e-6
N_DEV = 8
FF_SLOT = 2 * D_FF // N_DEV
HG_SLOT = HG_IN // N_DEV
HG_LEVELS = 6

ADAM_LR = 0.001
ADAM_B1 = 0.9
ADAM_B2 = 0.999
ADAM_EPS = 1e-08
ADAM_WD = 0.01
ADAM_STEP = 10

VMEM_LIMIT = 56 * 1024 * 1024
ROW_TILE = 1024
WIDE_ROW_TILE = 2048
NEG_BIG = -1e30

NN = (((1,), (0,)), ((), ()))
NT = (((1,), (1,)), ((), ()))
TN = (((0,), (0,)), ((), ()))


def _bdot(a, b, dn):
    return lax.dot_general(a.astype(BF16), b.astype(BF16), dn, preferred_element_type=F32)


def _sds(shape, dtype):
    return jax.ShapeDtypeStruct(tuple(shape), dtype)


def _params(sem):
    return pltpu.CompilerParams(dimension_semantics=sem, vmem_limit_bytes=VMEM_LIMIT)


DEP_SHAPE = (8, 128)


def _dep_spec(rank):
    return pl.BlockSpec(DEP_SHAPE, lambda *_: (0, 0))


def _matmul(name, a, b, *, dn, grid, a_spec, b_spec, o_spec, out_shape, acc_shape=None, extra=(), extra_specs=(),
            finish=None, dep=None, sem=("parallel", "parallel", "arbitrary")):
    nk = grid[2]
    many = isinstance(out_shape, (list, tuple))
    n_in = 2 + len(extra) + (dep is not None)
    n_out = len(out_shape) if many else 1

    def body(*refs):
        a_ref, b_ref = refs[0], refs[1]
        outs = refs[n_in:n_in + n_out]

        def prod():
            return _bdot(a_ref[...], b_ref[...], dn)

        def done(v):
            if finish is None:
                outs[0][...] = v.astype(outs[0].dtype)
            else:
                finish(v, refs[2:2 + len(extra)], outs)

        if nk == 1:
            done(prod())
        else:
            acc = refs[-1]
            k = pl.program_id(2)

            @pl.when(k == 0)
            def _():
                acc[...] = prod()

            @pl.when(k > 0)
            def _():
                acc[...] += prod()

            @pl.when(k == nk - 1)
            def _():
                done(acc[...])

    in_specs = [a_spec, b_spec, *extra_specs] + ([_dep_spec(3)] if dep is not None else [])
    args = (a, b, *extra) + ((dep,) if dep is not None else ())
    scratch = [] if nk == 1 else [pltpu.VMEM(acc_shape, F32)]
    return pl.pallas_call(
        body, name=name, grid=grid, in_specs=in_specs, out_specs=o_spec, out_shape=out_shape,
        scratch_shapes=scratch, compiler_params=_params(sem),
    )(*args)


def _rms(x):
    return lax.rsqrt(jnp.mean(x * x, axis=-1, keepdims=True) + EPS)


def _residual_finish(v, ex, outs):
    h = v + ex[0][...]
    outs[0][...] = h
    if len(ex) > 1:
        outs[1][...] = (h * _rms(h) * ex[1][...]).astype(outs[1].dtype)


def _norm_bwd_finish(v, ex, outs):
    x = ex[0][...]
    r = _rms(x)
    xh = x * r
    dyg = v * ex[1][...]
    outs[0][...] = ex[2][...] + r * (dyg - xh * jnp.mean(dyg * xh, axis=-1, keepdims=True))
    part = jnp.sum(v * xh, axis=0, keepdims=True)

    @pl.when(pl.program_id(0) == 0)
    def _():
        outs[1][...] = part

    @pl.when(pl.program_id(0) > 0)
    def _():
        outs[1][...] += part


def _row_io(tm, norm_g):
    row = pl.BlockSpec((tm, D_MODEL), lambda i, j, k: (i, 0))
    vec = pl.BlockSpec((1, D_MODEL), lambda i, j, k: (0, 0))
    if norm_g is None:
        return (row,), row, lambda S: _sds((S, D_MODEL), F32)
    return (row, vec), [row, row], lambda S: [_sds((S, D_MODEL), F32), _sds((S, D_MODEL), BF16)]


def _tile(n, t):
    return min(n, t)


def _proj_rows(name, hn, wt, l, out_dtype, dep=None):
    S, N = hn.shape[0], wt.shape[1]
    tm, tn = _tile(S, ROW_TILE), 512
    return _matmul(
        name, hn, wt, dn=NT, grid=(S // tm, N // tn, 1),
        a_spec=pl.BlockSpec((tm, D_MODEL), lambda i, j, k: (i, 0)),
        b_spec=pl.BlockSpec((None, tn, D_MODEL), lambda i, j, k: (l, j, 0)),
        o_spec=pl.BlockSpec((tm, tn), lambda i, j, k: (i, j)),
        out_shape=_sds((S, N), out_dtype), dep=dep)


def _slot_weight(w, transposed):
    if transposed:
        return w.shape[2], (None, None, w.shape[2], D_MODEL), NT, NN
    return w.shape[3], (None, None, D_MODEL, w.shape[3]), NN, NT


def _proj_slots(name, hn, w, l, transposed=False, dep=None):
    S = hn.shape[0]
    r, blk, dn, _ = _slot_weight(w, transposed)
    tm = _tile(S, WIDE_ROW_TILE)
    return _matmul(
        name, hn, w, dn=dn, grid=(N_DEV, S // tm, 1),
        a_spec=pl.BlockSpec((tm, D_MODEL), lambda j, i, k: (i, 0)),
        b_spec=pl.BlockSpec(blk, lambda j, i, k: (l, j, 0, 0)),
        o_spec=pl.BlockSpec((None, tm, r), lambda j, i, k: (j, i, 0)),
        out_shape=_sds((N_DEV, S, r), F32), dep=dep)


def _out_proj(name, o, w, l, h, norm_g=None):
    S, K = o.shape
    tm = _tile(S, ROW_TILE)
    extra_specs, o_spec, out_shape = _row_io(tm, norm_g)
    return _matmul(
        name, o, w, dn=NN, grid=(S // tm, 1, 1),
        a_spec=pl.BlockSpec((tm, K), lambda i, j, k: (i, 0)),
        b_spec=pl.BlockSpec((None, K, D_MODEL), lambda i, j, k: (l, 0, 0)),
        o_spec=o_spec, out_shape=out_shape(S), extra=(h,) if norm_g is None else (h, norm_g),
        extra_specs=extra_specs, finish=_residual_finish)


def _down_proj(name, a, w, l, h, norm_g=None):
    nj, S, r = a.shape
    tm = _tile(S, ROW_TILE)
    extra_specs, o_spec, out_shape = _row_io(tm, norm_g)
    return _matmul(
        name, a, w, dn=NN, grid=(S // tm, 1, nj),
        a_spec=pl.BlockSpec((None, tm, r), lambda i, j, k: (k, i, 0)),
        b_spec=pl.BlockSpec((None, None, r, D_MODEL), lambda i, j, k: (l, k, 0, 0)),
        o_spec=o_spec, out_shape=out_shape(S), acc_shape=(tm, D_MODEL),
        extra=(h,) if norm_g is None else (h, norm_g), extra_specs=extra_specs, finish=_residual_finish)


def _dgrad_down(name, dh, w, l, dep=None):
    S = dh.shape[0]
    nj, r = w.shape[1], w.shape[2]
    tm = _tile(S, ROW_TILE)
    return _matmul(
        name, dh, w, dn=NT, grid=(nj, S // tm, 1),
        a_spec=pl.BlockSpec((tm, D_MODEL), lambda j, i, k: (i, 0)),
        b_spec=pl.BlockSpec((None, None, r, D_MODEL), lambda j, i, k: (l, j, 0, 0)),
        o_spec=pl.BlockSpec((None, tm, r), lambda j, i, k: (j, i, 0)),
        out_shape=_sds((nj, S, r), BF16), dep=dep)


def _wgrad_down(name, a, dh):
    nj, S, r = a.shape
    tk = _tile(S, ROW_TILE)
    return _matmul(
        name, a, dh, dn=TN, grid=(nj, 1, S // tk),
        a_spec=pl.BlockSpec((None, tk, r), lambda s, j, k: (s, k, 0)),
        b_spec=pl.BlockSpec((tk, D_MODEL), lambda s, j, k: (k, 0)),
        o_spec=pl.BlockSpec((None, r, D_MODEL), lambda s, j, k: (s, 0, 0)),
        out_shape=_sds((nj, r, D_MODEL), BF16), acc_shape=(r, D_MODEL))


def _norm_bwd_io(tm, S):
    row = pl.BlockSpec((tm, D_MODEL), lambda i, j, k: (i, 0))
    vec = pl.BlockSpec((1, D_MODEL), lambda i, j, k: (0, 0))
    return dict(extra_specs=(row, vec, row), o_spec=[row, vec], out_shape=[_sds((S, D_MODEL), F32), _sds((1, D_MODEL), F32)],
                finish=_norm_bwd_finish, sem=("arbitrary", "arbitrary", "arbitrary"))


def _dgrad_slots(name, dz, w, l, norm, transposed=False):
    nj, S, r = dz.shape
    _, blk, _, dn = _slot_weight(w, transposed)
    tm = _tile(S, ROW_TILE)
    return _matmul(
        name, dz, w, dn=dn, grid=(S // tm, 1, nj),
        a_spec=pl.BlockSpec((None, tm, r), lambda i, j, k: (k, i, 0)),
        b_spec=pl.BlockSpec(blk, lambda i, j, k: (l, k, 0, 0)),
        acc_shape=(tm, D_MODEL), extra=norm, **_norm_bwd_io(tm, S))


def _wgrad_slots(name, hn, dz, transposed=False):
    nj, S, r = dz.shape
    tk = _tile(S, ROW_TILE)
    hn_spec = pl.BlockSpec((tk, D_MODEL), lambda s, j, k: (k, 0))
    dz_spec = pl.BlockSpec((None, tk, r), lambda s, j, k: (s, k, 0))
    if transposed:
        return _matmul(
            name, dz, hn, dn=TN, grid=(nj, 1, S // tk), a_spec=dz_spec, b_spec=hn_spec,
            o_spec=pl.BlockSpec((None, r, D_MODEL), lambda s, j, k: (s, 0, 0)),
            out_shape=_sds((nj, r, D_MODEL), BF16), acc_shape=(r, D_MODEL))
    return _matmul(
        name, hn, dz, dn=TN, grid=(nj, 1, S // tk), a_spec=hn_spec, b_spec=dz_spec,
        o_spec=pl.BlockSpec((None, D_MODEL, r), lambda s, j, k: (s, 0, 0)),
        out_shape=_sds((nj, D_MODEL, r), BF16), acc_shape=(D_MODEL, r))


def _dgrad_out(name, dh, w, l, out_dtype, dep=None):
    S, K = dh.shape[0], w.shape[1]
    tm = _tile(S, ROW_TILE)
    return _matmul(
        name, dh, w, dn=NT, grid=(S // tm, 1, 1),
        a_spec=pl.BlockSpec((tm, D_MODEL), lambda i, j, k: (i, 0)),
        b_spec=pl.BlockSpec((None, K, D_MODEL), lambda i, j, k: (l, 0, 0)),
        o_spec=pl.BlockSpec((tm, K), lambda i, j, k: (i, 0)),
        out_shape=_sds((S, K), out_dtype), dep=dep)


def _wgrad_rows(name, a, b):
    S, K = a.shape
    tk = _tile(S, ROW_TILE)
    return _matmul(
        name, a, b, dn=TN, grid=(1, 1, S // tk),
        a_spec=pl.BlockSpec((tk, K), lambda i, j, k: (k, 0)),
        b_spec=pl.BlockSpec((tk, D_MODEL), lambda i, j, k: (k, 0)),
        o_spec=pl.BlockSpec((K, D_MODEL), lambda i, j, k: (0, 0)),
        out_shape=_sds((K, D_MODEL), BF16), acc_shape=(K, D_MODEL))


def _dgrad_rows(name, dz, wt, l, norm):
    S, N = dz.shape
    tm = _tile(S, ROW_TILE)
    return _matmul(
        name, dz, wt, dn=NN, grid=(S // tm, 1, 1),
        a_spec=pl.BlockSpec((tm, N), lambda i, j, k: (i, 0)),
        b_spec=pl.BlockSpec((None, N, D_MODEL), lambda i, j, k: (l, 0, 0)),
        extra=norm, **_norm_bwd_io(tm, S))


def _rmsnorm_fwd(name, h, g):
    S = h.shape[0]
    tm = _tile(S, ROW_TILE)

    def body(h_ref, g_ref, o_ref):
        x = h_ref[...]
        o_ref[...] = (x * _rms(x) * g_ref[...]).astype(o_ref.dtype)

    row = pl.BlockSpec((tm, D_MODEL), lambda i: (i, 0))
    return pl.pallas_call(
        body, name=name, grid=(S // tm,), in_specs=[row, pl.BlockSpec((1, D_MODEL), lambda i: (0, 0))],
        out_specs=row, out_shape=_sds((S, D_MODEL), BF16), compiler_params=_params(("parallel",)),
    )(h, g)


def _loss_head(name, h, g, target):
    S = h.shape[0]
    tm = _tile(S, ROW_TILE)

    def body(h_ref, g_ref, t_ref, dh_ref, dg_ref, ls_ref):
        x = h_ref[...]
        r = lax.rsqrt(jnp.mean(x * x, axis=-1, keepdims=True) + EPS)
        xh = x * r
        diff = xh * g_ref[...] - t_ref[...]
        dyf = diff * (1.0 / D_MODEL)
        dyg = dyf * g_ref[...]
        dh_ref[...] = r * (dyg - xh * jnp.mean(dyg * xh, axis=-1, keepdims=True))
        part = jnp.sum(dyf * xh, axis=0, keepdims=True)
        lpart = jnp.sum(diff * diff, axis=0, keepdims=True) * (0.5 / D_MODEL)

        @pl.when(pl.program_id(0) == 0)
        def _():
            dg_ref[...] = part
            ls_ref[...] = lpart

        @pl.when(pl.program_id(0) > 0)
        def _():
            dg_ref[...] += part
            ls_ref[...] += lpart

    row = pl.BlockSpec((tm, D_MODEL), lambda i: (i, 0))
    vec = pl.BlockSpec((1, D_MODEL), lambda i: (0, 0))
    return pl.pallas_call(
        body, name=name, grid=(S // tm,), in_specs=[row, vec, row], out_specs=[row, vec, vec],
        out_shape=[_sds((S, D_MODEL), F32), _sds((1, D_MODEL), F32), _sds((1, D_MODEL), F32)],
        compiler_params=_params(("arbitrary",)),
    )(h, g, target)


ATTN_SCALE = HEAD_DIM ** -0.5
ALIBI_SLOPES = [2.0 ** (-8.0 * (h + 1) / N_Q_HEADS) for h in range(N_Q_HEADS)]
K_COL = N_Q_HEADS * HEAD_DIM
KV_COLS = N_KV_HEADS * HEAD_DIM
V_COL = K_COL + KV_COLS


def _attn_masks(n):
    qi = lax.broadcasted_iota(jnp.int32, (ATTN_BLOCK, ATTN_BLOCK), 0)
    ki = lax.broadcasted_iota(jnp.int32, (ATTN_BLOCK, ATTN_BLOCK), 1)
    dist_c = (qi - ki).astype(F32)
    return dist_c + float(ATTN_BLOCK), dist_c, (ki > qi) & (n > 0), qi >= ki


def _attn_probs(raw_p, raw_c, sink, slope, masks):
    dist_p, dist_c, valid_p, valid_c = masks
    sp = jnp.where(valid_p, raw_p * ATTN_SCALE - slope * dist_p, NEG_BIG)
    sc = jnp.where(valid_c, raw_c * ATTN_SCALE - slope * dist_c, NEG_BIG)
    m = jnp.maximum(jnp.maximum(jnp.max(sp, axis=-1, keepdims=True), jnp.max(sc, axis=-1, keepdims=True)), sink)
    ep, ec, es = jnp.exp(sp - m), jnp.exp(sc - m), jnp.exp(sink - m)
    inv = 1.0 / (jnp.sum(ep, axis=-1, keepdims=True) + jnp.sum(ec, axis=-1, keepdims=True) + es)
    return ep * inv, ec * inv, es * inv


def _group_rows(ref, m):
    return jnp.concatenate([ref[:, HEAD_DIM * (Q_PER_KV * m + g):HEAD_DIM * (Q_PER_KV * m + g + 1)]
                            for g in range(Q_PER_KV)], axis=0)


def _head_rows(x, g):
    return x[ATTN_BLOCK * g:ATTN_BLOCK * (g + 1)]


def _attn_specs(nblk):
    last = nblk - 1
    kcol, vcol = K_COL // KV_COLS, V_COL // KV_COLS
    return [
        pl.BlockSpec((ATTN_BLOCK, K_COL), lambda n: (jnp.minimum(n, last), 0)),
        pl.BlockSpec((ATTN_BLOCK, KV_COLS), lambda n: (jnp.minimum(n, last), kcol)),
        pl.BlockSpec((ATTN_BLOCK, KV_COLS), lambda n: (jnp.maximum(jnp.minimum(n, last) - 1, 0), kcol)),
        pl.BlockSpec((ATTN_BLOCK, KV_COLS), lambda n: (jnp.minimum(n, last), vcol)),
        pl.BlockSpec((ATTN_BLOCK, KV_COLS), lambda n: (jnp.maximum(jnp.minimum(n, last) - 1, 0), vcol)),
    ]


def _attn_fwd(name, proj, sinks):
    S = proj.shape[0]
    nblk = S // ATTN_BLOCK

    def body(q_ref, kc_ref, kp_ref, vc_ref, vp_ref, sk_ref, o_ref):
        masks = _attn_masks(pl.program_id(0))
        for m in range(N_KV_HEADS):
            ks = slice(HEAD_DIM * m, HEAD_DIM * (m + 1))
            kp, kc, vp, vc = kp_ref[:, ks], kc_ref[:, ks], vp_ref[:, ks], vc_ref[:, ks]
            q4 = _group_rows(q_ref, m)
            raw_p, raw_c = _bdot(q4, kp, NT), _bdot(q4, kc, NT)
            pps, pcs = [], []
            for g in range(Q_PER_KV):
                hh = Q_PER_KV * m + g
                pp, pc, _ = _attn_probs(_head_rows(raw_p, g), _head_rows(raw_c, g), sk_ref[0, hh], ALIBI_SLOPES[hh], masks)
                pps.append(pp.astype(BF16))
                pcs.append(pc.astype(BF16))
            o4 = _bdot(jnp.concatenate(pps, axis=0), vp, NN) + _bdot(jnp.concatenate(pcs, axis=0), vc, NN)
            for g in range(Q_PER_KV):
                hh = Q_PER_KV * m + g
                o_ref[:, HEAD_DIM * hh:HEAD_DIM * (hh + 1)] = _head_rows(o4, g).astype(o_ref.dtype)

    return pl.pallas_call(
        body, name=name, grid=(nblk,),
        in_specs=_attn_specs(nblk) + [pl.BlockSpec(memory_space=pltpu.SMEM)],
        out_specs=pl.BlockSpec((ATTN_BLOCK, K_COL), lambda n: (n, 0)),
        out_shape=_sds((S, K_COL), BF16), compiler_params=_params(("parallel",)),
    )(proj, proj, proj, proj, proj, sinks)


def _attn_bwd(name, proj, sinks, do):
    S = proj.shape[0]
    nblk = S // ATTN_BLOCK

    def body(q_ref, kc_ref, kp_ref, vc_ref, vp_ref, do_ref, sk_ref, dz_ref, ds_ref, carry, cur, padd):
        n = pl.program_id(0)

        @pl.when(n == 0)
        def _():
            carry[...] = jnp.zeros_like(carry)
            ds_ref[...] = jnp.zeros_like(ds_ref)

        @pl.when(n < nblk)
        def _():
            masks = _attn_masks(n)
            lane = lax.broadcasted_iota(jnp.int32, (1, 128), 1)
            dsv = jnp.zeros((1, 128), F32)
            for m in range(N_KV_HEADS):
                ks = slice(HEAD_DIM * m, HEAD_DIM * (m + 1))
                kp, kc, vp, vc = kp_ref[:, ks], kc_ref[:, ks], vp_ref[:, ks], vc_ref[:, ks]
                q4, do4 = _group_rows(q_ref, m), _group_rows(do_ref, m)
                raw_p, raw_c = _bdot(q4, kp, NT), _bdot(q4, kc, NT)
                dpp4, dpc4 = _bdot(do4, vp, NT), _bdot(do4, vc, NT)
                pps, pcs, dsps, dscs = [], [], [], []
                for g in range(Q_PER_KV):
                    hh = Q_PER_KV * m + g
                    pp, pc, ps = _attn_probs(_head_rows(raw_p, g), _head_rows(raw_c, g), sk_ref[0, hh], ALIBI_SLOPES[hh], masks)
                    dpp, dpc = _head_rows(dpp4, g), _head_rows(dpc4, g)
                    delta = jnp.sum(pp * dpp, axis=-1, keepdims=True) + jnp.sum(pc * dpc, axis=-1, keepdims=True)
                    dsv = dsv + jnp.where(lane == hh, -jnp.sum(ps * delta, axis=0, keepdims=True), 0.0)
                    pps.append(pp.astype(BF16))
                    pcs.append(pc.astype(BF16))
                    dsps.append((pp * (dpp - delta)).astype(BF16))
                    dscs.append((pc * (dpc - delta)).astype(BF16))
                pp4, pc4 = jnp.concatenate(pps, axis=0), jnp.concatenate(pcs, axis=0)
                dsp4, dsc4 = jnp.concatenate(dsps, axis=0), jnp.concatenate(dscs, axis=0)
                dq4 = (_bdot(dsp4, kp, NN) + _bdot(dsc4, kc, NN)) * ATTN_SCALE
                for g in range(Q_PER_KV):
                    hh = Q_PER_KV * m + g
                    cur[:, HEAD_DIM * hh:HEAD_DIM * (hh + 1)] = _head_rows(dq4, g)
                cur[:, K_COL + HEAD_DIM * m:K_COL + HEAD_DIM * (m + 1)] = _bdot(dsc4, q4, TN) * ATTN_SCALE
                cur[:, V_COL + HEAD_DIM * m:V_COL + HEAD_DIM * (m + 1)] = _bdot(pc4, do4, TN)
                padd[:, ks] = _bdot(dsp4, q4, TN) * ATTN_SCALE
                padd[:, KV_COLS + HEAD_DIM * m:KV_COLS + HEAD_DIM * (m + 1)] = _bdot(pp4, do4, TN)
            ds_ref[...] += dsv
            dz_ref[:, :K_COL] = carry[:, :K_COL].astype(dz_ref.dtype)
            dz_ref[:, K_COL:] = (carry[:, K_COL:] + padd[...]).astype(dz_ref.dtype)
            carry[...] = cur[...]

        @pl.when(n == nblk)
        def _():
            dz_ref[...] = carry[...].astype(dz_ref.dtype)

    return pl.pallas_call(
        body, name=name, grid=(nblk + 1,),
        in_specs=_attn_specs(nblk) + [
            pl.BlockSpec((ATTN_BLOCK, K_COL), lambda n: (jnp.minimum(n, nblk - 1), 0)),
            pl.BlockSpec(memory_space=pltpu.SMEM)],
        out_specs=[pl.BlockSpec((ATTN_BLOCK, ATTN_IN), lambda n: (jnp.maximum(n - 1, 0), 0)),
                   pl.BlockSpec((1, 128), lambda n: (0, 0))],
        out_shape=[_sds((S, ATTN_IN), BF16), _sds((1, 128), F32)],
        scratch_shapes=[pltpu.VMEM((ATTN_BLOCK, ATTN_IN), F32), pltpu.VMEM((ATTN_BLOCK, ATTN_IN), F32),
                        pltpu.VMEM((ATTN_BLOCK, 2 * KV_COLS), F32)],
        compiler_params=_params(("arbitrary",)),
    )(proj, proj, proj, proj, proj, do, sinks)


def _hg_consts():
    C = HG_CHUNK
    tri = np.tril(np.ones((C, C)))
    t = np.arange(C)
    rows, masks = [tri], []
    for lvl in range(HG_LEVELS):
        n = C >> (lvl + 1)
        sel = np.zeros((C, C))
        sel[t, (t // (2 * n)) * (2 * n) + n - 1] = 1.0
        rows.append(sel @ tri)
        tt, ss = t[:, None], t[None, :]
        masks.append((tt // (2 * n) == ss // (2 * n)) & ((tt // n) % 2 == 1) & ((ss // n) % 2 == 0))
    masks.append(np.eye(C, dtype=bool))
    stk = np.concatenate(rows, axis=0)
    return jnp.asarray(stk, BF16), jnp.asarray(np.stack(masks), F32)


def _sigmoid(x):
    return 1.0 / (1.0 + jnp.exp(-x))


def _split(x, parts):
    out, rest = [], x
    for _ in range(parts):
        out.append(rest.astype(BF16))
        rest = rest - out[-1].astype(F32)
    return out


def _dot01(m01, x, dn, parts=3):
    return sum(lax.dot_general(m01, p, dn, preferred_element_type=F32) for p in _split(x, parts))


def _ref_rows(b, n):
    C = b.shape[1]
    if 2 * n >= 8:
        b3 = b.reshape(HG_CHUNK // (2 * n), 2 * n, C)
        return jnp.broadcast_to(b3[:, n - 1:n, :], b3.shape).reshape(HG_CHUNK, C)
    pos = lax.broadcasted_iota(jnp.int32, b.shape, 0) % (2 * n)
    out = b
    for p in range(2 * n):
        if p != n - 1:
            out = jnp.where(pos == p, pltpu.roll(b, (p - (n - 1)) % HG_CHUNK, 0), out)
    return out


def _hg_common(z_ref, lb_ref, stk_ref):
    qr, fr = z_ref[0], z_ref[1]
    lb = lb_ref[...]
    sq, sg, sgn = _sigmoid(qr), _sigmoid(fr), _sigmoid(-fr)
    ft = lb + (1.0 - lb) * sg
    b = _dot01(stk_ref[0:HG_CHUNK, :], jnp.log(ft), NN)
    ws = [jnp.exp(-jnp.abs(b - _ref_rows(b, HG_CHUNK >> (l + 1)))) for l in range(HG_LEVELS)]
    blast = b[HG_CHUNK - 1:HG_CHUNK]
    return dict(qr=qr, fr=fr, lb=lb, sq=sq, sg=sg, sgn=sgn, ft=ft, q=qr * sq, kk=(1.0 - lb) * sgn, b=b,
                ws=ws, eb=jnp.exp(b), ed=jnp.exp(blast - b), elast=jnp.exp(blast))


def _hg_factors(qh, kh, ws, sl):
    return ([(qh * ws[l][:, sl]).astype(BF16) for l in range(HG_LEVELS)],
            [(kh * ws[l][:, sl]).astype(BF16) for l in range(HG_LEVELS)])


def _hg_intra(qh, kh, ws, msk_ref, sl):
    qls, kls = _hg_factors(qh, kh, ws, sl)
    a = msk_ref[HG_LEVELS] * _bdot(qh, kh, NT)
    for l in range(HG_LEVELS):
        a = a + msk_ref[l] * _bdot(qls[l], kls[l], NT)
    return a


def _hg_fwd(name, z, lb, ng):
    S = z.shape[2]
    nc = S // HG_CHUNK
    stk, msk = _hg_consts()

    def body(z_ref, lb_ref, ng_ref, stk_ref, msk_ref, og_ref, st_ref, a_ref, o_ref, state):
        @pl.when(pl.program_id(1) == 0)
        def _():
            state[...] = jnp.zeros_like(state)

        cm = _hg_common(z_ref, lb_ref, stk_ref)
        v, gt = z_ref[2], z_ref[3]
        kd = cm["kk"] * cm["ed"]
        for hh in range(4):
            sl = slice(HG_K * hh, HG_K * (hh + 1))
            st = state[hh]
            st_ref[hh] = st
            qh, kh, vh = cm["q"][:, sl], cm["kk"][:, sl], v[:, sl]
            a = _hg_intra(qh, kh, cm["ws"], msk_ref, sl).astype(BF16)
            a_ref[hh] = a
            o = _bdot(a, vh, NN) + _bdot(qh * cm["eb"][:, sl], st, NT)
            o_ref[:, sl] = o
            state[hh] = cm["elast"][:, sl] * st + _bdot(vh, kd[:, sl], TN)
            r = lax.rsqrt(jnp.mean(o * o, axis=-1, keepdims=True) + EPS)
            gh = gt[:, sl]
            og_ref[:, sl] = (o * r * ng_ref[...] * (gh * _sigmoid(gh))).astype(og_ref.dtype)

    return pl.pallas_call(
        body, name=name, grid=(2, nc),
        in_specs=[pl.BlockSpec((4, None, HG_CHUNK, HG_SLOT), lambda g, c: (0, g, c, 0)),
                  pl.BlockSpec((1, HG_SLOT), lambda g, c: (0, g)),
                  pl.BlockSpec((1, HG_K), lambda g, c: (0, 0)),
                  pl.BlockSpec(stk.shape, lambda g, c: (0, 0)),
                  pl.BlockSpec(msk.shape, lambda g, c: (0, 0, 0))],
        out_specs=[pl.BlockSpec((HG_CHUNK, HG_SLOT), lambda g, c: (c, g)),
                   pl.BlockSpec((None, 4, HG_K, HG_K), lambda g, c: (c, g, 0, 0)),
                   pl.BlockSpec((None, 4, HG_CHUNK, HG_CHUNK), lambda g, c: (c, g, 0, 0)),
                   pl.BlockSpec((HG_CHUNK, HG_SLOT), lambda g, c: (c, g))],
        out_shape=[_sds((S, D_MODEL), BF16), _sds((nc, HG_HEADS, HG_K, HG_K), F32),
                   _sds((nc, HG_HEADS, HG_CHUNK, HG_CHUNK), BF16), _sds((S, D_MODEL), F32)],
        scratch_shapes=[pltpu.VMEM((4, HG_K, HG_K), F32)],
        compiler_params=_params(("parallel", "arbitrary")),
    )(z, lb, ng, stk, msk)


def _hg_bwd(name, z, lb, ng, states, intra, o_pre, dog):
    S = z.shape[2]
    nc = S // HG_CHUNK
    stk, msk = _hg_consts()

    def body(z_ref, lb_ref, ng_ref, stk_ref, msk_ref, st_ref, a_ref, o_ref, dog_ref, dz_ref, dlb_ref, dng_ref, dstate):
        @pl.when(pl.program_id(1) == 0)
        def _():
            dstate[...] = jnp.zeros_like(dstate)
            dlb_ref[...] = jnp.zeros_like(dlb_ref)
            dng_ref[...] = jnp.zeros_like(dng_ref)

        cm = _hg_common(z_ref, lb_ref, stk_ref)
        v, gt = z_ref[2], z_ref[3]
        ng = ng_ref[...]
        kd = cm["kk"] * cm["ed"]
        row = lax.broadcasted_iota(jnp.int32, (HG_CHUNK, 1), 0)
        dng = jnp.zeros((1, HG_K), F32)
        dq_h, dkk_h, db_h, dv_h, dgt_h = [], [], [], [], []
        dr_h = [[] for _ in range(HG_LEVELS)]
        for hh in range(4):
            sl = slice(HG_K * hh, HG_K * (hh + 1))
            st, dst = st_ref[hh], dstate[hh]
            qh, kh, vh, ebh, edh, kdh = cm["q"][:, sl], cm["kk"][:, sl], v[:, sl], cm["eb"][:, sl], cm["ed"][:, sl], kd[:, sl]
            elh = cm["elast"][:, sl]
            qls, kls = _hg_factors(qh, kh, cm["ws"], sl)
            a, o = a_ref[hh], o_ref[:, sl]
            qe = qh * ebh
            r = lax.rsqrt(jnp.mean(o * o, axis=-1, keepdims=True) + EPS)
            xh = o * r
            gh = gt[:, sl]
            sgg = _sigmoid(gh)
            dog = dog_ref[:, sl].astype(F32)
            dy = dog * (gh * sgg)
            dgt_h.append(dog * (xh * ng) * (sgg * (1.0 + gh * (1.0 - sgg))))
            dng = dng + jnp.sum(dy * xh, axis=0, keepdims=True)
            dyg = dy * ng
            do = r * (dyg - xh * jnp.mean(dyg * xh, axis=-1, keepdims=True))
            da = _bdot(do, vh, NT)
            dv_h.append(_bdot(a, do, TN) + _bdot(kdh, dst, NT))
            dkd = _bdot(vh, dst, NN)
            delast = jnp.sum(st * dst, axis=0, keepdims=True)
            dqe = _bdot(do, st, NN)
            dstate[hh] = elh * dst + _bdot(do, qe, TN)
            gk = dkd * kdh
            dblast = jnp.sum(gk, axis=0, keepdims=True) + delast * elh
            db = dqe * qe - gk + jnp.where(row == HG_CHUNK - 1, dblast, 0.0)
            dp = (msk_ref[HG_LEVELS] * da).astype(BF16)
            dq = dqe * ebh + _bdot(dp, kh, NN)
            dkk = dkd * edh + _bdot(dp, qh, TN)
            for l in range(HG_LEVELS):
                dp = (msk_ref[l] * da).astype(BF16)
                dql, dkl = _bdot(dp, kls[l], NN), _bdot(dp, qls[l], TN)
                w = cm["ws"][l][:, sl]
                dq = dq + dql * w
                dkk = dkk + dkl * w
                half = jnp.where(((row >> (HG_LEVELS - 1 - l)) & 1) == 1, 1.0, -1.0)
                dd = half * w * (dql * qh + dkl * kh)
                db = db + dd
                dr_h[l].append(-dd)
            dq_h.append(dq)
            dkk_h.append(dkk)
            db_h.append(db)
        cat = lambda xs: jnp.concatenate(xs, axis=1)
        cot = jnp.concatenate([cat(db_h)] + [cat(dr_h[l]) for l in range(HG_LEVELS)], axis=0)
        dlf = _dot01(stk_ref[...], cot, TN, parts=2)
        dq, dkk = cat(dq_h), cat(dkk_h)
        dft = dlf / cm["ft"]
        one_lb = 1.0 - cm["lb"]
        dz_ref[0] = (dq * (cm["sq"] * (1.0 + cm["qr"] * (1.0 - cm["sq"])))).astype(dz_ref.dtype)
        dz_ref[1] = ((dft - dkk) * one_lb * cm["sg"] * cm["sgn"]).astype(dz_ref.dtype)
        dz_ref[2] = cat(dv_h).astype(dz_ref.dtype)
        dz_ref[3] = cat(dgt_h).astype(dz_ref.dtype)
        dlb_ref[...] += jnp.sum((dft - dkk) * cm["sgn"], axis=0, keepdims=True)
        dng_ref[...] += dng

    rev = lambda c: nc - 1 - c
    return pl.pallas_call(
        body, name=name, grid=(2, nc),
        in_specs=[pl.BlockSpec((4, None, HG_CHUNK, HG_SLOT), lambda g, c: (0, g, rev(c), 0)),
                  pl.BlockSpec((1, HG_SLOT), lambda g, c: (0, g)),
                  pl.BlockSpec((1, HG_K), lambda g, c: (0, 0)),
                  pl.BlockSpec(stk.shape, lambda g, c: (0, 0)),
                  pl.BlockSpec(msk.shape, lambda g, c: (0, 0, 0)),
                  pl.BlockSpec((None, 4, HG_K, HG_K), lambda g, c: (rev(c), g, 0, 0)),
                  pl.BlockSpec((None, 4, HG_CHUNK, HG_CHUNK), lambda g, c: (rev(c), g, 0, 0)),
                  pl.BlockSpec((HG_CHUNK, HG_SLOT), lambda g, c: (rev(c), g)),
                  pl.BlockSpec((HG_CHUNK, HG_SLOT), lambda g, c: (rev(c), g))],
        out_specs=[pl.BlockSpec((4, None, HG_CHUNK, HG_SLOT), lambda g, c: (0, g, rev(c), 0)),
                   pl.BlockSpec((1, HG_SLOT), lambda g, c: (0, g)),
                   pl.BlockSpec((None, 1, HG_K), lambda g, c: (g, 0, 0))],
        out_shape=[_sds(z.shape, BF16), _sds((1, 2 * HG_SLOT), F32), _sds((2, 1, HG_K), F32)],
        scratch_shapes=[pltpu.VMEM((4, HG_K, HG_K), F32)],
        compiler_params=_params(("parallel", "arbitrary")),
    )(z, lb, ng, stk, msk, states, intra, o_pre, dog)


def _lb_fwd(name, logits):
    def body(l_ref, o_ref):
        x = l_ref[...]
        e = jnp.exp(x - jnp.max(x, axis=0, keepdims=True))
        s = e / jnp.sum(e, axis=0, keepdims=True)
        o_ref[0:1, :] = s[1:2]
        o_ref[1:2, :] = s[1:2] + s[2:3] + s[3:4]

    return pl.pallas_call(body, name=name, out_shape=_sds((2, logits.shape[1]), F32))(logits)


def _lb_bwd(name, logits, dlb):
    def body(l_ref, d_ref, o_ref):
        x = l_ref[...]
        e = jnp.exp(x - jnp.max(x, axis=0, keepdims=True))
        s = e / jnp.sum(e, axis=0, keepdims=True)
        d1, d3 = d_ref[0:1, :], d_ref[1:2, :]
        ds = [jnp.zeros_like(d1), d1 + d3, d3, d3]
        dot = sum(ds[r] * s[r:r + 1] for r in range(1, DEPTH))
        for r in range(DEPTH):
            o_ref[r:r + 1, :] = s[r:r + 1] * (ds[r] - dot)

    return pl.pallas_call(body, name=name, out_shape=_sds(logits.shape, F32))(logits, dlb)


SUB = 8


def _rows_down(x, prev, k):
    row = lax.broadcasted_iota(jnp.int32, x.shape, 0)
    return jnp.where(row >= k, pltpu.roll(x, k, 0), pltpu.roll(prev, k, 0))


def _rows_up(x, nxt, k):
    row = lax.broadcasted_iota(jnp.int32, x.shape, 0)
    return jnp.where(row < SUB - k, pltpu.roll(x, SUB - k, 0), pltpu.roll(nxt, SUB - k, 0))


def _conv_block(w_ref, b_ref, p, x, prev):
    x0, x1 = _rows_down(x, prev, 2), _rows_down(x, prev, 1)
    return b_ref[p] + w_ref[p, 0:1, :] * x0 + w_ref[p, 1:2, :] * x1 + w_ref[p, 2:3, :] * x, x0, x1


def _convgate_fwd(name, u, cw, cb):
    S = u.shape[2]
    tm = _tile(S, ROW_TILE)

    def body(u_ref, w_ref, b_ref, a_ref, halo):
        @pl.when(pl.program_id(1) == 0)
        def _():
            halo[...] = jnp.zeros_like(halo)

        def step(r, prev):
            pg, pv = prev
            out = []
            for s in range(2):
                rows = pl.ds(pl.multiple_of(r * 2 * SUB + s * SUB, SUB), SUB)
                xg, xv = u_ref[0, rows, :], u_ref[1, rows, :]
                cg = _conv_block(w_ref, b_ref, 0, xg, pg)[0]
                cv = _conv_block(w_ref, b_ref, 1, xv, pv)[0]
                out.append(cg * _sigmoid(cg) * cv)
                pg, pv = xg, xv
            a_ref[pl.ds(pl.multiple_of(r * 2 * SUB, 2 * SUB), 2 * SUB), :] = jnp.concatenate(out, axis=0).astype(a_ref.dtype)
            return pg, pv

        pg, pv = lax.fori_loop(0, tm // (2 * SUB), step, (halo[0], halo[1]))
        halo[0] = pg
        halo[1] = pv

    return pl.pallas_call(
        body, name=name, grid=(4, S // tm),
        in_specs=[pl.BlockSpec((2, None, tm, FF_SLOT), lambda j, t: (0, j, t, 0)),
                  pl.BlockSpec((2, None, 3, FF_SLOT), lambda j, t: (0, j, 0, 0)),
                  pl.BlockSpec((2, None, 1, FF_SLOT), lambda j, t: (0, j, 0, 0))],
        out_specs=pl.BlockSpec((None, tm, FF_SLOT), lambda j, t: (j, t, 0)),
        out_shape=_sds((4, S, FF_SLOT), BF16),
        scratch_shapes=[pltpu.VMEM((2, SUB, FF_SLOT), F32)],
        compiler_params=_params(("parallel", "arbitrary")),
    )(u, cw, cb)


def _convgate_bwd(name, u, cw, cb, da):
    S = u.shape[2]
    tm = _tile(S, ROW_TILE)
    nt = S // tm

    def body(u_ref, uh_ref, w_ref, b_ref, da_ref, du_out, dw_ref, db_ref, after, first, acc, du_ref):
        t = pl.program_id(1)

        @pl.when(t == 0)
        def _():
            after[...] = jnp.zeros_like(after)
            acc[...] = jnp.zeros_like(acc)

        def du_block(p, d, nxt):
            return w_ref[p, 2:3, :] * d + w_ref[p, 1:2, :] * _rows_up(d, nxt, 1) + w_ref[p, 0:1, :] * _rows_up(d, nxt, 2)

        def step(r, carry):
            pg, pv, dg_last, dv_last = carry
            dav = da_ref[pl.ds(pl.multiple_of(r * 2 * SUB, 2 * SUB), 2 * SUB), :].astype(F32)
            for s in range(2):
                at = r * 2 * SUB + s * SUB
                rows = pl.ds(pl.multiple_of(at, SUB), SUB)
                xg, xv = u_ref[0, rows, :], u_ref[1, rows, :]
                cg, x0g, x1g = _conv_block(w_ref, b_ref, 0, xg, pg)
                cv, x0v, x1v = _conv_block(w_ref, b_ref, 1, xv, pv)
                sg = _sigmoid(cg)
                dab = dav[s * SUB:(s + 1) * SUB]
                dg = dab * cv * (sg * (1.0 + cg * (1.0 - sg)))
                dv = dab * cg * sg
                for p, d, taps in ((0, dg, (x0g, x1g, xg)), (1, dv, (x0v, x1v, xv))):
                    for j in range(3):
                        acc[p, j] += d * taps[j]
                    acc[p, 3] += d
                if s == 0:
                    @pl.when(r == 0)
                    def _():
                        first[0] = dg
                        first[1] = dv

                    @pl.when(r > 0)
                    def _():
                        before = pl.ds(pl.multiple_of(at - SUB, SUB), SUB)
                        du_ref[0, before, :] = du_block(0, dg_last, dg)
                        du_ref[1, before, :] = du_block(1, dv_last, dv)
                else:
                    before = pl.ds(pl.multiple_of(at - SUB, SUB), SUB)
                    du_ref[0, before, :] = du_block(0, dg_last, dg)
                    du_ref[1, before, :] = du_block(1, dv_last, dv)
                pg, pv, dg_last, dv_last = xg, xv, dg, dv
            return pg, pv, dg_last, dv_last

        halo = jnp.where(t < nt - 1, uh_ref[...], 0.0)
        zero = jnp.zeros((SUB, FF_SLOT), F32)
        _, _, dg_last, dv_last = lax.fori_loop(0, tm // (2 * SUB), step, (halo[0], halo[1], zero, zero))
        du_ref[0, tm - SUB:tm, :] = du_block(0, dg_last, after[0])
        du_ref[1, tm - SUB:tm, :] = du_block(1, dv_last, after[1])
        du_out[...] = du_ref[...].astype(du_out.dtype)
        after[...] = first[...]
        for p in range(2):
            for j in range(3):
                dw_ref[p, j:j + 1, :] = jnp.sum(acc[p, j], axis=0, keepdims=True)
            db_ref[p] = jnp.sum(acc[p, 3], axis=0, keepdims=True)

    rev = lambda t: nt - 1 - t
    return pl.pallas_call(
        body, name=name, grid=(4, nt),
        in_specs=[pl.BlockSpec((2, None, tm, FF_SLOT), lambda j, t: (0, j, rev(t), 0)),
                  pl.BlockSpec((2, None, SUB, FF_SLOT), lambda j, t: (0, j, jnp.maximum(rev(t) * (tm // SUB) - 1, 0), 0)),
                  pl.BlockSpec((2, None, 3, FF_SLOT), lambda j, t: (0, j, 0, 0)),
                  pl.BlockSpec((2, None, 1, FF_SLOT), lambda j, t: (0, j, 0, 0)),
                  pl.BlockSpec((None, tm, FF_SLOT), lambda j, t: (j, rev(t), 0))],
        out_specs=[pl.BlockSpec((2, None, tm, FF_SLOT), lambda j, t: (0, j, rev(t), 0)),
                   pl.BlockSpec((2, None, 3, FF_SLOT), lambda j, t: (0, j, 0, 0)),
                   pl.BlockSpec((2, None, 1, FF_SLOT), lambda j, t: (0, j, 0, 0))],
        out_shape=[_sds(u.shape, BF16), _sds(cw.shape, F32), _sds(cb.shape, F32)],
        scratch_shapes=[pltpu.VMEM((2, SUB, FF_SLOT), F32), pltpu.VMEM((2, SUB, FF_SLOT), F32),
                        pltpu.VMEM((2, 4, SUB, FF_SLOT), F32), pltpu.VMEM((2, tm, FF_SLOT), F32)],
        compiler_params=_params(("parallel", "arbitrary")),
    )(u, u, cw, cb, da)


def _row_tile(R):
    for t in range(256, 15, -16):
        if R % t == 0:
            return t
    return R


def _adamw(name, gsrcs, w, m, v):
    L = len(gsrcs)
    n, A, C = gsrcs[0].shape
    tr = _row_tile(A)

    def body(*refs):
        g_refs = refs[:L]
        w_ref, m_ref, v_ref, go_ref, d_ref, mo_ref, vo_ref = refs[L:]
        for k in range(L):
            @pl.when(pl.program_id(0) == k)
            def _(k=k):
                g = g_refs[k][0].astype(F32)
                for s in range(1, n):
                    g = g + g_refs[k][s].astype(F32)
                m2 = ADAM_B1 * m_ref[...] + (1.0 - ADAM_B1) * g
                v2 = ADAM_B2 * v_ref[...] + (1.0 - ADAM_B2) * (g * g)
                m_hat = m2 / (1.0 - ADAM_B1 ** ADAM_STEP)
                v_hat = v2 / (1.0 - ADAM_B2 ** ADAM_STEP)
                go_ref[...] = g
                d_ref[...] = -ADAM_LR * (m_hat / (jnp.sqrt(v_hat) + ADAM_EPS) + ADAM_WD * w_ref[...])
                mo_ref[...] = m2
                vo_ref[...] = v2

    g_specs = [pl.BlockSpec((n, tr, C), lambda l, i, k=k: (0, jnp.where(l == k, i, 0), 0)) for k in range(L)]
    blk = pl.BlockSpec((None, tr, C), lambda l, i: (l, i, 0))
    return pl.pallas_call(
        body, name=name, grid=(L, A // tr), in_specs=g_specs + [blk, blk, blk],
        out_specs=[blk] * 4, out_shape=[_sds((L, A, C), F32)] * 4, compiler_params=_params(("parallel", "parallel")),
    )(*gsrcs, w, m, v)


MESH = pl.DeviceIdType.MESH
HBM_SPEC = pl.BlockSpec(memory_space=pltpu.HBM)
N_PEERS = N_DEV - 1


def _mesh_place():
    x, y, c = lax.axis_index("x"), lax.axis_index("y"), lax.axis_index("c")
    peers = []
    for p in range(1, N_DEV):
        px = 1 - x if p & 4 else x
        py = 1 - y if p & 2 else y
        pc = 1 - c if p & 1 else c
        peers.append(((px, py, pc), 4 * px + 2 * py + pc))
    return 4 * x + 2 * y + c, peers


SEM_SPEC = pl.BlockSpec(memory_space=pltpu.SEMAPHORE)
ANY_SPEC = pl.BlockSpec(memory_space=pl.ANY)
EFFECT = pltpu.SideEffectType.DATAFLOW_SIDE_EFFECTING


def _exchange_refs(scatter, src, land, send, recv, k, p, dev, idx, me):
    return pltpu.make_async_remote_copy(src_ref=src[k].at[idx] if scatter else src[k], dst_ref=land[k].at[me],
                                        send_sem=send.at[k * N_PEERS + p], recv_sem=recv.at[k * N_PEERS + p], device_id=dev,
                                        device_id_type=MESH)


def _exchange_start(name, srcs, scatter, gate):
    n = len(srcs)
    lands = [lax.empty(s.shape if scatter else (N_DEV,) + s.shape, s.dtype) for s in srcs]

    def body(*refs):
        src, land = refs[:n], refs[n:2 * n]
        send, recv, own = refs[2 * n + 1:2 * n + 4]
        token = refs[-1]
        me, peers = _mesh_place()
        for k in range(n):
            pltpu.make_async_copy(src[k].at[me] if scatter else src[k], land[k].at[me], own.at[k]).start()
            for p, (dev, idx) in enumerate(peers):
                _exchange_refs(scatter, src, land, send, recv, k, p, dev, idx, me).start()
        token[...] = jnp.zeros_like(token)

    hbm = lambda a: pltpu.HBM(a.shape, a.dtype)
    outs = pl.pallas_call(
        body, name=name,
        out_shape=(pltpu.SemaphoreType.DMA((n * N_PEERS,)), pltpu.SemaphoreType.DMA((n * N_PEERS,)),
                   pltpu.SemaphoreType.DMA((n,)), *[hbm(s) for s in srcs], *[hbm(s) for s in lands], _sds(DEP_SHAPE, F32)),
        in_specs=[HBM_SPEC] * (2 * n) + [ANY_SPEC],
        out_specs=(SEM_SPEC, SEM_SPEC, SEM_SPEC, *[HBM_SPEC] * (2 * n), pl.BlockSpec(memory_space=pltpu.VMEM)),
        input_output_aliases={j: 3 + j for j in range(2 * n)},
        compiler_params=pltpu.CompilerParams(has_side_effects=EFFECT),
    )(*[pltpu.with_memory_space_constraint(s, pltpu.HBM) for s in srcs],
      *[pltpu.with_memory_space_constraint(s, pltpu.HBM) for s in lands], gate)
    return outs[:3], None, list(outs[3:3 + n]), list(outs[3 + n:3 + 2 * n]), outs[-1]


def _exchange_wait(name, started, scatter, after):
    (send, recv, own), _, srcs, lands, _ = started
    n = len(srcs)

    def body(*refs):
        src, land = refs[:n], refs[n:2 * n]
        send, recv, own = refs[2 * n:2 * n + 3]
        me, peers = _mesh_place()
        for k in range(n):
            pltpu.make_async_copy(src[k].at[me] if scatter else src[k], land[k].at[me], own.at[k]).wait()
            for p, (dev, idx) in enumerate(peers):
                cp = pltpu.make_async_remote_copy(src_ref=src[k].at[idx] if scatter else src[k], dst_ref=land[k].at[idx],
                                                  send_sem=send.at[k * N_PEERS + p], recv_sem=recv.at[k * N_PEERS + p], device_id=dev,
                                                  device_id_type=MESH)
                cp.wait_send()
                cp.wait_recv()

    hbm = lambda a: pltpu.HBM(a.shape, a.dtype)
    outs = pl.pallas_call(
        body, name=name, out_shape=(*[hbm(s) for s in srcs], *[hbm(s) for s in lands]),
        in_specs=[HBM_SPEC] * (2 * n) + [SEM_SPEC, SEM_SPEC, SEM_SPEC, ANY_SPEC], out_specs=tuple([HBM_SPEC] * (2 * n)),
        input_output_aliases={j: j for j in range(2 * n)},
        compiler_params=pltpu.CompilerParams(has_side_effects=EFFECT),
    )(*srcs, *lands, send, recv, own, after)
    return list(outs[n:])


def _allsum_rows(name, part):
    R, C = part.shape

    def body(p_ref, o_ref, gath, send, recv):
        me, peers = _mesh_place()
        gath[me] = p_ref[...]
        sends = []
        for p, (dev, _) in enumerate(peers):
            cp = pltpu.make_async_remote_copy(src_ref=p_ref, dst_ref=gath.at[me], send_sem=send.at[p], recv_sem=recv.at[p],
                                              device_id=dev, device_id_type=MESH)
            cp.start()
            sends.append(cp)
        for p, (dev, idx) in enumerate(peers):
            pltpu.make_async_remote_copy(src_ref=p_ref, dst_ref=gath.at[idx], send_sem=send.at[p], recv_sem=recv.at[p],
                                         device_id=dev, device_id_type=MESH).wait_recv()
        for cp in sends:
            cp.wait_send()
        tot = gath[0]
        for j in range(1, N_DEV):
            tot = tot + gath[j]
        o_ref[...] = tot

    vm = pl.BlockSpec(memory_space=pltpu.VMEM)
    return pl.pallas_call(
        body, name=name, in_specs=[vm], out_specs=vm, out_shape=_sds((R, C), F32),
        scratch_shapes=[pltpu.VMEM((N_DEV, R, C), F32), pltpu.SemaphoreType.DMA((N_PEERS,)),
                        pltpu.SemaphoreType.DMA((N_PEERS,))],
        compiler_params=pltpu.CompilerParams(vmem_limit_bytes=VMEM_LIMIT),
    )(part)


def _rows(a, width=D_MODEL):
    flat = a.reshape(-1)
    return jnp.pad(flat, (0, (-flat.shape[0]) % width)).reshape(-1, width)


def _pack_rows(parts):
    blocks = []
    for p in parts:
        r = _rows(p)
        blocks.append(jnp.pad(r, ((0, (-r.shape[0]) % 8), (0, 0))))
    return jnp.concatenate(blocks, axis=0)


def _unpack_rows(rows, shapes):
    out, at = [], 0
    for s in shapes:
        size = int(np.prod(s))
        n = -(-size // D_MODEL)
        out.append(rows[at:at + n].reshape(-1)[:size].reshape(s))
        at += -(-n // 8) * 8
    return out


def kernel(x, norm_mix, norm_ffn, norm_final, attn_w_in, attn_w_out, attn_sinks, hgrn_w_in, hgrn_w_out, hgrn_norm, hgrn_lb_logits, ffn_w_up, ffn_conv_w, ffn_conv_b, ffn_w_down, loss_target, m_norm_mix, m_norm_ffn, m_norm_final, m_attn_w_in, m_attn_w_out, m_attn_sinks, m_hgrn_w_in, m_hgrn_w_out, m_hgrn_norm, m_hgrn_lb_logits, m_ffn_w_up, m_ffn_conv_w, m_ffn_conv_b, m_ffn_w_down, v_norm_mix, v_norm_ffn, v_norm_final, v_attn_w_in, v_attn_w_out, v_attn_sinks, v_hgrn_w_in, v_hgrn_w_out, v_hgrn_norm, v_hgrn_lb_logits, v_ffn_w_up, v_ffn_conv_w, v_ffn_conv_b, v_ffn_w_down):
    S = x.shape[1]
    n_attn, n_hgrn = attn_w_in.shape[0], hgrn_w_in.shape[0]
    me = 4 * lax.axis_index("x") + 2 * lax.axis_index("y") + lax.axis_index("c")

    wa_in_t, wa_out_b = attn_w_in.transpose(0, 2, 1).astype(BF16), attn_w_out.astype(BF16)
    wh_in_b, wh_out_b = hgrn_w_in.astype(BF16), hgrn_w_out.astype(BF16)
    wf_up_b, wf_down_b = ffn_w_up.transpose(0, 2, 1).astype(BF16), ffn_w_down.astype(BF16)
    conv_b = ffn_conv_b.reshape(DEPTH, 2, 4, 1, FF_SLOT)
    lb = _lb_fwd("lb_fwd", hgrn_lb_logits)

    def unit_shards(l, part):
        if part == "ffn":
            return [wf_up_b[l], wf_down_b[l], ffn_conv_w[l]]
        return [wa_in_t[l // 2], wa_out_b[l // 2]] if l % 2 == 0 else [wh_in_b[l // 2], wh_out_b[l // 2]]

    def unit_weights(l, part, w):
        if part == "ffn":
            return w[0][None], w[1].reshape(1, 4, FF_SLOT, D_MODEL), w[2].reshape(2, 4, 3, FF_SLOT)
        if l % 2 == 0:
            return w[0].reshape(1, ATTN_IN, D_MODEL), w[1].reshape(1, D_MODEL, D_MODEL)
        return w[0][None], w[1].reshape(1, D_MODEL, D_MODEL)

    units = [(l, part) for l in range(DEPTH) for part in ("mix", "ffn")]
    gathers = [_exchange_start("gather_start0", unit_shards(*units[0]), False, norm_final)]
    gathers.append(_exchange_start("gather_start1", unit_shards(*units[1]), False, gathers[0][4]))
    arrived = _exchange_wait("gather_wait0", gathers[0], False, gathers[1][4])
    weights, saved = {}, [dict() for _ in range(DEPTH)]
    h = x[0]
    hn = _rmsnorm_fwd("norm_mix_fwd0", h, norm_mix[0:1])
    for n, (l, part) in enumerate(units):
        i, sv = l // 2, saved[l]
        weights[l, part] = w = unit_weights(l, part, arrived)
        dep = None
        if n + 2 < len(units):
            gathers.append(_exchange_start(f"gather_start{n + 2}", unit_shards(*units[n + 2]), False, arrived[0]))
            dep = gathers[n + 2][4]
        if part == "mix":
            sv["h"], sv["hn"] = h, hn
            if l % 2 == 0:
                sv["proj"] = _proj_rows(f"attn_proj{i}", hn, w[0], 0, BF16, dep)
                sv["o"] = _attn_fwd(f"attn_fwd{i}", sv["proj"], attn_sinks[i:i + 1])
                h, hn = _out_proj(f"attn_out{i}", sv["o"], w[1], 0, h, norm_ffn[l:l + 1])
            else:
                sv["z"] = _proj_slots(f"hgrn_proj{i}", hn, w[0], 0, dep=dep).reshape(4, 2, S, HG_SLOT)
                sv["o"], *sv["kept"] = _hg_fwd(f"hgrn_fwd{i}", sv["z"], lb[i:i + 1], hgrn_norm[i:i + 1])
                h, hn = _out_proj(f"hgrn_out{i}", sv["o"], w[1], 0, h, norm_ffn[l:l + 1])
        else:
            sv["h2"], sv["hn2"] = h, hn
            sv["u"] = _proj_slots(f"ffn_up{l}", hn, w[0], 0, True, dep).reshape(2, 4, S, FF_SLOT)
            sv["a"] = _convgate_fwd(f"ffn_gate{l}", sv["u"], w[2], conv_b[l])
            if l + 1 < DEPTH:
                h, hn = _down_proj(f"ffn_down{l}", sv["a"], w[1], 0, h, norm_mix[l + 1:l + 2])
            else:
                h = _down_proj(f"ffn_down{l}", sv["a"], w[1], 0, h)
        if n + 1 < len(units):
            arrived = _exchange_wait(f"gather_wait{n + 1}", gathers[n + 1], False, h)
    dh, d_norm_final, loss_rows = _loss_head("loss_head", h, norm_final[None], loss_target[0])

    d_conv_w, d_conv_b, d_norm_mix, d_norm_ffn = [None] * DEPTH, [None] * DEPTH, [None] * DEPTH, [None] * DEPTH
    d_sinks, d_lb, d_hgrn_norm = [None] * n_attn, [None] * n_hgrn, [None] * n_hgrn
    received, started, before = {}, None, None
    for l, part in reversed(units):
        i, sv, w = l // 2, saved[l], weights[l, part]
        dep = None if started is None else started[4]
        if part == "ffn":
            da = _dgrad_down(f"ffn_down_dgrad{l}", dh, w[1], 0, dep)
            g_down = _wgrad_down(f"ffn_down_wgrad{l}", sv["a"], dh).reshape(N_DEV, D_FF // N_DEV, D_MODEL)
            du, d_conv_w[l], d_conv_b[l] = _convgate_bwd(f"ffn_gate_bwd{l}", sv["u"], w[2], conv_b[l], da)
            du = du.reshape(N_DEV, S, FF_SLOT)
            grads = [_wgrad_slots(f"ffn_up_wgrad{l}", sv["hn2"], du, True), g_down]
            dh, d_norm_ffn[l] = _dgrad_slots(f"ffn_up_dgrad{l}", du, w[0], 0, (sv["h2"], norm_ffn[l:l + 1], dh), True)
        else:
            if l % 2 == 0:
                do = _dgrad_out(f"attn_out_dgrad{i}", dh, w[1], 0, BF16, dep)
                g_out = _wgrad_rows(f"attn_out_wgrad{i}", sv["o"], dh)
                dproj, d_sinks[i] = _attn_bwd(f"attn_bwd{i}", sv["proj"], attn_sinks[i:i + 1], do)
                g_in = _wgrad_rows(f"attn_proj_wgrad{i}", dproj, sv["hn"]).reshape(N_DEV, ATTN_IN // N_DEV, D_MODEL)
                dh_new = _dgrad_rows(f"attn_proj_dgrad{i}", dproj, w[0], 0, (sv["h"], norm_mix[l:l + 1], dh))
            else:
                dog = _dgrad_out(f"hgrn_out_dgrad{i}", dh, w[1], 0, F32, dep)
                g_out = _wgrad_rows(f"hgrn_out_wgrad{i}", sv["o"], dh)
                dz, d_lb[i], dng = _hg_bwd(f"hgrn_bwd{i}", sv["z"], lb[i:i + 1], hgrn_norm[i:i + 1], *sv["kept"], dog)
                d_hgrn_norm[i] = dng[0] + dng[1]
                dz = dz.reshape(N_DEV, S, HG_SLOT)
                g_in = _wgrad_slots(f"hgrn_proj_wgrad{i}", sv["hn"], dz)
                dh_new = _dgrad_slots(f"hgrn_proj_dgrad{i}", dz, w[0], 0, (sv["h"], norm_mix[l:l + 1], dh))
            grads = [g_in, g_out.reshape(N_DEV, D_MODEL // N_DEV, D_MODEL)]
            dh, d_norm_mix[l] = dh_new
        gate = dh
        if started is not None:
            received[before] = _exchange_wait(f"scatter_wait_{before[1]}{before[0]}", started, True, dh)
            gate = received[before][0]
        started, before = _exchange_start(f"scatter_start_{part}{l}", grads, True, gate), (l, part)
    grad_x = dh[None]

    attn_layers, hgrn_layers = range(0, DEPTH, 2), range(1, DEPTH, 2)

    def transposed_update(name, gsrcs, w, m, v):
        outs = _adamw(name, gsrcs, *[t.transpose(0, 2, 1) for t in (w, m, v)])
        return [o.transpose(0, 2, 1) for o in outs]

    big = {
        "ffn_w_up": transposed_update("adamw_ffn_up", [received[l, "ffn"][0] for l in range(DEPTH)], ffn_w_up, m_ffn_w_up,
                                      v_ffn_w_up),
        "hgrn_w_in": _adamw("adamw_hgrn_in", [received[l, "mix"][0] for l in hgrn_layers], hgrn_w_in, m_hgrn_w_in, v_hgrn_w_in),
        "hgrn_w_out": _adamw("adamw_hgrn_out", [received[l, "mix"][1] for l in hgrn_layers], hgrn_w_out, m_hgrn_w_out, v_hgrn_w_out),
        "ffn_w_down": _adamw("adamw_ffn_down", [received[l, "ffn"][1] for l in range(DEPTH)], ffn_w_down, m_ffn_w_down, v_ffn_w_down),
    }
    received[before] = _exchange_wait(f"scatter_wait_{before[1]}{before[0]}", started, True, big["ffn_w_down"][3])

    small_shapes = [(DEPTH, D_MODEL), (DEPTH, D_MODEL), (1, D_MODEL), (1, D_MODEL), (n_hgrn, D_MODEL), (n_attn, 128),
                    (n_hgrn, HG_K), (DEPTH, 2 * D_FF), (DEPTH, N_DEV, 3, FF_SLOT)]
    total = _allsum_rows("allsum_small", _pack_rows([
        jnp.concatenate(d_norm_mix), jnp.concatenate(d_norm_ffn), d_norm_final, loss_rows, jnp.concatenate(d_lb),
        jnp.concatenate(d_sinks), jnp.concatenate(d_hgrn_norm), jnp.stack(d_conv_b), jnp.stack(d_conv_w)]))
    (g_norm_mix, g_norm_ffn, g_norm_final, loss_sum, g_lb, g_sinks, g_hgrn_norm, g_conv_b, g_conv_w_all) = _unpack_rows(
        total, small_shapes)
    loss = jnp.sum(loss_sum)
    g_norm_final = g_norm_final[0]
    g_sinks = g_sinks[:, :N_Q_HEADS]
    g_lb_logits = _lb_bwd("lb_bwd", hgrn_lb_logits, g_lb)
    g_conv_w = lax.dynamic_index_in_dim(g_conv_w_all, me, axis=1, keepdims=False)

    big.update({
        "attn_w_in": transposed_update("adamw_attn_in", [received[l, "mix"][0] for l in attn_layers], attn_w_in,
                                       m_attn_w_in, v_attn_w_in),
        "attn_w_out": _adamw("adamw_attn_out", [received[l, "mix"][1] for l in attn_layers], attn_w_out, m_attn_w_out, v_attn_w_out),
        "ffn_conv_w": _adamw("adamw_conv_w", [g_conv_w[l][None] for l in range(DEPTH)], ffn_conv_w, m_ffn_conv_w, v_ffn_conv_w),
    })
    small_w = [norm_mix, norm_ffn, norm_final, attn_sinks, hgrn_norm, hgrn_lb_logits, ffn_conv_b]
    small_m = [m_norm_mix, m_norm_ffn, m_norm_final, m_attn_sinks, m_hgrn_norm, m_hgrn_lb_logits, m_ffn_conv_b]
    small_v = [v_norm_mix, v_norm_ffn, v_norm_final, v_attn_sinks, v_hgrn_norm, v_hgrn_lb_logits, v_ffn_conv_b]
    small_g = [g_norm_mix, g_norm_ffn, g_norm_final, g_sinks, g_hgrn_norm, g_lb_logits, g_conv_b]
    outs = _adamw("adamw_small", [_pack_rows(small_g)[None]], *[_pack_rows(t)[None] for t in (small_w, small_m, small_v)])
    outs = [o[0] for o in outs]
    shapes = [w.shape for w in small_w]
    small = {n: [t[j] for t in [_unpack_rows(o, shapes) for o in outs]]
             for j, n in enumerate(["norm_mix", "norm_ffn", "norm_final", "attn_sinks", "hgrn_norm", "hgrn_lb_logits", "ffn_conv_b"])}
    order = ["norm_mix", "norm_ffn", "norm_final", "attn_w_in", "attn_w_out", "attn_sinks", "hgrn_w_in", "hgrn_w_out",
             "hgrn_norm", "hgrn_lb_logits", "ffn_w_up", "ffn_conv_w", "ffn_conv_b", "ffn_w_down"]
    res = {**big, **small}
    return (loss, grad_x, *[res[n][0] for n in order], *[res[n][1] for n in order], *[res[n][2] for n in order],
            *[res[n][3] for n in order])
```

```python
import numpy as np
import jax
import jax.numpy as jnp
from jax import lax
from jax.experimental import pallas as pl
from jax.experimental.pallas import tpu as pltpu

F32 = jnp.float32
BF16 = jnp.bfloat16

D_MODEL = 1024
DEPTH = 4
HEAD_DIM = 64
N_Q_HEADS = 16
N_KV_HEADS = 4
Q_PER_KV = 4
ATTN_BLOCK = 128
ATTN_IN = 1536
HG_HEADS = 8
HG_K = 128
HG_CHUNK = 64
HG_IN = 4096
D_FF = 2816
EPS = 1e-6
N_DEV = 8
FF_SLOT = 2 * D_FF // N_DEV
HG_SLOT = HG_IN // N_DEV
HG_LEVELS = 6

ADAM_LR = 0.001
ADAM_B1 = 0.9
ADAM_B2 = 0.999
ADAM_EPS = 1e-08
ADAM_WD = 0.01
ADAM_STEP = 10

VMEM_LIMIT = 56 * 1024 * 1024
ROW_TILE = 1024
WIDE_ROW_TILE = 2048
NEG_BIG = -1e30

NN = (((1,), (0,)), ((), ()))
NT = (((1,), (1,)), ((), ()))
TN = (((0,), (0,)), ((), ()))


def _bdot(a, b, dn):
    return lax.dot_general(a.astype(BF16), b.astype(BF16), dn, preferred_element_type=F32)


def _sds(shape, dtype):
    return jax.ShapeDtypeStruct(tuple(shape), dtype)


def _params(sem):
    return pltpu.CompilerParams(dimension_semantics=sem, vmem_limit_bytes=VMEM_LIMIT)


DEP_SHAPE = (8, 128)


def _dep_spec(rank):
    return pl.BlockSpec(DEP_SHAPE, lambda *_: (0, 0))


def _matmul(name, a, b, *, dn, grid, a_spec, b_spec, o_spec, out_shape, acc_shape=None, extra=(), extra_specs=(),
            finish=None, dep=None, sem=("parallel", "parallel", "arbitrary")):
    nk = grid[2]
    many = isinstance(out_shape, (list, tuple))
    n_in = 2 + len(extra) + (dep is not None)
    n_out = len(out_shape) if many else 1

    def body(*refs):
        a_ref, b_ref = refs[0], refs[1]
        outs = refs[n_in:n_in + n_out]

        def prod():
            return _bdot(a_ref[...], b_ref[...], dn)

        def done(v):
            if finish is None:
                outs[0][...] = v.astype(outs[0].dtype)
            else:
                finish(v, refs[2:2 + len(extra)], outs)

        if nk == 1:
            done(prod())
        else:
            acc = refs[-1]
            k = pl.program_id(2)

            @pl.when(k == 0)
            def _():
                acc[...] = prod()

            @pl.when(k > 0)
            def _():
                acc[...] += prod()

            @pl.when(k == nk - 1)
            def _():
                done(acc[...])

    in_specs = [a_spec, b_spec, *extra_specs] + ([_dep_spec(3)] if dep is not None else [])
    args = (a, b, *extra) + ((dep,) if dep is not None else ())
    scratch = [] if nk == 1 else [pltpu.VMEM(acc_shape, F32)]
    return pl.pallas_call(
        body, name=name, grid=grid, in_specs=in_specs, out_specs=o_spec, out_shape=out_shape,
        scratch_shapes=scratch, compiler_params=_params(sem),
    )(*args)


def _rms(x):
    return lax.rsqrt(jnp.mean(x * x, axis=-1, keepdims=True) + EPS)


def _residual_finish(v, ex, outs):
    h = v + ex[0][...]
    outs[0][...] = h
    if len(ex) > 1:
        outs[1][...] = (h * _rms(h) * ex[1][...]).astype(outs[1].dtype)


def _norm_bwd_finish(v, ex, outs):
    x = ex[0][...]
    r = _rms(x)
    xh = x * r
    dyg = v * ex[1][...]
    outs[0][...] = ex[2][...] + r * (dyg - xh * jnp.mean(dyg * xh, axis=-1, keepdims=True))
    part = jnp.sum(v * xh, axis=0, keepdims=True)

    @pl.when(pl.program_id(0) == 0)
    def _():
        outs[1][...] = part

    @pl.when(pl.program_id(0) > 0)
    def _():
        outs[1][...] += part


def _row_io(tm, norm_g):
    row = pl.BlockSpec((tm, D_MODEL), lambda i, j, k: (i, 0))
    vec = pl.BlockSpec((1, D_MODEL), lambda i, j, k: (0, 0))
    if norm_g is None:
        return (row,), row, lambda S: _sds((S, D_MODEL), F32)
    return (row, vec), [row, row], lambda S: [_sds((S, D_MODEL), F32), _sds((S, D_MODEL), BF16)]


def _tile(n, t):
    return min(n, t)


def _proj_rows(name, hn, wt, l, out_dtype, dep=None):
    S, N = hn.shape[0], wt.shape[1]
    tm, tn = _tile(S, ROW_TILE), 512
    return _matmul(
        name, hn, wt, dn=NT, grid=(S // tm, N // tn, 1),
        a_spec=pl.BlockSpec((tm, D_MODEL), lambda i, j, k: (i, 0)),
        b_spec=pl.BlockSpec((None, tn, D_MODEL), lambda i, j, k: (l, j, 0)),
        o_spec=pl.BlockSpec((tm, tn), lambda i, j, k: (i, j)),
        out_shape=_sds((S, N), out_dtype), dep=dep)


def _slot_weight(w, transposed):
    if transposed:
        return w.shape[2], (None, None, w.shape[2], D_MODEL), NT, NN
    return w.shape[3], (None, None, D_MODEL, w.shape[3]), NN, NT


def _proj_slots(name, hn, w, l, transposed=False, dep=None):
    S = hn.shape[0]
    r, blk, dn, _ = _slot_weight(w, transposed)
    tm = _tile(S, WIDE_ROW_TILE)
    return _matmul(
        name, hn, w, dn=dn, grid=(N_DEV, S // tm, 1),
        a_spec=pl.BlockSpec((tm, D_MODEL), lambda j, i, k: (i, 0)),
        b_spec=pl.BlockSpec(blk, lambda j, i, k: (l, j, 0, 0)),
        o_spec=pl.BlockSpec((None, tm, r), lambda j, i, k: (j, i, 0)),
        out_shape=_sds((N_DEV, S, r), F32), dep=dep)


def _out_proj(name, o, w, l, h, norm_g=None):
    S, K = o.shape
    tm = _tile(S, ROW_TILE)
    extra_specs, o_spec, out_shape = _row_io(tm, norm_g)
    return _matmul(
        name, o, w, dn=NN, grid=(S // tm, 1, 1),
        a_spec=pl.BlockSpec((tm, K), lambda i, j, k: (i, 0)),
        b_spec=pl.BlockSpec((None, K, D_MODEL), lambda i, j, k: (l, 0, 0)),
        o_spec=o_spec, out_shape=out_shape(S), extra=(h,) if norm_g is None else (h, norm_g),
        extra_specs=extra_specs, finish=_residual_finish)


def _down_proj(name, a, w, l, h, norm_g=None):
    nj, S, r = a.shape
    tm = _tile(S, ROW_TILE)
    extra_specs, o_spec, out_shape = _row_io(tm, norm_g)
    return _matmul(
        name, a, w, dn=NN, grid=(S // tm, 1, nj),
        a_spec=pl.BlockSpec((None, tm, r), lambda i, j, k: (k, i, 0)),
        b_spec=pl.BlockSpec((None, None, r, D_MODEL), lambda i, j, k: (l, k, 0, 0)),
        o_spec=o_spec, out_shape=out_shape(S), acc_shape=(tm, D_MODEL),
        extra=(h,) if norm_g is None else (h, norm_g), extra_specs=extra_specs, finish=_residual_finish)


def _dgrad_down(name, dh, w, l, dep=None):
    S = dh.shape[0]
    nj, r = w.shape[1], w.shape[2]
    tm = _tile(S, ROW_TILE)
    return _matmul(
        name, dh, w, dn=NT, grid=(nj, S // tm, 1),
        a_spec=pl.BlockSpec((tm, D_MODEL), lambda j, i, k: (i, 0)),
        b_spec=pl.BlockSpec((None, None, r, D_MODEL), lambda j, i, k: (l, j, 0, 0)),
        o_spec=pl.BlockSpec((None, tm, r), lambda j, i, k: (j, i, 0)),
        out_shape=_sds((nj, S, r), BF16), dep=dep)


def _wgrad_down(name, a, dh):
    nj, S, r = a.shape
    tk = _tile(S, ROW_TILE)
    return _matmul(
        name, a, dh, dn=TN, grid=(nj, 1, S // tk),
        a_spec=pl.BlockSpec((None, tk, r), lambda s, j, k: (s, k, 0)),
        b_spec=pl.BlockSpec((tk, D_MODEL), lambda s, j, k: (k, 0)),
        o_spec=pl.BlockSpec((None, r, D_MODEL), lambda s, j, k: (s, 0, 0)),
        out_shape=_sds((nj, r, D_MODEL), BF16), acc_shape=(r, D_MODEL))


def _norm_bwd_io(tm, S):
    row = pl.BlockSpec((tm, D_MODEL), lambda i, j, k: (i, 0))
    vec = pl.BlockSpec((1, D_MODEL), lambda i, j, k: (0, 0))
    return dict(extra_specs=(row, vec, row), o_spec=[row, vec], out_shape=[_sds((S, D_MODEL), F32), _sds((1, D_MODEL), F32)],
                finish=_norm_bwd_finish, sem=("arbitrary", "arbitrary", "arbitrary"))


def _dgrad_slots(name, dz, w, l, norm, transposed=False):
    nj, S, r = dz.shape
    _, blk, _, dn = _slot_weight(w, transposed)
    tm = _tile(S, ROW_TILE)
    return _matmul(
        name, dz, w, dn=dn, grid=(S // tm, 1, nj),
        a_spec=pl.BlockSpec((None, tm, r), lambda i, j, k: (k, i, 0)),
        b_spec=pl.BlockSpec(blk, lambda i, j, k: (l, k, 0, 0)),
        acc_shape=(tm, D_MODEL), extra=norm, **_norm_bwd_io(tm, S))


def _wgrad_slots(name, hn, dz, transposed=False):
    nj, S, r = dz.shape
    tk = _tile(S, ROW_TILE)
    hn_spec = pl.BlockSpec((tk, D_MODEL), lambda s, j, k: (k, 0))
    dz_spec = pl.BlockSpec((None, tk, r), lambda s, j, k: (s, k, 0))
    if transposed:
        return _matmul(
            name, dz, hn, dn=TN, grid=(nj, 1, S // tk), a_spec=dz_spec, b_spec=hn_spec,
            o_spec=pl.BlockSpec((None, r, D_MODEL), lambda s, j, k: (s, 0, 0)),
            out_shape=_sds((nj, r, D_MODEL), BF16), acc_shape=(r, D_MODEL))
    return _matmul(
        name, hn, dz, dn=TN, grid=(nj, 1, S // tk), a_spec=hn_spec, b_spec=dz_spec,
        o_spec=pl.BlockSpec((None, D_MODEL, r), lambda s, j, k: (s, 0, 0)),
        out_shape=_sds((nj, D_MODEL, r), BF16), acc_shape=(D_MODEL, r))


def _dgrad_out(name, dh, w, l, out_dtype, dep=None):
    S, K = dh.shape[0], w.shape[1]
    tm = _tile(S, ROW_TILE)
    return _matmul(
        name, dh, w, dn=NT, grid=(S // tm, 1, 1),
        a_spec=pl.BlockSpec((tm, D_MODEL), lambda i, j, k: (i, 0)),
        b_spec=pl.BlockSpec((None, K, D_MODEL), lambda i, j, k: (l, 0, 0)),
        o_spec=pl.BlockSpec((tm, K), lambda i, j, k: (i, 0)),
        out_shape=_sds((S, K), out_dtype), dep=dep)


def _wgrad_rows(name, a, b):
    S, K = a.shape
    tk = _tile(S, ROW_TILE)
    return _matmul(
        name, a, b, dn=TN, grid=(1, 1, S // tk),
        a_spec=pl.BlockSpec((tk, K), lambda i, j, k: (k, 0)),
        b_spec=pl.BlockSpec((tk, D_MODEL), lambda i, j, k: (k, 0)),
        o_spec=pl.BlockSpec((K, D_MODEL), lambda i, j, k: (0, 0)),
        out_shape=_sds((K, D_MODEL), BF16), acc_shape=(K, D_MODEL))


def _dgrad_rows(name, dz, wt, l, norm):
    S, N = dz.shape
    tm = _tile(S, ROW_TILE)
    return _matmul(
        name, dz, wt, dn=NN, grid=(S // tm, 1, 1),
        a_spec=pl.BlockSpec((tm, N), lambda i, j, k: (i, 0)),
        b_spec=pl.BlockSpec((None, N, D_MODEL), lambda i, j, k: (l, 0, 0)),
        extra=norm, **_norm_bwd_io(tm, S))


def _rmsnorm_fwd(name, h, g):
    S = h.shape[0]
    tm = _tile(S, ROW_TILE)

    def body(h_ref, g_ref, o_ref):
        x = h_ref[...]
        o_ref[...] = (x * _rms(x) * g_ref[...]).astype(o_ref.dtype)

    row = pl.BlockSpec((tm, D_MODEL), lambda i: (i, 0))
    return pl.pallas_call(
        body, name=name, grid=(S // tm,), in_specs=[row, pl.BlockSpec((1, D_MODEL), lambda i: (0, 0))],
        out_specs=row, out_shape=_sds((S, D_MODEL), BF16), compiler_params=_params(("parallel",)),
    )(h, g)


def _loss_head(name, h, g, target):
    S = h.shape[0]
    tm = _tile(S, ROW_TILE)

    def body(h_ref, g_ref, t_ref, dh_ref, dg_ref, ls_ref):
        x = h_ref[...]
        r = lax.rsqrt(jnp.mean(x * x, axis=-1, keepdims=True) + EPS)
        xh = x * r
        diff = xh * g_ref[...] - t_ref[...]
        dyf = diff * (1.0 / D_MODEL)
        dyg = dyf * g_ref[...]
        dh_ref[...] = r * (dyg - xh * jnp.mean(dyg * xh, axis=-1, keepdims=True))
        part = jnp.sum(dyf * xh, axis=0, keepdims=True)
        lpart = jnp.sum(diff * diff, axis=0, keepdims=True) * (0.5 / D_MODEL)

        @pl.when(pl.program_id(0) == 0)
        def _():
            dg_ref[...] = part
            ls_ref[...] = lpart

        @pl.when(pl.program_id(0) > 0)
        def _():
            dg_ref[...] += part
            ls_ref[...] += lpart

    row = pl.BlockSpec((tm, D_MODEL), lambda i: (i, 0))
    vec = pl.BlockSpec((1, D_MODEL), lambda i: (0, 0))
    return pl.pallas_call(
        body, name=name, grid=(S // tm,), in_specs=[row, vec, row], out_specs=[row, vec, vec],
        out_shape=[_sds((S, D_MODEL), F32), _sds((1, D_MODEL), F32), _sds((1, D_MODEL), F32)],
        compiler_params=_params(("arbitrary",)),
    )(h, g, target)


ATTN_SCALE = HEAD_DIM ** -0.5
ALIBI_SLOPES = [2.0 ** (-8.0 * (h + 1) / N_Q_HEADS) for h in range(N_Q_HEADS)]
K_COL = N_Q_HEADS * HEAD_DIM
KV_COLS = N_KV_HEADS * HEAD_DIM
V_COL = K_COL + KV_COLS


def _attn_masks(n):
    qi = lax.broadcasted_iota(jnp.int32, (ATTN_BLOCK, ATTN_BLOCK), 0)
    ki = lax.broadcasted_iota(jnp.int32, (ATTN_BLOCK, ATTN_BLOCK), 1)
    dist_c = (qi - ki).astype(F32)
    return dist_c + float(ATTN_BLOCK), dist_c, (ki > qi) & (n > 0), qi >= ki


def _attn_probs(raw_p, raw_c, sink, slope, masks):
    dist_p, dist_c, valid_p, valid_c = masks
    sp = jnp.where(valid_p, raw_p * ATTN_SCALE - slope * dist_p, NEG_BIG)
    sc = jnp.where(valid_c, raw_c * ATTN_SCALE - slope * dist_c, NEG_BIG)
    m = jnp.maximum(jnp.maximum(jnp.max(sp, axis=-1, keepdims=True), jnp.max(sc, axis=-1, keepdims=True)), sink)
    ep, ec, es = jnp.exp(sp - m), jnp.exp(sc - m), jnp.exp(sink - m)
    inv = 1.0 / (jnp.sum(ep, axis=-1, keepdims=True) + jnp.sum(ec, axis=-1, keepdims=True) + es)
    return ep * inv, ec * inv, es * inv


def _group_rows(ref, m):
    return jnp.concatenate([ref[:, HEAD_DIM * (Q_PER_KV * m + g):HEAD_DIM * (Q_PER_KV * m + g + 1)]
                            for g in range(Q_PER_KV)], axis=0)


def _head_rows(x, g):
    return x[ATTN_BLOCK * g:ATTN_BLOCK * (g + 1)]


def _attn_specs(nblk):
    last = nblk - 1
    kcol, vcol = K_COL // KV_COLS, V_COL // KV_COLS
    return [
        pl.BlockSpec((ATTN_BLOCK, K_COL), lambda n: (jnp.minimum(n, last), 0)),
        pl.BlockSpec((ATTN_BLOCK, KV_COLS), lambda n: (jnp.minimum(n, last), kcol)),
        pl.BlockSpec((ATTN_BLOCK, KV_COLS), lambda n: (jnp.maximum(jnp.minimum(n, last) - 1, 0), kcol)),
        pl.BlockSpec((ATTN_BLOCK, KV_COLS), lambda n: (jnp.minimum(n, last), vcol)),
        pl.BlockSpec((ATTN_BLOCK, KV_COLS), lambda n: (jnp.maximum(jnp.minimum(n, last) - 1, 0), vcol)),
    ]


P_COLS = 2 * ATTN_BLOCK


def _attn_fwd(name, proj, sinks):
    S = proj.shape[0]
    nblk = S // ATTN_BLOCK

    def body(q_ref, kc_ref, kp_ref, vc_ref, vp_ref, sk_ref, o_ref, p_ref, ps_ref):
        masks = _attn_masks(pl.program_id(0))
        lane = lax.broadcasted_iota(jnp.int32, (ATTN_BLOCK, 128), 1)
        sink_p = jnp.zeros((ATTN_BLOCK, 128), F32)
        for m in range(N_KV_HEADS):
            ks = slice(HEAD_DIM * m, HEAD_DIM * (m + 1))
            kp, kc, vp, vc = kp_ref[:, ks], kc_ref[:, ks], vp_ref[:, ks], vc_ref[:, ks]
            q4 = _group_rows(q_ref, m)
            raw_p, raw_c = _bdot(q4, kp, NT), _bdot(q4, kc, NT)
            pps, pcs = [], []
            for g in range(Q_PER_KV):
                hh = Q_PER_KV * m + g
                pp, pc, ps = _attn_probs(_head_rows(raw_p, g), _head_rows(raw_c, g), sk_ref[0, hh], ALIBI_SLOPES[hh], masks)
                pps.append(pp.astype(BF16))
                pcs.append(pc.astype(BF16))
                p_ref[:, P_COLS * hh:P_COLS * hh + ATTN_BLOCK] = pps[g]
                p_ref[:, P_COLS * hh + ATTN_BLOCK:P_COLS * (hh + 1)] = pcs[g]
                sink_p = jnp.where(lane == hh, ps, sink_p)
            o4 = _bdot(jnp.concatenate(pps, axis=0), vp, NN) + _bdot(jnp.concatenate(pcs, axis=0), vc, NN)
            for g in range(Q_PER_KV):
                hh = Q_PER_KV * m + g
                o_ref[:, HEAD_DIM * hh:HEAD_DIM * (hh + 1)] = _head_rows(o4, g).astype(o_ref.dtype)
        ps_ref[...] = sink_p

    row = lambda cols: pl.BlockSpec((ATTN_BLOCK, cols), lambda n: (n, 0))
    return pl.pallas_call(
        body, name=name, grid=(nblk,),
        in_specs=_attn_specs(nblk) + [pl.BlockSpec(memory_space=pltpu.SMEM)],
        out_specs=[row(K_COL), row(N_Q_HEADS * P_COLS), row(128)],
        out_shape=[_sds((S, K_COL), BF16), _sds((S, N_Q_HEADS * P_COLS), BF16), _sds((S, 128), F32)],
        compiler_params=_params(("parallel",)),
    )(proj, proj, proj, proj, proj, sinks)


def _attn_bwd(name, proj, probs, sink_probs, do):
    S = proj.shape[0]
    nblk = S // ATTN_BLOCK

    def body(q_ref, kc_ref, kp_ref, vc_ref, vp_ref, do_ref, p_ref, ps_ref, dz_ref, ds_ref, carry, cur, padd):
        n = pl.program_id(0)

        @pl.when(n == 0)
        def _():
            carry[...] = jnp.zeros_like(carry)
            ds_ref[...] = jnp.zeros_like(ds_ref)

        @pl.when(n < nblk)
        def _():
            lane = lax.broadcasted_iota(jnp.int32, (ATTN_BLOCK, 128), 1)
            sink_p = ps_ref[...]
            dsv = jnp.zeros((1, 128), F32)
            for m in range(N_KV_HEADS):
                ks = slice(HEAD_DIM * m, HEAD_DIM * (m + 1))
                kp, kc, vp, vc = kp_ref[:, ks], kc_ref[:, ks], vp_ref[:, ks], vc_ref[:, ks]
                q4, do4 = _group_rows(q_ref, m), _group_rows(do_ref, m)
                dpp4, dpc4 = _bdot(do4, vp, NT), _bdot(do4, vc, NT)
                pps, pcs, dsps, dscs = [], [], [], []
                for g in range(Q_PER_KV):
                    hh = Q_PER_KV * m + g
                    pps.append(p_ref[:, P_COLS * hh:P_COLS * hh + ATTN_BLOCK])
                    pcs.append(p_ref[:, P_COLS * hh + ATTN_BLOCK:P_COLS * (hh + 1)])
                    pp, pc = pps[g].astype(F32), pcs[g].astype(F32)
                    dpp, dpc = _head_rows(dpp4, g), _head_rows(dpc4, g)
                    delta = jnp.sum(pp * dpp, axis=-1, keepdims=True) + jnp.sum(pc * dpc, axis=-1, keepdims=True)
                    dsv = dsv - jnp.sum(jnp.where(lane == hh, sink_p, 0.0) * delta, axis=0, keepdims=True)
                    dsps.append((pp * (dpp - delta)).astype(BF16))
                    dscs.append((pc * (dpc - delta)).astype(BF16))
                pp4, pc4 = jnp.concatenate(pps, axis=0), jnp.concatenate(pcs, axis=0)
                dsp4, dsc4 = jnp.concatenate(dsps, axis=0), jnp.concatenate(dscs, axis=0)
                dq4 = (_bdot(dsp4, kp, NN) + _bdot(dsc4, kc, NN)) * ATTN_SCALE
                for g in range(Q_PER_KV):
                    hh = Q_PER_KV * m + g
                    cur[:, HEAD_DIM * hh:HEAD_DIM * (hh + 1)] = _head_rows(dq4, g)
                cur[:, K_COL + HEAD_DIM * m:K_COL + HEAD_DIM * (m + 1)] = _bdot(dsc4, q4, TN) * ATTN_SCALE
                cur[:, V_COL + HEAD_DIM * m:V_COL + HEAD_DIM * (m + 1)] = _bdot(pc4, do4, TN)
                padd[:, ks] = _bdot(dsp4, q4, TN) * ATTN_SCALE
                padd[:, KV_COLS + HEAD_DIM * m:KV_COLS + HEAD_DIM * (m + 1)] = _bdot(pp4, do4, TN)
            ds_ref[...] += dsv
            dz_ref[:, :K_COL] = carry[:, :K_COL].astype(dz_ref.dtype)
            dz_ref[:, K_COL:] = (carry[:, K_COL:] + padd[...]).astype(dz_ref.dtype)
            carry[...] = cur[...]

        @pl.when(n == nblk)
        def _():
            dz_ref[...] = carry[...].astype(dz_ref.dtype)

    return pl.pallas_call(
        body, name=name, grid=(nblk + 1,),
        in_specs=_attn_specs(nblk) + [
            pl.BlockSpec((ATTN_BLOCK, cols), lambda n: (jnp.minimum(n, nblk - 1), 0))
            for cols in (K_COL, N_Q_HEADS * P_COLS, 128)],
        out_specs=[pl.BlockSpec((ATTN_BLOCK, ATTN_IN), lambda n: (jnp.maximum(n - 1, 0), 0)),
                   pl.BlockSpec((1, 128), lambda n: (0, 0))],
        out_shape=[_sds((S, ATTN_IN), BF16), _sds((1, 128), F32)],
        scratch_shapes=[pltpu.VMEM((ATTN_BLOCK, ATTN_IN), F32), pltpu.VMEM((ATTN_BLOCK, ATTN_IN), F32),
                        pltpu.VMEM((ATTN_BLOCK, 2 * KV_COLS), F32)],
        compiler_params=_params(("arbitrary",)),
    )(proj, proj, proj, proj, proj, do, probs, sink_probs)


def _hg_consts():
    C = HG_CHUNK
    tri = np.tril(np.ones((C, C)))
    t = np.arange(C)
    rows, masks = [tri], []
    for lvl in range(HG_LEVELS):
        n = C >> (lvl + 1)
        sel = np.zeros((C, C))
        sel[t, (t // (2 * n)) * (2 * n) + n - 1] = 1.0
        rows.append(sel @ tri)
        tt, ss = t[:, None], t[None, :]
        masks.append((tt // (2 * n) == ss // (2 * n)) & ((tt // n) % 2 == 1) & ((ss // n) % 2 == 0))
    masks.append(np.eye(C, dtype=bool))
    stk = np.concatenate(rows, axis=0)
    return jnp.asarray(stk, BF16), jnp.asarray(np.stack(masks), F32)


def _sigmoid(x):
    return 1.0 / (1.0 + jnp.exp(-x))


def _split(x, parts):
    out, rest = [], x
    for _ in range(parts):
        out.append(rest.astype(BF16))
        rest = rest - out[-1].astype(F32)
    return out


def _dot01(m01, x, dn, parts=3):
    return sum(lax.dot_general(m01, p, dn, preferred_element_type=F32) for p in _split(x, parts))


def _ref_rows(b, n):
    C = b.shape[1]
    if 2 * n >= 8:
        b3 = b.reshape(HG_CHUNK // (2 * n), 2 * n, C)
        return jnp.broadcast_to(b3[:, n - 1:n, :], b3.shape).reshape(HG_CHUNK, C)
    pos = lax.broadcasted_iota(jnp.int32, b.shape, 0) % (2 * n)
    out = b
    for p in range(2 * n):
        if p != n - 1:
            out = jnp.where(pos == p, pltpu.roll(b, (p - (n - 1)) % HG_CHUNK, 0), out)
    return out


def _hg_common(z_ref, lb_ref, stk_ref):
    qr, fr = z_ref[0], z_ref[1]
    lb = lb_ref[...]
    sq, sg, sgn = _sigmoid(qr), _sigmoid(fr), _sigmoid(-fr)
    ft = lb + (1.0 - lb) * sg
    b = _dot01(stk_ref[0:HG_CHUNK, :], jnp.log(ft), NN)
    ws = [jnp.exp(-jnp.abs(b - _ref_rows(b, HG_CHUNK >> (l + 1)))) for l in range(HG_LEVELS)]
    blast = b[HG_CHUNK - 1:HG_CHUNK]
    return dict(qr=qr, fr=fr, lb=lb, sq=sq, sg=sg, sgn=sgn, ft=ft, q=qr * sq, kk=(1.0 - lb) * sgn, b=b,
                ws=ws, eb=jnp.exp(b), ed=jnp.exp(blast - b), elast=jnp.exp(blast))


def _hg_factors(qh, kh, ws, sl):
    return ([(qh * ws[l][:, sl]).astype(BF16) for l in range(HG_LEVELS)],
            [(kh * ws[l][:, sl]).astype(BF16) for l in range(HG_LEVELS)])


def _hg_intra(qh, kh, ws, msk_ref, sl):
    qls, kls = _hg_factors(qh, kh, ws, sl)
    a = msk_ref[HG_LEVELS] * _bdot(qh, kh, NT)
    for l in range(HG_LEVELS):
        a = a + msk_ref[l] * _bdot(qls[l], kls[l], NT)
    return a


def _hg_fwd(name, z, lb, ng):
    S = z.shape[2]
    nc = S // HG_CHUNK
    stk, msk = _hg_consts()

    def body(z_ref, lb_ref, ng_ref, stk_ref, msk_ref, og_ref, st_ref, a_ref, o_ref, state):
        @pl.when(pl.program_id(1) == 0)
        def _():
            state[...] = jnp.zeros_like(state)

        cm = _hg_common(z_ref, lb_ref, stk_ref)
        v, gt = z_ref[2], z_ref[3]
        kd = cm["kk"] * cm["ed"]
        for hh in range(4):
            sl = slice(HG_K * hh, HG_K * (hh + 1))
            st = state[hh]
            st_ref[hh] = st
            qh, kh, vh = cm["q"][:, sl], cm["kk"][:, sl], v[:, sl]
            a = _hg_intra(qh, kh, cm["ws"], msk_ref, sl).astype(BF16)
            a_ref[hh] = a
            o = _bdot(a, vh, NN) + _bdot(qh * cm["eb"][:, sl], st, NT)
            o_ref[:, sl] = o
            state[hh] = cm["elast"][:, sl] * st + _bdot(vh, kd[:, sl], TN)
            r = lax.rsqrt(jnp.mean(o * o, axis=-1, keepdims=True) + EPS)
            gh = gt[:, sl]
            og_ref[:, sl] = (o * r * ng_ref[...] * (gh * _sigmoid(gh))).astype(og_ref.dtype)

    return pl.pallas_call(
        body, name=name, grid=(2, nc),
        in_specs=[pl.BlockSpec((4, None, HG_CHUNK, HG_SLOT), lambda g, c: (0, g, c, 0)),
                  pl.BlockSpec((1, HG_SLOT), lambda g, c: (0, g)),
                  pl.BlockSpec((1, HG_K), lambda g, c: (0, 0)),
                  pl.BlockSpec(stk.shape, lambda g, c: (0, 0)),
                  pl.BlockSpec(msk.shape, lambda g, c: (0, 0, 0))],
        out_specs=[pl.BlockSpec((HG_CHUNK, HG_SLOT), lambda g, c: (c, g)),
                   pl.BlockSpec((None, 4, HG_K, HG_K), lambda g, c: (c, g, 0, 0)),
                   pl.BlockSpec((None, 4, HG_CHUNK, HG_CHUNK), lambda g, c: (c, g, 0, 0)),
                   pl.BlockSpec((HG_CHUNK, HG_SLOT), lambda g, c: (c, g))],
        out_shape=[_sds((S, D_MODEL), BF16), _sds((nc, HG_HEADS, HG_K, HG_K), F32),
                   _sds((nc, HG_HEADS, HG_CHUNK, HG_CHUNK), BF16), _sds((S, D_MODEL), F32)],
        scratch_shapes=[pltpu.VMEM((4, HG_K, HG_K), F32)],
        compiler_params=_params(("parallel", "arbitrary")),
    )(z, lb, ng, stk, msk)


def _hg_bwd(name, z, lb, ng, states, intra, o_pre, dog):
    S = z.shape[2]
    nc = S // HG_CHUNK
    stk, msk = _hg_consts()

    def body(z_ref, lb_ref, ng_ref, stk_ref, msk_ref, st_ref, a_ref, o_ref, dog_ref, dz_ref, dlb_ref, dng_ref, dstate):
        @pl.when(pl.program_id(1) == 0)
        def _():
            dstate[...] = jnp.zeros_like(dstate)
            dlb_ref[...] = jnp.zeros_like(dlb_ref)
            dng_ref[...] = jnp.zeros_like(dng_ref)

        cm = _hg_common(z_ref, lb_ref, stk_ref)
        v, gt = z_ref[2], z_ref[3]
        ng = ng_ref[...]
        kd = cm["kk"] * cm["ed"]
        row = lax.broadcasted_iota(jnp.int32, (HG_CHUNK, 1), 0)
        dng = jnp.zeros((1, HG_K), F32)
        dq_h, dkk_h, db_h, dv_h, dgt_h = [], [], [], [], []
        dr_h = [[] for _ in range(HG_LEVELS)]
        for hh in range(4):
            sl = slice(HG_K * hh, HG_K * (hh + 1))
            st, dst = st_ref[hh], dstate[hh]
            qh, kh, vh, ebh, edh, kdh = cm["q"][:, sl], cm["kk"][:, sl], v[:, sl], cm["eb"][:, sl], cm["ed"][:, sl], kd[:, sl]
            elh = cm["elast"][:, sl]
            qls, kls = _hg_factors(qh, kh, cm["ws"], sl)
            a, o = a_ref[hh], o_ref[:, sl]
            qe = qh * ebh
            r = lax.rsqrt(jnp.mean(o * o, axis=-1, keepdims=True) + EPS)
            xh = o * r
            gh = gt[:, sl]
            sgg = _sigmoid(gh)
            dog = dog_ref[:, sl].astype(F32)
            dy = dog * (gh * sgg)
            dgt_h.append(dog * (xh * ng) * (sgg * (1.0 + gh * (1.0 - sgg))))
            dng = dng + jnp.sum(dy * xh, axis=0, keepdims=True)
            dyg = dy * ng
            do = r * (dyg - xh * jnp.mean(dyg * xh, axis=-1, keepdims=True))
            da = _bdot(do, vh, NT)
            dv_h.append(_bdot(a, do, TN) + _bdot(kdh, dst, NT))
            dkd = _bdot(vh, dst, NN)
            delast = jnp.sum(st * dst, axis=0, keepdims=True)
            dqe = _bdot(do, st, NN)
            dstate[hh] = elh * dst + _bdot(do, qe, TN)
            gk = dkd * kdh
            dblast = jnp.sum(gk, axis=0, keepdims=True) + delast * elh
            db = dqe * qe - gk + jnp.where(row == HG_CHUNK - 1, dblast, 0.0)
            dp = (msk_ref[HG_LEVELS] * da).astype(BF16)
            dq = dqe * ebh + _bdot(dp, kh, NN)
            dkk = dkd * edh + _bdot(dp, qh, TN)
            for l in range(HG_LEVELS):
                dp = (msk_ref[l] * da).astype(BF16)
                dql, dkl = _bdot(dp, kls[l], NN), _bdot(dp, qls[l], TN)
                w = cm["ws"][l][:, sl]
                dq = dq + dql * w
                dkk = dkk + dkl * w
                half = jnp.where(((row >> (HG_LEVELS - 1 - l)) & 1) == 1, 1.0, -1.0)
                dd = half * w * (dql * qh + dkl * kh)
                db = db + dd
                dr_h[l].append(-dd)
            dq_h.append(dq)
            dkk_h.append(dkk)
            db_h.append(db)
        cat = lambda xs: jnp.concatenate(xs, axis=1)
        cot = jnp.concatenate([cat(db_h)] + [cat(dr_h[l]) for l in range(HG_LEVELS)], axis=0)
        dlf = _dot01(stk_ref[...], cot, TN, parts=2)
        dq, dkk = cat(dq_h), cat(dkk_h)
        dft = dlf / cm["ft"]
        one_lb = 1.0 - cm["lb"]
        dz_ref[0] = (dq * (cm["sq"] * (1.0 + cm["qr"] * (1.0 - cm["sq"])))).astype(dz_ref.dtype)
        dz_ref[1] = ((dft - dkk) * one_lb * cm["sg"] * cm["sgn"]).astype(dz_ref.dtype)
        dz_ref[2] = cat(dv_h).astype(dz_ref.dtype)
        dz_ref[3] = cat(dgt_h).astype(dz_ref.dtype)
        dlb_ref[...] += jnp.sum((dft - dkk) * cm["sgn"], axis=0, keepdims=True)
        dng_ref[...] += dng

    rev = lambda c: nc - 1 - c
    return pl.pallas_call(
        body, name=name, grid=(2, nc),
        in_specs=[pl.BlockSpec((4, None, HG_CHUNK, HG_SLOT), lambda g, c: (0, g, rev(c), 0)),
                  pl.BlockSpec((1, HG_SLOT), lambda g, c: (0, g)),
                  pl.BlockSpec((1, HG_K), lambda g, c: (0, 0)),
                  pl.BlockSpec(stk.shape, lambda g, c: (0, 0)),
                  pl.BlockSpec(msk.shape, lambda g, c: (0, 0, 0)),
                  pl.BlockSpec((None, 4, HG_K, HG_K), lambda g, c: (rev(c), g, 0, 0)),
                  pl.BlockSpec((None, 4, HG_CHUNK, HG_CHUNK), lambda g, c: (rev(c), g, 0, 0)),
                  pl.BlockSpec((HG_CHUNK, HG_SLOT), lambda g, c: (rev(c), g)),
                  pl.BlockSpec((HG_CHUNK, HG_SLOT), lambda g, c: (rev(c), g))],
        out_specs=[pl.BlockSpec((4, None, HG_CHUNK, HG_SLOT), lambda g, c: (0, g, rev(c), 0)),
                   pl.BlockSpec((1, HG_SLOT), lambda g, c: (0, g)),
                   pl.BlockSpec((None, 1, HG_K), lambda g, c: (g, 0, 0))],
        out_shape=[_sds(z.shape, BF16), _sds((1, 2 * HG_SLOT), F32), _sds((2, 1, HG_K), F32)],
        scratch_shapes=[pltpu.VMEM((4, HG_K, HG_K), F32)],
        compiler_params=_params(("parallel", "arbitrary")),
    )(z, lb, ng, stk, msk, states, intra, o_pre, dog)


def _lb_fwd(name, logits):
    def body(l_ref, o_ref):
        x = l_ref[...]
        e = jnp.exp(x - jnp.max(x, axis=0, keepdims=True))
        s = e / jnp.sum(e, axis=0, keepdims=True)
        o_ref[0:1, :] = s[1:2]
        o_ref[1:2, :] = s[1:2] + s[2:3] + s[3:4]

    return pl.pallas_call(body, name=name, out_shape=_sds((2, logits.shape[1]), F32))(logits)


def _lb_bwd(name, logits, dlb):
    def body(l_ref, d_ref, o_ref):
        x = l_ref[...]
        e = jnp.exp(x - jnp.max(x, axis=0, keepdims=True))
        s = e / jnp.sum(e, axis=0, keepdims=True)
        d1, d3 = d_ref[0:1, :], d_ref[1:2, :]
        ds = [jnp.zeros_like(d1), d1 + d3, d3, d3]
        dot = sum(ds[r] * s[r:r + 1] for r in range(1, DEPTH))
        for r in range(DEPTH):
            o_ref[r:r + 1, :] = s[r:r + 1] * (ds[r] - dot)

    return pl.pallas_call(body, name=name, out_shape=_sds(logits.shape, F32))(logits, dlb)


SUB = 8


def _rows_down(x, prev, k):
    row = lax.broadcasted_iota(jnp.int32, x.shape, 0)
    return jnp.where(row >= k, pltpu.roll(x, k, 0), pltpu.roll(prev, k, 0))


def _rows_up(x, nxt, k):
    row = lax.broadcasted_iota(jnp.int32, x.shape, 0)
    return jnp.where(row < SUB - k, pltpu.roll(x, SUB - k, 0), pltpu.roll(nxt, SUB - k, 0))


def _conv_block(w_ref, b_ref, p, x, prev):
    x0, x1 = _rows_down(x, prev, 2), _rows_down(x, prev, 1)
    return b_ref[p] + w_ref[p, 0:1, :] * x0 + w_ref[p, 1:2, :] * x1 + w_ref[p, 2:3, :] * x, x0, x1


def _convgate_fwd(name, u, cw, cb):
    S = u.shape[2]
    tm = _tile(S, ROW_TILE)

    def body(u_ref, w_ref, b_ref, a_ref, halo):
        @pl.when(pl.program_id(1) == 0)
        def _():
            halo[...] = jnp.zeros_like(halo)

        def step(r, prev):
            pg, pv = prev
            out = []
            for s in range(2):
                rows = pl.ds(pl.multiple_of(r * 2 * SUB + s * SUB, SUB), SUB)
                xg, xv = u_ref[0, rows, :], u_ref[1, rows, :]
                cg = _conv_block(w_ref, b_ref, 0, xg, pg)[0]
                cv = _conv_block(w_ref, b_ref, 1, xv, pv)[0]
                out.append(cg * _sigmoid(cg) * cv)
                pg, pv = xg, xv
            a_ref[pl.ds(pl.multiple_of(r * 2 * SUB, 2 * SUB), 2 * SUB), :] = jnp.concatenate(out, axis=0).astype(a_ref.dtype)
            return pg, pv

        pg, pv = lax.fori_loop(0, tm // (2 * SUB), step, (halo[0], halo[1]))
        halo[0] = pg
        halo[1] = pv

    return pl.pallas_call(
        body, name=name, grid=(4, S // tm),
        in_specs=[pl.BlockSpec((2, None, tm, FF_SLOT), lambda j, t: (0, j, t, 0)),
                  pl.BlockSpec((2, None, 3, FF_SLOT), lambda j, t: (0, j, 0, 0)),
                  pl.BlockSpec((2, None, 1, FF_SLOT), lambda j, t: (0, j, 0, 0))],
        out_specs=pl.BlockSpec((None, tm, FF_SLOT), lambda j, t: (j, t, 0)),
        out_shape=_sds((4, S, FF_SLOT), BF16),
        scratch_shapes=[pltpu.VMEM((2, SUB, FF_SLOT), F32)],
        compiler_params=_params(("parallel", "arbitrary")),
    )(u, cw, cb)


def _convgate_bwd(name, u, cw, cb, da):
    S = u.shape[2]
    tm = _tile(S, ROW_TILE)
    nt = S // tm

    def body(u_ref, uh_ref, w_ref, b_ref, da_ref, du_out, dw_ref, db_ref, after, first, acc, du_ref):
        t = pl.program_id(1)

        @pl.when(t == 0)
        def _():
            after[...] = jnp.zeros_like(after)
            acc[...] = jnp.zeros_like(acc)

        def du_block(p, d, nxt):
            return w_ref[p, 2:3, :] * d + w_ref[p, 1:2, :] * _rows_up(d, nxt, 1) + w_ref[p, 0:1, :] * _rows_up(d, nxt, 2)

        def step(r, carry):
            pg, pv, dg_last, dv_last = carry
            dav = da_ref[pl.ds(pl.multiple_of(r * 2 * SUB, 2 * SUB), 2 * SUB), :].astype(F32)
            for s in range(2):
                at = r * 2 * SUB + s * SUB
                rows = pl.ds(pl.multiple_of(at, SUB), SUB)
                xg, xv = u_ref[0, rows, :], u_ref[1, rows, :]
                cg, x0g, x1g = _conv_block(w_ref, b_ref, 0, xg, pg)
                cv, x0v, x1v = _conv_block(w_ref, b_ref, 1, xv, pv)
                sg = _sigmoid(cg)
                dab = dav[s * SUB:(s + 1) * SUB]
                dg = dab * cv * (sg * (1.0 + cg * (1.0 - sg)))
                dv = dab * cg * sg
                for p, d, taps in ((0, dg, (x0g, x1g, xg)), (1, dv, (x0v, x1v, xv))):
                    for j in range(3):
                        acc[p, j] += d * taps[j]
                    acc[p, 3] += d
                if s == 0:
                    @pl.when(r == 0)
                    def _():
                        first[0] = dg
                        first[1] = dv

                    @pl.when(r > 0)
                    def _():
                        before = pl.ds(pl.multiple_of(at - SUB, SUB), SUB)
                        du_ref[0, before, :] = du_block(0, dg_last, dg)
                        du_ref[1, before, :] = du_block(1, dv_last, dv)
                else:
                    before = pl.ds(pl.multiple_of(at - SUB, SUB), SUB)
                    du_ref[0, before, :] = du_block(0, dg_last, dg)
                    du_ref[1, before, :] = du_block(1, dv_last, dv)
                pg, pv, dg_last, dv_last = xg, xv, dg, dv
            return pg, pv, dg_last, dv_last

        halo = jnp.where(t < nt - 1, uh_ref[...], 0.0)
        zero = jnp.zeros((SUB, FF_SLOT), F32)
        _, _, dg_last, dv_last = lax.fori_loop(0, tm // (2 * SUB), step, (halo[0], halo[1], zero, zero))
        du_ref[0, tm - SUB:tm, :] = du_block(0, dg_last, after[0])
        du_ref[1, tm - SUB:tm, :] = du_block(1, dv_last, after[1])
        du_out[...] = du_ref[...].astype(du_out.dtype)
        after[...] = first[...]
        for p in range(2):
            for j in range(3):
                dw_ref[p, j:j + 1, :] = jnp.sum(acc[p, j], axis=0, keepdims=True)
            db_ref[p] = jnp.sum(acc[p, 3], axis=0, keepdims=True)

    rev = lambda t: nt - 1 - t
    return pl.pallas_call(
        body, name=name, grid=(4, nt),
        in_specs=[pl.BlockSpec((2, None, tm, FF_SLOT), lambda j, t: (0, j, rev(t), 0)),
                  pl.BlockSpec((2, None, SUB, FF_SLOT), lambda j, t: (0, j, jnp.maximum(rev(t) * (tm // SUB) - 1, 0), 0)),
                  pl.BlockSpec((2, None, 3, FF_SLOT), lambda j, t: (0, j, 0, 0)),
                  pl.BlockSpec((2, None, 1, FF_SLOT), lambda j, t: (0, j, 0, 0)),
                  pl.BlockSpec((None, tm, FF_SLOT), lambda j, t: (j, rev(t), 0))],
        out_specs=[pl.BlockSpec((2, None, tm, FF_SLOT), lambda j, t: (0, j, rev(t), 0)),
                   pl.BlockSpec((2, None, 3, FF_SLOT), lambda j, t: (0, j, 0, 0)),
                   pl.BlockSpec((2, None, 1, FF_SLOT), lambda j, t: (0, j, 0, 0))],
        out_shape=[_sds(u.shape, BF16), _sds(cw.shape, F32), _sds(cb.shape, F32)],
        scratch_shapes=[pltpu.VMEM((2, SUB, FF_SLOT), F32), pltpu.VMEM((2, SUB, FF_SLOT), F32),
                        pltpu.VMEM((2, 4, SUB, FF_SLOT), F32), pltpu.VMEM((2, tm, FF_SLOT), F32)],
        compiler_params=_params(("parallel", "arbitrary")),
    )(u, u, cw, cb, da)


def _row_tile(R):
    for t in range(256, 15, -16):
        if R % t == 0:
            return t
    return R


def _adamw(name, gsrcs, w, m, v, dep=None):
    L = len(gsrcs)
    n, A, C = gsrcs[0].shape
    tr = _row_tile(A)
    deps = () if dep is None else (dep,)

    def body(*refs):
        g_refs = refs[:L]
        w_ref, m_ref, v_ref = refs[L:L + 3]
        go_ref, d_ref, mo_ref, vo_ref = refs[L + 3 + len(deps):]
        for k in range(L):
            @pl.when(pl.program_id(0) == k)
            def _(k=k):
                g = g_refs[k][0].astype(F32)
                for s in range(1, n):
                    g = g + g_refs[k][s].astype(F32)
                m2 = ADAM_B1 * m_ref[...] + (1.0 - ADAM_B1) * g
                v2 = ADAM_B2 * v_ref[...] + (1.0 - ADAM_B2) * (g * g)
                m_hat = m2 / (1.0 - ADAM_B1 ** ADAM_STEP)
                v_hat = v2 / (1.0 - ADAM_B2 ** ADAM_STEP)
                go_ref[...] = g
                d_ref[...] = -ADAM_LR * (m_hat / (jnp.sqrt(v_hat) + ADAM_EPS) + ADAM_WD * w_ref[...])
                mo_ref[...] = m2
                vo_ref[...] = v2

    g_specs = [pl.BlockSpec((n, tr, C), lambda l, i, k=k: (0, jnp.where(l == k, i, 0), 0)) for k in range(L)]
    blk = pl.BlockSpec((None, tr, C), lambda l, i: (l, i, 0))
    return pl.pallas_call(
        body, name=name, grid=(L, A // tr), in_specs=g_specs + [blk, blk, blk] + [_dep_spec(2)] * len(deps),
        out_specs=[blk] * 4, out_shape=[_sds((L, A, C), F32)] * 4, compiler_params=_params(("parallel", "parallel")),
    )(*gsrcs, w, m, v, *deps)


MESH = pl.DeviceIdType.MESH
HBM_SPEC = pl.BlockSpec(memory_space=pltpu.HBM)
N_PEERS = N_DEV - 1


def _mesh_place():
    x, y, c = lax.axis_index("x"), lax.axis_index("y"), lax.axis_index("c")
    peers = []
    for p in range(1, N_DEV):
        px = 1 - x if p & 4 else x
        py = 1 - y if p & 2 else y
        pc = 1 - c if p & 1 else c
        peers.append(((px, py, pc), 4 * px + 2 * py + pc))
    return 4 * x + 2 * y + c, peers


SEM_SPEC = pl.BlockSpec(memory_space=pltpu.SEMAPHORE)
ANY_SPEC = pl.BlockSpec(memory_space=pl.ANY)
EFFECT = pltpu.SideEffectType.DATAFLOW_SIDE_EFFECTING


def _exchange_refs(scatter, src, land, send, recv, k, p, dev, idx, me):
    return pltpu.make_async_remote_copy(src_ref=src[k].at[idx] if scatter else src[k], dst_ref=land[k].at[me],
                                        send_sem=send.at[k * N_PEERS + p], recv_sem=recv.at[k * N_PEERS + p], device_id=dev,
                                        device_id_type=MESH)


def _exchange_start(name, srcs, scatter, gate):
    n = len(srcs)
    lands = [lax.empty(s.shape if scatter else (N_DEV,) + s.shape, s.dtype) for s in srcs]

    def body(*refs):
        src, land = refs[:n], refs[n:2 * n]
        send, recv, own = refs[2 * n + 1:2 * n + 4]
        token = refs[-1]
        me, peers = _mesh_place()
        for k in range(n):
            pltpu.make_async_copy(src[k].at[me] if scatter else src[k], land[k].at[me], own.at[k]).start()
            for p, (dev, idx) in enumerate(peers):
                _exchange_refs(scatter, src, land, send, recv, k, p, dev, idx, me).start()
        token[...] = jnp.zeros_like(token)

    hbm = lambda a: pltpu.HBM(a.shape, a.dtype)
    outs = pl.pallas_call(
        body, name=name,
        out_shape=(pltpu.SemaphoreType.DMA((n * N_PEERS,)), pltpu.SemaphoreType.DMA((n * N_PEERS,)),
                   pltpu.SemaphoreType.DMA((n,)), *[hbm(s) for s in srcs], *[hbm(s) for s in lands], _sds(DEP_SHAPE, F32)),
        in_specs=[HBM_SPEC] * (2 * n) + [ANY_SPEC],
        out_specs=(SEM_SPEC, SEM_SPEC, SEM_SPEC, *[HBM_SPEC] * (2 * n), pl.BlockSpec(memory_space=pltpu.VMEM)),
        input_output_aliases={j: 3 + j for j in range(2 * n)},
        compiler_params=pltpu.CompilerParams(has_side_effects=EFFECT),
    )(*[pltpu.with_memory_space_constraint(s, pltpu.HBM) for s in srcs],
      *[pltpu.with_memory_space_constraint(s, pltpu.HBM) for s in lands], gate)
    return outs[:3], None, list(outs[3:3 + n]), list(outs[3 + n:3 + 2 * n]), outs[-1]


def _exchange_wait(name, started, scatter, after):
    (send, recv, own), _, srcs, lands, _ = started
    n = len(srcs)

    def body(*refs):
        src, land = refs[:n], refs[n:2 * n]
        send, recv, own = refs[2 * n:2 * n + 3]
        me, peers = _mesh_place()
        for k in range(n):
            pltpu.make_async_copy(src[k].at[me] if scatter else src[k], land[k].at[me], own.at[k]).wait()
            for p, (dev, idx) in enumerate(peers):
                cp = pltpu.make_async_remote_copy(src_ref=src[k].at[idx] if scatter else src[k], dst_ref=land[k].at[idx],
                                                  send_sem=send.at[k * N_PEERS + p], recv_sem=recv.at[k * N_PEERS + p], device_id=dev,
                                                  device_id_type=MESH)
                cp.wait_send()
                cp.wait_recv()

    hbm = lambda a: pltpu.HBM(a.shape, a.dtype)
    outs = pl.pallas_call(
        body, name=name, out_shape=(*[hbm(s) for s in srcs], *[hbm(s) for s in lands]),
        in_specs=[HBM_SPEC] * (2 * n) + [SEM_SPEC, SEM_SPEC, SEM_SPEC, ANY_SPEC], out_specs=tuple([HBM_SPEC] * (2 * n)),
        input_output_aliases={j: j for j in range(2 * n)},
        compiler_params=pltpu.CompilerParams(has_side_effects=EFFECT),
    )(*srcs, *lands, send, recv, own, after)
    return list(outs[n:])


def _allsum_rows(name, part):
    R, C = part.shape

    def body(p_ref, o_ref, gath, send, recv):
        me, peers = _mesh_place()
        gath[me] = p_ref[...]
        sends = []
        for p, (dev, _) in enumerate(peers):
            cp = pltpu.make_async_remote_copy(src_ref=p_ref, dst_ref=gath.at[me], send_sem=send.at[p], recv_sem=recv.at[p],
                                              device_id=dev, device_id_type=MESH)
            cp.start()
            sends.append(cp)
        for p, (dev, idx) in enumerate(peers):
            pltpu.make_async_remote_copy(src_ref=p_ref, dst_ref=gath.at[idx], send_sem=send.at[p], recv_sem=recv.at[p],
                                         device_id=dev, device_id_type=MESH).wait_recv()
        for cp in sends:
            cp.wait_send()
        tot = gath[0]
        for j in range(1, N_DEV):
            tot = tot + gath[j]
        o_ref[...] = tot

    vm = pl.BlockSpec(memory_space=pltpu.VMEM)
    return pl.pallas_call(
        body, name=name, in_specs=[vm], out_specs=vm, out_shape=_sds((R, C), F32),
        scratch_shapes=[pltpu.VMEM((N_DEV, R, C), F32), pltpu.SemaphoreType.DMA((N_PEERS,)),
                        pltpu.SemaphoreType.DMA((N_PEERS,))],
        compiler_params=pltpu.CompilerParams(vmem_limit_bytes=VMEM_LIMIT),
    )(part)


def _rows(a, width=D_MODEL):
    flat = a.reshape(-1)
    return jnp.pad(flat, (0, (-flat.shape[0]) % width)).reshape(-1, width)


def _pack_rows(parts):
    blocks = []
    for p in parts:
        r = _rows(p)
        blocks.append(jnp.pad(r, ((0, (-r.shape[0]) % 8), (0, 0))))
    return jnp.concatenate(blocks, axis=0)


def _unpack_rows(rows, shapes):
    out, at = [], 0
    for s in shapes:
        size = int(np.prod(s))
        n = -(-size // D_MODEL)
        out.append(rows[at:at + n].reshape(-1)[:size].reshape(s))
        at += -(-n // 8) * 8
    return out


def kernel(x, norm_mix, norm_ffn, norm_final, attn_w_in, attn_w_out, attn_sinks, hgrn_w_in, hgrn_w_out, hgrn_norm, hgrn_lb_logits, ffn_w_up, ffn_conv_w, ffn_conv_b, ffn_w_down, loss_target, m_norm_mix, m_norm_ffn, m_norm_final, m_attn_w_in, m_attn_w_out, m_attn_sinks, m_hgrn_w_in, m_hgrn_w_out, m_hgrn_norm, m_hgrn_lb_logits, m_ffn_w_up, m_ffn_conv_w, m_ffn_conv_b, m_ffn_w_down, v_norm_mix, v_norm_ffn, v_norm_final, v_attn_w_in, v_attn_w_out, v_attn_sinks, v_hgrn_w_in, v_hgrn_w_out, v_hgrn_norm, v_hgrn_lb_logits, v_ffn_w_up, v_ffn_conv_w, v_ffn_conv_b, v_ffn_w_down):
    S = x.shape[1]
    n_attn, n_hgrn = attn_w_in.shape[0], hgrn_w_in.shape[0]
    me = 4 * lax.axis_index("x") + 2 * lax.axis_index("y") + lax.axis_index("c")

    wa_in_t, wa_out_b = attn_w_in.transpose(0, 2, 1).astype(BF16), attn_w_out.astype(BF16)
    wh_in_b, wh_out_b = hgrn_w_in.astype(BF16), hgrn_w_out.astype(BF16)
    wf_up_b, wf_down_b = ffn_w_up.transpose(0, 2, 1).astype(BF16), ffn_w_down.astype(BF16)
    conv_b = ffn_conv_b.reshape(DEPTH, 2, 4, 1, FF_SLOT)
    lb = _lb_fwd("lb_fwd", hgrn_lb_logits)

    def unit_shards(l, part):
        if part == "ffn":
            return [wf_up_b[l], wf_down_b[l], ffn_conv_w[l]]
        return [wa_in_t[l // 2], wa_out_b[l // 2]] if l % 2 == 0 else [wh_in_b[l // 2], wh_out_b[l // 2]]

    def unit_weights(l, part, w):
        if part == "ffn":
            return w[0][None], w[1].reshape(1, 4, FF_SLOT, D_MODEL), w[2].reshape(2, 4, 3, FF_SLOT)
        if l % 2 == 0:
            return w[0].reshape(1, ATTN_IN, D_MODEL), w[1].reshape(1, D_MODEL, D_MODEL)
        return w[0][None], w[1].reshape(1, D_MODEL, D_MODEL)

    units = [(l, part) for l in range(DEPTH) for part in ("mix", "ffn")]
    gathers = [_exchange_start("gather_start0", unit_shards(*units[0]), False, norm_final)]
    gathers.append(_exchange_start("gather_start1", unit_shards(*units[1]), False, gathers[0][4]))
    arrived = _exchange_wait("gather_wait0", gathers[0], False, gathers[1][4])
    weights, saved = {}, [dict() for _ in range(DEPTH)]
    h = x[0]
    hn = _rmsnorm_fwd("norm_mix_fwd0", h, norm_mix[0:1])
    for n, (l, part) in enumerate(units):
        i, sv = l // 2, saved[l]
        weights[l, part] = w = unit_weights(l, part, arrived)
        dep = None
        if n + 2 < len(units):
            gathers.append(_exchange_start(f"gather_start{n + 2}", unit_shards(*units[n + 2]), False, arrived[0]))
            dep = gathers[n + 2][4]
        if part == "mix":
            sv["h"], sv["hn"] = h, hn
            if l % 2 == 0:
                sv["proj"] = _proj_rows(f"attn_proj{i}", hn, w[0], 0, BF16, dep)
                sv["o"], *sv["kept"] = _attn_fwd(f"attn_fwd{i}", sv["proj"], attn_sinks[i:i + 1])
                h, hn = _out_proj(f"attn_out{i}", sv["o"], w[1], 0, h, norm_ffn[l:l + 1])
            else:
                sv["z"] = _proj_slots(f"hgrn_proj{i}", hn, w[0], 0, dep=dep).reshape(4, 2, S, HG_SLOT)
                sv["o"], *sv["kept"] = _hg_fwd(f"hgrn_fwd{i}", sv["z"], lb[i:i + 1], hgrn_norm[i:i + 1])
                h, hn = _out_proj(f"hgrn_out{i}", sv["o"], w[1], 0, h, norm_ffn[l:l + 1])
        else:
            sv["h2"], sv["hn2"] = h, hn
            sv["u"] = _proj_slots(f"ffn_up{l}", hn, w[0], 0, True, dep).reshape(2, 4, S, FF_SLOT)
            sv["a"] = _convgate_fwd(f"ffn_gate{l}", sv["u"], w[2], conv_b[l])
            if l + 1 < DEPTH:
                h, hn = _down_proj(f"ffn_down{l}", sv["a"], w[1], 0, h, norm_mix[l + 1:l + 2])
            else:
                h = _down_proj(f"ffn_down{l}", sv["a"], w[1], 0, h)
        if n + 1 < len(units):
            arrived = _exchange_wait(f"gather_wait{n + 1}", gathers[n + 1], False, h)
    dh, d_norm_final, loss_rows = _loss_head("loss_head", h, norm_final[None], loss_target[0])

    d_conv_w, d_conv_b, d_norm_mix, d_norm_ffn = [None] * DEPTH, [None] * DEPTH, [None] * DEPTH, [None] * DEPTH
    d_sinks, d_lb, d_hgrn_norm = [None] * n_attn, [None] * n_hgrn, [None] * n_hgrn
    received, started, before = {}, None, None
    for l, part in reversed(units):
        i, sv, w = l // 2, saved[l], weights[l, part]
        dep = None if started is None else started[4]
        if part == "ffn":
            da = _dgrad_down(f"ffn_down_dgrad{l}", dh, w[1], 0, dep)
            g_down = _wgrad_down(f"ffn_down_wgrad{l}", sv["a"], dh).reshape(N_DEV, D_FF // N_DEV, D_MODEL)
            du, d_conv_w[l], d_conv_b[l] = _convgate_bwd(f"ffn_gate_bwd{l}", sv["u"], w[2], conv_b[l], da)
            du = du.reshape(N_DEV, S, FF_SLOT)
            grads = [_wgrad_slots(f"ffn_up_wgrad{l}", sv["hn2"], du, True), g_down]
            dh, d_norm_ffn[l] = _dgrad_slots(f"ffn_up_dgrad{l}", du, w[0], 0, (sv["h2"], norm_ffn[l:l + 1], dh), True)
        else:
            if l % 2 == 0:
                do = _dgrad_out(f"attn_out_dgrad{i}", dh, w[1], 0, BF16, dep)
                g_out = _wgrad_rows(f"attn_out_wgrad{i}", sv["o"], dh)
                dproj, d_sinks[i] = _attn_bwd(f"attn_bwd{i}", sv["proj"], *sv["kept"], do)
                g_in = _wgrad_rows(f"attn_proj_wgrad{i}", dproj, sv["hn"]).reshape(N_DEV, ATTN_IN // N_DEV, D_MODEL)
                dh_new = _dgrad_rows(f"attn_proj_dgrad{i}", dproj, w[0], 0, (sv["h"], norm_mix[l:l + 1], dh))
            else:
                dog = _dgrad_out(f"hgrn_out_dgrad{i}", dh, w[1], 0, F32, dep)
                g_out = _wgrad_rows(f"hgrn_out_wgrad{i}", sv["o"], dh)
                dz, d_lb[i], dng = _hg_bwd(f"hgrn_bwd{i}", sv["z"], lb[i:i + 1], hgrn_norm[i:i + 1], *sv["kept"], dog)
                d_hgrn_norm[i] = dng[0] + dng[1]
                dz = dz.reshape(N_DEV, S, HG_SLOT)
                g_in = _wgrad_slots(f"hgrn_proj_wgrad{i}", sv["hn"], dz)
                dh_new = _dgrad_slots(f"hgrn_proj_dgrad{i}", dz, w[0], 0, (sv["h"], norm_mix[l:l + 1], dh))
            grads = [g_in, g_out.reshape(N_DEV, D_MODEL // N_DEV, D_MODEL)]
            dh, d_norm_mix[l] = dh_new
        gate = dh
        if started is not None:
            received[before] = _exchange_wait(f"scatter_wait_{before[1]}{before[0]}", started, True, dh)
            gate = received[before][0]
        started, before = _exchange_start(f"scatter_start_{part}{l}", grads, True, gate), (l, part)
    grad_x = dh[None]

    small_shapes = [(DEPTH, D_MODEL), (DEPTH, D_MODEL), (1, D_MODEL), (1, D_MODEL), (n_hgrn, D_MODEL), (n_attn, 128),
                    (n_hgrn, HG_K), (DEPTH, 2 * D_FF), (DEPTH, N_DEV, 3, FF_SLOT)]
    partial = _pack_rows([
        jnp.concatenate(d_norm_mix), jnp.concatenate(d_norm_ffn), d_norm_final, loss_rows, jnp.concatenate(d_lb),
        jnp.concatenate(d_sinks), jnp.concatenate(d_hgrn_norm), jnp.stack(d_conv_b), jnp.stack(d_conv_w)])
    total = _allsum_rows("allsum_small", partial + started[4][0, 0])
    (g_norm_mix, g_norm_ffn, g_norm_final, loss_sum, g_lb, g_sinks, g_hgrn_norm, g_conv_b, g_conv_w_all) = _unpack_rows(
        total, small_shapes)
    attn_layers, hgrn_layers = range(0, DEPTH, 2), range(1, DEPTH, 2)

    def transposed(ts):
        return [t.transpose(0, 2, 1) for t in ts]

    up_t = _adamw("adamw_ffn_up", [received[l, "ffn"][0] for l in range(DEPTH)], *transposed((ffn_w_up, m_ffn_w_up, v_ffn_w_up)),
                  dep=total[0:DEP_SHAPE[0], 0:DEP_SHAPE[1]])
    big = {
        "ffn_w_up": transposed(up_t),
        "hgrn_w_in": _adamw("adamw_hgrn_in", [received[l, "mix"][0] for l in hgrn_layers], hgrn_w_in, m_hgrn_w_in, v_hgrn_w_in),
        "hgrn_w_out": _adamw("adamw_hgrn_out", [received[l, "mix"][1] for l in hgrn_layers], hgrn_w_out, m_hgrn_w_out, v_hgrn_w_out),
        "ffn_w_down": _adamw("adamw_ffn_down", [received[l, "ffn"][1] for l in range(DEPTH)], ffn_w_down, m_ffn_w_down, v_ffn_w_down),
    }
    received[before] = _exchange_wait(f"scatter_wait_{before[1]}{before[0]}", started, True, up_t[3])

    loss = jnp.sum(loss_sum)
    g_norm_final = g_norm_final[0]
    g_sinks = g_sinks[:, :N_Q_HEADS]
    g_lb_logits = _lb_bwd("lb_bwd", hgrn_lb_logits, g_lb)
    g_conv_w = lax.dynamic_index_in_dim(g_conv_w_all, me, axis=1, keepdims=False)

    big.update({
        "attn_w_in": transposed(_adamw("adamw_attn_in", [received[l, "mix"][0] for l in attn_layers],
                                       *transposed((attn_w_in, m_attn_w_in, v_attn_w_in)))),
        "attn_w_out": _adamw("adamw_attn_out", [received[l, "mix"][1] for l in attn_layers], attn_w_out, m_attn_w_out, v_attn_w_out),
        "ffn_conv_w": _adamw("adamw_conv_w", [g_conv_w[l][None] for l in range(DEPTH)], ffn_conv_w, m_ffn_conv_w, v_ffn_conv_w),
    })
    small_w = [norm_mix, norm_ffn, norm_final, attn_sinks, hgrn_norm, hgrn_lb_logits, ffn_conv_b]
    small_m = [m_norm_mix, m_norm_ffn, m_norm_final, m_attn_sinks, m_hgrn_norm, m_hgrn_lb_logits, m_ffn_conv_b]
    small_v = [v_norm_mix, v_norm_ffn, v_norm_final, v_attn_sinks, v_hgrn_norm, v_hgrn_lb_logits, v_ffn_conv_b]
    small_g = [g_norm_mix, g_norm_ffn, g_norm_final, g_sinks, g_hgrn_norm, g_lb_logits, g_conv_b]
    outs = _adamw("adamw_small", [_pack_rows(small_g)[None]], *[_pack_rows(t)[None] for t in (small_w, small_m, small_v)])
    outs = [o[0] for o in outs]
    shapes = [w.shape for w in small_w]
    small = {n: [t[j] for t in [_unpack_rows(o, shapes) for o in outs]]
             for j, n in enumerate(["norm_mix", "norm_ffn", "norm_final", "attn_sinks", "hgrn_norm", "hgrn_lb_logits", "ffn_conv_b"])}
    order = ["norm_mix", "norm_ffn", "norm_final", "attn_w_in", "attn_w_out", "attn_sinks", "hgrn_w_in", "hgrn_w_out",
             "hgrn_norm", "hgrn_lb_logits", "ffn_w_up", "ffn_conv_w", "ffn_conv_b", "ffn_w_down"]
    res = {**big, **small}
    return (loss, grad_x, *[res[n][0] for n in order], *[res[n][1] for n in order], *[res[n][2] for n in order],
            *[res[n][3] for n in order])
```

```python
import numpy as np
import jax
import jax.numpy as jnp
from jax import lax
from jax.experimental import pallas as pl
from jax.experimental.pallas import tpu as pltpu

F32 = jnp.float32
BF16 = jnp.bfloat16

D_MODEL = 1024
DEPTH = 4
HEAD_DIM = 64
N_Q_HEADS = 16
N_KV_HEADS = 4
Q_PER_KV = 4
ATTN_BLOCK = 128
ATTN_IN = 1536
HG_HEADS = 8
HG_K = 128
HG_CHUNK = 64
HG_IN = 4096
D_FF = 2816
EPS = 1e-6
N_DEV = 8
FF_SLOT = 2 * D_FF // N_DEV
HG_SLOT = HG_IN // N_DEV
HG_LEVELS = 6

ADAM_LR = 0.001
ADAM_B1 = 0.9
ADAM_B2 = 0.999
ADAM_EPS = 1e-08
ADAM_WD = 0.01
ADAM_STEP = 10

VMEM_LIMIT = 56 * 1024 * 1024
ROW_TILE = 1024
WIDE_ROW_TILE = 2048
NEG_BIG = -1e30

NN = (((1,), (0,)), ((), ()))
NT = (((1,), (1,)), ((), ()))
TN = (((0,), (0,)), ((), ()))


def _bdot(a, b, dn):
    return lax.dot_general(a.astype(BF16), b.astype(BF16), dn, preferred_element_type=F32)


def _sds(shape, dtype):
    return jax.ShapeDtypeStruct(tuple(shape), dtype)


def _params(sem):
    return pltpu.CompilerParams(dimension_semantics=sem, vmem_limit_bytes=VMEM_LIMIT)


DEP_SHAPE = (8, 128)


def _dep_spec(rank):
    return pl.BlockSpec(DEP_SHAPE, lambda *_: (0, 0))


def _matmul(name, a, b, *, dn, grid, a_spec, b_spec, o_spec, out_shape, acc_shape=None, extra=(), extra_specs=(),
            finish=None, dep=None, sem=("parallel", "parallel", "arbitrary")):
    nk = grid[2]
    many = isinstance(out_shape, (list, tuple))
    n_in = 2 + len(extra) + (dep is not None)
    n_out = len(out_shape) if many else 1

    def body(*refs):
        a_ref, b_ref = refs[0], refs[1]
        outs = refs[n_in:n_in + n_out]

        def prod():
            return _bdot(a_ref[...], b_ref[...], dn)

        def done(v):
            if finish is None:
                outs[0][...] = v.astype(outs[0].dtype)
            else:
                finish(v, refs[2:2 + len(extra)], outs)

        if nk == 1:
            done(prod())
        else:
            acc = refs[-1]
            k = pl.program_id(2)

            @pl.when(k == 0)
            def _():
                acc[...] = prod()

            @pl.when(k > 0)
            def _():
                acc[...] += prod()

            @pl.when(k == nk - 1)
            def _():
                done(acc[...])

    in_specs = [a_spec, b_spec, *extra_specs] + ([_dep_spec(3)] if dep is not None else [])
    args = (a, b, *extra) + ((dep,) if dep is not None else ())
    scratch = [] if nk == 1 else [pltpu.VMEM(acc_shape, F32)]
    return pl.pallas_call(
        body, name=name, grid=grid, in_specs=in_specs, out_specs=o_spec, out_shape=out_shape,
        scratch_shapes=scratch, compiler_params=_params(sem),
    )(*args)


def _rms(x):
    return lax.rsqrt(jnp.mean(x * x, axis=-1, keepdims=True) + EPS)


def _residual_finish(v, ex, outs):
    h = v + ex[0][...]
    outs[0][...] = h
    if len(ex) > 1:
        outs[1][...] = (h * _rms(h) * ex[1][...]).astype(outs[1].dtype)


def _norm_bwd_finish(v, ex, outs):
    x = ex[0][...]
    r = _rms(x)
    xh = x * r
    dyg = v * ex[1][...]
    outs[0][...] = ex[2][...] + r * (dyg - xh * jnp.mean(dyg * xh, axis=-1, keepdims=True))
    part = jnp.sum(v * xh, axis=0, keepdims=True)

    @pl.when(pl.program_id(0) == 0)
    def _():
        outs[1][...] = part

    @pl.when(pl.program_id(0) > 0)
    def _():
        outs[1][...] += part


def _row_io(tm, norm_g):
    row = pl.BlockSpec((tm, D_MODEL), lambda i, j, k: (i, 0))
    vec = pl.BlockSpec((1, D_MODEL), lambda i, j, k: (0, 0))
    if norm_g is None:
        return (row,), row, lambda S: _sds((S, D_MODEL), F32)
    return (row, vec), [row, row], lambda S: [_sds((S, D_MODEL), F32), _sds((S, D_MODEL), BF16)]


def _tile(n, t):
    return min(n, t)


def _proj_rows(name, hn, wt, l, out_dtype, dep=None):
    S, N = hn.shape[0], wt.shape[1]
    tm, tn = _tile(S, ROW_TILE), 512
    return _matmul(
        name, hn, wt, dn=NT, grid=(S // tm, N // tn, 1),
        a_spec=pl.BlockSpec((tm, D_MODEL), lambda i, j, k: (i, 0)),
        b_spec=pl.BlockSpec((None, tn, D_MODEL), lambda i, j, k: (l, j, 0)),
        o_spec=pl.BlockSpec((tm, tn), lambda i, j, k: (i, j)),
        out_shape=_sds((S, N), out_dtype), dep=dep)


def _slot_weight(w, transposed):
    if transposed:
        return w.shape[2], (None, None, w.shape[2], D_MODEL), NT, NN
    return w.shape[3], (None, None, D_MODEL, w.shape[3]), NN, NT


def _proj_slots(name, hn, w, l, transposed=False, dep=None):
    S = hn.shape[0]
    r, blk, dn, _ = _slot_weight(w, transposed)
    tm = _tile(S, WIDE_ROW_TILE)
    return _matmul(
        name, hn, w, dn=dn, grid=(N_DEV, S // tm, 1),
        a_spec=pl.BlockSpec((tm, D_MODEL), lambda j, i, k: (i, 0)),
        b_spec=pl.BlockSpec(blk, lambda j, i, k: (l, j, 0, 0)),
        o_spec=pl.BlockSpec((None, tm, r), lambda j, i, k: (j, i, 0)),
        out_shape=_sds((N_DEV, S, r), F32), dep=dep)


def _out_proj(name, o, w, l, h, norm_g=None):
    S, K = o.shape
    tm = _tile(S, ROW_TILE)
    extra_specs, o_spec, out_shape = _row_io(tm, norm_g)
    return _matmul(
        name, o, w, dn=NN, grid=(S // tm, 1, 1),
        a_spec=pl.BlockSpec((tm, K), lambda i, j, k: (i, 0)),
        b_spec=pl.BlockSpec((None, K, D_MODEL), lambda i, j, k: (l, 0, 0)),
        o_spec=o_spec, out_shape=out_shape(S), extra=(h,) if norm_g is None else (h, norm_g),
        extra_specs=extra_specs, finish=_residual_finish)


def _down_proj(name, a, w, l, h, norm_g=None):
    nj, S, r = a.shape
    tm = _tile(S, ROW_TILE)
    extra_specs, o_spec, out_shape = _row_io(tm, norm_g)
    return _matmul(
        name, a, w, dn=NN, grid=(S // tm, 1, nj),
        a_spec=pl.BlockSpec((None, tm, r), lambda i, j, k: (k, i, 0)),
        b_spec=pl.BlockSpec((None, None, r, D_MODEL), lambda i, j, k: (l, k, 0, 0)),
        o_spec=o_spec, out_shape=out_shape(S), acc_shape=(tm, D_MODEL),
        extra=(h,) if norm_g is None else (h, norm_g), extra_specs=extra_specs, finish=_residual_finish)


def _dgrad_down(name, dh, w, l, dep=None):
    S = dh.shape[0]
    nj, r = w.shape[1], w.shape[2]
    tm = _tile(S, ROW_TILE)
    return _matmul(
        name, dh, w, dn=NT, grid=(nj, S // tm, 1),
        a_spec=pl.BlockSpec((tm, D_MODEL), lambda j, i, k: (i, 0)),
        b_spec=pl.BlockSpec((None, None, r, D_MODEL), lambda j, i, k: (l, j, 0, 0)),
        o_spec=pl.BlockSpec((None, tm, r), lambda j, i, k: (j, i, 0)),
        out_shape=_sds((nj, S, r), BF16), dep=dep)


def _wgrad_down(name, a, dh):
    nj, S, r = a.shape
    tk = _tile(S, ROW_TILE)
    return _matmul(
        name, a, dh, dn=TN, grid=(nj, 1, S // tk),
        a_spec=pl.BlockSpec((None, tk, r), lambda s, j, k: (s, k, 0)),
        b_spec=pl.BlockSpec((tk, D_MODEL), lambda s, j, k: (k, 0)),
        o_spec=pl.BlockSpec((None, r, D_MODEL), lambda s, j, k: (s, 0, 0)),
        out_shape=_sds((nj, r, D_MODEL), BF16), acc_shape=(r, D_MODEL))


def _norm_bwd_io(tm, S):
    row = pl.BlockSpec((tm, D_MODEL), lambda i, j, k: (i, 0))
    vec = pl.BlockSpec((1, D_MODEL), lambda i, j, k: (0, 0))
    return dict(extra_specs=(row, vec, row), o_spec=[row, vec], out_shape=[_sds((S, D_MODEL), F32), _sds((1, D_MODEL), F32)],
                finish=_norm_bwd_finish, sem=("arbitrary", "arbitrary", "arbitrary"))


def _dgrad_slots(name, dz, w, l, norm, transposed=False):
    nj, S, r = dz.shape
    _, blk, _, dn = _slot_weight(w, transposed)
    tm = _tile(S, ROW_TILE)
    return _matmul(
        name, dz, w, dn=dn, grid=(S // tm, 1, nj),
        a_spec=pl.BlockSpec((None, tm, r), lambda i, j, k: (k, i, 0)),
        b_spec=pl.BlockSpec(blk, lambda i, j, k: (l, k, 0, 0)),
        acc_shape=(tm, D_MODEL), extra=norm, **_norm_bwd_io(tm, S))


def _wgrad_slots(name, hn, dz, transposed=False):
    nj, S, r = dz.shape
    tk = _tile(S, ROW_TILE)
    hn_spec = pl.BlockSpec((tk, D_MODEL), lambda s, j, k: (k, 0))
    dz_spec = pl.BlockSpec((None, tk, r), lambda s, j, k: (s, k, 0))
    if transposed:
        return _matmul(
            name, dz, hn, dn=TN, grid=(nj, 1, S // tk), a_spec=dz_spec, b_spec=hn_spec,
            o_spec=pl.BlockSpec((None, r, D_MODEL), lambda s, j, k: (s, 0, 0)),
            out_shape=_sds((nj, r, D_MODEL), BF16), acc_shape=(r, D_MODEL))
    return _matmul(
        name, hn, dz, dn=TN, grid=(nj, 1, S // tk), a_spec=hn_spec, b_spec=dz_spec,
        o_spec=pl.BlockSpec((None, D_MODEL, r), lambda s, j, k: (s, 0, 0)),
        out_shape=_sds((nj, D_MODEL, r), BF16), acc_shape=(D_MODEL, r))


def _dgrad_out(name, dh, w, l, out_dtype, dep=None):
    S, K = dh.shape[0], w.shape[1]
    tm = _tile(S, ROW_TILE)
    return _matmul(
        name, dh, w, dn=NT, grid=(S // tm, 1, 1),
        a_spec=pl.BlockSpec((tm, D_MODEL), lambda i, j, k: (i, 0)),
        b_spec=pl.BlockSpec((None, K, D_MODEL), lambda i, j, k: (l, 0, 0)),
        o_spec=pl.BlockSpec((tm, K), lambda i, j, k: (i, 0)),
        out_shape=_sds((S, K), out_dtype), dep=dep)


def _wgrad_rows(name, a, b):
    S, K = a.shape
    tk = _tile(S, ROW_TILE)
    return _matmul(
        name, a, b, dn=TN, grid=(1, 1, S // tk),
        a_spec=pl.BlockSpec((tk, K), lambda i, j, k: (k, 0)),
        b_spec=pl.BlockSpec((tk, D_MODEL), lambda i, j, k: (k, 0)),
        o_spec=pl.BlockSpec((K, D_MODEL), lambda i, j, k: (0, 0)),
        out_shape=_sds((K, D_MODEL), BF16), acc_shape=(K, D_MODEL))


def _dgrad_rows(name, dz, wt, l, norm):
    S, N = dz.shape
    tm = _tile(S, ROW_TILE)
    return _matmul(
        name, dz, wt, dn=NN, grid=(S // tm, 1, 1),
        a_spec=pl.BlockSpec((tm, N), lambda i, j, k: (i, 0)),
        b_spec=pl.BlockSpec((None, N, D_MODEL), lambda i, j, k: (l, 0, 0)),
        extra=norm, **_norm_bwd_io(tm, S))


def _rmsnorm_fwd(name, h, g):
    S = h.shape[0]
    tm = _tile(S, ROW_TILE)

    def body(h_ref, g_ref, o_ref):
        x = h_ref[...]
        o_ref[...] = (x * _rms(x) * g_ref[...]).astype(o_ref.dtype)

    row = pl.BlockSpec((tm, D_MODEL), lambda i: (i, 0))
    return pl.pallas_call(
        body, name=name, grid=(S // tm,), in_specs=[row, pl.BlockSpec((1, D_MODEL), lambda i: (0, 0))],
        out_specs=row, out_shape=_sds((S, D_MODEL), BF16), compiler_params=_params(("parallel",)),
    )(h, g)


def _loss_head(name, h, g, target):
    S = h.shape[0]
    tm = _tile(S, ROW_TILE)

    def body(h_ref, g_ref, t_ref, dh_ref, dg_ref, ls_ref):
        x = h_ref[...]
        r = lax.rsqrt(jnp.mean(x * x, axis=-1, keepdims=True) + EPS)
        xh = x * r
        diff = xh * g_ref[...] - t_ref[...]
        dyf = diff * (1.0 / D_MODEL)
        dyg = dyf * g_ref[...]
        dh_ref[...] = r * (dyg - xh * jnp.mean(dyg * xh, axis=-1, keepdims=True))
        part = jnp.sum(dyf * xh, axis=0, keepdims=True)
        lpart = jnp.sum(diff * diff, axis=0, keepdims=True) * (0.5 / D_MODEL)

        @pl.when(pl.program_id(0) == 0)
        def _():
            dg_ref[...] = part
            ls_ref[...] = lpart

        @pl.when(pl.program_id(0) > 0)
        def _():
            dg_ref[...] += part
            ls_ref[...] += lpart

    row = pl.BlockSpec((tm, D_MODEL), lambda i: (i, 0))
    vec = pl.BlockSpec((1, D_MODEL), lambda i: (0, 0))
    return pl.pallas_call(
        body, name=name, grid=(S // tm,), in_specs=[row, vec, row], out_specs=[row, vec, vec],
        out_shape=[_sds((S, D_MODEL), F32), _sds((1, D_MODEL), F32), _sds((1, D_MODEL), F32)],
        compiler_params=_params(("arbitrary",)),
    )(h, g, target)


ATTN_SCALE = HEAD_DIM ** -0.5
ALIBI_SLOPES = [2.0 ** (-8.0 * (h + 1) / N_Q_HEADS) for h in range(N_Q_HEADS)]
K_COL = N_Q_HEADS * HEAD_DIM
KV_COLS = N_KV_HEADS * HEAD_DIM
V_COL = K_COL + KV_COLS


def _attn_masks(n):
    qi = lax.broadcasted_iota(jnp.int32, (ATTN_BLOCK, ATTN_BLOCK), 0)
    ki = lax.broadcasted_iota(jnp.int32, (ATTN_BLOCK, ATTN_BLOCK), 1)
    dist_c = (qi - ki).astype(F32)
    return dist_c + float(ATTN_BLOCK), dist_c, (ki > qi) & (n > 0), qi >= ki


def _attn_probs(raw_p, raw_c, sink, slope, masks):
    dist_p, dist_c, valid_p, valid_c = masks
    sp = jnp.where(valid_p, raw_p * ATTN_SCALE - slope * dist_p, NEG_BIG)
    sc = jnp.where(valid_c, raw_c * ATTN_SCALE - slope * dist_c, NEG_BIG)
    m = jnp.maximum(jnp.maximum(jnp.max(sp, axis=-1, keepdims=True), jnp.max(sc, axis=-1, keepdims=True)), sink)
    ep, ec, es = jnp.exp(sp - m), jnp.exp(sc - m), jnp.exp(sink - m)
    inv = 1.0 / (jnp.sum(ep, axis=-1, keepdims=True) + jnp.sum(ec, axis=-1, keepdims=True) + es)
    return ep * inv, ec * inv, es * inv


def _group_rows(ref, m):
    return jnp.concatenate([ref[:, HEAD_DIM * (Q_PER_KV * m + g):HEAD_DIM * (Q_PER_KV * m + g + 1)]
                            for g in range(Q_PER_KV)], axis=0)


def _head_rows(x, g):
    return x[ATTN_BLOCK * g:ATTN_BLOCK * (g + 1)]


def _attn_specs(nblk):
    last = nblk - 1
    kcol, vcol = K_COL // KV_COLS, V_COL // KV_COLS
    return [
        pl.BlockSpec((ATTN_BLOCK, K_COL), lambda n: (jnp.minimum(n, last), 0)),
        pl.BlockSpec((ATTN_BLOCK, KV_COLS), lambda n: (jnp.minimum(n, last), kcol)),
        pl.BlockSpec((ATTN_BLOCK, KV_COLS), lambda n: (jnp.maximum(jnp.minimum(n, last) - 1, 0), kcol)),
        pl.BlockSpec((ATTN_BLOCK, KV_COLS), lambda n: (jnp.minimum(n, last), vcol)),
        pl.BlockSpec((ATTN_BLOCK, KV_COLS), lambda n: (jnp.maximum(jnp.minimum(n, last) - 1, 0), vcol)),
    ]


P_COLS = 2 * ATTN_BLOCK


def _attn_fwd(name, proj, sinks):
    S = proj.shape[0]
    nblk = S // ATTN_BLOCK

    def body(q_ref, kc_ref, kp_ref, vc_ref, vp_ref, sk_ref, o_ref, p_ref, ps_ref):
        masks = _attn_masks(pl.program_id(0))
        lane = lax.broadcasted_iota(jnp.int32, (ATTN_BLOCK, 128), 1)
        sink_p = jnp.zeros((ATTN_BLOCK, 128), F32)
        for m in range(N_KV_HEADS):
            ks = slice(HEAD_DIM * m, HEAD_DIM * (m + 1))
            kp, kc, vp, vc = kp_ref[:, ks], kc_ref[:, ks], vp_ref[:, ks], vc_ref[:, ks]
            q4 = _group_rows(q_ref, m)
            raw_p, raw_c = _bdot(q4, kp, NT), _bdot(q4, kc, NT)
            pps, pcs = [], []
            for g in range(Q_PER_KV):
                hh = Q_PER_KV * m + g
                pp, pc, ps = _attn_probs(_head_rows(raw_p, g), _head_rows(raw_c, g), sk_ref[0, hh], ALIBI_SLOPES[hh], masks)
                pps.append(pp.astype(BF16))
                pcs.append(pc.astype(BF16))
                p_ref[:, P_COLS * hh:P_COLS * hh + ATTN_BLOCK] = pps[g]
                p_ref[:, P_COLS * hh + ATTN_BLOCK:P_COLS * (hh + 1)] = pcs[g]
                sink_p = jnp.where(lane == hh, ps, sink_p)
            o4 = _bdot(jnp.concatenate(pps, axis=0), vp, NN) + _bdot(jnp.concatenate(pcs, axis=0), vc, NN)
            for g in range(Q_PER_KV):
                hh = Q_PER_KV * m + g
                o_ref[:, HEAD_DIM * hh:HEAD_DIM * (hh + 1)] = _head_rows(o4, g).astype(o_ref.dtype)
        ps_ref[...] = sink_p

    row = lambda cols: pl.BlockSpec((ATTN_BLOCK, cols), lambda n: (n, 0))
    return pl.pallas_call(
        body, name=name, grid=(nblk,),
        in_specs=_attn_specs(nblk) + [pl.BlockSpec(memory_space=pltpu.SMEM)],
        out_specs=[row(K_COL), row(N_Q_HEADS * P_COLS), row(128)],
        out_shape=[_sds((S, K_COL), BF16), _sds((S, N_Q_HEADS * P_COLS), BF16), _sds((S, 128), F32)],
        compiler_params=_params(("parallel",)),
    )(proj, proj, proj, proj, proj, sinks)


def _attn_bwd(name, proj, probs, sink_probs, do):
    S = proj.shape[0]
    nblk = S // ATTN_BLOCK

    def body(q_ref, kc_ref, kp_ref, vc_ref, vp_ref, do_ref, p_ref, ps_ref, dz_ref, ds_ref, carry, cur, padd):
        n = pl.program_id(0)

        @pl.when(n == 0)
        def _():
            carry[...] = jnp.zeros_like(carry)
            ds_ref[...] = jnp.zeros_like(ds_ref)

        @pl.when(n < nblk)
        def _():
            lane = lax.broadcasted_iota(jnp.int32, (ATTN_BLOCK, 128), 1)
            sink_p = ps_ref[...]
            dsv = jnp.zeros((1, 128), F32)
            for m in range(N_KV_HEADS):
                ks = slice(HEAD_DIM * m, HEAD_DIM * (m + 1))
                kp, kc, vp, vc = kp_ref[:, ks], kc_ref[:, ks], vp_ref[:, ks], vc_ref[:, ks]
                q4, do4 = _group_rows(q_ref, m), _group_rows(do_ref, m)
                dpp4, dpc4 = _bdot(do4, vp, NT), _bdot(do4, vc, NT)
                pps, pcs, dsps, dscs = [], [], [], []
                for g in range(Q_PER_KV):
                    hh = Q_PER_KV * m + g
                    pps.append(p_ref[:, P_COLS * hh:P_COLS * hh + ATTN_BLOCK])
                    pcs.append(p_ref[:, P_COLS * hh + ATTN_BLOCK:P_COLS * (hh + 1)])
                    pp, pc = pps[g].astype(F32), pcs[g].astype(F32)
                    dpp, dpc = _head_rows(dpp4, g), _head_rows(dpc4, g)
                    delta = jnp.sum(pp * dpp, axis=-1, keepdims=True) + jnp.sum(pc * dpc, axis=-1, keepdims=True)
                    dsv = dsv - jnp.sum(jnp.where(lane == hh, sink_p, 0.0) * delta, axis=0, keepdims=True)
                    dsps.append((pp * (dpp - delta)).astype(BF16))
                    dscs.append((pc * (dpc - delta)).astype(BF16))
                pp4, pc4 = jnp.concatenate(pps, axis=0), jnp.concatenate(pcs, axis=0)
                dsp4, dsc4 = jnp.concatenate(dsps, axis=0), jnp.concatenate(dscs, axis=0)
                dq4 = (_bdot(dsp4, kp, NN) + _bdot(dsc4, kc, NN)) * ATTN_SCALE
                for g in range(Q_PER_KV):
                    hh = Q_PER_KV * m + g
                    cur[:, HEAD_DIM * hh:HEAD_DIM * (hh + 1)] = _head_rows(dq4, g)
                cur[:, K_COL + HEAD_DIM * m:K_COL + HEAD_DIM * (m + 1)] = _bdot(dsc4, q4, TN) * ATTN_SCALE
                cur[:, V_COL + HEAD_DIM * m:V_COL + HEAD_DIM * (m + 1)] = _bdot(pc4, do4, TN)
                padd[:, ks] = _bdot(dsp4, q4, TN) * ATTN_SCALE
                padd[:, KV_COLS + HEAD_DIM * m:KV_COLS + HEAD_DIM * (m + 1)] = _bdot(pp4, do4, TN)
            ds_ref[...] += dsv
            dz_ref[:, :K_COL] = carry[:, :K_COL].astype(dz_ref.dtype)
            dz_ref[:, K_COL:] = (carry[:, K_COL:] + padd[...]).astype(dz_ref.dtype)
            carry[...] = cur[...]

        @pl.when(n == nblk)
        def _():
            dz_ref[...] = carry[...].astype(dz_ref.dtype)

    return pl.pallas_call(
        body, name=name, grid=(nblk + 1,),
        in_specs=_attn_specs(nblk) + [
            pl.BlockSpec((ATTN_BLOCK, cols), lambda n: (jnp.minimum(n, nblk - 1), 0))
            for cols in (K_COL, N_Q_HEADS * P_COLS, 128)],
        out_specs=[pl.BlockSpec((ATTN_BLOCK, ATTN_IN), lambda n: (jnp.maximum(n - 1, 0), 0)),
                   pl.BlockSpec((1, 128), lambda n: (0, 0))],
        out_shape=[_sds((S, ATTN_IN), BF16), _sds((1, 128), F32)],
        scratch_shapes=[pltpu.VMEM((ATTN_BLOCK, ATTN_IN), F32), pltpu.VMEM((ATTN_BLOCK, ATTN_IN), F32),
                        pltpu.VMEM((ATTN_BLOCK, 2 * KV_COLS), F32)],
        compiler_params=_params(("arbitrary",)),
    )(proj, proj, proj, proj, proj, do, probs, sink_probs)


def _hg_consts():
    C = HG_CHUNK
    tri = np.tril(np.ones((C, C)))
    t = np.arange(C)
    rows, masks = [tri], []
    for lvl in range(HG_LEVELS):
        n = C >> (lvl + 1)
        sel = np.zeros((C, C))
        sel[t, (t // (2 * n)) * (2 * n) + n - 1] = 1.0
        rows.append(sel @ tri)
        tt, ss = t[:, None], t[None, :]
        masks.append((tt // (2 * n) == ss // (2 * n)) & ((tt // n) % 2 == 1) & ((ss // n) % 2 == 0))
    masks.append(np.eye(C, dtype=bool))
    stk = np.concatenate(rows, axis=0)
    return jnp.asarray(stk, BF16), jnp.asarray(np.stack(masks), F32)


def _sigmoid(x):
    return 1.0 / (1.0 + jnp.exp(-x))


def _split(x, parts):
    out, rest = [], x
    for _ in range(parts):
        out.append(rest.astype(BF16))
        rest = rest - out[-1].astype(F32)
    return out


def _dot01(m01, x, dn, parts=3):
    return sum(lax.dot_general(m01, p, dn, preferred_element_type=F32) for p in _split(x, parts))


def _ref_rows(b, n):
    C = b.shape[1]
    if 2 * n >= 8:
        b3 = b.reshape(HG_CHUNK // (2 * n), 2 * n, C)
        return jnp.broadcast_to(b3[:, n - 1:n, :], b3.shape).reshape(HG_CHUNK, C)
    pos = lax.broadcasted_iota(jnp.int32, b.shape, 0) % (2 * n)
    out = b
    for p in range(2 * n):
        if p != n - 1:
            out = jnp.where(pos == p, pltpu.roll(b, (p - (n - 1)) % HG_CHUNK, 0), out)
    return out


def _hg_common(z_ref, lb_ref, stk_ref):
    qr, fr = z_ref[0], z_ref[1]
    lb = lb_ref[...]
    sq, sg, sgn = _sigmoid(qr), _sigmoid(fr), _sigmoid(-fr)
    ft = lb + (1.0 - lb) * sg
    b = _dot01(stk_ref[0:HG_CHUNK, :], jnp.log(ft), NN)
    ws = [jnp.exp(-jnp.abs(b - _ref_rows(b, HG_CHUNK >> (l + 1)))) for l in range(HG_LEVELS)]
    blast = b[HG_CHUNK - 1:HG_CHUNK]
    return dict(qr=qr, fr=fr, lb=lb, sq=sq, sg=sg, sgn=sgn, ft=ft, q=qr * sq, kk=(1.0 - lb) * sgn, b=b,
                ws=ws, eb=jnp.exp(b), ed=jnp.exp(blast - b), elast=jnp.exp(blast))


def _hg_factors(qh, kh, ws, sl):
    return ([(qh * ws[l][:, sl]).astype(BF16) for l in range(HG_LEVELS)],
            [(kh * ws[l][:, sl]).astype(BF16) for l in range(HG_LEVELS)])


def _hg_intra(qh, kh, ws, msk_ref, sl):
    qls, kls = _hg_factors(qh, kh, ws, sl)
    a = msk_ref[HG_LEVELS] * _bdot(qh, kh, NT)
    for l in range(HG_LEVELS):
        a = a + msk_ref[l] * _bdot(qls[l], kls[l], NT)
    return a


def _hg_fwd(name, z, lb, ng):
    S = z.shape[2]
    nc = S // HG_CHUNK
    stk, msk = _hg_consts()

    def body(z_ref, lb_ref, ng_ref, stk_ref, msk_ref, og_ref, st_ref, a_ref, o_ref, state):
        @pl.when(pl.program_id(1) == 0)
        def _():
            state[...] = jnp.zeros_like(state)

        cm = _hg_common(z_ref, lb_ref, stk_ref)
        v, gt = z_ref[2], z_ref[3]
        kd = cm["kk"] * cm["ed"]
        for hh in range(4):
            sl = slice(HG_K * hh, HG_K * (hh + 1))
            st = state[hh]
            st_ref[hh] = st
            qh, kh, vh = cm["q"][:, sl], cm["kk"][:, sl], v[:, sl]
            a = _hg_intra(qh, kh, cm["ws"], msk_ref, sl).astype(BF16)
            a_ref[hh] = a
            o = _bdot(a, vh, NN) + _bdot(qh * cm["eb"][:, sl], st, NT)
            o_ref[:, sl] = o
            state[hh] = cm["elast"][:, sl] * st + _bdot(vh, kd[:, sl], TN)
            r = lax.rsqrt(jnp.mean(o * o, axis=-1, keepdims=True) + EPS)
            gh = gt[:, sl]
            og_ref[:, sl] = (o * r * ng_ref[...] * (gh * _sigmoid(gh))).astype(og_ref.dtype)

    return pl.pallas_call(
        body, name=name, grid=(2, nc),
        in_specs=[pl.BlockSpec((4, None, HG_CHUNK, HG_SLOT), lambda g, c: (0, g, c, 0)),
                  pl.BlockSpec((1, HG_SLOT), lambda g, c: (0, g)),
                  pl.BlockSpec((1, HG_K), lambda g, c: (0, 0)),
                  pl.BlockSpec(stk.shape, lambda g, c: (0, 0)),
                  pl.BlockSpec(msk.shape, lambda g, c: (0, 0, 0))],
        out_specs=[pl.BlockSpec((HG_CHUNK, HG_SLOT), lambda g, c: (c, g)),
                   pl.BlockSpec((None, 4, HG_K, HG_K), lambda g, c: (c, g, 0, 0)),
                   pl.BlockSpec((None, 4, HG_CHUNK, HG_CHUNK), lambda g, c: (c, g, 0, 0)),
                   pl.BlockSpec((HG_CHUNK, HG_SLOT), lambda g, c: (c, g))],
        out_shape=[_sds((S, D_MODEL), BF16), _sds((nc, HG_HEADS, HG_K, HG_K), F32),
                   _sds((nc, HG_HEADS, HG_CHUNK, HG_CHUNK), BF16), _sds((S, D_MODEL), F32)],
        scratch_shapes=[pltpu.VMEM((4, HG_K, HG_K), F32)],
        compiler_params=_params(("parallel", "arbitrary")),
    )(z, lb, ng, stk, msk)


def _hg_bwd(name, z, lb, ng, states, intra, o_pre, dog):
    S = z.shape[2]
    nc = S // HG_CHUNK
    stk, msk = _hg_consts()

    def body(z_ref, lb_ref, ng_ref, stk_ref, msk_ref, st_ref, a_ref, o_ref, dog_ref, dz_ref, dlb_ref, dng_ref, dstate):
        @pl.when(pl.program_id(1) == 0)
        def _():
            dstate[...] = jnp.zeros_like(dstate)
            dlb_ref[...] = jnp.zeros_like(dlb_ref)
            dng_ref[...] = jnp.zeros_like(dng_ref)

        cm = _hg_common(z_ref, lb_ref, stk_ref)
        v, gt = z_ref[2], z_ref[3]
        ng = ng_ref[...]
        kd = cm["kk"] * cm["ed"]
        row = lax.broadcasted_iota(jnp.int32, (HG_CHUNK, 1), 0)
        dng = jnp.zeros((1, HG_K), F32)
        dq_h, dkk_h, db_h, dv_h, dgt_h = [], [], [], [], []
        dr_h = [[] for _ in range(HG_LEVELS)]
        for hh in range(4):
            sl = slice(HG_K * hh, HG_K * (hh + 1))
            st, dst = st_ref[hh], dstate[hh]
            qh, kh, vh, ebh, edh, kdh = cm["q"][:, sl], cm["kk"][:, sl], v[:, sl], cm["eb"][:, sl], cm["ed"][:, sl], kd[:, sl]
            elh = cm["elast"][:, sl]
            qls, kls = _hg_factors(qh, kh, cm["ws"], sl)
            a, o = a_ref[hh], o_ref[:, sl]
            qe = qh * ebh
            r = lax.rsqrt(jnp.mean(o * o, axis=-1, keepdims=True) + EPS)
            xh = o * r
            gh = gt[:, sl]
            sgg = _sigmoid(gh)
            dog = dog_ref[:, sl].astype(F32)
            dy = dog * (gh * sgg)
            dgt_h.append(dog * (xh * ng) * (sgg * (1.0 + gh * (1.0 - sgg))))
            dng = dng + jnp.sum(dy * xh, axis=0, keepdims=True)
            dyg = dy * ng
            do = r * (dyg - xh * jnp.mean(dyg * xh, axis=-1, keepdims=True))
            da = _bdot(do, vh, NT)
            dv_h.append(_bdot(a, do, TN) + _bdot(kdh, dst, NT))
            dkd = _bdot(vh, dst, NN)
            delast = jnp.sum(st * dst, axis=0, keepdims=True)
            dqe = _bdot(do, st, NN)
            dstate[hh] = elh * dst + _bdot(do, qe, TN)
            gk = dkd * kdh
            dblast = jnp.sum(gk, axis=0, keepdims=True) + delast * elh
            db = dqe * qe - gk + jnp.where(row == HG_CHUNK - 1, dblast, 0.0)
            dp = (msk_ref[HG_LEVELS] * da).astype(BF16)
            dq = dqe * ebh + _bdot(dp, kh, NN)
            dkk = dkd * edh + _bdot(dp, qh, TN)
            for l in range(HG_LEVELS):
                dp = (msk_ref[l] * da).astype(BF16)
                dql, dkl = _bdot(dp, kls[l], NN), _bdot(dp, qls[l], TN)
                w = cm["ws"][l][:, sl]
                dq = dq + dql * w
                dkk = dkk + dkl * w
                half = jnp.where(((row >> (HG_LEVELS - 1 - l)) & 1) == 1, 1.0, -1.0)
                dd = half * w * (dql * qh + dkl * kh)
                db = db + dd
                dr_h[l].append(-dd)
            dq_h.append(dq)
            dkk_h.append(dkk)
            db_h.append(db)
        cat = lambda xs: jnp.concatenate(xs, axis=1)
        cot = jnp.concatenate([cat(db_h)] + [cat(dr_h[l]) for l in range(HG_LEVELS)], axis=0)
        dlf = _dot01(stk_ref[...], cot, TN, parts=2)
        dq, dkk = cat(dq_h), cat(dkk_h)
        dft = dlf / cm["ft"]
        one_lb = 1.0 - cm["lb"]
        dz_ref[0] = (dq * (cm["sq"] * (1.0 + cm["qr"] * (1.0 - cm["sq"])))).astype(dz_ref.dtype)
        dz_ref[1] = ((dft - dkk) * one_lb * cm["sg"] * cm["sgn"]).astype(dz_ref.dtype)
        dz_ref[2] = cat(dv_h).astype(dz_ref.dtype)
        dz_ref[3] = cat(dgt_h).astype(dz_ref.dtype)
        dlb_ref[...] += jnp.sum((dft - dkk) * cm["sgn"], axis=0, keepdims=True)
        dng_ref[...] += dng

    rev = lambda c: nc - 1 - c
    return pl.pallas_call(
        body, name=name, grid=(2, nc),
        in_specs=[pl.BlockSpec((4, None, HG_CHUNK, HG_SLOT), lambda g, c: (0, g, rev(c), 0)),
                  pl.BlockSpec((1, HG_SLOT), lambda g, c: (0, g)),
                  pl.BlockSpec((1, HG_K), lambda g, c: (0, 0)),
                  pl.BlockSpec(stk.shape, lambda g, c: (0, 0)),
                  pl.BlockSpec(msk.shape, lambda g, c: (0, 0, 0)),
                  pl.BlockSpec((None, 4, HG_K, HG_K), lambda g, c: (rev(c), g, 0, 0)),
                  pl.BlockSpec((None, 4, HG_CHUNK, HG_CHUNK), lambda g, c: (rev(c), g, 0, 0)),
                  pl.BlockSpec((HG_CHUNK, HG_SLOT), lambda g, c: (rev(c), g)),
                  pl.BlockSpec((HG_CHUNK, HG_SLOT), lambda g, c: (rev(c), g))],
        out_specs=[pl.BlockSpec((4, None, HG_CHUNK, HG_SLOT), lambda g, c: (0, g, rev(c), 0)),
                   pl.BlockSpec((1, HG_SLOT), lambda g, c: (0, g)),
                   pl.BlockSpec((None, 1, HG_K), lambda g, c: (g, 0, 0))],
        out_shape=[_sds(z.shape, BF16), _sds((1, 2 * HG_SLOT), F32), _sds((2, 1, HG_K), F32)],
        scratch_shapes=[pltpu.VMEM((4, HG_K, HG_K), F32)],
        compiler_params=_params(("parallel", "arbitrary")),
    )(z, lb, ng, stk, msk, states, intra, o_pre, dog)


def _lb_fwd(name, logits):
    def body(l_ref, o_ref):
        x = l_ref[...]
        e = jnp.exp(x - jnp.max(x, axis=0, keepdims=True))
        s = e / jnp.sum(e, axis=0, keepdims=True)
        o_ref[0:1, :] = s[1:2]
        o_ref[1:2, :] = s[1:2] + s[2:3] + s[3:4]

    return pl.pallas_call(body, name=name, out_shape=_sds((2, logits.shape[1]), F32))(logits)


def _lb_bwd(name, logits, dlb):
    def body(l_ref, d_ref, o_ref):
        x = l_ref[...]
        e = jnp.exp(x - jnp.max(x, axis=0, keepdims=True))
        s = e / jnp.sum(e, axis=0, keepdims=True)
        d1, d3 = d_ref[0:1, :], d_ref[1:2, :]
        ds = [jnp.zeros_like(d1), d1 + d3, d3, d3]
        dot = sum(ds[r] * s[r:r + 1] for r in range(1, DEPTH))
        for r in range(DEPTH):
            o_ref[r:r + 1, :] = s[r:r + 1] * (ds[r] - dot)

    return pl.pallas_call(body, name=name, out_shape=_sds(logits.shape, F32))(logits, dlb)


SUB = 8


def _rows_down(x, prev, k):
    row = lax.broadcasted_iota(jnp.int32, x.shape, 0)
    return jnp.where(row >= k, pltpu.roll(x, k, 0), pltpu.roll(prev, k, 0))


def _rows_up(x, nxt, k):
    row = lax.broadcasted_iota(jnp.int32, x.shape, 0)
    return jnp.where(row < SUB - k, pltpu.roll(x, SUB - k, 0), pltpu.roll(nxt, SUB - k, 0))


def _conv_block(w_ref, b_ref, p, x, prev):
    x0, x1 = _rows_down(x, prev, 2), _rows_down(x, prev, 1)
    return b_ref[p] + w_ref[p, 0:1, :] * x0 + w_ref[p, 1:2, :] * x1 + w_ref[p, 2:3, :] * x, x0, x1


def _convgate_fwd(name, u, cw, cb):
    S = u.shape[2]
    tm = _tile(S, ROW_TILE)

    def body(u_ref, w_ref, b_ref, a_ref, halo):
        @pl.when(pl.program_id(1) == 0)
        def _():
            halo[...] = jnp.zeros_like(halo)

        def step(r, prev):
            pg, pv = prev
            out = []
            for s in range(2):
                rows = pl.ds(pl.multiple_of(r * 2 * SUB + s * SUB, SUB), SUB)
                xg, xv = u_ref[0, rows, :], u_ref[1, rows, :]
                cg = _conv_block(w_ref, b_ref, 0, xg, pg)[0]
                cv = _conv_block(w_ref, b_ref, 1, xv, pv)[0]
                out.append(cg * _sigmoid(cg) * cv)
                pg, pv = xg, xv
            a_ref[pl.ds(pl.multiple_of(r * 2 * SUB, 2 * SUB), 2 * SUB), :] = jnp.concatenate(out, axis=0).astype(a_ref.dtype)
            return pg, pv

        pg, pv = lax.fori_loop(0, tm // (2 * SUB), step, (halo[0], halo[1]))
        halo[0] = pg
        halo[1] = pv

    return pl.pallas_call(
        body, name=name, grid=(4, S // tm),
        in_specs=[pl.BlockSpec((2, None, tm, FF_SLOT), lambda j, t: (0, j, t, 0)),
                  pl.BlockSpec((2, None, 3, FF_SLOT), lambda j, t: (0, j, 0, 0)),
                  pl.BlockSpec((2, None, 1, FF_SLOT), lambda j, t: (0, j, 0, 0))],
        out_specs=pl.BlockSpec((None, tm, FF_SLOT), lambda j, t: (j, t, 0)),
        out_shape=_sds((4, S, FF_SLOT), BF16),
        scratch_shapes=[pltpu.VMEM((2, SUB, FF_SLOT), F32)],
        compiler_params=_params(("parallel", "arbitrary")),
    )(u, cw, cb)


def _convgate_bwd(name, u, cw, cb, da):
    S = u.shape[2]
    tm = _tile(S, ROW_TILE)
    nt = S // tm

    def body(u_ref, uh_ref, w_ref, b_ref, da_ref, du_out, dw_ref, db_ref, after, first, acc, du_ref):
        t = pl.program_id(1)

        @pl.when(t == 0)
        def _():
            after[...] = jnp.zeros_like(after)
            acc[...] = jnp.zeros_like(acc)

        def du_block(p, d, nxt):
            return w_ref[p, 2:3, :] * d + w_ref[p, 1:2, :] * _rows_up(d, nxt, 1) + w_ref[p, 0:1, :] * _rows_up(d, nxt, 2)

        def step(r, carry):
            pg, pv, dg_last, dv_last = carry
            dav = da_ref[pl.ds(pl.multiple_of(r * 2 * SUB, 2 * SUB), 2 * SUB), :].astype(F32)
            for s in range(2):
                at = r * 2 * SUB + s * SUB
                rows = pl.ds(pl.multiple_of(at, SUB), SUB)
                xg, xv = u_ref[0, rows, :], u_ref[1, rows, :]
                cg, x0g, x1g = _conv_block(w_ref, b_ref, 0, xg, pg)
                cv, x0v, x1v = _conv_block(w_ref, b_ref, 1, xv, pv)
                sg = _sigmoid(cg)
                dab = dav[s * SUB:(s + 1) * SUB]
                dg = dab * cv * (sg * (1.0 + cg * (1.0 - sg)))
                dv = dab * cg * sg
                for p, d, taps in ((0, dg, (x0g, x1g, xg)), (1, dv, (x0v, x1v, xv))):
                    for j in range(3):
                        acc[p, j] += d * taps[j]
                    acc[p, 3] += d
                if s == 0:
                    @pl.when(r == 0)
                    def _():
                        first[0] = dg
                        first[1] = dv

                    @pl.when(r > 0)
                    def _():
                        before = pl.ds(pl.multiple_of(at - SUB, SUB), SUB)
                        du_ref[0, before, :] = du_block(0, dg_last, dg)
                        du_ref[1, before, :] = du_block(1, dv_last, dv)
                else:
                    before = pl.ds(pl.multiple_of(at - SUB, SUB), SUB)
                    du_ref[0, before, :] = du_block(0, dg_last, dg)
                    du_ref[1, before, :] = du_block(1, dv_last, dv)
                pg, pv, dg_last, dv_last = xg, xv, dg, dv
            return pg, pv, dg_last, dv_last

        halo = jnp.where(t < nt - 1, uh_ref[...], 0.0)
        zero = jnp.zeros((SUB, FF_SLOT), F32)
        _, _, dg_last, dv_last = lax.fori_loop(0, tm // (2 * SUB), step, (halo[0], halo[1], zero, zero))
        du_ref[0, tm - SUB:tm, :] = du_block(0, dg_last, after[0])
        du_ref[1, tm - SUB:tm, :] = du_block(1, dv_last, after[1])
        du_out[...] = du_ref[...].astype(du_out.dtype)
        after[...] = first[...]
        for p in range(2):
            for j in range(3):
                dw_ref[p, j:j + 1, :] = jnp.sum(acc[p, j], axis=0, keepdims=True)
            db_ref[p] = jnp.sum(acc[p, 3], axis=0, keepdims=True)

    rev = lambda t: nt - 1 - t
    return pl.pallas_call(
        body, name=name, grid=(4, nt),
        in_specs=[pl.BlockSpec((2, None, tm, FF_SLOT), lambda j, t: (0, j, rev(t), 0)),
                  pl.BlockSpec((2, None, SUB, FF_SLOT), lambda j, t: (0, j, jnp.maximum(rev(t) * (tm // SUB) - 1, 0), 0)),
                  pl.BlockSpec((2, None, 3, FF_SLOT), lambda j, t: (0, j, 0, 0)),
                  pl.BlockSpec((2, None, 1, FF_SLOT), lambda j, t: (0, j, 0, 0)),
                  pl.BlockSpec((None, tm, FF_SLOT), lambda j, t: (j, rev(t), 0))],
        out_specs=[pl.BlockSpec((2, None, tm, FF_SLOT), lambda j, t: (0, j, rev(t), 0)),
                   pl.BlockSpec((2, None, 3, FF_SLOT), lambda j, t: (0, j, 0, 0)),
                   pl.BlockSpec((2, None, 1, FF_SLOT), lambda j, t: (0, j, 0, 0))],
        out_shape=[_sds(u.shape, BF16), _sds(cw.shape, F32), _sds(cb.shape, F32)],
        scratch_shapes=[pltpu.VMEM((2, SUB, FF_SLOT), F32), pltpu.VMEM((2, SUB, FF_SLOT), F32),
                        pltpu.VMEM((2, 4, SUB, FF_SLOT), F32), pltpu.VMEM((2, tm, FF_SLOT), F32)],
        compiler_params=_params(("parallel", "arbitrary")),
    )(u, u, cw, cb, da)


def _row_tile(R):
    for t in range(256, 15, -16):
        if R % t == 0:
            return t
    return R


def _adamw(name, gsrcs, w, m, v, dep=None):
    L = len(gsrcs)
    n, A, C = gsrcs[0].shape
    tr = _row_tile(A)
    deps = () if dep is None else (dep,)

    def body(*refs):
        g_refs = refs[:L]
        w_ref, m_ref, v_ref = refs[L:L + 3]
        go_ref, d_ref, mo_ref, vo_ref = refs[L + 3 + len(deps):]
        for k in range(L):
            @pl.when(pl.program_id(0) == k)
            def _(k=k):
                g = g_refs[k][0].astype(F32)
                for s in range(1, n):
                    g = g + g_refs[k][s].astype(F32)
                m2 = ADAM_B1 * m_ref[...] + (1.0 - ADAM_B1) * g
                v2 = ADAM_B2 * v_ref[...] + (1.0 - ADAM_B2) * (g * g)
                m_hat = m2 / (1.0 - ADAM_B1 ** ADAM_STEP)
                v_hat = v2 / (1.0 - ADAM_B2 ** ADAM_STEP)
                go_ref[...] = g
                d_ref[...] = -ADAM_LR * (m_hat / (jnp.sqrt(v_hat) + ADAM_EPS) + ADAM_WD * w_ref[...])
                mo_ref[...] = m2
                vo_ref[...] = v2

    g_specs = [pl.BlockSpec((n, tr, C), lambda l, i, k=k: (0, jnp.where(l == k, i, 0), 0)) for k in range(L)]
    blk = pl.BlockSpec((None, tr, C), lambda l, i: (l, i, 0))
    return pl.pallas_call(
        body, name=name, grid=(L, A // tr), in_specs=g_specs + [blk, blk, blk] + [_dep_spec(2)] * len(deps),
        out_specs=[blk] * 4, out_shape=[_sds((L, A, C), F32)] * 4, compiler_params=_params(("parallel", "parallel")),
    )(*gsrcs, w, m, v, *deps)


MESH = pl.DeviceIdType.MESH
HBM_SPEC = pl.BlockSpec(memory_space=pltpu.HBM)
N_PEERS = N_DEV - 1


def _mesh_place():
    x, y, c = lax.axis_index("x"), lax.axis_index("y"), lax.axis_index("c")
    peers = []
    for p in range(1, N_DEV):
        px = 1 - x if p & 4 else x
        py = 1 - y if p & 2 else y
        pc = 1 - c if p & 1 else c
        peers.append(((px, py, pc), 4 * px + 2 * py + pc))
    return 4 * x + 2 * y + c, peers


SEM_SPEC = pl.BlockSpec(memory_space=pltpu.SEMAPHORE)
ANY_SPEC = pl.BlockSpec(memory_space=pl.ANY)
EFFECT = pltpu.SideEffectType.DATAFLOW_SIDE_EFFECTING


def _exchange_refs(scatter, src, land, send, recv, k, p, dev, idx, me):
    return pltpu.make_async_remote_copy(src_ref=src[k].at[idx] if scatter else src[k], dst_ref=land[k].at[me],
                                        send_sem=send.at[k * N_PEERS + p], recv_sem=recv.at[k * N_PEERS + p], device_id=dev,
                                        device_id_type=MESH)


def _exchange_start(name, srcs, scatter, gate):
    n = len(srcs)
    lands = [lax.empty(s.shape if scatter else (N_DEV,) + s.shape, s.dtype) for s in srcs]

    def body(*refs):
        src, land = refs[:n], refs[n:2 * n]
        send, recv, own = refs[2 * n + 1:2 * n + 4]
        token = refs[-1]
        me, peers = _mesh_place()
        for k in range(n):
            pltpu.make_async_copy(src[k].at[me] if scatter else src[k], land[k].at[me], own.at[k]).start()
            for p, (dev, idx) in enumerate(peers):
                _exchange_refs(scatter, src, land, send, recv, k, p, dev, idx, me).start()
        token[...] = jnp.zeros_like(token)

    hbm = lambda a: pltpu.HBM(a.shape, a.dtype)
    outs = pl.pallas_call(
        body, name=name,
        out_shape=(pltpu.SemaphoreType.DMA((n * N_PEERS,)), pltpu.SemaphoreType.DMA((n * N_PEERS,)),
                   pltpu.SemaphoreType.DMA((n,)), *[hbm(s) for s in srcs], *[hbm(s) for s in lands], _sds(DEP_SHAPE, F32)),
        in_specs=[HBM_SPEC] * (2 * n) + [ANY_SPEC],
        out_specs=(SEM_SPEC, SEM_SPEC, SEM_SPEC, *[HBM_SPEC] * (2 * n), pl.BlockSpec(memory_space=pltpu.VMEM)),
        input_output_aliases={j: 3 + j for j in range(2 * n)},
        compiler_params=pltpu.CompilerParams(has_side_effects=EFFECT),
    )(*[pltpu.with_memory_space_constraint(s, pltpu.HBM) for s in srcs],
      *[pltpu.with_memory_space_constraint(s, pltpu.HBM) for s in lands], gate)
    return outs[:3], None, list(outs[3:3 + n]), list(outs[3 + n:3 + 2 * n]), outs[-1]


def _exchange_wait(name, started, scatter, after):
    (send, recv, own), _, srcs, lands, _ = started
    n = len(srcs)

    def body(*refs):
        src, land = refs[:n], refs[n:2 * n]
        send, recv, own = refs[2 * n:2 * n + 3]
        me, peers = _mesh_place()
        for k in range(n):
            pltpu.make_async_copy(src[k].at[me] if scatter else src[k], land[k].at[me], own.at[k]).wait()
            for p, (dev, idx) in enumerate(peers):
                cp = pltpu.make_async_remote_copy(src_ref=src[k].at[idx] if scatter else src[k], dst_ref=land[k].at[idx],
                                                  send_sem=send.at[k * N_PEERS + p], recv_sem=recv.at[k * N_PEERS + p], device_id=dev,
                                                  device_id_type=MESH)
                cp.wait_send()
                cp.wait_recv()

    hbm = lambda a: pltpu.HBM(a.shape, a.dtype)
    outs = pl.pallas_call(
        body, name=name, out_shape=(*[hbm(s) for s in srcs], *[hbm(s) for s in lands]),
        in_specs=[HBM_SPEC] * (2 * n) + [SEM_SPEC, SEM_SPEC, SEM_SPEC, ANY_SPEC], out_specs=tuple([HBM_SPEC] * (2 * n)),
        input_output_aliases={j: j for j in range(2 * n)},
        compiler_params=pltpu.CompilerParams(has_side_effects=EFFECT),
    )(*srcs, *lands, send, recv, own, after)
    return list(outs[n:])


def _sum_devices(name, parts):
    def body(p_ref, o_ref):
        tot = p_ref[0]
        for j in range(1, N_DEV):
            tot = tot + p_ref[j]
        o_ref[...] = tot

    return pl.pallas_call(body, name=name, out_shape=_sds(parts.shape[1:], F32),
                          compiler_params=pltpu.CompilerParams(vmem_limit_bytes=VMEM_LIMIT))(parts)


def _rows(a, width=D_MODEL):
    flat = a.reshape(-1)
    return jnp.pad(flat, (0, (-flat.shape[0]) % width)).reshape(-1, width)


def _pack_rows(parts):
    blocks = []
    for p in parts:
        r = _rows(p)
        blocks.append(jnp.pad(r, ((0, (-r.shape[0]) % 8), (0, 0))))
    return jnp.concatenate(blocks, axis=0)


def _unpack_rows(rows, shapes):
    out, at = [], 0
    for s in shapes:
        size = int(np.prod(s))
        n = -(-size // D_MODEL)
        out.append(rows[at:at + n].reshape(-1)[:size].reshape(s))
        at += -(-n // 8) * 8
    return out


def kernel(x, norm_mix, norm_ffn, norm_final, attn_w_in, attn_w_out, attn_sinks, hgrn_w_in, hgrn_w_out, hgrn_norm, hgrn_lb_logits, ffn_w_up, ffn_conv_w, ffn_conv_b, ffn_w_down, loss_target, m_norm_mix, m_norm_ffn, m_norm_final, m_attn_w_in, m_attn_w_out, m_attn_sinks, m_hgrn_w_in, m_hgrn_w_out, m_hgrn_norm, m_hgrn_lb_logits, m_ffn_w_up, m_ffn_conv_w, m_ffn_conv_b, m_ffn_w_down, v_norm_mix, v_norm_ffn, v_norm_final, v_attn_w_in, v_attn_w_out, v_attn_sinks, v_hgrn_w_in, v_hgrn_w_out, v_hgrn_norm, v_hgrn_lb_logits, v_ffn_w_up, v_ffn_conv_w, v_ffn_conv_b, v_ffn_w_down):
    S = x.shape[1]
    n_attn, n_hgrn = attn_w_in.shape[0], hgrn_w_in.shape[0]
    me = 4 * lax.axis_index("x") + 2 * lax.axis_index("y") + lax.axis_index("c")

    wa_in_t, wa_out_b = attn_w_in.transpose(0, 2, 1).astype(BF16), attn_w_out.astype(BF16)
    wh_in_b, wh_out_b = hgrn_w_in.astype(BF16), hgrn_w_out.astype(BF16)
    wf_up_b, wf_down_b = ffn_w_up.transpose(0, 2, 1).astype(BF16), ffn_w_down.astype(BF16)
    conv_b = ffn_conv_b.reshape(DEPTH, 2, 4, 1, FF_SLOT)
    lb = _lb_fwd("lb_fwd", hgrn_lb_logits)

    def unit_shards(l, part):
        if part == "ffn":
            return [wf_up_b[l], wf_down_b[l], ffn_conv_w[l]]
        return [wa_in_t[l // 2], wa_out_b[l // 2]] if l % 2 == 0 else [wh_in_b[l // 2], wh_out_b[l // 2]]

    def unit_weights(l, part, w):
        if part == "ffn":
            return w[0][None], w[1].reshape(1, 4, FF_SLOT, D_MODEL), w[2].reshape(2, 4, 3, FF_SLOT)
        if l % 2 == 0:
            return w[0].reshape(1, ATTN_IN, D_MODEL), w[1].reshape(1, D_MODEL, D_MODEL)
        return w[0][None], w[1].reshape(1, D_MODEL, D_MODEL)

    units = [(l, part) for l in range(DEPTH) for part in ("mix", "ffn")]
    gathers = [_exchange_start("gather_start0", unit_shards(*units[0]), False, norm_final)]
    gathers.append(_exchange_start("gather_start1", unit_shards(*units[1]), False, gathers[0][4]))
    arrived = _exchange_wait("gather_wait0", gathers[0], False, gathers[1][4])
    weights, saved = {}, [dict() for _ in range(DEPTH)]
    h = x[0]
    hn = _rmsnorm_fwd("norm_mix_fwd0", h, norm_mix[0:1])
    for n, (l, part) in enumerate(units):
        i, sv = l // 2, saved[l]
        weights[l, part] = w = unit_weights(l, part, arrived)
        dep = None
        if n + 2 < len(units):
            gathers.append(_exchange_start(f"gather_start{n + 2}", unit_shards(*units[n + 2]), False, arrived[0]))
            dep = gathers[n + 2][4]
        if part == "mix":
            sv["h"], sv["hn"] = h, hn
            if l % 2 == 0:
                sv["proj"] = _proj_rows(f"attn_proj{i}", hn, w[0], 0, BF16, dep)
                sv["o"], *sv["kept"] = _attn_fwd(f"attn_fwd{i}", sv["proj"], attn_sinks[i:i + 1])
                h, hn = _out_proj(f"attn_out{i}", sv["o"], w[1], 0, h, norm_ffn[l:l + 1])
            else:
                sv["z"] = _proj_slots(f"hgrn_proj{i}", hn, w[0], 0, dep=dep).reshape(4, 2, S, HG_SLOT)
                sv["o"], *sv["kept"] = _hg_fwd(f"hgrn_fwd{i}", sv["z"], lb[i:i + 1], hgrn_norm[i:i + 1])
                h, hn = _out_proj(f"hgrn_out{i}", sv["o"], w[1], 0, h, norm_ffn[l:l + 1])
        else:
            sv["h2"], sv["hn2"] = h, hn
            sv["u"] = _proj_slots(f"ffn_up{l}", hn, w[0], 0, True, dep).reshape(2, 4, S, FF_SLOT)
            sv["a"] = _convgate_fwd(f"ffn_gate{l}", sv["u"], w[2], conv_b[l])
            if l + 1 < DEPTH:
                h, hn = _down_proj(f"ffn_down{l}", sv["a"], w[1], 0, h, norm_mix[l + 1:l + 2])
            else:
                h = _down_proj(f"ffn_down{l}", sv["a"], w[1], 0, h)
        if n + 1 < len(units):
            arrived = _exchange_wait(f"gather_wait{n + 1}", gathers[n + 1], False, h)
    dh, d_norm_final, loss_rows = _loss_head("loss_head", h, norm_final[None], loss_target[0])

    d_conv_w, d_conv_b, d_norm_mix, d_norm_ffn = [None] * DEPTH, [None] * DEPTH, [None] * DEPTH, [None] * DEPTH
    d_sinks, d_lb, d_hgrn_norm = [None] * n_attn, [None] * n_hgrn, [None] * n_hgrn
    received, pending = {}, []
    for l, part in reversed(units):
        i, sv, w = l // 2, saved[l], weights[l, part]
        dep = pending[-1][1][4] if pending else None
        if part == "ffn":
            da = _dgrad_down(f"ffn_down_dgrad{l}", dh, w[1], 0, dep)
            g_down = _wgrad_down(f"ffn_down_wgrad{l}", sv["a"], dh).reshape(N_DEV, D_FF // N_DEV, D_MODEL)
            du, d_conv_w[l], d_conv_b[l] = _convgate_bwd(f"ffn_gate_bwd{l}", sv["u"], w[2], conv_b[l], da)
            du = du.reshape(N_DEV, S, FF_SLOT)
            grads = [_wgrad_slots(f"ffn_up_wgrad{l}", sv["hn2"], du, True), g_down]
            dh, d_norm_ffn[l] = _dgrad_slots(f"ffn_up_dgrad{l}", du, w[0], 0, (sv["h2"], norm_ffn[l:l + 1], dh), True)
        else:
            if l % 2 == 0:
                do = _dgrad_out(f"attn_out_dgrad{i}", dh, w[1], 0, BF16, dep)
                g_out = _wgrad_rows(f"attn_out_wgrad{i}", sv["o"], dh)
                dproj, d_sinks[i] = _attn_bwd(f"attn_bwd{i}", sv["proj"], *sv["kept"], do)
                g_in = _wgrad_rows(f"attn_proj_wgrad{i}", dproj, sv["hn"]).reshape(N_DEV, ATTN_IN // N_DEV, D_MODEL)
                dh_new = _dgrad_rows(f"attn_proj_dgrad{i}", dproj, w[0], 0, (sv["h"], norm_mix[l:l + 1], dh))
            else:
                dog = _dgrad_out(f"hgrn_out_dgrad{i}", dh, w[1], 0, F32, dep)
                g_out = _wgrad_rows(f"hgrn_out_wgrad{i}", sv["o"], dh)
                dz, d_lb[i], dng = _hg_bwd(f"hgrn_bwd{i}", sv["z"], lb[i:i + 1], hgrn_norm[i:i + 1], *sv["kept"], dog)
                d_hgrn_norm[i] = dng[0] + dng[1]
                dz = dz.reshape(N_DEV, S, HG_SLOT)
                g_in = _wgrad_slots(f"hgrn_proj_wgrad{i}", sv["hn"], dz)
                dh_new = _dgrad_slots(f"hgrn_proj_dgrad{i}", dz, w[0], 0, (sv["h"], norm_mix[l:l + 1], dh))
            grads = [g_in, g_out.reshape(N_DEV, D_MODEL // N_DEV, D_MODEL)]
            dh, d_norm_mix[l] = dh_new
        gate = dh
        if len(pending) == 2:
            key, oldest = pending.pop(0)
            received[key] = _exchange_wait(f"scatter_wait_{key[1]}{key[0]}", oldest, True, dh)
            gate = received[key][0]
        pending.append(((l, part), _exchange_start(f"scatter_start_{part}{l}", grads, True, gate)))
    grad_x = dh[None]

    small_shapes = [(DEPTH, D_MODEL), (DEPTH, D_MODEL), (1, D_MODEL), (1, D_MODEL), (n_hgrn, D_MODEL), (n_attn, 128),
                    (n_hgrn, HG_K), (DEPTH, 2 * D_FF), (DEPTH, N_DEV, 3, FF_SLOT)]
    partial = _pack_rows([
        jnp.concatenate(d_norm_mix), jnp.concatenate(d_norm_ffn), d_norm_final, loss_rows, jnp.concatenate(d_lb),
        jnp.concatenate(d_sinks), jnp.concatenate(d_hgrn_norm), jnp.stack(d_conv_b), jnp.stack(d_conv_w)])
    small_started = _exchange_start("small_start", [partial], False, pending[-1][1][4])
    attn_layers, hgrn_layers = range(0, DEPTH, 2), range(1, DEPTH, 2)

    def transposed(ts):
        return [t.transpose(0, 2, 1) for t in ts]

    big = {"hgrn_w_in": _adamw("adamw_hgrn_in", [received[l, "mix"][0] for l in hgrn_layers], hgrn_w_in, m_hgrn_w_in,
                               v_hgrn_w_in, dep=small_started[4])}
    big["hgrn_w_out"] = _adamw("adamw_hgrn_out", [received[l, "mix"][1] for l in hgrn_layers], hgrn_w_out, m_hgrn_w_out, v_hgrn_w_out)
    key, oldest = pending.pop(0)
    received[key] = _exchange_wait(f"scatter_wait_{key[1]}{key[0]}", oldest, True, big["hgrn_w_in"][3])
    up_t = _adamw("adamw_ffn_up", [received[l, "ffn"][0] for l in range(DEPTH)], *transposed((ffn_w_up, m_ffn_w_up, v_ffn_w_up)))
    big["ffn_w_up"] = transposed(up_t)
    big["ffn_w_down"] = _adamw("adamw_ffn_down", [received[l, "ffn"][1] for l in range(DEPTH)], ffn_w_down, m_ffn_w_down, v_ffn_w_down)
    key, oldest = pending.pop(0)
    received[key] = _exchange_wait(f"scatter_wait_{key[1]}{key[0]}", oldest, True, up_t[3])
    total = _sum_devices("sum_small", _exchange_wait("small_wait", small_started, False, up_t[3])[0])
    (g_norm_mix, g_norm_ffn, g_norm_final, loss_sum, g_lb, g_sinks, g_hgrn_norm, g_conv_b, g_conv_w_all) = _unpack_rows(
        total, small_shapes)

    loss = jnp.sum(loss_sum)
    g_norm_final = g_norm_final[0]
    g_sinks = g_sinks[:, :N_Q_HEADS]
    g_lb_logits = _lb_bwd("lb_bwd", hgrn_lb_logits, g_lb)
    g_conv_w = lax.dynamic_index_in_dim(g_conv_w_all, me, axis=1, keepdims=False)

    big.update({
        "attn_w_in": transposed(_adamw("adamw_attn_in", [received[l, "mix"][0] for l in attn_layers],
                                       *transposed((attn_w_in, m_attn_w_in, v_attn_w_in)))),
        "attn_w_out": _adamw("adamw_attn_out", [received[l, "mix"][1] for l in attn_layers], attn_w_out, m_attn_w_out, v_attn_w_out),
        "ffn_conv_w": _adamw("adamw_conv_w", [g_conv_w[l][None] for l in range(DEPTH)], ffn_conv_w, m_ffn_conv_w, v_ffn_conv_w),
    })
    small_w = [norm_mix, norm_ffn, norm_final, attn_sinks, hgrn_norm, hgrn_lb_logits, ffn_conv_b]
    small_m = [m_norm_mix, m_norm_ffn, m_norm_final, m_attn_sinks, m_hgrn_norm, m_hgrn_lb_logits, m_ffn_conv_b]
    small_v = [v_norm_mix, v_norm_ffn, v_norm_final, v_attn_sinks, v_hgrn_norm, v_hgrn_lb_logits, v_ffn_conv_b]
    small_g = [g_norm_mix, g_norm_ffn, g_norm_final, g_sinks, g_hgrn_norm, g_lb_logits, g_conv_b]
    outs = _adamw("adamw_small", [_pack_rows(small_g)[None]], *[_pack_rows(t)[None] for t in (small_w, small_m, small_v)])
    outs = [o[0] for o in outs]
    shapes = [w.shape for w in small_w]
    small = {n: [t[j] for t in [_unpack_rows(o, shapes) for o in outs]]
             for j, n in enumerate(["norm_mix", "norm_ffn", "norm_final", "attn_sinks", "hgrn_norm", "hgrn_lb_logits", "ffn_conv_b"])}
    order = ["norm_mix", "norm_ffn", "norm_final", "attn_w_in", "attn_w_out", "attn_sinks", "hgrn_w_in", "hgrn_w_out",
             "hgrn_norm", "hgrn_lb_logits", "ffn_w_up", "ffn_conv_w", "ffn_conv_b", "ffn_w_down"]
    res = {**big, **small}
    return (loss, grad_x, *[res[n][0] for n in order], *[res[n][1] for n in order], *[res[n][2] for n in order],
            *[res[n][3] for n in order])
```

```python
import numpy as np
import jax
import jax.numpy as jnp
from jax import lax
from jax.experimental import pallas as pl
from jax.experimental.pallas import tpu as pltpu

F32 = jnp.float32
BF16 = jnp.bfloat16

D_MODEL = 1024
DEPTH = 4
HEAD_DIM = 64
N_Q_HEADS = 16
N_KV_HEADS = 4
Q_PER_KV = 4
ATTN_BLOCK = 128
ATTN_IN = 1536
HG_HEADS = 8
HG_K = 128
HG_CHUNK = 64
HG_IN = 4096
D_FF = 2816
EPS = 1e-6
N_DEV = 8
FF_SLOT = 2 * D_FF // N_DEV
HG_SLOT = HG_IN // N_DEV
HG_LEVELS = 6

ADAM_LR = 0.001
ADAM_B1 = 0.9
ADAM_B2 = 0.999
ADAM_EPS = 1e-08
ADAM_WD = 0.01
ADAM_STEP = 10

VMEM_LIMIT = 56 * 1024 * 1024
ROW_TILE = 1024
WIDE_ROW_TILE = 2048
NEG_BIG = -1e30

NN = (((1,), (0,)), ((), ()))
NT = (((1,), (1,)), ((), ()))
TN = (((0,), (0,)), ((), ()))


def _bdot(a, b, dn):
    return lax.dot_general(a.astype(BF16), b.astype(BF16), dn, preferred_element_type=F32)


def _sds(shape, dtype):
    return jax.ShapeDtypeStruct(tuple(shape), dtype)


def _params(sem):
    return pltpu.CompilerParams(dimension_semantics=sem, vmem_limit_bytes=VMEM_LIMIT)


DEP_SHAPE = (8, 128)


def _dep_spec(rank):
    return pl.BlockSpec(DEP_SHAPE, lambda *_: (0, 0))


def _matmul(name, a, b, *, dn, grid, a_spec, b_spec, o_spec, out_shape, acc_shape=None, extra=(), extra_specs=(),
            finish=None, dep=None, sem=("parallel", "parallel", "arbitrary")):
    nk = grid[2]
    many = isinstance(out_shape, (list, tuple))
    n_in = 2 + len(extra) + (dep is not None)
    n_out = len(out_shape) if many else 1

    def body(*refs):
        a_ref, b_ref = refs[0], refs[1]
        outs = refs[n_in:n_in + n_out]

        def prod():
            return _bdot(a_ref[...], b_ref[...], dn)

        def done(v):
            if finish is None:
                outs[0][...] = v.astype(outs[0].dtype)
            else:
                finish(v, refs[2:2 + len(extra)], outs)

        if nk == 1:
            done(prod())
        else:
            acc = refs[-1]
            k = pl.program_id(2)

            @pl.when(k == 0)
            def _():
                acc[...] = prod()

            @pl.when(k > 0)
            def _():
                acc[...] += prod()

            @pl.when(k == nk - 1)
            def _():
                done(acc[...])

    in_specs = [a_spec, b_spec, *extra_specs] + ([_dep_spec(3)] if dep is not None else [])
    args = (a, b, *extra) + ((dep,) if dep is not None else ())
    scratch = [] if nk == 1 else [pltpu.VMEM(acc_shape, F32)]
    return pl.pallas_call(
        body, name=name, grid=grid, in_specs=in_specs, out_specs=o_spec, out_shape=out_shape,
        scratch_shapes=scratch, compiler_params=_params(sem),
    )(*args)


def _rms(x):
    return lax.rsqrt(jnp.mean(x * x, axis=-1, keepdims=True) + EPS)


def _residual_finish(v, ex, outs):
    h = v + ex[0][...]
    outs[0][...] = h
    if len(ex) > 1:
        outs[1][...] = (h * _rms(h) * ex[1][...]).astype(outs[1].dtype)


def _norm_bwd_finish(v, ex, outs):
    x = ex[0][...]
    r = _rms(x)
    xh = x * r
    dyg = v * ex[1][...]
    outs[0][...] = ex[2][...] + r * (dyg - xh * jnp.mean(dyg * xh, axis=-1, keepdims=True))
    part = jnp.sum(v * xh, axis=0, keepdims=True)

    @pl.when(pl.program_id(0) == 0)
    def _():
        outs[1][...] = part

    @pl.when(pl.program_id(0) > 0)
    def _():
        outs[1][...] += part


def _row_io(tm, norm_g):
    row = pl.BlockSpec((tm, D_MODEL), lambda i, j, k: (i, 0))
    vec = pl.BlockSpec((1, D_MODEL), lambda i, j, k: (0, 0))
    if norm_g is None:
        return (row,), row, lambda S: _sds((S, D_MODEL), F32)
    return (row, vec), [row, row], lambda S: [_sds((S, D_MODEL), F32), _sds((S, D_MODEL), BF16)]


def _tile(n, t):
    return min(n, t)


def _proj_rows(name, hn, wt, l, out_dtype, dep=None):
    S, N = hn.shape[0], wt.shape[1]
    tm, tn = _tile(S, ROW_TILE), 512
    return _matmul(
        name, hn, wt, dn=NT, grid=(S // tm, N // tn, 1),
        a_spec=pl.BlockSpec((tm, D_MODEL), lambda i, j, k: (i, 0)),
        b_spec=pl.BlockSpec((None, tn, D_MODEL), lambda i, j, k: (l, j, 0)),
        o_spec=pl.BlockSpec((tm, tn), lambda i, j, k: (i, j)),
        out_shape=_sds((S, N), out_dtype), dep=dep)


def _slot_weight(w, transposed):
    if transposed:
        return w.shape[2], (None, None, w.shape[2], D_MODEL), NT, NN
    return w.shape[3], (None, None, D_MODEL, w.shape[3]), NN, NT


def _proj_slots(name, hn, w, l, transposed=False, dep=None):
    S = hn.shape[0]
    r, blk, dn, _ = _slot_weight(w, transposed)
    tm = _tile(S, WIDE_ROW_TILE)
    return _matmul(
        name, hn, w, dn=dn, grid=(N_DEV, S // tm, 1),
        a_spec=pl.BlockSpec((tm, D_MODEL), lambda j, i, k: (i, 0)),
        b_spec=pl.BlockSpec(blk, lambda j, i, k: (l, j, 0, 0)),
        o_spec=pl.BlockSpec((None, tm, r), lambda j, i, k: (j, i, 0)),
        out_shape=_sds((N_DEV, S, r), F32), dep=dep)


def _out_proj(name, o, w, l, h, norm_g=None):
    S, K = o.shape
    tm = _tile(S, ROW_TILE)
    extra_specs, o_spec, out_shape = _row_io(tm, norm_g)
    return _matmul(
        name, o, w, dn=NN, grid=(S // tm, 1, 1),
        a_spec=pl.BlockSpec((tm, K), lambda i, j, k: (i, 0)),
        b_spec=pl.BlockSpec((None, K, D_MODEL), lambda i, j, k: (l, 0, 0)),
        o_spec=o_spec, out_shape=out_shape(S), extra=(h,) if norm_g is None else (h, norm_g),
        extra_specs=extra_specs, finish=_residual_finish)


def _down_proj(name, a, w, l, h, norm_g=None):
    nj, S, r = a.shape
    tm = _tile(S, ROW_TILE)
    extra_specs, o_spec, out_shape = _row_io(tm, norm_g)
    return _matmul(
        name, a, w, dn=NN, grid=(S // tm, 1, nj),
        a_spec=pl.BlockSpec((None, tm, r), lambda i, j, k: (k, i, 0)),
        b_spec=pl.BlockSpec((None, None, r, D_MODEL), lambda i, j, k: (l, k, 0, 0)),
        o_spec=o_spec, out_shape=out_shape(S), acc_shape=(tm, D_MODEL),
        extra=(h,) if norm_g is None else (h, norm_g), extra_specs=extra_specs, finish=_residual_finish)


def _dgrad_down(name, dh, w, l, dep=None):
    S = dh.shape[0]
    nj, r = w.shape[1], w.shape[2]
    tm = _tile(S, ROW_TILE)
    return _matmul(
        name, dh, w, dn=NT, grid=(nj, S // tm, 1),
        a_spec=pl.BlockSpec((tm, D_MODEL), lambda j, i, k: (i, 0)),
        b_spec=pl.BlockSpec((None, None, r, D_MODEL), lambda j, i, k: (l, j, 0, 0)),
        o_spec=pl.BlockSpec((None, tm, r), lambda j, i, k: (j, i, 0)),
        out_shape=_sds((nj, S, r), BF16), dep=dep)


def _wgrad_down(name, a, dh):
    nj, S, r = a.shape
    tk = _tile(S, ROW_TILE)
    return _matmul(
        name, a, dh, dn=TN, grid=(nj, 1, S // tk),
        a_spec=pl.BlockSpec((None, tk, r), lambda s, j, k: (s, k, 0)),
        b_spec=pl.BlockSpec((tk, D_MODEL), lambda s, j, k: (k, 0)),
        o_spec=pl.BlockSpec((None, r, D_MODEL), lambda s, j, k: (s, 0, 0)),
        out_shape=_sds((nj, r, D_MODEL), BF16), acc_shape=(r, D_MODEL))


def _norm_bwd_io(tm, S):
    row = pl.BlockSpec((tm, D_MODEL), lambda i, j, k: (i, 0))
    vec = pl.BlockSpec((1, D_MODEL), lambda i, j, k: (0, 0))
    return dict(extra_specs=(row, vec, row), o_spec=[row, vec], out_shape=[_sds((S, D_MODEL), F32), _sds((1, D_MODEL), F32)],
                finish=_norm_bwd_finish, sem=("arbitrary", "arbitrary", "arbitrary"))


def _dgrad_slots(name, dz, w, l, norm, transposed=False):
    nj, S, r = dz.shape
    _, blk, _, dn = _slot_weight(w, transposed)
    tm = _tile(S, ROW_TILE)
    return _matmul(
        name, dz, w, dn=dn, grid=(S // tm, 1, nj),
        a_spec=pl.BlockSpec((None, tm, r), lambda i, j, k: (k, i, 0)),
        b_spec=pl.BlockSpec(blk, lambda i, j, k: (l, k, 0, 0)),
        acc_shape=(tm, D_MODEL), extra=norm, **_norm_bwd_io(tm, S))


def _wgrad_slots(name, hn, dz, transposed=False):
    nj, S, r = dz.shape
    tk = _tile(S, ROW_TILE)
    hn_spec = pl.BlockSpec((tk, D_MODEL), lambda s, j, k: (k, 0))
    dz_spec = pl.BlockSpec((None, tk, r), lambda s, j, k: (s, k, 0))
    if transposed:
        return _matmul(
            name, dz, hn, dn=TN, grid=(nj, 1, S // tk), a_spec=dz_spec, b_spec=hn_spec,
            o_spec=pl.BlockSpec((None, r, D_MODEL), lambda s, j, k: (s, 0, 0)),
            out_shape=_sds((nj, r, D_MODEL), BF16), acc_shape=(r, D_MODEL))
    return _matmul(
        name, hn, dz, dn=TN, grid=(nj, 1, S // tk), a_spec=hn_spec, b_spec=dz_spec,
        o_spec=pl.BlockSpec((None, D_MODEL, r), lambda s, j, k: (s, 0, 0)),
        out_shape=_sds((nj, D_MODEL, r), BF16), acc_shape=(D_MODEL, r))


def _dgrad_out(name, dh, w, l, out_dtype, dep=None):
    S, K = dh.shape[0], w.shape[1]
    tm = _tile(S, ROW_TILE)
    return _matmul(
        name, dh, w, dn=NT, grid=(S // tm, 1, 1),
        a_spec=pl.BlockSpec((tm, D_MODEL), lambda i, j, k: (i, 0)),
        b_spec=pl.BlockSpec((None, K, D_MODEL), lambda i, j, k: (l, 0, 0)),
        o_spec=pl.BlockSpec((tm, K), lambda i, j, k: (i, 0)),
        out_shape=_sds((S, K), out_dtype), dep=dep)


def _wgrad_rows(name, a, b):
    S, K = a.shape
    tk = _tile(S, ROW_TILE)
    return _matmul(
        name, a, b, dn=TN, grid=(1, 1, S // tk),
        a_spec=pl.BlockSpec((tk, K), lambda i, j, k: (k, 0)),
        b_spec=pl.BlockSpec((tk, D_MODEL), lambda i, j, k: (k, 0)),
        o_spec=pl.BlockSpec((K, D_MODEL), lambda i, j, k: (0, 0)),
        out_shape=_sds((K, D_MODEL), BF16), acc_shape=(K, D_MODEL))


def _dgrad_rows(name, dz, wt, l, norm):
    S, N = dz.shape
    tm = _tile(S, ROW_TILE)
    return _matmul(
        name, dz, wt, dn=NN, grid=(S // tm, 1, 1),
        a_spec=pl.BlockSpec((tm, N), lambda i, j, k: (i, 0)),
        b_spec=pl.BlockSpec((None, N, D_MODEL), lambda i, j, k: (l, 0, 0)),
        extra=norm, **_norm_bwd_io(tm, S))


def _rmsnorm_fwd(name, h, g):
    S = h.shape[0]
    tm = _tile(S, ROW_TILE)

    def body(h_ref, g_ref, o_ref):
        x = h_ref[...]
        o_ref[...] = (x * _rms(x) * g_ref[...]).astype(o_ref.dtype)

    row = pl.BlockSpec((tm, D_MODEL), lambda i: (i, 0))
    return pl.pallas_call(
        body, name=name, grid=(S // tm,), in_specs=[row, pl.BlockSpec((1, D_MODEL), lambda i: (0, 0))],
        out_specs=row, out_shape=_sds((S, D_MODEL), BF16), compiler_params=_params(("parallel",)),
    )(h, g)


def _loss_head(name, h, g, target):
    S = h.shape[0]
    tm = _tile(S, ROW_TILE)

    def body(h_ref, g_ref, t_ref, dh_ref, dg_ref, ls_ref):
        x = h_ref[...]
        r = lax.rsqrt(jnp.mean(x * x, axis=-1, keepdims=True) + EPS)
        xh = x * r
        diff = xh * g_ref[...] - t_ref[...]
        dyf = diff * (1.0 / D_MODEL)
        dyg = dyf * g_ref[...]
        dh_ref[...] = r * (dyg - xh * jnp.mean(dyg * xh, axis=-1, keepdims=True))
        part = jnp.sum(dyf * xh, axis=0, keepdims=True)
        lpart = jnp.sum(diff * diff, axis=0, keepdims=True) * (0.5 / D_MODEL)

        @pl.when(pl.program_id(0) == 0)
        def _():
            dg_ref[...] = part
            ls_ref[...] = lpart

        @pl.when(pl.program_id(0) > 0)
        def _():
            dg_ref[...] += part
            ls_ref[...] += lpart

    row = pl.BlockSpec((tm, D_MODEL), lambda i: (i, 0))
    vec = pl.BlockSpec((1, D_MODEL), lambda i: (0, 0))
    return pl.pallas_call(
        body, name=name, grid=(S // tm,), in_specs=[row, vec, row], out_specs=[row, vec, vec],
        out_shape=[_sds((S, D_MODEL), F32), _sds((1, D_MODEL), F32), _sds((1, D_MODEL), F32)],
        compiler_params=_params(("arbitrary",)),
    )(h, g, target)


ATTN_SCALE = HEAD_DIM ** -0.5
ALIBI_SLOPES = [2.0 ** (-8.0 * (h + 1) / N_Q_HEADS) for h in range(N_Q_HEADS)]
K_COL = N_Q_HEADS * HEAD_DIM
KV_COLS = N_KV_HEADS * HEAD_DIM
V_COL = K_COL + KV_COLS


def _attn_masks(n):
    qi = lax.broadcasted_iota(jnp.int32, (ATTN_BLOCK, ATTN_BLOCK), 0)
    ki = lax.broadcasted_iota(jnp.int32, (ATTN_BLOCK, ATTN_BLOCK), 1)
    dist_c = (qi - ki).astype(F32)
    return dist_c + float(ATTN_BLOCK), dist_c, (ki > qi) & (n > 0), qi >= ki


def _attn_probs(raw_p, raw_c, sink, slope, masks):
    dist_p, dist_c, valid_p, valid_c = masks
    sp = jnp.where(valid_p, raw_p * ATTN_SCALE - slope * dist_p, NEG_BIG)
    sc = jnp.where(valid_c, raw_c * ATTN_SCALE - slope * dist_c, NEG_BIG)
    m = jnp.maximum(jnp.maximum(jnp.max(sp, axis=-1, keepdims=True), jnp.max(sc, axis=-1, keepdims=True)), sink)
    ep, ec, es = jnp.exp(sp - m), jnp.exp(sc - m), jnp.exp(sink - m)
    inv = 1.0 / (jnp.sum(ep, axis=-1, keepdims=True) + jnp.sum(ec, axis=-1, keepdims=True) + es)
    return ep * inv, ec * inv, es * inv


def _group_rows(ref, m):
    return jnp.concatenate([ref[:, HEAD_DIM * (Q_PER_KV * m + g):HEAD_DIM * (Q_PER_KV * m + g + 1)]
                            for g in range(Q_PER_KV)], axis=0)


def _head_rows(x, g):
    return x[ATTN_BLOCK * g:ATTN_BLOCK * (g + 1)]


def _attn_specs(nblk):
    last = nblk - 1
    kcol, vcol = K_COL // KV_COLS, V_COL // KV_COLS
    return [
        pl.BlockSpec((ATTN_BLOCK, K_COL), lambda n: (jnp.minimum(n, last), 0)),
        pl.BlockSpec((ATTN_BLOCK, KV_COLS), lambda n: (jnp.minimum(n, last), kcol)),
        pl.BlockSpec((ATTN_BLOCK, KV_COLS), lambda n: (jnp.maximum(jnp.minimum(n, last) - 1, 0), kcol)),
        pl.BlockSpec((ATTN_BLOCK, KV_COLS), lambda n: (jnp.minimum(n, last), vcol)),
        pl.BlockSpec((ATTN_BLOCK, KV_COLS), lambda n: (jnp.maximum(jnp.minimum(n, last) - 1, 0), vcol)),
    ]


P_COLS = 2 * ATTN_BLOCK


def _attn_fwd(name, proj, sinks):
    S = proj.shape[0]
    nblk = S // ATTN_BLOCK

    def body(q_ref, kc_ref, kp_ref, vc_ref, vp_ref, sk_ref, o_ref, p_ref, ps_ref):
        masks = _attn_masks(pl.program_id(0))
        lane = lax.broadcasted_iota(jnp.int32, (ATTN_BLOCK, 128), 1)
        sink_p = jnp.zeros((ATTN_BLOCK, 128), F32)
        for m in range(N_KV_HEADS):
            ks = slice(HEAD_DIM * m, HEAD_DIM * (m + 1))
            kp, kc, vp, vc = kp_ref[:, ks], kc_ref[:, ks], vp_ref[:, ks], vc_ref[:, ks]
            q4 = _group_rows(q_ref, m)
            raw_p, raw_c = _bdot(q4, kp, NT), _bdot(q4, kc, NT)
            pps, pcs = [], []
            for g in range(Q_PER_KV):
                hh = Q_PER_KV * m + g
                pp, pc, ps = _attn_probs(_head_rows(raw_p, g), _head_rows(raw_c, g), sk_ref[0, hh], ALIBI_SLOPES[hh], masks)
                pps.append(pp.astype(BF16))
                pcs.append(pc.astype(BF16))
                p_ref[:, P_COLS * hh:P_COLS * hh + ATTN_BLOCK] = pps[g]
                p_ref[:, P_COLS * hh + ATTN_BLOCK:P_COLS * (hh + 1)] = pcs[g]
                sink_p = jnp.where(lane == hh, ps, sink_p)
            o4 = _bdot(jnp.concatenate(pps, axis=0), vp, NN) + _bdot(jnp.concatenate(pcs, axis=0), vc, NN)
            for g in range(Q_PER_KV):
                hh = Q_PER_KV * m + g
                o_ref[:, HEAD_DIM * hh:HEAD_DIM * (hh + 1)] = _head_rows(o4, g).astype(o_ref.dtype)
        ps_ref[...] = sink_p

    row = lambda cols: pl.BlockSpec((ATTN_BLOCK, cols), lambda n: (n, 0))
    return pl.pallas_call(
        body, name=name, grid=(nblk,),
        in_specs=_attn_specs(nblk) + [pl.BlockSpec(memory_space=pltpu.SMEM)],
        out_specs=[row(K_COL), row(N_Q_HEADS * P_COLS), row(128)],
        out_shape=[_sds((S, K_COL), BF16), _sds((S, N_Q_HEADS * P_COLS), BF16), _sds((S, 128), F32)],
        compiler_params=_params(("parallel",)),
    )(proj, proj, proj, proj, proj, sinks)


def _attn_bwd(name, proj, probs, sink_probs, do):
    S = proj.shape[0]
    nblk = S // ATTN_BLOCK

    def body(q_ref, kc_ref, kp_ref, vc_ref, vp_ref, do_ref, p_ref, ps_ref, dz_ref, ds_ref, carry, cur, padd):
        n = pl.program_id(0)

        @pl.when(n == 0)
        def _():
            carry[...] = jnp.zeros_like(carry)
            ds_ref[...] = jnp.zeros_like(ds_ref)

        @pl.when(n < nblk)
        def _():
            lane = lax.broadcasted_iota(jnp.int32, (ATTN_BLOCK, 128), 1)
            sink_p = ps_ref[...]
            dsv = jnp.zeros((1, 128), F32)
            for m in range(N_KV_HEADS):
                ks = slice(HEAD_DIM * m, HEAD_DIM * (m + 1))
                kp, kc, vp, vc = kp_ref[:, ks], kc_ref[:, ks], vp_ref[:, ks], vc_ref[:, ks]
                q4, do4 = _group_rows(q_ref, m), _group_rows(do_ref, m)
                dpp4, dpc4 = _bdot(do4, vp, NT), _bdot(do4, vc, NT)
                pps, pcs, dsps, dscs = [], [], [], []
                for g in range(Q_PER_KV):
                    hh = Q_PER_KV * m + g
                    pps.append(p_ref[:, P_COLS * hh:P_COLS * hh + ATTN_BLOCK])
                    pcs.append(p_ref[:, P_COLS * hh + ATTN_BLOCK:P_COLS * (hh + 1)])
                    pp, pc = pps[g].astype(F32), pcs[g].astype(F32)
                    dpp, dpc = _head_rows(dpp4, g), _head_rows(dpc4, g)
                    delta = jnp.sum(pp * dpp, axis=-1, keepdims=True) + jnp.sum(pc * dpc, axis=-1, keepdims=True)
                    dsv = dsv - jnp.sum(jnp.where(lane == hh, sink_p, 0.0) * delta, axis=0, keepdims=True)
                    dsps.append((pp * (dpp - delta)).astype(BF16))
                    dscs.append((pc * (dpc - delta)).astype(BF16))
                pp4, pc4 = jnp.concatenate(pps, axis=0), jnp.concatenate(pcs, axis=0)
                dsp4, dsc4 = jnp.concatenate(dsps, axis=0), jnp.concatenate(dscs, axis=0)
                dq4 = (_bdot(dsp4, kp, NN) + _bdot(dsc4, kc, NN)) * ATTN_SCALE
                for g in range(Q_PER_KV):
                    hh = Q_PER_KV * m + g
                    cur[:, HEAD_DIM * hh:HEAD_DIM * (hh + 1)] = _head_rows(dq4, g)
                cur[:, K_COL + HEAD_DIM * m:K_COL + HEAD_DIM * (m + 1)] = _bdot(dsc4, q4, TN) * ATTN_SCALE
                cur[:, V_COL + HEAD_DIM * m:V_COL + HEAD_DIM * (m + 1)] = _bdot(pc4, do4, TN)
                padd[:, ks] = _bdot(dsp4, q4, TN) * ATTN_SCALE
                padd[:, KV_COLS + HEAD_DIM * m:KV_COLS + HEAD_DIM * (m + 1)] = _bdot(pp4, do4, TN)
            ds_ref[...] += dsv
            dz_ref[:, :K_COL] = carry[:, :K_COL].astype(dz_ref.dtype)
            dz_ref[:, K_COL:] = (carry[:, K_COL:] + padd[...]).astype(dz_ref.dtype)
            carry[...] = cur[...]

        @pl.when(n == nblk)
        def _():
            dz_ref[...] = carry[...].astype(dz_ref.dtype)

    return pl.pallas_call(
        body, name=name, grid=(nblk + 1,),
        in_specs=_attn_specs(nblk) + [
            pl.BlockSpec((ATTN_BLOCK, cols), lambda n: (jnp.minimum(n, nblk - 1), 0))
            for cols in (K_COL, N_Q_HEADS * P_COLS, 128)],
        out_specs=[pl.BlockSpec((ATTN_BLOCK, ATTN_IN), lambda n: (jnp.maximum(n - 1, 0), 0)),
                   pl.BlockSpec((1, 128), lambda n: (0, 0))],
        out_shape=[_sds((S, ATTN_IN), BF16), _sds((1, 128), F32)],
        scratch_shapes=[pltpu.VMEM((ATTN_BLOCK, ATTN_IN), F32), pltpu.VMEM((ATTN_BLOCK, ATTN_IN), F32),
                        pltpu.VMEM((ATTN_BLOCK, 2 * KV_COLS), F32)],
        compiler_params=_params(("arbitrary",)),
    )(proj, proj, proj, proj, proj, do, probs, sink_probs)


def _hg_consts():
    C = HG_CHUNK
    tri = np.tril(np.ones((C, C)))
    t = np.arange(C)
    rows, masks = [tri], []
    for lvl in range(HG_LEVELS):
        n = C >> (lvl + 1)
        sel = np.zeros((C, C))
        sel[t, (t // (2 * n)) * (2 * n) + n - 1] = 1.0
        rows.append(sel @ tri)
        tt, ss = t[:, None], t[None, :]
        masks.append((tt // (2 * n) == ss // (2 * n)) & ((tt // n) % 2 == 1) & ((ss // n) % 2 == 0))
    masks.append(np.eye(C, dtype=bool))
    stk = np.concatenate(rows, axis=0)
    return jnp.asarray(stk, BF16), jnp.asarray(np.stack(masks), F32)


def _sigmoid(x):
    return 1.0 / (1.0 + jnp.exp(-x))


def _split(x, parts):
    out, rest = [], x
    for _ in range(parts):
        out.append(rest.astype(BF16))
        rest = rest - out[-1].astype(F32)
    return out


def _dot01(m01, x, dn, parts=3):
    return sum(lax.dot_general(m01, p, dn, preferred_element_type=F32) for p in _split(x, parts))


def _ref_rows(b, n):
    C = b.shape[1]
    if 2 * n >= 8:
        b3 = b.reshape(HG_CHUNK // (2 * n), 2 * n, C)
        return jnp.broadcast_to(b3[:, n - 1:n, :], b3.shape).reshape(HG_CHUNK, C)
    pos = lax.broadcasted_iota(jnp.int32, b.shape, 0) % (2 * n)
    out = b
    for p in range(2 * n):
        if p != n - 1:
            out = jnp.where(pos == p, pltpu.roll(b, (p - (n - 1)) % HG_CHUNK, 0), out)
    return out


def _hg_common(z_ref, lb_ref, stk_ref):
    qr, fr = z_ref[0], z_ref[1]
    lb = lb_ref[...]
    sq, sg, sgn = _sigmoid(qr), _sigmoid(fr), _sigmoid(-fr)
    ft = lb + (1.0 - lb) * sg
    b = _dot01(stk_ref[0:HG_CHUNK, :], jnp.log(ft), NN)
    ws = [jnp.exp(-jnp.abs(b - _ref_rows(b, HG_CHUNK >> (l + 1)))) for l in range(HG_LEVELS)]
    blast = b[HG_CHUNK - 1:HG_CHUNK]
    return dict(qr=qr, fr=fr, lb=lb, sq=sq, sg=sg, sgn=sgn, ft=ft, q=qr * sq, kk=(1.0 - lb) * sgn, b=b,
                ws=ws, eb=jnp.exp(b), ed=jnp.exp(blast - b), elast=jnp.exp(blast))


def _hg_factors(qh, kh, ws, sl):
    return ([(qh * ws[l][:, sl]).astype(BF16) for l in range(HG_LEVELS)],
            [(kh * ws[l][:, sl]).astype(BF16) for l in range(HG_LEVELS)])


def _hg_intra(qh, kh, ws, msk_ref, sl):
    qls, kls = _hg_factors(qh, kh, ws, sl)
    a = msk_ref[HG_LEVELS] * _bdot(qh, kh, NT)
    for l in range(HG_LEVELS):
        a = a + msk_ref[l] * _bdot(qls[l], kls[l], NT)
    return a


def _hg_fwd(name, z, lb, ng):
    S = z.shape[2]
    nc = S // HG_CHUNK
    stk, msk = _hg_consts()

    def body(z_ref, lb_ref, ng_ref, stk_ref, msk_ref, og_ref, st_ref, a_ref, o_ref, state):
        @pl.when(pl.program_id(1) == 0)
        def _():
            state[...] = jnp.zeros_like(state)

        cm = _hg_common(z_ref, lb_ref, stk_ref)
        v, gt = z_ref[2], z_ref[3]
        kd = cm["kk"] * cm["ed"]
        for hh in range(4):
            sl = slice(HG_K * hh, HG_K * (hh + 1))
            st = state[hh]
            st_ref[hh] = st
            qh, kh, vh = cm["q"][:, sl], cm["kk"][:, sl], v[:, sl]
            a = _hg_intra(qh, kh, cm["ws"], msk_ref, sl).astype(BF16)
            a_ref[hh] = a
            o = _bdot(a, vh, NN) + _bdot(qh * cm["eb"][:, sl], st, NT)
            o_ref[:, sl] = o
            state[hh] = cm["elast"][:, sl] * st + _bdot(vh, kd[:, sl], TN)
            r = lax.rsqrt(jnp.mean(o * o, axis=-1, keepdims=True) + EPS)
            gh = gt[:, sl]
            og_ref[:, sl] = (o * r * ng_ref[...] * (gh * _sigmoid(gh))).astype(og_ref.dtype)

    return pl.pallas_call(
        body, name=name, grid=(2, nc),
        in_specs=[pl.BlockSpec((4, None, HG_CHUNK, HG_SLOT), lambda g, c: (0, g, c, 0)),
                  pl.BlockSpec((1, HG_SLOT), lambda g, c: (0, g)),
                  pl.BlockSpec((1, HG_K), lambda g, c: (0, 0)),
                  pl.BlockSpec(stk.shape, lambda g, c: (0, 0)),
                  pl.BlockSpec(msk.shape, lambda g, c: (0, 0, 0))],
        out_specs=[pl.BlockSpec((HG_CHUNK, HG_SLOT), lambda g, c: (c, g)),
                   pl.BlockSpec((None, 4, HG_K, HG_K), lambda g, c: (c, g, 0, 0)),
                   pl.BlockSpec((None, 4, HG_CHUNK, HG_CHUNK), lambda g, c: (c, g, 0, 0)),
                   pl.BlockSpec((HG_CHUNK, HG_SLOT), lambda g, c: (c, g))],
        out_shape=[_sds((S, D_MODEL), BF16), _sds((nc, HG_HEADS, HG_K, HG_K), F32),
                   _sds((nc, HG_HEADS, HG_CHUNK, HG_CHUNK), BF16), _sds((S, D_MODEL), F32)],
        scratch_shapes=[pltpu.VMEM((4, HG_K, HG_K), F32)],
        compiler_params=_params(("parallel", "arbitrary")),
    )(z, lb, ng, stk, msk)


def _hg_bwd(name, z, lb, ng, states, intra, o_pre, dog):
    S = z.shape[2]
    nc = S // HG_CHUNK
    stk, msk = _hg_consts()

    def body(z_ref, lb_ref, ng_ref, stk_ref, msk_ref, st_ref, a_ref, o_ref, dog_ref, dz_ref, dlb_ref, dng_ref, dstate):
        @pl.when(pl.program_id(1) == 0)
        def _():
            dstate[...] = jnp.zeros_like(dstate)
            dlb_ref[...] = jnp.zeros_like(dlb_ref)
            dng_ref[...] = jnp.zeros_like(dng_ref)

        cm = _hg_common(z_ref, lb_ref, stk_ref)
        v, gt = z_ref[2], z_ref[3]
        ng = ng_ref[...]
        kd = cm["kk"] * cm["ed"]
        row = lax.broadcasted_iota(jnp.int32, (HG_CHUNK, 1), 0)
        dng = jnp.zeros((1, HG_K), F32)
        dq_h, dkk_h, db_h, dv_h, dgt_h = [], [], [], [], []
        dr_h = [[] for _ in range(HG_LEVELS)]
        for hh in range(4):
            sl = slice(HG_K * hh, HG_K * (hh + 1))
            st, dst = st_ref[hh], dstate[hh]
            qh, kh, vh, ebh, edh, kdh = cm["q"][:, sl], cm["kk"][:, sl], v[:, sl], cm["eb"][:, sl], cm["ed"][:, sl], kd[:, sl]
            elh = cm["elast"][:, sl]
            qls, kls = _hg_factors(qh, kh, cm["ws"], sl)
            a, o = a_ref[hh], o_ref[:, sl]
            qe = qh * ebh
            r = lax.rsqrt(jnp.mean(o * o, axis=-1, keepdims=True) + EPS)
            xh = o * r
            gh = gt[:, sl]
            sgg = _sigmoid(gh)
            dog = dog_ref[:, sl].astype(F32)
            dy = dog * (gh * sgg)
            dgt_h.append(dog * (xh * ng) * (sgg * (1.0 + gh * (1.0 - sgg))))
            dng = dng + jnp.sum(dy * xh, axis=0, keepdims=True)
            dyg = dy * ng
            do = r * (dyg - xh * jnp.mean(dyg * xh, axis=-1, keepdims=True))
            da = _bdot(do, vh, NT)
            dv_h.append(_bdot(a, do, TN) + _bdot(kdh, dst, NT))
            dkd = _bdot(vh, dst, NN)
            delast = jnp.sum(st * dst, axis=0, keepdims=True)
            dqe = _bdot(do, st, NN)
            dstate[hh] = elh * dst + _bdot(do, qe, TN)
            gk = dkd * kdh
            dblast = jnp.sum(gk, axis=0, keepdims=True) + delast * elh
            db = dqe * qe - gk + jnp.where(row == HG_CHUNK - 1, dblast, 0.0)
            dp = (msk_ref[HG_LEVELS] * da).astype(BF16)
            dq = dqe * ebh + _bdot(dp, kh, NN)
            dkk = dkd * edh + _bdot(dp, qh, TN)
            for l in range(HG_LEVELS):
                dp = (msk_ref[l] * da).astype(BF16)
                dql, dkl = _bdot(dp, kls[l], NN), _bdot(dp, qls[l], TN)
                w = cm["ws"][l][:, sl]
                dq = dq + dql * w
                dkk = dkk + dkl * w
                half = jnp.where(((row >> (HG_LEVELS - 1 - l)) & 1) == 1, 1.0, -1.0)
                dd = half * w * (dql * qh + dkl * kh)
                db = db + dd
                dr_h[l].append(-dd)
            dq_h.append(dq)
            dkk_h.append(dkk)
            db_h.append(db)
        cat = lambda xs: jnp.concatenate(xs, axis=1)
        cot = jnp.concatenate([cat(db_h)] + [cat(dr_h[l]) for l in range(HG_LEVELS)], axis=0)
        dlf = _dot01(stk_ref[...], cot, TN, parts=2)
        dq, dkk = cat(dq_h), cat(dkk_h)
        dft = dlf / cm["ft"]
        one_lb = 1.0 - cm["lb"]
        dz_ref[0] = (dq * (cm["sq"] * (1.0 + cm["qr"] * (1.0 - cm["sq"])))).astype(dz_ref.dtype)
        dz_ref[1] = ((dft - dkk) * one_lb * cm["sg"] * cm["sgn"]).astype(dz_ref.dtype)
        dz_ref[2] = cat(dv_h).astype(dz_ref.dtype)
        dz_ref[3] = cat(dgt_h).astype(dz_ref.dtype)
        dlb_ref[...] += jnp.sum((dft - dkk) * cm["sgn"], axis=0, keepdims=True)
        dng_ref[...] += dng

    rev = lambda c: nc - 1 - c
    return pl.pallas_call(
        body, name=name, grid=(2, nc),
        in_specs=[pl.BlockSpec((4, None, HG_CHUNK, HG_SLOT), lambda g, c: (0, g, rev(c), 0)),
                  pl.BlockSpec((1, HG_SLOT), lambda g, c: (0, g)),
                  pl.BlockSpec((1, HG_K), lambda g, c: (0, 0)),
                  pl.BlockSpec(stk.shape, lambda g, c: (0, 0)),
                  pl.BlockSpec(msk.shape, lambda g, c: (0, 0, 0)),
                  pl.BlockSpec((None, 4, HG_K, HG_K), lambda g, c: (rev(c), g, 0, 0)),
                  pl.BlockSpec((None, 4, HG_CHUNK, HG_CHUNK), lambda g, c: (rev(c), g, 0, 0)),
                  pl.BlockSpec((HG_CHUNK, HG_SLOT), lambda g, c: (rev(c), g)),
                  pl.BlockSpec((HG_CHUNK, HG_SLOT), lambda g, c: (rev(c), g))],
        out_specs=[pl.BlockSpec((4, None, HG_CHUNK, HG_SLOT), lambda g, c: (0, g, rev(c), 0)),
                   pl.BlockSpec((1, HG_SLOT), lambda g, c: (0, g)),
                   pl.BlockSpec((None, 1, HG_K), lambda g, c: (g, 0, 0))],
        out_shape=[_sds(z.shape, BF16), _sds((1, 2 * HG_SLOT), F32), _sds((2, 1, HG_K), F32)],
        scratch_shapes=[pltpu.VMEM((4, HG_K, HG_K), F32)],
        compiler_params=_params(("parallel", "arbitrary")),
    )(z, lb, ng, stk, msk, states, intra, o_pre, dog)


def _lb_fwd(name, logits):
    def body(l_ref, o_ref):
        x = l_ref[...]
        e = jnp.exp(x - jnp.max(x, axis=0, keepdims=True))
        s = e / jnp.sum(e, axis=0, keepdims=True)
        o_ref[0:1, :] = s[1:2]
        o_ref[1:2, :] = s[1:2] + s[2:3] + s[3:4]

    return pl.pallas_call(body, name=name, out_shape=_sds((2, logits.shape[1]), F32))(logits)


def _lb_bwd(name, logits, dlb):
    def body(l_ref, d_ref, o_ref):
        x = l_ref[...]
        e = jnp.exp(x - jnp.max(x, axis=0, keepdims=True))
        s = e / jnp.sum(e, axis=0, keepdims=True)
        d1, d3 = d_ref[0:1, :], d_ref[1:2, :]
        ds = [jnp.zeros_like(d1), d1 + d3, d3, d3]
        dot = sum(ds[r] * s[r:r + 1] for r in range(1, DEPTH))
        for r in range(DEPTH):
            o_ref[r:r + 1, :] = s[r:r + 1] * (ds[r] - dot)

    return pl.pallas_call(body, name=name, out_shape=_sds(logits.shape, F32))(logits, dlb)


SUB = 8


def _rows_down(x, prev, k):
    row = lax.broadcasted_iota(jnp.int32, x.shape, 0)
    return jnp.where(row >= k, pltpu.roll(x, k, 0), pltpu.roll(prev, k, 0))


def _rows_up(x, nxt, k):
    row = lax.broadcasted_iota(jnp.int32, x.shape, 0)
    return jnp.where(row < SUB - k, pltpu.roll(x, SUB - k, 0), pltpu.roll(nxt, SUB - k, 0))


def _conv_block(w_ref, b_ref, p, x, prev):
    return (b_ref[p] + w_ref[p, 0:1, :] * _rows_down(x, prev, 2) + w_ref[p, 1:2, :] * _rows_down(x, prev, 1)
            + w_ref[p, 2:3, :] * x)


def _convgate_fwd(name, u, cw, cb):
    S = u.shape[2]
    tm = _tile(S, ROW_TILE)

    def body(u_ref, w_ref, b_ref, a_ref, c_ref, halo):
        @pl.when(pl.program_id(1) == 0)
        def _():
            halo[...] = jnp.zeros_like(halo)

        def step(r, prev):
            pg, pv = prev
            out, cgs, cvs = [], [], []
            for s in range(2):
                rows = pl.ds(pl.multiple_of(r * 2 * SUB + s * SUB, SUB), SUB)
                xg, xv = u_ref[0, rows, :], u_ref[1, rows, :]
                cgs.append(_conv_block(w_ref, b_ref, 0, xg, pg))
                cvs.append(_conv_block(w_ref, b_ref, 1, xv, pv))
                out.append(cgs[s] * _sigmoid(cgs[s]) * cvs[s])
                pg, pv = xg, xv
            rows = pl.ds(pl.multiple_of(r * 2 * SUB, 2 * SUB), 2 * SUB)
            a_ref[rows, :] = jnp.concatenate(out, axis=0).astype(a_ref.dtype)
            c_ref[0, rows, :] = jnp.concatenate(cgs, axis=0).astype(c_ref.dtype)
            c_ref[1, rows, :] = jnp.concatenate(cvs, axis=0).astype(c_ref.dtype)
            return pg, pv

        pg, pv = lax.fori_loop(0, tm // (2 * SUB), step, (halo[0], halo[1]))
        halo[0] = pg
        halo[1] = pv

    pair = pl.BlockSpec((2, None, tm, FF_SLOT), lambda j, t: (0, j, t, 0))
    return pl.pallas_call(
        body, name=name, grid=(4, S // tm),
        in_specs=[pair, pl.BlockSpec((2, None, 3, FF_SLOT), lambda j, t: (0, j, 0, 0)),
                  pl.BlockSpec((2, None, 1, FF_SLOT), lambda j, t: (0, j, 0, 0))],
        out_specs=[pl.BlockSpec((None, tm, FF_SLOT), lambda j, t: (j, t, 0)), pair],
        out_shape=[_sds((4, S, FF_SLOT), BF16), _sds(u.shape, BF16)],
        scratch_shapes=[pltpu.VMEM((2, SUB, FF_SLOT), F32)],
        compiler_params=_params(("parallel", "arbitrary")),
    )(u, cw, cb)


def _convgate_bwd(name, u, convs, cw, da):
    S = u.shape[2]
    tm = _tile(S, ROW_TILE)
    nt = S // tm

    def body(u_ref, c_ref, w_ref, da_ref, du_out, dw_ref, db_ref, after, first, acc, du_ref):
        @pl.when(pl.program_id(1) == 0)
        def _():
            after[...] = jnp.zeros_like(after)
            acc[...] = jnp.zeros_like(acc)

        def finish(p, x, d, nxt, rows):
            taps = (_rows_up(d, nxt, 2), _rows_up(d, nxt, 1), d)
            du_ref[p, rows, :] = w_ref[p, 0:1, :] * taps[0] + w_ref[p, 1:2, :] * taps[1] + w_ref[p, 2:3, :] * d
            for j in range(3):
                acc[p, j] += taps[j] * x
            acc[p, 3] += d

        def step(r, carry):
            xg_last, xv_last, dg_last, dv_last = carry
            rows16 = pl.ds(pl.multiple_of(r * 2 * SUB, 2 * SUB), 2 * SUB)
            dav = da_ref[rows16, :].astype(F32)
            cg16, cv16 = c_ref[0, rows16, :].astype(F32), c_ref[1, rows16, :].astype(F32)
            for s in range(2):
                at = r * 2 * SUB + s * SUB
                part = slice(s * SUB, (s + 1) * SUB)
                cg, cv, dab = cg16[part], cv16[part], dav[part]
                sg = _sigmoid(cg)
                dg = dab * cv * (sg * (1.0 + cg * (1.0 - sg)))
                dv = dab * cg * sg
                before = pl.ds(pl.multiple_of(at - SUB, SUB), SUB)
                if s == 0:
                    @pl.when(r == 0)
                    def _():
                        first[0] = dg
                        first[1] = dv

                    @pl.when(r > 0)
                    def _():
                        finish(0, xg_last, dg_last, dg, before)
                        finish(1, xv_last, dv_last, dv, before)
                else:
                    finish(0, xg_last, dg_last, dg, before)
                    finish(1, xv_last, dv_last, dv, before)
                rows = pl.ds(pl.multiple_of(at, SUB), SUB)
                xg_last, xv_last, dg_last, dv_last = u_ref[0, rows, :], u_ref[1, rows, :], dg, dv
            return xg_last, xv_last, dg_last, dv_last

        zero = jnp.zeros((SUB, FF_SLOT), F32)
        xg_last, xv_last, dg_last, dv_last = lax.fori_loop(0, tm // (2 * SUB), step, (zero, zero, zero, zero))
        finish(0, xg_last, dg_last, after[0], slice(tm - SUB, tm))
        finish(1, xv_last, dv_last, after[1], slice(tm - SUB, tm))
        du_out[...] = du_ref[...].astype(du_out.dtype)
        after[...] = first[...]
        for p in range(2):
            for j in range(3):
                dw_ref[p, j:j + 1, :] = jnp.sum(acc[p, j], axis=0, keepdims=True)
            db_ref[p] = jnp.sum(acc[p, 3], axis=0, keepdims=True)

    rev = lambda t: nt - 1 - t
    pair = pl.BlockSpec((2, None, tm, FF_SLOT), lambda j, t: (0, j, rev(t), 0))
    taps = pl.BlockSpec((2, None, 3, FF_SLOT), lambda j, t: (0, j, 0, 0))
    bias = pl.BlockSpec((2, None, 1, FF_SLOT), lambda j, t: (0, j, 0, 0))
    return pl.pallas_call(
        body, name=name, grid=(4, nt),
        in_specs=[pair, pair, taps, pl.BlockSpec((None, tm, FF_SLOT), lambda j, t: (j, rev(t), 0))],
        out_specs=[pair, taps, bias],
        out_shape=[_sds(u.shape, BF16), _sds(cw.shape, F32), _sds((2, 4, 1, FF_SLOT), F32)],
        scratch_shapes=[pltpu.VMEM((2, SUB, FF_SLOT), F32), pltpu.VMEM((2, SUB, FF_SLOT), F32),
                        pltpu.VMEM((2, 4, SUB, FF_SLOT), F32), pltpu.VMEM((2, tm, FF_SLOT), F32)],
        compiler_params=_params(("parallel", "arbitrary")),
    )(u, convs, cw, da)


def _row_tile(R):
    for t in range(256, 15, -16):
        if R % t == 0:
            return t
    return R


def _adamw(name, gsrcs, w, m, v, dep=None):
    L = len(gsrcs)
    n, A, C = gsrcs[0].shape
    tr = _row_tile(A)
    deps = () if dep is None else (dep,)

    def body(*refs):
        g_refs = refs[:L]
        w_ref, m_ref, v_ref = refs[L:L + 3]
        go_ref, d_ref, mo_ref, vo_ref = refs[L + 3 + len(deps):]
        for k in range(L):
            @pl.when(pl.program_id(0) == k)
            def _(k=k):
                g = g_refs[k][0].astype(F32)
                for s in range(1, n):
                    g = g + g_refs[k][s].astype(F32)
                m2 = ADAM_B1 * m_ref[...] + (1.0 - ADAM_B1) * g
                v2 = ADAM_B2 * v_ref[...] + (1.0 - ADAM_B2) * (g * g)
                m_hat = m2 / (1.0 - ADAM_B1 ** ADAM_STEP)
                v_hat = v2 / (1.0 - ADAM_B2 ** ADAM_STEP)
                go_ref[...] = g
                d_ref[...] = -ADAM_LR * (m_hat / (jnp.sqrt(v_hat) + ADAM_EPS) + ADAM_WD * w_ref[...])
                mo_ref[...] = m2
                vo_ref[...] = v2

    g_specs = [pl.BlockSpec((n, tr, C), lambda l, i, k=k: (0, jnp.where(l == k, i, 0), 0)) for k in range(L)]
    blk = pl.BlockSpec((None, tr, C), lambda l, i: (l, i, 0))
    return pl.pallas_call(
        body, name=name, grid=(L, A // tr), in_specs=g_specs + [blk, blk, blk] + [_dep_spec(2)] * len(deps),
        out_specs=[blk] * 4, out_shape=[_sds((L, A, C), F32)] * 4, compiler_params=_params(("parallel", "parallel")),
    )(*gsrcs, w, m, v, *deps)


MESH = pl.DeviceIdType.MESH
HBM_SPEC = pl.BlockSpec(memory_space=pltpu.HBM)
N_PEERS = N_DEV - 1


def _mesh_place():
    x, y, c = lax.axis_index("x"), lax.axis_index("y"), lax.axis_index("c")
    peers = []
    for p in range(1, N_DEV):
        px = 1 - x if p & 4 else x
        py = 1 - y if p & 2 else y
        pc = 1 - c if p & 1 else c
        peers.append(((px, py, pc), 4 * px + 2 * py + pc))
    return 4 * x + 2 * y + c, peers


SEM_SPEC = pl.BlockSpec(memory_space=pltpu.SEMAPHORE)
ANY_SPEC = pl.BlockSpec(memory_space=pl.ANY)
EFFECT = pltpu.SideEffectType.DATAFLOW_SIDE_EFFECTING


def _exchange_refs(scatter, src, land, send, recv, k, p, dev, idx, me):
    return pltpu.make_async_remote_copy(src_ref=src[k].at[idx] if scatter else src[k], dst_ref=land[k].at[me],
                                        send_sem=send.at[k * N_PEERS + p], recv_sem=recv.at[k * N_PEERS + p], device_id=dev,
                                        device_id_type=MESH)


def _exchange_start(name, srcs, scatter, gate):
    n = len(srcs)
    lands = [lax.empty(s.shape if scatter else (N_DEV,) + s.shape, s.dtype) for s in srcs]

    def body(*refs):
        src, land = refs[:n], refs[n:2 * n]
        send, recv, own = refs[2 * n + 1:2 * n + 4]
        token = refs[-1]
        me, peers = _mesh_place()
        for k in range(n):
            pltpu.make_async_copy(src[k].at[me] if scatter else src[k], land[k].at[me], own.at[k]).start()
            for p, (dev, idx) in enumerate(peers):
                _exchange_refs(scatter, src, land, send, recv, k, p, dev, idx, me).start()
        token[...] = jnp.zeros_like(token)

    hbm = lambda a: pltpu.HBM(a.shape, a.dtype)
    outs = pl.pallas_call(
        body, name=name,
        out_shape=(pltpu.SemaphoreType.DMA((n * N_PEERS,)), pltpu.SemaphoreType.DMA((n * N_PEERS,)),
                   pltpu.SemaphoreType.DMA((n,)), *[hbm(s) for s in srcs], *[hbm(s) for s in lands], _sds(DEP_SHAPE, F32)),
        in_specs=[HBM_SPEC] * (2 * n) + [ANY_SPEC],
        out_specs=(SEM_SPEC, SEM_SPEC, SEM_SPEC, *[HBM_SPEC] * (2 * n), pl.BlockSpec(memory_space=pltpu.VMEM)),
        input_output_aliases={j: 3 + j for j in range(2 * n)},
        compiler_params=pltpu.CompilerParams(has_side_effects=EFFECT),
    )(*[pltpu.with_memory_space_constraint(s, pltpu.HBM) for s in srcs],
      *[pltpu.with_memory_space_constraint(s, pltpu.HBM) for s in lands], gate)
    return outs[:3], None, list(outs[3:3 + n]), list(outs[3 + n:3 + 2 * n]), outs[-1]


def _exchange_wait(name, started, scatter, after):
    (send, recv, own), _, srcs, lands, _ = started
    n = len(srcs)

    def body(*refs):
        src, land = refs[:n], refs[n:2 * n]
        send, recv, own = refs[2 * n:2 * n + 3]
        me, peers = _mesh_place()
        for k in range(n):
            pltpu.make_async_copy(src[k].at[me] if scatter else src[k], land[k].at[me], own.at[k]).wait()
            for p, (dev, idx) in enumerate(peers):
                cp = pltpu.make_async_remote_copy(src_ref=src[k].at[idx] if scatter else src[k], dst_ref=land[k].at[idx],
                                                  send_sem=send.at[k * N_PEERS + p], recv_sem=recv.at[k * N_PEERS + p], device_id=dev,
                                                  device_id_type=MESH)
                cp.wait_send()
                cp.wait_recv()

    hbm = lambda a: pltpu.HBM(a.shape, a.dtype)
    outs = pl.pallas_call(
        body, name=name, out_shape=(*[hbm(s) for s in srcs], *[hbm(s) for s in lands]),
        in_specs=[HBM_SPEC] * (2 * n) + [SEM_SPEC, SEM_SPEC, SEM_SPEC, ANY_SPEC], out_specs=tuple([HBM_SPEC] * (2 * n)),
        input_output_aliases={j: j for j in range(2 * n)},
        compiler_params=pltpu.CompilerParams(has_side_effects=EFFECT),
    )(*srcs, *lands, send, recv, own, after)
    return list(outs[n:])


def _sum_devices(name, parts):
    def body(p_ref, o_ref):
        tot = p_ref[0]
        for j in range(1, N_DEV):
            tot = tot + p_ref[j]
        o_ref[...] = tot

    return pl.pallas_call(body, name=name, out_shape=_sds(parts.shape[1:], F32),
                          compiler_params=pltpu.CompilerParams(vmem_limit_bytes=VMEM_LIMIT))(parts)


def _rows(a, width=D_MODEL):
    flat = a.reshape(-1)
    return jnp.pad(flat, (0, (-flat.shape[0]) % width)).reshape(-1, width)


def _pack_rows(parts):
    blocks = []
    for p in parts:
        r = _rows(p)
        blocks.append(jnp.pad(r, ((0, (-r.shape[0]) % 8), (0, 0))))
    return jnp.concatenate(blocks, axis=0)


def _unpack_rows(rows, shapes):
    out, at = [], 0
    for s in shapes:
        size = int(np.prod(s))
        n = -(-size // D_MODEL)
        out.append(rows[at:at + n].reshape(-1)[:size].reshape(s))
        at += -(-n // 8) * 8
    return out


def kernel(x, norm_mix, norm_ffn, norm_final, attn_w_in, attn_w_out, attn_sinks, hgrn_w_in, hgrn_w_out, hgrn_norm, hgrn_lb_logits, ffn_w_up, ffn_conv_w, ffn_conv_b, ffn_w_down, loss_target, m_norm_mix, m_norm_ffn, m_norm_final, m_attn_w_in, m_attn_w_out, m_attn_sinks, m_hgrn_w_in, m_hgrn_w_out, m_hgrn_norm, m_hgrn_lb_logits, m_ffn_w_up, m_ffn_conv_w, m_ffn_conv_b, m_ffn_w_down, v_norm_mix, v_norm_ffn, v_norm_final, v_attn_w_in, v_attn_w_out, v_attn_sinks, v_hgrn_w_in, v_hgrn_w_out, v_hgrn_norm, v_hgrn_lb_logits, v_ffn_w_up, v_ffn_conv_w, v_ffn_conv_b, v_ffn_w_down):
    S = x.shape[1]
    n_attn, n_hgrn = attn_w_in.shape[0], hgrn_w_in.shape[0]
    me = 4 * lax.axis_index("x") + 2 * lax.axis_index("y") + lax.axis_index("c")

    wa_in_t, wa_out_b = attn_w_in.transpose(0, 2, 1).astype(BF16), attn_w_out.astype(BF16)
    wh_in_b, wh_out_b = hgrn_w_in.astype(BF16), hgrn_w_out.astype(BF16)
    wf_up_b, wf_down_b = ffn_w_up.transpose(0, 2, 1).astype(BF16), ffn_w_down.astype(BF16)
    conv_b = ffn_conv_b.reshape(DEPTH, 2, 4, 1, FF_SLOT)
    lb = _lb_fwd("lb_fwd", hgrn_lb_logits)

    def unit_shards(l, part):
        if part == "ffn":
            return [wf_up_b[l], wf_down_b[l], ffn_conv_w[l]]
        return [wa_in_t[l // 2], wa_out_b[l // 2]] if l % 2 == 0 else [wh_in_b[l // 2], wh_out_b[l // 2]]

    def unit_weights(l, part, w):
        if part == "ffn":
            return w[0][None], w[1].reshape(1, 4, FF_SLOT, D_MODEL), w[2].reshape(2, 4, 3, FF_SLOT)
        if l % 2 == 0:
            return w[0].reshape(1, ATTN_IN, D_MODEL), w[1].reshape(1, D_MODEL, D_MODEL)
        return w[0][None], w[1].reshape(1, D_MODEL, D_MODEL)

    units = [(l, part) for l in range(DEPTH) for part in ("mix", "ffn")]
    gathers = [_exchange_start("gather_start0", unit_shards(*units[0]), False, norm_final)]
    gathers.append(_exchange_start("gather_start1", unit_shards(*units[1]), False, gathers[0][4]))
    arrived = _exchange_wait("gather_wait0", gathers[0], False, gathers[1][4])
    weights, saved = {}, [dict() for _ in range(DEPTH)]
    h = x[0]
    hn = _rmsnorm_fwd("norm_mix_fwd0", h, norm_mix[0:1])
    for n, (l, part) in enumerate(units):
        i, sv = l // 2, saved[l]
        weights[l, part] = w = unit_weights(l, part, arrived)
        dep = None
        if n + 2 < len(units):
            gathers.append(_exchange_start(f"gather_start{n + 2}", unit_shards(*units[n + 2]), False, arrived[0]))
            dep = gathers[n + 2][4]
        if part == "mix":
            sv["h"], sv["hn"] = h, hn
            if l % 2 == 0:
                sv["proj"] = _proj_rows(f"attn_proj{i}", hn, w[0], 0, BF16, dep)
                sv["o"], *sv["kept"] = _attn_fwd(f"attn_fwd{i}", sv["proj"], attn_sinks[i:i + 1])
                h, hn = _out_proj(f"attn_out{i}", sv["o"], w[1], 0, h, norm_ffn[l:l + 1])
            else:
                sv["z"] = _proj_slots(f"hgrn_proj{i}", hn, w[0], 0, dep=dep).reshape(4, 2, S, HG_SLOT)
                sv["o"], *sv["kept"] = _hg_fwd(f"hgrn_fwd{i}", sv["z"], lb[i:i + 1], hgrn_norm[i:i + 1])
                h, hn = _out_proj(f"hgrn_out{i}", sv["o"], w[1], 0, h, norm_ffn[l:l + 1])
        else:
            sv["h2"], sv["hn2"] = h, hn
            sv["u"] = _proj_slots(f"ffn_up{l}", hn, w[0], 0, True, dep).reshape(2, 4, S, FF_SLOT)
            sv["a"], sv["convs"] = _convgate_fwd(f"ffn_gate{l}", sv["u"], w[2], conv_b[l])
            if l + 1 < DEPTH:
                h, hn = _down_proj(f"ffn_down{l}", sv["a"], w[1], 0, h, norm_mix[l + 1:l + 2])
            else:
                h = _down_proj(f"ffn_down{l}", sv["a"], w[1], 0, h)
        if n + 1 < len(units):
            arrived = _exchange_wait(f"gather_wait{n + 1}", gathers[n + 1], False, h)
    dh, d_norm_final, loss_rows = _loss_head("loss_head", h, norm_final[None], loss_target[0])

    d_conv_w, d_conv_b, d_norm_mix, d_norm_ffn = [None] * DEPTH, [None] * DEPTH, [None] * DEPTH, [None] * DEPTH
    d_sinks, d_lb, d_hgrn_norm = [None] * n_attn, [None] * n_hgrn, [None] * n_hgrn
    received, pending = {}, []
    for l, part in reversed(units):
        i, sv, w = l // 2, saved[l], weights[l, part]
        dep = pending[-1][1][4] if pending else None
        if part == "ffn":
            da = _dgrad_down(f"ffn_down_dgrad{l}", dh, w[1], 0, dep)
            g_down = _wgrad_down(f"ffn_down_wgrad{l}", sv["a"], dh).reshape(N_DEV, D_FF // N_DEV, D_MODEL)
            du, d_conv_w[l], d_conv_b[l] = _convgate_bwd(f"ffn_gate_bwd{l}", sv["u"], sv["convs"], w[2], da)
            du = du.reshape(N_DEV, S, FF_SLOT)
            grads = [_wgrad_slots(f"ffn_up_wgrad{l}", sv["hn2"], du, True), g_down]
            dh, d_norm_ffn[l] = _dgrad_slots(f"ffn_up_dgrad{l}", du, w[0], 0, (sv["h2"], norm_ffn[l:l + 1], dh), True)
        else:
            if l % 2 == 0:
                do = _dgrad_out(f"attn_out_dgrad{i}", dh, w[1], 0, BF16, dep)
                g_out = _wgrad_rows(f"attn_out_wgrad{i}", sv["o"], dh)
                dproj, d_sinks[i] = _attn_bwd(f"attn_bwd{i}", sv["proj"], *sv["kept"], do)
                g_in = _wgrad_rows(f"attn_proj_wgrad{i}", dproj, sv["hn"]).reshape(N_DEV, ATTN_IN // N_DEV, D_MODEL)
                dh_new = _dgrad_rows(f"attn_proj_dgrad{i}", dproj, w[0], 0, (sv["h"], norm_mix[l:l + 1], dh))
            else:
                dog = _dgrad_out(f"hgrn_out_dgrad{i}", dh, w[1], 0, F32, dep)
                g_out = _wgrad_rows(f"hgrn_out_wgrad{i}", sv["o"], dh)
                dz, d_lb[i], dng = _hg_bwd(f"hgrn_bwd{i}", sv["z"], lb[i:i + 1], hgrn_norm[i:i + 1], *sv["kept"], dog)
                d_hgrn_norm[i] = dng[0] + dng[1]
                dz = dz.reshape(N_DEV, S, HG_SLOT)
                g_in = _wgrad_slots(f"hgrn_proj_wgrad{i}", sv["hn"], dz)
                dh_new = _dgrad_slots(f"hgrn_proj_dgrad{i}", dz, w[0], 0, (sv["h"], norm_mix[l:l + 1], dh))
            grads = [g_in, g_out.reshape(N_DEV, D_MODEL // N_DEV, D_MODEL)]
            dh, d_norm_mix[l] = dh_new
        gate = dh
        if len(pending) == 2:
            key, oldest = pending.pop(0)
            received[key] = _exchange_wait(f"scatter_wait_{key[1]}{key[0]}", oldest, True, dh)
            gate = received[key][0]
        pending.append(((l, part), _exchange_start(f"scatter_start_{part}{l}", grads, True, gate)))
    grad_x = dh[None]

    small_shapes = [(DEPTH, D_MODEL), (DEPTH, D_MODEL), (1, D_MODEL), (1, D_MODEL), (n_hgrn, D_MODEL), (n_attn, 128),
                    (n_hgrn, HG_K), (DEPTH, 2 * D_FF), (DEPTH, N_DEV, 3, FF_SLOT)]
    partial = _pack_rows([
        jnp.concatenate(d_norm_mix), jnp.concatenate(d_norm_ffn), d_norm_final, loss_rows, jnp.concatenate(d_lb),
        jnp.concatenate(d_sinks), jnp.concatenate(d_hgrn_norm), jnp.stack(d_conv_b), jnp.stack(d_conv_w)])
    small_started = _exchange_start("small_start", [partial], False, pending[-1][1][4])
    attn_layers, hgrn_layers = range(0, DEPTH, 2), range(1, DEPTH, 2)

    def transposed(ts):
        return [t.transpose(0, 2, 1) for t in ts]

    big = {"hgrn_w_in": _adamw("adamw_hgrn_in", [received[l, "mix"][0] for l in hgrn_layers], hgrn_w_in, m_hgrn_w_in,
                               v_hgrn_w_in, dep=small_started[4])}
    big["hgrn_w_out"] = _adamw("adamw_hgrn_out", [received[l, "mix"][1] for l in hgrn_layers], hgrn_w_out, m_hgrn_w_out, v_hgrn_w_out)
    key, oldest = pending.pop(0)
    received[key] = _exchange_wait(f"scatter_wait_{key[1]}{key[0]}", oldest, True, big["hgrn_w_in"][3])
    up_t = _adamw("adamw_ffn_up", [received[l, "ffn"][0] for l in range(DEPTH)], *transposed((ffn_w_up, m_ffn_w_up, v_ffn_w_up)))
    big["ffn_w_up"] = transposed(up_t)
    big["ffn_w_down"] = _adamw("adamw_ffn_down", [received[l, "ffn"][1] for l in range(DEPTH)], ffn_w_down, m_ffn_w_down, v_ffn_w_down)
    key, oldest = pending.pop(0)
    received[key] = _exchange_wait(f"scatter_wait_{key[1]}{key[0]}", oldest, True, up_t[3])
    total = _sum_devices("sum_small", _exchange_wait("small_wait", small_started, False, up_t[3])[0])
    (g_norm_mix, g_norm_ffn, g_norm_final, loss_sum, g_lb, g_sinks, g_hgrn_norm, g_conv_b, g_conv_w_all) = _unpack_rows(
        total, small_shapes)

    loss = jnp.sum(loss_sum)
    g_norm_final = g_norm_final[0]
    g_sinks = g_sinks[:, :N_Q_HEADS]
    g_lb_logits = _lb_bwd("lb_bwd", hgrn_lb_logits, g_lb)
    g_conv_w = lax.dynamic_index_in_dim(g_conv_w_all, me, axis=1, keepdims=False)

    big.update({
        "attn_w_in": transposed(_adamw("adamw_attn_in", [received[l, "mix"][0] for l in attn_layers],
                                       *transposed((attn_w_in, m_attn_w_in, v_attn_w_in)))),
        "attn_w_out": _adamw("adamw_attn_out", [received[l, "mix"][1] for l in attn_layers], attn_w_out, m_attn_w_out, v_attn_w_out),
        "ffn_conv_w": _adamw("adamw_conv_w", [g_conv_w[l][None] for l in range(DEPTH)], ffn_conv_w, m_ffn_conv_w, v_ffn_conv_w),
    })
    small_w = [norm_mix, norm_ffn, norm_final, attn_sinks, hgrn_norm, hgrn_lb_logits, ffn_conv_b]
    small_m = [m_norm_mix, m_norm_ffn, m_norm_final, m_attn_sinks, m_hgrn_norm, m_hgrn_lb_logits, m_ffn_conv_b]
    small_v = [v_norm_mix, v_norm_ffn, v_norm_final, v_attn_sinks, v_hgrn_norm, v_hgrn_lb_logits, v_ffn_conv_b]
    small_g = [g_norm_mix, g_norm_ffn, g_norm_final, g_sinks, g_hgrn_norm, g_lb_logits, g_conv_b]
    outs = _adamw("adamw_small", [_pack_rows(small_g)[None]], *[_pack_rows(t)[None] for t in (small_w, small_m, small_v)])
    outs = [o[0] for o in outs]
    shapes = [w.shape for w in small_w]
    small = {n: [t[j] for t in [_unpack_rows(o, shapes) for o in outs]]
             for j, n in enumerate(["norm_mix", "norm_ffn", "norm_final", "attn_sinks", "hgrn_norm", "hgrn_lb_logits", "ffn_conv_b"])}
    order = ["norm_mix", "norm_ffn", "norm_final", "attn_w_in", "attn_w_out", "attn_sinks", "hgrn_w_in", "hgrn_w_out",
             "hgrn_norm", "hgrn_lb_logits", "ffn_w_up", "ffn_conv_w", "ffn_conv_b", "ffn_w_down"]
    res = {**big, **small}
    return (loss, grad_x, *[res[n][0] for n in order], *[res[n][1] for n in order], *[res[n][2] for n in order],
            *[res[n][3] for n in order])
```

```python
import numpy as np
import jax
import jax.numpy as jnp
from jax import lax
from jax.experimental import pallas as pl
from jax.experimental.pallas import tpu as pltpu

F32 = jnp.float32
BF16 = jnp.bfloat16

D_MODEL = 1024
DEPTH = 4
HEAD_DIM = 64
N_Q_HEADS = 16
N_KV_HEADS = 4
Q_PER_KV = 4
ATTN_BLOCK = 128
ATTN_IN = 1536
HG_HEADS = 8
HG_K = 128
HG_CHUNK = 64
HG_IN = 4096
D_FF = 2816
EPS = 1e-6
N_DEV = 8
FF_SLOT = 2 * D_FF // N_DEV
HG_SLOT = HG_IN // N_DEV
HG_LEVELS = 6

ADAM_LR = 0.001
ADAM_B1 = 0.9
ADAM_B2 = 0.999
ADAM_EPS = 1e-08
ADAM_WD = 0.01
ADAM_STEP = 10

VMEM_LIMIT = 56 * 1024 * 1024
ROW_TILE = 1024
WIDE_ROW_TILE = 2048
NEG_BIG = -1e30

NN = (((1,), (0,)), ((), ()))
NT = (((1,), (1,)), ((), ()))
TN = (((0,), (0,)), ((), ()))


def _bdot(a, b, dn):
    return lax.dot_general(a.astype(BF16), b.astype(BF16), dn, preferred_element_type=F32)


def _sds(shape, dtype):
    return jax.ShapeDtypeStruct(tuple(shape), dtype)


def _params(sem):
    return pltpu.CompilerParams(dimension_semantics=sem, vmem_limit_bytes=VMEM_LIMIT)


DEP_SHAPE = (8, 128)


def _dep_spec(rank):
    return pl.BlockSpec(DEP_SHAPE, lambda *_: (0, 0))


def _matmul(name, a, b, *, dn, grid, a_spec, b_spec, o_spec, out_shape, acc_shape=None, extra=(), extra_specs=(),
            finish=None, dep=None, sem=("parallel", "parallel", "arbitrary")):
    nk = grid[2]
    many = isinstance(out_shape, (list, tuple))
    n_in = 2 + len(extra) + (dep is not None)
    n_out = len(out_shape) if many else 1

    def body(*refs):
        a_ref, b_ref = refs[0], refs[1]
        outs = refs[n_in:n_in + n_out]

        def prod():
            return _bdot(a_ref[...], b_ref[...], dn)

        def done(v):
            if finish is None:
                outs[0][...] = v.astype(outs[0].dtype)
            else:
                finish(v, refs[2:2 + len(extra)], outs)

        if nk == 1:
            done(prod())
        else:
            acc = refs[-1]
            k = pl.program_id(2)

            @pl.when(k == 0)
            def _():
                acc[...] = prod()

            @pl.when(k > 0)
            def _():
                acc[...] += prod()

            @pl.when(k == nk - 1)
            def _():
                done(acc[...])

    in_specs = [a_spec, b_spec, *extra_specs] + ([_dep_spec(3)] if dep is not None else [])
    args = (a, b, *extra) + ((dep,) if dep is not None else ())
    scratch = [] if nk == 1 else [pltpu.VMEM(acc_shape, F32)]
    return pl.pallas_call(
        body, name=name, grid=grid, in_specs=in_specs, out_specs=o_spec, out_shape=out_shape,
        scratch_shapes=scratch, compiler_params=_params(sem),
    )(*args)


def _rms(x):
    return lax.rsqrt(jnp.mean(x * x, axis=-1, keepdims=True) + EPS)


def _residual_finish(v, ex, outs):
    h = v + ex[0][...]
    outs[0][...] = h
    if len(ex) > 1:
        outs[1][...] = (h * _rms(h) * ex[1][...]).astype(outs[1].dtype)


def _norm_bwd_finish(v, ex, outs):
    x = ex[0][...]
    r = _rms(x)
    xh = x * r
    dyg = v * ex[1][...]
    outs[0][...] = ex[2][...] + r * (dyg - xh * jnp.mean(dyg * xh, axis=-1, keepdims=True))
    part = jnp.sum(v * xh, axis=0, keepdims=True)

    @pl.when(pl.program_id(0) == 0)
    def _():
        outs[1][...] = part

    @pl.when(pl.program_id(0) > 0)
    def _():
        outs[1][...] += part


def _row_io(tm, norm_g):
    row = pl.BlockSpec((tm, D_MODEL), lambda i, j, k: (i, 0))
    vec = pl.BlockSpec((1, D_MODEL), lambda i, j, k: (0, 0))
    if norm_g is None:
        return (row,), row, lambda S: _sds((S, D_MODEL), F32)
    return (row, vec), [row, row], lambda S: [_sds((S, D_MODEL), F32), _sds((S, D_MODEL), BF16)]


def _tile(n, t):
    return min(n, t)


def _proj_rows(name, hn, wt, l, out_dtype, dep=None):
    S, N = hn.shape[0], wt.shape[1]
    tm, tn = _tile(S, ROW_TILE), 512
    return _matmul(
        name, hn, wt, dn=NT, grid=(S // tm, N // tn, 1),
        a_spec=pl.BlockSpec((tm, D_MODEL), lambda i, j, k: (i, 0)),
        b_spec=pl.BlockSpec((None, tn, D_MODEL), lambda i, j, k: (l, j, 0)),
        o_spec=pl.BlockSpec((tm, tn), lambda i, j, k: (i, j)),
        out_shape=_sds((S, N), out_dtype), dep=dep)


def _slot_weight(w, transposed):
    if transposed:
        return w.shape[2], (None, None, w.shape[2], D_MODEL), NT, NN
    return w.shape[3], (None, None, D_MODEL, w.shape[3]), NN, NT


def _proj_slots(name, hn, w, l, transposed=False, dep=None):
    S = hn.shape[0]
    r, blk, dn, _ = _slot_weight(w, transposed)
    tm = _tile(S, WIDE_ROW_TILE)
    return _matmul(
        name, hn, w, dn=dn, grid=(N_DEV, S // tm, 1),
        a_spec=pl.BlockSpec((tm, D_MODEL), lambda j, i, k: (i, 0)),
        b_spec=pl.BlockSpec(blk, lambda j, i, k: (l, j, 0, 0)),
        o_spec=pl.BlockSpec((None, tm, r), lambda j, i, k: (j, i, 0)),
        out_shape=_sds((N_DEV, S, r), F32), dep=dep)


def _out_proj(name, o, w, l, h, norm_g=None):
    S, K = o.shape
    tm = _tile(S, ROW_TILE)
    extra_specs, o_spec, out_shape = _row_io(tm, norm_g)
    return _matmul(
        name, o, w, dn=NN, grid=(S // tm, 1, 1),
        a_spec=pl.BlockSpec((tm, K), lambda i, j, k: (i, 0)),
        b_spec=pl.BlockSpec((None, K, D_MODEL), lambda i, j, k: (l, 0, 0)),
        o_spec=o_spec, out_shape=out_shape(S), extra=(h,) if norm_g is None else (h, norm_g),
        extra_specs=extra_specs, finish=_residual_finish)


def _down_proj(name, a, w, l, h, norm_g=None):
    nj, S, r = a.shape
    tm = _tile(S, ROW_TILE)
    extra_specs, o_spec, out_shape = _row_io(tm, norm_g)
    return _matmul(
        name, a, w, dn=NN, grid=(S // tm, 1, nj),
        a_spec=pl.BlockSpec((None, tm, r), lambda i, j, k: (k, i, 0)),
        b_spec=pl.BlockSpec((None, None, r, D_MODEL), lambda i, j, k: (l, k, 0, 0)),
        o_spec=o_spec, out_shape=out_shape(S), acc_shape=(tm, D_MODEL),
        extra=(h,) if norm_g is None else (h, norm_g), extra_specs=extra_specs, finish=_residual_finish)


def _dgrad_down(name, dh, w, l, dep=None):
    S = dh.shape[0]
    nj, r = w.shape[1], w.shape[2]
    tm = _tile(S, ROW_TILE)
    return _matmul(
        name, dh, w, dn=NT, grid=(nj, S // tm, 1),
        a_spec=pl.BlockSpec((tm, D_MODEL), lambda j, i, k: (i, 0)),
        b_spec=pl.BlockSpec((None, None, r, D_MODEL), lambda j, i, k: (l, j, 0, 0)),
        o_spec=pl.BlockSpec((None, tm, r), lambda j, i, k: (j, i, 0)),
        out_shape=_sds((nj, S, r), BF16), dep=dep)


def _wgrad_down(name, a, dh):
    nj, S, r = a.shape
    tk = _tile(S, ROW_TILE)
    return _matmul(
        name, a, dh, dn=TN, grid=(nj, 1, S // tk),
        a_spec=pl.BlockSpec((None, tk, r), lambda s, j, k: (s, k, 0)),
        b_spec=pl.BlockSpec((tk, D_MODEL), lambda s, j, k: (k, 0)),
        o_spec=pl.BlockSpec((None, r, D_MODEL), lambda s, j, k: (s, 0, 0)),
        out_shape=_sds((nj, r, D_MODEL), BF16), acc_shape=(r, D_MODEL))


def _norm_bwd_io(tm, S):
    row = pl.BlockSpec((tm, D_MODEL), lambda i, j, k: (i, 0))
    vec = pl.BlockSpec((1, D_MODEL), lambda i, j, k: (0, 0))
    return dict(extra_specs=(row, vec, row), o_spec=[row, vec], out_shape=[_sds((S, D_MODEL), F32), _sds((1, D_MODEL), F32)],
                finish=_norm_bwd_finish, sem=("arbitrary", "arbitrary", "arbitrary"))


def _dgrad_slots(name, dz, w, l, norm, transposed=False):
    nj, S, r = dz.shape
    _, blk, _, dn = _slot_weight(w, transposed)
    tm = _tile(S, ROW_TILE)
    return _matmul(
        name, dz, w, dn=dn, grid=(S // tm, 1, nj),
        a_spec=pl.BlockSpec((None, tm, r), lambda i, j, k: (k, i, 0)),
        b_spec=pl.BlockSpec(blk, lambda i, j, k: (l, k, 0, 0)),
        acc_shape=(tm, D_MODEL), extra=norm, **_norm_bwd_io(tm, S))


def _wgrad_slots(name, hn, dz, transposed=False):
    nj, S, r = dz.shape
    tk = _tile(S, ROW_TILE)
    hn_spec = pl.BlockSpec((tk, D_MODEL), lambda s, j, k: (k, 0))
    dz_spec = pl.BlockSpec((None, tk, r), lambda s, j, k: (s, k, 0))
    if transposed:
        return _matmul(
            name, dz, hn, dn=TN, grid=(nj, 1, S // tk), a_spec=dz_spec, b_spec=hn_spec,
            o_spec=pl.BlockSpec((None, r, D_MODEL), lambda s, j, k: (s, 0, 0)),
            out_shape=_sds((nj, r, D_MODEL), BF16), acc_shape=(r, D_MODEL))
    return _matmul(
        name, hn, dz, dn=TN, grid=(nj, 1, S // tk), a_spec=hn_spec, b_spec=dz_spec,
        o_spec=pl.BlockSpec((None, D_MODEL, r), lambda s, j, k: (s, 0, 0)),
        out_shape=_sds((nj, D_MODEL, r), BF16), acc_shape=(D_MODEL, r))


def _dgrad_out(name, dh, w, l, out_dtype, dep=None):
    S, K = dh.shape[0], w.shape[1]
    tm = _tile(S, ROW_TILE)
    return _matmul(
        name, dh, w, dn=NT, grid=(S // tm, 1, 1),
        a_spec=pl.BlockSpec((tm, D_MODEL), lambda i, j, k: (i, 0)),
        b_spec=pl.BlockSpec((None, K, D_MODEL), lambda i, j, k: (l, 0, 0)),
        o_spec=pl.BlockSpec((tm, K), lambda i, j, k: (i, 0)),
        out_shape=_sds((S, K), out_dtype), dep=dep)


def _wgrad_rows(name, a, b):
    S, K = a.shape
    tk = _tile(S, ROW_TILE)
    return _matmul(
        name, a, b, dn=TN, grid=(1, 1, S // tk),
        a_spec=pl.BlockSpec((tk, K), lambda i, j, k: (k, 0)),
        b_spec=pl.BlockSpec((tk, D_MODEL), lambda i, j, k: (k, 0)),
        o_spec=pl.BlockSpec((K, D_MODEL), lambda i, j, k: (0, 0)),
        out_shape=_sds((K, D_MODEL), BF16), acc_shape=(K, D_MODEL))


def _dgrad_rows(name, dz, wt, l, norm):
    S, N = dz.shape
    tm = _tile(S, ROW_TILE)
    return _matmul(
        name, dz, wt, dn=NN, grid=(S // tm, 1, 1),
        a_spec=pl.BlockSpec((tm, N), lambda i, j, k: (i, 0)),
        b_spec=pl.BlockSpec((None, N, D_MODEL), lambda i, j, k: (l, 0, 0)),
        extra=norm, **_norm_bwd_io(tm, S))


def _rmsnorm_fwd(name, h, g):
    S = h.shape[0]
    tm = _tile(S, ROW_TILE)

    def body(h_ref, g_ref, o_ref):
        x = h_ref[...]
        o_ref[...] = (x * _rms(x) * g_ref[...]).astype(o_ref.dtype)

    row = pl.BlockSpec((tm, D_MODEL), lambda i: (i, 0))
    return pl.pallas_call(
        body, name=name, grid=(S // tm,), in_specs=[row, pl.BlockSpec((1, D_MODEL), lambda i: (0, 0))],
        out_specs=row, out_shape=_sds((S, D_MODEL), BF16), compiler_params=_params(("parallel",)),
    )(h, g)


def _loss_head(name, h, g, target):
    S = h.shape[0]
    tm = _tile(S, ROW_TILE)

    def body(h_ref, g_ref, t_ref, dh_ref, dg_ref, ls_ref):
        x = h_ref[...]
        r = lax.rsqrt(jnp.mean(x * x, axis=-1, keepdims=True) + EPS)
        xh = x * r
        diff = xh * g_ref[...] - t_ref[...]
        dyf = diff * (1.0 / D_MODEL)
        dyg = dyf * g_ref[...]
        dh_ref[...] = r * (dyg - xh * jnp.mean(dyg * xh, axis=-1, keepdims=True))
        part = jnp.sum(dyf * xh, axis=0, keepdims=True)
        lpart = jnp.sum(diff * diff, axis=0, keepdims=True) * (0.5 / D_MODEL)

        @pl.when(pl.program_id(0) == 0)
        def _():
            dg_ref[...] = part
            ls_ref[...] = lpart

        @pl.when(pl.program_id(0) > 0)
        def _():
            dg_ref[...] += part
            ls_ref[...] += lpart

    row = pl.BlockSpec((tm, D_MODEL), lambda i: (i, 0))
    vec = pl.BlockSpec((1, D_MODEL), lambda i: (0, 0))
    return pl.pallas_call(
        body, name=name, grid=(S // tm,), in_specs=[row, vec, row], out_specs=[row, vec, vec],
        out_shape=[_sds((S, D_MODEL), F32), _sds((1, D_MODEL), F32), _sds((1, D_MODEL), F32)],
        compiler_params=_params(("arbitrary",)),
    )(h, g, target)


ATTN_SCALE = HEAD_DIM ** -0.5
ALIBI_SLOPES = [2.0 ** (-8.0 * (h + 1) / N_Q_HEADS) for h in range(N_Q_HEADS)]
K_COL = N_Q_HEADS * HEAD_DIM
KV_COLS = N_KV_HEADS * HEAD_DIM
V_COL = K_COL + KV_COLS


def _attn_masks(n):
    qi = lax.broadcasted_iota(jnp.int32, (ATTN_BLOCK, ATTN_BLOCK), 0)
    ki = lax.broadcasted_iota(jnp.int32, (ATTN_BLOCK, ATTN_BLOCK), 1)
    dist_c = (qi - ki).astype(F32)
    return dist_c + float(ATTN_BLOCK), dist_c, (ki > qi) & (n > 0), qi >= ki


def _attn_probs(raw_p, raw_c, sink, slope, masks):
    dist_p, dist_c, valid_p, valid_c = masks
    sp = jnp.where(valid_p, raw_p * ATTN_SCALE - slope * dist_p, NEG_BIG)
    sc = jnp.where(valid_c, raw_c * ATTN_SCALE - slope * dist_c, NEG_BIG)
    m = jnp.maximum(jnp.maximum(jnp.max(sp, axis=-1, keepdims=True), jnp.max(sc, axis=-1, keepdims=True)), sink)
    ep, ec, es = jnp.exp(sp - m), jnp.exp(sc - m), jnp.exp(sink - m)
    inv = 1.0 / (jnp.sum(ep, axis=-1, keepdims=True) + jnp.sum(ec, axis=-1, keepdims=True) + es)
    return ep * inv, ec * inv, es * inv


def _group_rows(ref, m):
    return jnp.concatenate([ref[:, HEAD_DIM * (Q_PER_KV * m + g):HEAD_DIM * (Q_PER_KV * m + g + 1)]
                            for g in range(Q_PER_KV)], axis=0)


def _head_rows(x, g):
    return x[ATTN_BLOCK * g:ATTN_BLOCK * (g + 1)]


def _attn_specs(nblk):
    last = nblk - 1
    kcol, vcol = K_COL // KV_COLS, V_COL // KV_COLS
    return [
        pl.BlockSpec((ATTN_BLOCK, K_COL), lambda n: (jnp.minimum(n, last), 0)),
        pl.BlockSpec((ATTN_BLOCK, KV_COLS), lambda n: (jnp.minimum(n, last), kcol)),
        pl.BlockSpec((ATTN_BLOCK, KV_COLS), lambda n: (jnp.maximum(jnp.minimum(n, last) - 1, 0), kcol)),
        pl.BlockSpec((ATTN_BLOCK, KV_COLS), lambda n: (jnp.minimum(n, last), vcol)),
        pl.BlockSpec((ATTN_BLOCK, KV_COLS), lambda n: (jnp.maximum(jnp.minimum(n, last) - 1, 0), vcol)),
    ]


P_COLS = 2 * ATTN_BLOCK


def _attn_fwd(name, proj, sinks):
    S = proj.shape[0]
    nblk = S // ATTN_BLOCK

    def body(q_ref, kc_ref, kp_ref, vc_ref, vp_ref, sk_ref, o_ref, p_ref, ps_ref):
        masks = _attn_masks(pl.program_id(0))
        lane = lax.broadcasted_iota(jnp.int32, (ATTN_BLOCK, 128), 1)
        sink_p = jnp.zeros((ATTN_BLOCK, 128), F32)
        for m in range(N_KV_HEADS):
            ks = slice(HEAD_DIM * m, HEAD_DIM * (m + 1))
            kp, kc, vp, vc = kp_ref[:, ks], kc_ref[:, ks], vp_ref[:, ks], vc_ref[:, ks]
            q4 = _group_rows(q_ref, m)
            raw_p, raw_c = _bdot(q4, kp, NT), _bdot(q4, kc, NT)
            pps, pcs = [], []
            for g in range(Q_PER_KV):
                hh = Q_PER_KV * m + g
                pp, pc, ps = _attn_probs(_head_rows(raw_p, g), _head_rows(raw_c, g), sk_ref[0, hh], ALIBI_SLOPES[hh], masks)
                pps.append(pp.astype(BF16))
                pcs.append(pc.astype(BF16))
                p_ref[:, P_COLS * hh:P_COLS * hh + ATTN_BLOCK] = pps[g]
                p_ref[:, P_COLS * hh + ATTN_BLOCK:P_COLS * (hh + 1)] = pcs[g]
                sink_p = jnp.where(lane == hh, ps, sink_p)
            o4 = _bdot(jnp.concatenate(pps, axis=0), vp, NN) + _bdot(jnp.concatenate(pcs, axis=0), vc, NN)
            for g in range(Q_PER_KV):
                hh = Q_PER_KV * m + g
                o_ref[:, HEAD_DIM * hh:HEAD_DIM * (hh + 1)] = _head_rows(o4, g).astype(o_ref.dtype)
        ps_ref[...] = sink_p

    row = lambda cols: pl.BlockSpec((ATTN_BLOCK, cols), lambda n: (n, 0))
    return pl.pallas_call(
        body, name=name, grid=(nblk,),
        in_specs=_attn_specs(nblk) + [pl.BlockSpec(memory_space=pltpu.SMEM)],
        out_specs=[row(K_COL), row(N_Q_HEADS * P_COLS), row(128)],
        out_shape=[_sds((S, K_COL), BF16), _sds((S, N_Q_HEADS * P_COLS), BF16), _sds((S, 128), F32)],
        compiler_params=_params(("parallel",)),
    )(proj, proj, proj, proj, proj, sinks)


def _attn_bwd(name, proj, probs, sink_probs, do):
    S = proj.shape[0]
    nblk = S // ATTN_BLOCK

    def body(q_ref, kc_ref, kp_ref, vc_ref, vp_ref, do_ref, p_ref, ps_ref, dz_ref, ds_ref, carry, cur, padd):
        n = pl.program_id(0)

        @pl.when(n == 0)
        def _():
            carry[...] = jnp.zeros_like(carry)
            ds_ref[...] = jnp.zeros_like(ds_ref)

        @pl.when(n < nblk)
        def _():
            lane = lax.broadcasted_iota(jnp.int32, (ATTN_BLOCK, 128), 1)
            sink_p = ps_ref[...]
            dsv = jnp.zeros((1, 128), F32)
            for m in range(N_KV_HEADS):
                ks = slice(HEAD_DIM * m, HEAD_DIM * (m + 1))
                kp, kc, vp, vc = kp_ref[:, ks], kc_ref[:, ks], vp_ref[:, ks], vc_ref[:, ks]
                q4, do4 = _group_rows(q_ref, m), _group_rows(do_ref, m)
                dpp4, dpc4 = _bdot(do4, vp, NT), _bdot(do4, vc, NT)
                pps, pcs, dsps, dscs = [], [], [], []
                for g in range(Q_PER_KV):
                    hh = Q_PER_KV * m + g
                    pps.append(p_ref[:, P_COLS * hh:P_COLS * hh + ATTN_BLOCK])
                    pcs.append(p_ref[:, P_COLS * hh + ATTN_BLOCK:P_COLS * (hh + 1)])
                    pp, pc = pps[g].astype(F32), pcs[g].astype(F32)
                    dpp, dpc = _head_rows(dpp4, g), _head_rows(dpc4, g)
                    delta = jnp.sum(pp * dpp, axis=-1, keepdims=True) + jnp.sum(pc * dpc, axis=-1, keepdims=True)
                    dsv = dsv - jnp.sum(jnp.where(lane == hh, sink_p, 0.0) * delta, axis=0, keepdims=True)
                    dsps.append((pp * (dpp - delta)).astype(BF16))
                    dscs.append((pc * (dpc - delta)).astype(BF16))
                pp4, pc4 = jnp.concatenate(pps, axis=0), jnp.concatenate(pcs, axis=0)
                dsp4, dsc4 = jnp.concatenate(dsps, axis=0), jnp.concatenate(dscs, axis=0)
                dq4 = (_bdot(dsp4, kp, NN) + _bdot(dsc4, kc, NN)) * ATTN_SCALE
                for g in range(Q_PER_KV):
                    hh = Q_PER_KV * m + g
                    cur[:, HEAD_DIM * hh:HEAD_DIM * (hh + 1)] = _head_rows(dq4, g)
                cur[:, K_COL + HEAD_DIM * m:K_COL + HEAD_DIM * (m + 1)] = _bdot(dsc4, q4, TN) * ATTN_SCALE
                cur[:, V_COL + HEAD_DIM * m:V_COL + HEAD_DIM * (m + 1)] = _bdot(pc4, do4, TN)
                padd[:, ks] = _bdot(dsp4, q4, TN) * ATTN_SCALE
                padd[:, KV_COLS + HEAD_DIM * m:KV_COLS + HEAD_DIM * (m + 1)] = _bdot(pp4, do4, TN)
            ds_ref[...] += dsv
            dz_ref[:, :K_COL] = carry[:, :K_COL].astype(dz_ref.dtype)
            dz_ref[:, K_COL:] = (carry[:, K_COL:] + padd[...]).astype(dz_ref.dtype)
            carry[...] = cur[...]

        @pl.when(n == nblk)
        def _():
            dz_ref[...] = carry[...].astype(dz_ref.dtype)

    return pl.pallas_call(
        body, name=name, grid=(nblk + 1,),
        in_specs=_attn_specs(nblk) + [
            pl.BlockSpec((ATTN_BLOCK, cols), lambda n: (jnp.minimum(n, nblk - 1), 0))
            for cols in (K_COL, N_Q_HEADS * P_COLS, 128)],
        out_specs=[pl.BlockSpec((ATTN_BLOCK, ATTN_IN), lambda n: (jnp.maximum(n - 1, 0), 0)),
                   pl.BlockSpec((1, 128), lambda n: (0, 0))],
        out_shape=[_sds((S, ATTN_IN), BF16), _sds((1, 128), F32)],
        scratch_shapes=[pltpu.VMEM((ATTN_BLOCK, ATTN_IN), F32), pltpu.VMEM((ATTN_BLOCK, ATTN_IN), F32),
                        pltpu.VMEM((ATTN_BLOCK, 2 * KV_COLS), F32)],
        compiler_params=_params(("arbitrary",)),
    )(proj, proj, proj, proj, proj, do, probs, sink_probs)


def _hg_consts():
    C = HG_CHUNK
    tri = np.tril(np.ones((C, C)))
    t = np.arange(C)
    rows, masks = [tri], []
    for lvl in range(HG_LEVELS):
        n = C >> (lvl + 1)
        sel = np.zeros((C, C))
        sel[t, (t // (2 * n)) * (2 * n) + n - 1] = 1.0
        rows.append(sel @ tri)
        tt, ss = t[:, None], t[None, :]
        masks.append((tt // (2 * n) == ss // (2 * n)) & ((tt // n) % 2 == 1) & ((ss // n) % 2 == 0))
    masks.append(np.eye(C, dtype=bool))
    stk = np.concatenate(rows, axis=0)
    return jnp.asarray(stk, BF16), jnp.asarray(np.stack(masks), F32)


def _sigmoid(x):
    return 1.0 / (1.0 + jnp.exp(-x))


def _split(x, parts):
    out, rest = [], x
    for _ in range(parts):
        out.append(rest.astype(BF16))
        rest = rest - out[-1].astype(F32)
    return out


def _dot01(m01, x, dn, parts=3):
    return sum(lax.dot_general(m01, p, dn, preferred_element_type=F32) for p in _split(x, parts))


def _ref_rows(b, n):
    C = b.shape[1]
    if 2 * n >= 8:
        b3 = b.reshape(HG_CHUNK // (2 * n), 2 * n, C)
        return jnp.broadcast_to(b3[:, n - 1:n, :], b3.shape).reshape(HG_CHUNK, C)
    pos = lax.broadcasted_iota(jnp.int32, b.shape, 0) % (2 * n)
    out = b
    for p in range(2 * n):
        if p != n - 1:
            out = jnp.where(pos == p, pltpu.roll(b, (p - (n - 1)) % HG_CHUNK, 0), out)
    return out


HG_STEP_CHUNKS = 4


def _chunk_rows(ci):
    return pl.ds(pl.multiple_of(ci * HG_CHUNK, HG_CHUNK), HG_CHUNK)


def _hg_common(z_ref, rows, lb_ref, stk_ref):
    qr, fr = z_ref[0, rows, :], z_ref[1, rows, :]
    lb = lb_ref[...]
    sq, sg, sgn = _sigmoid(qr), _sigmoid(fr), _sigmoid(-fr)
    ft = lb + (1.0 - lb) * sg
    b = _dot01(stk_ref[0:HG_CHUNK, :], jnp.log(ft), NN)
    ws = [jnp.exp(-jnp.abs(b - _ref_rows(b, HG_CHUNK >> (l + 1)))) for l in range(HG_LEVELS)]
    blast = b[HG_CHUNK - 1:HG_CHUNK]
    return dict(qr=qr, fr=fr, lb=lb, sq=sq, sg=sg, sgn=sgn, ft=ft, q=qr * sq, kk=(1.0 - lb) * sgn, b=b,
                ws=ws, eb=jnp.exp(b), ed=jnp.exp(blast - b), elast=jnp.exp(blast))


def _hg_factors(qh, kh, ws, sl):
    return ([(qh * ws[l][:, sl]).astype(BF16) for l in range(HG_LEVELS)],
            [(kh * ws[l][:, sl]).astype(BF16) for l in range(HG_LEVELS)])


def _hg_intra(qh, kh, ws, msk_ref, sl):
    qls, kls = _hg_factors(qh, kh, ws, sl)
    a = msk_ref[HG_LEVELS] * _bdot(qh, kh, NT)
    for l in range(HG_LEVELS):
        a = a + msk_ref[l] * _bdot(qls[l], kls[l], NT)
    return a


def _hg_fwd(name, z, lb, ng):
    S = z.shape[2]
    nc = S // HG_CHUNK
    per = min(HG_STEP_CHUNKS, nc)
    stk, msk = _hg_consts()

    def body(z_ref, lb_ref, ng_ref, stk_ref, msk_ref, og_ref, st_ref, a_ref, o_ref, state):
        @pl.when(pl.program_id(1) == 0)
        def _():
            state[...] = jnp.zeros_like(state)

        def chunk(ci, _):
            rows = _chunk_rows(ci)
            cm = _hg_common(z_ref, rows, lb_ref, stk_ref)
            v, gt = z_ref[2, rows, :], z_ref[3, rows, :]
            kd = cm["kk"] * cm["ed"]
            for hh in range(4):
                sl = slice(HG_K * hh, HG_K * (hh + 1))
                st = state[hh]
                st_ref[ci, hh] = st
                qh, kh, vh = cm["q"][:, sl], cm["kk"][:, sl], v[:, sl]
                a = _hg_intra(qh, kh, cm["ws"], msk_ref, sl).astype(BF16)
                a_ref[ci, hh] = a
                o = _bdot(a, vh, NN) + _bdot(qh * cm["eb"][:, sl], st, NT)
                o_ref[rows, sl] = o
                state[hh] = cm["elast"][:, sl] * st + _bdot(vh, kd[:, sl], TN)
                gh = gt[:, sl]
                og_ref[rows, sl] = (o * _rms(o) * ng_ref[...] * (gh * _sigmoid(gh))).astype(og_ref.dtype)
            return 0

        lax.fori_loop(0, per, chunk, 0)

    return pl.pallas_call(
        body, name=name, grid=(2, nc // per),
        in_specs=[pl.BlockSpec((4, None, per * HG_CHUNK, HG_SLOT), lambda g, c: (0, g, c, 0)),
                  pl.BlockSpec((1, HG_SLOT), lambda g, c: (0, g)),
                  pl.BlockSpec((1, HG_K), lambda g, c: (0, 0)),
                  pl.BlockSpec(stk.shape, lambda g, c: (0, 0)),
                  pl.BlockSpec(msk.shape, lambda g, c: (0, 0, 0))],
        out_specs=[pl.BlockSpec((per * HG_CHUNK, HG_SLOT), lambda g, c: (c, g)),
                   pl.BlockSpec((per, 4, HG_K, HG_K), lambda g, c: (c, g, 0, 0)),
                   pl.BlockSpec((per, 4, HG_CHUNK, HG_CHUNK), lambda g, c: (c, g, 0, 0)),
                   pl.BlockSpec((per * HG_CHUNK, HG_SLOT), lambda g, c: (c, g))],
        out_shape=[_sds((S, D_MODEL), BF16), _sds((nc, HG_HEADS, HG_K, HG_K), F32),
                   _sds((nc, HG_HEADS, HG_CHUNK, HG_CHUNK), BF16), _sds((S, D_MODEL), F32)],
        scratch_shapes=[pltpu.VMEM((4, HG_K, HG_K), F32)],
        compiler_params=_params(("parallel", "arbitrary")),
    )(z, lb, ng, stk, msk)


def _hg_bwd(name, z, lb, ng, states, intra, o_pre, dog):
    S = z.shape[2]
    nc = S // HG_CHUNK
    per = min(HG_STEP_CHUNKS, nc)
    stk, msk = _hg_consts()

    def body(z_ref, lb_ref, ng_ref, stk_ref, msk_ref, st_ref, a_ref, o_ref, dog_ref, dz_ref, dlb_ref, dng_ref, dstate):
        @pl.when(pl.program_id(1) == 0)
        def _():
            dstate[...] = jnp.zeros_like(dstate)
            dlb_ref[...] = jnp.zeros_like(dlb_ref)
            dng_ref[...] = jnp.zeros_like(dng_ref)

        def chunk(k, _):
            ci = per - 1 - k
            rows = _chunk_rows(ci)
            cm = _hg_common(z_ref, rows, lb_ref, stk_ref)
            v, gt = z_ref[2, rows, :], z_ref[3, rows, :]
            ng = ng_ref[...]
            kd = cm["kk"] * cm["ed"]
            row = lax.broadcasted_iota(jnp.int32, (HG_CHUNK, 1), 0)
            dng = jnp.zeros((1, HG_K), F32)
            dq_h, dkk_h, db_h, dv_h, dgt_h = [], [], [], [], []
            dr_h = [[] for _ in range(HG_LEVELS)]
            for hh in range(4):
                sl = slice(HG_K * hh, HG_K * (hh + 1))
                st, dst = st_ref[ci, hh], dstate[hh]
                qh, kh, vh, ebh, edh, kdh = cm["q"][:, sl], cm["kk"][:, sl], v[:, sl], cm["eb"][:, sl], cm["ed"][:, sl], kd[:, sl]
                elh = cm["elast"][:, sl]
                qls, kls = _hg_factors(qh, kh, cm["ws"], sl)
                a, o = a_ref[ci, hh], o_ref[rows, sl]
                qe = qh * ebh
                r = _rms(o)
                xh = o * r
                gh = gt[:, sl]
                sgg = _sigmoid(gh)
                dog = dog_ref[rows, sl].astype(F32)
                dy = dog * (gh * sgg)
                dgt_h.append(dog * (xh * ng) * (sgg * (1.0 + gh * (1.0 - sgg))))
                dng = dng + jnp.sum(dy * xh, axis=0, keepdims=True)
                dyg = dy * ng
                do = r * (dyg - xh * jnp.mean(dyg * xh, axis=-1, keepdims=True))
                da = _bdot(do, vh, NT)
                dv_h.append(_bdot(a, do, TN) + _bdot(kdh, dst, NT))
                dkd = _bdot(vh, dst, NN)
                delast = jnp.sum(st * dst, axis=0, keepdims=True)
                dqe = _bdot(do, st, NN)
                dstate[hh] = elh * dst + _bdot(do, qe, TN)
                gk = dkd * kdh
                dblast = jnp.sum(gk, axis=0, keepdims=True) + delast * elh
                db = dqe * qe - gk + jnp.where(row == HG_CHUNK - 1, dblast, 0.0)
                dp = (msk_ref[HG_LEVELS] * da).astype(BF16)
                dq = dqe * ebh + _bdot(dp, kh, NN)
                dkk = dkd * edh + _bdot(dp, qh, TN)
                for l in range(HG_LEVELS):
                    dp = (msk_ref[l] * da).astype(BF16)
                    dql, dkl = _bdot(dp, kls[l], NN), _bdot(dp, qls[l], TN)
                    w = cm["ws"][l][:, sl]
                    dq = dq + dql * w
                    dkk = dkk + dkl * w
                    half = jnp.where(((row >> (HG_LEVELS - 1 - l)) & 1) == 1, 1.0, -1.0)
                    dd = half * w * (dql * qh + dkl * kh)
                    db = db + dd
                    dr_h[l].append(-dd)
                dq_h.append(dq)
                dkk_h.append(dkk)
                db_h.append(db)
            cat = lambda xs: jnp.concatenate(xs, axis=1)
            cot = jnp.concatenate([cat(db_h)] + [cat(dr_h[l]) for l in range(HG_LEVELS)], axis=0)
            dlf = _dot01(stk_ref[...], cot, TN, parts=2)
            dq, dkk = cat(dq_h), cat(dkk_h)
            dft = dlf / cm["ft"]
            one_lb = 1.0 - cm["lb"]
            dz_ref[0, rows, :] = (dq * (cm["sq"] * (1.0 + cm["qr"] * (1.0 - cm["sq"])))).astype(dz_ref.dtype)
            dz_ref[1, rows, :] = ((dft - dkk) * one_lb * cm["sg"] * cm["sgn"]).astype(dz_ref.dtype)
            dz_ref[2, rows, :] = cat(dv_h).astype(dz_ref.dtype)
            dz_ref[3, rows, :] = cat(dgt_h).astype(dz_ref.dtype)
            dlb_ref[...] += jnp.sum((dft - dkk) * cm["sgn"], axis=0, keepdims=True)
            dng_ref[...] += dng
            return 0

        lax.fori_loop(0, per, chunk, 0)

    rev = lambda c: nc // per - 1 - c
    rows_blk = pl.BlockSpec((per * HG_CHUNK, HG_SLOT), lambda g, c: (rev(c), g))
    return pl.pallas_call(
        body, name=name, grid=(2, nc // per),
        in_specs=[pl.BlockSpec((4, None, per * HG_CHUNK, HG_SLOT), lambda g, c: (0, g, rev(c), 0)),
                  pl.BlockSpec((1, HG_SLOT), lambda g, c: (0, g)),
                  pl.BlockSpec((1, HG_K), lambda g, c: (0, 0)),
                  pl.BlockSpec(stk.shape, lambda g, c: (0, 0)),
                  pl.BlockSpec(msk.shape, lambda g, c: (0, 0, 0)),
                  pl.BlockSpec((per, 4, HG_K, HG_K), lambda g, c: (rev(c), g, 0, 0)),
                  pl.BlockSpec((per, 4, HG_CHUNK, HG_CHUNK), lambda g, c: (rev(c), g, 0, 0)),
                  rows_blk, rows_blk],
        out_specs=[pl.BlockSpec((4, None, per * HG_CHUNK, HG_SLOT), lambda g, c: (0, g, rev(c), 0)),
                   pl.BlockSpec((1, HG_SLOT), lambda g, c: (0, g)),
                   pl.BlockSpec((None, 1, HG_K), lambda g, c: (g, 0, 0))],
        out_shape=[_sds(z.shape, BF16), _sds((1, 2 * HG_SLOT), F32), _sds((2, 1, HG_K), F32)],
        scratch_shapes=[pltpu.VMEM((4, HG_K, HG_K), F32)],
        compiler_params=_params(("parallel", "arbitrary")),
    )(z, lb, ng, stk, msk, states, intra, o_pre, dog)


def _lb_fwd(name, logits):
    def body(l_ref, o_ref):
        x = l_ref[...]
        e = jnp.exp(x - jnp.max(x, axis=0, keepdims=True))
        s = e / jnp.sum(e, axis=0, keepdims=True)
        o_ref[0:1, :] = s[1:2]
        o_ref[1:2, :] = s[1:2] + s[2:3] + s[3:4]

    return pl.pallas_call(body, name=name, out_shape=_sds((2, logits.shape[1]), F32))(logits)


def _lb_bwd(name, logits, dlb):
    def body(l_ref, d_ref, o_ref):
        x = l_ref[...]
        e = jnp.exp(x - jnp.max(x, axis=0, keepdims=True))
        s = e / jnp.sum(e, axis=0, keepdims=True)
        d1, d3 = d_ref[0:1, :], d_ref[1:2, :]
        ds = [jnp.zeros_like(d1), d1 + d3, d3, d3]
        dot = sum(ds[r] * s[r:r + 1] for r in range(1, DEPTH))
        for r in range(DEPTH):
            o_ref[r:r + 1, :] = s[r:r + 1] * (ds[r] - dot)

    return pl.pallas_call(body, name=name, out_shape=_sds(logits.shape, F32))(logits, dlb)


SUB = 8


def _rows_down(x, prev, k):
    row = lax.broadcasted_iota(jnp.int32, x.shape, 0)
    return jnp.where(row >= k, pltpu.roll(x, k, 0), pltpu.roll(prev, k, 0))


def _rows_up(x, nxt, k):
    row = lax.broadcasted_iota(jnp.int32, x.shape, 0)
    return jnp.where(row < SUB - k, pltpu.roll(x, SUB - k, 0), pltpu.roll(nxt, SUB - k, 0))


def _conv_block(w_ref, b_ref, p, x, prev):
    return (b_ref[p] + w_ref[p, 0:1, :] * _rows_down(x, prev, 2) + w_ref[p, 1:2, :] * _rows_down(x, prev, 1)
            + w_ref[p, 2:3, :] * x)


def _convgate_fwd(name, u, cw, cb):
    S = u.shape[2]
    tm = _tile(S, ROW_TILE)

    def body(u_ref, w_ref, b_ref, a_ref, c_ref, halo):
        @pl.when(pl.program_id(1) == 0)
        def _():
            halo[...] = jnp.zeros_like(halo)

        def step(r, prev):
            pg, pv = prev
            out, cgs, cvs = [], [], []
            for s in range(2):
                rows = pl.ds(pl.multiple_of(r * 2 * SUB + s * SUB, SUB), SUB)
                xg, xv = u_ref[0, rows, :], u_ref[1, rows, :]
                cgs.append(_conv_block(w_ref, b_ref, 0, xg, pg))
                cvs.append(_conv_block(w_ref, b_ref, 1, xv, pv))
                out.append(cgs[s] * _sigmoid(cgs[s]) * cvs[s])
                pg, pv = xg, xv
            rows = pl.ds(pl.multiple_of(r * 2 * SUB, 2 * SUB), 2 * SUB)
            a_ref[rows, :] = jnp.concatenate(out, axis=0).astype(a_ref.dtype)
            c_ref[0, rows, :] = jnp.concatenate(cgs, axis=0).astype(c_ref.dtype)
            c_ref[1, rows, :] = jnp.concatenate(cvs, axis=0).astype(c_ref.dtype)
            return pg, pv

        pg, pv = lax.fori_loop(0, tm // (2 * SUB), step, (halo[0], halo[1]))
        halo[0] = pg
        halo[1] = pv

    pair = pl.BlockSpec((2, None, tm, FF_SLOT), lambda j, t: (0, j, t, 0))
    return pl.pallas_call(
        body, name=name, grid=(4, S // tm),
        in_specs=[pair, pl.BlockSpec((2, None, 3, FF_SLOT), lambda j, t: (0, j, 0, 0)),
                  pl.BlockSpec((2, None, 1, FF_SLOT), lambda j, t: (0, j, 0, 0))],
        out_specs=[pl.BlockSpec((None, tm, FF_SLOT), lambda j, t: (j, t, 0)), pair],
        out_shape=[_sds((4, S, FF_SLOT), BF16), _sds(u.shape, BF16)],
        scratch_shapes=[pltpu.VMEM((2, SUB, FF_SLOT), F32)],
        compiler_params=_params(("parallel", "arbitrary")),
    )(u, cw, cb)


def _convgate_bwd(name, u, convs, cw, da):
    S = u.shape[2]
    tm = _tile(S, ROW_TILE)
    nt = S // tm

    def body(u_ref, c_ref, w_ref, da_ref, du_out, dw_ref, db_ref, after, first, acc, du_ref):
        @pl.when(pl.program_id(1) == 0)
        def _():
            after[...] = jnp.zeros_like(after)
            acc[...] = jnp.zeros_like(acc)

        def finish(p, x, d, nxt, rows):
            taps = (_rows_up(d, nxt, 2), _rows_up(d, nxt, 1), d)
            du_ref[p, rows, :] = w_ref[p, 0:1, :] * taps[0] + w_ref[p, 1:2, :] * taps[1] + w_ref[p, 2:3, :] * d
            for j in range(3):
                acc[p, j] += taps[j] * x
            acc[p, 3] += d

        def step(r, carry):
            xg_last, xv_last, dg_last, dv_last = carry
            rows16 = pl.ds(pl.multiple_of(r * 2 * SUB, 2 * SUB), 2 * SUB)
            dav = da_ref[rows16, :].astype(F32)
            cg16, cv16 = c_ref[0, rows16, :].astype(F32), c_ref[1, rows16, :].astype(F32)
            for s in range(2):
                at = r * 2 * SUB + s * SUB
                part = slice(s * SUB, (s + 1) * SUB)
                cg, cv, dab = cg16[part], cv16[part], dav[part]
                sg = _sigmoid(cg)
                dg = dab * cv * (sg * (1.0 + cg * (1.0 - sg)))
                dv = dab * cg * sg
                before = pl.ds(pl.multiple_of(at - SUB, SUB), SUB)
                if s == 0:
                    @pl.when(r == 0)
                    def _():
                        first[0] = dg
                        first[1] = dv

                    @pl.when(r > 0)
                    def _():
                        finish(0, xg_last, dg_last, dg, before)
                        finish(1, xv_last, dv_last, dv, before)
                else:
                    finish(0, xg_last, dg_last, dg, before)
                    finish(1, xv_last, dv_last, dv, before)
                rows = pl.ds(pl.multiple_of(at, SUB), SUB)
                xg_last, xv_last, dg_last, dv_last = u_ref[0, rows, :], u_ref[1, rows, :], dg, dv
            return xg_last, xv_last, dg_last, dv_last

        zero = jnp.zeros((SUB, FF_SLOT), F32)
        xg_last, xv_last, dg_last, dv_last = lax.fori_loop(0, tm // (2 * SUB), step, (zero, zero, zero, zero))
        finish(0, xg_last, dg_last, after[0], slice(tm - SUB, tm))
        finish(1, xv_last, dv_last, after[1], slice(tm - SUB, tm))
        du_out[...] = du_ref[...].astype(du_out.dtype)
        after[...] = first[...]
        for p in range(2):
            for j in range(3):
                dw_ref[p, j:j + 1, :] = jnp.sum(acc[p, j], axis=0, keepdims=True)
            db_ref[p] = jnp.sum(acc[p, 3], axis=0, keepdims=True)

    rev = lambda t: nt - 1 - t
    pair = pl.BlockSpec((2, None, tm, FF_SLOT), lambda j, t: (0, j, rev(t), 0))
    taps = pl.BlockSpec((2, None, 3, FF_SLOT), lambda j, t: (0, j, 0, 0))
    bias = pl.BlockSpec((2, None, 1, FF_SLOT), lambda j, t: (0, j, 0, 0))
    return pl.pallas_call(
        body, name=name, grid=(4, nt),
        in_specs=[pair, pair, taps, pl.BlockSpec((None, tm, FF_SLOT), lambda j, t: (j, rev(t), 0))],
        out_specs=[pair, taps, bias],
        out_shape=[_sds(u.shape, BF16), _sds(cw.shape, F32), _sds((2, 4, 1, FF_SLOT), F32)],
        scratch_shapes=[pltpu.VMEM((2, SUB, FF_SLOT), F32), pltpu.VMEM((2, SUB, FF_SLOT), F32),
                        pltpu.VMEM((2, 4, SUB, FF_SLOT), F32), pltpu.VMEM((2, tm, FF_SLOT), F32)],
        compiler_params=_params(("parallel", "arbitrary")),
    )(u, convs, cw, da)


def _row_tile(R):
    for t in range(256, 15, -16):
        if R % t == 0:
            return t
    return R


def _adamw(name, gsrcs, w, m, v, dep=None):
    L = len(gsrcs)
    n, A, C = gsrcs[0].shape
    tr = _row_tile(A)
    deps = () if dep is None else (dep,)

    def body(*refs):
        g_refs = refs[:L]
        w_ref, m_ref, v_ref = refs[L:L + 3]
        go_ref, d_ref, mo_ref, vo_ref = refs[L + 3 + len(deps):]
        for k in range(L):
            @pl.when(pl.program_id(0) == k)
            def _(k=k):
                g = g_refs[k][0].astype(F32)
                for s in range(1, n):
                    g = g + g_refs[k][s].astype(F32)
                m2 = ADAM_B1 * m_ref[...] + (1.0 - ADAM_B1) * g
                v2 = ADAM_B2 * v_ref[...] + (1.0 - ADAM_B2) * (g * g)
                m_hat = m2 / (1.0 - ADAM_B1 ** ADAM_STEP)
                v_hat = v2 / (1.0 - ADAM_B2 ** ADAM_STEP)
                go_ref[...] = g
                d_ref[...] = -ADAM_LR * (m_hat / (jnp.sqrt(v_hat) + ADAM_EPS) + ADAM_WD * w_ref[...])
                mo_ref[...] = m2
                vo_ref[...] = v2

    g_specs = [pl.BlockSpec((n, tr, C), lambda l, i, k=k: (0, jnp.where(l == k, i, 0), 0)) for k in range(L)]
    blk = pl.BlockSpec((None, tr, C), lambda l, i: (l, i, 0))
    return pl.pallas_call(
        body, name=name, grid=(L, A // tr), in_specs=g_specs + [blk, blk, blk] + [_dep_spec(2)] * len(deps),
        out_specs=[blk] * 4, out_shape=[_sds((L, A, C), F32)] * 4, compiler_params=_params(("parallel", "parallel")),
    )(*gsrcs, w, m, v, *deps)


MESH = pl.DeviceIdType.MESH
HBM_SPEC = pl.BlockSpec(memory_space=pltpu.HBM)
N_PEERS = N_DEV - 1


def _mesh_place():
    x, y, c = lax.axis_index("x"), lax.axis_index("y"), lax.axis_index("c")
    peers = []
    for p in range(1, N_DEV):
        px = 1 - x if p & 4 else x
        py = 1 - y if p & 2 else y
        pc = 1 - c if p & 1 else c
        peers.append(((px, py, pc), 4 * px + 2 * py + pc))
    return 4 * x + 2 * y + c, peers


SEM_SPEC = pl.BlockSpec(memory_space=pltpu.SEMAPHORE)
ANY_SPEC = pl.BlockSpec(memory_space=pl.ANY)
EFFECT = pltpu.SideEffectType.DATAFLOW_SIDE_EFFECTING


def _exchange_refs(scatter, src, land, send, recv, k, p, dev, idx, me):
    return pltpu.make_async_remote_copy(src_ref=src[k].at[idx] if scatter else src[k], dst_ref=land[k].at[me],
                                        send_sem=send.at[k * N_PEERS + p], recv_sem=recv.at[k * N_PEERS + p], device_id=dev,
                                        device_id_type=MESH)


def _exchange_start(name, srcs, scatter, gate):
    n = len(srcs)
    lands = [lax.empty(s.shape if scatter else (N_DEV,) + s.shape, s.dtype) for s in srcs]

    def body(*refs):
        src, land = refs[:n], refs[n:2 * n]
        send, recv, own = refs[2 * n + 1:2 * n + 4]
        token = refs[-1]
        me, peers = _mesh_place()
        for k in range(n):
            pltpu.make_async_copy(src[k].at[me] if scatter else src[k], land[k].at[me], own.at[k]).start()
            for p, (dev, idx) in enumerate(peers):
                _exchange_refs(scatter, src, land, send, recv, k, p, dev, idx, me).start()
        token[...] = jnp.zeros_like(token)

    hbm = lambda a: pltpu.HBM(a.shape, a.dtype)
    outs = pl.pallas_call(
        body, name=name,
        out_shape=(pltpu.SemaphoreType.DMA((n * N_PEERS,)), pltpu.SemaphoreType.DMA((n * N_PEERS,)),
                   pltpu.SemaphoreType.DMA((n,)), *[hbm(s) for s in srcs], *[hbm(s) for s in lands], _sds(DEP_SHAPE, F32)),
        in_specs=[HBM_SPEC] * (2 * n) + [ANY_SPEC],
        out_specs=(SEM_SPEC, SEM_SPEC, SEM_SPEC, *[HBM_SPEC] * (2 * n), pl.BlockSpec(memory_space=pltpu.VMEM)),
        input_output_aliases={j: 3 + j for j in range(2 * n)},
        compiler_params=pltpu.CompilerParams(has_side_effects=EFFECT),
    )(*[pltpu.with_memory_space_constraint(s, pltpu.HBM) for s in srcs],
      *[pltpu.with_memory_space_constraint(s, pltpu.HBM) for s in lands], gate)
    return outs[:3], None, list(outs[3:3 + n]), list(outs[3 + n:3 + 2 * n]), outs[-1]


def _exchange_wait(name, started, scatter, after):
    (send, recv, own), _, srcs, lands, _ = started
    n = len(srcs)

    def body(*refs):
        src, land = refs[:n], refs[n:2 * n]
        send, recv, own = refs[2 * n:2 * n + 3]
        me, peers = _mesh_place()
        for k in range(n):
            pltpu.make_async_copy(src[k].at[me] if scatter else src[k], land[k].at[me], own.at[k]).wait()
            for p, (dev, idx) in enumerate(peers):
                cp = pltpu.make_async_remote_copy(src_ref=src[k].at[idx] if scatter else src[k], dst_ref=land[k].at[idx],
                                                  send_sem=send.at[k * N_PEERS + p], recv_sem=recv.at[k * N_PEERS + p], device_id=dev,
                                                  device_id_type=MESH)
                cp.wait_send()
                cp.wait_recv()

    hbm = lambda a: pltpu.HBM(a.shape, a.dtype)
    outs = pl.pallas_call(
        body, name=name, out_shape=(*[hbm(s) for s in srcs], *[hbm(s) for s in lands]),
        in_specs=[HBM_SPEC] * (2 * n) + [SEM_SPEC, SEM_SPEC, SEM_SPEC, ANY_SPEC], out_specs=tuple([HBM_SPEC] * (2 * n)),
        input_output_aliases={j: j for j in range(2 * n)},
        compiler_params=pltpu.CompilerParams(has_side_effects=EFFECT),
    )(*srcs, *lands, send, recv, own, after)
    return list(outs[n:])


def _sum_devices(name, parts):
    def body(p_ref, o_ref):
        tot = p_ref[0]
        for j in range(1, N_DEV):
            tot = tot + p_ref[j]
        o_ref[...] = tot

    return pl.pallas_call(body, name=name, out_shape=_sds(parts.shape[1:], F32),
                          compiler_params=pltpu.CompilerParams(vmem_limit_bytes=VMEM_LIMIT))(parts)


def _rows(a, width=D_MODEL):
    flat = a.reshape(-1)
    return jnp.pad(flat, (0, (-flat.shape[0]) % width)).reshape(-1, width)


def _pack_rows(parts):
    blocks = []
    for p in parts:
        r = _rows(p)
        blocks.append(jnp.pad(r, ((0, (-r.shape[0]) % 8), (0, 0))))
    return jnp.concatenate(blocks, axis=0)


def _unpack_rows(rows, shapes):
    out, at = [], 0
    for s in shapes:
        size = int(np.prod(s))
        n = -(-size // D_MODEL)
        out.append(rows[at:at + n].reshape(-1)[:size].reshape(s))
        at += -(-n // 8) * 8
    return out


def kernel(x, norm_mix, norm_ffn, norm_final, attn_w_in, attn_w_out, attn_sinks, hgrn_w_in, hgrn_w_out, hgrn_norm, hgrn_lb_logits, ffn_w_up, ffn_conv_w, ffn_conv_b, ffn_w_down, loss_target, m_norm_mix, m_norm_ffn, m_norm_final, m_attn_w_in, m_attn_w_out, m_attn_sinks, m_hgrn_w_in, m_hgrn_w_out, m_hgrn_norm, m_hgrn_lb_logits, m_ffn_w_up, m_ffn_conv_w, m_ffn_conv_b, m_ffn_w_down, v_norm_mix, v_norm_ffn, v_norm_final, v_attn_w_in, v_attn_w_out, v_attn_sinks, v_hgrn_w_in, v_hgrn_w_out, v_hgrn_norm, v_hgrn_lb_logits, v_ffn_w_up, v_ffn_conv_w, v_ffn_conv_b, v_ffn_w_down):
    S = x.shape[1]
    n_attn, n_hgrn = attn_w_in.shape[0], hgrn_w_in.shape[0]
    me = 4 * lax.axis_index("x") + 2 * lax.axis_index("y") + lax.axis_index("c")

    wa_in_t, wa_out_b = attn_w_in.transpose(0, 2, 1).astype(BF16), attn_w_out.astype(BF16)
    wh_in_b, wh_out_b = hgrn_w_in.astype(BF16), hgrn_w_out.astype(BF16)
    wf_up_b, wf_down_b = ffn_w_up.transpose(0, 2, 1).astype(BF16), ffn_w_down.astype(BF16)
    conv_b = ffn_conv_b.reshape(DEPTH, 2, 4, 1, FF_SLOT)
    lb = _lb_fwd("lb_fwd", hgrn_lb_logits)

    def unit_shards(l, part):
        if part == "ffn":
            return [wf_up_b[l], wf_down_b[l], ffn_conv_w[l]]
        return [wa_in_t[l // 2], wa_out_b[l // 2]] if l % 2 == 0 else [wh_in_b[l // 2], wh_out_b[l // 2]]

    def unit_weights(l, part, w):
        if part == "ffn":
            return w[0][None], w[1].reshape(1, 4, FF_SLOT, D_MODEL), w[2].reshape(2, 4, 3, FF_SLOT)
        if l % 2 == 0:
            return w[0].reshape(1, ATTN_IN, D_MODEL), w[1].reshape(1, D_MODEL, D_MODEL)
        return w[0][None], w[1].reshape(1, D_MODEL, D_MODEL)

    units = [(l, part) for l in range(DEPTH) for part in ("mix", "ffn")]
    gathers = [_exchange_start("gather_start0", unit_shards(*units[0]), False, norm_final)]
    gathers.append(_exchange_start("gather_start1", unit_shards(*units[1]), False, gathers[0][4]))
    arrived = _exchange_wait("gather_wait0", gathers[0], False, gathers[1][4])
    weights, saved = {}, [dict() for _ in range(DEPTH)]
    h = x[0]
    hn = _rmsnorm_fwd("norm_mix_fwd0", h, norm_mix[0:1])
    for n, (l, part) in enumerate(units):
        i, sv = l // 2, saved[l]
        weights[l, part] = w = unit_weights(l, part, arrived)
        dep = None
        if n + 2 < len(units):
            gathers.append(_exchange_start(f"gather_start{n + 2}", unit_shards(*units[n + 2]), False, arrived[0]))
            dep = gathers[n + 2][4]
        if part == "mix":
            sv["h"], sv["hn"] = h, hn
            if l % 2 == 0:
                sv["proj"] = _proj_rows(f"attn_proj{i}", hn, w[0], 0, BF16, dep)
                sv["o"], *sv["kept"] = _attn_fwd(f"attn_fwd{i}", sv["proj"], attn_sinks[i:i + 1])
                h, hn = _out_proj(f"attn_out{i}", sv["o"], w[1], 0, h, norm_ffn[l:l + 1])
            else:
                sv["z"] = _proj_slots(f"hgrn_proj{i}", hn, w[0], 0, dep=dep).reshape(4, 2, S, HG_SLOT)
                sv["o"], *sv["kept"] = _hg_fwd(f"hgrn_fwd{i}", sv["z"], lb[i:i + 1], hgrn_norm[i:i + 1])
                h, hn = _out_proj(f"hgrn_out{i}", sv["o"], w[1], 0, h, norm_ffn[l:l + 1])
        else:
            sv["h2"], sv["hn2"] = h, hn
            sv["u"] = _proj_slots(f"ffn_up{l}", hn, w[0], 0, True, dep).reshape(2, 4, S, FF_SLOT)
            sv["a"], sv["convs"] = _convgate_fwd(f"ffn_gate{l}", sv["u"], w[2], conv_b[l])
            if l + 1 < DEPTH:
                h, hn = _down_proj(f"ffn_down{l}", sv["a"], w[1], 0, h, norm_mix[l + 1:l + 2])
            else:
                h = _down_proj(f"ffn_down{l}", sv["a"], w[1], 0, h)
        if n + 1 < len(units):
            arrived = _exchange_wait(f"gather_wait{n + 1}", gathers[n + 1], False, h)
    dh, d_norm_final, loss_rows = _loss_head("loss_head", h, norm_final[None], loss_target[0])

    d_conv_w, d_conv_b, d_norm_mix, d_norm_ffn = [None] * DEPTH, [None] * DEPTH, [None] * DEPTH, [None] * DEPTH
    d_sinks, d_lb, d_hgrn_norm = [None] * n_attn, [None] * n_hgrn, [None] * n_hgrn
    received, pending = {}, []
    for l, part in reversed(units):
        i, sv, w = l // 2, saved[l], weights[l, part]
        dep = pending[-1][1][4] if pending else None
        if part == "ffn":
            da = _dgrad_down(f"ffn_down_dgrad{l}", dh, w[1], 0, dep)
            g_down = _wgrad_down(f"ffn_down_wgrad{l}", sv["a"], dh).reshape(N_DEV, D_FF // N_DEV, D_MODEL)
            du, d_conv_w[l], d_conv_b[l] = _convgate_bwd(f"ffn_gate_bwd{l}", sv["u"], sv["convs"], w[2], da)
            du = du.reshape(N_DEV, S, FF_SLOT)
            grads = [_wgrad_slots(f"ffn_up_wgrad{l}", sv["hn2"], du, True), g_down]
            dh, d_norm_ffn[l] = _dgrad_slots(f"ffn_up_dgrad{l}", du, w[0], 0, (sv["h2"], norm_ffn[l:l + 1], dh), True)
        else:
            if l % 2 == 0:
                do = _dgrad_out(f"attn_out_dgrad{i}", dh, w[1], 0, BF16, dep)
                g_out = _wgrad_rows(f"attn_out_wgrad{i}", sv["o"], dh)
                dproj, d_sinks[i] = _attn_bwd(f"attn_bwd{i}", sv["proj"], *sv["kept"], do)
                g_in = _wgrad_rows(f"attn_proj_wgrad{i}", dproj, sv["hn"]).reshape(N_DEV, ATTN_IN // N_DEV, D_MODEL)
                dh_new = _dgrad_rows(f"attn_proj_dgrad{i}", dproj, w[0], 0, (sv["h"], norm_mix[l:l + 1], dh))
            else:
                dog = _dgrad_out(f"hgrn_out_dgrad{i}", dh, w[1], 0, F32, dep)
                g_out = _wgrad_rows(f"hgrn_out_wgrad{i}", sv["o"], dh)
                dz, d_lb[i], dng = _hg_bwd(f"hgrn_bwd{i}", sv["z"], lb[i:i + 1], hgrn_norm[i:i + 1], *sv["kept"], dog)
                d_hgrn_norm[i] = dng[0] + dng[1]
                dz = dz.reshape(N_DEV, S, HG_SLOT)
                g_in = _wgrad_slots(f"hgrn_proj_wgrad{i}", sv["hn"], dz)
                dh_new = _dgrad_slots(f"hgrn_proj_dgrad{i}", dz, w[0], 0, (sv["h"], norm_mix[l:l + 1], dh))
            grads = [g_in, g_out.reshape(N_DEV, D_MODEL // N_DEV, D_MODEL)]
            dh, d_norm_mix[l] = dh_new
        gate = dh
        if len(pending) == 2:
            key, oldest = pending.pop(0)
            received[key] = _exchange_wait(f"scatter_wait_{key[1]}{key[0]}", oldest, True, dh)
            gate = received[key][0]
        pending.append(((l, part), _exchange_start(f"scatter_start_{part}{l}", grads, True, gate)))
    grad_x = dh[None]

    small_shapes = [(DEPTH, D_MODEL), (DEPTH, D_MODEL), (1, D_MODEL), (1, D_MODEL), (n_hgrn, D_MODEL), (n_attn, 128),
                    (n_hgrn, HG_K), (DEPTH, 2 * D_FF), (DEPTH, N_DEV, 3, FF_SLOT)]
    partial = _pack_rows([
        jnp.concatenate(d_norm_mix), jnp.concatenate(d_norm_ffn), d_norm_final, loss_rows, jnp.concatenate(d_lb),
        jnp.concatenate(d_sinks), jnp.concatenate(d_hgrn_norm), jnp.stack(d_conv_b), jnp.stack(d_conv_w)])
    small_started = _exchange_start("small_start", [partial], False, pending[-1][1][4])
    attn_layers, hgrn_layers = range(0, DEPTH, 2), range(1, DEPTH, 2)

    def transposed(ts):
        return [t.transpose(0, 2, 1) for t in ts]

    big = {"hgrn_w_in": _adamw("adamw_hgrn_in", [received[l, "mix"][0] for l in hgrn_layers], hgrn_w_in, m_hgrn_w_in,
                               v_hgrn_w_in, dep=small_started[4])}
    big["hgrn_w_out"] = _adamw("adamw_hgrn_out", [received[l, "mix"][1] for l in hgrn_layers], hgrn_w_out, m_hgrn_w_out, v_hgrn_w_out)
    key, oldest = pending.pop(0)
    received[key] = _exchange_wait(f"scatter_wait_{key[1]}{key[0]}", oldest, True, big["hgrn_w_in"][3])
    up_t = _adamw("adamw_ffn_up", [received[l, "ffn"][0] for l in range(DEPTH)], *transposed((ffn_w_up, m_ffn_w_up, v_ffn_w_up)))
    big["ffn_w_up"] = transposed(up_t)
    big["ffn_w_down"] = _adamw("adamw_ffn_down", [received[l, "ffn"][1] for l in range(DEPTH)], ffn_w_down, m_ffn_w_down, v_ffn_w_down)
    key, oldest = pending.pop(0)
    received[key] = _exchange_wait(f"scatter_wait_{key[1]}{key[0]}", oldest, True, up_t[3])
    total = _sum_devices("sum_small", _exchange_wait("small_wait", small_started, False, up_t[3])[0])
    (g_norm_mix, g_norm_ffn, g_norm_final, loss_sum, g_lb, g_sinks, g_hgrn_norm, g_conv_b, g_conv_w_all) = _unpack_rows(
        total, small_shapes)

    loss = jnp.sum(loss_sum)
    g_norm_final = g_norm_final[0]
    g_sinks = g_sinks[:, :N_Q_HEADS]
    g_lb_logits = _lb_bwd("lb_bwd", hgrn_lb_logits, g_lb)
    g_conv_w = lax.dynamic_index_in_dim(g_conv_w_all, me, axis=1, keepdims=False)

    big.update({
        "attn_w_in": transposed(_adamw("adamw_attn_in", [received[l, "mix"][0] for l in attn_layers],
                                       *transposed((attn_w_in, m_attn_w_in, v_attn_w_in)))),
        "attn_w_out": _adamw("adamw_attn_out", [received[l, "mix"][1] for l in attn_layers], attn_w_out, m_attn_w_out, v_attn_w_out),
        "ffn_conv_w": _adamw("adamw_conv_w", [g_conv_w[l][None] for l in range(DEPTH)], ffn_conv_w, m_ffn_conv_w, v_ffn_conv_w),
    })
    small_w = [norm_mix, norm_ffn, norm_final, attn_sinks, hgrn_norm, hgrn_lb_logits, ffn_conv_b]
    small_m = [m_norm_mix, m_norm_ffn, m_norm_final, m_attn_sinks, m_hgrn_norm, m_hgrn_lb_logits, m_ffn_conv_b]
    small_v = [v_norm_mix, v_norm_ffn, v_norm_final, v_attn_sinks, v_hgrn_norm, v_hgrn_lb_logits, v_ffn_conv_b]
    small_g = [g_norm_mix, g_norm_ffn, g_norm_final, g_sinks, g_hgrn_norm, g_lb_logits, g_conv_b]
    outs = _adamw("adamw_small", [_pack_rows(small_g)[None]], *[_pack_rows(t)[None] for t in (small_w, small_m, small_v)])
    outs = [o[0] for o in outs]
    shapes = [w.shape for w in small_w]
    small = {n: [t[j] for t in [_unpack_rows(o, shapes) for o in outs]]
             for j, n in enumerate(["norm_mix", "norm_ffn", "norm_final", "attn_sinks", "hgrn_norm", "hgrn_lb_logits", "ffn_conv_b"])}
    order = ["norm_mix", "norm_ffn", "norm_final", "attn_w_in", "attn_w_out", "attn_sinks", "hgrn_w_in", "hgrn_w_out",
             "hgrn_norm", "hgrn_lb_logits", "ffn_w_up", "ffn_conv_w", "ffn_conv_b", "ffn_w_down"]
    res = {**big, **small}
    return (loss, grad_x, *[res[n][0] for n in order], *[res[n][1] for n in order], *[res[n][2] for n in order],
            *[res[n][3] for n in order])
```

```python
import numpy as np
import jax
import jax.numpy as jnp
from jax import lax
from jax.experimental import pallas as pl
from jax.experimental.pallas import tpu as pltpu

F32 = jnp.float32
BF16 = jnp.bfloat16

D_MODEL = 1024
DEPTH = 4
HEAD_DIM = 64
N_Q_HEADS = 16
N_KV_HEADS = 4
Q_PER_KV = 4
ATTN_BLOCK = 128
ATTN_IN = 1536
HG_HEADS = 8
HG_K = 128
HG_CHUNK = 64
HG_IN = 4096
D_FF = 2816
EPS = 1e-6
N_DEV = 8
FF_SLOT = 2 * D_FF // N_DEV
HG_SLOT = HG_IN // N_DEV
HG_LEVELS = 6

ADAM_LR = 0.001
ADAM_B1 = 0.9
ADAM_B2 = 0.999
ADAM_EPS = 1e-08
ADAM_WD = 0.01
ADAM_STEP = 10

VMEM_LIMIT = 56 * 1024 * 1024
ROW_TILE = 1024
WIDE_ROW_TILE = 2048
NEG_BIG = -1e30

NN = (((1,), (0,)), ((), ()))
NT = (((1,), (1,)), ((), ()))
TN = (((0,), (0,)), ((), ()))


def _bdot(a, b, dn):
    return lax.dot_general(a.astype(BF16), b.astype(BF16), dn, preferred_element_type=F32)


def _sds(shape, dtype):
    return jax.ShapeDtypeStruct(tuple(shape), dtype)


def _params(sem):
    return pltpu.CompilerParams(dimension_semantics=sem, vmem_limit_bytes=VMEM_LIMIT)


DEP_SHAPE = (8, 128)


def _dep_spec(rank):
    return pl.BlockSpec(DEP_SHAPE, lambda *_: (0, 0))


def _matmul(name, a, b, *, dn, grid, a_spec, b_spec, o_spec, out_shape, acc_shape=None, extra=(), extra_specs=(),
            finish=None, dep=None, sem=("parallel", "parallel", "arbitrary")):
    nk = grid[2]
    many = isinstance(out_shape, (list, tuple))
    n_in = 2 + len(extra) + (dep is not None)
    n_out = len(out_shape) if many else 1

    def body(*refs):
        a_ref, b_ref = refs[0], refs[1]
        outs = refs[n_in:n_in + n_out]

        def prod():
            return _bdot(a_ref[...], b_ref[...], dn)

        def done(v):
            if finish is None:
                outs[0][...] = v.astype(outs[0].dtype)
            else:
                finish(v, refs[2:2 + len(extra)], outs)

        if nk == 1:
            done(prod())
        else:
            acc = refs[-1]
            k = pl.program_id(2)

            @pl.when(k == 0)
            def _():
                acc[...] = prod()

            @pl.when(k > 0)
            def _():
                acc[...] += prod()

            @pl.when(k == nk - 1)
            def _():
                done(acc[...])

    in_specs = [a_spec, b_spec, *extra_specs] + ([_dep_spec(3)] if dep is not None else [])
    args = (a, b, *extra) + ((dep,) if dep is not None else ())
    scratch = [] if nk == 1 else [pltpu.VMEM(acc_shape, F32)]
    return pl.pallas_call(
        body, name=name, grid=grid, in_specs=in_specs, out_specs=o_spec, out_shape=out_shape,
        scratch_shapes=scratch, compiler_params=_params(sem),
    )(*args)


def _rms(x):
    return lax.rsqrt(jnp.mean(x * x, axis=-1, keepdims=True) + EPS)


def _residual_finish(v, ex, outs):
    h = v + ex[0][...]
    outs[0][...] = h
    if len(ex) > 1:
        outs[1][...] = (h * _rms(h) * ex[1][...]).astype(outs[1].dtype)


def _norm_bwd_finish(v, ex, outs):
    x = ex[0][...]
    r = _rms(x)
    xh = x * r
    dyg = v * ex[1][...]
    outs[0][...] = ex[2][...] + r * (dyg - xh * jnp.mean(dyg * xh, axis=-1, keepdims=True))
    part = jnp.sum(v * xh, axis=0, keepdims=True)

    @pl.when(pl.program_id(0) == 0)
    def _():
        outs[1][...] = part

    @pl.when(pl.program_id(0) > 0)
    def _():
        outs[1][...] += part


def _row_io(tm, norm_g):
    row = pl.BlockSpec((tm, D_MODEL), lambda i, j, k: (i, 0))
    vec = pl.BlockSpec((1, D_MODEL), lambda i, j, k: (0, 0))
    if norm_g is None:
        return (row,), row, lambda S: _sds((S, D_MODEL), F32)
    return (row, vec), [row, row], lambda S: [_sds((S, D_MODEL), F32), _sds((S, D_MODEL), BF16)]


def _tile(n, t):
    return min(n, t)


def _proj_rows(name, hn, wt, l, out_dtype, dep=None):
    S, N = hn.shape[0], wt.shape[1]
    tm, tn = _tile(S, ROW_TILE), 512
    return _matmul(
        name, hn, wt, dn=NT, grid=(S // tm, N // tn, 1),
        a_spec=pl.BlockSpec((tm, D_MODEL), lambda i, j, k: (i, 0)),
        b_spec=pl.BlockSpec((None, tn, D_MODEL), lambda i, j, k: (l, j, 0)),
        o_spec=pl.BlockSpec((tm, tn), lambda i, j, k: (i, j)),
        out_shape=_sds((S, N), out_dtype), dep=dep)


def _slot_weight(w, transposed):
    if transposed:
        return w.shape[2], (None, None, w.shape[2], D_MODEL), NT, NN
    return w.shape[3], (None, None, D_MODEL, w.shape[3]), NN, NT


def _proj_slots(name, hn, w, l, transposed=False, dep=None):
    S = hn.shape[0]
    r, blk, dn, _ = _slot_weight(w, transposed)
    tm = _tile(S, WIDE_ROW_TILE)
    return _matmul(
        name, hn, w, dn=dn, grid=(N_DEV, S // tm, 1),
        a_spec=pl.BlockSpec((tm, D_MODEL), lambda j, i, k: (i, 0)),
        b_spec=pl.BlockSpec(blk, lambda j, i, k: (l, j, 0, 0)),
        o_spec=pl.BlockSpec((None, tm, r), lambda j, i, k: (j, i, 0)),
        out_shape=_sds((N_DEV, S, r), F32), dep=dep)


def _out_proj(name, o, w, l, h, norm_g=None):
    S, K = o.shape
    tm = _tile(S, ROW_TILE)
    extra_specs, o_spec, out_shape = _row_io(tm, norm_g)
    return _matmul(
        name, o, w, dn=NN, grid=(S // tm, 1, 1),
        a_spec=pl.BlockSpec((tm, K), lambda i, j, k: (i, 0)),
        b_spec=pl.BlockSpec((None, K, D_MODEL), lambda i, j, k: (l, 0, 0)),
        o_spec=o_spec, out_shape=out_shape(S), extra=(h,) if norm_g is None else (h, norm_g),
        extra_specs=extra_specs, finish=_residual_finish)


def _down_proj(name, a, w, l, h, norm_g=None):
    nj, S, r = a.shape
    tm = _tile(S, ROW_TILE)
    extra_specs, o_spec, out_shape = _row_io(tm, norm_g)
    return _matmul(
        name, a, w, dn=NN, grid=(S // tm, 1, nj),
        a_spec=pl.BlockSpec((None, tm, r), lambda i, j, k: (k, i, 0)),
        b_spec=pl.BlockSpec((None, None, r, D_MODEL), lambda i, j, k: (l, k, 0, 0)),
        o_spec=o_spec, out_shape=out_shape(S), acc_shape=(tm, D_MODEL),
        extra=(h,) if norm_g is None else (h, norm_g), extra_specs=extra_specs, finish=_residual_finish)


def _dgrad_down(name, dh, w, l, dep=None):
    S = dh.shape[0]
    nj, r = w.shape[1], w.shape[2]
    tm = _tile(S, ROW_TILE)
    return _matmul(
        name, dh, w, dn=NT, grid=(nj, S // tm, 1),
        a_spec=pl.BlockSpec((tm, D_MODEL), lambda j, i, k: (i, 0)),
        b_spec=pl.BlockSpec((None, None, r, D_MODEL), lambda j, i, k: (l, j, 0, 0)),
        o_spec=pl.BlockSpec((None, tm, r), lambda j, i, k: (j, i, 0)),
        out_shape=_sds((nj, S, r), BF16), dep=dep)


def _wgrad_down(name, a, dh):
    nj, S, r = a.shape
    tk = _tile(S, ROW_TILE)
    return _matmul(
        name, a, dh, dn=TN, grid=(nj, 1, S // tk),
        a_spec=pl.BlockSpec((None, tk, r), lambda s, j, k: (s, k, 0)),
        b_spec=pl.BlockSpec((tk, D_MODEL), lambda s, j, k: (k, 0)),
        o_spec=pl.BlockSpec((None, r, D_MODEL), lambda s, j, k: (s, 0, 0)),
        out_shape=_sds((nj, r, D_MODEL), BF16), acc_shape=(r, D_MODEL))


def _norm_bwd_io(tm, S):
    row = pl.BlockSpec((tm, D_MODEL), lambda i, j, k: (i, 0))
    vec = pl.BlockSpec((1, D_MODEL), lambda i, j, k: (0, 0))
    return dict(extra_specs=(row, vec, row), o_spec=[row, vec], out_shape=[_sds((S, D_MODEL), F32), _sds((1, D_MODEL), F32)],
                finish=_norm_bwd_finish, sem=("arbitrary", "arbitrary", "arbitrary"))


def _dgrad_slots(name, dz, w, l, norm, transposed=False):
    nj, S, r = dz.shape
    _, blk, _, dn = _slot_weight(w, transposed)
    tm = _tile(S, ROW_TILE)
    return _matmul(
        name, dz, w, dn=dn, grid=(S // tm, 1, nj),
        a_spec=pl.BlockSpec((None, tm, r), lambda i, j, k: (k, i, 0)),
        b_spec=pl.BlockSpec(blk, lambda i, j, k: (l, k, 0, 0)),
        acc_shape=(tm, D_MODEL), extra=norm, **_norm_bwd_io(tm, S))


def _wgrad_slots(name, hn, dz, transposed=False):
    nj, S, r = dz.shape
    tk = _tile(S, ROW_TILE)
    hn_spec = pl.BlockSpec((tk, D_MODEL), lambda s, j, k: (k, 0))
    dz_spec = pl.BlockSpec((None, tk, r), lambda s, j, k: (s, k, 0))
    if transposed:
        return _matmul(
            name, dz, hn, dn=TN, grid=(nj, 1, S // tk), a_spec=dz_spec, b_spec=hn_spec,
            o_spec=pl.BlockSpec((None, r, D_MODEL), lambda s, j, k: (s, 0, 0)),
            out_shape=_sds((nj, r, D_MODEL), BF16), acc_shape=(r, D_MODEL))
    return _matmul(
        name, hn, dz, dn=TN, grid=(nj, 1, S // tk), a_spec=hn_spec, b_spec=dz_spec,
        o_spec=pl.BlockSpec((None, D_MODEL, r), lambda s, j, k: (s, 0, 0)),
        out_shape=_sds((nj, D_MODEL, r), BF16), acc_shape=(D_MODEL, r))


def _dgrad_out(name, dh, w, l, out_dtype, dep=None):
    S, K = dh.shape[0], w.shape[1]
    tm = _tile(S, ROW_TILE)
    return _matmul(
        name, dh, w, dn=NT, grid=(S // tm, 1, 1),
        a_spec=pl.BlockSpec((tm, D_MODEL), lambda i, j, k: (i, 0)),
        b_spec=pl.BlockSpec((None, K, D_MODEL), lambda i, j, k: (l, 0, 0)),
        o_spec=pl.BlockSpec((tm, K), lambda i, j, k: (i, 0)),
        out_shape=_sds((S, K), out_dtype), dep=dep)


def _wgrad_rows(name, a, b):
    S, K = a.shape
    tk = _tile(S, ROW_TILE)
    return _matmul(
        name, a, b, dn=TN, grid=(1, 1, S // tk),
        a_spec=pl.BlockSpec((tk, K), lambda i, j, k: (k, 0)),
        b_spec=pl.BlockSpec((tk, D_MODEL), lambda i, j, k: (k, 0)),
        o_spec=pl.BlockSpec((K, D_MODEL), lambda i, j, k: (0, 0)),
        out_shape=_sds((K, D_MODEL), BF16), acc_shape=(K, D_MODEL))


def _dgrad_rows(name, dz, wt, l, norm):
    S, N = dz.shape
    tm = _tile(S, ROW_TILE)
    return _matmul(
        name, dz, wt, dn=NN, grid=(S // tm, 1, 1),
        a_spec=pl.BlockSpec((tm, N), lambda i, j, k: (i, 0)),
        b_spec=pl.BlockSpec((None, N, D_MODEL), lambda i, j, k: (l, 0, 0)),
        extra=norm, **_norm_bwd_io(tm, S))


def _rmsnorm_fwd(name, h, g):
    S = h.shape[0]
    tm = _tile(S, ROW_TILE)

    def body(h_ref, g_ref, o_ref):
        x = h_ref[...]
        o_ref[...] = (x * _rms(x) * g_ref[...]).astype(o_ref.dtype)

    row = pl.BlockSpec((tm, D_MODEL), lambda i: (i, 0))
    return pl.pallas_call(
        body, name=name, grid=(S // tm,), in_specs=[row, pl.BlockSpec((1, D_MODEL), lambda i: (0, 0))],
        out_specs=row, out_shape=_sds((S, D_MODEL), BF16), compiler_params=_params(("parallel",)),
    )(h, g)


def _loss_head(name, h, g, target):
    S = h.shape[0]
    tm = _tile(S, ROW_TILE)

    def body(h_ref, g_ref, t_ref, dh_ref, dg_ref, ls_ref):
        x = h_ref[...]
        r = lax.rsqrt(jnp.mean(x * x, axis=-1, keepdims=True) + EPS)
        xh = x * r
        diff = xh * g_ref[...] - t_ref[...]
        dyf = diff * (1.0 / D_MODEL)
        dyg = dyf * g_ref[...]
        dh_ref[...] = r * (dyg - xh * jnp.mean(dyg * xh, axis=-1, keepdims=True))
        part = jnp.sum(dyf * xh, axis=0, keepdims=True)
        lpart = jnp.sum(diff * diff, axis=0, keepdims=True) * (0.5 / D_MODEL)

        @pl.when(pl.program_id(0) == 0)
        def _():
            dg_ref[...] = part
            ls_ref[...] = lpart

        @pl.when(pl.program_id(0) > 0)
        def _():
            dg_ref[...] += part
            ls_ref[...] += lpart

    row = pl.BlockSpec((tm, D_MODEL), lambda i: (i, 0))
    vec = pl.BlockSpec((1, D_MODEL), lambda i: (0, 0))
    return pl.pallas_call(
        body, name=name, grid=(S // tm,), in_specs=[row, vec, row], out_specs=[row, vec, vec],
        out_shape=[_sds((S, D_MODEL), F32), _sds((1, D_MODEL), F32), _sds((1, D_MODEL), F32)],
        compiler_params=_params(("arbitrary",)),
    )(h, g, target)


ATTN_SCALE = HEAD_DIM ** -0.5
ALIBI_SLOPES = [2.0 ** (-8.0 * (h + 1) / N_Q_HEADS) for h in range(N_Q_HEADS)]
K_COL = N_Q_HEADS * HEAD_DIM
KV_COLS = N_KV_HEADS * HEAD_DIM
V_COL = K_COL + KV_COLS


def _attn_masks(n):
    qi = lax.broadcasted_iota(jnp.int32, (ATTN_BLOCK, ATTN_BLOCK), 0)
    ki = lax.broadcasted_iota(jnp.int32, (ATTN_BLOCK, ATTN_BLOCK), 1)
    dist_c = (qi - ki).astype(F32)
    return dist_c + float(ATTN_BLOCK), dist_c, (ki > qi) & (n > 0), qi >= ki


def _attn_probs(raw_p, raw_c, sink, slope, masks):
    dist_p, dist_c, valid_p, valid_c = masks
    sp = jnp.where(valid_p, raw_p * ATTN_SCALE - slope * dist_p, NEG_BIG)
    sc = jnp.where(valid_c, raw_c * ATTN_SCALE - slope * dist_c, NEG_BIG)
    m = jnp.maximum(jnp.maximum(jnp.max(sp, axis=-1, keepdims=True), jnp.max(sc, axis=-1, keepdims=True)), sink)
    ep, ec, es = jnp.exp(sp - m), jnp.exp(sc - m), jnp.exp(sink - m)
    inv = 1.0 / (jnp.sum(ep, axis=-1, keepdims=True) + jnp.sum(ec, axis=-1, keepdims=True) + es)
    return ep * inv, ec * inv, es * inv


def _group_rows(ref, m):
    return jnp.concatenate([ref[:, HEAD_DIM * (Q_PER_KV * m + g):HEAD_DIM * (Q_PER_KV * m + g + 1)]
                            for g in range(Q_PER_KV)], axis=0)


def _head_rows(x, g):
    return x[ATTN_BLOCK * g:ATTN_BLOCK * (g + 1)]


def _attn_specs(nblk):
    last = nblk - 1
    kcol, vcol = K_COL // KV_COLS, V_COL // KV_COLS
    return [
        pl.BlockSpec((ATTN_BLOCK, K_COL), lambda n: (jnp.minimum(n, last), 0)),
        pl.BlockSpec((ATTN_BLOCK, KV_COLS), lambda n: (jnp.minimum(n, last), kcol)),
        pl.BlockSpec((ATTN_BLOCK, KV_COLS), lambda n: (jnp.maximum(jnp.minimum(n, last) - 1, 0), kcol)),
        pl.BlockSpec((ATTN_BLOCK, KV_COLS), lambda n: (jnp.minimum(n, last), vcol)),
        pl.BlockSpec((ATTN_BLOCK, KV_COLS), lambda n: (jnp.maximum(jnp.minimum(n, last) - 1, 0), vcol)),
    ]


P_COLS = 2 * ATTN_BLOCK


def _attn_fwd(name, proj, sinks):
    S = proj.shape[0]
    nblk = S // ATTN_BLOCK

    def body(q_ref, kc_ref, kp_ref, vc_ref, vp_ref, sk_ref, o_ref, p_ref, ps_ref):
        masks = _attn_masks(pl.program_id(0))
        lane = lax.broadcasted_iota(jnp.int32, (ATTN_BLOCK, 128), 1)
        sink_p = jnp.zeros((ATTN_BLOCK, 128), F32)
        for m in range(N_KV_HEADS):
            ks = slice(HEAD_DIM * m, HEAD_DIM * (m + 1))
            kp, kc, vp, vc = kp_ref[:, ks], kc_ref[:, ks], vp_ref[:, ks], vc_ref[:, ks]
            q4 = _group_rows(q_ref, m)
            raw_p, raw_c = _bdot(q4, kp, NT), _bdot(q4, kc, NT)
            pps, pcs = [], []
            for g in range(Q_PER_KV):
                hh = Q_PER_KV * m + g
                pp, pc, ps = _attn_probs(_head_rows(raw_p, g), _head_rows(raw_c, g), sk_ref[0, hh], ALIBI_SLOPES[hh], masks)
                pps.append(pp.astype(BF16))
                pcs.append(pc.astype(BF16))
                p_ref[:, P_COLS * hh:P_COLS * hh + ATTN_BLOCK] = pps[g]
                p_ref[:, P_COLS * hh + ATTN_BLOCK:P_COLS * (hh + 1)] = pcs[g]
                sink_p = jnp.where(lane == hh, ps, sink_p)
            o4 = _bdot(jnp.concatenate(pps, axis=0), vp, NN) + _bdot(jnp.concatenate(pcs, axis=0), vc, NN)
            for g in range(Q_PER_KV):
                hh = Q_PER_KV * m + g
                o_ref[:, HEAD_DIM * hh:HEAD_DIM * (hh + 1)] = _head_rows(o4, g).astype(o_ref.dtype)
        ps_ref[...] = sink_p

    row = lambda cols: pl.BlockSpec((ATTN_BLOCK, cols), lambda n: (n, 0))
    return pl.pallas_call(
        body, name=name, grid=(nblk,),
        in_specs=_attn_specs(nblk) + [pl.BlockSpec(memory_space=pltpu.SMEM)],
        out_specs=[row(K_COL), row(N_Q_HEADS * P_COLS), row(128)],
        out_shape=[_sds((S, K_COL), BF16), _sds((S, N_Q_HEADS * P_COLS), BF16), _sds((S, 128), F32)],
        compiler_params=_params(("parallel",)),
    )(proj, proj, proj, proj, proj, sinks)


def _attn_bwd(name, proj, probs, sink_probs, do):
    S = proj.shape[0]
    nblk = S // ATTN_BLOCK

    def body(q_ref, kc_ref, kp_ref, vc_ref, vp_ref, do_ref, p_ref, ps_ref, dz_ref, ds_ref, carry, cur, padd):
        n = pl.program_id(0)

        @pl.when(n == 0)
        def _():
            carry[...] = jnp.zeros_like(carry)
            ds_ref[...] = jnp.zeros_like(ds_ref)

        @pl.when(n < nblk)
        def _():
            lane = lax.broadcasted_iota(jnp.int32, (ATTN_BLOCK, 128), 1)
            sink_p = ps_ref[...]
            dsv = jnp.zeros((1, 128), F32)
            for m in range(N_KV_HEADS):
                ks = slice(HEAD_DIM * m, HEAD_DIM * (m + 1))
                kp, kc, vp, vc = kp_ref[:, ks], kc_ref[:, ks], vp_ref[:, ks], vc_ref[:, ks]
                q4, do4 = _group_rows(q_ref, m), _group_rows(do_ref, m)
                dpp4, dpc4 = _bdot(do4, vp, NT), _bdot(do4, vc, NT)
                pps, pcs, dsps, dscs = [], [], [], []
                for g in range(Q_PER_KV):
                    hh = Q_PER_KV * m + g
                    pps.append(p_ref[:, P_COLS * hh:P_COLS * hh + ATTN_BLOCK])
                    pcs.append(p_ref[:, P_COLS * hh + ATTN_BLOCK:P_COLS * (hh + 1)])
                    pp, pc = pps[g].astype(F32), pcs[g].astype(F32)
                    dpp, dpc = _head_rows(dpp4, g), _head_rows(dpc4, g)
                    delta = jnp.sum(pp * dpp, axis=-1, keepdims=True) + jnp.sum(pc * dpc, axis=-1, keepdims=True)
                    dsv = dsv - jnp.sum(jnp.where(lane == hh, sink_p, 0.0) * delta, axis=0, keepdims=True)
                    dsps.append((pp * (dpp - delta)).astype(BF16))
                    dscs.append((pc * (dpc - delta)).astype(BF16))
                pp4, pc4 = jnp.concatenate(pps, axis=0), jnp.concatenate(pcs, axis=0)
                dsp4, dsc4 = jnp.concatenate(dsps, axis=0), jnp.concatenate(dscs, axis=0)
                dq4 = (_bdot(dsp4, kp, NN) + _bdot(dsc4, kc, NN)) * ATTN_SCALE
                for g in range(Q_PER_KV):
                    hh = Q_PER_KV * m + g
                    cur[:, HEAD_DIM * hh:HEAD_DIM * (hh + 1)] = _head_rows(dq4, g)
                cur[:, K_COL + HEAD_DIM * m:K_COL + HEAD_DIM * (m + 1)] = _bdot(dsc4, q4, TN) * ATTN_SCALE
                cur[:, V_COL + HEAD_DIM * m:V_COL + HEAD_DIM * (m + 1)] = _bdot(pc4, do4, TN)
                padd[:, ks] = _bdot(dsp4, q4, TN) * ATTN_SCALE
                padd[:, KV_COLS + HEAD_DIM * m:KV_COLS + HEAD_DIM * (m + 1)] = _bdot(pp4, do4, TN)
            ds_ref[...] += dsv
            dz_ref[:, :K_COL] = carry[:, :K_COL].astype(dz_ref.dtype)
            dz_ref[:, K_COL:] = (carry[:, K_COL:] + padd[...]).astype(dz_ref.dtype)
            carry[...] = cur[...]

        @pl.when(n == nblk)
        def _():
            dz_ref[...] = carry[...].astype(dz_ref.dtype)

    return pl.pallas_call(
        body, name=name, grid=(nblk + 1,),
        in_specs=_attn_specs(nblk) + [
            pl.BlockSpec((ATTN_BLOCK, cols), lambda n: (jnp.minimum(n, nblk - 1), 0))
            for cols in (K_COL, N_Q_HEADS * P_COLS, 128)],
        out_specs=[pl.BlockSpec((ATTN_BLOCK, ATTN_IN), lambda n: (jnp.maximum(n - 1, 0), 0)),
                   pl.BlockSpec((1, 128), lambda n: (0, 0))],
        out_shape=[_sds((S, ATTN_IN), BF16), _sds((1, 128), F32)],
        scratch_shapes=[pltpu.VMEM((ATTN_BLOCK, ATTN_IN), F32), pltpu.VMEM((ATTN_BLOCK, ATTN_IN), F32),
                        pltpu.VMEM((ATTN_BLOCK, 2 * KV_COLS), F32)],
        compiler_params=_params(("arbitrary",)),
    )(proj, proj, proj, proj, proj, do, probs, sink_probs)


def _hg_consts():
    C = HG_CHUNK
    tri = np.tril(np.ones((C, C)))
    t = np.arange(C)
    rows, masks = [tri], []
    for lvl in range(HG_LEVELS):
        n = C >> (lvl + 1)
        sel = np.zeros((C, C))
        sel[t, (t // (2 * n)) * (2 * n) + n - 1] = 1.0
        rows.append(sel @ tri)
        tt, ss = t[:, None], t[None, :]
        masks.append((tt // (2 * n) == ss // (2 * n)) & ((tt // n) % 2 == 1) & ((ss // n) % 2 == 0))
    masks.append(np.eye(C, dtype=bool))
    stk = np.concatenate(rows, axis=0)
    return jnp.asarray(stk, BF16), jnp.asarray(np.stack(masks), F32)


def _sigmoid(x):
    return 1.0 / (1.0 + jnp.exp(-x))


def _split(x, parts):
    out, rest = [], x
    for _ in range(parts):
        out.append(rest.astype(BF16))
        rest = rest - out[-1].astype(F32)
    return out


def _dot01(m01, x, dn, parts=3):
    return sum(lax.dot_general(m01, p, dn, preferred_element_type=F32) for p in _split(x, parts))


def _ref_rows(b, n):
    C = b.shape[1]
    if 2 * n >= 8:
        b3 = b.reshape(HG_CHUNK // (2 * n), 2 * n, C)
        return jnp.broadcast_to(b3[:, n - 1:n, :], b3.shape).reshape(HG_CHUNK, C)
    pos = lax.broadcasted_iota(jnp.int32, b.shape, 0) % (2 * n)
    out = b
    for p in range(2 * n):
        if p != n - 1:
            out = jnp.where(pos == p, pltpu.roll(b, (p - (n - 1)) % HG_CHUNK, 0), out)
    return out


HG_STEP_CHUNKS = 4


def _chunk_rows(ci):
    return pl.ds(pl.multiple_of(ci * HG_CHUNK, HG_CHUNK), HG_CHUNK)


def _hg_common(z_ref, rows, lb_ref, stk_ref):
    qr, fr = z_ref[0, rows, :], z_ref[1, rows, :]
    lb = lb_ref[...]
    sq, sg, sgn = _sigmoid(qr), _sigmoid(fr), _sigmoid(-fr)
    ft = lb + (1.0 - lb) * sg
    b = _dot01(stk_ref[0:HG_CHUNK, :], jnp.log(ft), NN)
    ws = [jnp.exp(-jnp.abs(b - _ref_rows(b, HG_CHUNK >> (l + 1)))) for l in range(HG_LEVELS)]
    blast = b[HG_CHUNK - 1:HG_CHUNK]
    return dict(qr=qr, fr=fr, lb=lb, sq=sq, sg=sg, sgn=sgn, ft=ft, q=qr * sq, kk=(1.0 - lb) * sgn, b=b,
                ws=ws, eb=jnp.exp(b), ed=jnp.exp(blast - b), elast=jnp.exp(blast))


def _hg_factors(qh, kh, ws, sl):
    return ([(qh * ws[l][:, sl]).astype(BF16) for l in range(HG_LEVELS)],
            [(kh * ws[l][:, sl]).astype(BF16) for l in range(HG_LEVELS)])


def _hg_intra(qh, kh, ws, msk_ref, sl):
    qls, kls = _hg_factors(qh, kh, ws, sl)
    a = msk_ref[HG_LEVELS] * _bdot(qh, kh, NT)
    for l in range(HG_LEVELS):
        a = a + msk_ref[l] * _bdot(qls[l], kls[l], NT)
    return a


def _hg_fwd(name, z, lb, ng):
    S = z.shape[2]
    nc = S // HG_CHUNK
    per = min(HG_STEP_CHUNKS, nc)
    stk, msk = _hg_consts()

    def body(z_ref, lb_ref, ng_ref, stk_ref, msk_ref, og_ref, st_ref, a_ref, o_ref, state):
        @pl.when(pl.program_id(1) == 0)
        def _():
            state[...] = jnp.zeros_like(state)

        def chunk(ci, _):
            rows = _chunk_rows(ci)
            cm = _hg_common(z_ref, rows, lb_ref, stk_ref)
            v, gt = z_ref[2, rows, :], z_ref[3, rows, :]
            kd = cm["kk"] * cm["ed"]
            for hh in range(4):
                sl = slice(HG_K * hh, HG_K * (hh + 1))
                st = state[hh]
                st_ref[ci, hh] = st
                qh, kh, vh = cm["q"][:, sl], cm["kk"][:, sl], v[:, sl]
                a = _hg_intra(qh, kh, cm["ws"], msk_ref, sl).astype(BF16)
                a_ref[ci, hh] = a
                o = _bdot(a, vh, NN) + _bdot(qh * cm["eb"][:, sl], st, NT)
                o_ref[rows, sl] = o
                state[hh] = cm["elast"][:, sl] * st + _bdot(vh, kd[:, sl], TN)
                gh = gt[:, sl]
                og_ref[rows, sl] = (o * _rms(o) * ng_ref[...] * (gh * _sigmoid(gh))).astype(og_ref.dtype)
            return 0

        lax.fori_loop(0, per, chunk, 0, unroll=True)

    return pl.pallas_call(
        body, name=name, grid=(2, nc // per),
        in_specs=[pl.BlockSpec((4, None, per * HG_CHUNK, HG_SLOT), lambda g, c: (0, g, c, 0)),
                  pl.BlockSpec((1, HG_SLOT), lambda g, c: (0, g)),
                  pl.BlockSpec((1, HG_K), lambda g, c: (0, 0)),
                  pl.BlockSpec(stk.shape, lambda g, c: (0, 0)),
                  pl.BlockSpec(msk.shape, lambda g, c: (0, 0, 0))],
        out_specs=[pl.BlockSpec((per * HG_CHUNK, HG_SLOT), lambda g, c: (c, g)),
                   pl.BlockSpec((per, 4, HG_K, HG_K), lambda g, c: (c, g, 0, 0)),
                   pl.BlockSpec((per, 4, HG_CHUNK, HG_CHUNK), lambda g, c: (c, g, 0, 0)),
                   pl.BlockSpec((per * HG_CHUNK, HG_SLOT), lambda g, c: (c, g))],
        out_shape=[_sds((S, D_MODEL), BF16), _sds((nc, HG_HEADS, HG_K, HG_K), F32),
                   _sds((nc, HG_HEADS, HG_CHUNK, HG_CHUNK), BF16), _sds((S, D_MODEL), F32)],
        scratch_shapes=[pltpu.VMEM((4, HG_K, HG_K), F32)],
        compiler_params=_params(("parallel", "arbitrary")),
    )(z, lb, ng, stk, msk)


def _hg_bwd(name, z, lb, ng, states, intra, o_pre, dog):
    S = z.shape[2]
    nc = S // HG_CHUNK
    per = min(HG_STEP_CHUNKS, nc)
    stk, msk = _hg_consts()

    def body(z_ref, lb_ref, ng_ref, stk_ref, msk_ref, st_ref, a_ref, o_ref, dog_ref, dz_ref, dlb_ref, dng_ref, dstate):
        @pl.when(pl.program_id(1) == 0)
        def _():
            dstate[...] = jnp.zeros_like(dstate)
            dlb_ref[...] = jnp.zeros_like(dlb_ref)
            dng_ref[...] = jnp.zeros_like(dng_ref)

        def chunk(k, _):
            ci = per - 1 - k
            rows = _chunk_rows(ci)
            cm = _hg_common(z_ref, rows, lb_ref, stk_ref)
            v, gt = z_ref[2, rows, :], z_ref[3, rows, :]
            ng = ng_ref[...]
            kd = cm["kk"] * cm["ed"]
            row = lax.broadcasted_iota(jnp.int32, (HG_CHUNK, 1), 0)
            dng = jnp.zeros((1, HG_K), F32)
            dq_h, dkk_h, db_h, dv_h, dgt_h = [], [], [], [], []
            dr_h = [[] for _ in range(HG_LEVELS)]
            for hh in range(4):
                sl = slice(HG_K * hh, HG_K * (hh + 1))
                st, dst = st_ref[ci, hh], dstate[hh]
                qh, kh, vh, ebh, edh, kdh = cm["q"][:, sl], cm["kk"][:, sl], v[:, sl], cm["eb"][:, sl], cm["ed"][:, sl], kd[:, sl]
                elh = cm["elast"][:, sl]
                qls, kls = _hg_factors(qh, kh, cm["ws"], sl)
                a, o = a_ref[ci, hh], o_ref[rows, sl]
                qe = qh * ebh
                r = _rms(o)
                xh = o * r
                gh = gt[:, sl]
                sgg = _sigmoid(gh)
                dog = dog_ref[rows, sl].astype(F32)
                dy = dog * (gh * sgg)
                dgt_h.append(dog * (xh * ng) * (sgg * (1.0 + gh * (1.0 - sgg))))
                dng = dng + jnp.sum(dy * xh, axis=0, keepdims=True)
                dyg = dy * ng
                do = r * (dyg - xh * jnp.mean(dyg * xh, axis=-1, keepdims=True))
                da = _bdot(do, vh, NT)
                dv_h.append(_bdot(a, do, TN) + _bdot(kdh, dst, NT))
                dkd = _bdot(vh, dst, NN)
                delast = jnp.sum(st * dst, axis=0, keepdims=True)
                dqe = _bdot(do, st, NN)
                dstate[hh] = elh * dst + _bdot(do, qe, TN)
                gk = dkd * kdh
                dblast = jnp.sum(gk, axis=0, keepdims=True) + delast * elh
                db = dqe * qe - gk + jnp.where(row == HG_CHUNK - 1, dblast, 0.0)
                dp = (msk_ref[HG_LEVELS] * da).astype(BF16)
                dq = dqe * ebh + _bdot(dp, kh, NN)
                dkk = dkd * edh + _bdot(dp, qh, TN)
                for l in range(HG_LEVELS):
                    dp = (msk_ref[l] * da).astype(BF16)
                    dql, dkl = _bdot(dp, kls[l], NN), _bdot(dp, qls[l], TN)
                    w = cm["ws"][l][:, sl]
                    dq = dq + dql * w
                    dkk = dkk + dkl * w
                    half = jnp.where(((row >> (HG_LEVELS - 1 - l)) & 1) == 1, 1.0, -1.0)
                    dd = half * w * (dql * qh + dkl * kh)
                    db = db + dd
                    dr_h[l].append(-dd)
                dq_h.append(dq)
                dkk_h.append(dkk)
                db_h.append(db)
            cat = lambda xs: jnp.concatenate(xs, axis=1)
            cot = jnp.concatenate([cat(db_h)] + [cat(dr_h[l]) for l in range(HG_LEVELS)], axis=0)
            dlf = _dot01(stk_ref[...], cot, TN, parts=2)
            dq, dkk = cat(dq_h), cat(dkk_h)
            dft = dlf / cm["ft"]
            one_lb = 1.0 - cm["lb"]
            dz_ref[0, rows, :] = (dq * (cm["sq"] * (1.0 + cm["qr"] * (1.0 - cm["sq"])))).astype(dz_ref.dtype)
            dz_ref[1, rows, :] = ((dft - dkk) * one_lb * cm["sg"] * cm["sgn"]).astype(dz_ref.dtype)
            dz_ref[2, rows, :] = cat(dv_h).astype(dz_ref.dtype)
            dz_ref[3, rows, :] = cat(dgt_h).astype(dz_ref.dtype)
            dlb_ref[...] += jnp.sum((dft - dkk) * cm["sgn"], axis=0, keepdims=True)
            dng_ref[...] += dng
            return 0

        lax.fori_loop(0, per, chunk, 0, unroll=True)

    rev = lambda c: nc // per - 1 - c
    rows_blk = pl.BlockSpec((per * HG_CHUNK, HG_SLOT), lambda g, c: (rev(c), g))
    return pl.pallas_call(
        body, name=name, grid=(2, nc // per),
        in_specs=[pl.BlockSpec((4, None, per * HG_CHUNK, HG_SLOT), lambda g, c: (0, g, rev(c), 0)),
                  pl.BlockSpec((1, HG_SLOT), lambda g, c: (0, g)),
                  pl.BlockSpec((1, HG_K), lambda g, c: (0, 0)),
                  pl.BlockSpec(stk.shape, lambda g, c: (0, 0)),
                  pl.BlockSpec(msk.shape, lambda g, c: (0, 0, 0)),
                  pl.BlockSpec((per, 4, HG_K, HG_K), lambda g, c: (rev(c), g, 0, 0)),
                  pl.BlockSpec((per, 4, HG_CHUNK, HG_CHUNK), lambda g, c: (rev(c), g, 0, 0)),
                  rows_blk, rows_blk],
        out_specs=[pl.BlockSpec((4, None, per * HG_CHUNK, HG_SLOT), lambda g, c: (0, g, rev(c), 0)),
                   pl.BlockSpec((1, HG_SLOT), lambda g, c: (0, g)),
                   pl.BlockSpec((None, 1, HG_K), lambda g, c: (g, 0, 0))],
        out_shape=[_sds(z.shape, BF16), _sds((1, 2 * HG_SLOT), F32), _sds((2, 1, HG_K), F32)],
        scratch_shapes=[pltpu.VMEM((4, HG_K, HG_K), F32)],
        compiler_params=_params(("parallel", "arbitrary")),
    )(z, lb, ng, stk, msk, states, intra, o_pre, dog)


def _lb_fwd(name, logits):
    def body(l_ref, o_ref):
        x = l_ref[...]
        e = jnp.exp(x - jnp.max(x, axis=0, keepdims=True))
        s = e / jnp.sum(e, axis=0, keepdims=True)
        o_ref[0:1, :] = s[1:2]
        o_ref[1:2, :] = s[1:2] + s[2:3] + s[3:4]

    return pl.pallas_call(body, name=name, out_shape=_sds((2, logits.shape[1]), F32))(logits)


def _lb_bwd(name, logits, dlb):
    def body(l_ref, d_ref, o_ref):
        x = l_ref[...]
        e = jnp.exp(x - jnp.max(x, axis=0, keepdims=True))
        s = e / jnp.sum(e, axis=0, keepdims=True)
        d1, d3 = d_ref[0:1, :], d_ref[1:2, :]
        ds = [jnp.zeros_like(d1), d1 + d3, d3, d3]
        dot = sum(ds[r] * s[r:r + 1] for r in range(1, DEPTH))
        for r in range(DEPTH):
            o_ref[r:r + 1, :] = s[r:r + 1] * (ds[r] - dot)

    return pl.pallas_call(body, name=name, out_shape=_sds(logits.shape, F32))(logits, dlb)


SUB = 8


def _rows_down(x, prev, k):
    row = lax.broadcasted_iota(jnp.int32, x.shape, 0)
    return jnp.where(row >= k, pltpu.roll(x, k, 0), pltpu.roll(prev, k, 0))


def _rows_up(x, nxt, k):
    row = lax.broadcasted_iota(jnp.int32, x.shape, 0)
    return jnp.where(row < SUB - k, pltpu.roll(x, SUB - k, 0), pltpu.roll(nxt, SUB - k, 0))


def _conv_block(w_ref, b_ref, p, x, prev):
    return (b_ref[p] + w_ref[p, 0:1, :] * _rows_down(x, prev, 2) + w_ref[p, 1:2, :] * _rows_down(x, prev, 1)
            + w_ref[p, 2:3, :] * x)


def _convgate_fwd(name, u, cw, cb):
    S = u.shape[2]
    tm = _tile(S, ROW_TILE)

    def body(u_ref, w_ref, b_ref, a_ref, c_ref, halo):
        @pl.when(pl.program_id(1) == 0)
        def _():
            halo[...] = jnp.zeros_like(halo)

        def step(r, prev):
            pg, pv = prev
            out, cgs, cvs = [], [], []
            for s in range(2):
                rows = pl.ds(pl.multiple_of(r * 2 * SUB + s * SUB, SUB), SUB)
                xg, xv = u_ref[0, rows, :], u_ref[1, rows, :]
                cgs.append(_conv_block(w_ref, b_ref, 0, xg, pg))
                cvs.append(_conv_block(w_ref, b_ref, 1, xv, pv))
                out.append(cgs[s] * _sigmoid(cgs[s]) * cvs[s])
                pg, pv = xg, xv
            rows = pl.ds(pl.multiple_of(r * 2 * SUB, 2 * SUB), 2 * SUB)
            a_ref[rows, :] = jnp.concatenate(out, axis=0).astype(a_ref.dtype)
            c_ref[0, rows, :] = jnp.concatenate(cgs, axis=0).astype(c_ref.dtype)
            c_ref[1, rows, :] = jnp.concatenate(cvs, axis=0).astype(c_ref.dtype)
            return pg, pv

        pg, pv = lax.fori_loop(0, tm // (2 * SUB), step, (halo[0], halo[1]))
        halo[0] = pg
        halo[1] = pv

    pair = pl.BlockSpec((2, None, tm, FF_SLOT), lambda j, t: (0, j, t, 0))
    return pl.pallas_call(
        body, name=name, grid=(4, S // tm),
        in_specs=[pair, pl.BlockSpec((2, None, 3, FF_SLOT), lambda j, t: (0, j, 0, 0)),
                  pl.BlockSpec((2, None, 1, FF_SLOT), lambda j, t: (0, j, 0, 0))],
        out_specs=[pl.BlockSpec((None, tm, FF_SLOT), lambda j, t: (j, t, 0)), pair],
        out_shape=[_sds((4, S, FF_SLOT), BF16), _sds(u.shape, BF16)],
        scratch_shapes=[pltpu.VMEM((2, SUB, FF_SLOT), F32)],
        compiler_params=_params(("parallel", "arbitrary")),
    )(u, cw, cb)


def _convgate_bwd(name, u, convs, cw, da):
    S = u.shape[2]
    tm = _tile(S, ROW_TILE)
    nt = S // tm

    def body(u_ref, c_ref, w_ref, da_ref, du_out, dw_ref, db_ref, after, first, acc, du_ref):
        @pl.when(pl.program_id(1) == 0)
        def _():
            after[...] = jnp.zeros_like(after)
            acc[...] = jnp.zeros_like(acc)

        def finish(p, x, d, nxt, rows):
            taps = (_rows_up(d, nxt, 2), _rows_up(d, nxt, 1), d)
            du_ref[p, rows, :] = w_ref[p, 0:1, :] * taps[0] + w_ref[p, 1:2, :] * taps[1] + w_ref[p, 2:3, :] * d
            for j in range(3):
                acc[p, j] += taps[j] * x
            acc[p, 3] += d

        def step(r, carry):
            xg_last, xv_last, dg_last, dv_last = carry
            rows16 = pl.ds(pl.multiple_of(r * 2 * SUB, 2 * SUB), 2 * SUB)
            dav = da_ref[rows16, :].astype(F32)
            cg16, cv16 = c_ref[0, rows16, :].astype(F32), c_ref[1, rows16, :].astype(F32)
            for s in range(2):
                at = r * 2 * SUB + s * SUB
                part = slice(s * SUB, (s + 1) * SUB)
                cg, cv, dab = cg16[part], cv16[part], dav[part]
                sg = _sigmoid(cg)
                dg = dab * cv * (sg * (1.0 + cg * (1.0 - sg)))
                dv = dab * cg * sg
                before = pl.ds(pl.multiple_of(at - SUB, SUB), SUB)
                if s == 0:
                    @pl.when(r == 0)
                    def _():
                        first[0] = dg
                        first[1] = dv

                    @pl.when(r > 0)
                    def _():
                        finish(0, xg_last, dg_last, dg, before)
                        finish(1, xv_last, dv_last, dv, before)
                else:
                    finish(0, xg_last, dg_last, dg, before)
                    finish(1, xv_last, dv_last, dv, before)
                rows = pl.ds(pl.multiple_of(at, SUB), SUB)
                xg_last, xv_last, dg_last, dv_last = u_ref[0, rows, :], u_ref[1, rows, :], dg, dv
            return xg_last, xv_last, dg_last, dv_last

        zero = jnp.zeros((SUB, FF_SLOT), F32)
        xg_last, xv_last, dg_last, dv_last = lax.fori_loop(0, tm // (2 * SUB), step, (zero, zero, zero, zero))
        finish(0, xg_last, dg_last, after[0], slice(tm - SUB, tm))
        finish(1, xv_last, dv_last, after[1], slice(tm - SUB, tm))
        du_out[...] = du_ref[...].astype(du_out.dtype)
        after[...] = first[...]
        for p in range(2):
            for j in range(3):
                dw_ref[p, j:j + 1, :] = jnp.sum(acc[p, j], axis=0, keepdims=True)
            db_ref[p] = jnp.sum(acc[p, 3], axis=0, keepdims=True)

    rev = lambda t: nt - 1 - t
    pair = pl.BlockSpec((2, None, tm, FF_SLOT), lambda j, t: (0, j, rev(t), 0))
    taps = pl.BlockSpec((2, None, 3, FF_SLOT), lambda j, t: (0, j, 0, 0))
    bias = pl.BlockSpec((2, None, 1, FF_SLOT), lambda j, t: (0, j, 0, 0))
    return pl.pallas_call(
        body, name=name, grid=(4, nt),
        in_specs=[pair, pair, taps, pl.BlockSpec((None, tm, FF_SLOT), lambda j, t: (j, rev(t), 0))],
        out_specs=[pair, taps, bias],
        out_shape=[_sds(u.shape, BF16), _sds(cw.shape, F32), _sds((2, 4, 1, FF_SLOT), F32)],
        scratch_shapes=[pltpu.VMEM((2, SUB, FF_SLOT), F32), pltpu.VMEM((2, SUB, FF_SLOT), F32),
                        pltpu.VMEM((2, 4, SUB, FF_SLOT), F32), pltpu.VMEM((2, tm, FF_SLOT), F32)],
        compiler_params=_params(("parallel", "arbitrary")),
    )(u, convs, cw, da)


def _row_tile(R):
    for t in range(256, 15, -16):
        if R % t == 0:
            return t
    return R


def _adamw(name, gsrcs, w, m, v, dep=None):
    L = len(gsrcs)
    n, A, C = gsrcs[0].shape
    tr = _row_tile(A)
    deps = () if dep is None else (dep,)

    def body(*refs):
        g_refs = refs[:L]
        w_ref, m_ref, v_ref = refs[L:L + 3]
        go_ref, d_ref, mo_ref, vo_ref = refs[L + 3 + len(deps):]
        for k in range(L):
            @pl.when(pl.program_id(0) == k)
            def _(k=k):
                g = g_refs[k][0].astype(F32)
                for s in range(1, n):
                    g = g + g_refs[k][s].astype(F32)
                m2 = ADAM_B1 * m_ref[...] + (1.0 - ADAM_B1) * g
                v2 = ADAM_B2 * v_ref[...] + (1.0 - ADAM_B2) * (g * g)
                m_hat = m2 / (1.0 - ADAM_B1 ** ADAM_STEP)
                v_hat = v2 / (1.0 - ADAM_B2 ** ADAM_STEP)
                go_ref[...] = g
                d_ref[...] = -ADAM_LR * (m_hat / (jnp.sqrt(v_hat) + ADAM_EPS) + ADAM_WD * w_ref[...])
                mo_ref[...] = m2
                vo_ref[...] = v2

    g_specs = [pl.BlockSpec((n, tr, C), lambda l, i, k=k: (0, jnp.where(l == k, i, 0), 0)) for k in range(L)]
    blk = pl.BlockSpec((None, tr, C), lambda l, i: (l, i, 0))
    return pl.pallas_call(
        body, name=name, grid=(L, A // tr), in_specs=g_specs + [blk, blk, blk] + [_dep_spec(2)] * len(deps),
        out_specs=[blk] * 4, out_shape=[_sds((L, A, C), F32)] * 4, compiler_params=_params(("parallel", "parallel")),
    )(*gsrcs, w, m, v, *deps)


MESH = pl.DeviceIdType.MESH
HBM_SPEC = pl.BlockSpec(memory_space=pltpu.HBM)
N_PEERS = N_DEV - 1


def _mesh_place():
    x, y, c = lax.axis_index("x"), lax.axis_index("y"), lax.axis_index("c")
    peers = []
    for p in range(1, N_DEV):
        px = 1 - x if p & 4 else x
        py = 1 - y if p & 2 else y
        pc = 1 - c if p & 1 else c
        peers.append(((px, py, pc), 4 * px + 2 * py + pc))
    return 4 * x + 2 * y + c, peers


SEM_SPEC = pl.BlockSpec(memory_space=pltpu.SEMAPHORE)
ANY_SPEC = pl.BlockSpec(memory_space=pl.ANY)
EFFECT = pltpu.SideEffectType.DATAFLOW_SIDE_EFFECTING


def _exchange_refs(scatter, src, land, send, recv, k, p, dev, idx, me):
    return pltpu.make_async_remote_copy(src_ref=src[k].at[idx] if scatter else src[k], dst_ref=land[k].at[me],
                                        send_sem=send.at[k * N_PEERS + p], recv_sem=recv.at[k * N_PEERS + p], device_id=dev,
                                        device_id_type=MESH)


def _exchange_start(name, srcs, scatter, gate):
    n = len(srcs)
    lands = [lax.empty(s.shape if scatter else (N_DEV,) + s.shape, s.dtype) for s in srcs]

    def body(*refs):
        src, land = refs[:n], refs[n:2 * n]
        send, recv, own = refs[2 * n + 1:2 * n + 4]
        token = refs[-1]
        me, peers = _mesh_place()
        for k in range(n):
            pltpu.make_async_copy(src[k].at[me] if scatter else src[k], land[k].at[me], own.at[k]).start()
            for p, (dev, idx) in enumerate(peers):
                _exchange_refs(scatter, src, land, send, recv, k, p, dev, idx, me).start()
        token[...] = jnp.zeros_like(token)

    hbm = lambda a: pltpu.HBM(a.shape, a.dtype)
    outs = pl.pallas_call(
        body, name=name,
        out_shape=(pltpu.SemaphoreType.DMA((n * N_PEERS,)), pltpu.SemaphoreType.DMA((n * N_PEERS,)),
                   pltpu.SemaphoreType.DMA((n,)), *[hbm(s) for s in srcs], *[hbm(s) for s in lands], _sds(DEP_SHAPE, F32)),
        in_specs=[HBM_SPEC] * (2 * n) + [ANY_SPEC],
        out_specs=(SEM_SPEC, SEM_SPEC, SEM_SPEC, *[HBM_SPEC] * (2 * n), pl.BlockSpec(memory_space=pltpu.VMEM)),
        input_output_aliases={j: 3 + j for j in range(2 * n)},
        compiler_params=pltpu.CompilerParams(has_side_effects=EFFECT),
    )(*[pltpu.with_memory_space_constraint(s, pltpu.HBM) for s in srcs],
      *[pltpu.with_memory_space_constraint(s, pltpu.HBM) for s in lands], gate)
    return outs[:3], None, list(outs[3:3 + n]), list(outs[3 + n:3 + 2 * n]), outs[-1]


def _exchange_wait(name, started, scatter, after):
    (send, recv, own), _, srcs, lands, _ = started
    n = len(srcs)

    def body(*refs):
        src, land = refs[:n], refs[n:2 * n]
        send, recv, own = refs[2 * n:2 * n + 3]
        me, peers = _mesh_place()
        for k in range(n):
            pltpu.make_async_copy(src[k].at[me] if scatter else src[k], land[k].at[me], own.at[k]).wait()
            for p, (dev, idx) in enumerate(peers):
                cp = pltpu.make_async_remote_copy(src_ref=src[k].at[idx] if scatter else src[k], dst_ref=land[k].at[idx],
                                                  send_sem=send.at[k * N_PEERS + p], recv_sem=recv.at[k * N_PEERS + p], device_id=dev,
                                                  device_id_type=MESH)
                cp.wait_send()
                cp.wait_recv()

    hbm = lambda a: pltpu.HBM(a.shape, a.dtype)
    outs = pl.pallas_call(
        body, name=name, out_shape=(*[hbm(s) for s in srcs], *[hbm(s) for s in lands]),
        in_specs=[HBM_SPEC] * (2 * n) + [SEM_SPEC, SEM_SPEC, SEM_SPEC, ANY_SPEC], out_specs=tuple([HBM_SPEC] * (2 * n)),
        input_output_aliases={j: j for j in range(2 * n)},
        compiler_params=pltpu.CompilerParams(has_side_effects=EFFECT),
    )(*srcs, *lands, send, recv, own, after)
    return list(outs[n:])


def _sum_devices(name, parts):
    def body(p_ref, o_ref):
        tot = p_ref[0]
        for j in range(1, N_DEV):
            tot = tot + p_ref[j]
        o_ref[...] = tot

    return pl.pallas_call(body, name=name, out_shape=_sds(parts.shape[1:], F32),
                          compiler_params=pltpu.CompilerParams(vmem_limit_bytes=VMEM_LIMIT))(parts)


def _rows(a, width=D_MODEL):
    flat = a.reshape(-1)
    return jnp.pad(flat, (0, (-flat.shape[0]) % width)).reshape(-1, width)


def _pack_rows(parts):
    blocks = []
    for p in parts:
        r = _rows(p)
        blocks.append(jnp.pad(r, ((0, (-r.shape[0]) % 8), (0, 0))))
    return jnp.concatenate(blocks, axis=0)


def _unpack_rows(rows, shapes):
    out, at = [], 0
    for s in shapes:
        size = int(np.prod(s))
        n = -(-size // D_MODEL)
        out.append(rows[at:at + n].reshape(-1)[:size].reshape(s))
        at += -(-n // 8) * 8
    return out


def kernel(x, norm_mix, norm_ffn, norm_final, attn_w_in, attn_w_out, attn_sinks, hgrn_w_in, hgrn_w_out, hgrn_norm, hgrn_lb_logits, ffn_w_up, ffn_conv_w, ffn_conv_b, ffn_w_down, loss_target, m_norm_mix, m_norm_ffn, m_norm_final, m_attn_w_in, m_attn_w_out, m_attn_sinks, m_hgrn_w_in, m_hgrn_w_out, m_hgrn_norm, m_hgrn_lb_logits, m_ffn_w_up, m_ffn_conv_w, m_ffn_conv_b, m_ffn_w_down, v_norm_mix, v_norm_ffn, v_norm_final, v_attn_w_in, v_attn_w_out, v_attn_sinks, v_hgrn_w_in, v_hgrn_w_out, v_hgrn_norm, v_hgrn_lb_logits, v_ffn_w_up, v_ffn_conv_w, v_ffn_conv_b, v_ffn_w_down):
    S = x.shape[1]
    n_attn, n_hgrn = attn_w_in.shape[0], hgrn_w_in.shape[0]
    me = 4 * lax.axis_index("x") + 2 * lax.axis_index("y") + lax.axis_index("c")

    wa_in_t, wa_out_b = attn_w_in.transpose(0, 2, 1).astype(BF16), attn_w_out.astype(BF16)
    wh_in_b, wh_out_b = hgrn_w_in.astype(BF16), hgrn_w_out.astype(BF16)
    wf_up_b, wf_down_b = ffn_w_up.transpose(0, 2, 1).astype(BF16), ffn_w_down.astype(BF16)
    conv_b = ffn_conv_b.reshape(DEPTH, 2, 4, 1, FF_SLOT)
    lb = _lb_fwd("lb_fwd", hgrn_lb_logits)

    def unit_shards(l, part):
        if part == "ffn":
            return [wf_up_b[l], wf_down_b[l], ffn_conv_w[l]]
        return [wa_in_t[l // 2], wa_out_b[l // 2]] if l % 2 == 0 else [wh_in_b[l // 2], wh_out_b[l // 2]]

    def unit_weights(l, part, w):
        if part == "ffn":
            return w[0][None], w[1].reshape(1, 4, FF_SLOT, D_MODEL), w[2].reshape(2, 4, 3, FF_SLOT)
        if l % 2 == 0:
            return w[0].reshape(1, ATTN_IN, D_MODEL), w[1].reshape(1, D_MODEL, D_MODEL)
        return w[0][None], w[1].reshape(1, D_MODEL, D_MODEL)

    units = [(l, part) for l in range(DEPTH) for part in ("mix", "ffn")]
    gathers = [_exchange_start("gather_start0", unit_shards(*units[0]), False, norm_final)]
    gathers.append(_exchange_start("gather_start1", unit_shards(*units[1]), False, gathers[0][4]))
    arrived = _exchange_wait("gather_wait0", gathers[0], False, gathers[1][4])
    weights, saved = {}, [dict() for _ in range(DEPTH)]
    h = x[0]
    hn = _rmsnorm_fwd("norm_mix_fwd0", h, norm_mix[0:1])
    for n, (l, part) in enumerate(units):
        i, sv = l // 2, saved[l]
        weights[l, part] = w = unit_weights(l, part, arrived)
        dep = None
        if n + 2 < len(units):
            gathers.append(_exchange_start(f"gather_start{n + 2}", unit_shards(*units[n + 2]), False, arrived[0]))
            dep = gathers[n + 2][4]
        if part == "mix":
            sv["h"], sv["hn"] = h, hn
            if l % 2 == 0:
                sv["proj"] = _proj_rows(f"attn_proj{i}", hn, w[0], 0, BF16, dep)
                sv["o"], *sv["kept"] = _attn_fwd(f"attn_fwd{i}", sv["proj"], attn_sinks[i:i + 1])
                h, hn = _out_proj(f"attn_out{i}", sv["o"], w[1], 0, h, norm_ffn[l:l + 1])
            else:
                sv["z"] = _proj_slots(f"hgrn_proj{i}", hn, w[0], 0, dep=dep).reshape(4, 2, S, HG_SLOT)
                sv["o"], *sv["kept"] = _hg_fwd(f"hgrn_fwd{i}", sv["z"], lb[i:i + 1], hgrn_norm[i:i + 1])
                h, hn = _out_proj(f"hgrn_out{i}", sv["o"], w[1], 0, h, norm_ffn[l:l + 1])
        else:
            sv["h2"], sv["hn2"] = h, hn
            sv["u"] = _proj_slots(f"ffn_up{l}", hn, w[0], 0, True, dep).reshape(2, 4, S, FF_SLOT)
            sv["a"], sv["convs"] = _convgate_fwd(f"ffn_gate{l}", sv["u"], w[2], conv_b[l])
            if l + 1 < DEPTH:
                h, hn = _down_proj(f"ffn_down{l}", sv["a"], w[1], 0, h, norm_mix[l + 1:l + 2])
            else:
                h = _down_proj(f"ffn_down{l}", sv["a"], w[1], 0, h)
        if n + 1 < len(units):
            arrived = _exchange_wait(f"gather_wait{n + 1}", gathers[n + 1], False, h)
    dh, d_norm_final, loss_rows = _loss_head("loss_head", h, norm_final[None], loss_target[0])

    d_conv_w, d_conv_b, d_norm_mix, d_norm_ffn = [None] * DEPTH, [None] * DEPTH, [None] * DEPTH, [None] * DEPTH
    d_sinks, d_lb, d_hgrn_norm = [None] * n_attn, [None] * n_hgrn, [None] * n_hgrn
    received, pending = {}, []
    for l, part in reversed(units):
        i, sv, w = l // 2, saved[l], weights[l, part]
        dep = pending[-1][1][4] if pending else None
        if part == "ffn":
            da = _dgrad_down(f"ffn_down_dgrad{l}", dh, w[1], 0, dep)
            g_down = _wgrad_down(f"ffn_down_wgrad{l}", sv["a"], dh).reshape(N_DEV, D_FF // N_DEV, D_MODEL)
            du, d_conv_w[l], d_conv_b[l] = _convgate_bwd(f"ffn_gate_bwd{l}", sv["u"], sv["convs"], w[2], da)
            du = du.reshape(N_DEV, S, FF_SLOT)
            grads = [_wgrad_slots(f"ffn_up_wgrad{l}", sv["hn2"], du, True), g_down]
            dh, d_norm_ffn[l] = _dgrad_slots(f"ffn_up_dgrad{l}", du, w[0], 0, (sv["h2"], norm_ffn[l:l + 1], dh), True)
        else:
            if l % 2 == 0:
                do = _dgrad_out(f"attn_out_dgrad{i}", dh, w[1], 0, BF16, dep)
                g_out = _wgrad_rows(f"attn_out_wgrad{i}", sv["o"], dh)
                dproj, d_sinks[i] = _attn_bwd(f"attn_bwd{i}", sv["proj"], *sv["kept"], do)
                g_in = _wgrad_rows(f"attn_proj_wgrad{i}", dproj, sv["hn"]).reshape(N_DEV, ATTN_IN // N_DEV, D_MODEL)
                dh_new = _dgrad_rows(f"attn_proj_dgrad{i}", dproj, w[0], 0, (sv["h"], norm_mix[l:l + 1], dh))
            else:
                dog = _dgrad_out(f"hgrn_out_dgrad{i}", dh, w[1], 0, F32, dep)
                g_out = _wgrad_rows(f"hgrn_out_wgrad{i}", sv["o"], dh)
                dz, d_lb[i], dng = _hg_bwd(f"hgrn_bwd{i}", sv["z"], lb[i:i + 1], hgrn_norm[i:i + 1], *sv["kept"], dog)
                d_hgrn_norm[i] = dng[0] + dng[1]
                dz = dz.reshape(N_DEV, S, HG_SLOT)
                g_in = _wgrad_slots(f"hgrn_proj_wgrad{i}", sv["hn"], dz)
                dh_new = _dgrad_slots(f"hgrn_proj_dgrad{i}", dz, w[0], 0, (sv["h"], norm_mix[l:l + 1], dh))
            grads = [g_in, g_out.reshape(N_DEV, D_MODEL // N_DEV, D_MODEL)]
            dh, d_norm_mix[l] = dh_new
        gate = dh
        if len(pending) == 2:
            key, oldest = pending.pop(0)
            received[key] = _exchange_wait(f"scatter_wait_{key[1]}{key[0]}", oldest, True, dh)
            gate = received[key][0]
        pending.append(((l, part), _exchange_start(f"scatter_start_{part}{l}", grads, True, gate)))
    grad_x = dh[None]

    small_shapes = [(DEPTH, D_MODEL), (DEPTH, D_MODEL), (1, D_MODEL), (1, D_MODEL), (n_hgrn, D_MODEL), (n_attn, 128),
                    (n_hgrn, HG_K), (DEPTH, 2 * D_FF), (DEPTH, N_DEV, 3, FF_SLOT)]
    partial = _pack_rows([
        jnp.concatenate(d_norm_mix), jnp.concatenate(d_norm_ffn), d_norm_final, loss_rows, jnp.concatenate(d_lb),
        jnp.concatenate(d_sinks), jnp.concatenate(d_hgrn_norm), jnp.stack(d_conv_b), jnp.stack(d_conv_w)])
    small_started = _exchange_start("small_start", [partial], False, pending[-1][1][4])
    attn_layers, hgrn_layers = range(0, DEPTH, 2), range(1, DEPTH, 2)

    def transposed(ts):
        return [t.transpose(0, 2, 1) for t in ts]

    big = {"hgrn_w_in": _adamw("adamw_hgrn_in", [received[l, "mix"][0] for l in hgrn_layers], hgrn_w_in, m_hgrn_w_in,
                               v_hgrn_w_in, dep=small_started[4])}
    big["hgrn_w_out"] = _adamw("adamw_hgrn_out", [received[l, "mix"][1] for l in hgrn_layers], hgrn_w_out, m_hgrn_w_out, v_hgrn_w_out)
    key, oldest = pending.pop(0)
    received[key] = _exchange_wait(f"scatter_wait_{key[1]}{key[0]}", oldest, True, big["hgrn_w_in"][3])
    up_t = _adamw("adamw_ffn_up", [received[l, "ffn"][0] for l in range(DEPTH)], *transposed((ffn_w_up, m_ffn_w_up, v_ffn_w_up)))
    big["ffn_w_up"] = transposed(up_t)
    big["ffn_w_down"] = _adamw("adamw_ffn_down", [received[l, "ffn"][1] for l in range(DEPTH)], ffn_w_down, m_ffn_w_down, v_ffn_w_down)
    key, oldest = pending.pop(0)
    received[key] = _exchange_wait(f"scatter_wait_{key[1]}{key[0]}", oldest, True, up_t[3])
    total = _sum_devices("sum_small", _exchange_wait("small_wait", small_started, False, up_t[3])[0])
    (g_norm_mix, g_norm_ffn, g_norm_final, loss_sum, g_lb, g_sinks, g_hgrn_norm, g_conv_b, g_conv_w_all) = _unpack_rows(
        total, small_shapes)

    loss = jnp.sum(loss_sum)
    g_norm_final = g_norm_final[0]
    g_sinks = g_sinks[:, :N_Q_HEADS]
    g_lb_logits = _lb_bwd("lb_bwd", hgrn_lb_logits, g_lb)
    g_conv_w = lax.dynamic_index_in_dim(g_conv_w_all, me, axis=1, keepdims=False)

    big.update({
        "attn_w_in": transposed(_adamw("adamw_attn_in", [received[l, "mix"][0] for l in attn_layers],
                                       *transposed((attn_w_in, m_attn_w_in, v_attn_w_in)))),
        "attn_w_out": _adamw("adamw_attn_out", [received[l, "mix"][1] for l in attn_layers], attn_w_out, m_attn_w_out, v_attn_w_out),
        "ffn_conv_w": _adamw("adamw_conv_w", [g_conv_w[l][None] for l in range(DEPTH)], ffn_conv_w, m_ffn_conv_w, v_ffn_conv_w),
    })
    small_w = [norm_mix, norm_ffn, norm_final, attn_sinks, hgrn_norm, hgrn_lb_logits, ffn_conv_b]
    small_m = [m_norm_mix, m_norm_ffn, m_norm_final, m_attn_sinks, m_hgrn_norm, m_hgrn_lb_logits, m_ffn_conv_b]
    small_v = [v_norm_mix, v_norm_ffn, v_norm_final, v_attn_sinks, v_hgrn_norm, v_hgrn_lb_logits, v_ffn_conv_b]
    small_g = [g_norm_mix, g_norm_ffn, g_norm_final, g_sinks, g_hgrn_norm, g_lb_logits, g_conv_b]
    outs = _adamw("adamw_small", [_pack_rows(small_g)[None]], *[_pack_rows(t)[None] for t in (small_w, small_m, small_v)])
    outs = [o[0] for o in outs]
    shapes = [w.shape for w in small_w]
    small = {n: [t[j] for t in [_unpack_rows(o, shapes) for o in outs]]
             for j, n in enumerate(["norm_mix", "norm_ffn", "norm_final", "attn_sinks", "hgrn_norm", "hgrn_lb_logits", "ffn_conv_b"])}
    order = ["norm_mix", "norm_ffn", "norm_final", "attn_w_in", "attn_w_out", "attn_sinks", "hgrn_w_in", "hgrn_w_out",
             "hgrn_norm", "hgrn_lb_logits", "ffn_w_up", "ffn_conv_w", "ffn_conv_b", "ffn_w_down"]
    res = {**big, **small}
    return (loss, grad_x, *[res[n][0] for n in order], *[res[n][1] for n in order], *[res[n][2] for n in order],
            *[res[n][3] for n in order])
```

```python
import numpy as np
import jax
import jax.numpy as jnp
from jax import lax
from jax.experimental import pallas as pl
from jax.experimental.pallas import tpu as pltpu

F32 = jnp.float32
BF16 = jnp.bfloat16

D_MODEL = 1024
DEPTH = 4
HEAD_DIM = 64
N_Q_HEADS = 16
N_KV_HEADS = 4
Q_PER_KV = 4
ATTN_BLOCK = 128
ATTN_IN = 1536
HG_HEADS = 8
HG_K = 128
HG_CHUNK = 64
HG_IN = 4096
D_FF = 2816
EPS = 1e-6
N_DEV = 8
FF_SLOT = 2 * D_FF // N_DEV
HG_SLOT = HG_IN // N_DEV
HG_LEVELS = 6

ADAM_LR = 0.001
ADAM_B1 = 0.9
ADAM_B2 = 0.999
ADAM_EPS = 1e-08
ADAM_WD = 0.01
ADAM_STEP = 10

VMEM_LIMIT = 56 * 1024 * 1024
ROW_TILE = 1024
WIDE_ROW_TILE = 2048
NEG_BIG = -1e30

NN = (((1,), (0,)), ((), ()))
NT = (((1,), (1,)), ((), ()))
TN = (((0,), (0,)), ((), ()))


def _bdot(a, b, dn):
    return lax.dot_general(a.astype(BF16), b.astype(BF16), dn, preferred_element_type=F32)


def _sds(shape, dtype):
    return jax.ShapeDtypeStruct(tuple(shape), dtype)


def _params(sem):
    return pltpu.CompilerParams(dimension_semantics=sem, vmem_limit_bytes=VMEM_LIMIT)


DEP_SHAPE = (8, 128)


def _dep_spec(rank):
    return pl.BlockSpec(DEP_SHAPE, lambda *_: (0, 0))


def _matmul(name, a, b, *, dn, grid, a_spec, b_spec, o_spec, out_shape, acc_shape=None, extra=(), extra_specs=(),
            finish=None, dep=None, sem=("parallel", "parallel", "arbitrary")):
    nk = grid[2]
    many = isinstance(out_shape, (list, tuple))
    n_in = 2 + len(extra) + (dep is not None)
    n_out = len(out_shape) if many else 1

    def body(*refs):
        a_ref, b_ref = refs[0], refs[1]
        outs = refs[n_in:n_in + n_out]

        def prod():
            return _bdot(a_ref[...], b_ref[...], dn)

        def done(v):
            if finish is None:
                outs[0][...] = v.astype(outs[0].dtype)
            else:
                finish(v, refs[2:2 + len(extra)], outs)

        if nk == 1:
            done(prod())
        else:
            acc = refs[-1]
            k = pl.program_id(2)

            @pl.when(k == 0)
            def _():
                acc[...] = prod()

            @pl.when(k > 0)
            def _():
                acc[...] += prod()

            @pl.when(k == nk - 1)
            def _():
                done(acc[...])

    in_specs = [a_spec, b_spec, *extra_specs] + ([_dep_spec(3)] if dep is not None else [])
    args = (a, b, *extra) + ((dep,) if dep is not None else ())
    scratch = [] if nk == 1 else [pltpu.VMEM(acc_shape, F32)]
    return pl.pallas_call(
        body, name=name, grid=grid, in_specs=in_specs, out_specs=o_spec, out_shape=out_shape,
        scratch_shapes=scratch, compiler_params=_params(sem),
    )(*args)


def _rms(x):
    return lax.rsqrt(jnp.mean(x * x, axis=-1, keepdims=True) + EPS)


def _residual_finish(v, ex, outs):
    h = v + ex[0][...]
    outs[0][...] = h
    if len(ex) > 1:
        outs[1][...] = (h * _rms(h) * ex[1][...]).astype(outs[1].dtype)


def _norm_bwd_finish(v, ex, outs):
    x = ex[0][...]
    r = _rms(x)
    xh = x * r
    dyg = v * ex[1][...]
    outs[0][...] = ex[2][...] + r * (dyg - xh * jnp.mean(dyg * xh, axis=-1, keepdims=True))
    part = jnp.sum(v * xh, axis=0, keepdims=True)

    @pl.when(pl.program_id(0) == 0)
    def _():
        outs[1][...] = part

    @pl.when(pl.program_id(0) > 0)
    def _():
        outs[1][...] += part


def _row_io(tm, norm_g):
    row = pl.BlockSpec((tm, D_MODEL), lambda i, j, k: (i, 0))
    vec = pl.BlockSpec((1, D_MODEL), lambda i, j, k: (0, 0))
    if norm_g is None:
        return (row,), row, lambda S: _sds((S, D_MODEL), F32)
    return (row, vec), [row, row], lambda S: [_sds((S, D_MODEL), F32), _sds((S, D_MODEL), BF16)]


def _tile(n, t):
    return min(n, t)


def _proj_rows(name, hn, wt, l, out_dtype, dep=None):
    S, N = hn.shape[0], wt.shape[1]
    tm, tn = _tile(S, ROW_TILE), 512
    return _matmul(
        name, hn, wt, dn=NT, grid=(S // tm, N // tn, 1),
        a_spec=pl.BlockSpec((tm, D_MODEL), lambda i, j, k: (i, 0)),
        b_spec=pl.BlockSpec((None, tn, D_MODEL), lambda i, j, k: (l, j, 0)),
        o_spec=pl.BlockSpec((tm, tn), lambda i, j, k: (i, j)),
        out_shape=_sds((S, N), out_dtype), dep=dep)


def _slot_weight(w, transposed):
    if transposed:
        return w.shape[2], (None, None, w.shape[2], D_MODEL), NT, NN
    return w.shape[3], (None, None, D_MODEL, w.shape[3]), NN, NT


def _proj_slots(name, hn, w, l, transposed=False, dep=None):
    S = hn.shape[0]
    r, blk, dn, _ = _slot_weight(w, transposed)
    tm = _tile(S, WIDE_ROW_TILE)
    return _matmul(
        name, hn, w, dn=dn, grid=(N_DEV, S // tm, 1),
        a_spec=pl.BlockSpec((tm, D_MODEL), lambda j, i, k: (i, 0)),
        b_spec=pl.BlockSpec(blk, lambda j, i, k: (l, j, 0, 0)),
        o_spec=pl.BlockSpec((None, tm, r), lambda j, i, k: (j, i, 0)),
        out_shape=_sds((N_DEV, S, r), F32), dep=dep)


def _out_proj(name, o, w, l, h, norm_g=None):
    S, K = o.shape
    tm = _tile(S, ROW_TILE)
    extra_specs, o_spec, out_shape = _row_io(tm, norm_g)
    return _matmul(
        name, o, w, dn=NN, grid=(S // tm, 1, 1),
        a_spec=pl.BlockSpec((tm, K), lambda i, j, k: (i, 0)),
        b_spec=pl.BlockSpec((None, K, D_MODEL), lambda i, j, k: (l, 0, 0)),
        o_spec=o_spec, out_shape=out_shape(S), extra=(h,) if norm_g is None else (h, norm_g),
        extra_specs=extra_specs, finish=_residual_finish)


def _down_proj(name, a, w, l, h, norm_g=None):
    nj, S, r = a.shape
    tm = _tile(S, ROW_TILE)
    extra_specs, o_spec, out_shape = _row_io(tm, norm_g)
    return _matmul(
        name, a, w, dn=NN, grid=(S // tm, 1, nj),
        a_spec=pl.BlockSpec((None, tm, r), lambda i, j, k: (k, i, 0)),
        b_spec=pl.BlockSpec((None, None, r, D_MODEL), lambda i, j, k: (l, k, 0, 0)),
        o_spec=o_spec, out_shape=out_shape(S), acc_shape=(tm, D_MODEL),
        extra=(h,) if norm_g is None else (h, norm_g), extra_specs=extra_specs, finish=_residual_finish)


def _dgrad_down(name, dh, w, l, dep=None):
    S = dh.shape[0]
    nj, r = w.shape[1], w.shape[2]
    tm = _tile(S, ROW_TILE)
    return _matmul(
        name, dh, w, dn=NT, grid=(nj, S // tm, 1),
        a_spec=pl.BlockSpec((tm, D_MODEL), lambda j, i, k: (i, 0)),
        b_spec=pl.BlockSpec((None, None, r, D_MODEL), lambda j, i, k: (l, j, 0, 0)),
        o_spec=pl.BlockSpec((None, tm, r), lambda j, i, k: (j, i, 0)),
        out_shape=_sds((nj, S, r), BF16), dep=dep)


def _wgrad_down(name, a, dh):
    nj, S, r = a.shape
    tk = _tile(S, ROW_TILE)
    return _matmul(
        name, a, dh, dn=TN, grid=(nj, 1, S // tk),
        a_spec=pl.BlockSpec((None, tk, r), lambda s, j, k: (s, k, 0)),
        b_spec=pl.BlockSpec((tk, D_MODEL), lambda s, j, k: (k, 0)),
        o_spec=pl.BlockSpec((None, r, D_MODEL), lambda s, j, k: (s, 0, 0)),
        out_shape=_sds((nj, r, D_MODEL), BF16), acc_shape=(r, D_MODEL))


def _norm_bwd_io(tm, S):
    row = pl.BlockSpec((tm, D_MODEL), lambda i, j, k: (i, 0))
    vec = pl.BlockSpec((1, D_MODEL), lambda i, j, k: (0, 0))
    return dict(extra_specs=(row, vec, row), o_spec=[row, vec], out_shape=[_sds((S, D_MODEL), F32), _sds((1, D_MODEL), F32)],
                finish=_norm_bwd_finish, sem=("arbitrary", "arbitrary", "arbitrary"))


def _dgrad_slots(name, dz, w, l, norm, transposed=False):
    nj, S, r = dz.shape
    _, blk, _, dn = _slot_weight(w, transposed)
    tm = _tile(S, ROW_TILE)
    return _matmul(
        name, dz, w, dn=dn, grid=(S // tm, 1, nj),
        a_spec=pl.BlockSpec((None, tm, r), lambda i, j, k: (k, i, 0)),
        b_spec=pl.BlockSpec(blk, lambda i, j, k: (l, k, 0, 0)),
        acc_shape=(tm, D_MODEL), extra=norm, **_norm_bwd_io(tm, S))


def _wgrad_slots(name, hn, dz, transposed=False):
    nj, S, r = dz.shape
    tk = _tile(S, ROW_TILE)
    hn_spec = pl.BlockSpec((tk, D_MODEL), lambda s, j, k: (k, 0))
    dz_spec = pl.BlockSpec((None, tk, r), lambda s, j, k: (s, k, 0))
    if transposed:
        return _matmul(
            name, dz, hn, dn=TN, grid=(nj, 1, S // tk), a_spec=dz_spec, b_spec=hn_spec,
            o_spec=pl.BlockSpec((None, r, D_MODEL), lambda s, j, k: (s, 0, 0)),
            out_shape=_sds((nj, r, D_MODEL), BF16), acc_shape=(r, D_MODEL))
    return _matmul(
        name, hn, dz, dn=TN, grid=(nj, 1, S // tk), a_spec=hn_spec, b_spec=dz_spec,
        o_spec=pl.BlockSpec((None, D_MODEL, r), lambda s, j, k: (s, 0, 0)),
        out_shape=_sds((nj, D_MODEL, r), BF16), acc_shape=(D_MODEL, r))


def _dgrad_out(name, dh, w, l, out_dtype, dep=None):
    S, K = dh.shape[0], w.shape[1]
    tm = _tile(S, ROW_TILE)
    return _matmul(
        name, dh, w, dn=NT, grid=(S // tm, 1, 1),
        a_spec=pl.BlockSpec((tm, D_MODEL), lambda i, j, k: (i, 0)),
        b_spec=pl.BlockSpec((None, K, D_MODEL), lambda i, j, k: (l, 0, 0)),
        o_spec=pl.BlockSpec((tm, K), lambda i, j, k: (i, 0)),
        out_shape=_sds((S, K), out_dtype), dep=dep)


def _wgrad_rows(name, a, b):
    S, K = a.shape
    tk = _tile(S, ROW_TILE)
    return _matmul(
        name, a, b, dn=TN, grid=(1, 1, S // tk),
        a_spec=pl.BlockSpec((tk, K), lambda i, j, k: (k, 0)),
        b_spec=pl.BlockSpec((tk, D_MODEL), lambda i, j, k: (k, 0)),
        o_spec=pl.BlockSpec((K, D_MODEL), lambda i, j, k: (0, 0)),
        out_shape=_sds((K, D_MODEL), BF16), acc_shape=(K, D_MODEL))


def _dgrad_rows(name, dz, wt, l, norm):
    S, N = dz.shape
    tm = _tile(S, ROW_TILE)
    return _matmul(
        name, dz, wt, dn=NN, grid=(S // tm, 1, 1),
        a_spec=pl.BlockSpec((tm, N), lambda i, j, k: (i, 0)),
        b_spec=pl.BlockSpec((None, N, D_MODEL), lambda i, j, k: (l, 0, 0)),
        extra=norm, **_norm_bwd_io(tm, S))


def _rmsnorm_fwd(name, h, g):
    S = h.shape[0]
    tm = _tile(S, ROW_TILE)

    def body(h_ref, g_ref, o_ref):
        x = h_ref[...]
        o_ref[...] = (x * _rms(x) * g_ref[...]).astype(o_ref.dtype)

    row = pl.BlockSpec((tm, D_MODEL), lambda i: (i, 0))
    return pl.pallas_call(
        body, name=name, grid=(S // tm,), in_specs=[row, pl.BlockSpec((1, D_MODEL), lambda i: (0, 0))],
        out_specs=row, out_shape=_sds((S, D_MODEL), BF16), compiler_params=_params(("parallel",)),
    )(h, g)


def _loss_head(name, h, g, target):
    S = h.shape[0]
    tm = _tile(S, ROW_TILE)

    def body(h_ref, g_ref, t_ref, dh_ref, dg_ref, ls_ref):
        x = h_ref[...]
        r = lax.rsqrt(jnp.mean(x * x, axis=-1, keepdims=True) + EPS)
        xh = x * r
        diff = xh * g_ref[...] - t_ref[...]
        dyf = diff * (1.0 / D_MODEL)
        dyg = dyf * g_ref[...]
        dh_ref[...] = r * (dyg - xh * jnp.mean(dyg * xh, axis=-1, keepdims=True))
        part = jnp.sum(dyf * xh, axis=0, keepdims=True)
        lpart = jnp.sum(diff * diff, axis=0, keepdims=True) * (0.5 / D_MODEL)

        @pl.when(pl.program_id(0) == 0)
        def _():
            dg_ref[...] = part
            ls_ref[...] = lpart

        @pl.when(pl.program_id(0) > 0)
        def _():
            dg_ref[...] += part
            ls_ref[...] += lpart

    row = pl.BlockSpec((tm, D_MODEL), lambda i: (i, 0))
    vec = pl.BlockSpec((1, D_MODEL), lambda i: (0, 0))
    return pl.pallas_call(
        body, name=name, grid=(S // tm,), in_specs=[row, vec, row], out_specs=[row, vec, vec],
        out_shape=[_sds((S, D_MODEL), F32), _sds((1, D_MODEL), F32), _sds((1, D_MODEL), F32)],
        compiler_params=_params(("arbitrary",)),
    )(h, g, target)


ATTN_SCALE = HEAD_DIM ** -0.5
ALIBI_SLOPES = [2.0 ** (-8.0 * (h + 1) / N_Q_HEADS) for h in range(N_Q_HEADS)]
K_COL = N_Q_HEADS * HEAD_DIM
KV_COLS = N_KV_HEADS * HEAD_DIM
V_COL = K_COL + KV_COLS


def _attn_masks(n):
    qi = lax.broadcasted_iota(jnp.int32, (ATTN_BLOCK, ATTN_BLOCK), 0)
    ki = lax.broadcasted_iota(jnp.int32, (ATTN_BLOCK, ATTN_BLOCK), 1)
    dist_c = (qi - ki).astype(F32)
    return dist_c + float(ATTN_BLOCK), dist_c, (ki > qi) & (n > 0), qi >= ki


def _attn_probs(raw_p, raw_c, sink, slope, masks):
    dist_p, dist_c, valid_p, valid_c = masks
    sp = jnp.where(valid_p, raw_p * ATTN_SCALE - slope * dist_p, NEG_BIG)
    sc = jnp.where(valid_c, raw_c * ATTN_SCALE - slope * dist_c, NEG_BIG)
    m = jnp.maximum(jnp.maximum(jnp.max(sp, axis=-1, keepdims=True), jnp.max(sc, axis=-1, keepdims=True)), sink)
    ep, ec, es = jnp.exp(sp - m), jnp.exp(sc - m), jnp.exp(sink - m)
    inv = 1.0 / (jnp.sum(ep, axis=-1, keepdims=True) + jnp.sum(ec, axis=-1, keepdims=True) + es)
    return ep * inv, ec * inv, es * inv


def _group_rows(ref, m):
    return jnp.concatenate([ref[:, HEAD_DIM * (Q_PER_KV * m + g):HEAD_DIM * (Q_PER_KV * m + g + 1)]
                            for g in range(Q_PER_KV)], axis=0)


def _head_rows(x, g):
    return x[ATTN_BLOCK * g:ATTN_BLOCK * (g + 1)]


def _attn_specs(nblk):
    last = nblk - 1
    kcol, vcol = K_COL // KV_COLS, V_COL // KV_COLS
    return [
        pl.BlockSpec((ATTN_BLOCK, K_COL), lambda n: (jnp.minimum(n, last), 0)),
        pl.BlockSpec((ATTN_BLOCK, KV_COLS), lambda n: (jnp.minimum(n, last), kcol)),
        pl.BlockSpec((ATTN_BLOCK, KV_COLS), lambda n: (jnp.maximum(jnp.minimum(n, last) - 1, 0), kcol)),
        pl.BlockSpec((ATTN_BLOCK, KV_COLS), lambda n: (jnp.minimum(n, last), vcol)),
        pl.BlockSpec((ATTN_BLOCK, KV_COLS), lambda n: (jnp.maximum(jnp.minimum(n, last) - 1, 0), vcol)),
    ]


P_COLS = 2 * ATTN_BLOCK


def _attn_fwd(name, proj, sinks):
    S = proj.shape[0]
    nblk = S // ATTN_BLOCK

    def body(q_ref, kc_ref, kp_ref, vc_ref, vp_ref, sk_ref, o_ref, p_ref, ps_ref):
        masks = _attn_masks(pl.program_id(0))
        lane = lax.broadcasted_iota(jnp.int32, (ATTN_BLOCK, 128), 1)
        sink_p = jnp.zeros((ATTN_BLOCK, 128), F32)
        for m in range(N_KV_HEADS):
            ks = slice(HEAD_DIM * m, HEAD_DIM * (m + 1))
            kp, kc, vp, vc = kp_ref[:, ks], kc_ref[:, ks], vp_ref[:, ks], vc_ref[:, ks]
            q4 = _group_rows(q_ref, m)
            raw_p, raw_c = _bdot(q4, kp, NT), _bdot(q4, kc, NT)
            pps, pcs = [], []
            for g in range(Q_PER_KV):
                hh = Q_PER_KV * m + g
                pp, pc, ps = _attn_probs(_head_rows(raw_p, g), _head_rows(raw_c, g), sk_ref[0, hh], ALIBI_SLOPES[hh], masks)
                pps.append(pp.astype(BF16))
                pcs.append(pc.astype(BF16))
                p_ref[:, P_COLS * hh:P_COLS * hh + ATTN_BLOCK] = pps[g]
                p_ref[:, P_COLS * hh + ATTN_BLOCK:P_COLS * (hh + 1)] = pcs[g]
                sink_p = jnp.where(lane == hh, ps, sink_p)
            o4 = _bdot(jnp.concatenate(pps, axis=0), vp, NN) + _bdot(jnp.concatenate(pcs, axis=0), vc, NN)
            for g in range(Q_PER_KV):
                hh = Q_PER_KV * m + g
                o_ref[:, HEAD_DIM * hh:HEAD_DIM * (hh + 1)] = _head_rows(o4, g).astype(o_ref.dtype)
        ps_ref[...] = sink_p

    row = lambda cols: pl.BlockSpec((ATTN_BLOCK, cols), lambda n: (n, 0))
    return pl.pallas_call(
        body, name=name, grid=(nblk,),
        in_specs=_attn_specs(nblk) + [pl.BlockSpec(memory_space=pltpu.SMEM)],
        out_specs=[row(K_COL), row(N_Q_HEADS * P_COLS), row(128)],
        out_shape=[_sds((S, K_COL), BF16), _sds((S, N_Q_HEADS * P_COLS), BF16), _sds((S, 128), F32)],
        compiler_params=_params(("parallel",)),
    )(proj, proj, proj, proj, proj, sinks)


def _attn_bwd(name, proj, probs, sink_probs, do):
    S = proj.shape[0]
    nblk = S // ATTN_BLOCK

    def body(q_ref, kc_ref, kp_ref, vc_ref, vp_ref, do_ref, p_ref, ps_ref, dz_ref, ds_ref, carry, cur, padd):
        n = pl.program_id(0)

        @pl.when(n == 0)
        def _():
            carry[...] = jnp.zeros_like(carry)
            ds_ref[...] = jnp.zeros_like(ds_ref)

        @pl.when(n < nblk)
        def _():
            lane = lax.broadcasted_iota(jnp.int32, (ATTN_BLOCK, 128), 1)
            sink_p = ps_ref[...]
            dsv = jnp.zeros((1, 128), F32)
            for m in range(N_KV_HEADS):
                ks = slice(HEAD_DIM * m, HEAD_DIM * (m + 1))
                kp, kc, vp, vc = kp_ref[:, ks], kc_ref[:, ks], vp_ref[:, ks], vc_ref[:, ks]
                q4, do4 = _group_rows(q_ref, m), _group_rows(do_ref, m)
                dpp4, dpc4 = _bdot(do4, vp, NT), _bdot(do4, vc, NT)
                pps, pcs, dsps, dscs = [], [], [], []
                for g in range(Q_PER_KV):
                    hh = Q_PER_KV * m + g
                    pps.append(p_ref[:, P_COLS * hh:P_COLS * hh + ATTN_BLOCK])
                    pcs.append(p_ref[:, P_COLS * hh + ATTN_BLOCK:P_COLS * (hh + 1)])
                    pp, pc = pps[g].astype(F32), pcs[g].astype(F32)
                    dpp, dpc = _head_rows(dpp4, g), _head_rows(dpc4, g)
                    delta = jnp.sum(pp * dpp, axis=-1, keepdims=True) + jnp.sum(pc * dpc, axis=-1, keepdims=True)
                    dsv = dsv - jnp.sum(jnp.where(lane == hh, sink_p, 0.0) * delta, axis=0, keepdims=True)
                    dsps.append((pp * (dpp - delta)).astype(BF16))
                    dscs.append((pc * (dpc - delta)).astype(BF16))
                pp4, pc4 = jnp.concatenate(pps, axis=0), jnp.concatenate(pcs, axis=0)
                dsp4, dsc4 = jnp.concatenate(dsps, axis=0), jnp.concatenate(dscs, axis=0)
                dq4 = (_bdot(dsp4, kp, NN) + _bdot(dsc4, kc, NN)) * ATTN_SCALE
                for g in range(Q_PER_KV):
                    hh = Q_PER_KV * m + g
                    cur[:, HEAD_DIM * hh:HEAD_DIM * (hh + 1)] = _head_rows(dq4, g)
                cur[:, K_COL + HEAD_DIM * m:K_COL + HEAD_DIM * (m + 1)] = _bdot(q4, dsc4, TN).T * ATTN_SCALE
                cur[:, V_COL + HEAD_DIM * m:V_COL + HEAD_DIM * (m + 1)] = _bdot(do4, pc4, TN).T
                padd[:, ks] = _bdot(q4, dsp4, TN).T * ATTN_SCALE
                padd[:, KV_COLS + HEAD_DIM * m:KV_COLS + HEAD_DIM * (m + 1)] = _bdot(do4, pp4, TN).T
            ds_ref[...] += dsv
            dz_ref[:, :K_COL] = carry[:, :K_COL].astype(dz_ref.dtype)
            dz_ref[:, K_COL:] = (carry[:, K_COL:] + padd[...]).astype(dz_ref.dtype)
            carry[...] = cur[...]

        @pl.when(n == nblk)
        def _():
            dz_ref[...] = carry[...].astype(dz_ref.dtype)

    return pl.pallas_call(
        body, name=name, grid=(nblk + 1,),
        in_specs=_attn_specs(nblk) + [
            pl.BlockSpec((ATTN_BLOCK, cols), lambda n: (jnp.minimum(n, nblk - 1), 0))
            for cols in (K_COL, N_Q_HEADS * P_COLS, 128)],
        out_specs=[pl.BlockSpec((ATTN_BLOCK, ATTN_IN), lambda n: (jnp.maximum(n - 1, 0), 0)),
                   pl.BlockSpec((1, 128), lambda n: (0, 0))],
        out_shape=[_sds((S, ATTN_IN), BF16), _sds((1, 128), F32)],
        scratch_shapes=[pltpu.VMEM((ATTN_BLOCK, ATTN_IN), F32), pltpu.VMEM((ATTN_BLOCK, ATTN_IN), F32),
                        pltpu.VMEM((ATTN_BLOCK, 2 * KV_COLS), F32)],
        compiler_params=_params(("arbitrary",)),
    )(proj, proj, proj, proj, proj, do, probs, sink_probs)


def _hg_consts():
    C = HG_CHUNK
    tri = np.tril(np.ones((C, C)))
    t = np.arange(C)
    rows, masks = [tri], []
    for lvl in range(HG_LEVELS):
        n = C >> (lvl + 1)
        sel = np.zeros((C, C))
        sel[t, (t // (2 * n)) * (2 * n) + n - 1] = 1.0
        rows.append(sel @ tri)
        tt, ss = t[:, None], t[None, :]
        masks.append((tt // (2 * n) == ss // (2 * n)) & ((tt // n) % 2 == 1) & ((ss // n) % 2 == 0))
    masks.append(np.eye(C, dtype=bool))
    stk = np.concatenate(rows, axis=0)
    return jnp.asarray(stk, BF16), jnp.asarray(np.stack(masks), F32)


def _sigmoid(x):
    return 1.0 / (1.0 + jnp.exp(-x))


def _split(x, parts):
    out, rest = [], x
    for _ in range(parts):
        out.append(rest.astype(BF16))
        rest = rest - out[-1].astype(F32)
    return out


def _dot01(m01, x, dn, parts=3):
    return sum(lax.dot_general(m01, p, dn, preferred_element_type=F32) for p in _split(x, parts))


def _ref_rows(b, n):
    C = b.shape[1]
    if 2 * n >= 8:
        b3 = b.reshape(HG_CHUNK // (2 * n), 2 * n, C)
        return jnp.broadcast_to(b3[:, n - 1:n, :], b3.shape).reshape(HG_CHUNK, C)
    pos = lax.broadcasted_iota(jnp.int32, b.shape, 0) % (2 * n)
    out = b
    for p in range(2 * n):
        if p != n - 1:
            out = jnp.where(pos == p, pltpu.roll(b, (p - (n - 1)) % HG_CHUNK, 0), out)
    return out


HG_STEP_CHUNKS = 4


def _chunk_rows(ci):
    return pl.ds(pl.multiple_of(ci * HG_CHUNK, HG_CHUNK), HG_CHUNK)


def _hg_common(z_ref, rows, lb_ref, stk_ref):
    qr, fr = z_ref[0, rows, :], z_ref[1, rows, :]
    lb = lb_ref[...]
    sq, sg, sgn = _sigmoid(qr), _sigmoid(fr), _sigmoid(-fr)
    ft = lb + (1.0 - lb) * sg
    b = _dot01(stk_ref[0:HG_CHUNK, :], jnp.log(ft), NN)
    ws = [jnp.exp(-jnp.abs(b - _ref_rows(b, HG_CHUNK >> (l + 1)))) for l in range(HG_LEVELS)]
    blast = b[HG_CHUNK - 1:HG_CHUNK]
    return dict(qr=qr, fr=fr, lb=lb, sq=sq, sg=sg, sgn=sgn, ft=ft, q=qr * sq, kk=(1.0 - lb) * sgn, b=b,
                ws=ws, eb=jnp.exp(b), ed=jnp.exp(blast - b), elast=jnp.exp(blast))


def _hg_factors(qh, kh, ws, sl):
    return ([(qh * ws[l][:, sl]).astype(BF16) for l in range(HG_LEVELS)],
            [(kh * ws[l][:, sl]).astype(BF16) for l in range(HG_LEVELS)])


def _hg_intra(qh, kh, ws, msk_ref, sl):
    qls, kls = _hg_factors(qh, kh, ws, sl)
    a = msk_ref[HG_LEVELS] * _bdot(qh, kh, NT)
    for l in range(HG_LEVELS):
        a = a + msk_ref[l] * _bdot(qls[l], kls[l], NT)
    return a


def _hg_fwd(name, z, lb, ng):
    S = z.shape[2]
    nc = S // HG_CHUNK
    per = min(HG_STEP_CHUNKS, nc)
    stk, msk = _hg_consts()

    def body(z_ref, lb_ref, ng_ref, stk_ref, msk_ref, og_ref, st_ref, a_ref, o_ref, state):
        @pl.when(pl.program_id(1) == 0)
        def _():
            state[...] = jnp.zeros_like(state)

        def chunk(ci, _):
            rows = _chunk_rows(ci)
            cm = _hg_common(z_ref, rows, lb_ref, stk_ref)
            v, gt = z_ref[2, rows, :], z_ref[3, rows, :]
            kd = cm["kk"] * cm["ed"]
            for hh in range(4):
                sl = slice(HG_K * hh, HG_K * (hh + 1))
                st = state[hh]
                st_ref[ci, hh] = st
                qh, kh, vh = cm["q"][:, sl], cm["kk"][:, sl], v[:, sl]
                a = _hg_intra(qh, kh, cm["ws"], msk_ref, sl).astype(BF16)
                a_ref[ci, hh] = a
                o = _bdot(a, vh, NN) + _bdot(qh * cm["eb"][:, sl], st, NT)
                o_ref[rows, sl] = o
                state[hh] = cm["elast"][:, sl] * st + _bdot(vh, kd[:, sl], TN)
                gh = gt[:, sl]
                og_ref[rows, sl] = (o * _rms(o) * ng_ref[...] * (gh * _sigmoid(gh))).astype(og_ref.dtype)
            return 0

        lax.fori_loop(0, per, chunk, 0, unroll=True)

    return pl.pallas_call(
        body, name=name, grid=(2, nc // per),
        in_specs=[pl.BlockSpec((4, None, per * HG_CHUNK, HG_SLOT), lambda g, c: (0, g, c, 0)),
                  pl.BlockSpec((1, HG_SLOT), lambda g, c: (0, g)),
                  pl.BlockSpec((1, HG_K), lambda g, c: (0, 0)),
                  pl.BlockSpec(stk.shape, lambda g, c: (0, 0)),
                  pl.BlockSpec(msk.shape, lambda g, c: (0, 0, 0))],
        out_specs=[pl.BlockSpec((per * HG_CHUNK, HG_SLOT), lambda g, c: (c, g)),
                   pl.BlockSpec((per, 4, HG_K, HG_K), lambda g, c: (c, g, 0, 0)),
                   pl.BlockSpec((per, 4, HG_CHUNK, HG_CHUNK), lambda g, c: (c, g, 0, 0)),
                   pl.BlockSpec((per * HG_CHUNK, HG_SLOT), lambda g, c: (c, g))],
        out_shape=[_sds((S, D_MODEL), BF16), _sds((nc, HG_HEADS, HG_K, HG_K), F32),
                   _sds((nc, HG_HEADS, HG_CHUNK, HG_CHUNK), BF16), _sds((S, D_MODEL), F32)],
        scratch_shapes=[pltpu.VMEM((4, HG_K, HG_K), F32)],
        compiler_params=_params(("parallel", "arbitrary")),
    )(z, lb, ng, stk, msk)


def _hg_bwd(name, z, lb, ng, states, intra, o_pre, dog):
    S = z.shape[2]
    nc = S // HG_CHUNK
    per = min(HG_STEP_CHUNKS, nc)
    stk, msk = _hg_consts()

    def body(z_ref, lb_ref, ng_ref, stk_ref, msk_ref, st_ref, a_ref, o_ref, dog_ref, dz_ref, dlb_ref, dng_ref, dstate):
        @pl.when(pl.program_id(1) == 0)
        def _():
            dstate[...] = jnp.zeros_like(dstate)
            dlb_ref[...] = jnp.zeros_like(dlb_ref)
            dng_ref[...] = jnp.zeros_like(dng_ref)

        def chunk(k, _):
            ci = per - 1 - k
            rows = _chunk_rows(ci)
            cm = _hg_common(z_ref, rows, lb_ref, stk_ref)
            v, gt = z_ref[2, rows, :], z_ref[3, rows, :]
            ng = ng_ref[...]
            kd = cm["kk"] * cm["ed"]
            row = lax.broadcasted_iota(jnp.int32, (HG_CHUNK, 1), 0)
            dng = jnp.zeros((1, HG_K), F32)
            dq_h, dkk_h, db_h, dv_h, dgt_h = [], [], [], [], []
            dr_h = [[] for _ in range(HG_LEVELS)]
            for hh in range(4):
                sl = slice(HG_K * hh, HG_K * (hh + 1))
                st, dst = st_ref[ci, hh], dstate[hh]
                qh, kh, vh, ebh, edh, kdh = cm["q"][:, sl], cm["kk"][:, sl], v[:, sl], cm["eb"][:, sl], cm["ed"][:, sl], kd[:, sl]
                elh = cm["elast"][:, sl]
                qls, kls = _hg_factors(qh, kh, cm["ws"], sl)
                a, o = a_ref[ci, hh], o_ref[rows, sl]
                qe = qh * ebh
                r = _rms(o)
                xh = o * r
                gh = gt[:, sl]
                sgg = _sigmoid(gh)
                dog = dog_ref[rows, sl].astype(F32)
                dy = dog * (gh * sgg)
                dgt_h.append(dog * (xh * ng) * (sgg * (1.0 + gh * (1.0 - sgg))))
                dng = dng + jnp.sum(dy * xh, axis=0, keepdims=True)
                dyg = dy * ng
                do = r * (dyg - xh * jnp.mean(dyg * xh, axis=-1, keepdims=True))
                da = _bdot(do, vh, NT)
                dv_h.append(_bdot(a, do, TN) + _bdot(kdh, dst, NT))
                dkd = _bdot(vh, dst, NN)
                delast = jnp.sum(st * dst, axis=0, keepdims=True)
                dqe = _bdot(do, st, NN)
                dstate[hh] = elh * dst + _bdot(do, qe, TN)
                gk = dkd * kdh
                dblast = jnp.sum(gk, axis=0, keepdims=True) + delast * elh
                db = dqe * qe - gk + jnp.where(row == HG_CHUNK - 1, dblast, 0.0)
                dp = (msk_ref[HG_LEVELS] * da).astype(BF16)
                dq = dqe * ebh + _bdot(dp, kh, NN)
                dkk = dkd * edh + _bdot(dp, qh, TN)
                for l in range(HG_LEVELS):
                    dp = (msk_ref[l] * da).astype(BF16)
                    dql, dkl = _bdot(dp, kls[l], NN), _bdot(dp, qls[l], TN)
                    w = cm["ws"][l][:, sl]
                    dq = dq + dql * w
                    dkk = dkk + dkl * w
                    half = jnp.where(((row >> (HG_LEVELS - 1 - l)) & 1) == 1, 1.0, -1.0)
                    dd = half * w * (dql * qh + dkl * kh)
                    db = db + dd
                    dr_h[l].append(-dd)
                dq_h.append(dq)
                dkk_h.append(dkk)
                db_h.append(db)
            cat = lambda xs: jnp.concatenate(xs, axis=1)
            cot = jnp.concatenate([cat(db_h)] + [cat(dr_h[l]) for l in range(HG_LEVELS)], axis=0)
            dlf = _dot01(stk_ref[...], cot, TN, parts=2)
            dq, dkk = cat(dq_h), cat(dkk_h)
            dft = dlf / cm["ft"]
            one_lb = 1.0 - cm["lb"]
            dz_ref[0, rows, :] = (dq * (cm["sq"] * (1.0 + cm["qr"] * (1.0 - cm["sq"])))).astype(dz_ref.dtype)
            dz_ref[1, rows, :] = ((dft - dkk) * one_lb * cm["sg"] * cm["sgn"]).astype(dz_ref.dtype)
            dz_ref[2, rows, :] = cat(dv_h).astype(dz_ref.dtype)
            dz_ref[3, rows, :] = cat(dgt_h).astype(dz_ref.dtype)
            dlb_ref[...] += jnp.sum((dft - dkk) * cm["sgn"], axis=0, keepdims=True)
            dng_ref[...] += dng
            return 0

        lax.fori_loop(0, per, chunk, 0, unroll=True)

    rev = lambda c: nc // per - 1 - c
    rows_blk = pl.BlockSpec((per * HG_CHUNK, HG_SLOT), lambda g, c: (rev(c), g))
    return pl.pallas_call(
        body, name=name, grid=(2, nc // per),
        in_specs=[pl.BlockSpec((4, None, per * HG_CHUNK, HG_SLOT), lambda g, c: (0, g, rev(c), 0)),
                  pl.BlockSpec((1, HG_SLOT), lambda g, c: (0, g)),
                  pl.BlockSpec((1, HG_K), lambda g, c: (0, 0)),
                  pl.BlockSpec(stk.shape, lambda g, c: (0, 0)),
                  pl.BlockSpec(msk.shape, lambda g, c: (0, 0, 0)),
                  pl.BlockSpec((per, 4, HG_K, HG_K), lambda g, c: (rev(c), g, 0, 0)),
                  pl.BlockSpec((per, 4, HG_CHUNK, HG_CHUNK), lambda g, c: (rev(c), g, 0, 0)),
                  rows_blk, rows_blk],
        out_specs=[pl.BlockSpec((4, None, per * HG_CHUNK, HG_SLOT), lambda g, c: (0, g, rev(c), 0)),
                   pl.BlockSpec((1, HG_SLOT), lambda g, c: (0, g)),
                   pl.BlockSpec((None, 1, HG_K), lambda g, c: (g, 0, 0))],
        out_shape=[_sds(z.shape, BF16), _sds((1, 2 * HG_SLOT), F32), _sds((2, 1, HG_K), F32)],
        scratch_shapes=[pltpu.VMEM((4, HG_K, HG_K), F32)],
        compiler_params=_params(("parallel", "arbitrary")),
    )(z, lb, ng, stk, msk, states, intra, o_pre, dog)


def _lb_fwd(name, logits):
    def body(l_ref, o_ref):
        x = l_ref[...]
        e = jnp.exp(x - jnp.max(x, axis=0, keepdims=True))
        s = e / jnp.sum(e, axis=0, keepdims=True)
        o_ref[0:1, :] = s[1:2]
        o_ref[1:2, :] = s[1:2] + s[2:3] + s[3:4]

    return pl.pallas_call(body, name=name, out_shape=_sds((2, logits.shape[1]), F32))(logits)


def _lb_bwd(name, logits, dlb):
    def body(l_ref, d_ref, o_ref):
        x = l_ref[...]
        e = jnp.exp(x - jnp.max(x, axis=0, keepdims=True))
        s = e / jnp.sum(e, axis=0, keepdims=True)
        d1, d3 = d_ref[0:1, :], d_ref[1:2, :]
        ds = [jnp.zeros_like(d1), d1 + d3, d3, d3]
        dot = sum(ds[r] * s[r:r + 1] for r in range(1, DEPTH))
        for r in range(DEPTH):
            o_ref[r:r + 1, :] = s[r:r + 1] * (ds[r] - dot)

    return pl.pallas_call(body, name=name, out_shape=_sds(logits.shape, F32))(logits, dlb)


SUB = 8


def _rows_down(x, prev, k):
    row = lax.broadcasted_iota(jnp.int32, x.shape, 0)
    return jnp.where(row >= k, pltpu.roll(x, k, 0), pltpu.roll(prev, k, 0))


def _rows_up(x, nxt, k):
    row = lax.broadcasted_iota(jnp.int32, x.shape, 0)
    return jnp.where(row < SUB - k, pltpu.roll(x, SUB - k, 0), pltpu.roll(nxt, SUB - k, 0))


def _conv_block(w_ref, b_ref, p, x, prev):
    return (b_ref[p] + w_ref[p, 0:1, :] * _rows_down(x, prev, 2) + w_ref[p, 1:2, :] * _rows_down(x, prev, 1)
            + w_ref[p, 2:3, :] * x)


def _convgate_fwd(name, u, cw, cb):
    S = u.shape[2]
    tm = _tile(S, ROW_TILE)

    def body(u_ref, w_ref, b_ref, a_ref, c_ref, halo):
        @pl.when(pl.program_id(1) == 0)
        def _():
            halo[...] = jnp.zeros_like(halo)

        def step(r, prev):
            pg, pv = prev
            out, cgs, cvs = [], [], []
            for s in range(2):
                rows = pl.ds(pl.multiple_of(r * 2 * SUB + s * SUB, SUB), SUB)
                xg, xv = u_ref[0, rows, :], u_ref[1, rows, :]
                cgs.append(_conv_block(w_ref, b_ref, 0, xg, pg))
                cvs.append(_conv_block(w_ref, b_ref, 1, xv, pv))
                out.append(cgs[s] * _sigmoid(cgs[s]) * cvs[s])
                pg, pv = xg, xv
            rows = pl.ds(pl.multiple_of(r * 2 * SUB, 2 * SUB), 2 * SUB)
            a_ref[rows, :] = jnp.concatenate(out, axis=0).astype(a_ref.dtype)
            c_ref[0, rows, :] = jnp.concatenate(cgs, axis=0).astype(c_ref.dtype)
            c_ref[1, rows, :] = jnp.concatenate(cvs, axis=0).astype(c_ref.dtype)
            return pg, pv

        pg, pv = lax.fori_loop(0, tm // (2 * SUB), step, (halo[0], halo[1]), unroll=2)
        halo[0] = pg
        halo[1] = pv

    pair = pl.BlockSpec((2, None, tm, FF_SLOT), lambda j, t: (0, j, t, 0))
    return pl.pallas_call(
        body, name=name, grid=(4, S // tm),
        in_specs=[pair, pl.BlockSpec((2, None, 3, FF_SLOT), lambda j, t: (0, j, 0, 0)),
                  pl.BlockSpec((2, None, 1, FF_SLOT), lambda j, t: (0, j, 0, 0))],
        out_specs=[pl.BlockSpec((None, tm, FF_SLOT), lambda j, t: (j, t, 0)), pair],
        out_shape=[_sds((4, S, FF_SLOT), BF16), _sds(u.shape, BF16)],
        scratch_shapes=[pltpu.VMEM((2, SUB, FF_SLOT), F32)],
        compiler_params=_params(("parallel", "arbitrary")),
    )(u, cw, cb)


def _convgate_bwd(name, u, convs, cw, da):
    S = u.shape[2]
    tm = _tile(S, ROW_TILE)
    nt = S // tm

    def body(u_ref, c_ref, w_ref, da_ref, du_out, dw_ref, db_ref, after, first, acc, du_ref):
        @pl.when(pl.program_id(1) == 0)
        def _():
            after[...] = jnp.zeros_like(after)
            acc[...] = jnp.zeros_like(acc)

        def finish(p, x, d, nxt, rows):
            taps = (_rows_up(d, nxt, 2), _rows_up(d, nxt, 1), d)
            du_ref[p, rows, :] = w_ref[p, 0:1, :] * taps[0] + w_ref[p, 1:2, :] * taps[1] + w_ref[p, 2:3, :] * d
            for j in range(3):
                acc[p, j] += taps[j] * x
            acc[p, 3] += d

        def step(r, carry):
            xg_last, xv_last, dg_last, dv_last = carry
            rows16 = pl.ds(pl.multiple_of(r * 2 * SUB, 2 * SUB), 2 * SUB)
            dav = da_ref[rows16, :].astype(F32)
            cg16, cv16 = c_ref[0, rows16, :].astype(F32), c_ref[1, rows16, :].astype(F32)
            for s in range(2):
                at = r * 2 * SUB + s * SUB
                part = slice(s * SUB, (s + 1) * SUB)
                cg, cv, dab = cg16[part], cv16[part], dav[part]
                sg = _sigmoid(cg)
                dg = dab * cv * (sg * (1.0 + cg * (1.0 - sg)))
                dv = dab * cg * sg
                before = pl.ds(pl.multiple_of(at - SUB, SUB), SUB)
                if s == 0:
                    @pl.when(r == 0)
                    def _():
                        first[0] = dg
                        first[1] = dv

                    @pl.when(r > 0)
                    def _():
                        finish(0, xg_last, dg_last, dg, before)
                        finish(1, xv_last, dv_last, dv, before)
                else:
                    finish(0, xg_last, dg_last, dg, before)
                    finish(1, xv_last, dv_last, dv, before)
                rows = pl.ds(pl.multiple_of(at, SUB), SUB)
                xg_last, xv_last, dg_last, dv_last = u_ref[0, rows, :], u_ref[1, rows, :], dg, dv
            return xg_last, xv_last, dg_last, dv_last

        zero = jnp.zeros((SUB, FF_SLOT), F32)
        xg_last, xv_last, dg_last, dv_last = lax.fori_loop(0, tm // (2 * SUB), step, (zero, zero, zero, zero))
        finish(0, xg_last, dg_last, after[0], slice(tm - SUB, tm))
        finish(1, xv_last, dv_last, after[1], slice(tm - SUB, tm))
        du_out[...] = du_ref[...].astype(du_out.dtype)
        after[...] = first[...]
        for p in range(2):
            for j in range(3):
                dw_ref[p, j:j + 1, :] = jnp.sum(acc[p, j], axis=0, keepdims=True)
            db_ref[p] = jnp.sum(acc[p, 3], axis=0, keepdims=True)

    rev = lambda t: nt - 1 - t
    pair = pl.BlockSpec((2, None, tm, FF_SLOT), lambda j, t: (0, j, rev(t), 0))
    taps = pl.BlockSpec((2, None, 3, FF_SLOT), lambda j, t: (0, j, 0, 0))
    bias = pl.BlockSpec((2, None, 1, FF_SLOT), lambda j, t: (0, j, 0, 0))
    return pl.pallas_call(
        body, name=name, grid=(4, nt),
        in_specs=[pair, pair, taps, pl.BlockSpec((None, tm, FF_SLOT), lambda j, t: (j, rev(t), 0))],
        out_specs=[pair, taps, bias],
        out_shape=[_sds(u.shape, BF16), _sds(cw.shape, F32), _sds((2, 4, 1, FF_SLOT), F32)],
        scratch_shapes=[pltpu.VMEM((2, SUB, FF_SLOT), F32), pltpu.VMEM((2, SUB, FF_SLOT), F32),
                        pltpu.VMEM((2, 4, SUB, FF_SLOT), F32), pltpu.VMEM((2, tm, FF_SLOT), F32)],
        compiler_params=_params(("parallel", "arbitrary")),
    )(u, convs, cw, da)


def _row_tile(R):
    for t in range(256, 15, -16):
        if R % t == 0:
            return t
    return R


def _adamw(name, gsrcs, w, m, v, dep=None):
    L = len(gsrcs)
    n, A, C = gsrcs[0].shape
    tr = _row_tile(A)
    deps = () if dep is None else (dep,)

    def body(*refs):
        g_refs = refs[:L]
        w_ref, m_ref, v_ref = refs[L:L + 3]
        go_ref, d_ref, mo_ref, vo_ref = refs[L + 3 + len(deps):]
        for k in range(L):
            @pl.when(pl.program_id(0) == k)
            def _(k=k):
                g = g_refs[k][0].astype(F32)
                for s in range(1, n):
                    g = g + g_refs[k][s].astype(F32)
                m2 = ADAM_B1 * m_ref[...] + (1.0 - ADAM_B1) * g
                v2 = ADAM_B2 * v_ref[...] + (1.0 - ADAM_B2) * (g * g)
                m_hat = m2 / (1.0 - ADAM_B1 ** ADAM_STEP)
                v_hat = v2 / (1.0 - ADAM_B2 ** ADAM_STEP)
                go_ref[...] = g
                d_ref[...] = -ADAM_LR * (m_hat / (jnp.sqrt(v_hat) + ADAM_EPS) + ADAM_WD * w_ref[...])
                mo_ref[...] = m2
                vo_ref[...] = v2

    g_specs = [pl.BlockSpec((n, tr, C), lambda l, i, k=k: (0, jnp.where(l == k, i, 0), 0)) for k in range(L)]
    blk = pl.BlockSpec((None, tr, C), lambda l, i: (l, i, 0))
    return pl.pallas_call(
        body, name=name, grid=(L, A // tr), in_specs=g_specs + [blk, blk, blk] + [_dep_spec(2)] * len(deps),
        out_specs=[blk] * 4, out_shape=[_sds((L, A, C), F32)] * 4, compiler_params=_params(("parallel", "parallel")),
    )(*gsrcs, w, m, v, *deps)


MESH = pl.DeviceIdType.MESH
HBM_SPEC = pl.BlockSpec(memory_space=pltpu.HBM)
N_PEERS = N_DEV - 1


def _mesh_place():
    x, y, c = lax.axis_index("x"), lax.axis_index("y"), lax.axis_index("c")
    peers = []
    for p in range(1, N_DEV):
        px = 1 - x if p & 4 else x
        py = 1 - y if p & 2 else y
        pc = 1 - c if p & 1 else c
        peers.append(((px, py, pc), 4 * px + 2 * py + pc))
    return 4 * x + 2 * y + c, peers


SEM_SPEC = pl.BlockSpec(memory_space=pltpu.SEMAPHORE)
ANY_SPEC = pl.BlockSpec(memory_space=pl.ANY)
EFFECT = pltpu.SideEffectType.DATAFLOW_SIDE_EFFECTING


def _scatters(scatter, k):
    return scatter if isinstance(scatter, bool) else scatter[k]


def _exchange_refs(scatter, src, land, send, recv, k, p, dev, idx, me):
    return pltpu.make_async_remote_copy(src_ref=src[k].at[idx] if _scatters(scatter, k) else src[k], dst_ref=land[k].at[me],
                                        send_sem=send.at[k * N_PEERS + p], recv_sem=recv.at[k * N_PEERS + p], device_id=dev,
                                        device_id_type=MESH)


def _exchange_start(name, srcs, scatter, gate):
    n = len(srcs)
    lands = [lax.empty(s.shape if _scatters(scatter, k) else (N_DEV,) + s.shape, s.dtype) for k, s in enumerate(srcs)]

    def body(*refs):
        src, land = refs[:n], refs[n:2 * n]
        send, recv, own = refs[2 * n + 1:2 * n + 4]
        token = refs[-1]
        me, peers = _mesh_place()
        for k in range(n):
            pltpu.make_async_copy(src[k].at[me] if _scatters(scatter, k) else src[k], land[k].at[me], own.at[k]).start()
            for p, (dev, idx) in enumerate(peers):
                _exchange_refs(scatter, src, land, send, recv, k, p, dev, idx, me).start()
        token[...] = jnp.zeros_like(token)

    hbm = lambda a: pltpu.HBM(a.shape, a.dtype)
    outs = pl.pallas_call(
        body, name=name,
        out_shape=(pltpu.SemaphoreType.DMA((n * N_PEERS,)), pltpu.SemaphoreType.DMA((n * N_PEERS,)),
                   pltpu.SemaphoreType.DMA((n,)), *[hbm(s) for s in srcs], *[hbm(s) for s in lands], _sds(DEP_SHAPE, F32)),
        in_specs=[HBM_SPEC] * (2 * n) + [ANY_SPEC],
        out_specs=(SEM_SPEC, SEM_SPEC, SEM_SPEC, *[HBM_SPEC] * (2 * n), pl.BlockSpec(memory_space=pltpu.VMEM)),
        input_output_aliases={j: 3 + j for j in range(2 * n)},
        compiler_params=pltpu.CompilerParams(has_side_effects=EFFECT),
    )(*[pltpu.with_memory_space_constraint(s, pltpu.HBM) for s in srcs],
      *[pltpu.with_memory_space_constraint(s, pltpu.HBM) for s in lands], gate)
    return outs[:3], None, list(outs[3:3 + n]), list(outs[3 + n:3 + 2 * n]), outs[-1]


def _exchange_wait(name, started, scatter, after):
    (send, recv, own), _, srcs, lands, _ = started
    n = len(srcs)

    def body(*refs):
        src, land = refs[:n], refs[n:2 * n]
        send, recv, own = refs[2 * n:2 * n + 3]
        me, peers = _mesh_place()
        for k in range(n):
            pltpu.make_async_copy(src[k].at[me] if _scatters(scatter, k) else src[k], land[k].at[me], own.at[k]).wait()
            for p, (dev, idx) in enumerate(peers):
                cp = pltpu.make_async_remote_copy(src_ref=src[k].at[idx] if _scatters(scatter, k) else src[k], dst_ref=land[k].at[idx],
                                                  send_sem=send.at[k * N_PEERS + p], recv_sem=recv.at[k * N_PEERS + p], device_id=dev,
                                                  device_id_type=MESH)
                cp.wait_send()
                cp.wait_recv()

    hbm = lambda a: pltpu.HBM(a.shape, a.dtype)
    outs = pl.pallas_call(
        body, name=name, out_shape=(*[hbm(s) for s in srcs], *[hbm(s) for s in lands]),
        in_specs=[HBM_SPEC] * (2 * n) + [SEM_SPEC, SEM_SPEC, SEM_SPEC, ANY_SPEC], out_specs=tuple([HBM_SPEC] * (2 * n)),
        input_output_aliases={j: j for j in range(2 * n)},
        compiler_params=pltpu.CompilerParams(has_side_effects=EFFECT),
    )(*srcs, *lands, send, recv, own, after)
    return list(outs[n:])


def _sum_devices(name, parts):
    def body(p_ref, o_ref):
        tot = p_ref[0]
        for j in range(1, N_DEV):
            tot = tot + p_ref[j]
        o_ref[...] = tot

    return pl.pallas_call(body, name=name, out_shape=_sds(parts.shape[1:], F32),
                          compiler_params=pltpu.CompilerParams(vmem_limit_bytes=VMEM_LIMIT))(parts)


def _rows(a, width=D_MODEL):
    flat = a.reshape(-1)
    return jnp.pad(flat, (0, (-flat.shape[0]) % width)).reshape(-1, width)


def _pack_rows(parts):
    blocks = []
    for p in parts:
        r = _rows(p)
        blocks.append(jnp.pad(r, ((0, (-r.shape[0]) % 8), (0, 0))))
    return jnp.concatenate(blocks, axis=0)


def _unpack_rows(rows, shapes):
    out, at = [], 0
    for s in shapes:
        size = int(np.prod(s))
        n = -(-size // D_MODEL)
        out.append(rows[at:at + n].reshape(-1)[:size].reshape(s))
        at += -(-n // 8) * 8
    return out


def kernel(x, norm_mix, norm_ffn, norm_final, attn_w_in, attn_w_out, attn_sinks, hgrn_w_in, hgrn_w_out, hgrn_norm, hgrn_lb_logits, ffn_w_up, ffn_conv_w, ffn_conv_b, ffn_w_down, loss_target, m_norm_mix, m_norm_ffn, m_norm_final, m_attn_w_in, m_attn_w_out, m_attn_sinks, m_hgrn_w_in, m_hgrn_w_out, m_hgrn_norm, m_hgrn_lb_logits, m_ffn_w_up, m_ffn_conv_w, m_ffn_conv_b, m_ffn_w_down, v_norm_mix, v_norm_ffn, v_norm_final, v_attn_w_in, v_attn_w_out, v_attn_sinks, v_hgrn_w_in, v_hgrn_w_out, v_hgrn_norm, v_hgrn_lb_logits, v_ffn_w_up, v_ffn_conv_w, v_ffn_conv_b, v_ffn_w_down):
    S = x.shape[1]
    n_attn, n_hgrn = attn_w_in.shape[0], hgrn_w_in.shape[0]

    wa_in_t, wa_out_b = attn_w_in.transpose(0, 2, 1).astype(BF16), attn_w_out.astype(BF16)
    wh_in_b, wh_out_b = hgrn_w_in.astype(BF16), hgrn_w_out.astype(BF16)
    wf_up_b, wf_down_b = ffn_w_up.transpose(0, 2, 1).astype(BF16), ffn_w_down.astype(BF16)
    conv_b = ffn_conv_b.reshape(DEPTH, 2, 4, 1, FF_SLOT)
    lb = _lb_fwd("lb_fwd", hgrn_lb_logits)

    def unit_shards(l, part):
        if part == "ffn":
            return [wf_up_b[l], wf_down_b[l], ffn_conv_w[l]]
        return [wa_in_t[l // 2], wa_out_b[l // 2]] if l % 2 == 0 else [wh_in_b[l // 2], wh_out_b[l // 2]]

    def unit_weights(l, part, w):
        if part == "ffn":
            return w[0][None], w[1].reshape(1, 4, FF_SLOT, D_MODEL), w[2].reshape(2, 4, 3, FF_SLOT)
        if l % 2 == 0:
            return w[0].reshape(1, ATTN_IN, D_MODEL), w[1].reshape(1, D_MODEL, D_MODEL)
        return w[0][None], w[1].reshape(1, D_MODEL, D_MODEL)

    units = [(l, part) for l in range(DEPTH) for part in ("mix", "ffn")]
    gathers = [_exchange_start("gather_start0", unit_shards(*units[0]), False, norm_final)]
    gathers.append(_exchange_start("gather_start1", unit_shards(*units[1]), False, gathers[0][4]))
    h = x[0]
    hn = _rmsnorm_fwd("norm_mix_fwd0", h, norm_mix[0:1])
    arrived = _exchange_wait("gather_wait0", gathers[0], False, hn)
    weights, saved = {}, [dict() for _ in range(DEPTH)]
    for n, (l, part) in enumerate(units):
        i, sv = l // 2, saved[l]
        weights[l, part] = w = unit_weights(l, part, arrived)
        dep = None
        if n + 2 < len(units):
            gathers.append(_exchange_start(f"gather_start{n + 2}", unit_shards(*units[n + 2]), False, arrived[0]))
            dep = gathers[n + 2][4]
        if part == "mix":
            sv["h"], sv["hn"] = h, hn
            if l % 2 == 0:
                sv["proj"] = _proj_rows(f"attn_proj{i}", hn, w[0], 0, BF16, dep)
                sv["o"], *sv["kept"] = _attn_fwd(f"attn_fwd{i}", sv["proj"], attn_sinks[i:i + 1])
                h, hn = _out_proj(f"attn_out{i}", sv["o"], w[1], 0, h, norm_ffn[l:l + 1])
            else:
                sv["z"] = _proj_slots(f"hgrn_proj{i}", hn, w[0], 0, dep=dep).reshape(4, 2, S, HG_SLOT)
                sv["o"], *sv["kept"] = _hg_fwd(f"hgrn_fwd{i}", sv["z"], lb[i:i + 1], hgrn_norm[i:i + 1])
                h, hn = _out_proj(f"hgrn_out{i}", sv["o"], w[1], 0, h, norm_ffn[l:l + 1])
        else:
            sv["h2"], sv["hn2"] = h, hn
            sv["u"] = _proj_slots(f"ffn_up{l}", hn, w[0], 0, True, dep).reshape(2, 4, S, FF_SLOT)
            sv["a"], sv["convs"] = _convgate_fwd(f"ffn_gate{l}", sv["u"], w[2], conv_b[l])
            if l + 1 < DEPTH:
                h, hn = _down_proj(f"ffn_down{l}", sv["a"], w[1], 0, h, norm_mix[l + 1:l + 2])
            else:
                h = _down_proj(f"ffn_down{l}", sv["a"], w[1], 0, h)
        if n + 1 < len(units):
            arrived = _exchange_wait(f"gather_wait{n + 1}", gathers[n + 1], False, h)
    dh, d_norm_final, loss_rows = _loss_head("loss_head", h, norm_final[None], loss_target[0])

    d_conv_w, d_conv_b, d_norm_mix, d_norm_ffn = [None] * DEPTH, [None] * DEPTH, [None] * DEPTH, [None] * DEPTH
    d_sinks, d_lb, d_hgrn_norm = [None] * n_attn, [None] * n_hgrn, [None] * n_hgrn
    received, pending = {}, []
    for l, part in reversed(units):
        i, sv, w = l // 2, saved[l], weights[l, part]
        dep = pending[-1][1][4] if pending else None
        if part == "ffn":
            da = _dgrad_down(f"ffn_down_dgrad{l}", dh, w[1], 0, dep)
            g_down = _wgrad_down(f"ffn_down_wgrad{l}", sv["a"], dh).reshape(N_DEV, D_FF // N_DEV, D_MODEL)
            du, d_conv_w[l], d_conv_b[l] = _convgate_bwd(f"ffn_gate_bwd{l}", sv["u"], sv["convs"], w[2], da)
            du = du.reshape(N_DEV, S, FF_SLOT)
            grads = [_wgrad_slots(f"ffn_up_wgrad{l}", sv["hn2"], du, True), g_down]
            dh, d_norm_ffn[l] = _dgrad_slots(f"ffn_up_dgrad{l}", du, w[0], 0, (sv["h2"], norm_ffn[l:l + 1], dh), True)
        else:
            if l % 2 == 0:
                do = _dgrad_out(f"attn_out_dgrad{i}", dh, w[1], 0, BF16, dep)
                g_out = _wgrad_rows(f"attn_out_wgrad{i}", sv["o"], dh)
                dproj, d_sinks[i] = _attn_bwd(f"attn_bwd{i}", sv["proj"], *sv["kept"], do)
                g_in = _wgrad_rows(f"attn_proj_wgrad{i}", dproj, sv["hn"]).reshape(N_DEV, ATTN_IN // N_DEV, D_MODEL)
                dh_new = _dgrad_rows(f"attn_proj_dgrad{i}", dproj, w[0], 0, (sv["h"], norm_mix[l:l + 1], dh))
            else:
                dog = _dgrad_out(f"hgrn_out_dgrad{i}", dh, w[1], 0, F32, dep)
                g_out = _wgrad_rows(f"hgrn_out_wgrad{i}", sv["o"], dh)
                dz, d_lb[i], dng = _hg_bwd(f"hgrn_bwd{i}", sv["z"], lb[i:i + 1], hgrn_norm[i:i + 1], *sv["kept"], dog)
                d_hgrn_norm[i] = dng[0] + dng[1]
                dz = dz.reshape(N_DEV, S, HG_SLOT)
                g_in = _wgrad_slots(f"hgrn_proj_wgrad{i}", sv["hn"], dz)
                dh_new = _dgrad_slots(f"hgrn_proj_dgrad{i}", dz, w[0], 0, (sv["h"], norm_mix[l:l + 1], dh))
            grads = [g_in, g_out.reshape(N_DEV, D_MODEL // N_DEV, D_MODEL)]
            dh, d_norm_mix[l] = dh_new
        gate = dh
        if len(pending) == 2:
            key, oldest = pending.pop(0)
            received[key] = _exchange_wait(f"scatter_wait_{key[1]}{key[0]}", oldest, True, dh)
            gate = received[key][0]
        pending.append(((l, part), _exchange_start(f"scatter_start_{part}{l}", grads, True, gate)))
    grad_x = dh[None]

    small_shapes = [(DEPTH, D_MODEL), (DEPTH, D_MODEL), (1, D_MODEL), (1, D_MODEL), (n_hgrn, D_MODEL), (n_attn, 128),
                    (n_hgrn, HG_K), (DEPTH, 2 * D_FF)]
    partial = _pack_rows([
        jnp.concatenate(d_norm_mix), jnp.concatenate(d_norm_ffn), d_norm_final, loss_rows, jnp.concatenate(d_lb),
        jnp.concatenate(d_sinks), jnp.concatenate(d_hgrn_norm), jnp.stack(d_conv_b)])
    d_taps = jnp.stack(d_conv_w).reshape(DEPTH, N_DEV, 3, FF_SLOT).transpose(1, 0, 2, 3).reshape(N_DEV, DEPTH * 3, FF_SLOT)
    small_started = _exchange_start("small_start", [partial, d_taps], (False, True), pending[-1][1][4])
    attn_layers, hgrn_layers = range(0, DEPTH, 2), range(1, DEPTH, 2)

    def transposed(ts):
        return [t.transpose(0, 2, 1) for t in ts]

    big = {"hgrn_w_in": _adamw("adamw_hgrn_in", [received[l, "mix"][0] for l in hgrn_layers], hgrn_w_in, m_hgrn_w_in,
                               v_hgrn_w_in, dep=small_started[4])}
    big["hgrn_w_out"] = _adamw("adamw_hgrn_out", [received[l, "mix"][1] for l in hgrn_layers], hgrn_w_out, m_hgrn_w_out, v_hgrn_w_out)
    key, oldest = pending.pop(0)
    received[key] = _exchange_wait(f"scatter_wait_{key[1]}{key[0]}", oldest, True, big["hgrn_w_in"][3])
    up_t = _adamw("adamw_ffn_up", [received[l, "ffn"][0] for l in range(DEPTH)], *transposed((ffn_w_up, m_ffn_w_up, v_ffn_w_up)))
    big["ffn_w_up"] = transposed(up_t)
    big["ffn_w_down"] = _adamw("adamw_ffn_down", [received[l, "ffn"][1] for l in range(DEPTH)], ffn_w_down, m_ffn_w_down, v_ffn_w_down)
    key, oldest = pending.pop(0)
    received[key] = _exchange_wait(f"scatter_wait_{key[1]}{key[0]}", oldest, True, up_t[3])
    small_parts, taps_parts = _exchange_wait("small_wait", small_started, (False, True), up_t[3])
    total = _sum_devices("sum_small", small_parts)
    (g_norm_mix, g_norm_ffn, g_norm_final, loss_sum, g_lb, g_sinks, g_hgrn_norm, g_conv_b) = _unpack_rows(total, small_shapes)

    loss = jnp.sum(loss_sum)
    g_norm_final = g_norm_final[0]
    g_sinks = g_sinks[:, :N_Q_HEADS]
    g_lb_logits = _lb_bwd("lb_bwd", hgrn_lb_logits, g_lb)

    big.update({
        "attn_w_in": transposed(_adamw("adamw_attn_in", [received[l, "mix"][0] for l in attn_layers],
                                       *transposed((attn_w_in, m_attn_w_in, v_attn_w_in)))),
        "attn_w_out": _adamw("adamw_attn_out", [received[l, "mix"][1] for l in attn_layers], attn_w_out, m_attn_w_out, v_attn_w_out),
        "ffn_conv_w": _adamw("adamw_conv_w", [taps_parts[:, 3 * l:3 * l + 3] for l in range(DEPTH)], ffn_conv_w, m_ffn_conv_w,
                             v_ffn_conv_w),
    })
    small_w = [norm_mix, norm_ffn, norm_final, attn_sinks, hgrn_norm, hgrn_lb_logits, ffn_conv_b]
    small_m = [m_norm_mix, m_norm_ffn, m_norm_final, m_attn_sinks, m_hgrn_norm, m_hgrn_lb_logits, m_ffn_conv_b]
    small_v = [v_norm_mix, v_norm_ffn, v_norm_final, v_attn_sinks, v_hgrn_norm, v_hgrn_lb_logits, v_ffn_conv_b]
    small_g = [g_norm_mix, g_norm_ffn, g_norm_final, g_sinks, g_hgrn_norm, g_lb_logits, g_conv_b]
    outs = _adamw("adamw_small", [_pack_rows(small_g)[None]], *[_pack_rows(t)[None] for t in (small_w, small_m, small_v)])
    outs = [o[0] for o in outs]
    shapes = [w.shape for w in small_w]
    small = {n: [t[j] for t in [_unpack_rows(o, shapes) for o in outs]]
             for j, n in enumerate(["norm_mix", "norm_ffn", "norm_final", "attn_sinks", "hgrn_norm", "hgrn_lb_logits", "ffn_conv_b"])}
    order = ["norm_mix", "norm_ffn", "norm_final", "attn_w_in", "attn_w_out", "attn_sinks", "hgrn_w_in", "hgrn_w_out",
             "hgrn_norm", "hgrn_lb_logits", "ffn_w_up", "ffn_conv_w", "ffn_conv_b", "ffn_w_down"]
    res = {**big, **small}
    return (loss, grad_x, *[res[n][0] for n in order], *[res[n][1] for n in order], *[res[n][2] for n in order],
            *[res[n][3] for n in order])
```

```python
import numpy as np
import jax
import jax.numpy as jnp
from jax import lax
from jax.experimental import pallas as pl
from jax.experimental.pallas import tpu as pltpu

F32 = jnp.float32
BF16 = jnp.bfloat16

D_MODEL = 1024
DEPTH = 4
HEAD_DIM = 64
N_Q_HEADS = 16
N_KV_HEADS = 4
Q_PER_KV = 4
ATTN_BLOCK = 128
ATTN_IN = 1536
HG_HEADS = 8
HG_K = 128
HG_CHUNK = 64
HG_IN = 4096
D_FF = 2816
EPS = 1e-6
N_DEV = 8
FF_SLOT = 2 * D_FF // N_DEV
HG_SLOT = HG_IN // N_DEV
HG_LEVELS = 6

ADAM_LR = 0.001
ADAM_B1 = 0.9
ADAM_B2 = 0.999
ADAM_EPS = 1e-08
ADAM_WD = 0.01
ADAM_STEP = 10

VMEM_LIMIT = 56 * 1024 * 1024
ROW_TILE = 1024
WIDE_ROW_TILE = 2048
NEG_BIG = -1e30

NN = (((1,), (0,)), ((), ()))
NT = (((1,), (1,)), ((), ()))
TN = (((0,), (0,)), ((), ()))


def _bdot(a, b, dn):
    return lax.dot_general(a.astype(BF16), b.astype(BF16), dn, preferred_element_type=F32)


def _sds(shape, dtype):
    return jax.ShapeDtypeStruct(tuple(shape), dtype)


def _params(sem):
    return pltpu.CompilerParams(dimension_semantics=sem, vmem_limit_bytes=VMEM_LIMIT)


DEP_SHAPE = (8, 128)


def _dep_spec(rank):
    return pl.BlockSpec(DEP_SHAPE, lambda *_: (0, 0))


def _matmul(name, a, b, *, dn, grid, a_spec, b_spec, o_spec, out_shape, acc_shape=None, extra=(), extra_specs=(),
            finish=None, dep=None, sem=("parallel", "parallel", "arbitrary")):
    nk = grid[2]
    many = isinstance(out_shape, (list, tuple))
    n_in = 2 + len(extra) + (dep is not None)
    n_out = len(out_shape) if many else 1

    def body(*refs):
        a_ref, b_ref = refs[0], refs[1]
        outs = refs[n_in:n_in + n_out]

        def prod():
            return _bdot(a_ref[...], b_ref[...], dn)

        def done(v):
            if finish is None:
                outs[0][...] = v.astype(outs[0].dtype)
            else:
                finish(v, refs[2:2 + len(extra)], outs)

        if nk == 1:
            done(prod())
        else:
            acc = refs[-1]
            k = pl.program_id(2)

            @pl.when(k == 0)
            def _():
                acc[...] = prod()

            @pl.when(k > 0)
            def _():
                acc[...] += prod()

            @pl.when(k == nk - 1)
            def _():
                done(acc[...])

    in_specs = [a_spec, b_spec, *extra_specs] + ([_dep_spec(3)] if dep is not None else [])
    args = (a, b, *extra) + ((dep,) if dep is not None else ())
    scratch = [] if nk == 1 else [pltpu.VMEM(acc_shape, F32)]
    return pl.pallas_call(
        body, name=name, grid=grid, in_specs=in_specs, out_specs=o_spec, out_shape=out_shape,
        scratch_shapes=scratch, compiler_params=_params(sem),
    )(*args)


def _rms(x):
    return lax.rsqrt(jnp.mean(x * x, axis=-1, keepdims=True) + EPS)


def _residual_finish(v, ex, outs):
    h = v + ex[0][...]
    outs[0][...] = h
    if len(ex) > 1:
        outs[1][...] = (h * _rms(h) * ex[1][...]).astype(outs[1].dtype)


def _norm_bwd_finish(v, ex, outs):
    x = ex[0][...]
    r = _rms(x)
    xh = x * r
    dyg = v * ex[1][...]
    outs[0][...] = ex[2][...] + r * (dyg - xh * jnp.mean(dyg * xh, axis=-1, keepdims=True))
    part = jnp.sum(v * xh, axis=0, keepdims=True)

    @pl.when(pl.program_id(0) == 0)
    def _():
        outs[1][...] = part

    @pl.when(pl.program_id(0) > 0)
    def _():
        outs[1][...] += part


def _row_io(tm, norm_g):
    row = pl.BlockSpec((tm, D_MODEL), lambda i, j, k: (i, 0))
    vec = pl.BlockSpec((1, D_MODEL), lambda i, j, k: (0, 0))
    if norm_g is None:
        return (row,), row, lambda S: _sds((S, D_MODEL), F32)
    return (row, vec), [row, row], lambda S: [_sds((S, D_MODEL), F32), _sds((S, D_MODEL), BF16)]


def _tile(n, t):
    return min(n, t)


def _proj_rows(name, hn, wt, l, out_dtype, dep=None):
    S, N = hn.shape[0], wt.shape[1]
    tm, tn = _tile(S, ROW_TILE), 512
    return _matmul(
        name, hn, wt, dn=NT, grid=(S // tm, N // tn, 1),
        a_spec=pl.BlockSpec((tm, D_MODEL), lambda i, j, k: (i, 0)),
        b_spec=pl.BlockSpec((None, tn, D_MODEL), lambda i, j, k: (l, j, 0)),
        o_spec=pl.BlockSpec((tm, tn), lambda i, j, k: (i, j)),
        out_shape=_sds((S, N), out_dtype), dep=dep)


def _slot_weight(w, transposed):
    if transposed:
        return w.shape[2], (None, None, w.shape[2], D_MODEL), NT, NN
    return w.shape[3], (None, None, D_MODEL, w.shape[3]), NN, NT


def _proj_slots(name, hn, w, l, transposed=False, dep=None, out_dtype=F32):
    S = hn.shape[0]
    r, blk, dn, _ = _slot_weight(w, transposed)
    tm = _tile(S, WIDE_ROW_TILE)
    return _matmul(
        name, hn, w, dn=dn, grid=(N_DEV, S // tm, 1),
        a_spec=pl.BlockSpec((tm, D_MODEL), lambda j, i, k: (i, 0)),
        b_spec=pl.BlockSpec(blk, lambda j, i, k: (l, j, 0, 0)),
        o_spec=pl.BlockSpec((None, tm, r), lambda j, i, k: (j, i, 0)),
        out_shape=_sds((N_DEV, S, r), out_dtype), dep=dep)


def _out_proj(name, o, w, l, h, norm_g=None):
    S, K = o.shape
    tm = _tile(S, ROW_TILE)
    extra_specs, o_spec, out_shape = _row_io(tm, norm_g)
    return _matmul(
        name, o, w, dn=NN, grid=(S // tm, 1, 1),
        a_spec=pl.BlockSpec((tm, K), lambda i, j, k: (i, 0)),
        b_spec=pl.BlockSpec((None, K, D_MODEL), lambda i, j, k: (l, 0, 0)),
        o_spec=o_spec, out_shape=out_shape(S), extra=(h,) if norm_g is None else (h, norm_g),
        extra_specs=extra_specs, finish=_residual_finish)


def _down_proj(name, a, w, l, h, norm_g=None):
    nj, S, r = a.shape
    tm = _tile(S, ROW_TILE)
    extra_specs, o_spec, out_shape = _row_io(tm, norm_g)
    return _matmul(
        name, a, w, dn=NN, grid=(S // tm, 1, nj),
        a_spec=pl.BlockSpec((None, tm, r), lambda i, j, k: (k, i, 0)),
        b_spec=pl.BlockSpec((None, None, r, D_MODEL), lambda i, j, k: (l, k, 0, 0)),
        o_spec=o_spec, out_shape=out_shape(S), acc_shape=(tm, D_MODEL),
        extra=(h,) if norm_g is None else (h, norm_g), extra_specs=extra_specs, finish=_residual_finish)


def _dgrad_down(name, dh, w, l, dep=None):
    S = dh.shape[0]
    nj, r = w.shape[1], w.shape[2]
    tm = _tile(S, ROW_TILE)
    return _matmul(
        name, dh, w, dn=NT, grid=(nj, S // tm, 1),
        a_spec=pl.BlockSpec((tm, D_MODEL), lambda j, i, k: (i, 0)),
        b_spec=pl.BlockSpec((None, None, r, D_MODEL), lambda j, i, k: (l, j, 0, 0)),
        o_spec=pl.BlockSpec((None, tm, r), lambda j, i, k: (j, i, 0)),
        out_shape=_sds((nj, S, r), BF16), dep=dep)


def _wgrad_down(name, a, dh):
    nj, S, r = a.shape
    tk = _tile(S, ROW_TILE)
    return _matmul(
        name, a, dh, dn=TN, grid=(nj, 1, S // tk),
        a_spec=pl.BlockSpec((None, tk, r), lambda s, j, k: (s, k, 0)),
        b_spec=pl.BlockSpec((tk, D_MODEL), lambda s, j, k: (k, 0)),
        o_spec=pl.BlockSpec((None, r, D_MODEL), lambda s, j, k: (s, 0, 0)),
        out_shape=_sds((nj, r, D_MODEL), BF16), acc_shape=(r, D_MODEL))


def _norm_bwd_io(tm, S):
    row = pl.BlockSpec((tm, D_MODEL), lambda i, j, k: (i, 0))
    vec = pl.BlockSpec((1, D_MODEL), lambda i, j, k: (0, 0))
    return dict(extra_specs=(row, vec, row), o_spec=[row, vec], out_shape=[_sds((S, D_MODEL), F32), _sds((1, D_MODEL), F32)],
                finish=_norm_bwd_finish, sem=("arbitrary", "arbitrary", "arbitrary"))


def _dgrad_slots(name, dz, w, l, norm, transposed=False):
    nj, S, r = dz.shape
    _, blk, _, dn = _slot_weight(w, transposed)
    tm = _tile(S, ROW_TILE)
    return _matmul(
        name, dz, w, dn=dn, grid=(S // tm, 1, nj),
        a_spec=pl.BlockSpec((None, tm, r), lambda i, j, k: (k, i, 0)),
        b_spec=pl.BlockSpec(blk, lambda i, j, k: (l, k, 0, 0)),
        acc_shape=(tm, D_MODEL), extra=norm, **_norm_bwd_io(tm, S))


def _wgrad_slots(name, hn, dz, transposed=False):
    nj, S, r = dz.shape
    tk = _tile(S, ROW_TILE)
    hn_spec = pl.BlockSpec((tk, D_MODEL), lambda s, j, k: (k, 0))
    dz_spec = pl.BlockSpec((None, tk, r), lambda s, j, k: (s, k, 0))
    if transposed:
        return _matmul(
            name, dz, hn, dn=TN, grid=(nj, 1, S // tk), a_spec=dz_spec, b_spec=hn_spec,
            o_spec=pl.BlockSpec((None, r, D_MODEL), lambda s, j, k: (s, 0, 0)),
            out_shape=_sds((nj, r, D_MODEL), BF16), acc_shape=(r, D_MODEL))
    return _matmul(
        name, hn, dz, dn=TN, grid=(nj, 1, S // tk), a_spec=hn_spec, b_spec=dz_spec,
        o_spec=pl.BlockSpec((None, D_MODEL, r), lambda s, j, k: (s, 0, 0)),
        out_shape=_sds((nj, D_MODEL, r), BF16), acc_shape=(D_MODEL, r))


def _dgrad_out(name, dh, w, l, out_dtype, dep=None):
    S, K = dh.shape[0], w.shape[1]
    tm = _tile(S, ROW_TILE)
    return _matmul(
        name, dh, w, dn=NT, grid=(S // tm, 1, 1),
        a_spec=pl.BlockSpec((tm, D_MODEL), lambda i, j, k: (i, 0)),
        b_spec=pl.BlockSpec((None, K, D_MODEL), lambda i, j, k: (l, 0, 0)),
        o_spec=pl.BlockSpec((tm, K), lambda i, j, k: (i, 0)),
        out_shape=_sds((S, K), out_dtype), dep=dep)


def _wgrad_rows(name, a, b):
    S, K = a.shape
    tk = _tile(S, ROW_TILE)
    return _matmul(
        name, a, b, dn=TN, grid=(1, 1, S // tk),
        a_spec=pl.BlockSpec((tk, K), lambda i, j, k: (k, 0)),
        b_spec=pl.BlockSpec((tk, D_MODEL), lambda i, j, k: (k, 0)),
        o_spec=pl.BlockSpec((K, D_MODEL), lambda i, j, k: (0, 0)),
        out_shape=_sds((K, D_MODEL), BF16), acc_shape=(K, D_MODEL))


def _dgrad_rows(name, dz, wt, l, norm):
    S, N = dz.shape
    tm = _tile(S, ROW_TILE)
    return _matmul(
        name, dz, wt, dn=NN, grid=(S // tm, 1, 1),
        a_spec=pl.BlockSpec((tm, N), lambda i, j, k: (i, 0)),
        b_spec=pl.BlockSpec((None, N, D_MODEL), lambda i, j, k: (l, 0, 0)),
        extra=norm, **_norm_bwd_io(tm, S))


def _rmsnorm_fwd(name, h, g):
    S = h.shape[0]
    tm = _tile(S, ROW_TILE)

    def body(h_ref, g_ref, o_ref):
        x = h_ref[...]
        o_ref[...] = (x * _rms(x) * g_ref[...]).astype(o_ref.dtype)

    row = pl.BlockSpec((tm, D_MODEL), lambda i: (i, 0))
    return pl.pallas_call(
        body, name=name, grid=(S // tm,), in_specs=[row, pl.BlockSpec((1, D_MODEL), lambda i: (0, 0))],
        out_specs=row, out_shape=_sds((S, D_MODEL), BF16), compiler_params=_params(("parallel",)),
    )(h, g)


def _loss_head(name, h, g, target):
    S = h.shape[0]
    tm = _tile(S, ROW_TILE)

    def body(h_ref, g_ref, t_ref, dh_ref, dg_ref, ls_ref):
        x = h_ref[...]
        r = lax.rsqrt(jnp.mean(x * x, axis=-1, keepdims=True) + EPS)
        xh = x * r
        diff = xh * g_ref[...] - t_ref[...]
        dyf = diff * (1.0 / D_MODEL)
        dyg = dyf * g_ref[...]
        dh_ref[...] = r * (dyg - xh * jnp.mean(dyg * xh, axis=-1, keepdims=True))
        part = jnp.sum(dyf * xh, axis=0, keepdims=True)
        lpart = jnp.sum(diff * diff, axis=0, keepdims=True) * (0.5 / D_MODEL)

        @pl.when(pl.program_id(0) == 0)
        def _():
            dg_ref[...] = part
            ls_ref[...] = lpart

        @pl.when(pl.program_id(0) > 0)
        def _():
            dg_ref[...] += part
            ls_ref[...] += lpart

    row = pl.BlockSpec((tm, D_MODEL), lambda i: (i, 0))
    vec = pl.BlockSpec((1, D_MODEL), lambda i: (0, 0))
    return pl.pallas_call(
        body, name=name, grid=(S // tm,), in_specs=[row, vec, row], out_specs=[row, vec, vec],
        out_shape=[_sds((S, D_MODEL), F32), _sds((1, D_MODEL), F32), _sds((1, D_MODEL), F32)],
        compiler_params=_params(("arbitrary",)),
    )(h, g, target)


ATTN_SCALE = HEAD_DIM ** -0.5
ALIBI_SLOPES = [2.0 ** (-8.0 * (h + 1) / N_Q_HEADS) for h in range(N_Q_HEADS)]
K_COL = N_Q_HEADS * HEAD_DIM
KV_COLS = N_KV_HEADS * HEAD_DIM
V_COL = K_COL + KV_COLS


def _attn_masks(n):
    qi = lax.broadcasted_iota(jnp.int32, (ATTN_BLOCK, ATTN_BLOCK), 0)
    ki = lax.broadcasted_iota(jnp.int32, (ATTN_BLOCK, ATTN_BLOCK), 1)
    dist_c = (qi - ki).astype(F32)
    return dist_c + float(ATTN_BLOCK), dist_c, (ki > qi) & (n > 0), qi >= ki


def _attn_probs(raw_p, raw_c, sink, slope, masks):
    dist_p, dist_c, valid_p, valid_c = masks
    sp = jnp.where(valid_p, raw_p * ATTN_SCALE - slope * dist_p, NEG_BIG)
    sc = jnp.where(valid_c, raw_c * ATTN_SCALE - slope * dist_c, NEG_BIG)
    m = jnp.maximum(jnp.maximum(jnp.max(sp, axis=-1, keepdims=True), jnp.max(sc, axis=-1, keepdims=True)), sink)
    ep, ec, es = jnp.exp(sp - m), jnp.exp(sc - m), jnp.exp(sink - m)
    inv = 1.0 / (jnp.sum(ep, axis=-1, keepdims=True) + jnp.sum(ec, axis=-1, keepdims=True) + es)
    return ep * inv, ec * inv, es * inv


def _group_rows(ref, m):
    return jnp.concatenate([ref[:, HEAD_DIM * (Q_PER_KV * m + g):HEAD_DIM * (Q_PER_KV * m + g + 1)]
                            for g in range(Q_PER_KV)], axis=0)


def _head_rows(x, g):
    return x[ATTN_BLOCK * g:ATTN_BLOCK * (g + 1)]


def _attn_specs(nblk):
    last = nblk - 1
    kcol, vcol = K_COL // KV_COLS, V_COL // KV_COLS
    return [
        pl.BlockSpec((ATTN_BLOCK, K_COL), lambda n: (jnp.minimum(n, last), 0)),
        pl.BlockSpec((ATTN_BLOCK, KV_COLS), lambda n: (jnp.minimum(n, last), kcol)),
        pl.BlockSpec((ATTN_BLOCK, KV_COLS), lambda n: (jnp.maximum(jnp.minimum(n, last) - 1, 0), kcol)),
        pl.BlockSpec((ATTN_BLOCK, KV_COLS), lambda n: (jnp.minimum(n, last), vcol)),
        pl.BlockSpec((ATTN_BLOCK, KV_COLS), lambda n: (jnp.maximum(jnp.minimum(n, last) - 1, 0), vcol)),
    ]


P_COLS = 2 * ATTN_BLOCK


def _attn_fwd(name, proj, sinks):
    S = proj.shape[0]
    nblk = S // ATTN_BLOCK

    def body(q_ref, kc_ref, kp_ref, vc_ref, vp_ref, sk_ref, o_ref, p_ref, ps_ref):
        masks = _attn_masks(pl.program_id(0))
        lane = lax.broadcasted_iota(jnp.int32, (ATTN_BLOCK, 128), 1)
        sink_p = jnp.zeros((ATTN_BLOCK, 128), F32)
        for m in range(N_KV_HEADS):
            ks = slice(HEAD_DIM * m, HEAD_DIM * (m + 1))
            kp, kc, vp, vc = kp_ref[:, ks], kc_ref[:, ks], vp_ref[:, ks], vc_ref[:, ks]
            q4 = _group_rows(q_ref, m)
            raw_p, raw_c = _bdot(q4, kp, NT), _bdot(q4, kc, NT)
            pps, pcs = [], []
            for g in range(Q_PER_KV):
                hh = Q_PER_KV * m + g
                pp, pc, ps = _attn_probs(_head_rows(raw_p, g), _head_rows(raw_c, g), sk_ref[0, hh], ALIBI_SLOPES[hh], masks)
                pps.append(pp.astype(BF16))
                pcs.append(pc.astype(BF16))
                p_ref[:, P_COLS * hh:P_COLS * hh + ATTN_BLOCK] = pps[g]
                p_ref[:, P_COLS * hh + ATTN_BLOCK:P_COLS * (hh + 1)] = pcs[g]
                sink_p = jnp.where(lane == hh, ps, sink_p)
            o4 = _bdot(jnp.concatenate(pps, axis=0), vp, NN) + _bdot(jnp.concatenate(pcs, axis=0), vc, NN)
            for g in range(Q_PER_KV):
                hh = Q_PER_KV * m + g
                o_ref[:, HEAD_DIM * hh:HEAD_DIM * (hh + 1)] = _head_rows(o4, g).astype(o_ref.dtype)
        ps_ref[...] = sink_p

    row = lambda cols: pl.BlockSpec((ATTN_BLOCK, cols), lambda n: (n, 0))
    return pl.pallas_call(
        body, name=name, grid=(nblk,),
        in_specs=_attn_specs(nblk) + [pl.BlockSpec(memory_space=pltpu.SMEM)],
        out_specs=[row(K_COL), row(N_Q_HEADS * P_COLS), row(128)],
        out_shape=[_sds((S, K_COL), BF16), _sds((S, N_Q_HEADS * P_COLS), BF16), _sds((S, 128), F32)],
        compiler_params=_params(("parallel",)),
    )(proj, proj, proj, proj, proj, sinks)


def _attn_bwd(name, proj, probs, sink_probs, do):
    S = proj.shape[0]
    nblk = S // ATTN_BLOCK

    def body(q_ref, kc_ref, kp_ref, vc_ref, vp_ref, do_ref, p_ref, ps_ref, dz_ref, ds_ref, carry, cur, padd):
        n = pl.program_id(0)

        @pl.when(n == 0)
        def _():
            carry[...] = jnp.zeros_like(carry)
            ds_ref[...] = jnp.zeros_like(ds_ref)

        @pl.when(n < nblk)
        def _():
            lane = lax.broadcasted_iota(jnp.int32, (ATTN_BLOCK, 128), 1)
            sink_p = ps_ref[...]
            dsv = jnp.zeros((1, 128), F32)
            for m in range(N_KV_HEADS):
                ks = slice(HEAD_DIM * m, HEAD_DIM * (m + 1))
                kp, kc, vp, vc = kp_ref[:, ks], kc_ref[:, ks], vp_ref[:, ks], vc_ref[:, ks]
                q4, do4 = _group_rows(q_ref, m), _group_rows(do_ref, m)
                dpp4, dpc4 = _bdot(do4, vp, NT), _bdot(do4, vc, NT)
                pps, pcs, dsps, dscs = [], [], [], []
                for g in range(Q_PER_KV):
                    hh = Q_PER_KV * m + g
                    pps.append(p_ref[:, P_COLS * hh:P_COLS * hh + ATTN_BLOCK])
                    pcs.append(p_ref[:, P_COLS * hh + ATTN_BLOCK:P_COLS * (hh + 1)])
                    pp, pc = pps[g].astype(F32), pcs[g].astype(F32)
                    dpp, dpc = _head_rows(dpp4, g), _head_rows(dpc4, g)
                    delta = jnp.sum(pp * dpp, axis=-1, keepdims=True) + jnp.sum(pc * dpc, axis=-1, keepdims=True)
                    dsv = dsv - jnp.sum(jnp.where(lane == hh, sink_p, 0.0) * delta, axis=0, keepdims=True)
                    dsps.append((pp * (dpp - delta)).astype(BF16))
                    dscs.append((pc * (dpc - delta)).astype(BF16))
                pp4, pc4 = jnp.concatenate(pps, axis=0), jnp.concatenate(pcs, axis=0)
                dsp4, dsc4 = jnp.concatenate(dsps, axis=0), jnp.concatenate(dscs, axis=0)
                dq4 = (_bdot(dsp4, kp, NN) + _bdot(dsc4, kc, NN)) * ATTN_SCALE
                for g in range(Q_PER_KV):
                    hh = Q_PER_KV * m + g
                    cur[:, HEAD_DIM * hh:HEAD_DIM * (hh + 1)] = _head_rows(dq4, g)
                cur[:, K_COL + HEAD_DIM * m:K_COL + HEAD_DIM * (m + 1)] = _bdot(dsc4, q4, TN) * ATTN_SCALE
                cur[:, V_COL + HEAD_DIM * m:V_COL + HEAD_DIM * (m + 1)] = _bdot(pc4, do4, TN)
                padd[:, ks] = _bdot(dsp4, q4, TN) * ATTN_SCALE
                padd[:, KV_COLS + HEAD_DIM * m:KV_COLS + HEAD_DIM * (m + 1)] = _bdot(pp4, do4, TN)
            ds_ref[...] += dsv
            dz_ref[:, :K_COL] = carry[:, :K_COL].astype(dz_ref.dtype)
            dz_ref[:, K_COL:] = (carry[:, K_COL:] + padd[...]).astype(dz_ref.dtype)
            carry[...] = cur[...]

        @pl.when(n == nblk)
        def _():
            dz_ref[...] = carry[...].astype(dz_ref.dtype)

    return pl.pallas_call(
        body, name=name, grid=(nblk + 1,),
        in_specs=_attn_specs(nblk) + [
            pl.BlockSpec((ATTN_BLOCK, cols), lambda n: (jnp.minimum(n, nblk - 1), 0))
            for cols in (K_COL, N_Q_HEADS * P_COLS, 128)],
        out_specs=[pl.BlockSpec((ATTN_BLOCK, ATTN_IN), lambda n: (jnp.maximum(n - 1, 0), 0)),
                   pl.BlockSpec((1, 128), lambda n: (0, 0))],
        out_shape=[_sds((S, ATTN_IN), BF16), _sds((1, 128), F32)],
        scratch_shapes=[pltpu.VMEM((ATTN_BLOCK, ATTN_IN), F32), pltpu.VMEM((ATTN_BLOCK, ATTN_IN), F32),
                        pltpu.VMEM((ATTN_BLOCK, 2 * KV_COLS), F32)],
        compiler_params=_params(("arbitrary",)),
    )(proj, proj, proj, proj, proj, do, probs, sink_probs)


def _hg_consts():
    C = HG_CHUNK
    tri = np.tril(np.ones((C, C)))
    t = np.arange(C)
    rows, masks = [tri], []
    for lvl in range(HG_LEVELS):
        n = C >> (lvl + 1)
        sel = np.zeros((C, C))
        sel[t, (t // (2 * n)) * (2 * n) + n - 1] = 1.0
        rows.append(sel @ tri)
        tt, ss = t[:, None], t[None, :]
        masks.append((tt // (2 * n) == ss // (2 * n)) & ((tt // n) % 2 == 1) & ((ss // n) % 2 == 0))
    masks.append(np.eye(C, dtype=bool))
    stk = np.concatenate(rows, axis=0)
    return jnp.asarray(stk, BF16), jnp.asarray(np.stack(masks), F32)


def _sigmoid(x):
    return 1.0 / (1.0 + jnp.exp(-x))


def _split(x, parts):
    out, rest = [], x
    for _ in range(parts):
        out.append(rest.astype(BF16))
        rest = rest - out[-1].astype(F32)
    return out


def _dot01(m01, x, dn, parts=3):
    return sum(lax.dot_general(m01, p, dn, preferred_element_type=F32) for p in _split(x, parts))


def _ref_rows(b, n):
    C = b.shape[1]
    if 2 * n >= 8:
        b3 = b.reshape(HG_CHUNK // (2 * n), 2 * n, C)
        return jnp.broadcast_to(b3[:, n - 1:n, :], b3.shape).reshape(HG_CHUNK, C)
    pos = lax.broadcasted_iota(jnp.int32, b.shape, 0) % (2 * n)
    out = b
    for p in range(2 * n):
        if p != n - 1:
            out = jnp.where(pos == p, pltpu.roll(b, (p - (n - 1)) % HG_CHUNK, 0), out)
    return out


HG_STEP_CHUNKS = 4


def _chunk_rows(ci):
    return pl.ds(pl.multiple_of(ci * HG_CHUNK, HG_CHUNK), HG_CHUNK)


def _hg_common(z_ref, rows, lb_ref, stk_ref):
    qr, fr = z_ref[0, rows, :], z_ref[1, rows, :]
    lb = lb_ref[...]
    sq, sg, sgn = _sigmoid(qr), _sigmoid(fr), _sigmoid(-fr)
    ft = lb + (1.0 - lb) * sg
    b = _dot01(stk_ref[0:HG_CHUNK, :], jnp.log(ft), NN)
    ws = [jnp.exp(-jnp.abs(b - _ref_rows(b, HG_CHUNK >> (l + 1)))) for l in range(HG_LEVELS)]
    blast = b[HG_CHUNK - 1:HG_CHUNK]
    return dict(qr=qr, fr=fr, lb=lb, sq=sq, sg=sg, sgn=sgn, ft=ft, q=qr * sq, kk=(1.0 - lb) * sgn, b=b,
                ws=ws, eb=jnp.exp(b), ed=jnp.exp(blast - b), elast=jnp.exp(blast))


def _hg_factors(qh, kh, ws, sl):
    return ([(qh * ws[l][:, sl]).astype(BF16) for l in range(HG_LEVELS)],
            [(kh * ws[l][:, sl]).astype(BF16) for l in range(HG_LEVELS)])


def _hg_intra(qh, kh, ws, msk_ref, sl):
    qls, kls = _hg_factors(qh, kh, ws, sl)
    a = msk_ref[HG_LEVELS] * _bdot(qh, kh, NT)
    for l in range(HG_LEVELS):
        a = a + msk_ref[l] * _bdot(qls[l], kls[l], NT)
    return a


def _hg_fwd(name, z, lb, ng):
    S = z.shape[2]
    nc = S // HG_CHUNK
    per = min(HG_STEP_CHUNKS, nc)
    stk, msk = _hg_consts()

    def body(z_ref, lb_ref, ng_ref, stk_ref, msk_ref, og_ref, st_ref, a_ref, o_ref, state):
        @pl.when(pl.program_id(1) == 0)
        def _():
            state[...] = jnp.zeros_like(state)

        def chunk(ci, _):
            rows = _chunk_rows(ci)
            cm = _hg_common(z_ref, rows, lb_ref, stk_ref)
            v, gt = z_ref[2, rows, :], z_ref[3, rows, :]
            kd = cm["kk"] * cm["ed"]
            for hh in range(4):
                sl = slice(HG_K * hh, HG_K * (hh + 1))
                st = state[hh]
                st_ref[ci, hh] = st
                qh, kh, vh = cm["q"][:, sl], cm["kk"][:, sl], v[:, sl]
                a = _hg_intra(qh, kh, cm["ws"], msk_ref, sl).astype(BF16)
                a_ref[ci, hh] = a
                o = _bdot(a, vh, NN) + _bdot(qh * cm["eb"][:, sl], st, NT)
                o_ref[rows, sl] = o
                state[hh] = cm["elast"][:, sl] * st + _bdot(vh, kd[:, sl], TN)
                gh = gt[:, sl]
                og_ref[rows, sl] = (o * _rms(o) * ng_ref[...] * (gh * _sigmoid(gh))).astype(og_ref.dtype)
            return 0

        lax.fori_loop(0, per, chunk, 0, unroll=True)

    return pl.pallas_call(
        body, name=name, grid=(2, nc // per),
        in_specs=[pl.BlockSpec((4, None, per * HG_CHUNK, HG_SLOT), lambda g, c: (0, g, c, 0)),
                  pl.BlockSpec((1, HG_SLOT), lambda g, c: (0, g)),
                  pl.BlockSpec((1, HG_K), lambda g, c: (0, 0)),
                  pl.BlockSpec(stk.shape, lambda g, c: (0, 0)),
                  pl.BlockSpec(msk.shape, lambda g, c: (0, 0, 0))],
        out_specs=[pl.BlockSpec((per * HG_CHUNK, HG_SLOT), lambda g, c: (c, g)),
                   pl.BlockSpec((per, 4, HG_K, HG_K), lambda g, c: (c, g, 0, 0)),
                   pl.BlockSpec((per, 4, HG_CHUNK, HG_CHUNK), lambda g, c: (c, g, 0, 0)),
                   pl.BlockSpec((per * HG_CHUNK, HG_SLOT), lambda g, c: (c, g))],
        out_shape=[_sds((S, D_MODEL), BF16), _sds((nc, HG_HEADS, HG_K, HG_K), F32),
                   _sds((nc, HG_HEADS, HG_CHUNK, HG_CHUNK), BF16), _sds((S, D_MODEL), F32)],
        scratch_shapes=[pltpu.VMEM((4, HG_K, HG_K), F32)],
        compiler_params=_params(("parallel", "arbitrary")),
    )(z, lb, ng, stk, msk)


def _hg_bwd(name, z, lb, ng, states, intra, o_pre, dog):
    S = z.shape[2]
    nc = S // HG_CHUNK
    per = min(HG_STEP_CHUNKS, nc)
    stk, msk = _hg_consts()

    def body(z_ref, lb_ref, ng_ref, stk_ref, msk_ref, st_ref, a_ref, o_ref, dog_ref, dz_ref, dlb_ref, dng_ref, dstate):
        @pl.when(pl.program_id(1) == 0)
        def _():
            dstate[...] = jnp.zeros_like(dstate)
            dlb_ref[...] = jnp.zeros_like(dlb_ref)
            dng_ref[...] = jnp.zeros_like(dng_ref)

        def chunk(k, _):
            ci = per - 1 - k
            rows = _chunk_rows(ci)
            cm = _hg_common(z_ref, rows, lb_ref, stk_ref)
            v, gt = z_ref[2, rows, :], z_ref[3, rows, :]
            ng = ng_ref[...]
            kd = cm["kk"] * cm["ed"]
            row = lax.broadcasted_iota(jnp.int32, (HG_CHUNK, 1), 0)
            dng = jnp.zeros((1, HG_K), F32)
            dq_h, dkk_h, db_h, dv_h, dgt_h = [], [], [], [], []
            dr_h = [[] for _ in range(HG_LEVELS)]
            for hh in range(4):
                sl = slice(HG_K * hh, HG_K * (hh + 1))
                st, dst = st_ref[ci, hh], dstate[hh]
                qh, kh, vh, ebh, edh, kdh = cm["q"][:, sl], cm["kk"][:, sl], v[:, sl], cm["eb"][:, sl], cm["ed"][:, sl], kd[:, sl]
                elh = cm["elast"][:, sl]
                qls, kls = _hg_factors(qh, kh, cm["ws"], sl)
                a, o = a_ref[ci, hh], o_ref[rows, sl]
                qe = qh * ebh
                r = _rms(o)
                xh = o * r
                gh = gt[:, sl]
                sgg = _sigmoid(gh)
                dog = dog_ref[rows, sl].astype(F32)
                dy = dog * (gh * sgg)
                dgt_h.append(dog * (xh * ng) * (sgg * (1.0 + gh * (1.0 - sgg))))
                dng = dng + jnp.sum(dy * xh, axis=0, keepdims=True)
                dyg = dy * ng
                do = r * (dyg - xh * jnp.mean(dyg * xh, axis=-1, keepdims=True))
                da = _bdot(do, vh, NT)
                dv_h.append(_bdot(a, do, TN) + _bdot(kdh, dst, NT))
                dkd = _bdot(vh, dst, NN)
                delast = jnp.sum(st * dst, axis=0, keepdims=True)
                dqe = _bdot(do, st, NN)
                dstate[hh] = elh * dst + _bdot(do, qe, TN)
                gk = dkd * kdh
                dblast = jnp.sum(gk, axis=0, keepdims=True) + delast * elh
                db = dqe * qe - gk + jnp.where(row == HG_CHUNK - 1, dblast, 0.0)
                dp = (msk_ref[HG_LEVELS] * da).astype(BF16)
                dq = dqe * ebh + _bdot(dp, kh, NN)
                dkk = dkd * edh + _bdot(dp, qh, TN)
                for l in range(HG_LEVELS):
                    dp = (msk_ref[l] * da).astype(BF16)
                    dql, dkl = _bdot(dp, kls[l], NN), _bdot(dp, qls[l], TN)
                    w = cm["ws"][l][:, sl]
                    dq = dq + dql * w
                    dkk = dkk + dkl * w
                    half = jnp.where(((row >> (HG_LEVELS - 1 - l)) & 1) == 1, 1.0, -1.0)
                    dd = half * w * (dql * qh + dkl * kh)
                    db = db + dd
                    dr_h[l].append(-dd)
                dq_h.append(dq)
                dkk_h.append(dkk)
                db_h.append(db)
            cat = lambda xs: jnp.concatenate(xs, axis=1)
            cot = jnp.concatenate([cat(db_h)] + [cat(dr_h[l]) for l in range(HG_LEVELS)], axis=0)
            dlf = _dot01(stk_ref[...], cot, TN, parts=2)
            dq, dkk = cat(dq_h), cat(dkk_h)
            dft = dlf / cm["ft"]
            one_lb = 1.0 - cm["lb"]
            dz_ref[0, rows, :] = (dq * (cm["sq"] * (1.0 + cm["qr"] * (1.0 - cm["sq"])))).astype(dz_ref.dtype)
            dz_ref[1, rows, :] = ((dft - dkk) * one_lb * cm["sg"] * cm["sgn"]).astype(dz_ref.dtype)
            dz_ref[2, rows, :] = cat(dv_h).astype(dz_ref.dtype)
            dz_ref[3, rows, :] = cat(dgt_h).astype(dz_ref.dtype)
            dlb_ref[...] += jnp.sum((dft - dkk) * cm["sgn"], axis=0, keepdims=True)
            dng_ref[...] += dng
            return 0

        lax.fori_loop(0, per, chunk, 0, unroll=True)

    rev = lambda c: nc // per - 1 - c
    rows_blk = pl.BlockSpec((per * HG_CHUNK, HG_SLOT), lambda g, c: (rev(c), g))
    return pl.pallas_call(
        body, name=name, grid=(2, nc // per),
        in_specs=[pl.BlockSpec((4, None, per * HG_CHUNK, HG_SLOT), lambda g, c: (0, g, rev(c), 0)),
                  pl.BlockSpec((1, HG_SLOT), lambda g, c: (0, g)),
                  pl.BlockSpec((1, HG_K), lambda g, c: (0, 0)),
                  pl.BlockSpec(stk.shape, lambda g, c: (0, 0)),
                  pl.BlockSpec(msk.shape, lambda g, c: (0, 0, 0)),
                  pl.BlockSpec((per, 4, HG_K, HG_K), lambda g, c: (rev(c), g, 0, 0)),
                  pl.BlockSpec((per, 4, HG_CHUNK, HG_CHUNK), lambda g, c: (rev(c), g, 0, 0)),
                  rows_blk, rows_blk],
        out_specs=[pl.BlockSpec((4, None, per * HG_CHUNK, HG_SLOT), lambda g, c: (0, g, rev(c), 0)),
                   pl.BlockSpec((1, HG_SLOT), lambda g, c: (0, g)),
                   pl.BlockSpec((None, 1, HG_K), lambda g, c: (g, 0, 0))],
        out_shape=[_sds(z.shape, BF16), _sds((1, 2 * HG_SLOT), F32), _sds((2, 1, HG_K), F32)],
        scratch_shapes=[pltpu.VMEM((4, HG_K, HG_K), F32)],
        compiler_params=_params(("parallel", "arbitrary")),
    )(z, lb, ng, stk, msk, states, intra, o_pre, dog)


def _lb_fwd(name, logits):
    def body(l_ref, o_ref):
        x = l_ref[...]
        e = jnp.exp(x - jnp.max(x, axis=0, keepdims=True))
        s = e / jnp.sum(e, axis=0, keepdims=True)
        o_ref[0:1, :] = s[1:2]
        o_ref[1:2, :] = s[1:2] + s[2:3] + s[3:4]

    return pl.pallas_call(body, name=name, out_shape=_sds((2, logits.shape[1]), F32))(logits)


def _lb_bwd(name, logits, dlb):
    def body(l_ref, d_ref, o_ref):
        x = l_ref[...]
        e = jnp.exp(x - jnp.max(x, axis=0, keepdims=True))
        s = e / jnp.sum(e, axis=0, keepdims=True)
        d1, d3 = d_ref[0:1, :], d_ref[1:2, :]
        ds = [jnp.zeros_like(d1), d1 + d3, d3, d3]
        dot = sum(ds[r] * s[r:r + 1] for r in range(1, DEPTH))
        for r in range(DEPTH):
            o_ref[r:r + 1, :] = s[r:r + 1] * (ds[r] - dot)

    return pl.pallas_call(body, name=name, out_shape=_sds(logits.shape, F32))(logits, dlb)


SUB = 8


def _rows_down(x, prev, k):
    row = lax.broadcasted_iota(jnp.int32, x.shape, 0)
    return jnp.where(row >= k, pltpu.roll(x, k, 0), pltpu.roll(prev, k, 0))


def _rows_up(x, nxt, k):
    row = lax.broadcasted_iota(jnp.int32, x.shape, 0)
    return jnp.where(row < SUB - k, pltpu.roll(x, SUB - k, 0), pltpu.roll(nxt, SUB - k, 0))


def _conv_block(w_ref, b_ref, p, x, prev):
    return (b_ref[p] + w_ref[p, 0:1, :] * _rows_down(x, prev, 2) + w_ref[p, 1:2, :] * _rows_down(x, prev, 1)
            + w_ref[p, 2:3, :] * x)


def _convgate_fwd(name, u, cw, cb):
    S = u.shape[2]
    tm = _tile(S, ROW_TILE)

    def body(u_ref, w_ref, b_ref, a_ref, c_ref, halo):
        @pl.when(pl.program_id(1) == 0)
        def _():
            halo[...] = jnp.zeros_like(halo)

        def step(r, prev):
            pg, pv = prev
            out, cgs, cvs = [], [], []
            rows = pl.ds(pl.multiple_of(r * 2 * SUB, 2 * SUB), 2 * SUB)
            ug16, uv16 = u_ref[0, rows, :].astype(F32), u_ref[1, rows, :].astype(F32)
            for s in range(2):
                xg, xv = ug16[s * SUB:(s + 1) * SUB], uv16[s * SUB:(s + 1) * SUB]
                cgs.append(_conv_block(w_ref, b_ref, 0, xg, pg))
                cvs.append(_conv_block(w_ref, b_ref, 1, xv, pv))
                out.append(cgs[s] * _sigmoid(cgs[s]) * cvs[s])
                pg, pv = xg, xv
            a_ref[rows, :] = jnp.concatenate(out, axis=0).astype(a_ref.dtype)
            c_ref[0, rows, :] = jnp.concatenate(cgs, axis=0).astype(c_ref.dtype)
            c_ref[1, rows, :] = jnp.concatenate(cvs, axis=0).astype(c_ref.dtype)
            return pg, pv

        pg, pv = lax.fori_loop(0, tm // (2 * SUB), step, (halo[0], halo[1]), unroll=2)
        halo[0] = pg
        halo[1] = pv

    pair = pl.BlockSpec((2, None, tm, FF_SLOT), lambda j, t: (0, j, t, 0))
    return pl.pallas_call(
        body, name=name, grid=(4, S // tm),
        in_specs=[pair, pl.BlockSpec((2, None, 3, FF_SLOT), lambda j, t: (0, j, 0, 0)),
                  pl.BlockSpec((2, None, 1, FF_SLOT), lambda j, t: (0, j, 0, 0))],
        out_specs=[pl.BlockSpec((None, tm, FF_SLOT), lambda j, t: (j, t, 0)), pair],
        out_shape=[_sds((4, S, FF_SLOT), BF16), _sds(u.shape, BF16)],
        scratch_shapes=[pltpu.VMEM((2, SUB, FF_SLOT), F32)],
        compiler_params=_params(("parallel", "arbitrary")),
    )(u, cw, cb)


def _convgate_bwd(name, u, convs, cw, da):
    S = u.shape[2]
    tm = _tile(S, ROW_TILE)
    nt = S // tm

    def body(u_ref, c_ref, w_ref, da_ref, du_out, dw_ref, db_ref, after, first, acc, du_ref):
        @pl.when(pl.program_id(1) == 0)
        def _():
            after[...] = jnp.zeros_like(after)
            acc[...] = jnp.zeros_like(acc)

        def finish(p, x, d, nxt, rows):
            taps = (_rows_up(d, nxt, 2), _rows_up(d, nxt, 1), d)
            du_ref[p, rows, :] = w_ref[p, 0:1, :] * taps[0] + w_ref[p, 1:2, :] * taps[1] + w_ref[p, 2:3, :] * d
            for j in range(3):
                acc[p, j] += taps[j] * x
            acc[p, 3] += d

        def step(r, carry):
            xg_last, xv_last, dg_last, dv_last = carry
            rows16 = pl.ds(pl.multiple_of(r * 2 * SUB, 2 * SUB), 2 * SUB)
            dav = da_ref[rows16, :].astype(F32)
            cg16, cv16 = c_ref[0, rows16, :].astype(F32), c_ref[1, rows16, :].astype(F32)
            ug16, uv16 = u_ref[0, rows16, :].astype(F32), u_ref[1, rows16, :].astype(F32)
            for s in range(2):
                at = r * 2 * SUB + s * SUB
                part = slice(s * SUB, (s + 1) * SUB)
                cg, cv, dab = cg16[part], cv16[part], dav[part]
                sg = _sigmoid(cg)
                dg = dab * cv * (sg * (1.0 + cg * (1.0 - sg)))
                dv = dab * cg * sg
                before = pl.ds(pl.multiple_of(at - SUB, SUB), SUB)
                if s == 0:
                    @pl.when(r == 0)
                    def _():
                        first[0] = dg
                        first[1] = dv

                    @pl.when(r > 0)
                    def _():
                        finish(0, xg_last, dg_last, dg, before)
                        finish(1, xv_last, dv_last, dv, before)
                else:
                    finish(0, xg_last, dg_last, dg, before)
                    finish(1, xv_last, dv_last, dv, before)
                xg_last, xv_last, dg_last, dv_last = ug16[part], uv16[part], dg, dv
            return xg_last, xv_last, dg_last, dv_last

        zero = jnp.zeros((SUB, FF_SLOT), F32)
        xg_last, xv_last, dg_last, dv_last = lax.fori_loop(0, tm // (2 * SUB), step, (zero, zero, zero, zero))
        finish(0, xg_last, dg_last, after[0], slice(tm - SUB, tm))
        finish(1, xv_last, dv_last, after[1], slice(tm - SUB, tm))
        du_out[...] = du_ref[...].astype(du_out.dtype)
        after[...] = first[...]
        for p in range(2):
            for j in range(3):
                dw_ref[p, j:j + 1, :] = jnp.sum(acc[p, j], axis=0, keepdims=True)
            db_ref[p] = jnp.sum(acc[p, 3], axis=0, keepdims=True)

    rev = lambda t: nt - 1 - t
    pair = pl.BlockSpec((2, None, tm, FF_SLOT), lambda j, t: (0, j, rev(t), 0))
    taps = pl.BlockSpec((2, None, 3, FF_SLOT), lambda j, t: (0, j, 0, 0))
    bias = pl.BlockSpec((2, None, 1, FF_SLOT), lambda j, t: (0, j, 0, 0))
    return pl.pallas_call(
        body, name=name, grid=(4, nt),
        in_specs=[pair, pair, taps, pl.BlockSpec((None, tm, FF_SLOT), lambda j, t: (j, rev(t), 0))],
        out_specs=[pair, taps, bias],
        out_shape=[_sds(u.shape, BF16), _sds(cw.shape, F32), _sds((2, 4, 1, FF_SLOT), F32)],
        scratch_shapes=[pltpu.VMEM((2, SUB, FF_SLOT), F32), pltpu.VMEM((2, SUB, FF_SLOT), F32),
                        pltpu.VMEM((2, 4, SUB, FF_SLOT), F32), pltpu.VMEM((2, tm, FF_SLOT), F32)],
        compiler_params=_params(("parallel", "arbitrary")),
    )(u, convs, cw, da)


def _row_tile(R):
    for t in range(256, 15, -16):
        if R % t == 0:
            return t
    return R


def _adamw(name, gsrcs, w, m, v, dep=None):
    L = len(gsrcs)
    n, A, C = gsrcs[0].shape
    tr = _row_tile(A)
    deps = () if dep is None else (dep,)

    def body(*refs):
        g_refs = refs[:L]
        w_ref, m_ref, v_ref = refs[L:L + 3]
        go_ref, d_ref, mo_ref, vo_ref = refs[L + 3 + len(deps):]
        for k in range(L):
            @pl.when(pl.program_id(0) == k)
            def _(k=k):
                g = g_refs[k][0].astype(F32)
                for s in range(1, n):
                    g = g + g_refs[k][s].astype(F32)
                m2 = ADAM_B1 * m_ref[...] + (1.0 - ADAM_B1) * g
                v2 = ADAM_B2 * v_ref[...] + (1.0 - ADAM_B2) * (g * g)
                m_hat = m2 / (1.0 - ADAM_B1 ** ADAM_STEP)
                v_hat = v2 / (1.0 - ADAM_B2 ** ADAM_STEP)
                go_ref[...] = g
                d_ref[...] = -ADAM_LR * (m_hat / (jnp.sqrt(v_hat) + ADAM_EPS) + ADAM_WD * w_ref[...])
                mo_ref[...] = m2
                vo_ref[...] = v2

    g_specs = [pl.BlockSpec((n, tr, C), lambda l, i, k=k: (0, jnp.where(l == k, i, 0), 0)) for k in range(L)]
    blk = pl.BlockSpec((None, tr, C), lambda l, i: (l, i, 0))
    return pl.pallas_call(
        body, name=name, grid=(L, A // tr), in_specs=g_specs + [blk, blk, blk] + [_dep_spec(2)] * len(deps),
        out_specs=[blk] * 4, out_shape=[_sds((L, A, C), F32)] * 4, compiler_params=_params(("parallel", "parallel")),
    )(*gsrcs, w, m, v, *deps)


MESH = pl.DeviceIdType.MESH
HBM_SPEC = pl.BlockSpec(memory_space=pltpu.HBM)
N_PEERS = N_DEV - 1


def _mesh_place():
    x, y, c = lax.axis_index("x"), lax.axis_index("y"), lax.axis_index("c")
    peers = []
    for p in range(1, N_DEV):
        px = 1 - x if p & 4 else x
        py = 1 - y if p & 2 else y
        pc = 1 - c if p & 1 else c
        peers.append(((px, py, pc), 4 * px + 2 * py + pc))
    return 4 * x + 2 * y + c, peers


SEM_SPEC = pl.BlockSpec(memory_space=pltpu.SEMAPHORE)
ANY_SPEC = pl.BlockSpec(memory_space=pl.ANY)
EFFECT = pltpu.SideEffectType.DATAFLOW_SIDE_EFFECTING


def _scatters(scatter, k):
    return scatter if isinstance(scatter, bool) else scatter[k]


def _exchange_refs(scatter, src, land, send, recv, k, p, dev, idx, me):
    return pltpu.make_async_remote_copy(src_ref=src[k].at[idx] if _scatters(scatter, k) else src[k], dst_ref=land[k].at[me],
                                        send_sem=send.at[k * N_PEERS + p], recv_sem=recv.at[k * N_PEERS + p], device_id=dev,
                                        device_id_type=MESH)


def _exchange_start(name, srcs, scatter, gate):
    n = len(srcs)
    lands = [lax.empty(s.shape if _scatters(scatter, k) else (N_DEV,) + s.shape, s.dtype) for k, s in enumerate(srcs)]

    def body(*refs):
        src, land = refs[:n], refs[n:2 * n]
        send, recv, own = refs[2 * n + 1:2 * n + 4]
        token = refs[-1]
        me, peers = _mesh_place()
        for k in range(n):
            pltpu.make_async_copy(src[k].at[me] if _scatters(scatter, k) else src[k], land[k].at[me], own.at[k]).start()
            for p, (dev, idx) in enumerate(peers):
                _exchange_refs(scatter, src, land, send, recv, k, p, dev, idx, me).start()
        token[...] = jnp.zeros_like(token)

    hbm = lambda a: pltpu.HBM(a.shape, a.dtype)
    outs = pl.pallas_call(
        body, name=name,
        out_shape=(pltpu.SemaphoreType.DMA((n * N_PEERS,)), pltpu.SemaphoreType.DMA((n * N_PEERS,)),
                   pltpu.SemaphoreType.DMA((n,)), *[hbm(s) for s in srcs], *[hbm(s) for s in lands], _sds(DEP_SHAPE, F32)),
        in_specs=[HBM_SPEC] * (2 * n) + [ANY_SPEC],
        out_specs=(SEM_SPEC, SEM_SPEC, SEM_SPEC, *[HBM_SPEC] * (2 * n), pl.BlockSpec(memory_space=pltpu.VMEM)),
        input_output_aliases={j: 3 + j for j in range(2 * n)},
        compiler_params=pltpu.CompilerParams(has_side_effects=EFFECT),
    )(*[pltpu.with_memory_space_constraint(s, pltpu.HBM) for s in srcs],
      *[pltpu.with_memory_space_constraint(s, pltpu.HBM) for s in lands], gate)
    return outs[:3], None, list(outs[3:3 + n]), list(outs[3 + n:3 + 2 * n]), outs[-1]


def _exchange_wait(name, started, scatter, after):
    (send, recv, own), _, srcs, lands, _ = started
    n = len(srcs)

    def body(*refs):
        src, land = refs[:n], refs[n:2 * n]
        send, recv, own = refs[2 * n:2 * n + 3]
        me, peers = _mesh_place()
        for k in range(n):
            pltpu.make_async_copy(src[k].at[me] if _scatters(scatter, k) else src[k], land[k].at[me], own.at[k]).wait()
            for p, (dev, idx) in enumerate(peers):
                cp = pltpu.make_async_remote_copy(src_ref=src[k].at[idx] if _scatters(scatter, k) else src[k], dst_ref=land[k].at[idx],
                                                  send_sem=send.at[k * N_PEERS + p], recv_sem=recv.at[k * N_PEERS + p], device_id=dev,
                                                  device_id_type=MESH)
                cp.wait_send()
                cp.wait_recv()

    hbm = lambda a: pltpu.HBM(a.shape, a.dtype)
    outs = pl.pallas_call(
        body, name=name, out_shape=(*[hbm(s) for s in srcs], *[hbm(s) for s in lands]),
        in_specs=[HBM_SPEC] * (2 * n) + [SEM_SPEC, SEM_SPEC, SEM_SPEC, ANY_SPEC], out_specs=tuple([HBM_SPEC] * (2 * n)),
        input_output_aliases={j: j for j in range(2 * n)},
        compiler_params=pltpu.CompilerParams(has_side_effects=EFFECT),
    )(*srcs, *lands, send, recv, own, after)
    return list(outs[n:])


def _sum_devices(name, parts):
    def body(p_ref, o_ref):
        tot = p_ref[0]
        for j in range(1, N_DEV):
            tot = tot + p_ref[j]
        o_ref[...] = tot

    return pl.pallas_call(body, name=name, out_shape=_sds(parts.shape[1:], F32),
                          compiler_params=pltpu.CompilerParams(vmem_limit_bytes=VMEM_LIMIT))(parts)


def _rows(a, width=D_MODEL):
    flat = a.reshape(-1)
    return jnp.pad(flat, (0, (-flat.shape[0]) % width)).reshape(-1, width)


def _pack_rows(parts):
    blocks = []
    for p in parts:
        r = _rows(p)
        blocks.append(jnp.pad(r, ((0, (-r.shape[0]) % 8), (0, 0))))
    return jnp.concatenate(blocks, axis=0)


def _unpack_rows(rows, shapes):
    out, at = [], 0
    for s in shapes:
        size = int(np.prod(s))
        n = -(-size // D_MODEL)
        out.append(rows[at:at + n].reshape(-1)[:size].reshape(s))
        at += -(-n // 8) * 8
    return out


def kernel(x, norm_mix, norm_ffn, norm_final, attn_w_in, attn_w_out, attn_sinks, hgrn_w_in, hgrn_w_out, hgrn_norm, hgrn_lb_logits, ffn_w_up, ffn_conv_w, ffn_conv_b, ffn_w_down, loss_target, m_norm_mix, m_norm_ffn, m_norm_final, m_attn_w_in, m_attn_w_out, m_attn_sinks, m_hgrn_w_in, m_hgrn_w_out, m_hgrn_norm, m_hgrn_lb_logits, m_ffn_w_up, m_ffn_conv_w, m_ffn_conv_b, m_ffn_w_down, v_norm_mix, v_norm_ffn, v_norm_final, v_attn_w_in, v_attn_w_out, v_attn_sinks, v_hgrn_w_in, v_hgrn_w_out, v_hgrn_norm, v_hgrn_lb_logits, v_ffn_w_up, v_ffn_conv_w, v_ffn_conv_b, v_ffn_w_down):
    S = x.shape[1]
    n_attn, n_hgrn = attn_w_in.shape[0], hgrn_w_in.shape[0]

    wa_in_t, wa_out_b = attn_w_in.transpose(0, 2, 1).astype(BF16), attn_w_out.astype(BF16)
    wh_in_b, wh_out_b = hgrn_w_in.astype(BF16), hgrn_w_out.astype(BF16)
    wf_up_b, wf_down_b = ffn_w_up.transpose(0, 2, 1).astype(BF16), ffn_w_down.astype(BF16)
    conv_b = ffn_conv_b.reshape(DEPTH, 2, 4, 1, FF_SLOT)
    lb = _lb_fwd("lb_fwd", hgrn_lb_logits)

    def unit_shards(l, part):
        if part == "ffn":
            return [wf_up_b[l], wf_down_b[l], ffn_conv_w[l]]
        return [wa_in_t[l // 2], wa_out_b[l // 2]] if l % 2 == 0 else [wh_in_b[l // 2], wh_out_b[l // 2]]

    def unit_weights(l, part, w):
        if part == "ffn":
            return w[0][None], w[1].reshape(1, 4, FF_SLOT, D_MODEL), w[2].reshape(2, 4, 3, FF_SLOT)
        if l % 2 == 0:
            return w[0].reshape(1, ATTN_IN, D_MODEL), w[1].reshape(1, D_MODEL, D_MODEL)
        return w[0][None], w[1].reshape(1, D_MODEL, D_MODEL)

    units = [(l, part) for l in range(DEPTH) for part in ("mix", "ffn")]
    gathers = [_exchange_start("gather_start0", unit_shards(*units[0]), False, norm_final)]
    gathers.append(_exchange_start("gather_start1", unit_shards(*units[1]), False, gathers[0][4]))
    arrived = _exchange_wait("gather_wait0", gathers[0], False, gathers[1][4])
    weights, saved = {}, [dict() for _ in range(DEPTH)]
    h = x[0]
    hn = _rmsnorm_fwd("norm_mix_fwd0", h, norm_mix[0:1])
    for n, (l, part) in enumerate(units):
        i, sv = l // 2, saved[l]
        weights[l, part] = w = unit_weights(l, part, arrived)
        dep = None
        if n + 2 < len(units):
            gathers.append(_exchange_start(f"gather_start{n + 2}", unit_shards(*units[n + 2]), False, arrived[0]))
            dep = gathers[n + 2][4]
        if part == "mix":
            sv["h"], sv["hn"] = h, hn
            if l % 2 == 0:
                sv["proj"] = _proj_rows(f"attn_proj{i}", hn, w[0], 0, BF16, dep)
                sv["o"], *sv["kept"] = _attn_fwd(f"attn_fwd{i}", sv["proj"], attn_sinks[i:i + 1])
                h, hn = _out_proj(f"attn_out{i}", sv["o"], w[1], 0, h, norm_ffn[l:l + 1])
            else:
                sv["z"] = _proj_slots(f"hgrn_proj{i}", hn, w[0], 0, dep=dep).reshape(4, 2, S, HG_SLOT)
                sv["o"], *sv["kept"] = _hg_fwd(f"hgrn_fwd{i}", sv["z"], lb[i:i + 1], hgrn_norm[i:i + 1])
                h, hn = _out_proj(f"hgrn_out{i}", sv["o"], w[1], 0, h, norm_ffn[l:l + 1])
        else:
            sv["h2"], sv["hn2"] = h, hn
            sv["u"] = _proj_slots(f"ffn_up{l}", hn, w[0], 0, True, dep, BF16).reshape(2, 4, S, FF_SLOT)
            sv["a"], sv["convs"] = _convgate_fwd(f"ffn_gate{l}", sv["u"], w[2], conv_b[l])
            if l + 1 < DEPTH:
                h, hn = _down_proj(f"ffn_down{l}", sv["a"], w[1], 0, h, norm_mix[l + 1:l + 2])
            else:
                h = _down_proj(f"ffn_down{l}", sv["a"], w[1], 0, h)
        if n + 1 < len(units):
            arrived = _exchange_wait(f"gather_wait{n + 1}", gathers[n + 1], False, h)
    dh, d_norm_final, loss_rows = _loss_head("loss_head", h, norm_final[None], loss_target[0])

    d_conv_w, d_conv_b, d_norm_mix, d_norm_ffn = [None] * DEPTH, [None] * DEPTH, [None] * DEPTH, [None] * DEPTH
    d_sinks, d_lb, d_hgrn_norm = [None] * n_attn, [None] * n_hgrn, [None] * n_hgrn
    received, pending = {}, []
    for l, part in reversed(units):
        i, sv, w = l // 2, saved[l], weights[l, part]
        dep = pending[-1][1][4] if pending else None
        if part == "ffn":
            da = _dgrad_down(f"ffn_down_dgrad{l}", dh, w[1], 0, dep)
            g_down = _wgrad_down(f"ffn_down_wgrad{l}", sv["a"], dh).reshape(N_DEV, D_FF // N_DEV, D_MODEL)
            du, d_conv_w[l], d_conv_b[l] = _convgate_bwd(f"ffn_gate_bwd{l}", sv["u"], sv["convs"], w[2], da)
            du = du.reshape(N_DEV, S, FF_SLOT)
            grads = [_wgrad_slots(f"ffn_up_wgrad{l}", sv["hn2"], du, True), g_down]
            dh, d_norm_ffn[l] = _dgrad_slots(f"ffn_up_dgrad{l}", du, w[0], 0, (sv["h2"], norm_ffn[l:l + 1], dh), True)
        else:
            if l % 2 == 0:
                do = _dgrad_out(f"attn_out_dgrad{i}", dh, w[1], 0, BF16, dep)
                g_out = _wgrad_rows(f"attn_out_wgrad{i}", sv["o"], dh)
                dproj, d_sinks[i] = _attn_bwd(f"attn_bwd{i}", sv["proj"], *sv["kept"], do)
                g_in = _wgrad_rows(f"attn_proj_wgrad{i}", dproj, sv["hn"]).reshape(N_DEV, ATTN_IN // N_DEV, D_MODEL)
                dh_new = _dgrad_rows(f"attn_proj_dgrad{i}", dproj, w[0], 0, (sv["h"], norm_mix[l:l + 1], dh))
            else:
                dog = _dgrad_out(f"hgrn_out_dgrad{i}", dh, w[1], 0, F32, dep)
                g_out = _wgrad_rows(f"hgrn_out_wgrad{i}", sv["o"], dh)
                dz, d_lb[i], dng = _hg_bwd(f"hgrn_bwd{i}", sv["z"], lb[i:i + 1], hgrn_norm[i:i + 1], *sv["kept"], dog)
                d_hgrn_norm[i] = dng[0] + dng[1]
                dz = dz.reshape(N_DEV, S, HG_SLOT)
                g_in = _wgrad_slots(f"hgrn_proj_wgrad{i}", sv["hn"], dz)
                dh_new = _dgrad_slots(f"hgrn_proj_dgrad{i}", dz, w[0], 0, (sv["h"], norm_mix[l:l + 1], dh))
            grads = [g_in, g_out.reshape(N_DEV, D_MODEL // N_DEV, D_MODEL)]
            dh, d_norm_mix[l] = dh_new
        gate = dh
        if len(pending) == 2:
            key, oldest = pending.pop(0)
            received[key] = _exchange_wait(f"scatter_wait_{key[1]}{key[0]}", oldest, True, dh)
            gate = received[key][0]
        pending.append(((l, part), _exchange_start(f"scatter_start_{part}{l}", grads, True, gate)))
    grad_x = dh[None]

    small_shapes = [(DEPTH, D_MODEL), (DEPTH, D_MODEL), (1, D_MODEL), (1, D_MODEL), (n_hgrn, D_MODEL), (n_attn, 128),
                    (n_hgrn, HG_K), (DEPTH, 2 * D_FF)]
    partial = _pack_rows([
        jnp.concatenate(d_norm_mix), jnp.concatenate(d_norm_ffn), d_norm_final, loss_rows, jnp.concatenate(d_lb),
        jnp.concatenate(d_sinks), jnp.concatenate(d_hgrn_norm), jnp.stack(d_conv_b)])
    d_taps = jnp.stack(d_conv_w).reshape(DEPTH, N_DEV, 3, FF_SLOT).transpose(1, 0, 2, 3).reshape(N_DEV, DEPTH * 3, FF_SLOT)
    small_started = _exchange_start("small_start", [partial, d_taps], (False, True), pending[-1][1][4])
    attn_layers, hgrn_layers = range(0, DEPTH, 2), range(1, DEPTH, 2)

    def transposed(ts):
        return [t.transpose(0, 2, 1) for t in ts]

    big = {"hgrn_w_in": _adamw("adamw_hgrn_in", [received[l, "mix"][0] for l in hgrn_layers], hgrn_w_in, m_hgrn_w_in,
                               v_hgrn_w_in, dep=small_started[4])}
    big["hgrn_w_out"] = _adamw("adamw_hgrn_out", [received[l, "mix"][1] for l in hgrn_layers], hgrn_w_out, m_hgrn_w_out, v_hgrn_w_out)
    key, oldest = pending.pop(0)
    received[key] = _exchange_wait(f"scatter_wait_{key[1]}{key[0]}", oldest, True, big["hgrn_w_in"][3])
    up_t = _adamw("adamw_ffn_up", [received[l, "ffn"][0] for l in range(DEPTH)], *transposed((ffn_w_up, m_ffn_w_up, v_ffn_w_up)))
    big["ffn_w_up"] = transposed(up_t)
    big["ffn_w_down"] = _adamw("adamw_ffn_down", [received[l, "ffn"][1] for l in range(DEPTH)], ffn_w_down, m_ffn_w_down, v_ffn_w_down)
    key, oldest = pending.pop(0)
    received[key] = _exchange_wait(f"scatter_wait_{key[1]}{key[0]}", oldest, True, up_t[3])
    small_parts, taps_parts = _exchange_wait("small_wait", small_started, (False, True), up_t[3])
    total = _sum_devices("sum_small", small_parts)
    (g_norm_mix, g_norm_ffn, g_norm_final, loss_sum, g_lb, g_sinks, g_hgrn_norm, g_conv_b) = _unpack_rows(total, small_shapes)

    loss = jnp.sum(loss_sum)
    g_norm_final = g_norm_final[0]
    g_sinks = g_sinks[:, :N_Q_HEADS]
    g_lb_logits = _lb_bwd("lb_bwd", hgrn_lb_logits, g_lb)

    big.update({
        "attn_w_in": transposed(_adamw("adamw_attn_in", [received[l, "mix"][0] for l in attn_layers],
                                       *transposed((attn_w_in, m_attn_w_in, v_attn_w_in)))),
        "attn_w_out": _adamw("adamw_attn_out", [received[l, "mix"][1] for l in attn_layers], attn_w_out, m_attn_w_out, v_attn_w_out),
        "ffn_conv_w": _adamw("adamw_conv_w", [taps_parts[:, 3 * l:3 * l + 3] for l in range(DEPTH)], ffn_conv_w, m_ffn_conv_w,
                             v_ffn_conv_w),
    })
    small_w = [norm_mix, norm_ffn, norm_final, attn_sinks, hgrn_norm, hgrn_lb_logits, ffn_conv_b]
    small_m = [m_norm_mix, m_norm_ffn, m_norm_final, m_attn_sinks, m_hgrn_norm, m_hgrn_lb_logits, m_ffn_conv_b]
    small_v = [v_norm_mix, v_norm_ffn, v_norm_final, v_attn_sinks, v_hgrn_norm, v_hgrn_lb_logits, v_ffn_conv_b]
    small_g = [g_norm_mix, g_norm_ffn, g_norm_final, g_sinks, g_hgrn_norm, g_lb_logits, g_conv_b]
    outs = _adamw("adamw_small", [_pack_rows(small_g)[None]], *[_pack_rows(t)[None] for t in (small_w, small_m, small_v)])
    outs = [o[0] for o in outs]
    shapes = [w.shape for w in small_w]
    small = {n: [t[j] for t in [_unpack_rows(o, shapes) for o in outs]]
             for j, n in enumerate(["norm_mix", "norm_ffn", "norm_final", "attn_sinks", "hgrn_norm", "hgrn_lb_logits", "ffn_conv_b"])}
    order = ["norm_mix", "norm_ffn", "norm_final", "attn_w_in", "attn_w_out", "attn_sinks", "hgrn_w_in", "hgrn_w_out",
             "hgrn_norm", "hgrn_lb_logits", "ffn_w_up", "ffn_conv_w", "ffn_conv_b", "ffn_w_down"]
    res = {**big, **small}
    return (loss, grad_x, *[res[n][0] for n in order], *[res[n][1] for n in order], *[res[n][2] for n in order],
            *[res[n][3] for n in order])
```

```python
import numpy as np
import jax
import jax.numpy as jnp
from jax import lax
from jax.experimental import pallas as pl
from jax.experimental.pallas import tpu as pltpu

F32 = jnp.float32
BF16 = jnp.bfloat16

D_MODEL = 1024
DEPTH = 4
HEAD_DIM = 64
N_Q_HEADS = 16
N_KV_HEADS = 4
Q_PER_KV = 4
ATTN_BLOCK = 128
ATTN_IN = 1536
HG_HEADS = 8
HG_K = 128
HG_CHUNK = 64
HG_IN = 4096
D_FF = 2816
EPS = 1e-6
N_DEV = 8
FF_SLOT = 2 * D_FF // N_DEV
HG_SLOT = HG_IN // N_DEV
HG_LEVELS = 6

ADAM_LR = 0.001
ADAM_B1 = 0.9
ADAM_B2 = 0.999
ADAM_EPS = 1e-08
ADAM_WD = 0.01
ADAM_STEP = 10

VMEM_LIMIT = 56 * 1024 * 1024
ROW_TILE = 1024
WIDE_ROW_TILE = 2048
NEG_BIG = -1e30

NN = (((1,), (0,)), ((), ()))
NT = (((1,), (1,)), ((), ()))
TN = (((0,), (0,)), ((), ()))


def _bdot(a, b, dn):
    return lax.dot_general(a.astype(BF16), b.astype(BF16), dn, preferred_element_type=F32)


def _sds(shape, dtype):
    return jax.ShapeDtypeStruct(tuple(shape), dtype)


def _params(sem):
    return pltpu.CompilerParams(dimension_semantics=sem, vmem_limit_bytes=VMEM_LIMIT)


DEP_SHAPE = (8, 128)


def _dep_spec(rank):
    return pl.BlockSpec(DEP_SHAPE, lambda *_: (0, 0))


def _matmul(name, a, b, *, dn, grid, a_spec, b_spec, o_spec, out_shape, acc_shape=None, extra=(), extra_specs=(),
            finish=None, dep=None, sem=("parallel", "parallel", "arbitrary")):
    nk = grid[2]
    many = isinstance(out_shape, (list, tuple))
    n_in = 2 + len(extra) + (dep is not None)
    n_out = len(out_shape) if many else 1

    def body(*refs):
        a_ref, b_ref = refs[0], refs[1]
        outs = refs[n_in:n_in + n_out]

        def prod():
            return _bdot(a_ref[...], b_ref[...], dn)

        def done(v):
            if finish is None:
                outs[0][...] = v.astype(outs[0].dtype)
            else:
                finish(v, refs[2:2 + len(extra)], outs)

        if nk == 1:
            done(prod())
        else:
            acc = refs[-1]
            k = pl.program_id(2)

            @pl.when(k == 0)
            def _():
                acc[...] = prod()

            @pl.when(k > 0)
            def _():
                acc[...] += prod()

            @pl.when(k == nk - 1)
            def _():
                done(acc[...])

    in_specs = [a_spec, b_spec, *extra_specs] + ([_dep_spec(3)] if dep is not None else [])
    args = (a, b, *extra) + ((dep,) if dep is not None else ())
    scratch = [] if nk == 1 else [pltpu.VMEM(acc_shape, F32)]
    return pl.pallas_call(
        body, name=name, grid=grid, in_specs=in_specs, out_specs=o_spec, out_shape=out_shape,
        scratch_shapes=scratch, compiler_params=_params(sem),
    )(*args)


def _rms(x):
    return lax.rsqrt(jnp.mean(x * x, axis=-1, keepdims=True) + EPS)


def _residual_finish(v, ex, outs):
    h = v + ex[0][...]
    outs[0][...] = h
    if len(ex) > 1:
        outs[1][...] = (h * _rms(h) * ex[1][...]).astype(outs[1].dtype)


def _norm_bwd_finish(v, ex, outs):
    x = ex[0][...]
    r = _rms(x)
    xh = x * r
    dyg = v * ex[1][...]
    outs[0][...] = ex[2][...] + r * (dyg - xh * jnp.mean(dyg * xh, axis=-1, keepdims=True))
    part = jnp.sum(v * xh, axis=0, keepdims=True)

    @pl.when(pl.program_id(0) == 0)
    def _():
        outs[1][...] = part

    @pl.when(pl.program_id(0) > 0)
    def _():
        outs[1][...] += part


def _row_io(tm, norm_g):
    row = pl.BlockSpec((tm, D_MODEL), lambda i, j, k: (i, 0))
    vec = pl.BlockSpec((1, D_MODEL), lambda i, j, k: (0, 0))
    if norm_g is None:
        return (row,), row, lambda S: _sds((S, D_MODEL), F32)
    return (row, vec), [row, row], lambda S: [_sds((S, D_MODEL), F32), _sds((S, D_MODEL), BF16)]


def _tile(n, t):
    return min(n, t)


def _proj_rows(name, hn, wt, l, out_dtype, dep=None):
    S, N = hn.shape[0], wt.shape[1]
    tm, tn = _tile(S, ROW_TILE), 512
    return _matmul(
        name, hn, wt, dn=NT, grid=(S // tm, N // tn, 1),
        a_spec=pl.BlockSpec((tm, D_MODEL), lambda i, j, k: (i, 0)),
        b_spec=pl.BlockSpec((None, tn, D_MODEL), lambda i, j, k: (l, j, 0)),
        o_spec=pl.BlockSpec((tm, tn), lambda i, j, k: (i, j)),
        out_shape=_sds((S, N), out_dtype), dep=dep)


def _slot_weight(w, transposed):
    if transposed:
        return w.shape[2], (None, None, w.shape[2], D_MODEL), NT, NN
    return w.shape[3], (None, None, D_MODEL, w.shape[3]), NN, NT


def _proj_slots(name, hn, w, l, transposed=False, dep=None, out_dtype=F32):
    S = hn.shape[0]
    r, blk, dn, _ = _slot_weight(w, transposed)
    tm = _tile(S, WIDE_ROW_TILE)
    return _matmul(
        name, hn, w, dn=dn, grid=(N_DEV, S // tm, 1),
        a_spec=pl.BlockSpec((tm, D_MODEL), lambda j, i, k: (i, 0)),
        b_spec=pl.BlockSpec(blk, lambda j, i, k: (l, j, 0, 0)),
        o_spec=pl.BlockSpec((None, tm, r), lambda j, i, k: (j, i, 0)),
        out_shape=_sds((N_DEV, S, r), out_dtype), dep=dep)


def _out_proj(name, o, w, l, h, norm_g=None):
    S, K = o.shape
    tm = _tile(S, ROW_TILE)
    extra_specs, o_spec, out_shape = _row_io(tm, norm_g)
    return _matmul(
        name, o, w, dn=NN, grid=(S // tm, 1, 1),
        a_spec=pl.BlockSpec((tm, K), lambda i, j, k: (i, 0)),
        b_spec=pl.BlockSpec((None, K, D_MODEL), lambda i, j, k: (l, 0, 0)),
        o_spec=o_spec, out_shape=out_shape(S), extra=(h,) if norm_g is None else (h, norm_g),
        extra_specs=extra_specs, finish=_residual_finish)


def _down_proj(name, a, w, l, h, norm_g=None):
    nj, S, r = a.shape
    tm = _tile(S, ROW_TILE)
    extra_specs, o_spec, out_shape = _row_io(tm, norm_g)
    return _matmul(
        name, a, w, dn=NN, grid=(S // tm, 1, nj),
        a_spec=pl.BlockSpec((None, tm, r), lambda i, j, k: (k, i, 0)),
        b_spec=pl.BlockSpec((None, None, r, D_MODEL), lambda i, j, k: (l, k, 0, 0)),
        o_spec=o_spec, out_shape=out_shape(S), acc_shape=(tm, D_MODEL),
        extra=(h,) if norm_g is None else (h, norm_g), extra_specs=extra_specs, finish=_residual_finish)


def _dgrad_down(name, dh, w, l, dep=None):
    S = dh.shape[0]
    nj, r = w.shape[1], w.shape[2]
    tm = _tile(S, ROW_TILE)
    return _matmul(
        name, dh, w, dn=NT, grid=(nj, S // tm, 1),
        a_spec=pl.BlockSpec((tm, D_MODEL), lambda j, i, k: (i, 0)),
        b_spec=pl.BlockSpec((None, None, r, D_MODEL), lambda j, i, k: (l, j, 0, 0)),
        o_spec=pl.BlockSpec((None, tm, r), lambda j, i, k: (j, i, 0)),
        out_shape=_sds((nj, S, r), BF16), dep=dep)


def _wgrad_down(name, a, dh):
    nj, S, r = a.shape
    tk = _tile(S, ROW_TILE)
    return _matmul(
        name, a, dh, dn=TN, grid=(nj, 1, S // tk),
        a_spec=pl.BlockSpec((None, tk, r), lambda s, j, k: (s, k, 0)),
        b_spec=pl.BlockSpec((tk, D_MODEL), lambda s, j, k: (k, 0)),
        o_spec=pl.BlockSpec((None, r, D_MODEL), lambda s, j, k: (s, 0, 0)),
        out_shape=_sds((nj, r, D_MODEL), BF16), acc_shape=(r, D_MODEL))


def _norm_bwd_io(tm, S):
    row = pl.BlockSpec((tm, D_MODEL), lambda i, j, k: (i, 0))
    vec = pl.BlockSpec((1, D_MODEL), lambda i, j, k: (0, 0))
    return dict(extra_specs=(row, vec, row), o_spec=[row, vec], out_shape=[_sds((S, D_MODEL), F32), _sds((1, D_MODEL), F32)],
                finish=_norm_bwd_finish, sem=("arbitrary", "arbitrary", "arbitrary"))


def _dgrad_slots(name, dz, w, l, norm, transposed=False):
    nj, S, r = dz.shape
    _, blk, _, dn = _slot_weight(w, transposed)
    tm = _tile(S, ROW_TILE)
    return _matmul(
        name, dz, w, dn=dn, grid=(S // tm, 1, nj),
        a_spec=pl.BlockSpec((None, tm, r), lambda i, j, k: (k, i, 0)),
        b_spec=pl.BlockSpec(blk, lambda i, j, k: (l, k, 0, 0)),
        acc_shape=(tm, D_MODEL), extra=norm, **_norm_bwd_io(tm, S))


def _wgrad_slots(name, hn, dz, transposed=False):
    nj, S, r = dz.shape
    tk = _tile(S, ROW_TILE)
    hn_spec = pl.BlockSpec((tk, D_MODEL), lambda s, j, k: (k, 0))
    dz_spec = pl.BlockSpec((None, tk, r), lambda s, j, k: (s, k, 0))
    if transposed:
        return _matmul(
            name, dz, hn, dn=TN, grid=(nj, 1, S // tk), a_spec=dz_spec, b_spec=hn_spec,
            o_spec=pl.BlockSpec((None, r, D_MODEL), lambda s, j, k: (s, 0, 0)),
            out_shape=_sds((nj, r, D_MODEL), BF16), acc_shape=(r, D_MODEL))
    return _matmul(
        name, hn, dz, dn=TN, grid=(nj, 1, S // tk), a_spec=hn_spec, b_spec=dz_spec,
        o_spec=pl.BlockSpec((None, D_MODEL, r), lambda s, j, k: (s, 0, 0)),
        out_shape=_sds((nj, D_MODEL, r), BF16), acc_shape=(D_MODEL, r))


def _dgrad_out(name, dh, w, l, out_dtype, dep=None):
    S, K = dh.shape[0], w.shape[1]
    tm = _tile(S, ROW_TILE)
    return _matmul(
        name, dh, w, dn=NT, grid=(S // tm, 1, 1),
        a_spec=pl.BlockSpec((tm, D_MODEL), lambda i, j, k: (i, 0)),
        b_spec=pl.BlockSpec((None, K, D_MODEL), lambda i, j, k: (l, 0, 0)),
        o_spec=pl.BlockSpec((tm, K), lambda i, j, k: (i, 0)),
        out_shape=_sds((S, K), out_dtype), dep=dep)


def _wgrad_rows(name, a, b):
    S, K = a.shape
    tk = _tile(S, ROW_TILE)
    return _matmul(
        name, a, b, dn=TN, grid=(1, 1, S // tk),
        a_spec=pl.BlockSpec((tk, K), lambda i, j, k: (k, 0)),
        b_spec=pl.BlockSpec((tk, D_MODEL), lambda i, j, k: (k, 0)),
        o_spec=pl.BlockSpec((K, D_MODEL), lambda i, j, k: (0, 0)),
        out_shape=_sds((K, D_MODEL), BF16), acc_shape=(K, D_MODEL))


def _dgrad_rows(name, dz, wt, l, norm):
    S, N = dz.shape
    tm = _tile(S, ROW_TILE)
    return _matmul(
        name, dz, wt, dn=NN, grid=(S // tm, 1, 1),
        a_spec=pl.BlockSpec((tm, N), lambda i, j, k: (i, 0)),
        b_spec=pl.BlockSpec((None, N, D_MODEL), lambda i, j, k: (l, 0, 0)),
        extra=norm, **_norm_bwd_io(tm, S))


def _rmsnorm_fwd(name, h, g):
    S = h.shape[0]
    tm = _tile(S, ROW_TILE)

    def body(h_ref, g_ref, o_ref):
        x = h_ref[...]
        o_ref[...] = (x * _rms(x) * g_ref[...]).astype(o_ref.dtype)

    row = pl.BlockSpec((tm, D_MODEL), lambda i: (i, 0))
    return pl.pallas_call(
        body, name=name, grid=(S // tm,), in_specs=[row, pl.BlockSpec((1, D_MODEL), lambda i: (0, 0))],
        out_specs=row, out_shape=_sds((S, D_MODEL), BF16), compiler_params=_params(("parallel",)),
    )(h, g)


def _loss_head(name, h, g, target):
    S = h.shape[0]
    tm = _tile(S, ROW_TILE)

    def body(h_ref, g_ref, t_ref, dh_ref, dg_ref, ls_ref):
        x = h_ref[...]
        r = lax.rsqrt(jnp.mean(x * x, axis=-1, keepdims=True) + EPS)
        xh = x * r
        diff = xh * g_ref[...] - t_ref[...]
        dyf = diff * (1.0 / D_MODEL)
        dyg = dyf * g_ref[...]
        dh_ref[...] = r * (dyg - xh * jnp.mean(dyg * xh, axis=-1, keepdims=True))
        part = jnp.sum(dyf * xh, axis=0, keepdims=True)
        lpart = jnp.sum(diff * diff, axis=0, keepdims=True) * (0.5 / D_MODEL)

        @pl.when(pl.program_id(0) == 0)
        def _():
            dg_ref[...] = part
            ls_ref[...] = lpart

        @pl.when(pl.program_id(0) > 0)
        def _():
            dg_ref[...] += part
            ls_ref[...] += lpart

    row = pl.BlockSpec((tm, D_MODEL), lambda i: (i, 0))
    vec = pl.BlockSpec((1, D_MODEL), lambda i: (0, 0))
    return pl.pallas_call(
        body, name=name, grid=(S // tm,), in_specs=[row, vec, row], out_specs=[row, vec, vec],
        out_shape=[_sds((S, D_MODEL), F32), _sds((1, D_MODEL), F32), _sds((1, D_MODEL), F32)],
        compiler_params=_params(("arbitrary",)),
    )(h, g, target)


ATTN_SCALE = HEAD_DIM ** -0.5
ALIBI_SLOPES = [2.0 ** (-8.0 * (h + 1) / N_Q_HEADS) for h in range(N_Q_HEADS)]
K_COL = N_Q_HEADS * HEAD_DIM
KV_COLS = N_KV_HEADS * HEAD_DIM
V_COL = K_COL + KV_COLS


def _attn_masks(n):
    qi = lax.broadcasted_iota(jnp.int32, (ATTN_BLOCK, ATTN_BLOCK), 0)
    ki = lax.broadcasted_iota(jnp.int32, (ATTN_BLOCK, ATTN_BLOCK), 1)
    dist_c = (qi - ki).astype(F32)
    return dist_c + float(ATTN_BLOCK), dist_c, (ki > qi) & (n > 0), qi >= ki


def _attn_probs(raw_p, raw_c, sink, slope, masks):
    dist_p, dist_c, valid_p, valid_c = masks
    sp = jnp.where(valid_p, raw_p * ATTN_SCALE - slope * dist_p, NEG_BIG)
    sc = jnp.where(valid_c, raw_c * ATTN_SCALE - slope * dist_c, NEG_BIG)
    m = jnp.maximum(jnp.maximum(jnp.max(sp, axis=-1, keepdims=True), jnp.max(sc, axis=-1, keepdims=True)), sink)
    ep, ec, es = jnp.exp(sp - m), jnp.exp(sc - m), jnp.exp(sink - m)
    inv = 1.0 / (jnp.sum(ep, axis=-1, keepdims=True) + jnp.sum(ec, axis=-1, keepdims=True) + es)
    return ep * inv, ec * inv, es * inv


def _group_rows(ref, m):
    return jnp.concatenate([ref[:, HEAD_DIM * (Q_PER_KV * m + g):HEAD_DIM * (Q_PER_KV * m + g + 1)]
                            for g in range(Q_PER_KV)], axis=0)


def _head_rows(x, g):
    return x[ATTN_BLOCK * g:ATTN_BLOCK * (g + 1)]


def _attn_specs(nblk):
    last = nblk - 1
    kcol, vcol = K_COL // KV_COLS, V_COL // KV_COLS
    return [
        pl.BlockSpec((ATTN_BLOCK, K_COL), lambda n: (jnp.minimum(n, last), 0)),
        pl.BlockSpec((ATTN_BLOCK, KV_COLS), lambda n: (jnp.minimum(n, last), kcol)),
        pl.BlockSpec((ATTN_BLOCK, KV_COLS), lambda n: (jnp.maximum(jnp.minimum(n, last) - 1, 0), kcol)),
        pl.BlockSpec((ATTN_BLOCK, KV_COLS), lambda n: (jnp.minimum(n, last), vcol)),
        pl.BlockSpec((ATTN_BLOCK, KV_COLS), lambda n: (jnp.maximum(jnp.minimum(n, last) - 1, 0), vcol)),
    ]


P_COLS = 2 * ATTN_BLOCK


def _attn_fwd(name, proj, sinks):
    S = proj.shape[0]
    nblk = S // ATTN_BLOCK

    def body(q_ref, kc_ref, kp_ref, vc_ref, vp_ref, sk_ref, o_ref, p_ref, ps_ref):
        masks = _attn_masks(pl.program_id(0))
        lane = lax.broadcasted_iota(jnp.int32, (ATTN_BLOCK, 128), 1)
        sink_p = jnp.zeros((ATTN_BLOCK, 128), F32)
        for m in range(N_KV_HEADS):
            ks = slice(HEAD_DIM * m, HEAD_DIM * (m + 1))
            kp, kc, vp, vc = kp_ref[:, ks], kc_ref[:, ks], vp_ref[:, ks], vc_ref[:, ks]
            q4 = _group_rows(q_ref, m)
            raw_p, raw_c = _bdot(q4, kp, NT), _bdot(q4, kc, NT)
            pps, pcs = [], []
            for g in range(Q_PER_KV):
                hh = Q_PER_KV * m + g
                pp, pc, ps = _attn_probs(_head_rows(raw_p, g), _head_rows(raw_c, g), sk_ref[0, hh], ALIBI_SLOPES[hh], masks)
                pps.append(pp.astype(BF16))
                pcs.append(pc.astype(BF16))
                p_ref[:, P_COLS * hh:P_COLS * hh + ATTN_BLOCK] = pps[g]
                p_ref[:, P_COLS * hh + ATTN_BLOCK:P_COLS * (hh + 1)] = pcs[g]
                sink_p = jnp.where(lane == hh, ps, sink_p)
            o4 = _bdot(jnp.concatenate(pps, axis=0), vp, NN) + _bdot(jnp.concatenate(pcs, axis=0), vc, NN)
            for g in range(Q_PER_KV):
                hh = Q_PER_KV * m + g
                o_ref[:, HEAD_DIM * hh:HEAD_DIM * (hh + 1)] = _head_rows(o4, g).astype(o_ref.dtype)
        ps_ref[...] = sink_p

    row = lambda cols: pl.BlockSpec((ATTN_BLOCK, cols), lambda n: (n, 0))
    return pl.pallas_call(
        body, name=name, grid=(nblk,),
        in_specs=_attn_specs(nblk) + [pl.BlockSpec(memory_space=pltpu.SMEM)],
        out_specs=[row(K_COL), row(N_Q_HEADS * P_COLS), row(128)],
        out_shape=[_sds((S, K_COL), BF16), _sds((S, N_Q_HEADS * P_COLS), BF16), _sds((S, 128), F32)],
        compiler_params=_params(("parallel",)),
    )(proj, proj, proj, proj, proj, sinks)


def _attn_bwd(name, proj, probs, sink_probs, do):
    S = proj.shape[0]
    nblk = S // ATTN_BLOCK

    def body(q_ref, kc_ref, kp_ref, vc_ref, vp_ref, do_ref, p_ref, ps_ref, dz_ref, ds_ref, carry, cur, padd):
        n = pl.program_id(0)

        @pl.when(n == 0)
        def _():
            carry[...] = jnp.zeros_like(carry)
            ds_ref[...] = jnp.zeros_like(ds_ref)

        @pl.when(n < nblk)
        def _():
            lane = lax.broadcasted_iota(jnp.int32, (ATTN_BLOCK, 128), 1)
            sink_p = ps_ref[...]
            dsv = jnp.zeros((1, 128), F32)
            for m in range(N_KV_HEADS):
                ks = slice(HEAD_DIM * m, HEAD_DIM * (m + 1))
                kp, kc, vp, vc = kp_ref[:, ks], kc_ref[:, ks], vp_ref[:, ks], vc_ref[:, ks]
                q4, do4 = _group_rows(q_ref, m), _group_rows(do_ref, m)
                dpp4, dpc4 = _bdot(do4, vp, NT), _bdot(do4, vc, NT)
                pps, pcs, dsps, dscs = [], [], [], []
                for g in range(Q_PER_KV):
                    hh = Q_PER_KV * m + g
                    pps.append(p_ref[:, P_COLS * hh:P_COLS * hh + ATTN_BLOCK])
                    pcs.append(p_ref[:, P_COLS * hh + ATTN_BLOCK:P_COLS * (hh + 1)])
                    pp, pc = pps[g].astype(F32), pcs[g].astype(F32)
                    dpp, dpc = _head_rows(dpp4, g), _head_rows(dpc4, g)
                    delta = jnp.sum(pp * dpp, axis=-1, keepdims=True) + jnp.sum(pc * dpc, axis=-1, keepdims=True)
                    dsv = dsv - jnp.sum(jnp.where(lane == hh, sink_p, 0.0) * delta, axis=0, keepdims=True)
                    dsps.append((pp * (dpp - delta)).astype(BF16))
                    dscs.append((pc * (dpc - delta)).astype(BF16))
                pp4, pc4 = jnp.concatenate(pps, axis=0), jnp.concatenate(pcs, axis=0)
                dsp4, dsc4 = jnp.concatenate(dsps, axis=0), jnp.concatenate(dscs, axis=0)
                dq4 = (_bdot(dsp4, kp, NN) + _bdot(dsc4, kc, NN)) * ATTN_SCALE
                for g in range(Q_PER_KV):
                    hh = Q_PER_KV * m + g
                    cur[:, HEAD_DIM * hh:HEAD_DIM * (hh + 1)] = _head_rows(dq4, g)
                cur[:, K_COL + HEAD_DIM * m:K_COL + HEAD_DIM * (m + 1)] = _bdot(dsc4, q4, TN) * ATTN_SCALE
                cur[:, V_COL + HEAD_DIM * m:V_COL + HEAD_DIM * (m + 1)] = _bdot(pc4, do4, TN)
                padd[:, ks] = _bdot(dsp4, q4, TN) * ATTN_SCALE
                padd[:, KV_COLS + HEAD_DIM * m:KV_COLS + HEAD_DIM * (m + 1)] = _bdot(pp4, do4, TN)
            ds_ref[...] += dsv
            dz_ref[:, :K_COL] = carry[:, :K_COL].astype(dz_ref.dtype)
            dz_ref[:, K_COL:] = (carry[:, K_COL:] + padd[...]).astype(dz_ref.dtype)
            carry[...] = cur[...]

        @pl.when(n == nblk)
        def _():
            dz_ref[...] = carry[...].astype(dz_ref.dtype)

    return pl.pallas_call(
        body, name=name, grid=(nblk + 1,),
        in_specs=_attn_specs(nblk) + [
            pl.BlockSpec((ATTN_BLOCK, cols), lambda n: (jnp.minimum(n, nblk - 1), 0))
            for cols in (K_COL, N_Q_HEADS * P_COLS, 128)],
        out_specs=[pl.BlockSpec((ATTN_BLOCK, ATTN_IN), lambda n: (jnp.maximum(n - 1, 0), 0)),
                   pl.BlockSpec((1, 128), lambda n: (0, 0))],
        out_shape=[_sds((S, ATTN_IN), BF16), _sds((1, 128), F32)],
        scratch_shapes=[pltpu.VMEM((ATTN_BLOCK, ATTN_IN), F32), pltpu.VMEM((ATTN_BLOCK, ATTN_IN), F32),
                        pltpu.VMEM((ATTN_BLOCK, 2 * KV_COLS), F32)],
        compiler_params=_params(("arbitrary",)),
    )(proj, proj, proj, proj, proj, do, probs, sink_probs)


def _hg_consts():
    C = HG_CHUNK
    tri = np.tril(np.ones((C, C)))
    t = np.arange(C)
    rows, masks = [tri], []
    for lvl in range(HG_LEVELS):
        n = C >> (lvl + 1)
        sel = np.zeros((C, C))
        sel[t, (t // (2 * n)) * (2 * n) + n - 1] = 1.0
        rows.append(sel @ tri)
        tt, ss = t[:, None], t[None, :]
        masks.append((tt // (2 * n) == ss // (2 * n)) & ((tt // n) % 2 == 1) & ((ss // n) % 2 == 0))
    masks.append(np.eye(C, dtype=bool))
    stk = np.concatenate(rows, axis=0)
    return jnp.asarray(stk, BF16), jnp.asarray(np.stack(masks), F32)


def _sigmoid(x):
    return 1.0 / (1.0 + jnp.exp(-x))


def _silu_sigmoid(x):
    return 0.5 + 0.5 * jnp.tanh(0.5 * x)


def _split(x, parts):
    out, rest = [], x
    for _ in range(parts):
        out.append(rest.astype(BF16))
        rest = rest - out[-1].astype(F32)
    return out


def _dot01(m01, x, dn, parts=3):
    return sum(lax.dot_general(m01, p, dn, preferred_element_type=F32) for p in _split(x, parts))


def _ref_rows(b, n):
    C = b.shape[1]
    if 2 * n >= 8:
        b3 = b.reshape(HG_CHUNK // (2 * n), 2 * n, C)
        return jnp.broadcast_to(b3[:, n - 1:n, :], b3.shape).reshape(HG_CHUNK, C)
    pos = lax.broadcasted_iota(jnp.int32, b.shape, 0) % (2 * n)
    out = b
    for p in range(2 * n):
        if p != n - 1:
            out = jnp.where(pos == p, pltpu.roll(b, (p - (n - 1)) % HG_CHUNK, 0), out)
    return out


HG_STEP_CHUNKS = 4


def _chunk_rows(ci):
    return pl.ds(pl.multiple_of(ci * HG_CHUNK, HG_CHUNK), HG_CHUNK)


def _hg_common(z_ref, rows, lb_ref, stk_ref):
    qr, fr = z_ref[0, rows, :], z_ref[1, rows, :]
    lb = lb_ref[...]
    sq, sg, sgn = _silu_sigmoid(qr), _sigmoid(fr), _sigmoid(-fr)
    ft = lb + (1.0 - lb) * sg
    b = _dot01(stk_ref[0:HG_CHUNK, :], jnp.log(ft), NN)
    ws = [jnp.exp(-jnp.abs(b - _ref_rows(b, HG_CHUNK >> (l + 1)))) for l in range(HG_LEVELS)]
    blast = b[HG_CHUNK - 1:HG_CHUNK]
    return dict(qr=qr, fr=fr, lb=lb, sq=sq, sg=sg, sgn=sgn, ft=ft, q=qr * sq, kk=(1.0 - lb) * sgn, b=b,
                ws=ws, eb=jnp.exp(b), ed=jnp.exp(blast - b), elast=jnp.exp(blast))


def _hg_factors(qh, kh, ws, sl):
    return ([(qh * ws[l][:, sl]).astype(BF16) for l in range(HG_LEVELS)],
            [(kh * ws[l][:, sl]).astype(BF16) for l in range(HG_LEVELS)])


def _hg_intra(qh, kh, ws, msk_ref, sl):
    qls, kls = _hg_factors(qh, kh, ws, sl)
    a = msk_ref[HG_LEVELS] * _bdot(qh, kh, NT)
    for l in range(HG_LEVELS):
        a = a + msk_ref[l] * _bdot(qls[l], kls[l], NT)
    return a


def _hg_fwd(name, z, lb, ng):
    S = z.shape[2]
    nc = S // HG_CHUNK
    per = min(HG_STEP_CHUNKS, nc)
    stk, msk = _hg_consts()

    def body(z_ref, lb_ref, ng_ref, stk_ref, msk_ref, og_ref, st_ref, a_ref, o_ref, state):
        @pl.when(pl.program_id(1) == 0)
        def _():
            state[...] = jnp.zeros_like(state)

        def chunk(ci, _):
            rows = _chunk_rows(ci)
            cm = _hg_common(z_ref, rows, lb_ref, stk_ref)
            v, gt = z_ref[2, rows, :], z_ref[3, rows, :]
            kd = cm["kk"] * cm["ed"]
            for hh in range(4):
                sl = slice(HG_K * hh, HG_K * (hh + 1))
                st = state[hh]
                st_ref[ci, hh] = st
                qh, kh, vh = cm["q"][:, sl], cm["kk"][:, sl], v[:, sl]
                a = _hg_intra(qh, kh, cm["ws"], msk_ref, sl).astype(BF16)
                a_ref[ci, hh] = a
                o = _bdot(a, vh, NN) + _bdot(qh * cm["eb"][:, sl], st, NT)
                o_ref[rows, sl] = o
                state[hh] = cm["elast"][:, sl] * st + _bdot(vh, kd[:, sl], TN)
                gh = gt[:, sl]
                og_ref[rows, sl] = (o * _rms(o) * ng_ref[...] * (gh * _silu_sigmoid(gh))).astype(og_ref.dtype)
            return 0

        lax.fori_loop(0, per, chunk, 0, unroll=True)

    return pl.pallas_call(
        body, name=name, grid=(2, nc // per),
        in_specs=[pl.BlockSpec((4, None, per * HG_CHUNK, HG_SLOT), lambda g, c: (0, g, c, 0)),
                  pl.BlockSpec((1, HG_SLOT), lambda g, c: (0, g)),
                  pl.BlockSpec((1, HG_K), lambda g, c: (0, 0)),
                  pl.BlockSpec(stk.shape, lambda g, c: (0, 0)),
                  pl.BlockSpec(msk.shape, lambda g, c: (0, 0, 0))],
        out_specs=[pl.BlockSpec((per * HG_CHUNK, HG_SLOT), lambda g, c: (c, g)),
                   pl.BlockSpec((per, 4, HG_K, HG_K), lambda g, c: (c, g, 0, 0)),
                   pl.BlockSpec((per, 4, HG_CHUNK, HG_CHUNK), lambda g, c: (c, g, 0, 0)),
                   pl.BlockSpec((per * HG_CHUNK, HG_SLOT), lambda g, c: (c, g))],
        out_shape=[_sds((S, D_MODEL), BF16), _sds((nc, HG_HEADS, HG_K, HG_K), F32),
                   _sds((nc, HG_HEADS, HG_CHUNK, HG_CHUNK), BF16), _sds((S, D_MODEL), F32)],
        scratch_shapes=[pltpu.VMEM((4, HG_K, HG_K), F32)],
        compiler_params=_params(("parallel", "arbitrary")),
    )(z, lb, ng, stk, msk)


def _hg_bwd(name, z, lb, ng, states, intra, o_pre, dog):
    S = z.shape[2]
    nc = S // HG_CHUNK
    per = min(HG_STEP_CHUNKS, nc)
    stk, msk = _hg_consts()

    def body(z_ref, lb_ref, ng_ref, stk_ref, msk_ref, st_ref, a_ref, o_ref, dog_ref, dz_ref, dlb_ref, dng_ref, dstate):
        @pl.when(pl.program_id(1) == 0)
        def _():
            dstate[...] = jnp.zeros_like(dstate)
            dlb_ref[...] = jnp.zeros_like(dlb_ref)
            dng_ref[...] = jnp.zeros_like(dng_ref)

        def chunk(k, _):
            ci = per - 1 - k
            rows = _chunk_rows(ci)
            cm = _hg_common(z_ref, rows, lb_ref, stk_ref)
            v, gt = z_ref[2, rows, :], z_ref[3, rows, :]
            ng = ng_ref[...]
            kd = cm["kk"] * cm["ed"]
            row = lax.broadcasted_iota(jnp.int32, (HG_CHUNK, 1), 0)
            dng = jnp.zeros((1, HG_K), F32)
            dq_h, dkk_h, db_h, dv_h, dgt_h = [], [], [], [], []
            dr_h = [[] for _ in range(HG_LEVELS)]
            for hh in range(4):
                sl = slice(HG_K * hh, HG_K * (hh + 1))
                st, dst = st_ref[ci, hh], dstate[hh]
                qh, kh, vh, ebh, edh, kdh = cm["q"][:, sl], cm["kk"][:, sl], v[:, sl], cm["eb"][:, sl], cm["ed"][:, sl], kd[:, sl]
                elh = cm["elast"][:, sl]
                qls, kls = _hg_factors(qh, kh, cm["ws"], sl)
                a, o = a_ref[ci, hh], o_ref[rows, sl]
                qe = qh * ebh
                r = _rms(o)
                xh = o * r
                gh = gt[:, sl]
                sgg = _silu_sigmoid(gh)
                dog = dog_ref[rows, sl].astype(F32)
                dy = dog * (gh * sgg)
                dgt_h.append(dog * (xh * ng) * (sgg * (1.0 + gh * (1.0 - sgg))))
                dng = dng + jnp.sum(dy * xh, axis=0, keepdims=True)
                dyg = dy * ng
                do = r * (dyg - xh * jnp.mean(dyg * xh, axis=-1, keepdims=True))
                da = _bdot(do, vh, NT)
                dv_h.append(_bdot(a, do, TN) + _bdot(kdh, dst, NT))
                dkd = _bdot(vh, dst, NN)
                delast = jnp.sum(st * dst, axis=0, keepdims=True)
                dqe = _bdot(do, st, NN)
                dstate[hh] = elh * dst + _bdot(do, qe, TN)
                gk = dkd * kdh
                dblast = jnp.sum(gk, axis=0, keepdims=True) + delast * elh
                db = dqe * qe - gk + jnp.where(row == HG_CHUNK - 1, dblast, 0.0)
                dp = (msk_ref[HG_LEVELS] * da).astype(BF16)
                dq = dqe * ebh + _bdot(dp, kh, NN)
                dkk = dkd * edh + _bdot(dp, qh, TN)
                for l in range(HG_LEVELS):
                    dp = (msk_ref[l] * da).astype(BF16)
                    dql, dkl = _bdot(dp, kls[l], NN), _bdot(dp, qls[l], TN)
                    w = cm["ws"][l][:, sl]
                    dq = dq + dql * w
                    dkk = dkk + dkl * w
                    half = jnp.where(((row >> (HG_LEVELS - 1 - l)) & 1) == 1, 1.0, -1.0)
                    dd = half * w * (dql * qh + dkl * kh)
                    db = db + dd
                    dr_h[l].append(-dd)
                dq_h.append(dq)
                dkk_h.append(dkk)
                db_h.append(db)
            cat = lambda xs: jnp.concatenate(xs, axis=1)
            cot = jnp.concatenate([cat(db_h)] + [cat(dr_h[l]) for l in range(HG_LEVELS)], axis=0)
            dlf = _dot01(stk_ref[...], cot, TN, parts=2)
            dq, dkk = cat(dq_h), cat(dkk_h)
            dft = dlf / cm["ft"]
            one_lb = 1.0 - cm["lb"]
            dz_ref[0, rows, :] = (dq * (cm["sq"] * (1.0 + cm["qr"] * (1.0 - cm["sq"])))).astype(dz_ref.dtype)
            dz_ref[1, rows, :] = ((dft - dkk) * one_lb * cm["sg"] * cm["sgn"]).astype(dz_ref.dtype)
            dz_ref[2, rows, :] = cat(dv_h).astype(dz_ref.dtype)
            dz_ref[3, rows, :] = cat(dgt_h).astype(dz_ref.dtype)
            dlb_ref[...] += jnp.sum((dft - dkk) * cm["sgn"], axis=0, keepdims=True)
            dng_ref[...] += dng
            return 0

        lax.fori_loop(0, per, chunk, 0, unroll=True)

    rev = lambda c: nc // per - 1 - c
    rows_blk = pl.BlockSpec((per * HG_CHUNK, HG_SLOT), lambda g, c: (rev(c), g))
    return pl.pallas_call(
        body, name=name, grid=(2, nc // per),
        in_specs=[pl.BlockSpec((4, None, per * HG_CHUNK, HG_SLOT), lambda g, c: (0, g, rev(c), 0)),
                  pl.BlockSpec((1, HG_SLOT), lambda g, c: (0, g)),
                  pl.BlockSpec((1, HG_K), lambda g, c: (0, 0)),
                  pl.BlockSpec(stk.shape, lambda g, c: (0, 0)),
                  pl.BlockSpec(msk.shape, lambda g, c: (0, 0, 0)),
                  pl.BlockSpec((per, 4, HG_K, HG_K), lambda g, c: (rev(c), g, 0, 0)),
                  pl.BlockSpec((per, 4, HG_CHUNK, HG_CHUNK), lambda g, c: (rev(c), g, 0, 0)),
                  rows_blk, rows_blk],
        out_specs=[pl.BlockSpec((4, None, per * HG_CHUNK, HG_SLOT), lambda g, c: (0, g, rev(c), 0)),
                   pl.BlockSpec((1, HG_SLOT), lambda g, c: (0, g)),
                   pl.BlockSpec((None, 1, HG_K), lambda g, c: (g, 0, 0))],
        out_shape=[_sds(z.shape, BF16), _sds((1, 2 * HG_SLOT), F32), _sds((2, 1, HG_K), F32)],
        scratch_shapes=[pltpu.VMEM((4, HG_K, HG_K), F32)],
        compiler_params=_params(("parallel", "arbitrary")),
    )(z, lb, ng, stk, msk, states, intra, o_pre, dog)


def _lb_fwd(name, logits):
    def body(l_ref, o_ref):
        x = l_ref[...]
        e = jnp.exp(x - jnp.max(x, axis=0, keepdims=True))
        s = e / jnp.sum(e, axis=0, keepdims=True)
        o_ref[0:1, :] = s[1:2]
        o_ref[1:2, :] = s[1:2] + s[2:3] + s[3:4]

    return pl.pallas_call(body, name=name, out_shape=_sds((2, logits.shape[1]), F32))(logits)


def _lb_bwd(name, logits, dlb):
    def body(l_ref, d_ref, o_ref):
        x = l_ref[...]
        e = jnp.exp(x - jnp.max(x, axis=0, keepdims=True))
        s = e / jnp.sum(e, axis=0, keepdims=True)
        d1, d3 = d_ref[0:1, :], d_ref[1:2, :]
        ds = [jnp.zeros_like(d1), d1 + d3, d3, d3]
        dot = sum(ds[r] * s[r:r + 1] for r in range(1, DEPTH))
        for r in range(DEPTH):
            o_ref[r:r + 1, :] = s[r:r + 1] * (ds[r] - dot)

    return pl.pallas_call(body, name=name, out_shape=_sds(logits.shape, F32))(logits, dlb)


SUB = 8


def _rows_down(x, prev, k):
    row = lax.broadcasted_iota(jnp.int32, x.shape, 0)
    return jnp.where(row >= k, pltpu.roll(x, k, 0), pltpu.roll(prev, k, 0))


def _rows_up(x, nxt, k):
    row = lax.broadcasted_iota(jnp.int32, x.shape, 0)
    return jnp.where(row < SUB - k, pltpu.roll(x, SUB - k, 0), pltpu.roll(nxt, SUB - k, 0))


def _conv_block(w_ref, b_ref, p, x, prev):
    return (b_ref[p] + w_ref[p, 0:1, :] * _rows_down(x, prev, 2) + w_ref[p, 1:2, :] * _rows_down(x, prev, 1)
            + w_ref[p, 2:3, :] * x)


def _convgate_fwd(name, u, cw, cb):
    S = u.shape[2]
    tm = _tile(S, ROW_TILE)

    def body(u_ref, w_ref, b_ref, a_ref, c_ref, halo):
        @pl.when(pl.program_id(1) == 0)
        def _():
            halo[...] = jnp.zeros_like(halo)

        def step(r, prev):
            pg, pv = prev
            out, cgs, cvs = [], [], []
            rows = pl.ds(pl.multiple_of(r * 2 * SUB, 2 * SUB), 2 * SUB)
            ug16, uv16 = u_ref[0, rows, :].astype(F32), u_ref[1, rows, :].astype(F32)
            for s in range(2):
                xg, xv = ug16[s * SUB:(s + 1) * SUB], uv16[s * SUB:(s + 1) * SUB]
                cgs.append(_conv_block(w_ref, b_ref, 0, xg, pg))
                cvs.append(_conv_block(w_ref, b_ref, 1, xv, pv))
                out.append(cgs[s] * _silu_sigmoid(cgs[s]) * cvs[s])
                pg, pv = xg, xv
            a_ref[rows, :] = jnp.concatenate(out, axis=0).astype(a_ref.dtype)
            c_ref[0, rows, :] = jnp.concatenate(cgs, axis=0).astype(c_ref.dtype)
            c_ref[1, rows, :] = jnp.concatenate(cvs, axis=0).astype(c_ref.dtype)
            return pg, pv

        pg, pv = lax.fori_loop(0, tm // (2 * SUB), step, (halo[0], halo[1]), unroll=2)
        halo[0] = pg
        halo[1] = pv

    pair = pl.BlockSpec((2, None, tm, FF_SLOT), lambda j, t: (0, j, t, 0))
    return pl.pallas_call(
        body, name=name, grid=(4, S // tm),
        in_specs=[pair, pl.BlockSpec((2, None, 3, FF_SLOT), lambda j, t: (0, j, 0, 0)),
                  pl.BlockSpec((2, None, 1, FF_SLOT), lambda j, t: (0, j, 0, 0))],
        out_specs=[pl.BlockSpec((None, tm, FF_SLOT), lambda j, t: (j, t, 0)), pair],
        out_shape=[_sds((4, S, FF_SLOT), BF16), _sds(u.shape, BF16)],
        scratch_shapes=[pltpu.VMEM((2, SUB, FF_SLOT), F32)],
        compiler_params=_params(("parallel", "arbitrary")),
    )(u, cw, cb)


def _convgate_bwd(name, u, convs, cw, da):
    S = u.shape[2]
    tm = _tile(S, ROW_TILE)
    nt = S // tm

    def body(u_ref, c_ref, w_ref, da_ref, du_out, dw_ref, db_ref, after, first, acc, du_ref):
        @pl.when(pl.program_id(1) == 0)
        def _():
            after[...] = jnp.zeros_like(after)
            acc[...] = jnp.zeros_like(acc)

        def finish(p, x, d, nxt, rows):
            taps = (_rows_up(d, nxt, 2), _rows_up(d, nxt, 1), d)
            du_ref[p, rows, :] = w_ref[p, 0:1, :] * taps[0] + w_ref[p, 1:2, :] * taps[1] + w_ref[p, 2:3, :] * d
            for j in range(3):
                acc[p, j] += taps[j] * x
            acc[p, 3] += d

        def step(r, carry):
            xg_last, xv_last, dg_last, dv_last = carry
            rows16 = pl.ds(pl.multiple_of(r * 2 * SUB, 2 * SUB), 2 * SUB)
            dav = da_ref[rows16, :].astype(F32)
            cg16, cv16 = c_ref[0, rows16, :].astype(F32), c_ref[1, rows16, :].astype(F32)
            ug16, uv16 = u_ref[0, rows16, :].astype(F32), u_ref[1, rows16, :].astype(F32)
            for s in range(2):
                at = r * 2 * SUB + s * SUB
                part = slice(s * SUB, (s + 1) * SUB)
                cg, cv, dab = cg16[part], cv16[part], dav[part]
                sg = _silu_sigmoid(cg)
                dg = dab * cv * (sg * (1.0 + cg * (1.0 - sg)))
                dv = dab * cg * sg
                before = pl.ds(pl.multiple_of(at - SUB, SUB), SUB)
                if s == 0:
                    @pl.when(r == 0)
                    def _():
                        first[0] = dg
                        first[1] = dv

                    @pl.when(r > 0)
                    def _():
                        finish(0, xg_last, dg_last, dg, before)
                        finish(1, xv_last, dv_last, dv, before)
                else:
                    finish(0, xg_last, dg_last, dg, before)
                    finish(1, xv_last, dv_last, dv, before)
                xg_last, xv_last, dg_last, dv_last = ug16[part], uv16[part], dg, dv
            return xg_last, xv_last, dg_last, dv_last

        zero = jnp.zeros((SUB, FF_SLOT), F32)
        xg_last, xv_last, dg_last, dv_last = lax.fori_loop(0, tm // (2 * SUB), step, (zero, zero, zero, zero))
        finish(0, xg_last, dg_last, after[0], slice(tm - SUB, tm))
        finish(1, xv_last, dv_last, after[1], slice(tm - SUB, tm))
        du_out[...] = du_ref[...].astype(du_out.dtype)
        after[...] = first[...]
        for p in range(2):
            for j in range(3):
                dw_ref[p, j:j + 1, :] = jnp.sum(acc[p, j], axis=0, keepdims=True)
            db_ref[p] = jnp.sum(acc[p, 3], axis=0, keepdims=True)

    rev = lambda t: nt - 1 - t
    pair = pl.BlockSpec((2, None, tm, FF_SLOT), lambda j, t: (0, j, rev(t), 0))
    taps = pl.BlockSpec((2, None, 3, FF_SLOT), lambda j, t: (0, j, 0, 0))
    bias = pl.BlockSpec((2, None, 1, FF_SLOT), lambda j, t: (0, j, 0, 0))
    return pl.pallas_call(
        body, name=name, grid=(4, nt),
        in_specs=[pair, pair, taps, pl.BlockSpec((None, tm, FF_SLOT), lambda j, t: (j, rev(t), 0))],
        out_specs=[pair, taps, bias],
        out_shape=[_sds(u.shape, BF16), _sds(cw.shape, F32), _sds((2, 4, 1, FF_SLOT), F32)],
        scratch_shapes=[pltpu.VMEM((2, SUB, FF_SLOT), F32), pltpu.VMEM((2, SUB, FF_SLOT), F32),
                        pltpu.VMEM((2, 4, SUB, FF_SLOT), F32), pltpu.VMEM((2, tm, FF_SLOT), F32)],
        compiler_params=_params(("parallel", "arbitrary")),
    )(u, convs, cw, da)


def _row_tile(R):
    for t in range(256, 15, -16):
        if R % t == 0:
            return t
    return R


def _adamw(name, gsrcs, w, m, v, dep=None):
    L = len(gsrcs)
    n, A, C = gsrcs[0].shape
    tr = _row_tile(A)
    deps = () if dep is None else (dep,)

    def body(*refs):
        g_refs = refs[:L]
        w_ref, m_ref, v_ref = refs[L:L + 3]
        go_ref, d_ref, mo_ref, vo_ref = refs[L + 3 + len(deps):]
        for k in range(L):
            @pl.when(pl.program_id(0) == k)
            def _(k=k):
                g = g_refs[k][0].astype(F32)
                for s in range(1, n):
                    g = g + g_refs[k][s].astype(F32)
                m2 = ADAM_B1 * m_ref[...] + (1.0 - ADAM_B1) * g
                v2 = ADAM_B2 * v_ref[...] + (1.0 - ADAM_B2) * (g * g)
                m_hat = m2 / (1.0 - ADAM_B1 ** ADAM_STEP)
                v_hat = v2 / (1.0 - ADAM_B2 ** ADAM_STEP)
                go_ref[...] = g
                d_ref[...] = -ADAM_LR * (m_hat / (jnp.sqrt(v_hat) + ADAM_EPS) + ADAM_WD * w_ref[...])
                mo_ref[...] = m2
                vo_ref[...] = v2

    g_specs = [pl.BlockSpec((n, tr, C), lambda l, i, k=k: (0, jnp.where(l == k, i, 0), 0)) for k in range(L)]
    blk = pl.BlockSpec((None, tr, C), lambda l, i: (l, i, 0))
    return pl.pallas_call(
        body, name=name, grid=(L, A // tr), in_specs=g_specs + [blk, blk, blk] + [_dep_spec(2)] * len(deps),
        out_specs=[blk] * 4, out_shape=[_sds((L, A, C), F32)] * 4, compiler_params=_params(("parallel", "parallel")),
    )(*gsrcs, w, m, v, *deps)


MESH = pl.DeviceIdType.MESH
HBM_SPEC = pl.BlockSpec(memory_space=pltpu.HBM)
N_PEERS = N_DEV - 1


def _mesh_place():
    x, y, c = lax.axis_index("x"), lax.axis_index("y"), lax.axis_index("c")
    peers = []
    for p in range(1, N_DEV):
        px = 1 - x if p & 4 else x
        py = 1 - y if p & 2 else y
        pc = 1 - c if p & 1 else c
        peers.append(((px, py, pc), 4 * px + 2 * py + pc))
    return 4 * x + 2 * y + c, peers


SEM_SPEC = pl.BlockSpec(memory_space=pltpu.SEMAPHORE)
ANY_SPEC = pl.BlockSpec(memory_space=pl.ANY)
EFFECT = pltpu.SideEffectType.DATAFLOW_SIDE_EFFECTING


def _scatters(scatter, k):
    return scatter if isinstance(scatter, bool) else scatter[k]


def _exchange_refs(scatter, src, land, send, recv, k, p, dev, idx, me):
    return pltpu.make_async_remote_copy(src_ref=src[k].at[idx] if _scatters(scatter, k) else src[k], dst_ref=land[k].at[me],
                                        send_sem=send.at[k * N_PEERS + p], recv_sem=recv.at[k * N_PEERS + p], device_id=dev,
                                        device_id_type=MESH)


def _exchange_start(name, srcs, scatter, gate):
    n = len(srcs)
    lands = [lax.empty(s.shape if _scatters(scatter, k) else (N_DEV,) + s.shape, s.dtype) for k, s in enumerate(srcs)]

    def body(*refs):
        src, land = refs[:n], refs[n:2 * n]
        send, recv, own = refs[2 * n + 1:2 * n + 4]
        token = refs[-1]
        me, peers = _mesh_place()
        for k in range(n):
            pltpu.make_async_copy(src[k].at[me] if _scatters(scatter, k) else src[k], land[k].at[me], own.at[k]).start()
            for p, (dev, idx) in enumerate(peers):
                _exchange_refs(scatter, src, land, send, recv, k, p, dev, idx, me).start()
        token[...] = jnp.zeros_like(token)

    hbm = lambda a: pltpu.HBM(a.shape, a.dtype)
    outs = pl.pallas_call(
        body, name=name,
        out_shape=(pltpu.SemaphoreType.DMA((n * N_PEERS,)), pltpu.SemaphoreType.DMA((n * N_PEERS,)),
                   pltpu.SemaphoreType.DMA((n,)), *[hbm(s) for s in srcs], *[hbm(s) for s in lands], _sds(DEP_SHAPE, F32)),
        in_specs=[HBM_SPEC] * (2 * n) + [ANY_SPEC],
        out_specs=(SEM_SPEC, SEM_SPEC, SEM_SPEC, *[HBM_SPEC] * (2 * n), pl.BlockSpec(memory_space=pltpu.VMEM)),
        input_output_aliases={j: 3 + j for j in range(2 * n)},
        compiler_params=pltpu.CompilerParams(has_side_effects=EFFECT),
    )(*[pltpu.with_memory_space_constraint(s, pltpu.HBM) for s in srcs],
      *[pltpu.with_memory_space_constraint(s, pltpu.HBM) for s in lands], gate)
    return outs[:3], None, list(outs[3:3 + n]), list(outs[3 + n:3 + 2 * n]), outs[-1]


def _exchange_wait(name, started, scatter, after):
    (send, recv, own), _, srcs, lands, _ = started
    n = len(srcs)

    def body(*refs):
        src, land = refs[:n], refs[n:2 * n]
        send, recv, own = refs[2 * n:2 * n + 3]
        me, peers = _mesh_place()
        for k in range(n):
            pltpu.make_async_copy(src[k].at[me] if _scatters(scatter, k) else src[k], land[k].at[me], own.at[k]).wait()
            for p, (dev, idx) in enumerate(peers):
                cp = pltpu.make_async_remote_copy(src_ref=src[k].at[idx] if _scatters(scatter, k) else src[k], dst_ref=land[k].at[idx],
                                                  send_sem=send.at[k * N_PEERS + p], recv_sem=recv.at[k * N_PEERS + p], device_id=dev,
                                                  device_id_type=MESH)
                cp.wait_send()
                cp.wait_recv()

    hbm = lambda a: pltpu.HBM(a.shape, a.dtype)
    outs = pl.pallas_call(
        body, name=name, out_shape=(*[hbm(s) for s in srcs], *[hbm(s) for s in lands]),
        in_specs=[HBM_SPEC] * (2 * n) + [SEM_SPEC, SEM_SPEC, SEM_SPEC, ANY_SPEC], out_specs=tuple([HBM_SPEC] * (2 * n)),
        input_output_aliases={j: j for j in range(2 * n)},
        compiler_params=pltpu.CompilerParams(has_side_effects=EFFECT),
    )(*srcs, *lands, send, recv, own, after)
    return list(outs[n:])


def _sum_devices(name, parts):
    def body(p_ref, o_ref):
        tot = p_ref[0]
        for j in range(1, N_DEV):
            tot = tot + p_ref[j]
        o_ref[...] = tot

    return pl.pallas_call(body, name=name, out_shape=_sds(parts.shape[1:], F32),
                          compiler_params=pltpu.CompilerParams(vmem_limit_bytes=VMEM_LIMIT))(parts)


def _rows(a, width=D_MODEL):
    flat = a.reshape(-1)
    return jnp.pad(flat, (0, (-flat.shape[0]) % width)).reshape(-1, width)


def _pack_rows(parts):
    blocks = []
    for p in parts:
        r = _rows(p)
        blocks.append(jnp.pad(r, ((0, (-r.shape[0]) % 8), (0, 0))))
    return jnp.concatenate(blocks, axis=0)


def _unpack_rows(rows, shapes):
    out, at = [], 0
    for s in shapes:
        size = int(np.prod(s))
        n = -(-size // D_MODEL)
        out.append(rows[at:at + n].reshape(-1)[:size].reshape(s))
        at += -(-n // 8) * 8
    return out


def kernel(x, norm_mix, norm_ffn, norm_final, attn_w_in, attn_w_out, attn_sinks, hgrn_w_in, hgrn_w_out, hgrn_norm, hgrn_lb_logits, ffn_w_up, ffn_conv_w, ffn_conv_b, ffn_w_down, loss_target, m_norm_mix, m_norm_ffn, m_norm_final, m_attn_w_in, m_attn_w_out, m_attn_sinks, m_hgrn_w_in, m_hgrn_w_out, m_hgrn_norm, m_hgrn_lb_logits, m_ffn_w_up, m_ffn_conv_w, m_ffn_conv_b, m_ffn_w_down, v_norm_mix, v_norm_ffn, v_norm_final, v_attn_w_in, v_attn_w_out, v_attn_sinks, v_hgrn_w_in, v_hgrn_w_out, v_hgrn_norm, v_hgrn_lb_logits, v_ffn_w_up, v_ffn_conv_w, v_ffn_conv_b, v_ffn_w_down):
    S = x.shape[1]
    n_attn, n_hgrn = attn_w_in.shape[0], hgrn_w_in.shape[0]

    wa_in_t, wa_out_b = attn_w_in.transpose(0, 2, 1).astype(BF16), attn_w_out.astype(BF16)
    wh_in_b, wh_out_b = hgrn_w_in.astype(BF16), hgrn_w_out.astype(BF16)
    wf_up_b, wf_down_b = ffn_w_up.transpose(0, 2, 1).astype(BF16), ffn_w_down.astype(BF16)
    conv_b = ffn_conv_b.reshape(DEPTH, 2, 4, 1, FF_SLOT)
    lb = _lb_fwd("lb_fwd", hgrn_lb_logits)

    def unit_shards(l, part):
        if part == "ffn":
            return [wf_up_b[l], wf_down_b[l], ffn_conv_w[l]]
        return [wa_in_t[l // 2], wa_out_b[l // 2]] if l % 2 == 0 else [wh_in_b[l // 2], wh_out_b[l // 2]]

    def unit_weights(l, part, w):
        if part == "ffn":
            return w[0][None], w[1].reshape(1, 4, FF_SLOT, D_MODEL), w[2].reshape(2, 4, 3, FF_SLOT)
        if l % 2 == 0:
            return w[0].reshape(1, ATTN_IN, D_MODEL), w[1].reshape(1, D_MODEL, D_MODEL)
        return w[0][None], w[1].reshape(1, D_MODEL, D_MODEL)

    units = [(l, part) for l in range(DEPTH) for part in ("mix", "ffn")]
    gathers = [_exchange_start("gather_start0", unit_shards(*units[0]), False, norm_final)]
    gathers.append(_exchange_start("gather_start1", unit_shards(*units[1]), False, gathers[0][4]))
    arrived = _exchange_wait("gather_wait0", gathers[0], False, gathers[1][4])
    weights, saved = {}, [dict() for _ in range(DEPTH)]
    h = x[0]
    hn = _rmsnorm_fwd("norm_mix_fwd0", h, norm_mix[0:1])
    for n, (l, part) in enumerate(units):
        i, sv = l // 2, saved[l]
        weights[l, part] = w = unit_weights(l, part, arrived)
        dep = None
        if n + 2 < len(units):
            gathers.append(_exchange_start(f"gather_start{n + 2}", unit_shards(*units[n + 2]), False, arrived[0]))
            dep = gathers[n + 2][4]
        if part == "mix":
            sv["h"], sv["hn"] = h, hn
            if l % 2 == 0:
                sv["proj"] = _proj_rows(f"attn_proj{i}", hn, w[0], 0, BF16, dep)
                sv["o"], *sv["kept"] = _attn_fwd(f"attn_fwd{i}", sv["proj"], attn_sinks[i:i + 1])
                h, hn = _out_proj(f"attn_out{i}", sv["o"], w[1], 0, h, norm_ffn[l:l + 1])
            else:
                sv["z"] = _proj_slots(f"hgrn_proj{i}", hn, w[0], 0, dep=dep).reshape(4, 2, S, HG_SLOT)
                sv["o"], *sv["kept"] = _hg_fwd(f"hgrn_fwd{i}", sv["z"], lb[i:i + 1], hgrn_norm[i:i + 1])
                h, hn = _out_proj(f"hgrn_out{i}", sv["o"], w[1], 0, h, norm_ffn[l:l + 1])
        else:
            sv["h2"], sv["hn2"] = h, hn
            sv["u"] = _proj_slots(f"ffn_up{l}", hn, w[0], 0, True, dep, BF16).reshape(2, 4, S, FF_SLOT)
            sv["a"], sv["convs"] = _convgate_fwd(f"ffn_gate{l}", sv["u"], w[2], conv_b[l])
            if l + 1 < DEPTH:
                h, hn = _down_proj(f"ffn_down{l}", sv["a"], w[1], 0, h, norm_mix[l + 1:l + 2])
            else:
                h = _down_proj(f"ffn_down{l}", sv["a"], w[1], 0, h)
        if n + 1 < len(units):
            arrived = _exchange_wait(f"gather_wait{n + 1}", gathers[n + 1], False, h)
    dh, d_norm_final, loss_rows = _loss_head("loss_head", h, norm_final[None], loss_target[0])

    d_conv_w, d_conv_b, d_norm_mix, d_norm_ffn = [None] * DEPTH, [None] * DEPTH, [None] * DEPTH, [None] * DEPTH
    d_sinks, d_lb, d_hgrn_norm = [None] * n_attn, [None] * n_hgrn, [None] * n_hgrn
    received, pending = {}, []
    for l, part in reversed(units):
        i, sv, w = l // 2, saved[l], weights[l, part]
        dep = pending[-1][1][4] if pending else None
        if part == "ffn":
            da = _dgrad_down(f"ffn_down_dgrad{l}", dh, w[1], 0, dep)
            g_down = _wgrad_down(f"ffn_down_wgrad{l}", sv["a"], dh).reshape(N_DEV, D_FF // N_DEV, D_MODEL)
            du, d_conv_w[l], d_conv_b[l] = _convgate_bwd(f"ffn_gate_bwd{l}", sv["u"], sv["convs"], w[2], da)
            du = du.reshape(N_DEV, S, FF_SLOT)
            grads = [_wgrad_slots(f"ffn_up_wgrad{l}", sv["hn2"], du, True), g_down]
            dh, d_norm_ffn[l] = _dgrad_slots(f"ffn_up_dgrad{l}", du, w[0], 0, (sv["h2"], norm_ffn[l:l + 1], dh), True)
        else:
            if l % 2 == 0:
                do = _dgrad_out(f"attn_out_dgrad{i}", dh, w[1], 0, BF16, dep)
                g_out = _wgrad_rows(f"attn_out_wgrad{i}", sv["o"], dh)
                dproj, d_sinks[i] = _attn_bwd(f"attn_bwd{i}", sv["proj"], *sv["kept"], do)
                g_in = _wgrad_rows(f"attn_proj_wgrad{i}", dproj, sv["hn"]).reshape(N_DEV, ATTN_IN // N_DEV, D_MODEL)
                dh_new = _dgrad_rows(f"attn_proj_dgrad{i}", dproj, w[0], 0, (sv["h"], norm_mix[l:l + 1], dh))
            else:
                dog = _dgrad_out(f"hgrn_out_dgrad{i}", dh, w[1], 0, F32, dep)
                g_out = _wgrad_rows(f"hgrn_out_wgrad{i}", sv["o"], dh)
                dz, d_lb[i], dng = _hg_bwd(f"hgrn_bwd{i}", sv["z"], lb[i:i + 1], hgrn_norm[i:i + 1], *sv["kept"], dog)
                d_hgrn_norm[i] = dng[0] + dng[1]
                dz = dz.reshape(N_DEV, S, HG_SLOT)
                g_in = _wgrad_slots(f"hgrn_proj_wgrad{i}", sv["hn"], dz)
                dh_new = _dgrad_slots(f"hgrn_proj_dgrad{i}", dz, w[0], 0, (sv["h"], norm_mix[l:l + 1], dh))
            grads = [g_in, g_out.reshape(N_DEV, D_MODEL // N_DEV, D_MODEL)]
            dh, d_norm_mix[l] = dh_new
        gate = dh
        if len(pending) == 2:
            key, oldest = pending.pop(0)
            received[key] = _exchange_wait(f"scatter_wait_{key[1]}{key[0]}", oldest, True, dh)
            gate = received[key][0]
        pending.append(((l, part), _exchange_start(f"scatter_start_{part}{l}", grads, True, gate)))
    grad_x = dh[None]

    small_shapes = [(DEPTH, D_MODEL), (DEPTH, D_MODEL), (1, D_MODEL), (1, D_MODEL), (n_hgrn, D_MODEL), (n_attn, 128),
                    (n_hgrn, HG_K), (DEPTH, 2 * D_FF)]
    partial = _pack_rows([
        jnp.concatenate(d_norm_mix), jnp.concatenate(d_norm_ffn), d_norm_final, loss_rows, jnp.concatenate(d_lb),
        jnp.concatenate(d_sinks), jnp.concatenate(d_hgrn_norm), jnp.stack(d_conv_b)])
    d_taps = jnp.stack(d_conv_w).reshape(DEPTH, N_DEV, 3, FF_SLOT).transpose(1, 0, 2, 3).reshape(N_DEV, DEPTH * 3, FF_SLOT)
    small_started = _exchange_start("small_start", [partial, d_taps], (False, True), pending[-1][1][4])
    attn_layers, hgrn_layers = range(0, DEPTH, 2), range(1, DEPTH, 2)

    def transposed(ts):
        return [t.transpose(0, 2, 1) for t in ts]

    big = {"hgrn_w_in": _adamw("adamw_hgrn_in", [received[l, "mix"][0] for l in hgrn_layers], hgrn_w_in, m_hgrn_w_in,
                               v_hgrn_w_in, dep=small_started[4])}
    big["hgrn_w_out"] = _adamw("adamw_hgrn_out", [received[l, "mix"][1] for l in hgrn_layers], hgrn_w_out, m_hgrn_w_out, v_hgrn_w_out)
    key, oldest = pending.pop(0)
    received[key] = _exchange_wait(f"scatter_wait_{key[1]}{key[0]}", oldest, True, big["hgrn_w_in"][3])
    up_t = _adamw("adamw_ffn_up", [received[l, "ffn"][0] for l in range(DEPTH)], *transposed((ffn_w_up, m_ffn_w_up, v_ffn_w_up)))
    big["ffn_w_up"] = transposed(up_t)
    big["ffn_w_down"] = _adamw("adamw_ffn_down", [received[l, "ffn"][1] for l in range(DEPTH)], ffn_w_down, m_ffn_w_down, v_ffn_w_down)
    key, oldest = pending.pop(0)
    received[key] = _exchange_wait(f"scatter_wait_{key[1]}{key[0]}", oldest, True, up_t[3])
    small_parts, taps_parts = _exchange_wait("small_wait", small_started, (False, True), up_t[3])
    total = _sum_devices("sum_small", small_parts)
    (g_norm_mix, g_norm_ffn, g_norm_final, loss_sum, g_lb, g_sinks, g_hgrn_norm, g_conv_b) = _unpack_rows(total, small_shapes)

    loss = jnp.sum(loss_sum)
    g_norm_final = g_norm_final[0]
    g_sinks = g_sinks[:, :N_Q_HEADS]
    g_lb_logits = _lb_bwd("lb_bwd", hgrn_lb_logits, g_lb)

    big.update({
        "attn_w_in": transposed(_adamw("adamw_attn_in", [received[l, "mix"][0] for l in attn_layers],
                                       *transposed((attn_w_in, m_attn_w_in, v_attn_w_in)))),
        "attn_w_out": _adamw("adamw_attn_out", [received[l, "mix"][1] for l in attn_layers], attn_w_out, m_attn_w_out, v_attn_w_out),
        "ffn_conv_w": _adamw("adamw_conv_w", [taps_parts[:, 3 * l:3 * l + 3] for l in range(DEPTH)], ffn_conv_w, m_ffn_conv_w,
                             v_ffn_conv_w),
    })
    small_w = [norm_mix, norm_ffn, norm_final, attn_sinks, hgrn_norm, hgrn_lb_logits, ffn_conv_b]
    small_m = [m_norm_mix, m_norm_ffn, m_norm_final, m_attn_sinks, m_hgrn_norm, m_hgrn_lb_logits, m_ffn_conv_b]
    small_v = [v_norm_mix, v_norm_ffn, v_norm_final, v_attn_sinks, v_hgrn_norm, v_hgrn_lb_logits, v_ffn_conv_b]
    small_g = [g_norm_mix, g_norm_ffn, g_norm_final, g_sinks, g_hgrn_norm, g_lb_logits, g_conv_b]
    outs = _adamw("adamw_small", [_pack_rows(small_g)[None]], *[_pack_rows(t)[None] for t in (small_w, small_m, small_v)])
    outs = [o[0] for o in outs]
    shapes = [w.shape for w in small_w]
    small = {n: [t[j] for t in [_unpack_rows(o, shapes) for o in outs]]
             for j, n in enumerate(["norm_mix", "norm_ffn", "norm_final", "attn_sinks", "hgrn_norm", "hgrn_lb_logits", "ffn_conv_b"])}
    order = ["norm_mix", "norm_ffn", "norm_final", "attn_w_in", "attn_w_out", "attn_sinks", "hgrn_w_in", "hgrn_w_out",
             "hgrn_norm", "hgrn_lb_logits", "ffn_w_up", "ffn_conv_w", "ffn_conv_b", "ffn_w_down"]
    res = {**big, **small}
    return (loss, grad_x, *[res[n][0] for n in order], *[res[n][1] for n in order], *[res[n][2] for n in order],
            *[res[n][3] for n in order])
```

```python
import numpy as np
import jax
import jax.numpy as jnp
from jax import lax
from jax.experimental import pallas as pl
from jax.experimental.pallas import tpu as pltpu

F32 = jnp.float32
BF16 = jnp.bfloat16

D_MODEL = 1024
DEPTH = 4
HEAD_DIM = 64
N_Q_HEADS = 16
N_KV_HEADS = 4
Q_PER_KV = 4
ATTN_BLOCK = 128
ATTN_IN = 1536
HG_HEADS = 8
HG_K = 128
HG_CHUNK = 64
HG_IN = 4096
D_FF = 2816
EPS = 1e-6
N_DEV = 8
FF_SLOT = 2 * D_FF // N_DEV
HG_SLOT = HG_IN // N_DEV
HG_LEVELS = 6

ADAM_LR = 0.001
ADAM_B1 = 0.9
ADAM_B2 = 0.999
ADAM_EPS = 1e-08
ADAM_WD = 0.01
ADAM_STEP = 10

VMEM_LIMIT = 56 * 1024 * 1024
ROW_TILE = 1024
WIDE_ROW_TILE = 2048
NEG_BIG = -1e30

NN = (((1,), (0,)), ((), ()))
NT = (((1,), (1,)), ((), ()))
TN = (((0,), (0,)), ((), ()))


def _bdot(a, b, dn):
    return lax.dot_general(a.astype(BF16), b.astype(BF16), dn, preferred_element_type=F32)


def _sds(shape, dtype):
    return jax.ShapeDtypeStruct(tuple(shape), dtype)


def _params(sem):
    return pltpu.CompilerParams(dimension_semantics=sem, vmem_limit_bytes=VMEM_LIMIT)


DEP_SHAPE = (8, 128)


def _dep_spec(rank):
    return pl.BlockSpec(DEP_SHAPE, lambda *_: (0, 0))


def _matmul(name, a, b, *, dn, grid, a_spec, b_spec, o_spec, out_shape, acc_shape=None, extra=(), extra_specs=(),
            finish=None, dep=None, sem=("parallel", "parallel", "arbitrary")):
    nk = grid[2]
    many = isinstance(out_shape, (list, tuple))
    n_in = 2 + len(extra) + (dep is not None)
    n_out = len(out_shape) if many else 1

    def body(*refs):
        a_ref, b_ref = refs[0], refs[1]
        outs = refs[n_in:n_in + n_out]

        def prod():
            return _bdot(a_ref[...], b_ref[...], dn)

        def done(v):
            if finish is None:
                outs[0][...] = v.astype(outs[0].dtype)
            else:
                finish(v, refs[2:2 + len(extra)], outs)

        if nk == 1:
            done(prod())
        else:
            acc = refs[-1]
            k = pl.program_id(2)

            @pl.when(k == 0)
            def _():
                acc[...] = prod()

            @pl.when(k > 0)
            def _():
                acc[...] += prod()

            @pl.when(k == nk - 1)
            def _():
                done(acc[...])

    in_specs = [a_spec, b_spec, *extra_specs] + ([_dep_spec(3)] if dep is not None else [])
    args = (a, b, *extra) + ((dep,) if dep is not None else ())
    scratch = [] if nk == 1 else [pltpu.VMEM(acc_shape, F32)]
    return pl.pallas_call(
        body, name=name, grid=grid, in_specs=in_specs, out_specs=o_spec, out_shape=out_shape,
        scratch_shapes=scratch, compiler_params=_params(sem),
    )(*args)


def _rms(x):
    return lax.rsqrt(jnp.mean(x * x, axis=-1, keepdims=True) + EPS)


def _residual_finish(v, ex, outs):
    h = v + ex[0][...]
    outs[0][...] = h
    if len(ex) > 1:
        outs[1][...] = (h * _rms(h) * ex[1][...]).astype(outs[1].dtype)


def _norm_bwd_finish(v, ex, outs):
    x = ex[0][...]
    r = _rms(x)
    xh = x * r
    dyg = v * ex[1][...]
    dh = ex[2][...] + r * (dyg - xh * jnp.mean(dyg * xh, axis=-1, keepdims=True))
    outs[0][...] = dh
    outs[2][...] = dh.astype(outs[2].dtype)
    part = jnp.sum(v * xh, axis=0, keepdims=True)

    @pl.when(pl.program_id(0) == 0)
    def _():
        outs[1][...] = part

    @pl.when(pl.program_id(0) > 0)
    def _():
        outs[1][...] += part


def _row_io(tm, norm_g):
    row = pl.BlockSpec((tm, D_MODEL), lambda i, j, k: (i, 0))
    vec = pl.BlockSpec((1, D_MODEL), lambda i, j, k: (0, 0))
    if norm_g is None:
        return (row,), row, lambda S: _sds((S, D_MODEL), F32)
    return (row, vec), [row, row], lambda S: [_sds((S, D_MODEL), F32), _sds((S, D_MODEL), BF16)]


def _tile(n, t):
    return min(n, t)


def _proj_rows(name, hn, wt, l, out_dtype, dep=None):
    S, N = hn.shape[0], wt.shape[1]
    tm, tn = _tile(S, ROW_TILE), 512
    return _matmul(
        name, hn, wt, dn=NT, grid=(S // tm, N // tn, 1),
        a_spec=pl.BlockSpec((tm, D_MODEL), lambda i, j, k: (i, 0)),
        b_spec=pl.BlockSpec((None, tn, D_MODEL), lambda i, j, k: (l, j, 0)),
        o_spec=pl.BlockSpec((tm, tn), lambda i, j, k: (i, j)),
        out_shape=_sds((S, N), out_dtype), dep=dep)


def _slot_weight(w, transposed):
    if transposed:
        return w.shape[2], (None, None, w.shape[2], D_MODEL), NT, NN
    return w.shape[3], (None, None, D_MODEL, w.shape[3]), NN, NT


def _proj_slots(name, hn, w, l, transposed=False, dep=None, out_dtype=F32):
    S = hn.shape[0]
    r, blk, dn, _ = _slot_weight(w, transposed)
    tm = _tile(S, WIDE_ROW_TILE)
    return _matmul(
        name, hn, w, dn=dn, grid=(N_DEV, S // tm, 1),
        a_spec=pl.BlockSpec((tm, D_MODEL), lambda j, i, k: (i, 0)),
        b_spec=pl.BlockSpec(blk, lambda j, i, k: (l, j, 0, 0)),
        o_spec=pl.BlockSpec((None, tm, r), lambda j, i, k: (j, i, 0)),
        out_shape=_sds((N_DEV, S, r), out_dtype), dep=dep)


def _out_proj(name, o, w, l, h, norm_g=None):
    S, K = o.shape
    tm = _tile(S, ROW_TILE)
    extra_specs, o_spec, out_shape = _row_io(tm, norm_g)
    return _matmul(
        name, o, w, dn=NN, grid=(S // tm, 1, 1),
        a_spec=pl.BlockSpec((tm, K), lambda i, j, k: (i, 0)),
        b_spec=pl.BlockSpec((None, K, D_MODEL), lambda i, j, k: (l, 0, 0)),
        o_spec=o_spec, out_shape=out_shape(S), extra=(h,) if norm_g is None else (h, norm_g),
        extra_specs=extra_specs, finish=_residual_finish)


def _down_proj(name, a, w, l, h, norm_g=None):
    nj, S, r = a.shape
    tm = _tile(S, ROW_TILE)
    extra_specs, o_spec, out_shape = _row_io(tm, norm_g)
    return _matmul(
        name, a, w, dn=NN, grid=(S // tm, 1, nj),
        a_spec=pl.BlockSpec((None, tm, r), lambda i, j, k: (k, i, 0)),
        b_spec=pl.BlockSpec((None, None, r, D_MODEL), lambda i, j, k: (l, k, 0, 0)),
        o_spec=o_spec, out_shape=out_shape(S), acc_shape=(tm, D_MODEL),
        extra=(h,) if norm_g is None else (h, norm_g), extra_specs=extra_specs, finish=_residual_finish)


def _dgrad_down(name, dh, w, l, dep=None):
    S = dh.shape[0]
    nj, r = w.shape[1], w.shape[2]
    tm = _tile(S, ROW_TILE)
    return _matmul(
        name, dh, w, dn=NT, grid=(nj, S // tm, 1),
        a_spec=pl.BlockSpec((tm, D_MODEL), lambda j, i, k: (i, 0)),
        b_spec=pl.BlockSpec((None, None, r, D_MODEL), lambda j, i, k: (l, j, 0, 0)),
        o_spec=pl.BlockSpec((None, tm, r), lambda j, i, k: (j, i, 0)),
        out_shape=_sds((nj, S, r), BF16), dep=dep)


def _wgrad_down(name, a, dh):
    nj, S, r = a.shape
    tk = _tile(S, ROW_TILE)
    return _matmul(
        name, a, dh, dn=TN, grid=(nj, 1, S // tk),
        a_spec=pl.BlockSpec((None, tk, r), lambda s, j, k: (s, k, 0)),
        b_spec=pl.BlockSpec((tk, D_MODEL), lambda s, j, k: (k, 0)),
        o_spec=pl.BlockSpec((None, r, D_MODEL), lambda s, j, k: (s, 0, 0)),
        out_shape=_sds((nj, r, D_MODEL), BF16), acc_shape=(r, D_MODEL))


def _norm_bwd_io(tm, S):
    row = pl.BlockSpec((tm, D_MODEL), lambda i, j, k: (i, 0))
    vec = pl.BlockSpec((1, D_MODEL), lambda i, j, k: (0, 0))
    return dict(extra_specs=(row, vec, row), o_spec=[row, vec, row],
                out_shape=[_sds((S, D_MODEL), F32), _sds((1, D_MODEL), F32), _sds((S, D_MODEL), BF16)],
                finish=_norm_bwd_finish, sem=("arbitrary", "arbitrary", "arbitrary"))


def _dgrad_slots(name, dz, w, l, norm, transposed=False):
    nj, S, r = dz.shape
    _, blk, _, dn = _slot_weight(w, transposed)
    tm = _tile(S, ROW_TILE)
    return _matmul(
        name, dz, w, dn=dn, grid=(S // tm, 1, nj),
        a_spec=pl.BlockSpec((None, tm, r), lambda i, j, k: (k, i, 0)),
        b_spec=pl.BlockSpec(blk, lambda i, j, k: (l, k, 0, 0)),
        acc_shape=(tm, D_MODEL), extra=norm, **_norm_bwd_io(tm, S))


def _wgrad_slots(name, hn, dz, transposed=False):
    nj, S, r = dz.shape
    tk = _tile(S, ROW_TILE)
    hn_spec = pl.BlockSpec((tk, D_MODEL), lambda s, j, k: (k, 0))
    dz_spec = pl.BlockSpec((None, tk, r), lambda s, j, k: (s, k, 0))
    if transposed:
        return _matmul(
            name, dz, hn, dn=TN, grid=(nj, 1, S // tk), a_spec=dz_spec, b_spec=hn_spec,
            o_spec=pl.BlockSpec((None, r, D_MODEL), lambda s, j, k: (s, 0, 0)),
            out_shape=_sds((nj, r, D_MODEL), BF16), acc_shape=(r, D_MODEL))
    return _matmul(
        name, hn, dz, dn=TN, grid=(nj, 1, S // tk), a_spec=hn_spec, b_spec=dz_spec,
        o_spec=pl.BlockSpec((None, D_MODEL, r), lambda s, j, k: (s, 0, 0)),
        out_shape=_sds((nj, D_MODEL, r), BF16), acc_shape=(D_MODEL, r))


def _dgrad_out(name, dh, w, l, out_dtype, dep=None):
    S, K = dh.shape[0], w.shape[1]
    tm = _tile(S, ROW_TILE)
    return _matmul(
        name, dh, w, dn=NT, grid=(S // tm, 1, 1),
        a_spec=pl.BlockSpec((tm, D_MODEL), lambda i, j, k: (i, 0)),
        b_spec=pl.BlockSpec((None, K, D_MODEL), lambda i, j, k: (l, 0, 0)),
        o_spec=pl.BlockSpec((tm, K), lambda i, j, k: (i, 0)),
        out_shape=_sds((S, K), out_dtype), dep=dep)


def _wgrad_rows(name, a, b):
    S, K = a.shape
    tk = _tile(S, ROW_TILE)
    return _matmul(
        name, a, b, dn=TN, grid=(1, 1, S // tk),
        a_spec=pl.BlockSpec((tk, K), lambda i, j, k: (k, 0)),
        b_spec=pl.BlockSpec((tk, D_MODEL), lambda i, j, k: (k, 0)),
        o_spec=pl.BlockSpec((K, D_MODEL), lambda i, j, k: (0, 0)),
        out_shape=_sds((K, D_MODEL), BF16), acc_shape=(K, D_MODEL))


def _dgrad_rows(name, dz, wt, l, norm):
    S, N = dz.shape
    tm = _tile(S, ROW_TILE)
    return _matmul(
        name, dz, wt, dn=NN, grid=(S // tm, 1, 1),
        a_spec=pl.BlockSpec((tm, N), lambda i, j, k: (i, 0)),
        b_spec=pl.BlockSpec((None, N, D_MODEL), lambda i, j, k: (l, 0, 0)),
        extra=norm, **_norm_bwd_io(tm, S))


def _rmsnorm_fwd(name, h, g):
    S = h.shape[0]
    tm = _tile(S, ROW_TILE)

    def body(h_ref, g_ref, o_ref):
        x = h_ref[...]
        o_ref[...] = (x * _rms(x) * g_ref[...]).astype(o_ref.dtype)

    row = pl.BlockSpec((tm, D_MODEL), lambda i: (i, 0))
    return pl.pallas_call(
        body, name=name, grid=(S // tm,), in_specs=[row, pl.BlockSpec((1, D_MODEL), lambda i: (0, 0))],
        out_specs=row, out_shape=_sds((S, D_MODEL), BF16), compiler_params=_params(("parallel",)),
    )(h, g)


def _loss_head(name, h, g, target):
    S = h.shape[0]
    tm = _tile(S, ROW_TILE)

    def body(h_ref, g_ref, t_ref, dh_ref, dg_ref, ls_ref, dhb_ref):
        x = h_ref[...]
        r = _rms(x)
        xh = x * r
        diff = xh * g_ref[...] - t_ref[...]
        dyf = diff * (1.0 / D_MODEL)
        dyg = dyf * g_ref[...]
        dh = r * (dyg - xh * jnp.mean(dyg * xh, axis=-1, keepdims=True))
        dh_ref[...] = dh
        dhb_ref[...] = dh.astype(dhb_ref.dtype)
        part = jnp.sum(dyf * xh, axis=0, keepdims=True)
        lpart = jnp.sum(diff * diff, axis=0, keepdims=True) * (0.5 / D_MODEL)

        @pl.when(pl.program_id(0) == 0)
        def _():
            dg_ref[...] = part
            ls_ref[...] = lpart

        @pl.when(pl.program_id(0) > 0)
        def _():
            dg_ref[...] += part
            ls_ref[...] += lpart

    row = pl.BlockSpec((tm, D_MODEL), lambda i: (i, 0))
    vec = pl.BlockSpec((1, D_MODEL), lambda i: (0, 0))
    return pl.pallas_call(
        body, name=name, grid=(S // tm,), in_specs=[row, vec, row], out_specs=[row, vec, vec, row],
        out_shape=[_sds((S, D_MODEL), F32), _sds((1, D_MODEL), F32), _sds((1, D_MODEL), F32), _sds((S, D_MODEL), BF16)],
        compiler_params=_params(("arbitrary",)),
    )(h, g, target)


ATTN_SCALE = HEAD_DIM ** -0.5
ALIBI_SLOPES = [2.0 ** (-8.0 * (h + 1) / N_Q_HEADS) for h in range(N_Q_HEADS)]
K_COL = N_Q_HEADS * HEAD_DIM
KV_COLS = N_KV_HEADS * HEAD_DIM
V_COL = K_COL + KV_COLS


def _attn_masks(n):
    qi = lax.broadcasted_iota(jnp.int32, (ATTN_BLOCK, ATTN_BLOCK), 0)
    ki = lax.broadcasted_iota(jnp.int32, (ATTN_BLOCK, ATTN_BLOCK), 1)
    dist_c = (qi - ki).astype(F32)
    return dist_c + float(ATTN_BLOCK), dist_c, (ki > qi) & (n > 0), qi >= ki


def _attn_probs(raw_p, raw_c, sink, slope, masks):
    dist_p, dist_c, valid_p, valid_c = masks
    sp = jnp.where(valid_p, raw_p * ATTN_SCALE - slope * dist_p, NEG_BIG)
    sc = jnp.where(valid_c, raw_c * ATTN_SCALE - slope * dist_c, NEG_BIG)
    m = jnp.maximum(jnp.maximum(jnp.max(sp, axis=-1, keepdims=True), jnp.max(sc, axis=-1, keepdims=True)), sink)
    ep, ec, es = jnp.exp(sp - m), jnp.exp(sc - m), jnp.exp(sink - m)
    inv = 1.0 / (jnp.sum(ep, axis=-1, keepdims=True) + jnp.sum(ec, axis=-1, keepdims=True) + es)
    return ep * inv, ec * inv, es * inv


def _group_rows(ref, m):
    return jnp.concatenate([ref[:, HEAD_DIM * (Q_PER_KV * m + g):HEAD_DIM * (Q_PER_KV * m + g + 1)]
                            for g in range(Q_PER_KV)], axis=0)


def _head_rows(x, g):
    return x[ATTN_BLOCK * g:ATTN_BLOCK * (g + 1)]


def _attn_specs(nblk):
    last = nblk - 1
    kcol, vcol = K_COL // KV_COLS, V_COL // KV_COLS
    return [
        pl.BlockSpec((ATTN_BLOCK, K_COL), lambda n: (jnp.minimum(n, last), 0)),
        pl.BlockSpec((ATTN_BLOCK, KV_COLS), lambda n: (jnp.minimum(n, last), kcol)),
        pl.BlockSpec((ATTN_BLOCK, KV_COLS), lambda n: (jnp.maximum(jnp.minimum(n, last) - 1, 0), kcol)),
        pl.BlockSpec((ATTN_BLOCK, KV_COLS), lambda n: (jnp.minimum(n, last), vcol)),
        pl.BlockSpec((ATTN_BLOCK, KV_COLS), lambda n: (jnp.maximum(jnp.minimum(n, last) - 1, 0), vcol)),
    ]


P_COLS = 2 * ATTN_BLOCK


def _attn_fwd(name, proj, sinks):
    S = proj.shape[0]
    nblk = S // ATTN_BLOCK

    def body(q_ref, kc_ref, kp_ref, vc_ref, vp_ref, sk_ref, o_ref, p_ref, ps_ref):
        masks = _attn_masks(pl.program_id(0))
        lane = lax.broadcasted_iota(jnp.int32, (ATTN_BLOCK, 128), 1)
        sink_p = jnp.zeros((ATTN_BLOCK, 128), F32)
        for m in range(N_KV_HEADS):
            ks = slice(HEAD_DIM * m, HEAD_DIM * (m + 1))
            kp, kc, vp, vc = kp_ref[:, ks], kc_ref[:, ks], vp_ref[:, ks], vc_ref[:, ks]
            q4 = _group_rows(q_ref, m)
            raw_p, raw_c = _bdot(q4, kp, NT), _bdot(q4, kc, NT)
            pps, pcs = [], []
            for g in range(Q_PER_KV):
                hh = Q_PER_KV * m + g
                pp, pc, ps = _attn_probs(_head_rows(raw_p, g), _head_rows(raw_c, g), sk_ref[0, hh], ALIBI_SLOPES[hh], masks)
                pps.append(pp.astype(BF16))
                pcs.append(pc.astype(BF16))
                p_ref[:, P_COLS * hh:P_COLS * hh + ATTN_BLOCK] = pps[g]
                p_ref[:, P_COLS * hh + ATTN_BLOCK:P_COLS * (hh + 1)] = pcs[g]
                sink_p = jnp.where(lane == hh, ps, sink_p)
            o4 = _bdot(jnp.concatenate(pps, axis=0), vp, NN) + _bdot(jnp.concatenate(pcs, axis=0), vc, NN)
            for g in range(Q_PER_KV):
                hh = Q_PER_KV * m + g
                o_ref[:, HEAD_DIM * hh:HEAD_DIM * (hh + 1)] = _head_rows(o4, g).astype(o_ref.dtype)
        ps_ref[...] = sink_p

    row = lambda cols: pl.BlockSpec((ATTN_BLOCK, cols), lambda n: (n, 0))
    return pl.pallas_call(
        body, name=name, grid=(nblk,),
        in_specs=_attn_specs(nblk) + [pl.BlockSpec(memory_space=pltpu.SMEM)],
        out_specs=[row(K_COL), row(N_Q_HEADS * P_COLS), row(128)],
        out_shape=[_sds((S, K_COL), BF16), _sds((S, N_Q_HEADS * P_COLS), BF16), _sds((S, 128), F32)],
        compiler_params=_params(("parallel",)),
    )(proj, proj, proj, proj, proj, sinks)


def _attn_bwd(name, proj, probs, sink_probs, do):
    S = proj.shape[0]
    nblk = S // ATTN_BLOCK

    def body(q_ref, kc_ref, kp_ref, vc_ref, vp_ref, do_ref, p_ref, ps_ref, dz_ref, ds_ref, carry, cur, padd):
        n = pl.program_id(0)

        @pl.when(n == 0)
        def _():
            carry[...] = jnp.zeros_like(carry)
            ds_ref[...] = jnp.zeros_like(ds_ref)

        @pl.when(n < nblk)
        def _():
            lane = lax.broadcasted_iota(jnp.int32, (ATTN_BLOCK, 128), 1)
            sink_p = ps_ref[...]
            dsv = jnp.zeros((1, 128), F32)
            for m in range(N_KV_HEADS):
                ks = slice(HEAD_DIM * m, HEAD_DIM * (m + 1))
                kp, kc, vp, vc = kp_ref[:, ks], kc_ref[:, ks], vp_ref[:, ks], vc_ref[:, ks]
                q4, do4 = _group_rows(q_ref, m), _group_rows(do_ref, m)
                dpp4, dpc4 = _bdot(do4, vp, NT), _bdot(do4, vc, NT)
                pps, pcs, dsps, dscs = [], [], [], []
                for g in range(Q_PER_KV):
                    hh = Q_PER_KV * m + g
                    pps.append(p_ref[:, P_COLS * hh:P_COLS * hh + ATTN_BLOCK])
                    pcs.append(p_ref[:, P_COLS * hh + ATTN_BLOCK:P_COLS * (hh + 1)])
                    pp, pc = pps[g].astype(F32), pcs[g].astype(F32)
                    dpp, dpc = _head_rows(dpp4, g), _head_rows(dpc4, g)
                    delta = jnp.sum(pp * dpp, axis=-1, keepdims=True) + jnp.sum(pc * dpc, axis=-1, keepdims=True)
                    dsv = dsv - jnp.sum(jnp.where(lane == hh, sink_p, 0.0) * delta, axis=0, keepdims=True)
                    dsps.append((pp * (dpp - delta)).astype(BF16))
                    dscs.append((pc * (dpc - delta)).astype(BF16))
                pp4, pc4 = jnp.concatenate(pps, axis=0), jnp.concatenate(pcs, axis=0)
                dsp4, dsc4 = jnp.concatenate(dsps, axis=0), jnp.concatenate(dscs, axis=0)
                dq4 = (_bdot(dsp4, kp, NN) + _bdot(dsc4, kc, NN)) * ATTN_SCALE
                for g in range(Q_PER_KV):
                    hh = Q_PER_KV * m + g
                    cur[:, HEAD_DIM * hh:HEAD_DIM * (hh + 1)] = _head_rows(dq4, g)
                cur[:, K_COL + HEAD_DIM * m:K_COL + HEAD_DIM * (m + 1)] = _bdot(dsc4, q4, TN) * ATTN_SCALE
                cur[:, V_COL + HEAD_DIM * m:V_COL + HEAD_DIM * (m + 1)] = _bdot(pc4, do4, TN)
                padd[:, ks] = _bdot(dsp4, q4, TN) * ATTN_SCALE
                padd[:, KV_COLS + HEAD_DIM * m:KV_COLS + HEAD_DIM * (m + 1)] = _bdot(pp4, do4, TN)
            ds_ref[...] += dsv
            dz_ref[:, :K_COL] = carry[:, :K_COL].astype(dz_ref.dtype)
            dz_ref[:, K_COL:] = (carry[:, K_COL:] + padd[...]).astype(dz_ref.dtype)
            carry[...] = cur[...]

        @pl.when(n == nblk)
        def _():
            dz_ref[...] = carry[...].astype(dz_ref.dtype)

    return pl.pallas_call(
        body, name=name, grid=(nblk + 1,),
        in_specs=_attn_specs(nblk) + [
            pl.BlockSpec((ATTN_BLOCK, cols), lambda n: (jnp.minimum(n, nblk - 1), 0))
            for cols in (K_COL, N_Q_HEADS * P_COLS, 128)],
        out_specs=[pl.BlockSpec((ATTN_BLOCK, ATTN_IN), lambda n: (jnp.maximum(n - 1, 0), 0)),
                   pl.BlockSpec((1, 128), lambda n: (0, 0))],
        out_shape=[_sds((S, ATTN_IN), BF16), _sds((1, 128), F32)],
        scratch_shapes=[pltpu.VMEM((ATTN_BLOCK, ATTN_IN), F32), pltpu.VMEM((ATTN_BLOCK, ATTN_IN), F32),
                        pltpu.VMEM((ATTN_BLOCK, 2 * KV_COLS), F32)],
        compiler_params=_params(("arbitrary",)),
    )(proj, proj, proj, proj, proj, do, probs, sink_probs)


def _hg_consts():
    C = HG_CHUNK
    tri = np.tril(np.ones((C, C)))
    t = np.arange(C)
    rows, masks = [tri], []
    for lvl in range(HG_LEVELS):
        n = C >> (lvl + 1)
        sel = np.zeros((C, C))
        sel[t, (t // (2 * n)) * (2 * n) + n - 1] = 1.0
        rows.append(sel @ tri)
        tt, ss = t[:, None], t[None, :]
        masks.append((tt // (2 * n) == ss // (2 * n)) & ((tt // n) % 2 == 1) & ((ss // n) % 2 == 0))
    masks.append(np.eye(C, dtype=bool))
    stk = np.concatenate(rows, axis=0)
    return jnp.asarray(stk, BF16), jnp.asarray(np.stack(masks), F32)


def _sigmoid(x):
    return 1.0 / (1.0 + jnp.exp(-x))


def _silu_sigmoid(x):
    return 0.5 + 0.5 * jnp.tanh(0.5 * x)


def _split(x, parts):
    out, rest = [], x
    for _ in range(parts):
        out.append(rest.astype(BF16))
        rest = rest - out[-1].astype(F32)
    return out


def _dot01(m01, x, dn, parts=3):
    return sum(lax.dot_general(m01, p, dn, preferred_element_type=F32) for p in _split(x, parts))


def _ref_rows(b, n):
    C = b.shape[1]
    if 2 * n >= 8:
        b3 = b.reshape(HG_CHUNK // (2 * n), 2 * n, C)
        return jnp.broadcast_to(b3[:, n - 1:n, :], b3.shape).reshape(HG_CHUNK, C)
    pos = lax.broadcasted_iota(jnp.int32, b.shape, 0) % (2 * n)
    out = b
    for p in range(2 * n):
        if p != n - 1:
            out = jnp.where(pos == p, pltpu.roll(b, (p - (n - 1)) % HG_CHUNK, 0), out)
    return out


HG_STEP_CHUNKS = 4


def _chunk_rows(ci):
    return pl.ds(pl.multiple_of(ci * HG_CHUNK, HG_CHUNK), HG_CHUNK)


def _hg_common(z_ref, rows, lb_ref, stk_ref):
    qr, fr = z_ref[0, rows, :], z_ref[1, rows, :]
    lb = lb_ref[...]
    sq, sg, sgn = _silu_sigmoid(qr), _sigmoid(fr), _sigmoid(-fr)
    ft = lb + (1.0 - lb) * sg
    b = _dot01(stk_ref[0:HG_CHUNK, :], jnp.log(ft), NN)
    ws = [jnp.exp(-jnp.abs(b - _ref_rows(b, HG_CHUNK >> (l + 1)))) for l in range(HG_LEVELS)]
    blast = b[HG_CHUNK - 1:HG_CHUNK]
    return dict(qr=qr, fr=fr, lb=lb, sq=sq, sg=sg, sgn=sgn, ft=ft, q=qr * sq, kk=(1.0 - lb) * sgn, b=b,
                ws=ws, eb=jnp.exp(b), ed=jnp.exp(blast - b), elast=jnp.exp(blast))


def _hg_factors(qh, kh, ws, sl):
    return ([(qh * ws[l][:, sl]).astype(BF16) for l in range(HG_LEVELS)],
            [(kh * ws[l][:, sl]).astype(BF16) for l in range(HG_LEVELS)])


def _hg_intra(qh, kh, ws, msk_ref, sl):
    qls, kls = _hg_factors(qh, kh, ws, sl)
    a = msk_ref[HG_LEVELS] * _bdot(qh, kh, NT)
    for l in range(HG_LEVELS):
        a = a + msk_ref[l] * _bdot(qls[l], kls[l], NT)
    return a


def _hg_fwd(name, z, lb, ng):
    S = z.shape[2]
    nc = S // HG_CHUNK
    per = min(HG_STEP_CHUNKS, nc)
    stk, msk = _hg_consts()

    def body(z_ref, lb_ref, ng_ref, stk_ref, msk_ref, og_ref, st_ref, a_ref, o_ref, state):
        @pl.when(pl.program_id(1) == 0)
        def _():
            state[...] = jnp.zeros_like(state)

        def chunk(ci, _):
            rows = _chunk_rows(ci)
            cm = _hg_common(z_ref, rows, lb_ref, stk_ref)
            v, gt = z_ref[2, rows, :], z_ref[3, rows, :]
            kd = cm["kk"] * cm["ed"]
            for hh in range(4):
                sl = slice(HG_K * hh, HG_K * (hh + 1))
                st = state[hh]
                st_ref[ci, hh] = st
                qh, kh, vh = cm["q"][:, sl], cm["kk"][:, sl], v[:, sl]
                a = _hg_intra(qh, kh, cm["ws"], msk_ref, sl).astype(BF16)
                a_ref[ci, hh] = a
                o = _bdot(a, vh, NN) + _bdot(qh * cm["eb"][:, sl], st, NT)
                o_ref[rows, sl] = o
                state[hh] = cm["elast"][:, sl] * st + _bdot(vh, kd[:, sl], TN)
                gh = gt[:, sl]
                og_ref[rows, sl] = (o * _rms(o) * ng_ref[...] * (gh * _silu_sigmoid(gh))).astype(og_ref.dtype)
            return 0

        lax.fori_loop(0, per, chunk, 0, unroll=True)

    return pl.pallas_call(
        body, name=name, grid=(2, nc // per),
        in_specs=[pl.BlockSpec((4, None, per * HG_CHUNK, HG_SLOT), lambda g, c: (0, g, c, 0)),
                  pl.BlockSpec((1, HG_SLOT), lambda g, c: (0, g)),
                  pl.BlockSpec((1, HG_K), lambda g, c: (0, 0)),
                  pl.BlockSpec(stk.shape, lambda g, c: (0, 0)),
                  pl.BlockSpec(msk.shape, lambda g, c: (0, 0, 0))],
        out_specs=[pl.BlockSpec((per * HG_CHUNK, HG_SLOT), lambda g, c: (c, g)),
                   pl.BlockSpec((per, 4, HG_K, HG_K), lambda g, c: (c, g, 0, 0)),
                   pl.BlockSpec((per, 4, HG_CHUNK, HG_CHUNK), lambda g, c: (c, g, 0, 0)),
                   pl.BlockSpec((per * HG_CHUNK, HG_SLOT), lambda g, c: (c, g))],
        out_shape=[_sds((S, D_MODEL), BF16), _sds((nc, HG_HEADS, HG_K, HG_K), F32),
                   _sds((nc, HG_HEADS, HG_CHUNK, HG_CHUNK), BF16), _sds((S, D_MODEL), F32)],
        scratch_shapes=[pltpu.VMEM((4, HG_K, HG_K), F32)],
        compiler_params=_params(("parallel", "arbitrary")),
    )(z, lb, ng, stk, msk)


def _hg_bwd(name, z, lb, ng, states, intra, o_pre, dog):
    S = z.shape[2]
    nc = S // HG_CHUNK
    per = min(HG_STEP_CHUNKS, nc)
    stk, msk = _hg_consts()

    def body(z_ref, lb_ref, ng_ref, stk_ref, msk_ref, st_ref, a_ref, o_ref, dog_ref, dz_ref, dlb_ref, dng_ref, dstate):
        @pl.when(pl.program_id(1) == 0)
        def _():
            dstate[...] = jnp.zeros_like(dstate)
            dlb_ref[...] = jnp.zeros_like(dlb_ref)
            dng_ref[...] = jnp.zeros_like(dng_ref)

        def chunk(k, _):
            ci = per - 1 - k
            rows = _chunk_rows(ci)
            cm = _hg_common(z_ref, rows, lb_ref, stk_ref)
            v, gt = z_ref[2, rows, :], z_ref[3, rows, :]
            ng = ng_ref[...]
            kd = cm["kk"] * cm["ed"]
            row = lax.broadcasted_iota(jnp.int32, (HG_CHUNK, 1), 0)
            dng = jnp.zeros((1, HG_K), F32)
            dq_h, dkk_h, db_h, dv_h, dgt_h = [], [], [], [], []
            dr_h = [[] for _ in range(HG_LEVELS)]
            for hh in range(4):
                sl = slice(HG_K * hh, HG_K * (hh + 1))
                st, dst = st_ref[ci, hh], dstate[hh]
                qh, kh, vh, ebh, edh, kdh = cm["q"][:, sl], cm["kk"][:, sl], v[:, sl], cm["eb"][:, sl], cm["ed"][:, sl], kd[:, sl]
                elh = cm["elast"][:, sl]
                qls, kls = _hg_factors(qh, kh, cm["ws"], sl)
                a, o = a_ref[ci, hh], o_ref[rows, sl]
                qe = qh * ebh
                r = _rms(o)
                xh = o * r
                gh = gt[:, sl]
                sgg = _silu_sigmoid(gh)
                dog = dog_ref[rows, sl].astype(F32)
                dy = dog * (gh * sgg)
                dgt_h.append(dog * (xh * ng) * (sgg * (1.0 + gh * (1.0 - sgg))))
                dng = dng + jnp.sum(dy * xh, axis=0, keepdims=True)
                dyg = dy * ng
                do = r * (dyg - xh * jnp.mean(dyg * xh, axis=-1, keepdims=True))
                da = _bdot(do, vh, NT)
                dv_h.append(_bdot(a, do, TN) + _bdot(kdh, dst, NT))
                dkd = _bdot(vh, dst, NN)
                delast = jnp.sum(st * dst, axis=0, keepdims=True)
                dqe = _bdot(do, st, NN)
                dstate[hh] = elh * dst + _bdot(do, qe, TN)
                gk = dkd * kdh
                dblast = jnp.sum(gk, axis=0, keepdims=True) + delast * elh
                db = dqe * qe - gk + jnp.where(row == HG_CHUNK - 1, dblast, 0.0)
                dp = (msk_ref[HG_LEVELS] * da).astype(BF16)
                dq = dqe * ebh + _bdot(dp, kh, NN)
                dkk = dkd * edh + _bdot(dp, qh, TN)
                for l in range(HG_LEVELS):
                    dp = (msk_ref[l] * da).astype(BF16)
                    dql, dkl = _bdot(dp, kls[l], NN), _bdot(dp, qls[l], TN)
                    w = cm["ws"][l][:, sl]
                    dq = dq + dql * w
                    dkk = dkk + dkl * w
                    half = jnp.where(((row >> (HG_LEVELS - 1 - l)) & 1) == 1, 1.0, -1.0)
                    dd = half * w * (dql * qh + dkl * kh)
                    db = db + dd
                    dr_h[l].append(-dd)
                dq_h.append(dq)
                dkk_h.append(dkk)
                db_h.append(db)
            cat = lambda xs: jnp.concatenate(xs, axis=1)
            cot = jnp.concatenate([cat(db_h)] + [cat(dr_h[l]) for l in range(HG_LEVELS)], axis=0)
            dlf = _dot01(stk_ref[...], cot, TN, parts=2)
            dq, dkk = cat(dq_h), cat(dkk_h)
            dft = dlf / cm["ft"]
            one_lb = 1.0 - cm["lb"]
            dz_ref[0, rows, :] = (dq * (cm["sq"] * (1.0 + cm["qr"] * (1.0 - cm["sq"])))).astype(dz_ref.dtype)
            dz_ref[1, rows, :] = ((dft - dkk) * one_lb * cm["sg"] * cm["sgn"]).astype(dz_ref.dtype)
            dz_ref[2, rows, :] = cat(dv_h).astype(dz_ref.dtype)
            dz_ref[3, rows, :] = cat(dgt_h).astype(dz_ref.dtype)
            dlb_ref[...] += jnp.sum((dft - dkk) * cm["sgn"], axis=0, keepdims=True)
            dng_ref[...] += dng
            return 0

        lax.fori_loop(0, per, chunk, 0, unroll=True)

    rev = lambda c: nc // per - 1 - c
    rows_blk = pl.BlockSpec((per * HG_CHUNK, HG_SLOT), lambda g, c: (rev(c), g))
    return pl.pallas_call(
        body, name=name, grid=(2, nc // per),
        in_specs=[pl.BlockSpec((4, None, per * HG_CHUNK, HG_SLOT), lambda g, c: (0, g, rev(c), 0)),
                  pl.BlockSpec((1, HG_SLOT), lambda g, c: (0, g)),
                  pl.BlockSpec((1, HG_K), lambda g, c: (0, 0)),
                  pl.BlockSpec(stk.shape, lambda g, c: (0, 0)),
                  pl.BlockSpec(msk.shape, lambda g, c: (0, 0, 0)),
                  pl.BlockSpec((per, 4, HG_K, HG_K), lambda g, c: (rev(c), g, 0, 0)),
                  pl.BlockSpec((per, 4, HG_CHUNK, HG_CHUNK), lambda g, c: (rev(c), g, 0, 0)),
                  rows_blk, rows_blk],
        out_specs=[pl.BlockSpec((4, None, per * HG_CHUNK, HG_SLOT), lambda g, c: (0, g, rev(c), 0)),
                   pl.BlockSpec((1, HG_SLOT), lambda g, c: (0, g)),
                   pl.BlockSpec((None, 1, HG_K), lambda g, c: (g, 0, 0))],
        out_shape=[_sds(z.shape, BF16), _sds((1, 2 * HG_SLOT), F32), _sds((2, 1, HG_K), F32)],
        scratch_shapes=[pltpu.VMEM((4, HG_K, HG_K), F32)],
        compiler_params=_params(("parallel", "arbitrary")),
    )(z, lb, ng, stk, msk, states, intra, o_pre, dog)


def _lb_fwd(name, logits):
    def body(l_ref, o_ref):
        x = l_ref[...]
        e = jnp.exp(x - jnp.max(x, axis=0, keepdims=True))
        s = e / jnp.sum(e, axis=0, keepdims=True)
        o_ref[0:1, :] = s[1:2]
        o_ref[1:2, :] = s[1:2] + s[2:3] + s[3:4]

    return pl.pallas_call(body, name=name, out_shape=_sds((2, logits.shape[1]), F32))(logits)


def _lb_bwd(name, logits, dlb):
    def body(l_ref, d_ref, o_ref):
        x = l_ref[...]
        e = jnp.exp(x - jnp.max(x, axis=0, keepdims=True))
        s = e / jnp.sum(e, axis=0, keepdims=True)
        d1, d3 = d_ref[0:1, :], d_ref[1:2, :]
        ds = [jnp.zeros_like(d1), d1 + d3, d3, d3]
        dot = sum(ds[r] * s[r:r + 1] for r in range(1, DEPTH))
        for r in range(DEPTH):
            o_ref[r:r + 1, :] = s[r:r + 1] * (ds[r] - dot)

    return pl.pallas_call(body, name=name, out_shape=_sds(logits.shape, F32))(logits, dlb)


SUB = 8


def _rows_down(x, prev, k):
    row = lax.broadcasted_iota(jnp.int32, x.shape, 0)
    return jnp.where(row >= k, pltpu.roll(x, k, 0), pltpu.roll(prev, k, 0))


def _rows_up(x, nxt, k):
    row = lax.broadcasted_iota(jnp.int32, x.shape, 0)
    return jnp.where(row < SUB - k, pltpu.roll(x, SUB - k, 0), pltpu.roll(nxt, SUB - k, 0))


def _conv_block(w_ref, b_ref, p, x, prev):
    return (b_ref[p] + w_ref[p, 0:1, :] * _rows_down(x, prev, 2) + w_ref[p, 1:2, :] * _rows_down(x, prev, 1)
            + w_ref[p, 2:3, :] * x)


def _convgate_fwd(name, u, cw, cb):
    S = u.shape[2]
    tm = _tile(S, ROW_TILE)

    def body(u_ref, w_ref, b_ref, a_ref, c_ref, halo):
        @pl.when(pl.program_id(1) == 0)
        def _():
            halo[...] = jnp.zeros_like(halo)

        def step(r, prev):
            pg, pv = prev
            out, cgs, cvs = [], [], []
            rows = pl.ds(pl.multiple_of(r * 2 * SUB, 2 * SUB), 2 * SUB)
            ug16, uv16 = u_ref[0, rows, :].astype(F32), u_ref[1, rows, :].astype(F32)
            for s in range(2):
                xg, xv = ug16[s * SUB:(s + 1) * SUB], uv16[s * SUB:(s + 1) * SUB]
                cgs.append(_conv_block(w_ref, b_ref, 0, xg, pg))
                cvs.append(_conv_block(w_ref, b_ref, 1, xv, pv))
                out.append(cgs[s] * _silu_sigmoid(cgs[s]) * cvs[s])
                pg, pv = xg, xv
            a_ref[rows, :] = jnp.concatenate(out, axis=0).astype(a_ref.dtype)
            c_ref[0, rows, :] = jnp.concatenate(cgs, axis=0).astype(c_ref.dtype)
            c_ref[1, rows, :] = jnp.concatenate(cvs, axis=0).astype(c_ref.dtype)
            return pg, pv

        pg, pv = lax.fori_loop(0, tm // (2 * SUB), step, (halo[0], halo[1]), unroll=2)
        halo[0] = pg
        halo[1] = pv

    pair = pl.BlockSpec((2, None, tm, FF_SLOT), lambda j, t: (0, j, t, 0))
    return pl.pallas_call(
        body, name=name, grid=(4, S // tm),
        in_specs=[pair, pl.BlockSpec((2, None, 3, FF_SLOT), lambda j, t: (0, j, 0, 0)),
                  pl.BlockSpec((2, None, 1, FF_SLOT), lambda j, t: (0, j, 0, 0))],
        out_specs=[pl.BlockSpec((None, tm, FF_SLOT), lambda j, t: (j, t, 0)), pair],
        out_shape=[_sds((4, S, FF_SLOT), BF16), _sds(u.shape, BF16)],
        scratch_shapes=[pltpu.VMEM((2, SUB, FF_SLOT), F32)],
        compiler_params=_params(("parallel", "arbitrary")),
    )(u, cw, cb)


def _convgate_bwd(name, u, convs, cw, da):
    S = u.shape[2]
    tm = _tile(S, ROW_TILE)
    nt = S // tm

    def body(u_ref, c_ref, w_ref, da_ref, du_out, dw_ref, db_ref, after, first, acc, du_ref):
        @pl.when(pl.program_id(1) == 0)
        def _():
            after[...] = jnp.zeros_like(after)
            acc[...] = jnp.zeros_like(acc)

        def finish(p, x, d, nxt, rows):
            taps = (_rows_up(d, nxt, 2), _rows_up(d, nxt, 1), d)
            du_ref[p, rows, :] = w_ref[p, 0:1, :] * taps[0] + w_ref[p, 1:2, :] * taps[1] + w_ref[p, 2:3, :] * d
            for j in range(3):
                acc[p, j] += taps[j] * x
            acc[p, 3] += d

        def step(r, carry):
            xg_last, xv_last, dg_last, dv_last = carry
            rows16 = pl.ds(pl.multiple_of(r * 2 * SUB, 2 * SUB), 2 * SUB)
            dav = da_ref[rows16, :].astype(F32)
            cg16, cv16 = c_ref[0, rows16, :].astype(F32), c_ref[1, rows16, :].astype(F32)
            ug16, uv16 = u_ref[0, rows16, :].astype(F32), u_ref[1, rows16, :].astype(F32)
            for s in range(2):
                at = r * 2 * SUB + s * SUB
                part = slice(s * SUB, (s + 1) * SUB)
                cg, cv, dab = cg16[part], cv16[part], dav[part]
                sg = _silu_sigmoid(cg)
                dg = dab * cv * (sg * (1.0 + cg * (1.0 - sg)))
                dv = dab * cg * sg
                before = pl.ds(pl.multiple_of(at - SUB, SUB), SUB)
                if s == 0:
                    @pl.when(r == 0)
                    def _():
                        first[0] = dg
                        first[1] = dv

                    @pl.when(r > 0)
                    def _():
                        finish(0, xg_last, dg_last, dg, before)
                        finish(1, xv_last, dv_last, dv, before)
                else:
                    finish(0, xg_last, dg_last, dg, before)
                    finish(1, xv_last, dv_last, dv, before)
                xg_last, xv_last, dg_last, dv_last = ug16[part], uv16[part], dg, dv
            return xg_last, xv_last, dg_last, dv_last

        zero = jnp.zeros((SUB, FF_SLOT), F32)
        xg_last, xv_last, dg_last, dv_last = lax.fori_loop(0, tm // (2 * SUB), step, (zero, zero, zero, zero))
        finish(0, xg_last, dg_last, after[0], slice(tm - SUB, tm))
        finish(1, xv_last, dv_last, after[1], slice(tm - SUB, tm))
        du_out[...] = du_ref[...].astype(du_out.dtype)
        after[...] = first[...]
        for p in range(2):
            for j in range(3):
                dw_ref[p, j:j + 1, :] = jnp.sum(acc[p, j], axis=0, keepdims=True)
            db_ref[p] = jnp.sum(acc[p, 3], axis=0, keepdims=True)

    rev = lambda t: nt - 1 - t
    pair = pl.BlockSpec((2, None, tm, FF_SLOT), lambda j, t: (0, j, rev(t), 0))
    taps = pl.BlockSpec((2, None, 3, FF_SLOT), lambda j, t: (0, j, 0, 0))
    bias = pl.BlockSpec((2, None, 1, FF_SLOT), lambda j, t: (0, j, 0, 0))
    return pl.pallas_call(
        body, name=name, grid=(4, nt),
        in_specs=[pair, pair, taps, pl.BlockSpec((None, tm, FF_SLOT), lambda j, t: (j, rev(t), 0))],
        out_specs=[pair, taps, bias],
        out_shape=[_sds(u.shape, BF16), _sds(cw.shape, F32), _sds((2, 4, 1, FF_SLOT), F32)],
        scratch_shapes=[pltpu.VMEM((2, SUB, FF_SLOT), F32), pltpu.VMEM((2, SUB, FF_SLOT), F32),
                        pltpu.VMEM((2, 4, SUB, FF_SLOT), F32), pltpu.VMEM((2, tm, FF_SLOT), F32)],
        compiler_params=_params(("parallel", "arbitrary")),
    )(u, convs, cw, da)


def _row_tile(R):
    for t in range(256, 15, -16):
        if R % t == 0:
            return t
    return R


def _adamw(name, gsrcs, w, m, v, dep=None):
    L = len(gsrcs)
    n, A, C = gsrcs[0].shape
    tr = _row_tile(A)
    deps = () if dep is None else (dep,)

    def body(*refs):
        g_refs = refs[:L]
        w_ref, m_ref, v_ref = refs[L:L + 3]
        go_ref, d_ref, mo_ref, vo_ref = refs[L + 3 + len(deps):]
        for k in range(L):
            @pl.when(pl.program_id(0) == k)
            def _(k=k):
                g = g_refs[k][0].astype(F32)
                for s in range(1, n):
                    g = g + g_refs[k][s].astype(F32)
                m2 = ADAM_B1 * m_ref[...] + (1.0 - ADAM_B1) * g
                v2 = ADAM_B2 * v_ref[...] + (1.0 - ADAM_B2) * (g * g)
                m_hat = m2 / (1.0 - ADAM_B1 ** ADAM_STEP)
                v_hat = v2 / (1.0 - ADAM_B2 ** ADAM_STEP)
                go_ref[...] = g
                d_ref[...] = -ADAM_LR * (m_hat / (jnp.sqrt(v_hat) + ADAM_EPS) + ADAM_WD * w_ref[...])
                mo_ref[...] = m2
                vo_ref[...] = v2

    g_specs = [pl.BlockSpec((n, tr, C), lambda l, i, k=k: (0, jnp.where(l == k, i, 0), 0)) for k in range(L)]
    blk = pl.BlockSpec((None, tr, C), lambda l, i: (l, i, 0))
    return pl.pallas_call(
        body, name=name, grid=(L, A // tr), in_specs=g_specs + [blk, blk, blk] + [_dep_spec(2)] * len(deps),
        out_specs=[blk] * 4, out_shape=[_sds((L, A, C), F32)] * 4, compiler_params=_params(("parallel", "parallel")),
    )(*gsrcs, w, m, v, *deps)


MESH = pl.DeviceIdType.MESH
HBM_SPEC = pl.BlockSpec(memory_space=pltpu.HBM)
N_PEERS = N_DEV - 1


def _mesh_place():
    x, y, c = lax.axis_index("x"), lax.axis_index("y"), lax.axis_index("c")
    peers = []
    for p in range(1, N_DEV):
        px = 1 - x if p & 4 else x
        py = 1 - y if p & 2 else y
        pc = 1 - c if p & 1 else c
        peers.append(((px, py, pc), 4 * px + 2 * py + pc))
    return 4 * x + 2 * y + c, peers


SEM_SPEC = pl.BlockSpec(memory_space=pltpu.SEMAPHORE)
ANY_SPEC = pl.BlockSpec(memory_space=pl.ANY)
EFFECT = pltpu.SideEffectType.DATAFLOW_SIDE_EFFECTING


def _scatters(scatter, k):
    return scatter if isinstance(scatter, bool) else scatter[k]


def _exchange_refs(scatter, src, land, send, recv, k, p, dev, idx, me):
    return pltpu.make_async_remote_copy(src_ref=src[k].at[idx] if _scatters(scatter, k) else src[k], dst_ref=land[k].at[me],
                                        send_sem=send.at[k * N_PEERS + p], recv_sem=recv.at[k * N_PEERS + p], device_id=dev,
                                        device_id_type=MESH)


def _exchange_start(name, srcs, scatter, gate):
    n = len(srcs)
    lands = [lax.empty(s.shape if _scatters(scatter, k) else (N_DEV,) + s.shape, s.dtype) for k, s in enumerate(srcs)]

    def body(*refs):
        src, land = refs[:n], refs[n:2 * n]
        send, recv, own = refs[2 * n + 1:2 * n + 4]
        token = refs[-1]
        me, peers = _mesh_place()
        for k in range(n):
            pltpu.make_async_copy(src[k].at[me] if _scatters(scatter, k) else src[k], land[k].at[me], own.at[k]).start()
            for p, (dev, idx) in enumerate(peers):
                _exchange_refs(scatter, src, land, send, recv, k, p, dev, idx, me).start()
        token[...] = jnp.zeros_like(token)

    hbm = lambda a: pltpu.HBM(a.shape, a.dtype)
    outs = pl.pallas_call(
        body, name=name,
        out_shape=(pltpu.SemaphoreType.DMA((n * N_PEERS,)), pltpu.SemaphoreType.DMA((n * N_PEERS,)),
                   pltpu.SemaphoreType.DMA((n,)), *[hbm(s) for s in srcs], *[hbm(s) for s in lands], _sds(DEP_SHAPE, F32)),
        in_specs=[HBM_SPEC] * (2 * n) + [ANY_SPEC],
        out_specs=(SEM_SPEC, SEM_SPEC, SEM_SPEC, *[HBM_SPEC] * (2 * n), pl.BlockSpec(memory_space=pltpu.VMEM)),
        input_output_aliases={j: 3 + j for j in range(2 * n)},
        compiler_params=pltpu.CompilerParams(has_side_effects=EFFECT),
    )(*[pltpu.with_memory_space_constraint(s, pltpu.HBM) for s in srcs],
      *[pltpu.with_memory_space_constraint(s, pltpu.HBM) for s in lands], gate)
    return outs[:3], None, list(outs[3:3 + n]), list(outs[3 + n:3 + 2 * n]), outs[-1]


def _exchange_wait(name, started, scatter, after):
    (send, recv, own), _, srcs, lands, _ = started
    n = len(srcs)

    def body(*refs):
        src, land = refs[:n], refs[n:2 * n]
        send, recv, own = refs[2 * n:2 * n + 3]
        me, peers = _mesh_place()
        for k in range(n):
            pltpu.make_async_copy(src[k].at[me] if _scatters(scatter, k) else src[k], land[k].at[me], own.at[k]).wait()
            for p, (dev, idx) in enumerate(peers):
                cp = pltpu.make_async_remote_copy(src_ref=src[k].at[idx] if _scatters(scatter, k) else src[k], dst_ref=land[k].at[idx],
                                                  send_sem=send.at[k * N_PEERS + p], recv_sem=recv.at[k * N_PEERS + p], device_id=dev,
                                                  device_id_type=MESH)
                cp.wait_send()
                cp.wait_recv()

    hbm = lambda a: pltpu.HBM(a.shape, a.dtype)
    outs = pl.pallas_call(
        body, name=name, out_shape=(*[hbm(s) for s in srcs], *[hbm(s) for s in lands]),
        in_specs=[HBM_SPEC] * (2 * n) + [SEM_SPEC, SEM_SPEC, SEM_SPEC, ANY_SPEC], out_specs=tuple([HBM_SPEC] * (2 * n)),
        input_output_aliases={j: j for j in range(2 * n)},
        compiler_params=pltpu.CompilerParams(has_side_effects=EFFECT),
    )(*srcs, *lands, send, recv, own, after)
    return list(outs[n:])


def _sum_devices(name, parts):
    def body(p_ref, o_ref):
        tot = p_ref[0]
        for j in range(1, N_DEV):
            tot = tot + p_ref[j]
        o_ref[...] = tot

    return pl.pallas_call(body, name=name, out_shape=_sds(parts.shape[1:], F32),
                          compiler_params=pltpu.CompilerParams(vmem_limit_bytes=VMEM_LIMIT))(parts)


def _rows(a, width=D_MODEL):
    flat = a.reshape(-1)
    return jnp.pad(flat, (0, (-flat.shape[0]) % width)).reshape(-1, width)


def _pack_rows(parts):
    blocks = []
    for p in parts:
        r = _rows(p)
        blocks.append(jnp.pad(r, ((0, (-r.shape[0]) % 8), (0, 0))))
    return jnp.concatenate(blocks, axis=0)


def _unpack_rows(rows, shapes):
    out, at = [], 0
    for s in shapes:
        size = int(np.prod(s))
        n = -(-size // D_MODEL)
        out.append(rows[at:at + n].reshape(-1)[:size].reshape(s))
        at += -(-n // 8) * 8
    return out


def kernel(x, norm_mix, norm_ffn, norm_final, attn_w_in, attn_w_out, attn_sinks, hgrn_w_in, hgrn_w_out, hgrn_norm, hgrn_lb_logits, ffn_w_up, ffn_conv_w, ffn_conv_b, ffn_w_down, loss_target, m_norm_mix, m_norm_ffn, m_norm_final, m_attn_w_in, m_attn_w_out, m_attn_sinks, m_hgrn_w_in, m_hgrn_w_out, m_hgrn_norm, m_hgrn_lb_logits, m_ffn_w_up, m_ffn_conv_w, m_ffn_conv_b, m_ffn_w_down, v_norm_mix, v_norm_ffn, v_norm_final, v_attn_w_in, v_attn_w_out, v_attn_sinks, v_hgrn_w_in, v_hgrn_w_out, v_hgrn_norm, v_hgrn_lb_logits, v_ffn_w_up, v_ffn_conv_w, v_ffn_conv_b, v_ffn_w_down):
    S = x.shape[1]
    n_attn, n_hgrn = attn_w_in.shape[0], hgrn_w_in.shape[0]

    wa_in_t, wa_out_b = attn_w_in.transpose(0, 2, 1).astype(BF16), attn_w_out.astype(BF16)
    wh_in_b, wh_out_b = hgrn_w_in.astype(BF16), hgrn_w_out.astype(BF16)
    wf_up_b, wf_down_b = ffn_w_up.transpose(0, 2, 1).astype(BF16), ffn_w_down.astype(BF16)
    conv_b = ffn_conv_b.reshape(DEPTH, 2, 4, 1, FF_SLOT)
    lb = _lb_fwd("lb_fwd", hgrn_lb_logits)

    def unit_shards(l, part):
        if part == "ffn":
            return [wf_up_b[l], wf_down_b[l], ffn_conv_w[l]]
        return [wa_in_t[l // 2], wa_out_b[l // 2]] if l % 2 == 0 else [wh_in_b[l // 2], wh_out_b[l // 2]]

    def unit_weights(l, part, w):
        if part == "ffn":
            return w[0][None], w[1].reshape(1, 4, FF_SLOT, D_MODEL), w[2].reshape(2, 4, 3, FF_SLOT)
        if l % 2 == 0:
            return w[0].reshape(1, ATTN_IN, D_MODEL), w[1].reshape(1, D_MODEL, D_MODEL)
        return w[0][None], w[1].reshape(1, D_MODEL, D_MODEL)

    units = [(l, part) for l in range(DEPTH) for part in ("mix", "ffn")]
    gathers = [_exchange_start("gather_start0", unit_shards(*units[0]), False, norm_final)]
    gathers.append(_exchange_start("gather_start1", unit_shards(*units[1]), False, gathers[0][4]))
    arrived = _exchange_wait("gather_wait0", gathers[0], False, gathers[1][4])
    weights, saved = {}, [dict() for _ in range(DEPTH)]
    h = x[0]
    hn = _rmsnorm_fwd("norm_mix_fwd0", h, norm_mix[0:1])
    for n, (l, part) in enumerate(units):
        i, sv = l // 2, saved[l]
        weights[l, part] = w = unit_weights(l, part, arrived)
        dep = None
        if n + 2 < len(units):
            gathers.append(_exchange_start(f"gather_start{n + 2}", unit_shards(*units[n + 2]), False, arrived[0]))
            dep = gathers[n + 2][4]
        if part == "mix":
            sv["h"], sv["hn"] = h, hn
            if l % 2 == 0:
                sv["proj"] = _proj_rows(f"attn_proj{i}", hn, w[0], 0, BF16, dep)
                sv["o"], *sv["kept"] = _attn_fwd(f"attn_fwd{i}", sv["proj"], attn_sinks[i:i + 1])
                h, hn = _out_proj(f"attn_out{i}", sv["o"], w[1], 0, h, norm_ffn[l:l + 1])
            else:
                sv["z"] = _proj_slots(f"hgrn_proj{i}", hn, w[0], 0, dep=dep).reshape(4, 2, S, HG_SLOT)
                sv["o"], *sv["kept"] = _hg_fwd(f"hgrn_fwd{i}", sv["z"], lb[i:i + 1], hgrn_norm[i:i + 1])
                h, hn = _out_proj(f"hgrn_out{i}", sv["o"], w[1], 0, h, norm_ffn[l:l + 1])
        else:
            sv["h2"], sv["hn2"] = h, hn
            sv["u"] = _proj_slots(f"ffn_up{l}", hn, w[0], 0, True, dep, BF16).reshape(2, 4, S, FF_SLOT)
            sv["a"], sv["convs"] = _convgate_fwd(f"ffn_gate{l}", sv["u"], w[2], conv_b[l])
            if l + 1 < DEPTH:
                h, hn = _down_proj(f"ffn_down{l}", sv["a"], w[1], 0, h, norm_mix[l + 1:l + 2])
            else:
                h = _down_proj(f"ffn_down{l}", sv["a"], w[1], 0, h)
        if n + 1 < len(units):
            arrived = _exchange_wait(f"gather_wait{n + 1}", gathers[n + 1], False, h)
    dh, d_norm_final, loss_rows, dhb = _loss_head("loss_head", h, norm_final[None], loss_target[0])

    d_conv_w, d_conv_b, d_norm_mix, d_norm_ffn = [None] * DEPTH, [None] * DEPTH, [None] * DEPTH, [None] * DEPTH
    d_sinks, d_lb, d_hgrn_norm = [None] * n_attn, [None] * n_hgrn, [None] * n_hgrn
    received, pending = {}, []
    for l, part in reversed(units):
        i, sv, w = l // 2, saved[l], weights[l, part]
        dep = pending[-1][1][4] if pending else None
        if part == "ffn":
            da = _dgrad_down(f"ffn_down_dgrad{l}", dhb, w[1], 0, dep)
            g_down = _wgrad_down(f"ffn_down_wgrad{l}", sv["a"], dhb).reshape(N_DEV, D_FF // N_DEV, D_MODEL)
            du, d_conv_w[l], d_conv_b[l] = _convgate_bwd(f"ffn_gate_bwd{l}", sv["u"], sv["convs"], w[2], da)
            du = du.reshape(N_DEV, S, FF_SLOT)
            grads = [_wgrad_slots(f"ffn_up_wgrad{l}", sv["hn2"], du, True), g_down]
            dh, d_norm_ffn[l], dhb = _dgrad_slots(f"ffn_up_dgrad{l}", du, w[0], 0, (sv["h2"], norm_ffn[l:l + 1], dh), True)
        else:
            if l % 2 == 0:
                do = _dgrad_out(f"attn_out_dgrad{i}", dhb, w[1], 0, BF16, dep)
                g_out = _wgrad_rows(f"attn_out_wgrad{i}", sv["o"], dhb)
                dproj, d_sinks[i] = _attn_bwd(f"attn_bwd{i}", sv["proj"], *sv["kept"], do)
                g_in = _wgrad_rows(f"attn_proj_wgrad{i}", dproj, sv["hn"]).reshape(N_DEV, ATTN_IN // N_DEV, D_MODEL)
                dh_new = _dgrad_rows(f"attn_proj_dgrad{i}", dproj, w[0], 0, (sv["h"], norm_mix[l:l + 1], dh))
            else:
                dog = _dgrad_out(f"hgrn_out_dgrad{i}", dhb, w[1], 0, F32, dep)
                g_out = _wgrad_rows(f"hgrn_out_wgrad{i}", sv["o"], dhb)
                dz, d_lb[i], dng = _hg_bwd(f"hgrn_bwd{i}", sv["z"], lb[i:i + 1], hgrn_norm[i:i + 1], *sv["kept"], dog)
                d_hgrn_norm[i] = dng[0] + dng[1]
                dz = dz.reshape(N_DEV, S, HG_SLOT)
                g_in = _wgrad_slots(f"hgrn_proj_wgrad{i}", sv["hn"], dz)
                dh_new = _dgrad_slots(f"hgrn_proj_dgrad{i}", dz, w[0], 0, (sv["h"], norm_mix[l:l + 1], dh))
            grads = [g_in, g_out.reshape(N_DEV, D_MODEL // N_DEV, D_MODEL)]
            dh, d_norm_mix[l], dhb = dh_new
        gate = dh
        if len(pending) == 2:
            key, oldest = pending.pop(0)
            received[key] = _exchange_wait(f"scatter_wait_{key[1]}{key[0]}", oldest, True, dh)
            gate = received[key][0]
        pending.append(((l, part), _exchange_start(f"scatter_start_{part}{l}", grads, True, gate)))
    grad_x = dh[None]

    small_shapes = [(DEPTH, D_MODEL), (DEPTH, D_MODEL), (1, D_MODEL), (1, D_MODEL), (n_hgrn, D_MODEL), (n_attn, 128),
                    (n_hgrn, HG_K), (DEPTH, 2 * D_FF)]
    partial = _pack_rows([
        jnp.concatenate(d_norm_mix), jnp.concatenate(d_norm_ffn), d_norm_final, loss_rows, jnp.concatenate(d_lb),
        jnp.concatenate(d_sinks), jnp.concatenate(d_hgrn_norm), jnp.stack(d_conv_b)])
    d_taps = jnp.stack(d_conv_w).reshape(DEPTH, N_DEV, 3, FF_SLOT).transpose(1, 0, 2, 3).reshape(N_DEV, DEPTH * 3, FF_SLOT)
    small_started = _exchange_start("small_start", [partial, d_taps], (False, True), pending[-1][1][4])
    attn_layers, hgrn_layers = range(0, DEPTH, 2), range(1, DEPTH, 2)

    def transposed(ts):
        return [t.transpose(0, 2, 1) for t in ts]

    big = {"hgrn_w_in": _adamw("adamw_hgrn_in", [received[l, "mix"][0] for l in hgrn_layers], hgrn_w_in, m_hgrn_w_in,
                               v_hgrn_w_in, dep=small_started[4])}
    big["hgrn_w_out"] = _adamw("adamw_hgrn_out", [received[l, "mix"][1] for l in hgrn_layers], hgrn_w_out, m_hgrn_w_out, v_hgrn_w_out)
    key, oldest = pending.pop(0)
    received[key] = _exchange_wait(f"scatter_wait_{key[1]}{key[0]}", oldest, True, big["hgrn_w_in"][3])
    up_t = _adamw("adamw_ffn_up", [received[l, "ffn"][0] for l in range(DEPTH)], *transposed((ffn_w_up, m_ffn_w_up, v_ffn_w_up)))
    big["ffn_w_up"] = transposed(up_t)
    big["ffn_w_down"] = _adamw("adamw_ffn_down", [received[l, "ffn"][1] for l in range(DEPTH)], ffn_w_down, m_ffn_w_down, v_ffn_w_down)
    key, oldest = pending.pop(0)
    received[key] = _exchange_wait(f"scatter_wait_{key[1]}{key[0]}", oldest, True, up_t[3])
    small_parts, taps_parts = _exchange_wait("small_wait", small_started, (False, True), up_t[3])
    total = _sum_devices("sum_small", small_parts)
    (g_norm_mix, g_norm_ffn, g_norm_final, loss_sum, g_lb, g_sinks, g_hgrn_norm, g_conv_b) = _unpack_rows(total, small_shapes)

    loss = jnp.sum(loss_sum)
    g_norm_final = g_norm_final[0]
    g_sinks = g_sinks[:, :N_Q_HEADS]
    g_lb_logits = _lb_bwd("lb_bwd", hgrn_lb_logits, g_lb)

    big.update({
        "attn_w_in": transposed(_adamw("adamw_attn_in", [received[l, "mix"][0] for l in attn_layers],
                                       *transposed((attn_w_in, m_attn_w_in, v_attn_w_in)))),
        "attn_w_out": _adamw("adamw_attn_out", [received[l, "mix"][1] for l in attn_layers], attn_w_out, m_attn_w_out, v_attn_w_out),
        "ffn_conv_w": _adamw("adamw_conv_w", [taps_parts[:, 3 * l:3 * l + 3] for l in range(DEPTH)], ffn_conv_w, m_ffn_conv_w,
                             v_ffn_conv_w),
    })
    small_w = [norm_mix, norm_ffn, norm_final, attn_sinks, hgrn_norm, hgrn_lb_logits, ffn_conv_b]
    small_m = [m_norm_mix, m_norm_ffn, m_norm_final, m_attn_sinks, m_hgrn_norm, m_hgrn_lb_logits, m_ffn_conv_b]
    small_v = [v_norm_mix, v_norm_ffn, v_norm_final, v_attn_sinks, v_hgrn_norm, v_hgrn_lb_logits, v_ffn_conv_b]
    small_g = [g_norm_mix, g_norm_ffn, g_norm_final, g_sinks, g_hgrn_norm, g_lb_logits, g_conv_b]
    outs = _adamw("adamw_small", [_pack_rows(small_g)[None]], *[_pack_rows(t)[None] for t in (small_w, small_m, small_v)])
    outs = [o[0] for o in outs]
    shapes = [w.shape for w in small_w]
    small = {n: [t[j] for t in [_unpack_rows(o, shapes) for o in outs]]
             for j, n in enumerate(["norm_mix", "norm_ffn", "norm_final", "attn_sinks", "hgrn_norm", "hgrn_lb_logits", "ffn_conv_b"])}
    order = ["norm_mix", "norm_ffn", "norm_final", "attn_w_in", "attn_w_out", "attn_sinks", "hgrn_w_in", "hgrn_w_out",
             "hgrn_norm", "hgrn_lb_logits", "ffn_w_up", "ffn_conv_w", "ffn_conv_b", "ffn_w_down"]
    res = {**big, **small}
    return (loss, grad_x, *[res[n][0] for n in order], *[res[n][1] for n in order], *[res[n][2] for n in order],
            *[res[n][3] for n in order])
```

```python
import numpy as np
import jax
import jax.numpy as jnp
from jax import lax
from jax.experimental import pallas as pl
from jax.experimental.pallas import tpu as pltpu

F32 = jnp.float32
BF16 = jnp.bfloat16

D_MODEL = 1024
DEPTH = 4
HEAD_DIM = 64
N_Q_HEADS = 16
N_KV_HEADS = 4
Q_PER_KV = 4
ATTN_BLOCK = 128
ATTN_IN = 1536
HG_HEADS = 8
HG_K = 128
HG_CHUNK = 64
HG_IN = 4096
D_FF = 2816
EPS = 1e-6
N_DEV = 8
FF_SLOT = 2 * D_FF // N_DEV
HG_SLOT = HG_IN // N_DEV
HG_LEVELS = 6

ADAM_LR = 0.001
ADAM_B1 = 0.9
ADAM_B2 = 0.999
ADAM_EPS = 1e-08
ADAM_WD = 0.01
ADAM_STEP = 10

VMEM_LIMIT = 56 * 1024 * 1024
ROW_TILE = 1024
WIDE_ROW_TILE = 2048
NEG_BIG = -1e30

NN = (((1,), (0,)), ((), ()))
NT = (((1,), (1,)), ((), ()))
TN = (((0,), (0,)), ((), ()))


def _bdot(a, b, dn):
    return lax.dot_general(a.astype(BF16), b.astype(BF16), dn, preferred_element_type=F32)


def _sds(shape, dtype):
    return jax.ShapeDtypeStruct(tuple(shape), dtype)


def _params(sem):
    return pltpu.CompilerParams(dimension_semantics=sem, vmem_limit_bytes=VMEM_LIMIT)


DEP_SHAPE = (8, 128)


def _dep_spec(rank):
    return pl.BlockSpec(DEP_SHAPE, lambda *_: (0, 0))


def _matmul(name, a, b, *, dn, grid, a_spec, b_spec, o_spec, out_shape, acc_shape=None, extra=(), extra_specs=(),
            finish=None, dep=None, sem=("parallel", "parallel", "arbitrary")):
    nk = grid[2]
    many = isinstance(out_shape, (list, tuple))
    n_in = 2 + len(extra) + (dep is not None)
    n_out = len(out_shape) if many else 1

    def body(*refs):
        a_ref, b_ref = refs[0], refs[1]
        outs = refs[n_in:n_in + n_out]

        def prod():
            return _bdot(a_ref[...], b_ref[...], dn)

        def done(v):
            if finish is None:
                outs[0][...] = v.astype(outs[0].dtype)
            else:
                finish(v, refs[2:2 + len(extra)], outs)

        if nk == 1:
            done(prod())
        else:
            acc = refs[-1]
            k = pl.program_id(2)

            @pl.when(k == 0)
            def _():
                acc[...] = prod()

            @pl.when(k > 0)
            def _():
                acc[...] += prod()

            @pl.when(k == nk - 1)
            def _():
                done(acc[...])

    in_specs = [a_spec, b_spec, *extra_specs] + ([_dep_spec(3)] if dep is not None else [])
    args = (a, b, *extra) + ((dep,) if dep is not None else ())
    scratch = [] if nk == 1 else [pltpu.VMEM(acc_shape, F32)]
    return pl.pallas_call(
        body, name=name, grid=grid, in_specs=in_specs, out_specs=o_spec, out_shape=out_shape,
        scratch_shapes=scratch, compiler_params=_params(sem),
    )(*args)


def _rms(x):
    return lax.rsqrt(jnp.mean(x * x, axis=-1, keepdims=True) + EPS)


def _residual_finish(v, ex, outs):
    h = v + ex[0][...]
    outs[0][...] = h
    if len(ex) > 1:
        outs[1][...] = (h * _rms(h) * ex[1][...]).astype(outs[1].dtype)


def _norm_bwd_finish(v, ex, outs):
    x = ex[0][...]
    r = _rms(x)
    xh = x * r
    dyg = v * ex[1][...]
    dh = ex[2][...] + r * (dyg - xh * jnp.mean(dyg * xh, axis=-1, keepdims=True))
    outs[0][...] = dh
    outs[2][...] = dh.astype(outs[2].dtype)
    part = jnp.sum(v * xh, axis=0, keepdims=True)

    @pl.when(pl.program_id(0) == 0)
    def _():
        outs[1][...] = part

    @pl.when(pl.program_id(0) > 0)
    def _():
        outs[1][...] += part


def _row_io(tm, norm_g):
    row = pl.BlockSpec((tm, D_MODEL), lambda i, j, k: (i, 0))
    vec = pl.BlockSpec((1, D_MODEL), lambda i, j, k: (0, 0))
    if norm_g is None:
        return (row,), row, lambda S: _sds((S, D_MODEL), F32)
    return (row, vec), [row, row], lambda S: [_sds((S, D_MODEL), F32), _sds((S, D_MODEL), BF16)]


def _tile(n, t):
    return min(n, t)


def _proj_rows(name, hn, wt, l, out_dtype, dep=None):
    S, N = hn.shape[0], wt.shape[1]
    tm, tn = _tile(S, ROW_TILE), 512
    return _matmul(
        name, hn, wt, dn=NT, grid=(S // tm, N // tn, 1),
        a_spec=pl.BlockSpec((tm, D_MODEL), lambda i, j, k: (i, 0)),
        b_spec=pl.BlockSpec((None, tn, D_MODEL), lambda i, j, k: (l, j, 0)),
        o_spec=pl.BlockSpec((tm, tn), lambda i, j, k: (i, j)),
        out_shape=_sds((S, N), out_dtype), dep=dep)


def _slot_weight(w, transposed):
    if transposed:
        return w.shape[2], (None, None, w.shape[2], D_MODEL), NT, NN
    return w.shape[3], (None, None, D_MODEL, w.shape[3]), NN, NT


def _proj_slots(name, hn, w, l, transposed=False, dep=None, out_dtype=F32):
    S = hn.shape[0]
    r, blk, dn, _ = _slot_weight(w, transposed)
    tm = _tile(S, WIDE_ROW_TILE)
    return _matmul(
        name, hn, w, dn=dn, grid=(N_DEV, S // tm, 1),
        a_spec=pl.BlockSpec((tm, D_MODEL), lambda j, i, k: (i, 0)),
        b_spec=pl.BlockSpec(blk, lambda j, i, k: (l, j, 0, 0)),
        o_spec=pl.BlockSpec((None, tm, r), lambda j, i, k: (j, i, 0)),
        out_shape=_sds((N_DEV, S, r), out_dtype), dep=dep)


def _out_proj(name, o, w, l, h, norm_g=None):
    S, K = o.shape
    tm = _tile(S, ROW_TILE)
    extra_specs, o_spec, out_shape = _row_io(tm, norm_g)
    return _matmul(
        name, o, w, dn=NN, grid=(S // tm, 1, 1),
        a_spec=pl.BlockSpec((tm, K), lambda i, j, k: (i, 0)),
        b_spec=pl.BlockSpec((None, K, D_MODEL), lambda i, j, k: (l, 0, 0)),
        o_spec=o_spec, out_shape=out_shape(S), extra=(h,) if norm_g is None else (h, norm_g),
        extra_specs=extra_specs, finish=_residual_finish)


def _down_proj(name, a, w, l, h, norm_g=None):
    nj, S, r = a.shape
    tm = _tile(S, ROW_TILE)
    extra_specs, o_spec, out_shape = _row_io(tm, norm_g)
    return _matmul(
        name, a, w, dn=NN, grid=(S // tm, 1, nj),
        a_spec=pl.BlockSpec((None, tm, r), lambda i, j, k: (k, i, 0)),
        b_spec=pl.BlockSpec((None, None, r, D_MODEL), lambda i, j, k: (l, k, 0, 0)),
        o_spec=o_spec, out_shape=out_shape(S), acc_shape=(tm, D_MODEL),
        extra=(h,) if norm_g is None else (h, norm_g), extra_specs=extra_specs, finish=_residual_finish)


def _dgrad_down(name, dh, w, l, dep=None):
    S = dh.shape[0]
    nj, r = w.shape[1], w.shape[2]
    tm = _tile(S, ROW_TILE)
    return _matmul(
        name, dh, w, dn=NT, grid=(nj, S // tm, 1),
        a_spec=pl.BlockSpec((tm, D_MODEL), lambda j, i, k: (i, 0)),
        b_spec=pl.BlockSpec((None, None, r, D_MODEL), lambda j, i, k: (l, j, 0, 0)),
        o_spec=pl.BlockSpec((None, tm, r), lambda j, i, k: (j, i, 0)),
        out_shape=_sds((nj, S, r), BF16), dep=dep)


def _wgrad_down(name, a, dh):
    nj, S, r = a.shape
    tk = _tile(S, WIDE_ROW_TILE)
    return _matmul(
        name, a, dh, dn=TN, grid=(nj, 1, S // tk),
        a_spec=pl.BlockSpec((None, tk, r), lambda s, j, k: (s, k, 0)),
        b_spec=pl.BlockSpec((tk, D_MODEL), lambda s, j, k: (k, 0)),
        o_spec=pl.BlockSpec((None, r, D_MODEL), lambda s, j, k: (s, 0, 0)),
        out_shape=_sds((nj, r, D_MODEL), BF16), acc_shape=(r, D_MODEL))


def _norm_bwd_io(tm, S):
    row = pl.BlockSpec((tm, D_MODEL), lambda i, j, k: (i, 0))
    vec = pl.BlockSpec((1, D_MODEL), lambda i, j, k: (0, 0))
    return dict(extra_specs=(row, vec, row), o_spec=[row, vec, row],
                out_shape=[_sds((S, D_MODEL), F32), _sds((1, D_MODEL), F32), _sds((S, D_MODEL), BF16)],
                finish=_norm_bwd_finish, sem=("arbitrary", "arbitrary", "arbitrary"))


def _dgrad_slots(name, dz, w, l, norm, transposed=False):
    nj, S, r = dz.shape
    _, blk, _, dn = _slot_weight(w, transposed)
    tm = _tile(S, ROW_TILE)
    return _matmul(
        name, dz, w, dn=dn, grid=(S // tm, 1, nj),
        a_spec=pl.BlockSpec((None, tm, r), lambda i, j, k: (k, i, 0)),
        b_spec=pl.BlockSpec(blk, lambda i, j, k: (l, k, 0, 0)),
        acc_shape=(tm, D_MODEL), extra=norm, **_norm_bwd_io(tm, S))


def _wgrad_slots(name, hn, dz, transposed=False):
    nj, S, r = dz.shape
    tk = _tile(S, WIDE_ROW_TILE)
    hn_spec = pl.BlockSpec((tk, D_MODEL), lambda s, j, k: (k, 0))
    dz_spec = pl.BlockSpec((None, tk, r), lambda s, j, k: (s, k, 0))
    if transposed:
        return _matmul(
            name, dz, hn, dn=TN, grid=(nj, 1, S // tk), a_spec=dz_spec, b_spec=hn_spec,
            o_spec=pl.BlockSpec((None, r, D_MODEL), lambda s, j, k: (s, 0, 0)),
            out_shape=_sds((nj, r, D_MODEL), BF16), acc_shape=(r, D_MODEL))
    return _matmul(
        name, hn, dz, dn=TN, grid=(nj, 1, S // tk), a_spec=hn_spec, b_spec=dz_spec,
        o_spec=pl.BlockSpec((None, D_MODEL, r), lambda s, j, k: (s, 0, 0)),
        out_shape=_sds((nj, D_MODEL, r), BF16), acc_shape=(D_MODEL, r))


def _dgrad_out(name, dh, w, l, out_dtype, dep=None):
    S, K = dh.shape[0], w.shape[1]
    tm = _tile(S, ROW_TILE)
    return _matmul(
        name, dh, w, dn=NT, grid=(S // tm, 1, 1),
        a_spec=pl.BlockSpec((tm, D_MODEL), lambda i, j, k: (i, 0)),
        b_spec=pl.BlockSpec((None, K, D_MODEL), lambda i, j, k: (l, 0, 0)),
        o_spec=pl.BlockSpec((tm, K), lambda i, j, k: (i, 0)),
        out_shape=_sds((S, K), out_dtype), dep=dep)


def _wgrad_rows(name, a, b):
    S, K = a.shape
    tk = _tile(S, WIDE_ROW_TILE)
    return _matmul(
        name, a, b, dn=TN, grid=(1, 1, S // tk),
        a_spec=pl.BlockSpec((tk, K), lambda i, j, k: (k, 0)),
        b_spec=pl.BlockSpec((tk, D_MODEL), lambda i, j, k: (k, 0)),
        o_spec=pl.BlockSpec((K, D_MODEL), lambda i, j, k: (0, 0)),
        out_shape=_sds((K, D_MODEL), BF16), acc_shape=(K, D_MODEL))


def _dgrad_rows(name, dz, wt, l, norm):
    S, N = dz.shape
    tm = _tile(S, ROW_TILE)
    return _matmul(
        name, dz, wt, dn=NN, grid=(S // tm, 1, 1),
        a_spec=pl.BlockSpec((tm, N), lambda i, j, k: (i, 0)),
        b_spec=pl.BlockSpec((None, N, D_MODEL), lambda i, j, k: (l, 0, 0)),
        extra=norm, **_norm_bwd_io(tm, S))


def _rmsnorm_fwd(name, h, g):
    S = h.shape[0]
    tm = _tile(S, ROW_TILE)

    def body(h_ref, g_ref, o_ref):
        x = h_ref[...]
        o_ref[...] = (x * _rms(x) * g_ref[...]).astype(o_ref.dtype)

    row = pl.BlockSpec((tm, D_MODEL), lambda i: (i, 0))
    return pl.pallas_call(
        body, name=name, grid=(S // tm,), in_specs=[row, pl.BlockSpec((1, D_MODEL), lambda i: (0, 0))],
        out_specs=row, out_shape=_sds((S, D_MODEL), BF16), compiler_params=_params(("parallel",)),
    )(h, g)


def _loss_head(name, h, g, target):
    S = h.shape[0]
    tm = _tile(S, ROW_TILE)

    def body(h_ref, g_ref, t_ref, dh_ref, dg_ref, ls_ref, dhb_ref):
        x = h_ref[...]
        r = _rms(x)
        xh = x * r
        diff = xh * g_ref[...] - t_ref[...]
        dyf = diff * (1.0 / D_MODEL)
        dyg = dyf * g_ref[...]
        dh = r * (dyg - xh * jnp.mean(dyg * xh, axis=-1, keepdims=True))
        dh_ref[...] = dh
        dhb_ref[...] = dh.astype(dhb_ref.dtype)
        part = jnp.sum(dyf * xh, axis=0, keepdims=True)
        lpart = jnp.sum(diff * diff, axis=0, keepdims=True) * (0.5 / D_MODEL)

        @pl.when(pl.program_id(0) == 0)
        def _():
            dg_ref[...] = part
            ls_ref[...] = lpart

        @pl.when(pl.program_id(0) > 0)
        def _():
            dg_ref[...] += part
            ls_ref[...] += lpart

    row = pl.BlockSpec((tm, D_MODEL), lambda i: (i, 0))
    vec = pl.BlockSpec((1, D_MODEL), lambda i: (0, 0))
    return pl.pallas_call(
        body, name=name, grid=(S // tm,), in_specs=[row, vec, row], out_specs=[row, vec, vec, row],
        out_shape=[_sds((S, D_MODEL), F32), _sds((1, D_MODEL), F32), _sds((1, D_MODEL), F32), _sds((S, D_MODEL), BF16)],
        compiler_params=_params(("arbitrary",)),
    )(h, g, target)


ATTN_SCALE = HEAD_DIM ** -0.5
ALIBI_SLOPES = [2.0 ** (-8.0 * (h + 1) / N_Q_HEADS) for h in range(N_Q_HEADS)]
K_COL = N_Q_HEADS * HEAD_DIM
KV_COLS = N_KV_HEADS * HEAD_DIM
V_COL = K_COL + KV_COLS


def _attn_masks(n):
    qi = lax.broadcasted_iota(jnp.int32, (ATTN_BLOCK, ATTN_BLOCK), 0)
    ki = lax.broadcasted_iota(jnp.int32, (ATTN_BLOCK, ATTN_BLOCK), 1)
    dist_c = (qi - ki).astype(F32)
    return dist_c + float(ATTN_BLOCK), dist_c, (ki > qi) & (n > 0), qi >= ki


def _attn_probs(raw_p, raw_c, sink, slope, masks):
    dist_p, dist_c, valid_p, valid_c = masks
    sp = jnp.where(valid_p, raw_p * ATTN_SCALE - slope * dist_p, NEG_BIG)
    sc = jnp.where(valid_c, raw_c * ATTN_SCALE - slope * dist_c, NEG_BIG)
    m = jnp.maximum(jnp.maximum(jnp.max(sp, axis=-1, keepdims=True), jnp.max(sc, axis=-1, keepdims=True)), sink)
    ep, ec, es = jnp.exp(sp - m), jnp.exp(sc - m), jnp.exp(sink - m)
    inv = 1.0 / (jnp.sum(ep, axis=-1, keepdims=True) + jnp.sum(ec, axis=-1, keepdims=True) + es)
    return ep * inv, ec * inv, es * inv


def _group_rows(ref, m):
    return jnp.concatenate([ref[:, HEAD_DIM * (Q_PER_KV * m + g):HEAD_DIM * (Q_PER_KV * m + g + 1)]
                            for g in range(Q_PER_KV)], axis=0)


def _head_rows(x, g):
    return x[ATTN_BLOCK * g:ATTN_BLOCK * (g + 1)]


def _attn_specs(nblk):
    last = nblk - 1
    kcol, vcol = K_COL // KV_COLS, V_COL // KV_COLS
    return [
        pl.BlockSpec((ATTN_BLOCK, K_COL), lambda n: (jnp.minimum(n, last), 0)),
        pl.BlockSpec((ATTN_BLOCK, KV_COLS), lambda n: (jnp.minimum(n, last), kcol)),
        pl.BlockSpec((ATTN_BLOCK, KV_COLS), lambda n: (jnp.maximum(jnp.minimum(n, last) - 1, 0), kcol)),
        pl.BlockSpec((ATTN_BLOCK, KV_COLS), lambda n: (jnp.minimum(n, last), vcol)),
        pl.BlockSpec((ATTN_BLOCK, KV_COLS), lambda n: (jnp.maximum(jnp.minimum(n, last) - 1, 0), vcol)),
    ]


P_COLS = 2 * ATTN_BLOCK


def _attn_fwd(name, proj, sinks):
    S = proj.shape[0]
    nblk = S // ATTN_BLOCK

    def body(q_ref, kc_ref, kp_ref, vc_ref, vp_ref, sk_ref, o_ref, p_ref, ps_ref):
        masks = _attn_masks(pl.program_id(0))
        lane = lax.broadcasted_iota(jnp.int32, (ATTN_BLOCK, 128), 1)
        sink_p = jnp.zeros((ATTN_BLOCK, 128), F32)
        for m in range(N_KV_HEADS):
            ks = slice(HEAD_DIM * m, HEAD_DIM * (m + 1))
            kp, kc, vp, vc = kp_ref[:, ks], kc_ref[:, ks], vp_ref[:, ks], vc_ref[:, ks]
            q4 = _group_rows(q_ref, m)
            raw_p, raw_c = _bdot(q4, kp, NT), _bdot(q4, kc, NT)
            pps, pcs = [], []
            for g in range(Q_PER_KV):
                hh = Q_PER_KV * m + g
                pp, pc, ps = _attn_probs(_head_rows(raw_p, g), _head_rows(raw_c, g), sk_ref[0, hh], ALIBI_SLOPES[hh], masks)
                pps.append(pp.astype(BF16))
                pcs.append(pc.astype(BF16))
                p_ref[:, P_COLS * hh:P_COLS * hh + ATTN_BLOCK] = pps[g]
                p_ref[:, P_COLS * hh + ATTN_BLOCK:P_COLS * (hh + 1)] = pcs[g]
                sink_p = jnp.where(lane == hh, ps, sink_p)
            o4 = _bdot(jnp.concatenate(pps, axis=0), vp, NN) + _bdot(jnp.concatenate(pcs, axis=0), vc, NN)
            for g in range(Q_PER_KV):
                hh = Q_PER_KV * m + g
                o_ref[:, HEAD_DIM * hh:HEAD_DIM * (hh + 1)] = _head_rows(o4, g).astype(o_ref.dtype)
        ps_ref[...] = sink_p

    row = lambda cols: pl.BlockSpec((ATTN_BLOCK, cols), lambda n: (n, 0))
    return pl.pallas_call(
        body, name=name, grid=(nblk,),
        in_specs=_attn_specs(nblk) + [pl.BlockSpec(memory_space=pltpu.SMEM)],
        out_specs=[row(K_COL), row(N_Q_HEADS * P_COLS), row(128)],
        out_shape=[_sds((S, K_COL), BF16), _sds((S, N_Q_HEADS * P_COLS), BF16), _sds((S, 128), F32)],
        compiler_params=_params(("parallel",)),
    )(proj, proj, proj, proj, proj, sinks)


def _attn_bwd(name, proj, probs, sink_probs, do):
    S = proj.shape[0]
    nblk = S // ATTN_BLOCK

    def body(q_ref, kc_ref, kp_ref, vc_ref, vp_ref, do_ref, p_ref, ps_ref, dz_ref, ds_ref, carry, cur, padd):
        n = pl.program_id(0)

        @pl.when(n == 0)
        def _():
            carry[...] = jnp.zeros_like(carry)
            ds_ref[...] = jnp.zeros_like(ds_ref)

        @pl.when(n < nblk)
        def _():
            lane = lax.broadcasted_iota(jnp.int32, (ATTN_BLOCK, 128), 1)
            sink_p = ps_ref[...]
            dsv = jnp.zeros((1, 128), F32)
            for m in range(N_KV_HEADS):
                ks = slice(HEAD_DIM * m, HEAD_DIM * (m + 1))
                kp, kc, vp, vc = kp_ref[:, ks], kc_ref[:, ks], vp_ref[:, ks], vc_ref[:, ks]
                q4, do4 = _group_rows(q_ref, m), _group_rows(do_ref, m)
                dpp4, dpc4 = _bdot(do4, vp, NT), _bdot(do4, vc, NT)
                pps, pcs, dsps, dscs = [], [], [], []
                for g in range(Q_PER_KV):
                    hh = Q_PER_KV * m + g
                    pps.append(p_ref[:, P_COLS * hh:P_COLS * hh + ATTN_BLOCK])
                    pcs.append(p_ref[:, P_COLS * hh + ATTN_BLOCK:P_COLS * (hh + 1)])
                    pp, pc = pps[g].astype(F32), pcs[g].astype(F32)
                    dpp, dpc = _head_rows(dpp4, g), _head_rows(dpc4, g)
                    delta = jnp.sum(pp * dpp, axis=-1, keepdims=True) + jnp.sum(pc * dpc, axis=-1, keepdims=True)
                    dsv = dsv - jnp.sum(jnp.where(lane == hh, sink_p, 0.0) * delta, axis=0, keepdims=True)
                    dsps.append((pp * (dpp - delta)).astype(BF16))
                    dscs.append((pc * (dpc - delta)).astype(BF16))
                pp4, pc4 = jnp.concatenate(pps, axis=0), jnp.concatenate(pcs, axis=0)
                dsp4, dsc4 = jnp.concatenate(dsps, axis=0), jnp.concatenate(dscs, axis=0)
                dq4 = (_bdot(dsp4, kp, NN) + _bdot(dsc4, kc, NN)) * ATTN_SCALE
                for g in range(Q_PER_KV):
                    hh = Q_PER_KV * m + g
                    cur[:, HEAD_DIM * hh:HEAD_DIM * (hh + 1)] = _head_rows(dq4, g)
                cur[:, K_COL + HEAD_DIM * m:K_COL + HEAD_DIM * (m + 1)] = _bdot(dsc4, q4, TN) * ATTN_SCALE
                cur[:, V_COL + HEAD_DIM * m:V_COL + HEAD_DIM * (m + 1)] = _bdot(pc4, do4, TN)
                padd[:, ks] = _bdot(dsp4, q4, TN) * ATTN_SCALE
                padd[:, KV_COLS + HEAD_DIM * m:KV_COLS + HEAD_DIM * (m + 1)] = _bdot(pp4, do4, TN)
            ds_ref[...] += dsv
            dz_ref[:, :K_COL] = carry[:, :K_COL].astype(dz_ref.dtype)
            dz_ref[:, K_COL:] = (carry[:, K_COL:] + padd[...]).astype(dz_ref.dtype)
            carry[...] = cur[...]

        @pl.when(n == nblk)
        def _():
            dz_ref[...] = carry[...].astype(dz_ref.dtype)

    return pl.pallas_call(
        body, name=name, grid=(nblk + 1,),
        in_specs=_attn_specs(nblk) + [
            pl.BlockSpec((ATTN_BLOCK, cols), lambda n: (jnp.minimum(n, nblk - 1), 0))
            for cols in (K_COL, N_Q_HEADS * P_COLS, 128)],
        out_specs=[pl.BlockSpec((ATTN_BLOCK, ATTN_IN), lambda n: (jnp.maximum(n - 1, 0), 0)),
                   pl.BlockSpec((1, 128), lambda n: (0, 0))],
        out_shape=[_sds((S, ATTN_IN), BF16), _sds((1, 128), F32)],
        scratch_shapes=[pltpu.VMEM((ATTN_BLOCK, ATTN_IN), F32), pltpu.VMEM((ATTN_BLOCK, ATTN_IN), F32),
                        pltpu.VMEM((ATTN_BLOCK, 2 * KV_COLS), F32)],
        compiler_params=_params(("arbitrary",)),
    )(proj, proj, proj, proj, proj, do, probs, sink_probs)


def _hg_consts():
    C = HG_CHUNK
    tri = np.tril(np.ones((C, C)))
    t = np.arange(C)
    rows, masks = [tri], []
    for lvl in range(HG_LEVELS):
        n = C >> (lvl + 1)
        sel = np.zeros((C, C))
        sel[t, (t // (2 * n)) * (2 * n) + n - 1] = 1.0
        rows.append(sel @ tri)
        tt, ss = t[:, None], t[None, :]
        masks.append((tt // (2 * n) == ss // (2 * n)) & ((tt // n) % 2 == 1) & ((ss // n) % 2 == 0))
    masks.append(np.eye(C, dtype=bool))
    stk = np.concatenate(rows, axis=0)
    return jnp.asarray(stk, BF16), jnp.asarray(np.stack(masks), F32)


def _sigmoid(x):
    return 1.0 / (1.0 + jnp.exp(-x))


def _silu_sigmoid(x):
    return 0.5 + 0.5 * jnp.tanh(0.5 * x)


def _split(x, parts):
    out, rest = [], x
    for _ in range(parts):
        out.append(rest.astype(BF16))
        rest = rest - out[-1].astype(F32)
    return out


def _dot01(m01, x, dn, parts=3):
    return sum(lax.dot_general(m01, p, dn, preferred_element_type=F32) for p in _split(x, parts))


def _ref_rows(b, n):
    C = b.shape[1]
    if 2 * n >= 8:
        b3 = b.reshape(HG_CHUNK // (2 * n), 2 * n, C)
        return jnp.broadcast_to(b3[:, n - 1:n, :], b3.shape).reshape(HG_CHUNK, C)
    pos = lax.broadcasted_iota(jnp.int32, b.shape, 0) % (2 * n)
    out = b
    for p in range(2 * n):
        if p != n - 1:
            out = jnp.where(pos == p, pltpu.roll(b, (p - (n - 1)) % HG_CHUNK, 0), out)
    return out


HG_STEP_CHUNKS = 4


def _chunk_rows(ci):
    return pl.ds(pl.multiple_of(ci * HG_CHUNK, HG_CHUNK), HG_CHUNK)


def _hg_common(z_ref, rows, lb_ref, stk_ref):
    qr, fr = z_ref[0, rows, :], z_ref[1, rows, :]
    lb = lb_ref[...]
    sq, sg, sgn = _silu_sigmoid(qr), _sigmoid(fr), _sigmoid(-fr)
    ft = lb + (1.0 - lb) * sg
    b = _dot01(stk_ref[0:HG_CHUNK, :], jnp.log(ft), NN)
    ws = [jnp.exp(-jnp.abs(b - _ref_rows(b, HG_CHUNK >> (l + 1)))) for l in range(HG_LEVELS)]
    blast = b[HG_CHUNK - 1:HG_CHUNK]
    return dict(qr=qr, fr=fr, lb=lb, sq=sq, sg=sg, sgn=sgn, ft=ft, q=qr * sq, kk=(1.0 - lb) * sgn, b=b,
                ws=ws, eb=jnp.exp(b), ed=jnp.exp(blast - b), elast=jnp.exp(blast))


def _hg_factors(qh, kh, ws, sl):
    return ([(qh * ws[l][:, sl]).astype(BF16) for l in range(HG_LEVELS)],
            [(kh * ws[l][:, sl]).astype(BF16) for l in range(HG_LEVELS)])


def _hg_intra(qh, kh, ws, msk_ref, sl):
    qls, kls = _hg_factors(qh, kh, ws, sl)
    a = msk_ref[HG_LEVELS] * _bdot(qh, kh, NT)
    for l in range(HG_LEVELS):
        a = a + msk_ref[l] * _bdot(qls[l], kls[l], NT)
    return a


def _hg_fwd(name, z, lb, ng):
    S = z.shape[2]
    nc = S // HG_CHUNK
    per = min(HG_STEP_CHUNKS, nc)
    stk, msk = _hg_consts()

    def body(z_ref, lb_ref, ng_ref, stk_ref, msk_ref, og_ref, st_ref, a_ref, o_ref, state):
        @pl.when(pl.program_id(1) == 0)
        def _():
            state[...] = jnp.zeros_like(state)

        def chunk(ci, _):
            rows = _chunk_rows(ci)
            cm = _hg_common(z_ref, rows, lb_ref, stk_ref)
            v, gt = z_ref[2, rows, :], z_ref[3, rows, :]
            kd = cm["kk"] * cm["ed"]
            for hh in range(4):
                sl = slice(HG_K * hh, HG_K * (hh + 1))
                st = state[hh]
                st_ref[ci, hh] = st
                qh, kh, vh = cm["q"][:, sl], cm["kk"][:, sl], v[:, sl]
                a = _hg_intra(qh, kh, cm["ws"], msk_ref, sl).astype(BF16)
                a_ref[ci, hh] = a
                o = _bdot(a, vh, NN) + _bdot(qh * cm["eb"][:, sl], st, NT)
                o_ref[rows, sl] = o
                state[hh] = cm["elast"][:, sl] * st + _bdot(vh, kd[:, sl], TN)
                gh = gt[:, sl]
                og_ref[rows, sl] = (o * _rms(o) * ng_ref[...] * (gh * _silu_sigmoid(gh))).astype(og_ref.dtype)
            return 0

        lax.fori_loop(0, per, chunk, 0, unroll=True)

    return pl.pallas_call(
        body, name=name, grid=(2, nc // per),
        in_specs=[pl.BlockSpec((4, None, per * HG_CHUNK, HG_SLOT), lambda g, c: (0, g, c, 0)),
                  pl.BlockSpec((1, HG_SLOT), lambda g, c: (0, g)),
                  pl.BlockSpec((1, HG_K), lambda g, c: (0, 0)),
                  pl.BlockSpec(stk.shape, lambda g, c: (0, 0)),
                  pl.BlockSpec(msk.shape, lambda g, c: (0, 0, 0))],
        out_specs=[pl.BlockSpec((per * HG_CHUNK, HG_SLOT), lambda g, c: (c, g)),
                   pl.BlockSpec((per, 4, HG_K, HG_K), lambda g, c: (c, g, 0, 0)),
                   pl.BlockSpec((per, 4, HG_CHUNK, HG_CHUNK), lambda g, c: (c, g, 0, 0)),
                   pl.BlockSpec((per * HG_CHUNK, HG_SLOT), lambda g, c: (c, g))],
        out_shape=[_sds((S, D_MODEL), BF16), _sds((nc, HG_HEADS, HG_K, HG_K), F32),
                   _sds((nc, HG_HEADS, HG_CHUNK, HG_CHUNK), BF16), _sds((S, D_MODEL), F32)],
        scratch_shapes=[pltpu.VMEM((4, HG_K, HG_K), F32)],
        compiler_params=_params(("parallel", "arbitrary")),
    )(z, lb, ng, stk, msk)


def _hg_bwd(name, z, lb, ng, states, intra, o_pre, dog):
    S = z.shape[2]
    nc = S // HG_CHUNK
    per = min(HG_STEP_CHUNKS, nc)
    stk, msk = _hg_consts()

    def body(z_ref, lb_ref, ng_ref, stk_ref, msk_ref, st_ref, a_ref, o_ref, dog_ref, dz_ref, dlb_ref, dng_ref, dstate):
        @pl.when(pl.program_id(1) == 0)
        def _():
            dstate[...] = jnp.zeros_like(dstate)
            dlb_ref[...] = jnp.zeros_like(dlb_ref)
            dng_ref[...] = jnp.zeros_like(dng_ref)

        def chunk(k, _):
            ci = per - 1 - k
            rows = _chunk_rows(ci)
            cm = _hg_common(z_ref, rows, lb_ref, stk_ref)
            v, gt = z_ref[2, rows, :], z_ref[3, rows, :]
            ng = ng_ref[...]
            kd = cm["kk"] * cm["ed"]
            row = lax.broadcasted_iota(jnp.int32, (HG_CHUNK, 1), 0)
            dng = jnp.zeros((1, HG_K), F32)
            dq_h, dkk_h, db_h, dv_h, dgt_h = [], [], [], [], []
            dr_h = [[] for _ in range(HG_LEVELS)]
            for hh in range(4):
                sl = slice(HG_K * hh, HG_K * (hh + 1))
                st, dst = st_ref[ci, hh], dstate[hh]
                qh, kh, vh, ebh, edh, kdh = cm["q"][:, sl], cm["kk"][:, sl], v[:, sl], cm["eb"][:, sl], cm["ed"][:, sl], kd[:, sl]
                elh = cm["elast"][:, sl]
                qls, kls = _hg_factors(qh, kh, cm["ws"], sl)
                a, o = a_ref[ci, hh], o_ref[rows, sl]
                qe = qh * ebh
                r = _rms(o)
                xh = o * r
                gh = gt[:, sl]
                sgg = _silu_sigmoid(gh)
                dog = dog_ref[rows, sl].astype(F32)
                dy = dog * (gh * sgg)
                dgt_h.append(dog * (xh * ng) * (sgg * (1.0 + gh * (1.0 - sgg))))
                dng = dng + jnp.sum(dy * xh, axis=0, keepdims=True)
                dyg = dy * ng
                do = r * (dyg - xh * jnp.mean(dyg * xh, axis=-1, keepdims=True))
                da = _bdot(do, vh, NT)
                dv_h.append(_bdot(a, do, TN) + _bdot(kdh, dst, NT))
                dkd = _bdot(vh, dst, NN)
                delast = jnp.sum(st * dst, axis=0, keepdims=True)
                dqe = _bdot(do, st, NN)
                dstate[hh] = elh * dst + _bdot(do, qe, TN)
                gk = dkd * kdh
                dblast = jnp.sum(gk, axis=0, keepdims=True) + delast * elh
                db = dqe * qe - gk + jnp.where(row == HG_CHUNK - 1, dblast, 0.0)
                dp = (msk_ref[HG_LEVELS] * da).astype(BF16)
                dq = dqe * ebh + _bdot(dp, kh, NN)
                dkk = dkd * edh + _bdot(dp, qh, TN)
                for l in range(HG_LEVELS):
                    dp = (msk_ref[l] * da).astype(BF16)
                    dql, dkl = _bdot(dp, kls[l], NN), _bdot(dp, qls[l], TN)
                    w = cm["ws"][l][:, sl]
                    dq = dq + dql * w
                    dkk = dkk + dkl * w
                    half = jnp.where(((row >> (HG_LEVELS - 1 - l)) & 1) == 1, 1.0, -1.0)
                    dd = half * w * (dql * qh + dkl * kh)
                    db = db + dd
                    dr_h[l].append(-dd)
                dq_h.append(dq)
                dkk_h.append(dkk)
                db_h.append(db)
            cat = lambda xs: jnp.concatenate(xs, axis=1)
            cot = jnp.concatenate([cat(db_h)] + [cat(dr_h[l]) for l in range(HG_LEVELS)], axis=0)
            dlf = _dot01(stk_ref[...], cot, TN, parts=2)
            dq, dkk = cat(dq_h), cat(dkk_h)
            dft = dlf / cm["ft"]
            one_lb = 1.0 - cm["lb"]
            dz_ref[0, rows, :] = (dq * (cm["sq"] * (1.0 + cm["qr"] * (1.0 - cm["sq"])))).astype(dz_ref.dtype)
            dz_ref[1, rows, :] = ((dft - dkk) * one_lb * cm["sg"] * cm["sgn"]).astype(dz_ref.dtype)
            dz_ref[2, rows, :] = cat(dv_h).astype(dz_ref.dtype)
            dz_ref[3, rows, :] = cat(dgt_h).astype(dz_ref.dtype)
            dlb_ref[...] += jnp.sum((dft - dkk) * cm["sgn"], axis=0, keepdims=True)
            dng_ref[...] += dng
            return 0

        lax.fori_loop(0, per, chunk, 0, unroll=True)

    rev = lambda c: nc // per - 1 - c
    rows_blk = pl.BlockSpec((per * HG_CHUNK, HG_SLOT), lambda g, c: (rev(c), g))
    return pl.pallas_call(
        body, name=name, grid=(2, nc // per),
        in_specs=[pl.BlockSpec((4, None, per * HG_CHUNK, HG_SLOT), lambda g, c: (0, g, rev(c), 0)),
                  pl.BlockSpec((1, HG_SLOT), lambda g, c: (0, g)),
                  pl.BlockSpec((1, HG_K), lambda g, c: (0, 0)),
                  pl.BlockSpec(stk.shape, lambda g, c: (0, 0)),
                  pl.BlockSpec(msk.shape, lambda g, c: (0, 0, 0)),
                  pl.BlockSpec((per, 4, HG_K, HG_K), lambda g, c: (rev(c), g, 0, 0)),
                  pl.BlockSpec((per, 4, HG_CHUNK, HG_CHUNK), lambda g, c: (rev(c), g, 0, 0)),
                  rows_blk, rows_blk],
        out_specs=[pl.BlockSpec((4, None, per * HG_CHUNK, HG_SLOT), lambda g, c: (0, g, rev(c), 0)),
                   pl.BlockSpec((1, HG_SLOT), lambda g, c: (0, g)),
                   pl.BlockSpec((None, 1, HG_K), lambda g, c: (g, 0, 0))],
        out_shape=[_sds(z.shape, BF16), _sds((1, 2 * HG_SLOT), F32), _sds((2, 1, HG_K), F32)],
        scratch_shapes=[pltpu.VMEM((4, HG_K, HG_K), F32)],
        compiler_params=_params(("parallel", "arbitrary")),
    )(z, lb, ng, stk, msk, states, intra, o_pre, dog)


def _lb_fwd(name, logits):
    def body(l_ref, o_ref):
        x = l_ref[...]
        e = jnp.exp(x - jnp.max(x, axis=0, keepdims=True))
        s = e / jnp.sum(e, axis=0, keepdims=True)
        o_ref[0:1, :] = s[1:2]
        o_ref[1:2, :] = s[1:2] + s[2:3] + s[3:4]

    return pl.pallas_call(body, name=name, out_shape=_sds((2, logits.shape[1]), F32))(logits)


def _lb_bwd(name, logits, dlb):
    def body(l_ref, d_ref, o_ref):
        x = l_ref[...]
        e = jnp.exp(x - jnp.max(x, axis=0, keepdims=True))
        s = e / jnp.sum(e, axis=0, keepdims=True)
        d1, d3 = d_ref[0:1, :], d_ref[1:2, :]
        ds = [jnp.zeros_like(d1), d1 + d3, d3, d3]
        dot = sum(ds[r] * s[r:r + 1] for r in range(1, DEPTH))
        for r in range(DEPTH):
            o_ref[r:r + 1, :] = s[r:r + 1] * (ds[r] - dot)

    return pl.pallas_call(body, name=name, out_shape=_sds(logits.shape, F32))(logits, dlb)


SUB = 8


def _rows_down(x, prev, k):
    row = lax.broadcasted_iota(jnp.int32, x.shape, 0)
    return jnp.where(row >= k, pltpu.roll(x, k, 0), pltpu.roll(prev, k, 0))


def _rows_up(x, nxt, k):
    row = lax.broadcasted_iota(jnp.int32, x.shape, 0)
    return jnp.where(row < SUB - k, pltpu.roll(x, SUB - k, 0), pltpu.roll(nxt, SUB - k, 0))


def _conv_block(w_ref, b_ref, p, x, prev):
    return (b_ref[p] + w_ref[p, 0:1, :] * _rows_down(x, prev, 2) + w_ref[p, 1:2, :] * _rows_down(x, prev, 1)
            + w_ref[p, 2:3, :] * x)


def _convgate_fwd(name, u, cw, cb):
    S = u.shape[2]
    tm = _tile(S, ROW_TILE)

    def body(u_ref, w_ref, b_ref, a_ref, c_ref, halo):
        @pl.when(pl.program_id(1) == 0)
        def _():
            halo[...] = jnp.zeros_like(halo)

        def step(r, prev):
            pg, pv = prev
            out, cgs, cvs = [], [], []
            rows = pl.ds(pl.multiple_of(r * 2 * SUB, 2 * SUB), 2 * SUB)
            ug16, uv16 = u_ref[0, rows, :].astype(F32), u_ref[1, rows, :].astype(F32)
            for s in range(2):
                xg, xv = ug16[s * SUB:(s + 1) * SUB], uv16[s * SUB:(s + 1) * SUB]
                cgs.append(_conv_block(w_ref, b_ref, 0, xg, pg))
                cvs.append(_conv_block(w_ref, b_ref, 1, xv, pv))
                out.append(cgs[s] * _silu_sigmoid(cgs[s]) * cvs[s])
                pg, pv = xg, xv
            a_ref[rows, :] = jnp.concatenate(out, axis=0).astype(a_ref.dtype)
            c_ref[0, rows, :] = jnp.concatenate(cgs, axis=0).astype(c_ref.dtype)
            c_ref[1, rows, :] = jnp.concatenate(cvs, axis=0).astype(c_ref.dtype)
            return pg, pv

        pg, pv = lax.fori_loop(0, tm // (2 * SUB), step, (halo[0], halo[1]), unroll=2)
        halo[0] = pg
        halo[1] = pv

    pair = pl.BlockSpec((2, None, tm, FF_SLOT), lambda j, t: (0, j, t, 0))
    return pl.pallas_call(
        body, name=name, grid=(4, S // tm),
        in_specs=[pair, pl.BlockSpec((2, None, 3, FF_SLOT), lambda j, t: (0, j, 0, 0)),
                  pl.BlockSpec((2, None, 1, FF_SLOT), lambda j, t: (0, j, 0, 0))],
        out_specs=[pl.BlockSpec((None, tm, FF_SLOT), lambda j, t: (j, t, 0)), pair],
        out_shape=[_sds((4, S, FF_SLOT), BF16), _sds(u.shape, BF16)],
        scratch_shapes=[pltpu.VMEM((2, SUB, FF_SLOT), F32)],
        compiler_params=_params(("parallel", "arbitrary")),
    )(u, cw, cb)


def _convgate_bwd(name, u, convs, cw, da):
    S = u.shape[2]
    tm = _tile(S, ROW_TILE)
    nt = S // tm

    def body(u_ref, c_ref, w_ref, da_ref, du_out, dw_ref, db_ref, after, first, acc, du_ref):
        @pl.when(pl.program_id(1) == 0)
        def _():
            after[...] = jnp.zeros_like(after)
            acc[...] = jnp.zeros_like(acc)

        def finish(p, x, d, nxt, rows):
            taps = (_rows_up(d, nxt, 2), _rows_up(d, nxt, 1), d)
            du_ref[p, rows, :] = w_ref[p, 0:1, :] * taps[0] + w_ref[p, 1:2, :] * taps[1] + w_ref[p, 2:3, :] * d
            for j in range(3):
                acc[p, j] += taps[j] * x
            acc[p, 3] += d

        def step(r, carry):
            xg_last, xv_last, dg_last, dv_last = carry
            rows16 = pl.ds(pl.multiple_of(r * 2 * SUB, 2 * SUB), 2 * SUB)
            dav = da_ref[rows16, :].astype(F32)
            cg16, cv16 = c_ref[0, rows16, :].astype(F32), c_ref[1, rows16, :].astype(F32)
            ug16, uv16 = u_ref[0, rows16, :].astype(F32), u_ref[1, rows16, :].astype(F32)
            for s in range(2):
                at = r * 2 * SUB + s * SUB
                part = slice(s * SUB, (s + 1) * SUB)
                cg, cv, dab = cg16[part], cv16[part], dav[part]
                sg = _silu_sigmoid(cg)
                dg = dab * cv * (sg * (1.0 + cg * (1.0 - sg)))
                dv = dab * cg * sg
                before = pl.ds(pl.multiple_of(at - SUB, SUB), SUB)
                if s == 0:
                    @pl.when(r == 0)
                    def _():
                        first[0] = dg
                        first[1] = dv

                    @pl.when(r > 0)
                    def _():
                        finish(0, xg_last, dg_last, dg, before)
                        finish(1, xv_last, dv_last, dv, before)
                else:
                    finish(0, xg_last, dg_last, dg, before)
                    finish(1, xv_last, dv_last, dv, before)
                xg_last, xv_last, dg_last, dv_last = ug16[part], uv16[part], dg, dv
            return xg_last, xv_last, dg_last, dv_last

        zero = jnp.zeros((SUB, FF_SLOT), F32)
        xg_last, xv_last, dg_last, dv_last = lax.fori_loop(0, tm // (2 * SUB), step, (zero, zero, zero, zero))
        finish(0, xg_last, dg_last, after[0], slice(tm - SUB, tm))
        finish(1, xv_last, dv_last, after[1], slice(tm - SUB, tm))
        du_out[...] = du_ref[...].astype(du_out.dtype)
        after[...] = first[...]
        for p in range(2):
            for j in range(3):
                dw_ref[p, j:j + 1, :] = jnp.sum(acc[p, j], axis=0, keepdims=True)
            db_ref[p] = jnp.sum(acc[p, 3], axis=0, keepdims=True)

    rev = lambda t: nt - 1 - t
    pair = pl.BlockSpec((2, None, tm, FF_SLOT), lambda j, t: (0, j, rev(t), 0))
    taps = pl.BlockSpec((2, None, 3, FF_SLOT), lambda j, t: (0, j, 0, 0))
    bias = pl.BlockSpec((2, None, 1, FF_SLOT), lambda j, t: (0, j, 0, 0))
    return pl.pallas_call(
        body, name=name, grid=(4, nt),
        in_specs=[pair, pair, taps, pl.BlockSpec((None, tm, FF_SLOT), lambda j, t: (j, rev(t), 0))],
        out_specs=[pair, taps, bias],
        out_shape=[_sds(u.shape, BF16), _sds(cw.shape, F32), _sds((2, 4, 1, FF_SLOT), F32)],
        scratch_shapes=[pltpu.VMEM((2, SUB, FF_SLOT), F32), pltpu.VMEM((2, SUB, FF_SLOT), F32),
                        pltpu.VMEM((2, 4, SUB, FF_SLOT), F32), pltpu.VMEM((2, tm, FF_SLOT), F32)],
        compiler_params=_params(("parallel", "arbitrary")),
    )(u, convs, cw, da)


def _row_tile(R):
    for t in range(256, 15, -16):
        if R % t == 0:
            return t
    return R


def _adamw(name, gsrcs, w, m, v, dep=None):
    L = len(gsrcs)
    n, A, C = gsrcs[0].shape
    tr = _row_tile(A)
    deps = () if dep is None else (dep,)

    def body(*refs):
        g_refs = refs[:L]
        w_ref, m_ref, v_ref = refs[L:L + 3]
        go_ref, d_ref, mo_ref, vo_ref = refs[L + 3 + len(deps):]
        for k in range(L):
            @pl.when(pl.program_id(0) == k)
            def _(k=k):
                g = g_refs[k][0].astype(F32)
                for s in range(1, n):
                    g = g + g_refs[k][s].astype(F32)
                m2 = ADAM_B1 * m_ref[...] + (1.0 - ADAM_B1) * g
                v2 = ADAM_B2 * v_ref[...] + (1.0 - ADAM_B2) * (g * g)
                m_hat = m2 / (1.0 - ADAM_B1 ** ADAM_STEP)
                v_hat = v2 / (1.0 - ADAM_B2 ** ADAM_STEP)
                go_ref[...] = g
                d_ref[...] = -ADAM_LR * (m_hat / (jnp.sqrt(v_hat) + ADAM_EPS) + ADAM_WD * w_ref[...])
                mo_ref[...] = m2
                vo_ref[...] = v2

    g_specs = [pl.BlockSpec((n, tr, C), lambda l, i, k=k: (0, jnp.where(l == k, i, 0), 0)) for k in range(L)]
    blk = pl.BlockSpec((None, tr, C), lambda l, i: (l, i, 0))
    return pl.pallas_call(
        body, name=name, grid=(L, A // tr), in_specs=g_specs + [blk, blk, blk] + [_dep_spec(2)] * len(deps),
        out_specs=[blk] * 4, out_shape=[_sds((L, A, C), F32)] * 4, compiler_params=_params(("parallel", "parallel")),
    )(*gsrcs, w, m, v, *deps)


MESH = pl.DeviceIdType.MESH
HBM_SPEC = pl.BlockSpec(memory_space=pltpu.HBM)
N_PEERS = N_DEV - 1


def _mesh_place():
    x, y, c = lax.axis_index("x"), lax.axis_index("y"), lax.axis_index("c")
    peers = []
    for p in range(1, N_DEV):
        px = 1 - x if p & 4 else x
        py = 1 - y if p & 2 else y
        pc = 1 - c if p & 1 else c
        peers.append(((px, py, pc), 4 * px + 2 * py + pc))
    return 4 * x + 2 * y + c, peers


SEM_SPEC = pl.BlockSpec(memory_space=pltpu.SEMAPHORE)
ANY_SPEC = pl.BlockSpec(memory_space=pl.ANY)
EFFECT = pltpu.SideEffectType.DATAFLOW_SIDE_EFFECTING


def _scatters(scatter, k):
    return scatter if isinstance(scatter, bool) else scatter[k]


def _exchange_refs(scatter, src, land, send, recv, k, p, dev, idx, me):
    return pltpu.make_async_remote_copy(src_ref=src[k].at[idx] if _scatters(scatter, k) else src[k], dst_ref=land[k].at[me],
                                        send_sem=send.at[k * N_PEERS + p], recv_sem=recv.at[k * N_PEERS + p], device_id=dev,
                                        device_id_type=MESH)


def _exchange_start(name, srcs, scatter, gate):
    n = len(srcs)
    lands = [lax.empty(s.shape if _scatters(scatter, k) else (N_DEV,) + s.shape, s.dtype) for k, s in enumerate(srcs)]

    def body(*refs):
        src, land = refs[:n], refs[n:2 * n]
        send, recv, own = refs[2 * n + 1:2 * n + 4]
        token = refs[-1]
        me, peers = _mesh_place()
        for k in range(n):
            pltpu.make_async_copy(src[k].at[me] if _scatters(scatter, k) else src[k], land[k].at[me], own.at[k]).start()
            for p, (dev, idx) in enumerate(peers):
                _exchange_refs(scatter, src, land, send, recv, k, p, dev, idx, me).start()
        token[...] = jnp.zeros_like(token)

    hbm = lambda a: pltpu.HBM(a.shape, a.dtype)
    outs = pl.pallas_call(
        body, name=name,
        out_shape=(pltpu.SemaphoreType.DMA((n * N_PEERS,)), pltpu.SemaphoreType.DMA((n * N_PEERS,)),
                   pltpu.SemaphoreType.DMA((n,)), *[hbm(s) for s in srcs], *[hbm(s) for s in lands], _sds(DEP_SHAPE, F32)),
        in_specs=[HBM_SPEC] * (2 * n) + [ANY_SPEC],
        out_specs=(SEM_SPEC, SEM_SPEC, SEM_SPEC, *[HBM_SPEC] * (2 * n), pl.BlockSpec(memory_space=pltpu.VMEM)),
        input_output_aliases={j: 3 + j for j in range(2 * n)},
        compiler_params=pltpu.CompilerParams(has_side_effects=EFFECT),
    )(*[pltpu.with_memory_space_constraint(s, pltpu.HBM) for s in srcs],
      *[pltpu.with_memory_space_constraint(s, pltpu.HBM) for s in lands], gate)
    return outs[:3], None, list(outs[3:3 + n]), list(outs[3 + n:3 + 2 * n]), outs[-1]


def _exchange_wait(name, started, scatter, after):
    (send, recv, own), _, srcs, lands, _ = started
    n = len(srcs)

    def body(*refs):
        src, land = refs[:n], refs[n:2 * n]
        send, recv, own = refs[2 * n:2 * n + 3]
        me, peers = _mesh_place()
        for k in range(n):
            pltpu.make_async_copy(src[k].at[me] if _scatters(scatter, k) else src[k], land[k].at[me], own.at[k]).wait()
            for p, (dev, idx) in enumerate(peers):
                cp = pltpu.make_async_remote_copy(src_ref=src[k].at[idx] if _scatters(scatter, k) else src[k], dst_ref=land[k].at[idx],
                                                  send_sem=send.at[k * N_PEERS + p], recv_sem=recv.at[k * N_PEERS + p], device_id=dev,
                                                  device_id_type=MESH)
                cp.wait_send()
                cp.wait_recv()

    hbm = lambda a: pltpu.HBM(a.shape, a.dtype)
    outs = pl.pallas_call(
        body, name=name, out_shape=(*[hbm(s) for s in srcs], *[hbm(s) for s in lands]),
        in_specs=[HBM_SPEC] * (2 * n) + [SEM_SPEC, SEM_SPEC, SEM_SPEC, ANY_SPEC], out_specs=tuple([HBM_SPEC] * (2 * n)),
        input_output_aliases={j: j for j in range(2 * n)},
        compiler_params=pltpu.CompilerParams(has_side_effects=EFFECT),
    )(*srcs, *lands, send, recv, own, after)
    return list(outs[n:])


def _sum_devices(name, parts):
    def body(p_ref, o_ref):
        tot = p_ref[0]
        for j in range(1, N_DEV):
            tot = tot + p_ref[j]
        o_ref[...] = tot

    return pl.pallas_call(body, name=name, out_shape=_sds(parts.shape[1:], F32),
                          compiler_params=pltpu.CompilerParams(vmem_limit_bytes=VMEM_LIMIT))(parts)


def _rows(a, width=D_MODEL):
    flat = a.reshape(-1)
    return jnp.pad(flat, (0, (-flat.shape[0]) % width)).reshape(-1, width)


def _pack_rows(parts):
    blocks = []
    for p in parts:
        r = _rows(p)
        blocks.append(jnp.pad(r, ((0, (-r.shape[0]) % 8), (0, 0))))
    return jnp.concatenate(blocks, axis=0)


def _unpack_rows(rows, shapes):
    out, at = [], 0
    for s in shapes:
        size = int(np.prod(s))
        n = -(-size // D_MODEL)
        out.append(rows[at:at + n].reshape(-1)[:size].reshape(s))
        at += -(-n // 8) * 8
    return out


def kernel(x, norm_mix, norm_ffn, norm_final, attn_w_in, attn_w_out, attn_sinks, hgrn_w_in, hgrn_w_out, hgrn_norm, hgrn_lb_logits, ffn_w_up, ffn_conv_w, ffn_conv_b, ffn_w_down, loss_target, m_norm_mix, m_norm_ffn, m_norm_final, m_attn_w_in, m_attn_w_out, m_attn_sinks, m_hgrn_w_in, m_hgrn_w_out, m_hgrn_norm, m_hgrn_lb_logits, m_ffn_w_up, m_ffn_conv_w, m_ffn_conv_b, m_ffn_w_down, v_norm_mix, v_norm_ffn, v_norm_final, v_attn_w_in, v_attn_w_out, v_attn_sinks, v_hgrn_w_in, v_hgrn_w_out, v_hgrn_norm, v_hgrn_lb_logits, v_ffn_w_up, v_ffn_conv_w, v_ffn_conv_b, v_ffn_w_down):
    S = x.shape[1]
    n_attn, n_hgrn = attn_w_in.shape[0], hgrn_w_in.shape[0]

    wa_in_t, wa_out_b = attn_w_in.transpose(0, 2, 1).astype(BF16), attn_w_out.astype(BF16)
    wh_in_b, wh_out_b = hgrn_w_in.astype(BF16), hgrn_w_out.astype(BF16)
    wf_up_b, wf_down_b = ffn_w_up.transpose(0, 2, 1).astype(BF16), ffn_w_down.astype(BF16)
    conv_b = ffn_conv_b.reshape(DEPTH, 2, 4, 1, FF_SLOT)
    lb = _lb_fwd("lb_fwd", hgrn_lb_logits)

    def unit_shards(l, part):
        if part == "ffn":
            return [wf_up_b[l], wf_down_b[l], ffn_conv_w[l]]
        return [wa_in_t[l // 2], wa_out_b[l // 2]] if l % 2 == 0 else [wh_in_b[l // 2], wh_out_b[l // 2]]

    def unit_weights(l, part, w):
        if part == "ffn":
            return w[0][None], w[1].reshape(1, 4, FF_SLOT, D_MODEL), w[2].reshape(2, 4, 3, FF_SLOT)
        if l % 2 == 0:
            return w[0].reshape(1, ATTN_IN, D_MODEL), w[1].reshape(1, D_MODEL, D_MODEL)
        return w[0][None], w[1].reshape(1, D_MODEL, D_MODEL)

    units = [(l, part) for l in range(DEPTH) for part in ("mix", "ffn")]
    gathers = [_exchange_start("gather_start0", unit_shards(*units[0]), False, norm_final)]
    gathers.append(_exchange_start("gather_start1", unit_shards(*units[1]), False, gathers[0][4]))
    arrived = _exchange_wait("gather_wait0", gathers[0], False, gathers[1][4])
    weights, saved = {}, [dict() for _ in range(DEPTH)]
    h = x[0]
    hn = _rmsnorm_fwd("norm_mix_fwd0", h, norm_mix[0:1])
    for n, (l, part) in enumerate(units):
        i, sv = l // 2, saved[l]
        weights[l, part] = w = unit_weights(l, part, arrived)
        dep = None
        if n + 2 < len(units):
            gathers.append(_exchange_start(f"gather_start{n + 2}", unit_shards(*units[n + 2]), False, arrived[0]))
            dep = gathers[n + 2][4]
        if part == "mix":
            sv["h"], sv["hn"] = h, hn
            if l % 2 == 0:
                sv["proj"] = _proj_rows(f"attn_proj{i}", hn, w[0], 0, BF16, dep)
                sv["o"], *sv["kept"] = _attn_fwd(f"attn_fwd{i}", sv["proj"], attn_sinks[i:i + 1])
                h, hn = _out_proj(f"attn_out{i}", sv["o"], w[1], 0, h, norm_ffn[l:l + 1])
            else:
                sv["z"] = _proj_slots(f"hgrn_proj{i}", hn, w[0], 0, dep=dep).reshape(4, 2, S, HG_SLOT)
                sv["o"], *sv["kept"] = _hg_fwd(f"hgrn_fwd{i}", sv["z"], lb[i:i + 1], hgrn_norm[i:i + 1])
                h, hn = _out_proj(f"hgrn_out{i}", sv["o"], w[1], 0, h, norm_ffn[l:l + 1])
        else:
            sv["h2"], sv["hn2"] = h, hn
            sv["u"] = _proj_slots(f"ffn_up{l}", hn, w[0], 0, True, dep, BF16).reshape(2, 4, S, FF_SLOT)
            sv["a"], sv["convs"] = _convgate_fwd(f"ffn_gate{l}", sv["u"], w[2], conv_b[l])
            if l + 1 < DEPTH:
                h, hn = _down_proj(f"ffn_down{l}", sv["a"], w[1], 0, h, norm_mix[l + 1:l + 2])
            else:
                h = _down_proj(f"ffn_down{l}", sv["a"], w[1], 0, h)
        if n + 1 < len(units):
            arrived = _exchange_wait(f"gather_wait{n + 1}", gathers[n + 1], False, h)
    dh, d_norm_final, loss_rows, dhb = _loss_head("loss_head", h, norm_final[None], loss_target[0])

    d_conv_w, d_conv_b, d_norm_mix, d_norm_ffn = [None] * DEPTH, [None] * DEPTH, [None] * DEPTH, [None] * DEPTH
    d_sinks, d_lb, d_hgrn_norm = [None] * n_attn, [None] * n_hgrn, [None] * n_hgrn
    received, pending = {}, []
    for l, part in reversed(units):
        i, sv, w = l // 2, saved[l], weights[l, part]
        dep = pending[-1][1][4] if pending else None
        if part == "ffn":
            da = _dgrad_down(f"ffn_down_dgrad{l}", dhb, w[1], 0, dep)
            g_down = _wgrad_down(f"ffn_down_wgrad{l}", sv["a"], dhb).reshape(N_DEV, D_FF // N_DEV, D_MODEL)
            du, d_conv_w[l], d_conv_b[l] = _convgate_bwd(f"ffn_gate_bwd{l}", sv["u"], sv["convs"], w[2], da)
            du = du.reshape(N_DEV, S, FF_SLOT)
            grads = [_wgrad_slots(f"ffn_up_wgrad{l}", sv["hn2"], du, True), g_down]
            dh, d_norm_ffn[l], dhb = _dgrad_slots(f"ffn_up_dgrad{l}", du, w[0], 0, (sv["h2"], norm_ffn[l:l + 1], dh), True)
        else:
            if l % 2 == 0:
                do = _dgrad_out(f"attn_out_dgrad{i}", dhb, w[1], 0, BF16, dep)
                g_out = _wgrad_rows(f"attn_out_wgrad{i}", sv["o"], dhb)
                dproj, d_sinks[i] = _attn_bwd(f"attn_bwd{i}", sv["proj"], *sv["kept"], do)
                g_in = _wgrad_rows(f"attn_proj_wgrad{i}", dproj, sv["hn"]).reshape(N_DEV, ATTN_IN // N_DEV, D_MODEL)
                dh_new = _dgrad_rows(f"attn_proj_dgrad{i}", dproj, w[0], 0, (sv["h"], norm_mix[l:l + 1], dh))
            else:
                dog = _dgrad_out(f"hgrn_out_dgrad{i}", dhb, w[1], 0, F32, dep)
                g_out = _wgrad_rows(f"hgrn_out_wgrad{i}", sv["o"], dhb)
                dz, d_lb[i], dng = _hg_bwd(f"hgrn_bwd{i}", sv["z"], lb[i:i + 1], hgrn_norm[i:i + 1], *sv["kept"], dog)
                d_hgrn_norm[i] = dng[0] + dng[1]
                dz = dz.reshape(N_DEV, S, HG_SLOT)
                g_in = _wgrad_slots(f"hgrn_proj_wgrad{i}", sv["hn"], dz)
                dh_new = _dgrad_slots(f"hgrn_proj_dgrad{i}", dz, w[0], 0, (sv["h"], norm_mix[l:l + 1], dh))
            grads = [g_in, g_out.reshape(N_DEV, D_MODEL // N_DEV, D_MODEL)]
            dh, d_norm_mix[l], dhb = dh_new
        gate = dh
        if len(pending) == 2:
            key, oldest = pending.pop(0)
            received[key] = _exchange_wait(f"scatter_wait_{key[1]}{key[0]}", oldest, True, dh)
            gate = received[key][0]
        pending.append(((l, part), _exchange_start(f"scatter_start_{part}{l}", grads, True, gate)))
    grad_x = dh[None]

    small_shapes = [(DEPTH, D_MODEL), (DEPTH, D_MODEL), (1, D_MODEL), (1, D_MODEL), (n_hgrn, D_MODEL), (n_attn, 128),
                    (n_hgrn, HG_K), (DEPTH, 2 * D_FF)]
    partial = _pack_rows([
        jnp.concatenate(d_norm_mix), jnp.concatenate(d_norm_ffn), d_norm_final, loss_rows, jnp.concatenate(d_lb),
        jnp.concatenate(d_sinks), jnp.concatenate(d_hgrn_norm), jnp.stack(d_conv_b)])
    d_taps = jnp.stack(d_conv_w).reshape(DEPTH, N_DEV, 3, FF_SLOT).transpose(1, 0, 2, 3).reshape(N_DEV, DEPTH * 3, FF_SLOT)
    small_started = _exchange_start("small_start", [partial, d_taps], (False, True), pending[-1][1][4])
    attn_layers, hgrn_layers = range(0, DEPTH, 2), range(1, DEPTH, 2)

    def transposed(ts):
        return [t.transpose(0, 2, 1) for t in ts]

    big = {"hgrn_w_in": _adamw("adamw_hgrn_in", [received[l, "mix"][0] for l in hgrn_layers], hgrn_w_in, m_hgrn_w_in,
                               v_hgrn_w_in, dep=small_started[4])}
    big["hgrn_w_out"] = _adamw("adamw_hgrn_out", [received[l, "mix"][1] for l in hgrn_layers], hgrn_w_out, m_hgrn_w_out, v_hgrn_w_out)
    key, oldest = pending.pop(0)
    received[key] = _exchange_wait(f"scatter_wait_{key[1]}{key[0]}", oldest, True, big["hgrn_w_in"][3])
    up_t = _adamw("adamw_ffn_up", [received[l, "ffn"][0] for l in range(DEPTH)], *transposed((ffn_w_up, m_ffn_w_up, v_ffn_w_up)))
    big["ffn_w_up"] = transposed(up_t)
    big["ffn_w_down"] = _adamw("adamw_ffn_down", [received[l, "ffn"][1] for l in range(DEPTH)], ffn_w_down, m_ffn_w_down, v_ffn_w_down)
    key, oldest = pending.pop(0)
    received[key] = _exchange_wait(f"scatter_wait_{key[1]}{key[0]}", oldest, True, up_t[3])
    small_parts, taps_parts = _exchange_wait("small_wait", small_started, (False, True), up_t[3])
    total = _sum_devices("sum_small", small_parts)
    (g_norm_mix, g_norm_ffn, g_norm_final, loss_sum, g_lb, g_sinks, g_hgrn_norm, g_conv_b) = _unpack_rows(total, small_shapes)

    loss = jnp.sum(loss_sum)
    g_norm_final = g_norm_final[0]
    g_sinks = g_sinks[:, :N_Q_HEADS]
    g_lb_logits = _lb_bwd("lb_bwd", hgrn_lb_logits, g_lb)

    big.update({
        "attn_w_in": transposed(_adamw("adamw_attn_in", [received[l, "mix"][0] for l in attn_layers],
                                       *transposed((attn_w_in, m_attn_w_in, v_attn_w_in)))),
        "attn_w_out": _adamw("adamw_attn_out", [received[l, "mix"][1] for l in attn_layers], attn_w_out, m_attn_w_out, v_attn_w_out),
        "ffn_conv_w": _adamw("adamw_conv_w", [taps_parts[:, 3 * l:3 * l + 3] for l in range(DEPTH)], ffn_conv_w, m_ffn_conv_w,
                             v_ffn_conv_w),
    })
    small_w = [norm_mix, norm_ffn, norm_final, attn_sinks, hgrn_norm, hgrn_lb_logits, ffn_conv_b]
    small_m = [m_norm_mix, m_norm_ffn, m_norm_final, m_attn_sinks, m_hgrn_norm, m_hgrn_lb_logits, m_ffn_conv_b]
    small_v = [v_norm_mix, v_norm_ffn, v_norm_final, v_attn_sinks, v_hgrn_norm, v_hgrn_lb_logits, v_ffn_conv_b]
    small_g = [g_norm_mix, g_norm_ffn, g_norm_final, g_sinks, g_hgrn_norm, g_lb_logits, g_conv_b]
    outs = _adamw("adamw_small", [_pack_rows(small_g)[None]], *[_pack_rows(t)[None] for t in (small_w, small_m, small_v)])
    outs = [o[0] for o in outs]
    shapes = [w.shape for w in small_w]
    small = {n: [t[j] for t in [_unpack_rows(o, shapes) for o in outs]]
             for j, n in enumerate(["norm_mix", "norm_ffn", "norm_final", "attn_sinks", "hgrn_norm", "hgrn_lb_logits", "ffn_conv_b"])}
    order = ["norm_mix", "norm_ffn", "norm_final", "attn_w_in", "attn_w_out", "attn_sinks", "hgrn_w_in", "hgrn_w_out",
             "hgrn_norm", "hgrn_lb_logits", "ffn_w_up", "ffn_conv_w", "ffn_conv_b", "ffn_w_down"]
    res = {**big, **small}
    return (loss, grad_x, *[res[n][0] for n in order], *[res[n][1] for n in order], *[res[n][2] for n in order],
            *[res[n][3] for n in order])
```

```python
import numpy as np
import jax
import jax.numpy as jnp
from jax import lax
from jax.experimental import pallas as pl
from jax.experimental.pallas import tpu as pltpu

F32 = jnp.float32
BF16 = jnp.bfloat16

D_MODEL = 1024
DEPTH = 4
HEAD_DIM = 64
N_Q_HEADS = 16
N_KV_HEADS = 4
Q_PER_KV = 4
ATTN_BLOCK = 128
ATTN_IN = 1536
HG_HEADS = 8
HG_K = 128
HG_CHUNK = 64
HG_IN = 4096
D_FF = 2816
EPS = 1e-6
N_DEV = 8
FF_SLOT = 2 * D_FF // N_DEV
HG_SLOT = HG_IN // N_DEV
HG_LEVELS = 6

ADAM_LR = 0.001
ADAM_B1 = 0.9
ADAM_B2 = 0.999
ADAM_EPS = 1e-08
ADAM_WD = 0.01
ADAM_STEP = 10

VMEM_LIMIT = 56 * 1024 * 1024
ROW_TILE = 1024
WIDE_ROW_TILE = 2048
NEG_BIG = -1e30

NN = (((1,), (0,)), ((), ()))
NT = (((1,), (1,)), ((), ()))
TN = (((0,), (0,)), ((), ()))


def _bdot(a, b, dn):
    return lax.dot_general(a.astype(BF16), b.astype(BF16), dn, preferred_element_type=F32)


def _sds(shape, dtype):
    return jax.ShapeDtypeStruct(tuple(shape), dtype)


def _params(sem):
    return pltpu.CompilerParams(dimension_semantics=sem, vmem_limit_bytes=VMEM_LIMIT)


DEP_SHAPE = (8, 128)


def _dep_spec(rank):
    return pl.BlockSpec(DEP_SHAPE, lambda *_: (0, 0))


def _matmul(name, a, b, *, dn, grid, a_spec, b_spec, o_spec, out_shape, acc_shape=None, extra=(), extra_specs=(),
            finish=None, dep=None, sem=("parallel", "parallel", "arbitrary")):
    nk = grid[2]
    many = isinstance(out_shape, (list, tuple))
    n_in = 2 + len(extra) + (dep is not None)
    n_out = len(out_shape) if many else 1

    def body(*refs):
        a_ref, b_ref = refs[0], refs[1]
        outs = refs[n_in:n_in + n_out]

        def prod():
            return _bdot(a_ref[...], b_ref[...], dn)

        def done(v):
            if finish is None:
                outs[0][...] = v.astype(outs[0].dtype)
            else:
                finish(v, refs[2:2 + len(extra)], outs)

        if nk == 1:
            done(prod())
        else:
            acc = refs[-1]
            k = pl.program_id(2)

            @pl.when(k == 0)
            def _():
                acc[...] = prod()

            @pl.when(k > 0)
            def _():
                acc[...] += prod()

            @pl.when(k == nk - 1)
            def _():
                done(acc[...])

    in_specs = [a_spec, b_spec, *extra_specs] + ([_dep_spec(3)] if dep is not None else [])
    args = (a, b, *extra) + ((dep,) if dep is not None else ())
    scratch = [] if nk == 1 else [pltpu.VMEM(acc_shape, F32)]
    return pl.pallas_call(
        body, name=name, grid=grid, in_specs=in_specs, out_specs=o_spec, out_shape=out_shape,
        scratch_shapes=scratch, compiler_params=_params(sem),
    )(*args)


def _rms(x):
    return lax.rsqrt(jnp.mean(x * x, axis=-1, keepdims=True) + EPS)


def _residual_finish(v, ex, outs):
    h = v + ex[0][...]
    outs[0][...] = h
    if len(ex) > 1:
        outs[1][...] = (h * _rms(h) * ex[1][...]).astype(outs[1].dtype)


def _norm_bwd_finish(v, ex, outs):
    x = ex[0][...]
    r = _rms(x)
    xh = x * r
    dyg = v * ex[1][...]
    dh = ex[2][...] + r * (dyg - xh * jnp.mean(dyg * xh, axis=-1, keepdims=True))
    outs[0][...] = dh
    outs[2][...] = dh.astype(outs[2].dtype)
    part = jnp.sum(v * xh, axis=0, keepdims=True)

    @pl.when(pl.program_id(0) == 0)
    def _():
        outs[1][...] = part

    @pl.when(pl.program_id(0) > 0)
    def _():
        outs[1][...] += part


def _row_io(tm, norm_g):
    row = pl.BlockSpec((tm, D_MODEL), lambda i, j, k: (i, 0))
    vec = pl.BlockSpec((1, D_MODEL), lambda i, j, k: (0, 0))
    if norm_g is None:
        return (row,), row, lambda S: _sds((S, D_MODEL), F32)
    return (row, vec), [row, row], lambda S: [_sds((S, D_MODEL), F32), _sds((S, D_MODEL), BF16)]


def _tile(n, t):
    return min(n, t)


def _proj_rows(name, hn, wt, l, out_dtype, dep=None):
    S, N = hn.shape[0], wt.shape[1]
    tm, tn = _tile(S, WIDE_ROW_TILE), 512
    return _matmul(
        name, hn, wt, dn=NT, grid=(S // tm, N // tn, 1),
        a_spec=pl.BlockSpec((tm, D_MODEL), lambda i, j, k: (i, 0)),
        b_spec=pl.BlockSpec((None, tn, D_MODEL), lambda i, j, k: (l, j, 0)),
        o_spec=pl.BlockSpec((tm, tn), lambda i, j, k: (i, j)),
        out_shape=_sds((S, N), out_dtype), dep=dep)


def _slot_weight(w, transposed):
    if transposed:
        return w.shape[2], (None, None, w.shape[2], D_MODEL), NT, NN
    return w.shape[3], (None, None, D_MODEL, w.shape[3]), NN, NT


def _proj_slots(name, hn, w, l, transposed=False, dep=None, out_dtype=F32):
    S = hn.shape[0]
    r, blk, dn, _ = _slot_weight(w, transposed)
    tm = _tile(S, WIDE_ROW_TILE)
    return _matmul(
        name, hn, w, dn=dn, grid=(N_DEV, S // tm, 1),
        a_spec=pl.BlockSpec((tm, D_MODEL), lambda j, i, k: (i, 0)),
        b_spec=pl.BlockSpec(blk, lambda j, i, k: (l, j, 0, 0)),
        o_spec=pl.BlockSpec((None, tm, r), lambda j, i, k: (j, i, 0)),
        out_shape=_sds((N_DEV, S, r), out_dtype), dep=dep)


def _out_proj(name, o, w, l, h, norm_g=None):
    S, K = o.shape
    tm = _tile(S, ROW_TILE)
    extra_specs, o_spec, out_shape = _row_io(tm, norm_g)
    return _matmul(
        name, o, w, dn=NN, grid=(S // tm, 1, 1),
        a_spec=pl.BlockSpec((tm, K), lambda i, j, k: (i, 0)),
        b_spec=pl.BlockSpec((None, K, D_MODEL), lambda i, j, k: (l, 0, 0)),
        o_spec=o_spec, out_shape=out_shape(S), extra=(h,) if norm_g is None else (h, norm_g),
        extra_specs=extra_specs, finish=_residual_finish)


def _down_proj(name, a, w, l, h, norm_g=None):
    nj, S, r = a.shape
    tm = _tile(S, ROW_TILE)
    extra_specs, o_spec, out_shape = _row_io(tm, norm_g)
    return _matmul(
        name, a, w, dn=NN, grid=(S // tm, 1, nj),
        a_spec=pl.BlockSpec((None, tm, r), lambda i, j, k: (k, i, 0)),
        b_spec=pl.BlockSpec((None, None, r, D_MODEL), lambda i, j, k: (l, k, 0, 0)),
        o_spec=o_spec, out_shape=out_shape(S), acc_shape=(tm, D_MODEL),
        extra=(h,) if norm_g is None else (h, norm_g), extra_specs=extra_specs, finish=_residual_finish)


def _dgrad_down(name, dh, w, l, dep=None):
    S = dh.shape[0]
    nj, r = w.shape[1], w.shape[2]
    tm = _tile(S, WIDE_ROW_TILE)
    return _matmul(
        name, dh, w, dn=NT, grid=(nj, S // tm, 1),
        a_spec=pl.BlockSpec((tm, D_MODEL), lambda j, i, k: (i, 0)),
        b_spec=pl.BlockSpec((None, None, r, D_MODEL), lambda j, i, k: (l, j, 0, 0)),
        o_spec=pl.BlockSpec((None, tm, r), lambda j, i, k: (j, i, 0)),
        out_shape=_sds((nj, S, r), BF16), dep=dep)


def _wgrad_down(name, a, dh):
    nj, S, r = a.shape
    tk = S
    return _matmul(
        name, a, dh, dn=TN, grid=(nj, 1, S // tk),
        a_spec=pl.BlockSpec((None, tk, r), lambda s, j, k: (s, k, 0)),
        b_spec=pl.BlockSpec((tk, D_MODEL), lambda s, j, k: (k, 0)),
        o_spec=pl.BlockSpec((None, r, D_MODEL), lambda s, j, k: (s, 0, 0)),
        out_shape=_sds((nj, r, D_MODEL), BF16), acc_shape=(r, D_MODEL))


def _norm_bwd_io(tm, S):
    row = pl.BlockSpec((tm, D_MODEL), lambda i, j, k: (i, 0))
    vec = pl.BlockSpec((1, D_MODEL), lambda i, j, k: (0, 0))
    return dict(extra_specs=(row, vec, row), o_spec=[row, vec, row],
                out_shape=[_sds((S, D_MODEL), F32), _sds((1, D_MODEL), F32), _sds((S, D_MODEL), BF16)],
                finish=_norm_bwd_finish, sem=("arbitrary", "arbitrary", "arbitrary"))


def _dgrad_slots(name, dz, w, l, norm, transposed=False):
    nj, S, r = dz.shape
    _, blk, _, dn = _slot_weight(w, transposed)
    tm = _tile(S, ROW_TILE)
    return _matmul(
        name, dz, w, dn=dn, grid=(S // tm, 1, nj),
        a_spec=pl.BlockSpec((None, tm, r), lambda i, j, k: (k, i, 0)),
        b_spec=pl.BlockSpec(blk, lambda i, j, k: (l, k, 0, 0)),
        acc_shape=(tm, D_MODEL), extra=norm, **_norm_bwd_io(tm, S))


def _wgrad_slots(name, hn, dz, transposed=False):
    nj, S, r = dz.shape
    tk = S
    hn_spec = pl.BlockSpec((tk, D_MODEL), lambda s, j, k: (k, 0))
    dz_spec = pl.BlockSpec((None, tk, r), lambda s, j, k: (s, k, 0))
    if transposed:
        return _matmul(
            name, dz, hn, dn=TN, grid=(nj, 1, S // tk), a_spec=dz_spec, b_spec=hn_spec,
            o_spec=pl.BlockSpec((None, r, D_MODEL), lambda s, j, k: (s, 0, 0)),
            out_shape=_sds((nj, r, D_MODEL), BF16), acc_shape=(r, D_MODEL))
    return _matmul(
        name, hn, dz, dn=TN, grid=(nj, 1, S // tk), a_spec=hn_spec, b_spec=dz_spec,
        o_spec=pl.BlockSpec((None, D_MODEL, r), lambda s, j, k: (s, 0, 0)),
        out_shape=_sds((nj, D_MODEL, r), BF16), acc_shape=(D_MODEL, r))


def _dgrad_out(name, dh, w, l, out_dtype, dep=None):
    S, K = dh.shape[0], w.shape[1]
    tm = _tile(S, WIDE_ROW_TILE)
    return _matmul(
        name, dh, w, dn=NT, grid=(S // tm, 1, 1),
        a_spec=pl.BlockSpec((tm, D_MODEL), lambda i, j, k: (i, 0)),
        b_spec=pl.BlockSpec((None, K, D_MODEL), lambda i, j, k: (l, 0, 0)),
        o_spec=pl.BlockSpec((tm, K), lambda i, j, k: (i, 0)),
        out_shape=_sds((S, K), out_dtype), dep=dep)


def _wgrad_rows(name, a, b):
    S, K = a.shape
    tk = S
    return _matmul(
        name, a, b, dn=TN, grid=(1, 1, S // tk),
        a_spec=pl.BlockSpec((tk, K), lambda i, j, k: (k, 0)),
        b_spec=pl.BlockSpec((tk, D_MODEL), lambda i, j, k: (k, 0)),
        o_spec=pl.BlockSpec((K, D_MODEL), lambda i, j, k: (0, 0)),
        out_shape=_sds((K, D_MODEL), BF16), acc_shape=(K, D_MODEL))


def _dgrad_rows(name, dz, wt, l, norm):
    S, N = dz.shape
    tm = _tile(S, ROW_TILE)
    return _matmul(
        name, dz, wt, dn=NN, grid=(S // tm, 1, 1),
        a_spec=pl.BlockSpec((tm, N), lambda i, j, k: (i, 0)),
        b_spec=pl.BlockSpec((None, N, D_MODEL), lambda i, j, k: (l, 0, 0)),
        extra=norm, **_norm_bwd_io(tm, S))


def _rmsnorm_fwd(name, h, g):
    S = h.shape[0]
    tm = _tile(S, ROW_TILE)

    def body(h_ref, g_ref, o_ref):
        x = h_ref[...]
        o_ref[...] = (x * _rms(x) * g_ref[...]).astype(o_ref.dtype)

    row = pl.BlockSpec((tm, D_MODEL), lambda i: (i, 0))
    return pl.pallas_call(
        body, name=name, grid=(S // tm,), in_specs=[row, pl.BlockSpec((1, D_MODEL), lambda i: (0, 0))],
        out_specs=row, out_shape=_sds((S, D_MODEL), BF16), compiler_params=_params(("parallel",)),
    )(h, g)


def _loss_head(name, h, g, target):
    S = h.shape[0]
    tm = _tile(S, ROW_TILE)

    def body(h_ref, g_ref, t_ref, dh_ref, dg_ref, ls_ref, dhb_ref):
        x = h_ref[...]
        r = _rms(x)
        xh = x * r
        diff = xh * g_ref[...] - t_ref[...]
        dyf = diff * (1.0 / D_MODEL)
        dyg = dyf * g_ref[...]
        dh = r * (dyg - xh * jnp.mean(dyg * xh, axis=-1, keepdims=True))
        dh_ref[...] = dh
        dhb_ref[...] = dh.astype(dhb_ref.dtype)
        part = jnp.sum(dyf * xh, axis=0, keepdims=True)
        lpart = jnp.sum(diff * diff, axis=0, keepdims=True) * (0.5 / D_MODEL)

        @pl.when(pl.program_id(0) == 0)
        def _():
            dg_ref[...] = part
            ls_ref[...] = lpart

        @pl.when(pl.program_id(0) > 0)
        def _():
            dg_ref[...] += part
            ls_ref[...] += lpart

    row = pl.BlockSpec((tm, D_MODEL), lambda i: (i, 0))
    vec = pl.BlockSpec((1, D_MODEL), lambda i: (0, 0))
    return pl.pallas_call(
        body, name=name, grid=(S // tm,), in_specs=[row, vec, row], out_specs=[row, vec, vec, row],
        out_shape=[_sds((S, D_MODEL), F32), _sds((1, D_MODEL), F32), _sds((1, D_MODEL), F32), _sds((S, D_MODEL), BF16)],
        compiler_params=_params(("arbitrary",)),
    )(h, g, target)


ATTN_SCALE = HEAD_DIM ** -0.5
ALIBI_SLOPES = [2.0 ** (-8.0 * (h + 1) / N_Q_HEADS) for h in range(N_Q_HEADS)]
K_COL = N_Q_HEADS * HEAD_DIM
KV_COLS = N_KV_HEADS * HEAD_DIM
V_COL = K_COL + KV_COLS


def _attn_masks(n):
    qi = lax.broadcasted_iota(jnp.int32, (ATTN_BLOCK, ATTN_BLOCK), 0)
    ki = lax.broadcasted_iota(jnp.int32, (ATTN_BLOCK, ATTN_BLOCK), 1)
    dist_c = (qi - ki).astype(F32)
    return dist_c + float(ATTN_BLOCK), dist_c, (ki > qi) & (n > 0), qi >= ki


def _attn_probs(raw_p, raw_c, sink, slope, masks):
    dist_p, dist_c, valid_p, valid_c = masks
    sp = jnp.where(valid_p, raw_p * ATTN_SCALE - slope * dist_p, NEG_BIG)
    sc = jnp.where(valid_c, raw_c * ATTN_SCALE - slope * dist_c, NEG_BIG)
    m = jnp.maximum(jnp.maximum(jnp.max(sp, axis=-1, keepdims=True), jnp.max(sc, axis=-1, keepdims=True)), sink)
    ep, ec, es = jnp.exp(sp - m), jnp.exp(sc - m), jnp.exp(sink - m)
    inv = 1.0 / (jnp.sum(ep, axis=-1, keepdims=True) + jnp.sum(ec, axis=-1, keepdims=True) + es)
    return ep * inv, ec * inv, es * inv


def _group_rows(ref, m):
    return jnp.concatenate([ref[:, HEAD_DIM * (Q_PER_KV * m + g):HEAD_DIM * (Q_PER_KV * m + g + 1)]
                            for g in range(Q_PER_KV)], axis=0)


def _head_rows(x, g):
    return x[ATTN_BLOCK * g:ATTN_BLOCK * (g + 1)]


def _attn_specs(nblk):
    last = nblk - 1
    kcol, vcol = K_COL // KV_COLS, V_COL // KV_COLS
    return [
        pl.BlockSpec((ATTN_BLOCK, K_COL), lambda n: (jnp.minimum(n, last), 0)),
        pl.BlockSpec((ATTN_BLOCK, KV_COLS), lambda n: (jnp.minimum(n, last), kcol)),
        pl.BlockSpec((ATTN_BLOCK, KV_COLS), lambda n: (jnp.maximum(jnp.minimum(n, last) - 1, 0), kcol)),
        pl.BlockSpec((ATTN_BLOCK, KV_COLS), lambda n: (jnp.minimum(n, last), vcol)),
        pl.BlockSpec((ATTN_BLOCK, KV_COLS), lambda n: (jnp.maximum(jnp.minimum(n, last) - 1, 0), vcol)),
    ]


P_COLS = 2 * ATTN_BLOCK


def _attn_fwd(name, proj, sinks):
    S = proj.shape[0]
    nblk = S // ATTN_BLOCK

    def body(q_ref, kc_ref, kp_ref, vc_ref, vp_ref, sk_ref, o_ref, p_ref, ps_ref):
        masks = _attn_masks(pl.program_id(0))
        lane = lax.broadcasted_iota(jnp.int32, (ATTN_BLOCK, 128), 1)
        sink_p = jnp.zeros((ATTN_BLOCK, 128), F32)
        for m in range(N_KV_HEADS):
            ks = slice(HEAD_DIM * m, HEAD_DIM * (m + 1))
            kp, kc, vp, vc = kp_ref[:, ks], kc_ref[:, ks], vp_ref[:, ks], vc_ref[:, ks]
            q4 = _group_rows(q_ref, m)
            raw_p, raw_c = _bdot(q4, kp, NT), _bdot(q4, kc, NT)
            pps, pcs = [], []
            for g in range(Q_PER_KV):
                hh = Q_PER_KV * m + g
                pp, pc, ps = _attn_probs(_head_rows(raw_p, g), _head_rows(raw_c, g), sk_ref[0, hh], ALIBI_SLOPES[hh], masks)
                pps.append(pp.astype(BF16))
                pcs.append(pc.astype(BF16))
                p_ref[:, P_COLS * hh:P_COLS * hh + ATTN_BLOCK] = pps[g]
                p_ref[:, P_COLS * hh + ATTN_BLOCK:P_COLS * (hh + 1)] = pcs[g]
                sink_p = jnp.where(lane == hh, ps, sink_p)
            o4 = _bdot(jnp.concatenate(pps, axis=0), vp, NN) + _bdot(jnp.concatenate(pcs, axis=0), vc, NN)
            for g in range(Q_PER_KV):
                hh = Q_PER_KV * m + g
                o_ref[:, HEAD_DIM * hh:HEAD_DIM * (hh + 1)] = _head_rows(o4, g).astype(o_ref.dtype)
        ps_ref[...] = sink_p

    row = lambda cols: pl.BlockSpec((ATTN_BLOCK, cols), lambda n: (n, 0))
    return pl.pallas_call(
        body, name=name, grid=(nblk,),
        in_specs=_attn_specs(nblk) + [pl.BlockSpec(memory_space=pltpu.SMEM)],
        out_specs=[row(K_COL), row(N_Q_HEADS * P_COLS), row(128)],
        out_shape=[_sds((S, K_COL), BF16), _sds((S, N_Q_HEADS * P_COLS), BF16), _sds((S, 128), F32)],
        compiler_params=_params(("parallel",)),
    )(proj, proj, proj, proj, proj, sinks)


def _attn_bwd(name, proj, probs, sink_probs, do):
    S = proj.shape[0]
    nblk = S // ATTN_BLOCK

    def body(q_ref, kc_ref, kp_ref, vc_ref, vp_ref, do_ref, p_ref, ps_ref, dz_ref, ds_ref, carry, cur, padd):
        n = pl.program_id(0)

        @pl.when(n == 0)
        def _():
            carry[...] = jnp.zeros_like(carry)
            ds_ref[...] = jnp.zeros_like(ds_ref)

        @pl.when(n < nblk)
        def _():
            lane = lax.broadcasted_iota(jnp.int32, (ATTN_BLOCK, 128), 1)
            sink_p = ps_ref[...]
            dsv = jnp.zeros((1, 128), F32)
            for m in range(N_KV_HEADS):
                ks = slice(HEAD_DIM * m, HEAD_DIM * (m + 1))
                kp, kc, vp, vc = kp_ref[:, ks], kc_ref[:, ks], vp_ref[:, ks], vc_ref[:, ks]
                q4, do4 = _group_rows(q_ref, m), _group_rows(do_ref, m)
                dpp4, dpc4 = _bdot(do4, vp, NT), _bdot(do4, vc, NT)
                pps, pcs, dsps, dscs = [], [], [], []
                for g in range(Q_PER_KV):
                    hh = Q_PER_KV * m + g
                    pps.append(p_ref[:, P_COLS * hh:P_COLS * hh + ATTN_BLOCK])
                    pcs.append(p_ref[:, P_COLS * hh + ATTN_BLOCK:P_COLS * (hh + 1)])
                    pp, pc = pps[g].astype(F32), pcs[g].astype(F32)
                    dpp, dpc = _head_rows(dpp4, g), _head_rows(dpc4, g)
                    delta = jnp.sum(pp * dpp, axis=-1, keepdims=True) + jnp.sum(pc * dpc, axis=-1, keepdims=True)
                    dsv = dsv - jnp.sum(jnp.where(lane == hh, sink_p, 0.0) * delta, axis=0, keepdims=True)
                    dsps.append((pp * (dpp - delta)).astype(BF16))
                    dscs.append((pc * (dpc - delta)).astype(BF16))
                pp4, pc4 = jnp.concatenate(pps, axis=0), jnp.concatenate(pcs, axis=0)
                dsp4, dsc4 = jnp.concatenate(dsps, axis=0), jnp.concatenate(dscs, axis=0)
                dq4 = (_bdot(dsp4, kp, NN) + _bdot(dsc4, kc, NN)) * ATTN_SCALE
                for g in range(Q_PER_KV):
                    hh = Q_PER_KV * m + g
                    cur[:, HEAD_DIM * hh:HEAD_DIM * (hh + 1)] = _head_rows(dq4, g)
                cur[:, K_COL + HEAD_DIM * m:K_COL + HEAD_DIM * (m + 1)] = _bdot(dsc4, q4, TN) * ATTN_SCALE
                cur[:, V_COL + HEAD_DIM * m:V_COL + HEAD_DIM * (m + 1)] = _bdot(pc4, do4, TN)
                padd[:, ks] = _bdot(dsp4, q4, TN) * ATTN_SCALE
                padd[:, KV_COLS + HEAD_DIM * m:KV_COLS + HEAD_DIM * (m + 1)] = _bdot(pp4, do4, TN)
            ds_ref[...] += dsv
            dz_ref[:, :K_COL] = carry[:, :K_COL].astype(dz_ref.dtype)
            dz_ref[:, K_COL:] = (carry[:, K_COL:] + padd[...]).astype(dz_ref.dtype)
            carry[...] = cur[...]

        @pl.when(n == nblk)
        def _():
            dz_ref[...] = carry[...].astype(dz_ref.dtype)

    return pl.pallas_call(
        body, name=name, grid=(nblk + 1,),
        in_specs=_attn_specs(nblk) + [
            pl.BlockSpec((ATTN_BLOCK, cols), lambda n: (jnp.minimum(n, nblk - 1), 0))
            for cols in (K_COL, N_Q_HEADS * P_COLS, 128)],
        out_specs=[pl.BlockSpec((ATTN_BLOCK, ATTN_IN), lambda n: (jnp.maximum(n - 1, 0), 0)),
                   pl.BlockSpec((1, 128), lambda n: (0, 0))],
        out_shape=[_sds((S, ATTN_IN), BF16), _sds((1, 128), F32)],
        scratch_shapes=[pltpu.VMEM((ATTN_BLOCK, ATTN_IN), F32), pltpu.VMEM((ATTN_BLOCK, ATTN_IN), F32),
                        pltpu.VMEM((ATTN_BLOCK, 2 * KV_COLS), F32)],
        compiler_params=_params(("arbitrary",)),
    )(proj, proj, proj, proj, proj, do, probs, sink_probs)


def _hg_consts():
    C = HG_CHUNK
    tri = np.tril(np.ones((C, C)))
    t = np.arange(C)
    rows, masks = [tri], []
    for lvl in range(HG_LEVELS):
        n = C >> (lvl + 1)
        sel = np.zeros((C, C))
        sel[t, (t // (2 * n)) * (2 * n) + n - 1] = 1.0
        rows.append(sel @ tri)
        tt, ss = t[:, None], t[None, :]
        masks.append((tt // (2 * n) == ss // (2 * n)) & ((tt // n) % 2 == 1) & ((ss // n) % 2 == 0))
    masks.append(np.eye(C, dtype=bool))
    stk = np.concatenate(rows, axis=0)
    return jnp.asarray(stk, BF16), jnp.asarray(np.stack(masks), F32)


def _sigmoid(x):
    return 1.0 / (1.0 + jnp.exp(-x))


def _silu_sigmoid(x):
    return 0.5 + 0.5 * jnp.tanh(0.5 * x)


def _split(x, parts):
    out, rest = [], x
    for _ in range(parts):
        out.append(rest.astype(BF16))
        rest = rest - out[-1].astype(F32)
    return out


def _dot01(m01, x, dn, parts=3):
    return sum(lax.dot_general(m01, p, dn, preferred_element_type=F32) for p in _split(x, parts))


def _ref_rows(b, n):
    C = b.shape[1]
    if 2 * n >= 8:
        b3 = b.reshape(HG_CHUNK // (2 * n), 2 * n, C)
        return jnp.broadcast_to(b3[:, n - 1:n, :], b3.shape).reshape(HG_CHUNK, C)
    pos = lax.broadcasted_iota(jnp.int32, b.shape, 0) % (2 * n)
    out = b
    for p in range(2 * n):
        if p != n - 1:
            out = jnp.where(pos == p, pltpu.roll(b, (p - (n - 1)) % HG_CHUNK, 0), out)
    return out


HG_STEP_CHUNKS = 4


def _chunk_rows(ci):
    return pl.ds(pl.multiple_of(ci * HG_CHUNK, HG_CHUNK), HG_CHUNK)


def _hg_common(z_ref, rows, lb_ref, stk_ref):
    qr, fr = z_ref[0, rows, :], z_ref[1, rows, :]
    lb = lb_ref[...]
    sq, sg, sgn = _silu_sigmoid(qr), _sigmoid(fr), _sigmoid(-fr)
    ft = lb + (1.0 - lb) * sg
    b = _dot01(stk_ref[0:HG_CHUNK, :], jnp.log(ft), NN)
    ws = [jnp.exp(-jnp.abs(b - _ref_rows(b, HG_CHUNK >> (l + 1)))) for l in range(HG_LEVELS)]
    blast = b[HG_CHUNK - 1:HG_CHUNK]
    return dict(qr=qr, fr=fr, lb=lb, sq=sq, sg=sg, sgn=sgn, ft=ft, q=qr * sq, kk=(1.0 - lb) * sgn, b=b,
                ws=ws, eb=jnp.exp(b), ed=jnp.exp(blast - b), elast=jnp.exp(blast))


def _hg_factors(qh, kh, ws, sl):
    return ([(qh * ws[l][:, sl]).astype(BF16) for l in range(HG_LEVELS)],
            [(kh * ws[l][:, sl]).astype(BF16) for l in range(HG_LEVELS)])


def _hg_intra(qh, kh, ws, msk_ref, sl):
    qls, kls = _hg_factors(qh, kh, ws, sl)
    a = msk_ref[HG_LEVELS] * _bdot(qh, kh, NT)
    for l in range(HG_LEVELS):
        a = a + msk_ref[l] * _bdot(qls[l], kls[l], NT)
    return a


def _hg_fwd(name, z, lb, ng):
    S = z.shape[2]
    nc = S // HG_CHUNK
    per = min(HG_STEP_CHUNKS, nc)
    stk, msk = _hg_consts()

    def body(z_ref, lb_ref, ng_ref, stk_ref, msk_ref, og_ref, st_ref, a_ref, o_ref, state):
        @pl.when(pl.program_id(1) == 0)
        def _():
            state[...] = jnp.zeros_like(state)

        def chunk(ci, _):
            rows = _chunk_rows(ci)
            cm = _hg_common(z_ref, rows, lb_ref, stk_ref)
            v, gt = z_ref[2, rows, :], z_ref[3, rows, :]
            kd = cm["kk"] * cm["ed"]
            for hh in range(4):
                sl = slice(HG_K * hh, HG_K * (hh + 1))
                st = state[hh]
                st_ref[ci, hh] = st
                qh, kh, vh = cm["q"][:, sl], cm["kk"][:, sl], v[:, sl]
                a = _hg_intra(qh, kh, cm["ws"], msk_ref, sl).astype(BF16)
                a_ref[ci, hh] = a
                o = _bdot(a, vh, NN) + _bdot(qh * cm["eb"][:, sl], st, NT)
                o_ref[rows, sl] = o
                state[hh] = cm["elast"][:, sl] * st + _bdot(vh, kd[:, sl], TN)
                gh = gt[:, sl]
                og_ref[rows, sl] = (o * _rms(o) * ng_ref[...] * (gh * _silu_sigmoid(gh))).astype(og_ref.dtype)
            return 0

        lax.fori_loop(0, per, chunk, 0, unroll=True)

    return pl.pallas_call(
        body, name=name, grid=(2, nc // per),
        in_specs=[pl.BlockSpec((4, None, per * HG_CHUNK, HG_SLOT), lambda g, c: (0, g, c, 0)),
                  pl.BlockSpec((1, HG_SLOT), lambda g, c: (0, g)),
                  pl.BlockSpec((1, HG_K), lambda g, c: (0, 0)),
                  pl.BlockSpec(stk.shape, lambda g, c: (0, 0)),
                  pl.BlockSpec(msk.shape, lambda g, c: (0, 0, 0))],
        out_specs=[pl.BlockSpec((per * HG_CHUNK, HG_SLOT), lambda g, c: (c, g)),
                   pl.BlockSpec((per, 4, HG_K, HG_K), lambda g, c: (c, g, 0, 0)),
                   pl.BlockSpec((per, 4, HG_CHUNK, HG_CHUNK), lambda g, c: (c, g, 0, 0)),
                   pl.BlockSpec((per * HG_CHUNK, HG_SLOT), lambda g, c: (c, g))],
        out_shape=[_sds((S, D_MODEL), BF16), _sds((nc, HG_HEADS, HG_K, HG_K), F32),
                   _sds((nc, HG_HEADS, HG_CHUNK, HG_CHUNK), BF16), _sds((S, D_MODEL), F32)],
        scratch_shapes=[pltpu.VMEM((4, HG_K, HG_K), F32)],
        compiler_params=_params(("parallel", "arbitrary")),
    )(z, lb, ng, stk, msk)


def _hg_bwd(name, z, lb, ng, states, intra, o_pre, dog):
    S = z.shape[2]
    nc = S // HG_CHUNK
    per = min(HG_STEP_CHUNKS, nc)
    stk, msk = _hg_consts()

    def body(z_ref, lb_ref, ng_ref, stk_ref, msk_ref, st_ref, a_ref, o_ref, dog_ref, dz_ref, dlb_ref, dng_ref, dstate):
        @pl.when(pl.program_id(1) == 0)
        def _():
            dstate[...] = jnp.zeros_like(dstate)
            dlb_ref[...] = jnp.zeros_like(dlb_ref)
            dng_ref[...] = jnp.zeros_like(dng_ref)

        def chunk(k, _):
            ci = per - 1 - k
            rows = _chunk_rows(ci)
            cm = _hg_common(z_ref, rows, lb_ref, stk_ref)
            v, gt = z_ref[2, rows, :], z_ref[3, rows, :]
            ng = ng_ref[...]
            kd = cm["kk"] * cm["ed"]
            row = lax.broadcasted_iota(jnp.int32, (HG_CHUNK, 1), 0)
            dng = jnp.zeros((1, HG_K), F32)
            dq_h, dkk_h, db_h, dv_h, dgt_h = [], [], [], [], []
            dr_h = [[] for _ in range(HG_LEVELS)]
            for hh in range(4):
                sl = slice(HG_K * hh, HG_K * (hh + 1))
                st, dst = st_ref[ci, hh], dstate[hh]
                qh, kh, vh, ebh, edh, kdh = cm["q"][:, sl], cm["kk"][:, sl], v[:, sl], cm["eb"][:, sl], cm["ed"][:, sl], kd[:, sl]
                elh = cm["elast"][:, sl]
                qls, kls = _hg_factors(qh, kh, cm["ws"], sl)
                a, o = a_ref[ci, hh], o_ref[rows, sl]
                qe = qh * ebh
                r = _rms(o)
                xh = o * r
                gh = gt[:, sl]
                sgg = _silu_sigmoid(gh)
                dog = dog_ref[rows, sl].astype(F32)
                dy = dog * (gh * sgg)
                dgt_h.append(dog * (xh * ng) * (sgg * (1.0 + gh * (1.0 - sgg))))
                dng = dng + jnp.sum(dy * xh, axis=0, keepdims=True)
                dyg = dy * ng
                do = r * (dyg - xh * jnp.mean(dyg * xh, axis=-1, keepdims=True))
                da = _bdot(do, vh, NT)
                dv_h.append(_bdot(a, do, TN) + _bdot(kdh, dst, NT))
                dkd = _bdot(vh, dst, NN)
                delast = jnp.sum(st * dst, axis=0, keepdims=True)
                dqe = _bdot(do, st, NN)
                dstate[hh] = elh * dst + _bdot(do, qe, TN)
                gk = dkd * kdh
                dblast = jnp.sum(gk, axis=0, keepdims=True) + delast * elh
                db = dqe * qe - gk + jnp.where(row == HG_CHUNK - 1, dblast, 0.0)
                dp = (msk_ref[HG_LEVELS] * da).astype(BF16)
                dq = dqe * ebh + _bdot(dp, kh, NN)
                dkk = dkd * edh + _bdot(dp, qh, TN)
                for l in range(HG_LEVELS):
                    dp = (msk_ref[l] * da).astype(BF16)
                    dql, dkl = _bdot(dp, kls[l], NN), _bdot(dp, qls[l], TN)
                    w = cm["ws"][l][:, sl]
                    dq = dq + dql * w
                    dkk = dkk + dkl * w
                    half = jnp.where(((row >> (HG_LEVELS - 1 - l)) & 1) == 1, 1.0, -1.0)
                    dd = half * w * (dql * qh + dkl * kh)
                    db = db + dd
                    dr_h[l].append(-dd)
                dq_h.append(dq)
                dkk_h.append(dkk)
                db_h.append(db)
            cat = lambda xs: jnp.concatenate(xs, axis=1)
            cot = jnp.concatenate([cat(db_h)] + [cat(dr_h[l]) for l in range(HG_LEVELS)], axis=0)
            dlf = _dot01(stk_ref[...], cot, TN, parts=2)
            dq, dkk = cat(dq_h), cat(dkk_h)
            dft = dlf / cm["ft"]
            one_lb = 1.0 - cm["lb"]
            dz_ref[0, rows, :] = (dq * (cm["sq"] * (1.0 + cm["qr"] * (1.0 - cm["sq"])))).astype(dz_ref.dtype)
            dz_ref[1, rows, :] = ((dft - dkk) * one_lb * cm["sg"] * cm["sgn"]).astype(dz_ref.dtype)
            dz_ref[2, rows, :] = cat(dv_h).astype(dz_ref.dtype)
            dz_ref[3, rows, :] = cat(dgt_h).astype(dz_ref.dtype)
            dlb_ref[...] += jnp.sum((dft - dkk) * cm["sgn"], axis=0, keepdims=True)
            dng_ref[...] += dng
            return 0

        lax.fori_loop(0, per, chunk, 0, unroll=True)

    rev = lambda c: nc // per - 1 - c
    rows_blk = pl.BlockSpec((per * HG_CHUNK, HG_SLOT), lambda g, c: (rev(c), g))
    return pl.pallas_call(
        body, name=name, grid=(2, nc // per),
        in_specs=[pl.BlockSpec((4, None, per * HG_CHUNK, HG_SLOT), lambda g, c: (0, g, rev(c), 0)),
                  pl.BlockSpec((1, HG_SLOT), lambda g, c: (0, g)),
                  pl.BlockSpec((1, HG_K), lambda g, c: (0, 0)),
                  pl.BlockSpec(stk.shape, lambda g, c: (0, 0)),
                  pl.BlockSpec(msk.shape, lambda g, c: (0, 0, 0)),
                  pl.BlockSpec((per, 4, HG_K, HG_K), lambda g, c: (rev(c), g, 0, 0)),
                  pl.BlockSpec((per, 4, HG_CHUNK, HG_CHUNK), lambda g, c: (rev(c), g, 0, 0)),
                  rows_blk, rows_blk],
        out_specs=[pl.BlockSpec((4, None, per * HG_CHUNK, HG_SLOT), lambda g, c: (0, g, rev(c), 0)),
                   pl.BlockSpec((1, HG_SLOT), lambda g, c: (0, g)),
                   pl.BlockSpec((None, 1, HG_K), lambda g, c: (g, 0, 0))],
        out_shape=[_sds(z.shape, BF16), _sds((1, 2 * HG_SLOT), F32), _sds((2, 1, HG_K), F32)],
        scratch_shapes=[pltpu.VMEM((4, HG_K, HG_K), F32)],
        compiler_params=_params(("parallel", "arbitrary")),
    )(z, lb, ng, stk, msk, states, intra, o_pre, dog)


def _lb_fwd(name, logits):
    def body(l_ref, o_ref):
        x = l_ref[...]
        e = jnp.exp(x - jnp.max(x, axis=0, keepdims=True))
        s = e / jnp.sum(e, axis=0, keepdims=True)
        o_ref[0:1, :] = s[1:2]
        o_ref[1:2, :] = s[1:2] + s[2:3] + s[3:4]

    return pl.pallas_call(body, name=name, out_shape=_sds((2, logits.shape[1]), F32))(logits)


def _lb_bwd(name, logits, dlb):
    def body(l_ref, d_ref, o_ref):
        x = l_ref[...]
        e = jnp.exp(x - jnp.max(x, axis=0, keepdims=True))
        s = e / jnp.sum(e, axis=0, keepdims=True)
        d1, d3 = d_ref[0:1, :], d_ref[1:2, :]
        ds = [jnp.zeros_like(d1), d1 + d3, d3, d3]
        dot = sum(ds[r] * s[r:r + 1] for r in range(1, DEPTH))
        for r in range(DEPTH):
            o_ref[r:r + 1, :] = s[r:r + 1] * (ds[r] - dot)

    return pl.pallas_call(body, name=name, out_shape=_sds(logits.shape, F32))(logits, dlb)


SUB = 8


def _rows_down(x, prev, k):
    row = lax.broadcasted_iota(jnp.int32, x.shape, 0)
    return jnp.where(row >= k, pltpu.roll(x, k, 0), pltpu.roll(prev, k, 0))


def _rows_up(x, nxt, k):
    row = lax.broadcasted_iota(jnp.int32, x.shape, 0)
    return jnp.where(row < SUB - k, pltpu.roll(x, SUB - k, 0), pltpu.roll(nxt, SUB - k, 0))


def _conv_block(w_ref, b_ref, p, x, prev):
    return (b_ref[p] + w_ref[p, 0:1, :] * _rows_down(x, prev, 2) + w_ref[p, 1:2, :] * _rows_down(x, prev, 1)
            + w_ref[p, 2:3, :] * x)


def _convgate_fwd(name, u, cw, cb):
    S = u.shape[2]
    tm = _tile(S, ROW_TILE)

    def body(u_ref, w_ref, b_ref, a_ref, c_ref, halo):
        @pl.when(pl.program_id(1) == 0)
        def _():
            halo[...] = jnp.zeros_like(halo)

        def step(r, prev):
            pg, pv = prev
            out, cgs, cvs = [], [], []
            rows = pl.ds(pl.multiple_of(r * 2 * SUB, 2 * SUB), 2 * SUB)
            ug16, uv16 = u_ref[0, rows, :].astype(F32), u_ref[1, rows, :].astype(F32)
            for s in range(2):
                xg, xv = ug16[s * SUB:(s + 1) * SUB], uv16[s * SUB:(s + 1) * SUB]
                cgs.append(_conv_block(w_ref, b_ref, 0, xg, pg))
                cvs.append(_conv_block(w_ref, b_ref, 1, xv, pv))
                out.append(cgs[s] * _silu_sigmoid(cgs[s]) * cvs[s])
                pg, pv = xg, xv
            a_ref[rows, :] = jnp.concatenate(out, axis=0).astype(a_ref.dtype)
            c_ref[0, rows, :] = jnp.concatenate(cgs, axis=0).astype(c_ref.dtype)
            c_ref[1, rows, :] = jnp.concatenate(cvs, axis=0).astype(c_ref.dtype)
            return pg, pv

        pg, pv = lax.fori_loop(0, tm // (2 * SUB), step, (halo[0], halo[1]), unroll=2)
        halo[0] = pg
        halo[1] = pv

    pair = pl.BlockSpec((2, None, tm, FF_SLOT), lambda j, t: (0, j, t, 0))
    return pl.pallas_call(
        body, name=name, grid=(4, S // tm),
        in_specs=[pair, pl.BlockSpec((2, None, 3, FF_SLOT), lambda j, t: (0, j, 0, 0)),
                  pl.BlockSpec((2, None, 1, FF_SLOT), lambda j, t: (0, j, 0, 0))],
        out_specs=[pl.BlockSpec((None, tm, FF_SLOT), lambda j, t: (j, t, 0)), pair],
        out_shape=[_sds((4, S, FF_SLOT), BF16), _sds(u.shape, BF16)],
        scratch_shapes=[pltpu.VMEM((2, SUB, FF_SLOT), F32)],
        compiler_params=_params(("parallel", "arbitrary")),
    )(u, cw, cb)


def _convgate_bwd(name, u, convs, cw, da):
    S = u.shape[2]
    tm = _tile(S, ROW_TILE)
    nt = S // tm

    def body(u_ref, c_ref, w_ref, da_ref, du_out, dw_ref, db_ref, after, first, acc, du_ref):
        @pl.when(pl.program_id(1) == 0)
        def _():
            after[...] = jnp.zeros_like(after)
            acc[...] = jnp.zeros_like(acc)

        def finish(p, x, d, nxt, rows):
            taps = (_rows_up(d, nxt, 2), _rows_up(d, nxt, 1), d)
            du_ref[p, rows, :] = w_ref[p, 0:1, :] * taps[0] + w_ref[p, 1:2, :] * taps[1] + w_ref[p, 2:3, :] * d
            for j in range(3):
                acc[p, j] += taps[j] * x
            acc[p, 3] += d

        def step(r, carry):
            xg_last, xv_last, dg_last, dv_last = carry
            rows16 = pl.ds(pl.multiple_of(r * 2 * SUB, 2 * SUB), 2 * SUB)
            dav = da_ref[rows16, :].astype(F32)
            cg16, cv16 = c_ref[0, rows16, :].astype(F32), c_ref[1, rows16, :].astype(F32)
            ug16, uv16 = u_ref[0, rows16, :].astype(F32), u_ref[1, rows16, :].astype(F32)
            for s in range(2):
                at = r * 2 * SUB + s * SUB
                part = slice(s * SUB, (s + 1) * SUB)
                cg, cv, dab = cg16[part], cv16[part], dav[part]
                sg = _silu_sigmoid(cg)
                dg = dab * cv * (sg * (1.0 + cg * (1.0 - sg)))
                dv = dab * cg * sg
                before = pl.ds(pl.multiple_of(at - SUB, SUB), SUB)
                if s == 0:
                    @pl.when(r == 0)
                    def _():
                        first[0] = dg
                        first[1] = dv

                    @pl.when(r > 0)
                    def _():
                        finish(0, xg_last, dg_last, dg, before)
                        finish(1, xv_last, dv_last, dv, before)
                else:
                    finish(0, xg_last, dg_last, dg, before)
                    finish(1, xv_last, dv_last, dv, before)
                xg_last, xv_last, dg_last, dv_last = ug16[part], uv16[part], dg, dv
            return xg_last, xv_last, dg_last, dv_last

        zero = jnp.zeros((SUB, FF_SLOT), F32)
        xg_last, xv_last, dg_last, dv_last = lax.fori_loop(0, tm // (2 * SUB), step, (zero, zero, zero, zero))
        finish(0, xg_last, dg_last, after[0], slice(tm - SUB, tm))
        finish(1, xv_last, dv_last, after[1], slice(tm - SUB, tm))
        du_out[...] = du_ref[...].astype(du_out.dtype)
        after[...] = first[...]
        for p in range(2):
            for j in range(3):
                dw_ref[p, j:j + 1, :] = jnp.sum(acc[p, j], axis=0, keepdims=True)
            db_ref[p] = jnp.sum(acc[p, 3], axis=0, keepdims=True)

    rev = lambda t: nt - 1 - t
    pair = pl.BlockSpec((2, None, tm, FF_SLOT), lambda j, t: (0, j, rev(t), 0))
    taps = pl.BlockSpec((2, None, 3, FF_SLOT), lambda j, t: (0, j, 0, 0))
    bias = pl.BlockSpec((2, None, 1, FF_SLOT), lambda j, t: (0, j, 0, 0))
    return pl.pallas_call(
        body, name=name, grid=(4, nt),
        in_specs=[pair, pair, taps, pl.BlockSpec((None, tm, FF_SLOT), lambda j, t: (j, rev(t), 0))],
        out_specs=[pair, taps, bias],
        out_shape=[_sds(u.shape, BF16), _sds(cw.shape, F32), _sds((2, 4, 1, FF_SLOT), F32)],
        scratch_shapes=[pltpu.VMEM((2, SUB, FF_SLOT), F32), pltpu.VMEM((2, SUB, FF_SLOT), F32),
                        pltpu.VMEM((2, 4, SUB, FF_SLOT), F32), pltpu.VMEM((2, tm, FF_SLOT), F32)],
        compiler_params=_params(("parallel", "arbitrary")),
    )(u, convs, cw, da)


def _row_tile(R):
    for t in range(256, 15, -16):
        if R % t == 0:
            return t
    return R


def _adamw(name, gsrcs, w, m, v, dep=None):
    L = len(gsrcs)
    n, A, C = gsrcs[0].shape
    tr = _row_tile(A)
    deps = () if dep is None else (dep,)

    def body(*refs):
        g_refs = refs[:L]
        w_ref, m_ref, v_ref = refs[L:L + 3]
        go_ref, d_ref, mo_ref, vo_ref = refs[L + 3 + len(deps):]
        for k in range(L):
            @pl.when(pl.program_id(0) == k)
            def _(k=k):
                g = g_refs[k][0].astype(F32)
                for s in range(1, n):
                    g = g + g_refs[k][s].astype(F32)
                m2 = ADAM_B1 * m_ref[...] + (1.0 - ADAM_B1) * g
                v2 = ADAM_B2 * v_ref[...] + (1.0 - ADAM_B2) * (g * g)
                m_hat = m2 / (1.0 - ADAM_B1 ** ADAM_STEP)
                v_hat = v2 / (1.0 - ADAM_B2 ** ADAM_STEP)
                go_ref[...] = g
                d_ref[...] = -ADAM_LR * (m_hat / (jnp.sqrt(v_hat) + ADAM_EPS) + ADAM_WD * w_ref[...])
                mo_ref[...] = m2
                vo_ref[...] = v2

    g_specs = [pl.BlockSpec((n, tr, C), lambda l, i, k=k: (0, jnp.where(l == k, i, 0), 0)) for k in range(L)]
    blk = pl.BlockSpec((None, tr, C), lambda l, i: (l, i, 0))
    return pl.pallas_call(
        body, name=name, grid=(L, A // tr), in_specs=g_specs + [blk, blk, blk] + [_dep_spec(2)] * len(deps),
        out_specs=[blk] * 4, out_shape=[_sds((L, A, C), F32)] * 4, compiler_params=_params(("parallel", "parallel")),
    )(*gsrcs, w, m, v, *deps)


MESH = pl.DeviceIdType.MESH
HBM_SPEC = pl.BlockSpec(memory_space=pltpu.HBM)
N_PEERS = N_DEV - 1


def _mesh_place():
    x, y, c = lax.axis_index("x"), lax.axis_index("y"), lax.axis_index("c")
    peers = []
    for p in range(1, N_DEV):
        px = 1 - x if p & 4 else x
        py = 1 - y if p & 2 else y
        pc = 1 - c if p & 1 else c
        peers.append(((px, py, pc), 4 * px + 2 * py + pc))
    return 4 * x + 2 * y + c, peers


SEM_SPEC = pl.BlockSpec(memory_space=pltpu.SEMAPHORE)
ANY_SPEC = pl.BlockSpec(memory_space=pl.ANY)
EFFECT = pltpu.SideEffectType.DATAFLOW_SIDE_EFFECTING


def _scatters(scatter, k):
    return scatter if isinstance(scatter, bool) else scatter[k]


def _exchange_refs(scatter, src, land, send, recv, k, p, dev, idx, me):
    return pltpu.make_async_remote_copy(src_ref=src[k].at[idx] if _scatters(scatter, k) else src[k], dst_ref=land[k].at[me],
                                        send_sem=send.at[k * N_PEERS + p], recv_sem=recv.at[k * N_PEERS + p], device_id=dev,
                                        device_id_type=MESH)


def _exchange_start(name, srcs, scatter, gate):
    n = len(srcs)
    lands = [lax.empty(s.shape if _scatters(scatter, k) else (N_DEV,) + s.shape, s.dtype) for k, s in enumerate(srcs)]

    def body(*refs):
        src, land = refs[:n], refs[n:2 * n]
        send, recv, own = refs[2 * n + 1:2 * n + 4]
        token = refs[-1]
        me, peers = _mesh_place()
        for k in range(n):
            pltpu.make_async_copy(src[k].at[me] if _scatters(scatter, k) else src[k], land[k].at[me], own.at[k]).start()
            for p, (dev, idx) in enumerate(peers):
                _exchange_refs(scatter, src, land, send, recv, k, p, dev, idx, me).start()
        token[...] = jnp.zeros_like(token)

    hbm = lambda a: pltpu.HBM(a.shape, a.dtype)
    outs = pl.pallas_call(
        body, name=name,
        out_shape=(pltpu.SemaphoreType.DMA((n * N_PEERS,)), pltpu.SemaphoreType.DMA((n * N_PEERS,)),
                   pltpu.SemaphoreType.DMA((n,)), *[hbm(s) for s in srcs], *[hbm(s) for s in lands], _sds(DEP_SHAPE, F32)),
        in_specs=[HBM_SPEC] * (2 * n) + [ANY_SPEC],
        out_specs=(SEM_SPEC, SEM_SPEC, SEM_SPEC, *[HBM_SPEC] * (2 * n), pl.BlockSpec(memory_space=pltpu.VMEM)),
        input_output_aliases={j: 3 + j for j in range(2 * n)},
        compiler_params=pltpu.CompilerParams(has_side_effects=EFFECT),
    )(*[pltpu.with_memory_space_constraint(s, pltpu.HBM) for s in srcs],
      *[pltpu.with_memory_space_constraint(s, pltpu.HBM) for s in lands], gate)
    return outs[:3], None, list(outs[3:3 + n]), list(outs[3 + n:3 + 2 * n]), outs[-1]


def _exchange_wait(name, started, scatter, after):
    (send, recv, own), _, srcs, lands, _ = started
    n = len(srcs)

    def body(*refs):
        src, land = refs[:n], refs[n:2 * n]
        send, recv, own = refs[2 * n:2 * n + 3]
        me, peers = _mesh_place()
        for k in range(n):
            pltpu.make_async_copy(src[k].at[me] if _scatters(scatter, k) else src[k], land[k].at[me], own.at[k]).wait()
            for p, (dev, idx) in enumerate(peers):
                cp = pltpu.make_async_remote_copy(src_ref=src[k].at[idx] if _scatters(scatter, k) else src[k], dst_ref=land[k].at[idx],
                                                  send_sem=send.at[k * N_PEERS + p], recv_sem=recv.at[k * N_PEERS + p], device_id=dev,
                                                  device_id_type=MESH)
                cp.wait_send()
                cp.wait_recv()

    hbm = lambda a: pltpu.HBM(a.shape, a.dtype)
    outs = pl.pallas_call(
        body, name=name, out_shape=(*[hbm(s) for s in srcs], *[hbm(s) for s in lands]),
        in_specs=[HBM_SPEC] * (2 * n) + [SEM_SPEC, SEM_SPEC, SEM_SPEC, ANY_SPEC], out_specs=tuple([HBM_SPEC] * (2 * n)),
        input_output_aliases={j: j for j in range(2 * n)},
        compiler_params=pltpu.CompilerParams(has_side_effects=EFFECT),
    )(*srcs, *lands, send, recv, own, after)
    return list(outs[n:])


def _sum_devices(name, parts):
    def body(p_ref, o_ref):
        tot = p_ref[0]
        for j in range(1, N_DEV):
            tot = tot + p_ref[j]
        o_ref[...] = tot

    return pl.pallas_call(body, name=name, out_shape=_sds(parts.shape[1:], F32),
                          compiler_params=pltpu.CompilerParams(vmem_limit_bytes=VMEM_LIMIT))(parts)


def _rows(a, width=D_MODEL):
    flat = a.reshape(-1)
    return jnp.pad(flat, (0, (-flat.shape[0]) % width)).reshape(-1, width)


def _pack_rows(parts):
    blocks = []
    for p in parts:
        r = _rows(p)
        blocks.append(jnp.pad(r, ((0, (-r.shape[0]) % 8), (0, 0))))
    return jnp.concatenate(blocks, axis=0)


def _unpack_rows(rows, shapes):
    out, at = [], 0
    for s in shapes:
        size = int(np.prod(s))
        n = -(-size // D_MODEL)
        out.append(rows[at:at + n].reshape(-1)[:size].reshape(s))
        at += -(-n // 8) * 8
    return out


def kernel(x, norm_mix, norm_ffn, norm_final, attn_w_in, attn_w_out, attn_sinks, hgrn_w_in, hgrn_w_out, hgrn_norm, hgrn_lb_logits, ffn_w_up, ffn_conv_w, ffn_conv_b, ffn_w_down, loss_target, m_norm_mix, m_norm_ffn, m_norm_final, m_attn_w_in, m_attn_w_out, m_attn_sinks, m_hgrn_w_in, m_hgrn_w_out, m_hgrn_norm, m_hgrn_lb_logits, m_ffn_w_up, m_ffn_conv_w, m_ffn_conv_b, m_ffn_w_down, v_norm_mix, v_norm_ffn, v_norm_final, v_attn_w_in, v_attn_w_out, v_attn_sinks, v_hgrn_w_in, v_hgrn_w_out, v_hgrn_norm, v_hgrn_lb_logits, v_ffn_w_up, v_ffn_conv_w, v_ffn_conv_b, v_ffn_w_down):
    S = x.shape[1]
    n_attn, n_hgrn = attn_w_in.shape[0], hgrn_w_in.shape[0]

    wa_in_t, wa_out_b = attn_w_in.transpose(0, 2, 1).astype(BF16), attn_w_out.astype(BF16)
    wh_in_b, wh_out_b = hgrn_w_in.astype(BF16), hgrn_w_out.astype(BF16)
    wf_up_b, wf_down_b = ffn_w_up.transpose(0, 2, 1).astype(BF16), ffn_w_down.astype(BF16)
    conv_b = ffn_conv_b.reshape(DEPTH, 2, 4, 1, FF_SLOT)
    lb = _lb_fwd("lb_fwd", hgrn_lb_logits)

    def unit_shards(l, part):
        if part == "ffn":
            return [wf_up_b[l], wf_down_b[l], ffn_conv_w[l]]
        return [wa_in_t[l // 2], wa_out_b[l // 2]] if l % 2 == 0 else [wh_in_b[l // 2], wh_out_b[l // 2]]

    def unit_weights(l, part, w):
        if part == "ffn":
            return w[0][None], w[1].reshape(1, 4, FF_SLOT, D_MODEL), w[2].reshape(2, 4, 3, FF_SLOT)
        if l % 2 == 0:
            return w[0].reshape(1, ATTN_IN, D_MODEL), w[1].reshape(1, D_MODEL, D_MODEL)
        return w[0][None], w[1].reshape(1, D_MODEL, D_MODEL)

    units = [(l, part) for l in range(DEPTH) for part in ("mix", "ffn")]
    gathers = [_exchange_start("gather_start0", unit_shards(*units[0]), False, norm_final)]
    gathers.append(_exchange_start("gather_start1", unit_shards(*units[1]), False, gathers[0][4]))
    arrived = _exchange_wait("gather_wait0", gathers[0], False, gathers[1][4])
    weights, saved = {}, [dict() for _ in range(DEPTH)]
    h = x[0]
    hn = _rmsnorm_fwd("norm_mix_fwd0", h, norm_mix[0:1])
    for n, (l, part) in enumerate(units):
        i, sv = l // 2, saved[l]
        weights[l, part] = w = unit_weights(l, part, arrived)
        dep = None
        if n + 2 < len(units):
            gathers.append(_exchange_start(f"gather_start{n + 2}", unit_shards(*units[n + 2]), False, arrived[0]))
            dep = gathers[n + 2][4]
        if part == "mix":
            sv["h"], sv["hn"] = h, hn
            if l % 2 == 0:
                sv["proj"] = _proj_rows(f"attn_proj{i}", hn, w[0], 0, BF16, dep)
                sv["o"], *sv["kept"] = _attn_fwd(f"attn_fwd{i}", sv["proj"], attn_sinks[i:i + 1])
                h, hn = _out_proj(f"attn_out{i}", sv["o"], w[1], 0, h, norm_ffn[l:l + 1])
            else:
                sv["z"] = _proj_slots(f"hgrn_proj{i}", hn, w[0], 0, dep=dep).reshape(4, 2, S, HG_SLOT)
                sv["o"], *sv["kept"] = _hg_fwd(f"hgrn_fwd{i}", sv["z"], lb[i:i + 1], hgrn_norm[i:i + 1])
                h, hn = _out_proj(f"hgrn_out{i}", sv["o"], w[1], 0, h, norm_ffn[l:l + 1])
        else:
            sv["h2"], sv["hn2"] = h, hn
            sv["u"] = _proj_slots(f"ffn_up{l}", hn, w[0], 0, True, dep, BF16).reshape(2, 4, S, FF_SLOT)
            sv["a"], sv["convs"] = _convgate_fwd(f"ffn_gate{l}", sv["u"], w[2], conv_b[l])
            if l + 1 < DEPTH:
                h, hn = _down_proj(f"ffn_down{l}", sv["a"], w[1], 0, h, norm_mix[l + 1:l + 2])
            else:
                h = _down_proj(f"ffn_down{l}", sv["a"], w[1], 0, h)
        if n + 1 < len(units):
            arrived = _exchange_wait(f"gather_wait{n + 1}", gathers[n + 1], False, h)
    dh, d_norm_final, loss_rows, dhb = _loss_head("loss_head", h, norm_final[None], loss_target[0])

    d_conv_w, d_conv_b, d_norm_mix, d_norm_ffn = [None] * DEPTH, [None] * DEPTH, [None] * DEPTH, [None] * DEPTH
    d_sinks, d_lb, d_hgrn_norm = [None] * n_attn, [None] * n_hgrn, [None] * n_hgrn
    received, pending = {}, []
    for l, part in reversed(units):
        i, sv, w = l // 2, saved[l], weights[l, part]
        dep = pending[-1][1][4] if pending else None
        if part == "ffn":
            da = _dgrad_down(f"ffn_down_dgrad{l}", dhb, w[1], 0, dep)
            g_down = _wgrad_down(f"ffn_down_wgrad{l}", sv["a"], dhb).reshape(N_DEV, D_FF // N_DEV, D_MODEL)
            du, d_conv_w[l], d_conv_b[l] = _convgate_bwd(f"ffn_gate_bwd{l}", sv["u"], sv["convs"], w[2], da)
            du = du.reshape(N_DEV, S, FF_SLOT)
            grads = [_wgrad_slots(f"ffn_up_wgrad{l}", sv["hn2"], du, True), g_down]
            dh, d_norm_ffn[l], dhb = _dgrad_slots(f"ffn_up_dgrad{l}", du, w[0], 0, (sv["h2"], norm_ffn[l:l + 1], dh), True)
        else:
            if l % 2 == 0:
                do = _dgrad_out(f"attn_out_dgrad{i}", dhb, w[1], 0, BF16, dep)
                g_out = _wgrad_rows(f"attn_out_wgrad{i}", sv["o"], dhb)
                dproj, d_sinks[i] = _attn_bwd(f"attn_bwd{i}", sv["proj"], *sv["kept"], do)
                g_in = _wgrad_rows(f"attn_proj_wgrad{i}", dproj, sv["hn"]).reshape(N_DEV, ATTN_IN // N_DEV, D_MODEL)
                dh_new = _dgrad_rows(f"attn_proj_dgrad{i}", dproj, w[0], 0, (sv["h"], norm_mix[l:l + 1], dh))
            else:
                dog = _dgrad_out(f"hgrn_out_dgrad{i}", dhb, w[1], 0, F32, dep)
                g_out = _wgrad_rows(f"hgrn_out_wgrad{i}", sv["o"], dhb)
                dz, d_lb[i], dng = _hg_bwd(f"hgrn_bwd{i}", sv["z"], lb[i:i + 1], hgrn_norm[i:i + 1], *sv["kept"], dog)
                d_hgrn_norm[i] = dng[0] + dng[1]
                dz = dz.reshape(N_DEV, S, HG_SLOT)
                g_in = _wgrad_slots(f"hgrn_proj_wgrad{i}", sv["hn"], dz)
                dh_new = _dgrad_slots(f"hgrn_proj_dgrad{i}", dz, w[0], 0, (sv["h"], norm_mix[l:l + 1], dh))
            grads = [g_in, g_out.reshape(N_DEV, D_MODEL // N_DEV, D_MODEL)]
            dh, d_norm_mix[l], dhb = dh_new
        gate = dh
        if len(pending) == 2:
            key, oldest = pending.pop(0)
            received[key] = _exchange_wait(f"scatter_wait_{key[1]}{key[0]}", oldest, True, dh)
            gate = received[key][0]
        pending.append(((l, part), _exchange_start(f"scatter_start_{part}{l}", grads, True, gate)))
    grad_x = dh[None]

    small_shapes = [(DEPTH, D_MODEL), (DEPTH, D_MODEL), (1, D_MODEL), (1, D_MODEL), (n_hgrn, D_MODEL), (n_attn, 128),
                    (n_hgrn, HG_K), (DEPTH, 2 * D_FF)]
    partial = _pack_rows([
        jnp.concatenate(d_norm_mix), jnp.concatenate(d_norm_ffn), d_norm_final, loss_rows, jnp.concatenate(d_lb),
        jnp.concatenate(d_sinks), jnp.concatenate(d_hgrn_norm), jnp.stack(d_conv_b)])
    d_taps = jnp.stack(d_conv_w).reshape(DEPTH, N_DEV, 3, FF_SLOT).transpose(1, 0, 2, 3).reshape(N_DEV, DEPTH * 3, FF_SLOT)
    small_started = _exchange_start("small_start", [partial, d_taps], (False, True), pending[-1][1][4])
    attn_layers, hgrn_layers = range(0, DEPTH, 2), range(1, DEPTH, 2)

    def transposed(ts):
        return [t.transpose(0, 2, 1) for t in ts]

    big = {"hgrn_w_in": _adamw("adamw_hgrn_in", [received[l, "mix"][0] for l in hgrn_layers], hgrn_w_in, m_hgrn_w_in,
                               v_hgrn_w_in, dep=small_started[4])}
    big["hgrn_w_out"] = _adamw("adamw_hgrn_out", [received[l, "mix"][1] for l in hgrn_layers], hgrn_w_out, m_hgrn_w_out, v_hgrn_w_out)
    key, oldest = pending.pop(0)
    received[key] = _exchange_wait(f"scatter_wait_{key[1]}{key[0]}", oldest, True, big["hgrn_w_in"][3])
    up_t = _adamw("adamw_ffn_up", [received[l, "ffn"][0] for l in range(DEPTH)], *transposed((ffn_w_up, m_ffn_w_up, v_ffn_w_up)))
    big["ffn_w_up"] = transposed(up_t)
    big["ffn_w_down"] = _adamw("adamw_ffn_down", [received[l, "ffn"][1] for l in range(DEPTH)], ffn_w_down, m_ffn_w_down, v_ffn_w_down)
    key, oldest = pending.pop(0)
    received[key] = _exchange_wait(f"scatter_wait_{key[1]}{key[0]}", oldest, True, up_t[3])
    small_parts, taps_parts = _exchange_wait("small_wait", small_started, (False, True), up_t[3])
    total = _sum_devices("sum_small", small_parts)
    (g_norm_mix, g_norm_ffn, g_norm_final, loss_sum, g_lb, g_sinks, g_hgrn_norm, g_conv_b) = _unpack_rows(total, small_shapes)

    loss = jnp.sum(loss_sum)
    g_norm_final = g_norm_final[0]
    g_sinks = g_sinks[:, :N_Q_HEADS]
    g_lb_logits = _lb_bwd("lb_bwd", hgrn_lb_logits, g_lb)

    big.update({
        "attn_w_in": transposed(_adamw("adamw_attn_in", [received[l, "mix"][0] for l in attn_layers],
                                       *transposed((attn_w_in, m_attn_w_in, v_attn_w_in)))),
        "attn_w_out": _adamw("adamw_attn_out", [received[l, "mix"][1] for l in attn_layers], attn_w_out, m_attn_w_out, v_attn_w_out),
        "ffn_conv_w": _adamw("adamw_conv_w", [taps_parts[:, 3 * l:3 * l + 3] for l in range(DEPTH)], ffn_conv_w, m_ffn_conv_w,
                             v_ffn_conv_w),
    })
    small_w = [norm_mix, norm_ffn, norm_final, attn_sinks, hgrn_norm, hgrn_lb_logits, ffn_conv_b]
    small_m = [m_norm_mix, m_norm_ffn, m_norm_final, m_attn_sinks, m_hgrn_norm, m_hgrn_lb_logits, m_ffn_conv_b]
    small_v = [v_norm_mix, v_norm_ffn, v_norm_final, v_attn_sinks, v_hgrn_norm, v_hgrn_lb_logits, v_ffn_conv_b]
    small_g = [g_norm_mix, g_norm_ffn, g_norm_final, g_sinks, g_hgrn_norm, g_lb_logits, g_conv_b]
    outs = _adamw("adamw_small", [_pack_rows(small_g)[None]], *[_pack_rows(t)[None] for t in (small_w, small_m, small_v)])
    outs = [o[0] for o in outs]
    shapes = [w.shape for w in small_w]
    small = {n: [t[j] for t in [_unpack_rows(o, shapes) for o in outs]]
             for j, n in enumerate(["norm_mix", "norm_ffn", "norm_final", "attn_sinks", "hgrn_norm", "hgrn_lb_logits", "ffn_conv_b"])}
    order = ["norm_mix", "norm_ffn", "norm_final", "attn_w_in", "attn_w_out", "attn_sinks", "hgrn_w_in", "hgrn_w_out",
             "hgrn_norm", "hgrn_lb_logits", "ffn_w_up", "ffn_conv_w", "ffn_conv_b", "ffn_w_down"]
    res = {**big, **small}
    return (loss, grad_x, *[res[n][0] for n in order], *[res[n][1] for n in order], *[res[n][2] for n in order],
            *[res[n][3] for n in order])
```

```python
import numpy as np
import jax
import jax.numpy as jnp
from jax import lax
from jax.experimental import pallas as pl
from jax.experimental.pallas import tpu as pltpu

F32 = jnp.float32
BF16 = jnp.bfloat16

D_MODEL = 1024
DEPTH = 4
HEAD_DIM = 64
N_Q_HEADS = 16
N_KV_HEADS = 4
Q_PER_KV = 4
ATTN_BLOCK = 128
ATTN_IN = 1536
HG_HEADS = 8
HG_K = 128
HG_CHUNK = 64
HG_IN = 4096
D_FF = 2816
EPS = 1e-6
N_DEV = 8
FF_SLOT = 2 * D_FF // N_DEV
HG_SLOT = HG_IN // N_DEV
HG_LEVELS = 6

ADAM_LR = 0.001
ADAM_B1 = 0.9
ADAM_B2 = 0.999
ADAM_EPS = 1e-08
ADAM_WD = 0.01
ADAM_STEP = 10

VMEM_LIMIT = 56 * 1024 * 1024
ROW_TILE = 1024
WIDE_ROW_TILE = 2048
SLOTS_PER_STEP = 2
NEG_BIG = -1e30

NN = (((1,), (0,)), ((), ()))
NT = (((1,), (1,)), ((), ()))
TN = (((0,), (0,)), ((), ()))


def _bdot(a, b, dn):
    return lax.dot_general(a.astype(BF16), b.astype(BF16), dn, preferred_element_type=F32)


def _sds(shape, dtype):
    return jax.ShapeDtypeStruct(tuple(shape), dtype)


def _params(sem):
    return pltpu.CompilerParams(dimension_semantics=sem, vmem_limit_bytes=VMEM_LIMIT)


DEP_SHAPE = (8, 128)


def _dep_spec(rank):
    return pl.BlockSpec(DEP_SHAPE, lambda *_: (0, 0))


def _matmul(name, a, b, *, dn, grid, a_spec, b_spec, o_spec, out_shape, acc_shape=None, extra=(), extra_specs=(),
            finish=None, dep=None, sem=("parallel", "parallel", "arbitrary")):
    nk = grid[2]
    many = isinstance(out_shape, (list, tuple))
    n_in = 2 + len(extra) + (dep is not None)
    n_out = len(out_shape) if many else 1

    def body(*refs):
        a_ref, b_ref = refs[0], refs[1]
        outs = refs[n_in:n_in + n_out]

        def prod():
            if len(a_ref.shape) == 3:
                return sum(_bdot(a_ref[s], b_ref[s], dn) for s in range(a_ref.shape[0]))
            return _bdot(a_ref[...], b_ref[...], dn)

        def done(v):
            if finish is None:
                outs[0][...] = v.astype(outs[0].dtype)
            else:
                finish(v, refs[2:2 + len(extra)], outs)

        if nk == 1:
            done(prod())
        else:
            acc = refs[-1]
            k = pl.program_id(2)

            @pl.when(k == 0)
            def _():
                acc[...] = prod()

            @pl.when(k > 0)
            def _():
                acc[...] += prod()

            @pl.when(k == nk - 1)
            def _():
                done(acc[...])

    in_specs = [a_spec, b_spec, *extra_specs] + ([_dep_spec(3)] if dep is not None else [])
    args = (a, b, *extra) + ((dep,) if dep is not None else ())
    scratch = [] if nk == 1 else [pltpu.VMEM(acc_shape, F32)]
    return pl.pallas_call(
        body, name=name, grid=grid, in_specs=in_specs, out_specs=o_spec, out_shape=out_shape,
        scratch_shapes=scratch, compiler_params=_params(sem),
    )(*args)


def _rms(x):
    return lax.rsqrt(jnp.mean(x * x, axis=-1, keepdims=True) + EPS)


def _residual_finish(v, ex, outs):
    h = v + ex[0][...]
    outs[0][...] = h
    if len(ex) > 1:
        outs[1][...] = (h * _rms(h) * ex[1][...]).astype(outs[1].dtype)


def _norm_bwd_finish(v, ex, outs):
    x = ex[0][...]
    r = _rms(x)
    xh = x * r
    dyg = v * ex[1][...]
    dh = ex[2][...] + r * (dyg - xh * jnp.mean(dyg * xh, axis=-1, keepdims=True))
    outs[0][...] = dh
    outs[2][...] = dh.astype(outs[2].dtype)
    part = jnp.sum(v * xh, axis=0, keepdims=True)

    @pl.when(pl.program_id(0) == 0)
    def _():
        outs[1][...] = part

    @pl.when(pl.program_id(0) > 0)
    def _():
        outs[1][...] += part


def _row_io(tm, norm_g):
    row = pl.BlockSpec((tm, D_MODEL), lambda i, j, k: (i, 0))
    vec = pl.BlockSpec((1, D_MODEL), lambda i, j, k: (0, 0))
    if norm_g is None:
        return (row,), row, lambda S: _sds((S, D_MODEL), F32)
    return (row, vec), [row, row], lambda S: [_sds((S, D_MODEL), F32), _sds((S, D_MODEL), BF16)]


def _tile(n, t):
    return min(n, t)


def _proj_rows(name, hn, wt, l, out_dtype, dep=None):
    S, N = hn.shape[0], wt.shape[1]
    tm, tn = _tile(S, WIDE_ROW_TILE), 512
    return _matmul(
        name, hn, wt, dn=NT, grid=(S // tm, N // tn, 1),
        a_spec=pl.BlockSpec((tm, D_MODEL), lambda i, j, k: (i, 0)),
        b_spec=pl.BlockSpec((None, tn, D_MODEL), lambda i, j, k: (l, j, 0)),
        o_spec=pl.BlockSpec((tm, tn), lambda i, j, k: (i, j)),
        out_shape=_sds((S, N), out_dtype), dep=dep)


def _slot_weight(w, transposed):
    if transposed:
        return w.shape[2], (None, None, w.shape[2], D_MODEL), NT, NN
    return w.shape[3], (None, None, D_MODEL, w.shape[3]), NN, NT


def _proj_slots(name, hn, w, l, transposed=False, dep=None, out_dtype=F32):
    S = hn.shape[0]
    r, blk, dn, _ = _slot_weight(w, transposed)
    tm = _tile(S, WIDE_ROW_TILE)
    return _matmul(
        name, hn, w, dn=dn, grid=(N_DEV, S // tm, 1),
        a_spec=pl.BlockSpec((tm, D_MODEL), lambda j, i, k: (i, 0)),
        b_spec=pl.BlockSpec(blk, lambda j, i, k: (l, j, 0, 0)),
        o_spec=pl.BlockSpec((None, tm, r), lambda j, i, k: (j, i, 0)),
        out_shape=_sds((N_DEV, S, r), out_dtype), dep=dep)


def _out_proj(name, o, w, l, h, norm_g=None):
    S, K = o.shape
    tm = _tile(S, ROW_TILE)
    extra_specs, o_spec, out_shape = _row_io(tm, norm_g)
    return _matmul(
        name, o, w, dn=NN, grid=(S // tm, 1, 1),
        a_spec=pl.BlockSpec((tm, K), lambda i, j, k: (i, 0)),
        b_spec=pl.BlockSpec((None, K, D_MODEL), lambda i, j, k: (l, 0, 0)),
        o_spec=o_spec, out_shape=out_shape(S), extra=(h,) if norm_g is None else (h, norm_g),
        extra_specs=extra_specs, finish=_residual_finish)


def _down_proj(name, a, w, l, h, norm_g=None):
    nj, S, r = a.shape
    tm = _tile(S, ROW_TILE)
    extra_specs, o_spec, out_shape = _row_io(tm, norm_g)
    return _matmul(
        name, a, w, dn=NN, grid=(S // tm, 1, nj // SLOTS_PER_STEP),
        a_spec=pl.BlockSpec((SLOTS_PER_STEP, tm, r), lambda i, j, k: (k, i, 0)),
        b_spec=pl.BlockSpec((None, SLOTS_PER_STEP, r, D_MODEL), lambda i, j, k: (l, k, 0, 0)),
        o_spec=o_spec, out_shape=out_shape(S), acc_shape=(tm, D_MODEL),
        extra=(h,) if norm_g is None else (h, norm_g), extra_specs=extra_specs, finish=_residual_finish)


def _dgrad_down(name, dh, w, l, dep=None):
    S = dh.shape[0]
    nj, r = w.shape[1], w.shape[2]
    tm = _tile(S, WIDE_ROW_TILE)
    return _matmul(
        name, dh, w, dn=NT, grid=(nj, S // tm, 1),
        a_spec=pl.BlockSpec((tm, D_MODEL), lambda j, i, k: (i, 0)),
        b_spec=pl.BlockSpec((None, None, r, D_MODEL), lambda j, i, k: (l, j, 0, 0)),
        o_spec=pl.BlockSpec((None, tm, r), lambda j, i, k: (j, i, 0)),
        out_shape=_sds((nj, S, r), BF16), dep=dep)


def _wgrad_down(name, a, dh):
    nj, S, r = a.shape
    tk = S
    return _matmul(
        name, a, dh, dn=TN, grid=(nj, 1, S // tk),
        a_spec=pl.BlockSpec((None, tk, r), lambda s, j, k: (s, k, 0)),
        b_spec=pl.BlockSpec((tk, D_MODEL), lambda s, j, k: (k, 0)),
        o_spec=pl.BlockSpec((None, r, D_MODEL), lambda s, j, k: (s, 0, 0)),
        out_shape=_sds((nj, r, D_MODEL), BF16), acc_shape=(r, D_MODEL))


def _norm_bwd_io(tm, S):
    row = pl.BlockSpec((tm, D_MODEL), lambda i, j, k: (i, 0))
    vec = pl.BlockSpec((1, D_MODEL), lambda i, j, k: (0, 0))
    return dict(extra_specs=(row, vec, row), o_spec=[row, vec, row],
                out_shape=[_sds((S, D_MODEL), F32), _sds((1, D_MODEL), F32), _sds((S, D_MODEL), BF16)],
                finish=_norm_bwd_finish, sem=("arbitrary", "arbitrary", "arbitrary"))


def _dgrad_slots(name, dz, w, l, norm, transposed=False):
    nj, S, r = dz.shape
    _, blk, _, dn = _slot_weight(w, transposed)
    tm = _tile(S, ROW_TILE)
    return _matmul(
        name, dz, w, dn=dn, grid=(S // tm, 1, nj // SLOTS_PER_STEP),
        a_spec=pl.BlockSpec((SLOTS_PER_STEP, tm, r), lambda i, j, k: (k, i, 0)),
        b_spec=pl.BlockSpec((None, SLOTS_PER_STEP) + blk[2:], lambda i, j, k: (l, k, 0, 0)),
        acc_shape=(tm, D_MODEL), extra=norm, **_norm_bwd_io(tm, S))


def _wgrad_slots(name, hn, dz, transposed=False):
    nj, S, r = dz.shape
    tk = S
    hn_spec = pl.BlockSpec((tk, D_MODEL), lambda s, j, k: (k, 0))
    dz_spec = pl.BlockSpec((None, tk, r), lambda s, j, k: (s, k, 0))
    if transposed:
        return _matmul(
            name, dz, hn, dn=TN, grid=(nj, 1, S // tk), a_spec=dz_spec, b_spec=hn_spec,
            o_spec=pl.BlockSpec((None, r, D_MODEL), lambda s, j, k: (s, 0, 0)),
            out_shape=_sds((nj, r, D_MODEL), BF16), acc_shape=(r, D_MODEL))
    return _matmul(
        name, hn, dz, dn=TN, grid=(nj, 1, S // tk), a_spec=hn_spec, b_spec=dz_spec,
        o_spec=pl.BlockSpec((None, D_MODEL, r), lambda s, j, k: (s, 0, 0)),
        out_shape=_sds((nj, D_MODEL, r), BF16), acc_shape=(D_MODEL, r))


def _dgrad_out(name, dh, w, l, out_dtype, dep=None):
    S, K = dh.shape[0], w.shape[1]
    tm = _tile(S, WIDE_ROW_TILE)
    return _matmul(
        name, dh, w, dn=NT, grid=(S // tm, 1, 1),
        a_spec=pl.BlockSpec((tm, D_MODEL), lambda i, j, k: (i, 0)),
        b_spec=pl.BlockSpec((None, K, D_MODEL), lambda i, j, k: (l, 0, 0)),
        o_spec=pl.BlockSpec((tm, K), lambda i, j, k: (i, 0)),
        out_shape=_sds((S, K), out_dtype), dep=dep)


def _wgrad_rows(name, a, b):
    S, K = a.shape
    tk = S
    return _matmul(
        name, a, b, dn=TN, grid=(1, 1, S // tk),
        a_spec=pl.BlockSpec((tk, K), lambda i, j, k: (k, 0)),
        b_spec=pl.BlockSpec((tk, D_MODEL), lambda i, j, k: (k, 0)),
        o_spec=pl.BlockSpec((K, D_MODEL), lambda i, j, k: (0, 0)),
        out_shape=_sds((K, D_MODEL), BF16), acc_shape=(K, D_MODEL))


def _dgrad_rows(name, dz, wt, l, norm):
    S, N = dz.shape
    tm = _tile(S, ROW_TILE)
    return _matmul(
        name, dz, wt, dn=NN, grid=(S // tm, 1, 1),
        a_spec=pl.BlockSpec((tm, N), lambda i, j, k: (i, 0)),
        b_spec=pl.BlockSpec((None, N, D_MODEL), lambda i, j, k: (l, 0, 0)),
        extra=norm, **_norm_bwd_io(tm, S))


def _rmsnorm_fwd(name, h, g):
    S = h.shape[0]
    tm = _tile(S, ROW_TILE)

    def body(h_ref, g_ref, o_ref):
        x = h_ref[...]
        o_ref[...] = (x * _rms(x) * g_ref[...]).astype(o_ref.dtype)

    row = pl.BlockSpec((tm, D_MODEL), lambda i: (i, 0))
    return pl.pallas_call(
        body, name=name, grid=(S // tm,), in_specs=[row, pl.BlockSpec((1, D_MODEL), lambda i: (0, 0))],
        out_specs=row, out_shape=_sds((S, D_MODEL), BF16), compiler_params=_params(("parallel",)),
    )(h, g)


def _loss_head(name, h, g, target):
    S = h.shape[0]
    tm = _tile(S, ROW_TILE)

    def body(h_ref, g_ref, t_ref, dh_ref, dg_ref, ls_ref, dhb_ref):
        x = h_ref[...]
        r = _rms(x)
        xh = x * r
        diff = xh * g_ref[...] - t_ref[...]
        dyf = diff * (1.0 / D_MODEL)
        dyg = dyf * g_ref[...]
        dh = r * (dyg - xh * jnp.mean(dyg * xh, axis=-1, keepdims=True))
        dh_ref[...] = dh
        dhb_ref[...] = dh.astype(dhb_ref.dtype)
        part = jnp.sum(dyf * xh, axis=0, keepdims=True)
        lpart = jnp.sum(diff * diff, axis=0, keepdims=True) * (0.5 / D_MODEL)

        @pl.when(pl.program_id(0) == 0)
        def _():
            dg_ref[...] = part
            ls_ref[...] = lpart

        @pl.when(pl.program_id(0) > 0)
        def _():
            dg_ref[...] += part
            ls_ref[...] += lpart

    row = pl.BlockSpec((tm, D_MODEL), lambda i: (i, 0))
    vec = pl.BlockSpec((1, D_MODEL), lambda i: (0, 0))
    return pl.pallas_call(
        body, name=name, grid=(S // tm,), in_specs=[row, vec, row], out_specs=[row, vec, vec, row],
        out_shape=[_sds((S, D_MODEL), F32), _sds((1, D_MODEL), F32), _sds((1, D_MODEL), F32), _sds((S, D_MODEL), BF16)],
        compiler_params=_params(("arbitrary",)),
    )(h, g, target)


ATTN_SCALE = HEAD_DIM ** -0.5
ALIBI_SLOPES = [2.0 ** (-8.0 * (h + 1) / N_Q_HEADS) for h in range(N_Q_HEADS)]
K_COL = N_Q_HEADS * HEAD_DIM
KV_COLS = N_KV_HEADS * HEAD_DIM
V_COL = K_COL + KV_COLS


def _attn_masks(n):
    qi = lax.broadcasted_iota(jnp.int32, (ATTN_BLOCK, ATTN_BLOCK), 0)
    ki = lax.broadcasted_iota(jnp.int32, (ATTN_BLOCK, ATTN_BLOCK), 1)
    dist_c = (qi - ki).astype(F32)
    return dist_c + float(ATTN_BLOCK), dist_c, (ki > qi) & (n > 0), qi >= ki


def _attn_probs(raw_p, raw_c, sink, slope, masks):
    dist_p, dist_c, valid_p, valid_c = masks
    sp = jnp.where(valid_p, raw_p * ATTN_SCALE - slope * dist_p, NEG_BIG)
    sc = jnp.where(valid_c, raw_c * ATTN_SCALE - slope * dist_c, NEG_BIG)
    m = jnp.maximum(jnp.maximum(jnp.max(sp, axis=-1, keepdims=True), jnp.max(sc, axis=-1, keepdims=True)), sink)
    ep, ec, es = jnp.exp(sp - m), jnp.exp(sc - m), jnp.exp(sink - m)
    inv = 1.0 / (jnp.sum(ep, axis=-1, keepdims=True) + jnp.sum(ec, axis=-1, keepdims=True) + es)
    return ep * inv, ec * inv, es * inv


def _group_rows(ref, m):
    return jnp.concatenate([ref[:, HEAD_DIM * (Q_PER_KV * m + g):HEAD_DIM * (Q_PER_KV * m + g + 1)]
                            for g in range(Q_PER_KV)], axis=0)


def _head_rows(x, g):
    return x[ATTN_BLOCK * g:ATTN_BLOCK * (g + 1)]


def _attn_specs(nblk):
    last = nblk - 1
    kcol, vcol = K_COL // KV_COLS, V_COL // KV_COLS
    return [
        pl.BlockSpec((ATTN_BLOCK, K_COL), lambda n: (jnp.minimum(n, last), 0)),
        pl.BlockSpec((ATTN_BLOCK, KV_COLS), lambda n: (jnp.minimum(n, last), kcol)),
        pl.BlockSpec((ATTN_BLOCK, KV_COLS), lambda n: (jnp.maximum(jnp.minimum(n, last) - 1, 0), kcol)),
        pl.BlockSpec((ATTN_BLOCK, KV_COLS), lambda n: (jnp.minimum(n, last), vcol)),
        pl.BlockSpec((ATTN_BLOCK, KV_COLS), lambda n: (jnp.maximum(jnp.minimum(n, last) - 1, 0), vcol)),
    ]


P_COLS = 2 * ATTN_BLOCK


def _attn_fwd(name, proj, sinks):
    S = proj.shape[0]
    nblk = S // ATTN_BLOCK

    def body(q_ref, kc_ref, kp_ref, vc_ref, vp_ref, sk_ref, o_ref, p_ref, ps_ref):
        masks = _attn_masks(pl.program_id(0))
        lane = lax.broadcasted_iota(jnp.int32, (ATTN_BLOCK, 128), 1)
        sink_p = jnp.zeros((ATTN_BLOCK, 128), F32)
        for m in range(N_KV_HEADS):
            ks = slice(HEAD_DIM * m, HEAD_DIM * (m + 1))
            kp, kc, vp, vc = kp_ref[:, ks], kc_ref[:, ks], vp_ref[:, ks], vc_ref[:, ks]
            q4 = _group_rows(q_ref, m)
            raw_p, raw_c = _bdot(q4, kp, NT), _bdot(q4, kc, NT)
            pps, pcs = [], []
            for g in range(Q_PER_KV):
                hh = Q_PER_KV * m + g
                pp, pc, ps = _attn_probs(_head_rows(raw_p, g), _head_rows(raw_c, g), sk_ref[0, hh], ALIBI_SLOPES[hh], masks)
                pps.append(pp.astype(BF16))
                pcs.append(pc.astype(BF16))
                p_ref[:, P_COLS * hh:P_COLS * hh + ATTN_BLOCK] = pps[g]
                p_ref[:, P_COLS * hh + ATTN_BLOCK:P_COLS * (hh + 1)] = pcs[g]
                sink_p = jnp.where(lane == hh, ps, sink_p)
            o4 = _bdot(jnp.concatenate(pps, axis=0), vp, NN) + _bdot(jnp.concatenate(pcs, axis=0), vc, NN)
            for g in range(Q_PER_KV):
                hh = Q_PER_KV * m + g
                o_ref[:, HEAD_DIM * hh:HEAD_DIM * (hh + 1)] = _head_rows(o4, g).astype(o_ref.dtype)
        ps_ref[...] = sink_p

    row = lambda cols: pl.BlockSpec((ATTN_BLOCK, cols), lambda n: (n, 0))
    return pl.pallas_call(
        body, name=name, grid=(nblk,),
        in_specs=_attn_specs(nblk) + [pl.BlockSpec(memory_space=pltpu.SMEM)],
        out_specs=[row(K_COL), row(N_Q_HEADS * P_COLS), row(128)],
        out_shape=[_sds((S, K_COL), BF16), _sds((S, N_Q_HEADS * P_COLS), BF16), _sds((S, 128), F32)],
        compiler_params=_params(("parallel",)),
    )(proj, proj, proj, proj, proj, sinks)


def _attn_bwd(name, proj, probs, sink_probs, do):
    S = proj.shape[0]
    nblk = S // ATTN_BLOCK

    def body(q_ref, kc_ref, kp_ref, vc_ref, vp_ref, do_ref, p_ref, ps_ref, dz_ref, ds_ref, carry, cur, padd):
        n = pl.program_id(0)

        @pl.when(n == 0)
        def _():
            carry[...] = jnp.zeros_like(carry)
            ds_ref[...] = jnp.zeros_like(ds_ref)

        @pl.when(n < nblk)
        def _():
            lane = lax.broadcasted_iota(jnp.int32, (ATTN_BLOCK, 128), 1)
            sink_p = ps_ref[...]
            dsv = jnp.zeros((1, 128), F32)
            for m in range(N_KV_HEADS):
                ks = slice(HEAD_DIM * m, HEAD_DIM * (m + 1))
                kp, kc, vp, vc = kp_ref[:, ks], kc_ref[:, ks], vp_ref[:, ks], vc_ref[:, ks]
                q4, do4 = _group_rows(q_ref, m), _group_rows(do_ref, m)
                dpp4, dpc4 = _bdot(do4, vp, NT), _bdot(do4, vc, NT)
                pps, pcs, dsps, dscs = [], [], [], []
                for g in range(Q_PER_KV):
                    hh = Q_PER_KV * m + g
                    pps.append(p_ref[:, P_COLS * hh:P_COLS * hh + ATTN_BLOCK])
                    pcs.append(p_ref[:, P_COLS * hh + ATTN_BLOCK:P_COLS * (hh + 1)])
                    pp, pc = pps[g].astype(F32), pcs[g].astype(F32)
                    dpp, dpc = _head_rows(dpp4, g), _head_rows(dpc4, g)
                    delta = jnp.sum(pp * dpp, axis=-1, keepdims=True) + jnp.sum(pc * dpc, axis=-1, keepdims=True)
                    dsv = dsv - jnp.sum(jnp.where(lane == hh, sink_p, 0.0) * delta, axis=0, keepdims=True)
                    dsps.append((pp * (dpp - delta)).astype(BF16))
                    dscs.append((pc * (dpc - delta)).astype(BF16))
                pp4, pc4 = jnp.concatenate(pps, axis=0), jnp.concatenate(pcs, axis=0)
                dsp4, dsc4 = jnp.concatenate(dsps, axis=0), jnp.concatenate(dscs, axis=0)
                dq4 = (_bdot(dsp4, kp, NN) + _bdot(dsc4, kc, NN)) * ATTN_SCALE
                for g in range(Q_PER_KV):
                    hh = Q_PER_KV * m + g
                    cur[:, HEAD_DIM * hh:HEAD_DIM * (hh + 1)] = _head_rows(dq4, g)
                cur[:, K_COL + HEAD_DIM * m:K_COL + HEAD_DIM * (m + 1)] = _bdot(dsc4, q4, TN) * ATTN_SCALE
                cur[:, V_COL + HEAD_DIM * m:V_COL + HEAD_DIM * (m + 1)] = _bdot(pc4, do4, TN)
                padd[:, ks] = _bdot(dsp4, q4, TN) * ATTN_SCALE
                padd[:, KV_COLS + HEAD_DIM * m:KV_COLS + HEAD_DIM * (m + 1)] = _bdot(pp4, do4, TN)
            ds_ref[...] += dsv
            dz_ref[:, :K_COL] = carry[:, :K_COL].astype(dz_ref.dtype)
            dz_ref[:, K_COL:] = (carry[:, K_COL:] + padd[...]).astype(dz_ref.dtype)
            carry[...] = cur[...]

        @pl.when(n == nblk)
        def _():
            dz_ref[...] = carry[...].astype(dz_ref.dtype)

    return pl.pallas_call(
        body, name=name, grid=(nblk + 1,),
        in_specs=_attn_specs(nblk) + [
            pl.BlockSpec((ATTN_BLOCK, cols), lambda n: (jnp.minimum(n, nblk - 1), 0))
            for cols in (K_COL, N_Q_HEADS * P_COLS, 128)],
        out_specs=[pl.BlockSpec((ATTN_BLOCK, ATTN_IN), lambda n: (jnp.maximum(n - 1, 0), 0)),
                   pl.BlockSpec((1, 128), lambda n: (0, 0))],
        out_shape=[_sds((S, ATTN_IN), BF16), _sds((1, 128), F32)],
        scratch_shapes=[pltpu.VMEM((ATTN_BLOCK, ATTN_IN), F32), pltpu.VMEM((ATTN_BLOCK, ATTN_IN), F32),
                        pltpu.VMEM((ATTN_BLOCK, 2 * KV_COLS), F32)],
        compiler_params=_params(("arbitrary",)),
    )(proj, proj, proj, proj, proj, do, probs, sink_probs)


def _hg_consts():
    C = HG_CHUNK
    tri = np.tril(np.ones((C, C)))
    t = np.arange(C)
    rows, masks = [tri], []
    for lvl in range(HG_LEVELS):
        n = C >> (lvl + 1)
        sel = np.zeros((C, C))
        sel[t, (t // (2 * n)) * (2 * n) + n - 1] = 1.0
        rows.append(sel @ tri)
        tt, ss = t[:, None], t[None, :]
        masks.append((tt // (2 * n) == ss // (2 * n)) & ((tt // n) % 2 == 1) & ((ss // n) % 2 == 0))
    masks.append(np.eye(C, dtype=bool))
    stk = np.concatenate(rows, axis=0)
    return jnp.asarray(stk, BF16), jnp.asarray(np.stack(masks), F32)


def _sigmoid(x):
    return 1.0 / (1.0 + jnp.exp(-x))


def _silu_sigmoid(x):
    return 0.5 + 0.5 * jnp.tanh(0.5 * x)


def _split(x, parts):
    out, rest = [], x
    for _ in range(parts):
        out.append(rest.astype(BF16))
        rest = rest - out[-1].astype(F32)
    return out


def _dot01(m01, x, dn, parts=3):
    return sum(lax.dot_general(m01, p, dn, preferred_element_type=F32) for p in _split(x, parts))


def _ref_rows(b, n):
    C = b.shape[1]
    if 2 * n >= 8:
        b3 = b.reshape(HG_CHUNK // (2 * n), 2 * n, C)
        return jnp.broadcast_to(b3[:, n - 1:n, :], b3.shape).reshape(HG_CHUNK, C)
    pos = lax.broadcasted_iota(jnp.int32, b.shape, 0) % (2 * n)
    out = b
    for p in range(2 * n):
        if p != n - 1:
            out = jnp.where(pos == p, pltpu.roll(b, (p - (n - 1)) % HG_CHUNK, 0), out)
    return out


HG_STEP_CHUNKS = 4


def _chunk_rows(ci):
    return pl.ds(pl.multiple_of(ci * HG_CHUNK, HG_CHUNK), HG_CHUNK)


def _hg_common(z_ref, rows, lb_ref, stk_ref):
    qr, fr = z_ref[0, rows, :], z_ref[1, rows, :]
    lb = lb_ref[...]
    sq, sg, sgn = _silu_sigmoid(qr), _sigmoid(fr), _sigmoid(-fr)
    ft = lb + (1.0 - lb) * sg
    b = _dot01(stk_ref[0:HG_CHUNK, :], jnp.log(ft), NN)
    ws = [jnp.exp(-jnp.abs(b - _ref_rows(b, HG_CHUNK >> (l + 1)))) for l in range(HG_LEVELS)]
    blast = b[HG_CHUNK - 1:HG_CHUNK]
    return dict(qr=qr, fr=fr, lb=lb, sq=sq, sg=sg, sgn=sgn, ft=ft, q=qr * sq, kk=(1.0 - lb) * sgn, b=b,
                ws=ws, eb=jnp.exp(b), ed=jnp.exp(blast - b), elast=jnp.exp(blast))


def _hg_factors(qh, kh, ws, sl):
    return ([(qh * ws[l][:, sl]).astype(BF16) for l in range(HG_LEVELS)],
            [(kh * ws[l][:, sl]).astype(BF16) for l in range(HG_LEVELS)])


def _hg_intra(qh, kh, ws, msk_ref, sl):
    qls, kls = _hg_factors(qh, kh, ws, sl)
    a = msk_ref[HG_LEVELS] * _bdot(qh, kh, NT)
    for l in range(HG_LEVELS):
        a = a + msk_ref[l] * _bdot(qls[l], kls[l], NT)
    return a


def _hg_fwd(name, z, lb, ng):
    S = z.shape[2]
    nc = S // HG_CHUNK
    per = min(HG_STEP_CHUNKS, nc)
    stk, msk = _hg_consts()

    def body(z_ref, lb_ref, ng_ref, stk_ref, msk_ref, og_ref, st_ref, a_ref, o_ref, state):
        @pl.when(pl.program_id(1) == 0)
        def _():
            state[...] = jnp.zeros_like(state)

        def chunk(ci, _):
            rows = _chunk_rows(ci)
            cm = _hg_common(z_ref, rows, lb_ref, stk_ref)
            v, gt = z_ref[2, rows, :], z_ref[3, rows, :]
            kd = cm["kk"] * cm["ed"]
            for hh in range(4):
                sl = slice(HG_K * hh, HG_K * (hh + 1))
                st = state[hh]
                st_ref[ci, hh] = st
                qh, kh, vh = cm["q"][:, sl], cm["kk"][:, sl], v[:, sl]
                a = _hg_intra(qh, kh, cm["ws"], msk_ref, sl).astype(BF16)
                a_ref[ci, hh] = a
                o = _bdot(a, vh, NN) + _bdot(qh * cm["eb"][:, sl], st, NT)
                o_ref[rows, sl] = o
                state[hh] = cm["elast"][:, sl] * st + _bdot(vh, kd[:, sl], TN)
                gh = gt[:, sl]
                og_ref[rows, sl] = (o * _rms(o) * ng_ref[...] * (gh * _silu_sigmoid(gh))).astype(og_ref.dtype)
            return 0

        lax.fori_loop(0, per, chunk, 0, unroll=True)

    return pl.pallas_call(
        body, name=name, grid=(2, nc // per),
        in_specs=[pl.BlockSpec((4, None, per * HG_CHUNK, HG_SLOT), lambda g, c: (0, g, c, 0)),
                  pl.BlockSpec((1, HG_SLOT), lambda g, c: (0, g)),
                  pl.BlockSpec((1, HG_K), lambda g, c: (0, 0)),
                  pl.BlockSpec(stk.shape, lambda g, c: (0, 0)),
                  pl.BlockSpec(msk.shape, lambda g, c: (0, 0, 0))],
        out_specs=[pl.BlockSpec((per * HG_CHUNK, HG_SLOT), lambda g, c: (c, g)),
                   pl.BlockSpec((per, 4, HG_K, HG_K), lambda g, c: (c, g, 0, 0)),
                   pl.BlockSpec((per, 4, HG_CHUNK, HG_CHUNK), lambda g, c: (c, g, 0, 0)),
                   pl.BlockSpec((per * HG_CHUNK, HG_SLOT), lambda g, c: (c, g))],
        out_shape=[_sds((S, D_MODEL), BF16), _sds((nc, HG_HEADS, HG_K, HG_K), F32),
                   _sds((nc, HG_HEADS, HG_CHUNK, HG_CHUNK), BF16), _sds((S, D_MODEL), F32)],
        scratch_shapes=[pltpu.VMEM((4, HG_K, HG_K), F32)],
        compiler_params=_params(("parallel", "arbitrary")),
    )(z, lb, ng, stk, msk)


def _hg_bwd(name, z, lb, ng, states, intra, o_pre, dog):
    S = z.shape[2]
    nc = S // HG_CHUNK
    per = min(HG_STEP_CHUNKS, nc)
    stk, msk = _hg_consts()

    def body(z_ref, lb_ref, ng_ref, stk_ref, msk_ref, st_ref, a_ref, o_ref, dog_ref, dz_ref, dlb_ref, dng_ref, dstate):
        @pl.when(pl.program_id(1) == 0)
        def _():
            dstate[...] = jnp.zeros_like(dstate)
            dlb_ref[...] = jnp.zeros_like(dlb_ref)
            dng_ref[...] = jnp.zeros_like(dng_ref)

        def chunk(k, _):
            ci = per - 1 - k
            rows = _chunk_rows(ci)
            cm = _hg_common(z_ref, rows, lb_ref, stk_ref)
            v, gt = z_ref[2, rows, :], z_ref[3, rows, :]
            ng = ng_ref[...]
            kd = cm["kk"] * cm["ed"]
            row = lax.broadcasted_iota(jnp.int32, (HG_CHUNK, 1), 0)
            dng = jnp.zeros((1, HG_K), F32)
            dq_h, dkk_h, db_h, dv_h, dgt_h = [], [], [], [], []
            dr_h = [[] for _ in range(HG_LEVELS)]
            for hh in range(4):
                sl = slice(HG_K * hh, HG_K * (hh + 1))
                st, dst = st_ref[ci, hh], dstate[hh]
                qh, kh, vh, ebh, edh, kdh = cm["q"][:, sl], cm["kk"][:, sl], v[:, sl], cm["eb"][:, sl], cm["ed"][:, sl], kd[:, sl]
                elh = cm["elast"][:, sl]
                qls, kls = _hg_factors(qh, kh, cm["ws"], sl)
                a, o = a_ref[ci, hh], o_ref[rows, sl]
                qe = qh * ebh
                r = _rms(o)
                xh = o * r
                gh = gt[:, sl]
                sgg = _silu_sigmoid(gh)
                dog = dog_ref[rows, sl].astype(F32)
                dy = dog * (gh * sgg)
                dgt_h.append(dog * (xh * ng) * (sgg * (1.0 + gh * (1.0 - sgg))))
                dng = dng + jnp.sum(dy * xh, axis=0, keepdims=True)
                dyg = dy * ng
                do = r * (dyg - xh * jnp.mean(dyg * xh, axis=-1, keepdims=True))
                da = _bdot(do, vh, NT)
                dv_h.append(_bdot(a, do, TN) + _bdot(kdh, dst, NT))
                dkd = _bdot(vh, dst, NN)
                delast = jnp.sum(st * dst, axis=0, keepdims=True)
                dqe = _bdot(do, st, NN)
                dstate[hh] = elh * dst + _bdot(do, qe, TN)
                gk = dkd * kdh
                dblast = jnp.sum(gk, axis=0, keepdims=True) + delast * elh
                db = dqe * qe - gk + jnp.where(row == HG_CHUNK - 1, dblast, 0.0)
                dp = (msk_ref[HG_LEVELS] * da).astype(BF16)
                dq = dqe * ebh + _bdot(dp, kh, NN)
                dkk = dkd * edh + _bdot(dp, qh, TN)
                for l in range(HG_LEVELS):
                    dp = (msk_ref[l] * da).astype(BF16)
                    dql, dkl = _bdot(dp, kls[l], NN), _bdot(dp, qls[l], TN)
                    w = cm["ws"][l][:, sl]
                    dq = dq + dql * w
                    dkk = dkk + dkl * w
                    half = jnp.where(((row >> (HG_LEVELS - 1 - l)) & 1) == 1, 1.0, -1.0)
                    dd = half * w * (dql * qh + dkl * kh)
                    db = db + dd
                    dr_h[l].append(-dd)
                dq_h.append(dq)
                dkk_h.append(dkk)
                db_h.append(db)
            cat = lambda xs: jnp.concatenate(xs, axis=1)
            cot = jnp.concatenate([cat(db_h)] + [cat(dr_h[l]) for l in range(HG_LEVELS)], axis=0)
            dlf = _dot01(stk_ref[...], cot, TN, parts=2)
            dq, dkk = cat(dq_h), cat(dkk_h)
            dft = dlf / cm["ft"]
            one_lb = 1.0 - cm["lb"]
            dz_ref[0, rows, :] = (dq * (cm["sq"] * (1.0 + cm["qr"] * (1.0 - cm["sq"])))).astype(dz_ref.dtype)
            dz_ref[1, rows, :] = ((dft - dkk) * one_lb * cm["sg"] * cm["sgn"]).astype(dz_ref.dtype)
            dz_ref[2, rows, :] = cat(dv_h).astype(dz_ref.dtype)
            dz_ref[3, rows, :] = cat(dgt_h).astype(dz_ref.dtype)
            dlb_ref[...] += jnp.sum((dft - dkk) * cm["sgn"], axis=0, keepdims=True)
            dng_ref[...] += dng
            return 0

        lax.fori_loop(0, per, chunk, 0, unroll=True)

    rev = lambda c: nc // per - 1 - c
    rows_blk = pl.BlockSpec((per * HG_CHUNK, HG_SLOT), lambda g, c: (rev(c), g))
    return pl.pallas_call(
        body, name=name, grid=(2, nc // per),
        in_specs=[pl.BlockSpec((4, None, per * HG_CHUNK, HG_SLOT), lambda g, c: (0, g, rev(c), 0)),
                  pl.BlockSpec((1, HG_SLOT), lambda g, c: (0, g)),
                  pl.BlockSpec((1, HG_K), lambda g, c: (0, 0)),
                  pl.BlockSpec(stk.shape, lambda g, c: (0, 0)),
                  pl.BlockSpec(msk.shape, lambda g, c: (0, 0, 0)),
                  pl.BlockSpec((per, 4, HG_K, HG_K), lambda g, c: (rev(c), g, 0, 0)),
                  pl.BlockSpec((per, 4, HG_CHUNK, HG_CHUNK), lambda g, c: (rev(c), g, 0, 0)),
                  rows_blk, rows_blk],
        out_specs=[pl.BlockSpec((4, None, per * HG_CHUNK, HG_SLOT), lambda g, c: (0, g, rev(c), 0)),
                   pl.BlockSpec((1, HG_SLOT), lambda g, c: (0, g)),
                   pl.BlockSpec((None, 1, HG_K), lambda g, c: (g, 0, 0))],
        out_shape=[_sds(z.shape, BF16), _sds((1, 2 * HG_SLOT), F32), _sds((2, 1, HG_K), F32)],
        scratch_shapes=[pltpu.VMEM((4, HG_K, HG_K), F32)],
        compiler_params=_params(("parallel", "arbitrary")),
    )(z, lb, ng, stk, msk, states, intra, o_pre, dog)


def _lb_fwd(name, logits):
    def body(l_ref, o_ref):
        x = l_ref[...]
        e = jnp.exp(x - jnp.max(x, axis=0, keepdims=True))
        s = e / jnp.sum(e, axis=0, keepdims=True)
        o_ref[0:1, :] = s[1:2]
        o_ref[1:2, :] = s[1:2] + s[2:3] + s[3:4]

    return pl.pallas_call(body, name=name, out_shape=_sds((2, logits.shape[1]), F32))(logits)


def _lb_bwd(name, logits, dlb):
    def body(l_ref, d_ref, o_ref):
        x = l_ref[...]
        e = jnp.exp(x - jnp.max(x, axis=0, keepdims=True))
        s = e / jnp.sum(e, axis=0, keepdims=True)
        d1, d3 = d_ref[0:1, :], d_ref[1:2, :]
        ds = [jnp.zeros_like(d1), d1 + d3, d3, d3]
        dot = sum(ds[r] * s[r:r + 1] for r in range(1, DEPTH))
        for r in range(DEPTH):
            o_ref[r:r + 1, :] = s[r:r + 1] * (ds[r] - dot)

    return pl.pallas_call(body, name=name, out_shape=_sds(logits.shape, F32))(logits, dlb)


SUB = 8


def _rows_down(x, prev, k):
    row = lax.broadcasted_iota(jnp.int32, x.shape, 0)
    return jnp.where(row >= k, pltpu.roll(x, k, 0), pltpu.roll(prev, k, 0))


def _rows_up(x, nxt, k):
    row = lax.broadcasted_iota(jnp.int32, x.shape, 0)
    return jnp.where(row < SUB - k, pltpu.roll(x, SUB - k, 0), pltpu.roll(nxt, SUB - k, 0))


def _conv_block(w_ref, b_ref, p, x, prev):
    return (b_ref[p] + w_ref[p, 0:1, :] * _rows_down(x, prev, 2) + w_ref[p, 1:2, :] * _rows_down(x, prev, 1)
            + w_ref[p, 2:3, :] * x)


def _convgate_fwd(name, u, cw, cb):
    S = u.shape[2]
    tm = _tile(S, ROW_TILE)

    def body(u_ref, w_ref, b_ref, a_ref, c_ref, halo):
        @pl.when(pl.program_id(1) == 0)
        def _():
            halo[...] = jnp.zeros_like(halo)

        def step(r, prev):
            pg, pv = prev
            out, cgs, cvs = [], [], []
            rows = pl.ds(pl.multiple_of(r * 2 * SUB, 2 * SUB), 2 * SUB)
            ug16, uv16 = u_ref[0, rows, :].astype(F32), u_ref[1, rows, :].astype(F32)
            for s in range(2):
                xg, xv = ug16[s * SUB:(s + 1) * SUB], uv16[s * SUB:(s + 1) * SUB]
                cgs.append(_conv_block(w_ref, b_ref, 0, xg, pg))
                cvs.append(_conv_block(w_ref, b_ref, 1, xv, pv))
                out.append(cgs[s] * _silu_sigmoid(cgs[s]) * cvs[s])
                pg, pv = xg, xv
            a_ref[rows, :] = jnp.concatenate(out, axis=0).astype(a_ref.dtype)
            c_ref[0, rows, :] = jnp.concatenate(cgs, axis=0).astype(c_ref.dtype)
            c_ref[1, rows, :] = jnp.concatenate(cvs, axis=0).astype(c_ref.dtype)
            return pg, pv

        pg, pv = lax.fori_loop(0, tm // (2 * SUB), step, (halo[0], halo[1]), unroll=2)
        halo[0] = pg
        halo[1] = pv

    pair = pl.BlockSpec((2, None, tm, FF_SLOT), lambda j, t: (0, j, t, 0))
    return pl.pallas_call(
        body, name=name, grid=(4, S // tm),
        in_specs=[pair, pl.BlockSpec((2, None, 3, FF_SLOT), lambda j, t: (0, j, 0, 0)),
                  pl.BlockSpec((2, None, 1, FF_SLOT), lambda j, t: (0, j, 0, 0))],
        out_specs=[pl.BlockSpec((None, tm, FF_SLOT), lambda j, t: (j, t, 0)), pair],
        out_shape=[_sds((4, S, FF_SLOT), BF16), _sds(u.shape, BF16)],
        scratch_shapes=[pltpu.VMEM((2, SUB, FF_SLOT), F32)],
        compiler_params=_params(("parallel", "arbitrary")),
    )(u, cw, cb)


def _convgate_bwd(name, u, convs, cw, da):
    S = u.shape[2]
    tm = _tile(S, ROW_TILE)
    nt = S // tm

    def body(u_ref, c_ref, w_ref, da_ref, du_out, dw_ref, db_ref, after, first, acc, du_ref):
        @pl.when(pl.program_id(1) == 0)
        def _():
            after[...] = jnp.zeros_like(after)
            acc[...] = jnp.zeros_like(acc)

        def finish(p, x, d, nxt, rows):
            taps = (_rows_up(d, nxt, 2), _rows_up(d, nxt, 1), d)
            du_ref[p, rows, :] = w_ref[p, 0:1, :] * taps[0] + w_ref[p, 1:2, :] * taps[1] + w_ref[p, 2:3, :] * d
            for j in range(3):
                acc[p, j] += taps[j] * x
            acc[p, 3] += d

        def step(r, carry):
            xg_last, xv_last, dg_last, dv_last = carry
            rows16 = pl.ds(pl.multiple_of(r * 2 * SUB, 2 * SUB), 2 * SUB)
            dav = da_ref[rows16, :].astype(F32)
            cg16, cv16 = c_ref[0, rows16, :].astype(F32), c_ref[1, rows16, :].astype(F32)
            ug16, uv16 = u_ref[0, rows16, :].astype(F32), u_ref[1, rows16, :].astype(F32)
            for s in range(2):
                at = r * 2 * SUB + s * SUB
                part = slice(s * SUB, (s + 1) * SUB)
                cg, cv, dab = cg16[part], cv16[part], dav[part]
                sg = _silu_sigmoid(cg)
                dg = dab * cv * (sg * (1.0 + cg * (1.0 - sg)))
                dv = dab * cg * sg
                before = pl.ds(pl.multiple_of(at - SUB, SUB), SUB)
                if s == 0:
                    @pl.when(r == 0)
                    def _():
                        first[0] = dg
                        first[1] = dv

                    @pl.when(r > 0)
                    def _():
                        finish(0, xg_last, dg_last, dg, before)
                        finish(1, xv_last, dv_last, dv, before)
                else:
                    finish(0, xg_last, dg_last, dg, before)
                    finish(1, xv_last, dv_last, dv, before)
                xg_last, xv_last, dg_last, dv_last = ug16[part], uv16[part], dg, dv
            return xg_last, xv_last, dg_last, dv_last

        zero = jnp.zeros((SUB, FF_SLOT), F32)
        xg_last, xv_last, dg_last, dv_last = lax.fori_loop(0, tm // (2 * SUB), step, (zero, zero, zero, zero))
        finish(0, xg_last, dg_last, after[0], slice(tm - SUB, tm))
        finish(1, xv_last, dv_last, after[1], slice(tm - SUB, tm))
        du_out[...] = du_ref[...].astype(du_out.dtype)
        after[...] = first[...]
        for p in range(2):
            for j in range(3):
                dw_ref[p, j:j + 1, :] = jnp.sum(acc[p, j], axis=0, keepdims=True)
            db_ref[p] = jnp.sum(acc[p, 3], axis=0, keepdims=True)

    rev = lambda t: nt - 1 - t
    pair = pl.BlockSpec((2, None, tm, FF_SLOT), lambda j, t: (0, j, rev(t), 0))
    taps = pl.BlockSpec((2, None, 3, FF_SLOT), lambda j, t: (0, j, 0, 0))
    bias = pl.BlockSpec((2, None, 1, FF_SLOT), lambda j, t: (0, j, 0, 0))
    return pl.pallas_call(
        body, name=name, grid=(4, nt),
        in_specs=[pair, pair, taps, pl.BlockSpec((None, tm, FF_SLOT), lambda j, t: (j, rev(t), 0))],
        out_specs=[pair, taps, bias],
        out_shape=[_sds(u.shape, BF16), _sds(cw.shape, F32), _sds((2, 4, 1, FF_SLOT), F32)],
        scratch_shapes=[pltpu.VMEM((2, SUB, FF_SLOT), F32), pltpu.VMEM((2, SUB, FF_SLOT), F32),
                        pltpu.VMEM((2, 4, SUB, FF_SLOT), F32), pltpu.VMEM((2, tm, FF_SLOT), F32)],
        compiler_params=_params(("parallel", "arbitrary")),
    )(u, convs, cw, da)


def _row_tile(R):
    for t in range(256, 15, -16):
        if R % t == 0:
            return t
    return R


def _adamw(name, gsrcs, w, m, v, dep=None):
    L = len(gsrcs)
    n, A, C = gsrcs[0].shape
    tr = _row_tile(A)
    deps = () if dep is None else (dep,)

    def body(*refs):
        g_refs = refs[:L]
        w_ref, m_ref, v_ref = refs[L:L + 3]
        go_ref, d_ref, mo_ref, vo_ref = refs[L + 3 + len(deps):]
        for k in range(L):
            @pl.when(pl.program_id(0) == k)
            def _(k=k):
                g = g_refs[k][0].astype(F32)
                for s in range(1, n):
                    g = g + g_refs[k][s].astype(F32)
                m2 = ADAM_B1 * m_ref[...] + (1.0 - ADAM_B1) * g
                v2 = ADAM_B2 * v_ref[...] + (1.0 - ADAM_B2) * (g * g)
                m_hat = m2 / (1.0 - ADAM_B1 ** ADAM_STEP)
                v_hat = v2 / (1.0 - ADAM_B2 ** ADAM_STEP)
                go_ref[...] = g
                d_ref[...] = -ADAM_LR * (m_hat / (jnp.sqrt(v_hat) + ADAM_EPS) + ADAM_WD * w_ref[...])
                mo_ref[...] = m2
                vo_ref[...] = v2

    g_specs = [pl.BlockSpec((n, tr, C), lambda l, i, k=k: (0, jnp.where(l == k, i, 0), 0)) for k in range(L)]
    blk = pl.BlockSpec((None, tr, C), lambda l, i: (l, i, 0))
    return pl.pallas_call(
        body, name=name, grid=(L, A // tr), in_specs=g_specs + [blk, blk, blk] + [_dep_spec(2)] * len(deps),
        out_specs=[blk] * 4, out_shape=[_sds((L, A, C), F32)] * 4, compiler_params=_params(("parallel", "parallel")),
    )(*gsrcs, w, m, v, *deps)


MESH = pl.DeviceIdType.MESH
HBM_SPEC = pl.BlockSpec(memory_space=pltpu.HBM)
N_PEERS = N_DEV - 1


def _mesh_place():
    x, y, c = lax.axis_index("x"), lax.axis_index("y"), lax.axis_index("c")
    peers = []
    for p in range(1, N_DEV):
        px = 1 - x if p & 4 else x
        py = 1 - y if p & 2 else y
        pc = 1 - c if p & 1 else c
        peers.append(((px, py, pc), 4 * px + 2 * py + pc))
    return 4 * x + 2 * y + c, peers


SEM_SPEC = pl.BlockSpec(memory_space=pltpu.SEMAPHORE)
ANY_SPEC = pl.BlockSpec(memory_space=pl.ANY)
EFFECT = pltpu.SideEffectType.DATAFLOW_SIDE_EFFECTING


def _scatters(scatter, k):
    return scatter if isinstance(scatter, bool) else scatter[k]


def _exchange_refs(scatter, src, land, send, recv, k, p, dev, idx, me):
    return pltpu.make_async_remote_copy(src_ref=src[k].at[idx] if _scatters(scatter, k) else src[k], dst_ref=land[k].at[me],
                                        send_sem=send.at[k * N_PEERS + p], recv_sem=recv.at[k * N_PEERS + p], device_id=dev,
                                        device_id_type=MESH)


def _exchange_start(name, srcs, scatter, gate):
    n = len(srcs)
    lands = [lax.empty(s.shape if _scatters(scatter, k) else (N_DEV,) + s.shape, s.dtype) for k, s in enumerate(srcs)]

    def body(*refs):
        src, land = refs[:n], refs[n:2 * n]
        send, recv, own = refs[2 * n + 1:2 * n + 4]
        token = refs[-1]
        me, peers = _mesh_place()
        for k in range(n):
            pltpu.make_async_copy(src[k].at[me] if _scatters(scatter, k) else src[k], land[k].at[me], own.at[k]).start()
            for p, (dev, idx) in enumerate(peers):
                _exchange_refs(scatter, src, land, send, recv, k, p, dev, idx, me).start()
        token[...] = jnp.zeros_like(token)

    hbm = lambda a: pltpu.HBM(a.shape, a.dtype)
    outs = pl.pallas_call(
        body, name=name,
        out_shape=(pltpu.SemaphoreType.DMA((n * N_PEERS,)), pltpu.SemaphoreType.DMA((n * N_PEERS,)),
                   pltpu.SemaphoreType.DMA((n,)), *[hbm(s) for s in srcs], *[hbm(s) for s in lands], _sds(DEP_SHAPE, F32)),
        in_specs=[HBM_SPEC] * (2 * n) + [ANY_SPEC],
        out_specs=(SEM_SPEC, SEM_SPEC, SEM_SPEC, *[HBM_SPEC] * (2 * n), pl.BlockSpec(memory_space=pltpu.VMEM)),
        input_output_aliases={j: 3 + j for j in range(2 * n)},
        compiler_params=pltpu.CompilerParams(has_side_effects=EFFECT),
    )(*[pltpu.with_memory_space_constraint(s, pltpu.HBM) for s in srcs],
      *[pltpu.with_memory_space_constraint(s, pltpu.HBM) for s in lands], gate)
    return outs[:3], None, list(outs[3:3 + n]), list(outs[3 + n:3 + 2 * n]), outs[-1]


def _exchange_wait(name, started, scatter, after):
    (send, recv, own), _, srcs, lands, _ = started
    n = len(srcs)

    def body(*refs):
        src, land = refs[:n], refs[n:2 * n]
        send, recv, own = refs[2 * n:2 * n + 3]
        me, peers = _mesh_place()
        for k in range(n):
            pltpu.make_async_copy(src[k].at[me] if _scatters(scatter, k) else src[k], land[k].at[me], own.at[k]).wait()
            for p, (dev, idx) in enumerate(peers):
                cp = pltpu.make_async_remote_copy(src_ref=src[k].at[idx] if _scatters(scatter, k) else src[k], dst_ref=land[k].at[idx],
                                                  send_sem=send.at[k * N_PEERS + p], recv_sem=recv.at[k * N_PEERS + p], device_id=dev,
                                                  device_id_type=MESH)
                cp.wait_send()
                cp.wait_recv()

    hbm = lambda a: pltpu.HBM(a.shape, a.dtype)
    outs = pl.pallas_call(
        body, name=name, out_shape=(*[hbm(s) for s in srcs], *[hbm(s) for s in lands]),
        in_specs=[HBM_SPEC] * (2 * n) + [SEM_SPEC, SEM_SPEC, SEM_SPEC, ANY_SPEC], out_specs=tuple([HBM_SPEC] * (2 * n)),
        input_output_aliases={j: j for j in range(2 * n)},
        compiler_params=pltpu.CompilerParams(has_side_effects=EFFECT),
    )(*srcs, *lands, send, recv, own, after)
    return list(outs[n:])


def _sum_devices(name, parts):
    def body(p_ref, o_ref):
        tot = p_ref[0]
        for j in range(1, N_DEV):
            tot = tot + p_ref[j]
        o_ref[...] = tot

    return pl.pallas_call(body, name=name, out_shape=_sds(parts.shape[1:], F32),
                          compiler_params=pltpu.CompilerParams(vmem_limit_bytes=VMEM_LIMIT))(parts)


def _rows(a, width=D_MODEL):
    flat = a.reshape(-1)
    return jnp.pad(flat, (0, (-flat.shape[0]) % width)).reshape(-1, width)


def _pack_rows(parts):
    blocks = []
    for p in parts:
        r = _rows(p)
        blocks.append(jnp.pad(r, ((0, (-r.shape[0]) % 8), (0, 0))))
    return jnp.concatenate(blocks, axis=0)


def _unpack_rows(rows, shapes):
    out, at = [], 0
    for s in shapes:
        size = int(np.prod(s))
        n = -(-size // D_MODEL)
        out.append(rows[at:at + n].reshape(-1)[:size].reshape(s))
        at += -(-n // 8) * 8
    return out


def kernel(x, norm_mix, norm_ffn, norm_final, attn_w_in, attn_w_out, attn_sinks, hgrn_w_in, hgrn_w_out, hgrn_norm, hgrn_lb_logits, ffn_w_up, ffn_conv_w, ffn_conv_b, ffn_w_down, loss_target, m_norm_mix, m_norm_ffn, m_norm_final, m_attn_w_in, m_attn_w_out, m_attn_sinks, m_hgrn_w_in, m_hgrn_w_out, m_hgrn_norm, m_hgrn_lb_logits, m_ffn_w_up, m_ffn_conv_w, m_ffn_conv_b, m_ffn_w_down, v_norm_mix, v_norm_ffn, v_norm_final, v_attn_w_in, v_attn_w_out, v_attn_sinks, v_hgrn_w_in, v_hgrn_w_out, v_hgrn_norm, v_hgrn_lb_logits, v_ffn_w_up, v_ffn_conv_w, v_ffn_conv_b, v_ffn_w_down):
    S = x.shape[1]
    n_attn, n_hgrn = attn_w_in.shape[0], hgrn_w_in.shape[0]

    wa_in_t, wa_out_b = attn_w_in.transpose(0, 2, 1).astype(BF16), attn_w_out.astype(BF16)
    wh_in_b, wh_out_b = hgrn_w_in.astype(BF16), hgrn_w_out.astype(BF16)
    wf_up_b, wf_down_b = ffn_w_up.transpose(0, 2, 1).astype(BF16), ffn_w_down.astype(BF16)
    conv_b = ffn_conv_b.reshape(DEPTH, 2, 4, 1, FF_SLOT)
    lb = _lb_fwd("lb_fwd", hgrn_lb_logits)

    def unit_shards(l, part):
        if part == "ffn":
            return [wf_up_b[l], wf_down_b[l], ffn_conv_w[l]]
        return [wa_in_t[l // 2], wa_out_b[l // 2]] if l % 2 == 0 else [wh_in_b[l // 2], wh_out_b[l // 2]]

    def unit_weights(l, part, w):
        if part == "ffn":
            return w[0][None], w[1].reshape(1, 4, FF_SLOT, D_MODEL), w[2].reshape(2, 4, 3, FF_SLOT)
        if l % 2 == 0:
            return w[0].reshape(1, ATTN_IN, D_MODEL), w[1].reshape(1, D_MODEL, D_MODEL)
        return w[0][None], w[1].reshape(1, D_MODEL, D_MODEL)

    units = [(l, part) for l in range(DEPTH) for part in ("mix", "ffn")]
    gathers = [_exchange_start("gather_start0", unit_shards(*units[0]), False, norm_final)]
    gathers.append(_exchange_start("gather_start1", unit_shards(*units[1]), False, gathers[0][4]))
    arrived = _exchange_wait("gather_wait0", gathers[0], False, gathers[1][4])
    weights, saved = {}, [dict() for _ in range(DEPTH)]
    h = x[0]
    hn = _rmsnorm_fwd("norm_mix_fwd0", h, norm_mix[0:1])
    for n, (l, part) in enumerate(units):
        i, sv = l // 2, saved[l]
        weights[l, part] = w = unit_weights(l, part, arrived)
        dep = None
        if n + 2 < len(units):
            gathers.append(_exchange_start(f"gather_start{n + 2}", unit_shards(*units[n + 2]), False, arrived[0]))
            dep = gathers[n + 2][4]
        if part == "mix":
            sv["h"], sv["hn"] = h, hn
            if l % 2 == 0:
                sv["proj"] = _proj_rows(f"attn_proj{i}", hn, w[0], 0, BF16, dep)
                sv["o"], *sv["kept"] = _attn_fwd(f"attn_fwd{i}", sv["proj"], attn_sinks[i:i + 1])
                h, hn = _out_proj(f"attn_out{i}", sv["o"], w[1], 0, h, norm_ffn[l:l + 1])
            else:
                sv["z"] = _proj_slots(f"hgrn_proj{i}", hn, w[0], 0, dep=dep).reshape(4, 2, S, HG_SLOT)
                sv["o"], *sv["kept"] = _hg_fwd(f"hgrn_fwd{i}", sv["z"], lb[i:i + 1], hgrn_norm[i:i + 1])
                h, hn = _out_proj(f"hgrn_out{i}", sv["o"], w[1], 0, h, norm_ffn[l:l + 1])
        else:
            sv["h2"], sv["hn2"] = h, hn
            sv["u"] = _proj_slots(f"ffn_up{l}", hn, w[0], 0, True, dep, BF16).reshape(2, 4, S, FF_SLOT)
            sv["a"], sv["convs"] = _convgate_fwd(f"ffn_gate{l}", sv["u"], w[2], conv_b[l])
            if l + 1 < DEPTH:
                h, hn = _down_proj(f"ffn_down{l}", sv["a"], w[1], 0, h, norm_mix[l + 1:l + 2])
            else:
                h = _down_proj(f"ffn_down{l}", sv["a"], w[1], 0, h)
        if n + 1 < len(units):
            arrived = _exchange_wait(f"gather_wait{n + 1}", gathers[n + 1], False, h)
    dh, d_norm_final, loss_rows, dhb = _loss_head("loss_head", h, norm_final[None], loss_target[0])

    d_conv_w, d_conv_b, d_norm_mix, d_norm_ffn = [None] * DEPTH, [None] * DEPTH, [None] * DEPTH, [None] * DEPTH
    d_sinks, d_lb, d_hgrn_norm = [None] * n_attn, [None] * n_hgrn, [None] * n_hgrn
    received, pending = {}, []
    for l, part in reversed(units):
        i, sv, w = l // 2, saved[l], weights[l, part]
        dep = pending[-1][1][4] if pending else None
        if part == "ffn":
            da = _dgrad_down(f"ffn_down_dgrad{l}", dhb, w[1], 0, dep)
            g_down = _wgrad_down(f"ffn_down_wgrad{l}", sv["a"], dhb).reshape(N_DEV, D_FF // N_DEV, D_MODEL)
            du, d_conv_w[l], d_conv_b[l] = _convgate_bwd(f"ffn_gate_bwd{l}", sv["u"], sv["convs"], w[2], da)
            du = du.reshape(N_DEV, S, FF_SLOT)
            grads = [_wgrad_slots(f"ffn_up_wgrad{l}", sv["hn2"], du, True), g_down]
            dh, d_norm_ffn[l], dhb = _dgrad_slots(f"ffn_up_dgrad{l}", du, w[0], 0, (sv["h2"], norm_ffn[l:l + 1], dh), True)
        else:
            if l % 2 == 0:
                do = _dgrad_out(f"attn_out_dgrad{i}", dhb, w[1], 0, BF16, dep)
                g_out = _wgrad_rows(f"attn_out_wgrad{i}", sv["o"], dhb)
                dproj, d_sinks[i] = _attn_bwd(f"attn_bwd{i}", sv["proj"], *sv["kept"], do)
                g_in = _wgrad_rows(f"attn_proj_wgrad{i}", dproj, sv["hn"]).reshape(N_DEV, ATTN_IN // N_DEV, D_MODEL)
                dh_new = _dgrad_rows(f"attn_proj_dgrad{i}", dproj, w[0], 0, (sv["h"], norm_mix[l:l + 1], dh))
            else:
                dog = _dgrad_out(f"hgrn_out_dgrad{i}", dhb, w[1], 0, F32, dep)
                g_out = _wgrad_rows(f"hgrn_out_wgrad{i}", sv["o"], dhb)
                dz, d_lb[i], dng = _hg_bwd(f"hgrn_bwd{i}", sv["z"], lb[i:i + 1], hgrn_norm[i:i + 1], *sv["kept"], dog)
                d_hgrn_norm[i] = dng[0] + dng[1]
                dz = dz.reshape(N_DEV, S, HG_SLOT)
                g_in = _wgrad_slots(f"hgrn_proj_wgrad{i}", sv["hn"], dz)
                dh_new = _dgrad_slots(f"hgrn_proj_dgrad{i}", dz, w[0], 0, (sv["h"], norm_mix[l:l + 1], dh))
            grads = [g_in, g_out.reshape(N_DEV, D_MODEL // N_DEV, D_MODEL)]
            dh, d_norm_mix[l], dhb = dh_new
        gate = dh
        if len(pending) == 2:
            key, oldest = pending.pop(0)
            received[key] = _exchange_wait(f"scatter_wait_{key[1]}{key[0]}", oldest, True, dh)
            gate = received[key][0]
        pending.append(((l, part), _exchange_start(f"scatter_start_{part}{l}", grads, True, gate)))
    grad_x = dh[None]

    small_shapes = [(DEPTH, D_MODEL), (DEPTH, D_MODEL), (1, D_MODEL), (1, D_MODEL), (n_hgrn, D_MODEL), (n_attn, 128),
                    (n_hgrn, HG_K), (DEPTH, 2 * D_FF)]
    partial = _pack_rows([
        jnp.concatenate(d_norm_mix), jnp.concatenate(d_norm_ffn), d_norm_final, loss_rows, jnp.concatenate(d_lb),
        jnp.concatenate(d_sinks), jnp.concatenate(d_hgrn_norm), jnp.stack(d_conv_b)])
    d_taps = jnp.stack(d_conv_w).reshape(DEPTH, N_DEV, 3, FF_SLOT).transpose(1, 0, 2, 3).reshape(N_DEV, DEPTH * 3, FF_SLOT)
    small_started = _exchange_start("small_start", [partial, d_taps], (False, True), pending[-1][1][4])
    attn_layers, hgrn_layers = range(0, DEPTH, 2), range(1, DEPTH, 2)

    def transposed(ts):
        return [t.transpose(0, 2, 1) for t in ts]

    big = {"hgrn_w_in": _adamw("adamw_hgrn_in", [received[l, "mix"][0] for l in hgrn_layers], hgrn_w_in, m_hgrn_w_in,
                               v_hgrn_w_in, dep=small_started[4])}
    big["hgrn_w_out"] = _adamw("adamw_hgrn_out", [received[l, "mix"][1] for l in hgrn_layers], hgrn_w_out, m_hgrn_w_out, v_hgrn_w_out)
    key, oldest = pending.pop(0)
    received[key] = _exchange_wait(f"scatter_wait_{key[1]}{key[0]}", oldest, True, big["hgrn_w_in"][3])
    up_t = _adamw("adamw_ffn_up", [received[l, "ffn"][0] for l in range(DEPTH)], *transposed((ffn_w_up, m_ffn_w_up, v_ffn_w_up)))
    big["ffn_w_up"] = transposed(up_t)
    big["ffn_w_down"] = _adamw("adamw_ffn_down", [received[l, "ffn"][1] for l in range(DEPTH)], ffn_w_down, m_ffn_w_down, v_ffn_w_down)
    key, oldest = pending.pop(0)
    received[key] = _exchange_wait(f"scatter_wait_{key[1]}{key[0]}", oldest, True, up_t[3])
    small_parts, taps_parts = _exchange_wait("small_wait", small_started, (False, True), up_t[3])
    total = _sum_devices("sum_small", small_parts)
    (g_norm_mix, g_norm_ffn, g_norm_final, loss_sum, g_lb, g_sinks, g_hgrn_norm, g_conv_b) = _unpack_rows(total, small_shapes)

    loss = jnp.sum(loss_sum)
    g_norm_final = g_norm_final[0]
    g_sinks = g_sinks[:, :N_Q_HEADS]
    g_lb_logits = _lb_bwd("lb_bwd", hgrn_lb_logits, g_lb)

    big.update({
        "attn_w_in": transposed(_adamw("adamw_attn_in", [received[l, "mix"][0] for l in attn_layers],
                                       *transposed((attn_w_in, m_attn_w_in, v_attn_w_in)))),
        "attn_w_out": _adamw("adamw_attn_out", [received[l, "mix"][1] for l in attn_layers], attn_w_out, m_attn_w_out, v_attn_w_out),
        "ffn_conv_w": _adamw("adamw_conv_w", [taps_parts[:, 3 * l:3 * l + 3] for l in range(DEPTH)], ffn_conv_w, m_ffn_conv_w,
                             v_ffn_conv_w),
    })
    small_w = [norm_mix, norm_ffn, norm_final, attn_sinks, hgrn_norm, hgrn_lb_logits, ffn_conv_b]
    small_m = [m_norm_mix, m_norm_ffn, m_norm_final, m_attn_sinks, m_hgrn_norm, m_hgrn_lb_logits, m_ffn_conv_b]
    small_v = [v_norm_mix, v_norm_ffn, v_norm_final, v_attn_sinks, v_hgrn_norm, v_hgrn_lb_logits, v_ffn_conv_b]
    small_g = [g_norm_mix, g_norm_ffn, g_norm_final, g_sinks, g_hgrn_norm, g_lb_logits, g_conv_b]
    outs = _adamw("adamw_small", [_pack_rows(small_g)[None]], *[_pack_rows(t)[None] for t in (small_w, small_m, small_v)])
    outs = [o[0] for o in outs]
    shapes = [w.shape for w in small_w]
    small = {n: [t[j] for t in [_unpack_rows(o, shapes) for o in outs]]
             for j, n in enumerate(["norm_mix", "norm_ffn", "norm_final", "attn_sinks", "hgrn_norm", "hgrn_lb_logits", "ffn_conv_b"])}
    order = ["norm_mix", "norm_ffn", "norm_final", "attn_w_in", "attn_w_out", "attn_sinks", "hgrn_w_in", "hgrn_w_out",
             "hgrn_norm", "hgrn_lb_logits", "ffn_w_up", "ffn_conv_w", "ffn_conv_b", "ffn_w_down"]
    res = {**big, **small}
    return (loss, grad_x, *[res[n][0] for n in order], *[res[n][1] for n in order], *[res[n][2] for n in order],
            *[res[n][3] for n in order])
```

```python
import numpy as np
import jax
import jax.numpy as jnp
from jax import lax
from jax.experimental import pallas as pl
from jax.experimental.pallas import tpu as pltpu

F32 = jnp.float32
BF16 = jnp.bfloat16

D_MODEL = 1024
DEPTH = 4
HEAD_DIM = 64
N_Q_HEADS = 16
N_KV_HEADS = 4
Q_PER_KV = 4
ATTN_BLOCK = 128
ATTN_IN = 1536
HG_HEADS = 8
HG_K = 128
HG_CHUNK = 64
HG_IN = 4096
D_FF = 2816
EPS = 1e-6
N_DEV = 8
FF_SLOT = 2 * D_FF // N_DEV
HG_SLOT = HG_IN // N_DEV
HG_LEVELS = 6

ADAM_LR = 0.001
ADAM_B1 = 0.9
ADAM_B2 = 0.999
ADAM_EPS = 1e-08
ADAM_WD = 0.01
ADAM_STEP = 10

VMEM_LIMIT = 56 * 1024 * 1024
ROW_TILE = 1024
WIDE_ROW_TILE = 2048
SLOTS_PER_STEP = 2
NEG_BIG = -1e30

NN = (((1,), (0,)), ((), ()))
NT = (((1,), (1,)), ((), ()))
TN = (((0,), (0,)), ((), ()))


def _bdot(a, b, dn):
    return lax.dot_general(a.astype(BF16), b.astype(BF16), dn, preferred_element_type=F32)


def _sds(shape, dtype):
    return jax.ShapeDtypeStruct(tuple(shape), dtype)


def _params(sem):
    return pltpu.CompilerParams(dimension_semantics=sem, vmem_limit_bytes=VMEM_LIMIT)


DEP_SHAPE = (8, 128)


def _dep_spec(rank):
    return pl.BlockSpec(DEP_SHAPE, lambda *_: (0, 0))


def _matmul(name, a, b, *, dn, grid, a_spec, b_spec, o_spec, out_shape, acc_shape=None, extra=(), extra_specs=(),
            finish=None, dep=None, sem=("parallel", "parallel", "arbitrary")):
    nk = grid[2]
    many = isinstance(out_shape, (list, tuple))
    n_in = 2 + len(extra) + (dep is not None)
    n_out = len(out_shape) if many else 1

    def body(*refs):
        a_ref, b_ref = refs[0], refs[1]
        outs = refs[n_in:n_in + n_out]

        def prod():
            if len(a_ref.shape) == 3:
                return sum(_bdot(a_ref[s], b_ref[s], dn) for s in range(a_ref.shape[0]))
            return _bdot(a_ref[...], b_ref[...], dn)

        def done(v):
            if finish is None:
                outs[0][...] = v.astype(outs[0].dtype)
            else:
                finish(v, refs[2:2 + len(extra)], outs)

        if nk == 1:
            done(prod())
        else:
            acc = refs[-1]
            k = pl.program_id(2)

            @pl.when(k == 0)
            def _():
                acc[...] = prod()

            @pl.when(k > 0)
            def _():
                acc[...] += prod()

            @pl.when(k == nk - 1)
            def _():
                done(acc[...])

    in_specs = [a_spec, b_spec, *extra_specs] + ([_dep_spec(3)] if dep is not None else [])
    args = (a, b, *extra) + ((dep,) if dep is not None else ())
    scratch = [] if nk == 1 else [pltpu.VMEM(acc_shape, F32)]
    return pl.pallas_call(
        body, name=name, grid=grid, in_specs=in_specs, out_specs=o_spec, out_shape=out_shape,
        scratch_shapes=scratch, compiler_params=_params(sem),
    )(*args)


def _rms(x):
    return lax.rsqrt(jnp.mean(x * x, axis=-1, keepdims=True) + EPS)


def _residual_finish(v, ex, outs):
    h = v + ex[0][...]
    outs[0][...] = h
    if len(ex) > 1:
        outs[1][...] = (h * _rms(h) * ex[1][...]).astype(outs[1].dtype)


def _norm_bwd_finish(v, ex, outs):
    x = ex[0][...]
    r = _rms(x)
    xh = x * r
    dyg = v * ex[1][...]
    dh = ex[2][...] + r * (dyg - xh * jnp.mean(dyg * xh, axis=-1, keepdims=True))
    outs[0][...] = dh
    outs[2][...] = dh.astype(outs[2].dtype)
    part = jnp.sum(v * xh, axis=0, keepdims=True)

    @pl.when(pl.program_id(0) == 0)
    def _():
        outs[1][...] = part

    @pl.when(pl.program_id(0) > 0)
    def _():
        outs[1][...] += part


def _row_io(tm, norm_g):
    row = pl.BlockSpec((tm, D_MODEL), lambda i, j, k: (i, 0))
    vec = pl.BlockSpec((1, D_MODEL), lambda i, j, k: (0, 0))
    if norm_g is None:
        return (row,), row, lambda S: _sds((S, D_MODEL), F32)
    return (row, vec), [row, row], lambda S: [_sds((S, D_MODEL), F32), _sds((S, D_MODEL), BF16)]


def _tile(n, t):
    return min(n, t)


def _proj_rows(name, hn, wt, l, out_dtype, dep=None):
    S, N = hn.shape[0], wt.shape[1]
    tm, tn = _tile(S, WIDE_ROW_TILE), 512
    return _matmul(
        name, hn, wt, dn=NT, grid=(S // tm, N // tn, 1),
        a_spec=pl.BlockSpec((tm, D_MODEL), lambda i, j, k: (i, 0)),
        b_spec=pl.BlockSpec((None, tn, D_MODEL), lambda i, j, k: (l, j, 0)),
        o_spec=pl.BlockSpec((tm, tn), lambda i, j, k: (i, j)),
        out_shape=_sds((S, N), out_dtype), dep=dep)


def _slot_weight(w, transposed):
    if transposed:
        return w.shape[2], (None, None, w.shape[2], D_MODEL), NT, NN
    return w.shape[3], (None, None, D_MODEL, w.shape[3]), NN, NT


def _proj_slots(name, hn, w, l, transposed=False, dep=None, out_dtype=F32):
    S = hn.shape[0]
    r, blk, dn, _ = _slot_weight(w, transposed)
    tm = _tile(S, WIDE_ROW_TILE)
    return _matmul(
        name, hn, w, dn=dn, grid=(N_DEV, S // tm, 1),
        a_spec=pl.BlockSpec((tm, D_MODEL), lambda j, i, k: (i, 0)),
        b_spec=pl.BlockSpec(blk, lambda j, i, k: (l, j, 0, 0)),
        o_spec=pl.BlockSpec((None, tm, r), lambda j, i, k: (j, i, 0)),
        out_shape=_sds((N_DEV, S, r), out_dtype), dep=dep)


def _out_proj(name, o, w, l, h, norm_g=None):
    S, K = o.shape
    tm = _tile(S, ROW_TILE)
    extra_specs, o_spec, out_shape = _row_io(tm, norm_g)
    return _matmul(
        name, o, w, dn=NN, grid=(S // tm, 1, 1),
        a_spec=pl.BlockSpec((tm, K), lambda i, j, k: (i, 0)),
        b_spec=pl.BlockSpec((None, K, D_MODEL), lambda i, j, k: (l, 0, 0)),
        o_spec=o_spec, out_shape=out_shape(S), extra=(h,) if norm_g is None else (h, norm_g),
        extra_specs=extra_specs, finish=_residual_finish)


def _down_proj(name, a, w, l, h, norm_g=None):
    nj, S, r = a.shape
    tm = _tile(S, ROW_TILE)
    extra_specs, o_spec, out_shape = _row_io(tm, norm_g)
    return _matmul(
        name, a, w, dn=NN, grid=(S // tm, 1, nj // SLOTS_PER_STEP),
        a_spec=pl.BlockSpec((SLOTS_PER_STEP, tm, r), lambda i, j, k: (k, i, 0)),
        b_spec=pl.BlockSpec((None, SLOTS_PER_STEP, r, D_MODEL), lambda i, j, k: (l, k, 0, 0)),
        o_spec=o_spec, out_shape=out_shape(S), acc_shape=(tm, D_MODEL),
        extra=(h,) if norm_g is None else (h, norm_g), extra_specs=extra_specs, finish=_residual_finish)


def _dgrad_down(name, dh, w, l, dep=None):
    S = dh.shape[0]
    nj, r = w.shape[1], w.shape[2]
    tm = _tile(S, WIDE_ROW_TILE)
    return _matmul(
        name, dh, w, dn=NT, grid=(nj, S // tm, 1),
        a_spec=pl.BlockSpec((tm, D_MODEL), lambda j, i, k: (i, 0)),
        b_spec=pl.BlockSpec((None, None, r, D_MODEL), lambda j, i, k: (l, j, 0, 0)),
        o_spec=pl.BlockSpec((None, tm, r), lambda j, i, k: (j, i, 0)),
        out_shape=_sds((nj, S, r), BF16), dep=dep)


def _wgrad_down(name, a, dh):
    nj, S, r = a.shape
    tk = S
    return _matmul(
        name, a, dh, dn=TN, grid=(nj, 1, S // tk),
        a_spec=pl.BlockSpec((None, tk, r), lambda s, j, k: (s, k, 0)),
        b_spec=pl.BlockSpec((tk, D_MODEL), lambda s, j, k: (k, 0)),
        o_spec=pl.BlockSpec((None, r, D_MODEL), lambda s, j, k: (s, 0, 0)),
        out_shape=_sds((nj, r, D_MODEL), BF16), acc_shape=(r, D_MODEL))


def _norm_bwd_io(tm, S):
    row = pl.BlockSpec((tm, D_MODEL), lambda i, j, k: (i, 0))
    vec = pl.BlockSpec((1, D_MODEL), lambda i, j, k: (0, 0))
    return dict(extra_specs=(row, vec, row), o_spec=[row, vec, row],
                out_shape=[_sds((S, D_MODEL), F32), _sds((1, D_MODEL), F32), _sds((S, D_MODEL), BF16)],
                finish=_norm_bwd_finish, sem=("arbitrary", "arbitrary", "arbitrary"))


def _dgrad_slots(name, dz, w, l, norm, transposed=False):
    nj, S, r = dz.shape
    _, blk, _, dn = _slot_weight(w, transposed)
    tm = _tile(S, ROW_TILE)
    return _matmul(
        name, dz, w, dn=dn, grid=(S // tm, 1, nj // SLOTS_PER_STEP),
        a_spec=pl.BlockSpec((SLOTS_PER_STEP, tm, r), lambda i, j, k: (k, i, 0)),
        b_spec=pl.BlockSpec((None, SLOTS_PER_STEP) + blk[2:], lambda i, j, k: (l, k, 0, 0)),
        acc_shape=(tm, D_MODEL), extra=norm, **_norm_bwd_io(tm, S))


def _wgrad_slots(name, hn, dz, transposed=False):
    nj, S, r = dz.shape
    tk = S
    hn_spec = pl.BlockSpec((tk, D_MODEL), lambda s, j, k: (k, 0))
    dz_spec = pl.BlockSpec((None, tk, r), lambda s, j, k: (s, k, 0))
    if transposed:
        return _matmul(
            name, dz, hn, dn=TN, grid=(nj, 1, S // tk), a_spec=dz_spec, b_spec=hn_spec,
            o_spec=pl.BlockSpec((None, r, D_MODEL), lambda s, j, k: (s, 0, 0)),
            out_shape=_sds((nj, r, D_MODEL), BF16), acc_shape=(r, D_MODEL))
    return _matmul(
        name, hn, dz, dn=TN, grid=(nj, 1, S // tk), a_spec=hn_spec, b_spec=dz_spec,
        o_spec=pl.BlockSpec((None, D_MODEL, r), lambda s, j, k: (s, 0, 0)),
        out_shape=_sds((nj, D_MODEL, r), BF16), acc_shape=(D_MODEL, r))


def _dgrad_out(name, dh, w, l, out_dtype, dep=None):
    S, K = dh.shape[0], w.shape[1]
    tm = _tile(S, WIDE_ROW_TILE)
    return _matmul(
        name, dh, w, dn=NT, grid=(S // tm, 1, 1),
        a_spec=pl.BlockSpec((tm, D_MODEL), lambda i, j, k: (i, 0)),
        b_spec=pl.BlockSpec((None, K, D_MODEL), lambda i, j, k: (l, 0, 0)),
        o_spec=pl.BlockSpec((tm, K), lambda i, j, k: (i, 0)),
        out_shape=_sds((S, K), out_dtype), dep=dep)


def _wgrad_rows(name, a, b):
    S, K = a.shape
    tk = S
    return _matmul(
        name, a, b, dn=TN, grid=(1, 1, S // tk),
        a_spec=pl.BlockSpec((tk, K), lambda i, j, k: (k, 0)),
        b_spec=pl.BlockSpec((tk, D_MODEL), lambda i, j, k: (k, 0)),
        o_spec=pl.BlockSpec((K, D_MODEL), lambda i, j, k: (0, 0)),
        out_shape=_sds((K, D_MODEL), BF16), acc_shape=(K, D_MODEL))


def _dgrad_rows(name, dz, wt, l, norm):
    S, N = dz.shape
    tm = _tile(S, ROW_TILE)
    return _matmul(
        name, dz, wt, dn=NN, grid=(S // tm, 1, 1),
        a_spec=pl.BlockSpec((tm, N), lambda i, j, k: (i, 0)),
        b_spec=pl.BlockSpec((None, N, D_MODEL), lambda i, j, k: (l, 0, 0)),
        extra=norm, **_norm_bwd_io(tm, S))


def _rmsnorm_fwd(name, h, g):
    S = h.shape[0]
    tm = _tile(S, ROW_TILE)

    def body(h_ref, g_ref, o_ref):
        x = h_ref[...]
        o_ref[...] = (x * _rms(x) * g_ref[...]).astype(o_ref.dtype)

    row = pl.BlockSpec((tm, D_MODEL), lambda i: (i, 0))
    return pl.pallas_call(
        body, name=name, grid=(S // tm,), in_specs=[row, pl.BlockSpec((1, D_MODEL), lambda i: (0, 0))],
        out_specs=row, out_shape=_sds((S, D_MODEL), BF16), compiler_params=_params(("parallel",)),
    )(h, g)


def _loss_head(name, h, g, target):
    S = h.shape[0]
    tm = _tile(S, ROW_TILE)

    def body(h_ref, g_ref, t_ref, dh_ref, dg_ref, ls_ref, dhb_ref):
        x = h_ref[...]
        r = _rms(x)
        xh = x * r
        diff = xh * g_ref[...] - t_ref[...]
        dyf = diff * (1.0 / D_MODEL)
        dyg = dyf * g_ref[...]
        dh = r * (dyg - xh * jnp.mean(dyg * xh, axis=-1, keepdims=True))
        dh_ref[...] = dh
        dhb_ref[...] = dh.astype(dhb_ref.dtype)
        part = jnp.sum(dyf * xh, axis=0, keepdims=True)
        lpart = jnp.sum(diff * diff, axis=0, keepdims=True) * (0.5 / D_MODEL)

        @pl.when(pl.program_id(0) == 0)
        def _():
            dg_ref[...] = part
            ls_ref[...] = lpart

        @pl.when(pl.program_id(0) > 0)
        def _():
            dg_ref[...] += part
            ls_ref[...] += lpart

    row = pl.BlockSpec((tm, D_MODEL), lambda i: (i, 0))
    vec = pl.BlockSpec((1, D_MODEL), lambda i: (0, 0))
    return pl.pallas_call(
        body, name=name, grid=(S // tm,), in_specs=[row, vec, row], out_specs=[row, vec, vec, row],
        out_shape=[_sds((S, D_MODEL), F32), _sds((1, D_MODEL), F32), _sds((1, D_MODEL), F32), _sds((S, D_MODEL), BF16)],
        compiler_params=_params(("arbitrary",)),
    )(h, g, target)


ATTN_SCALE = HEAD_DIM ** -0.5
ALIBI_SLOPES = [2.0 ** (-8.0 * (h + 1) / N_Q_HEADS) for h in range(N_Q_HEADS)]
K_COL = N_Q_HEADS * HEAD_DIM
KV_COLS = N_KV_HEADS * HEAD_DIM
V_COL = K_COL + KV_COLS


def _attn_masks(n):
    qi = lax.broadcasted_iota(jnp.int32, (ATTN_BLOCK, ATTN_BLOCK), 0)
    ki = lax.broadcasted_iota(jnp.int32, (ATTN_BLOCK, ATTN_BLOCK), 1)
    dist_c = (qi - ki).astype(F32)
    return dist_c + float(ATTN_BLOCK), dist_c, (ki > qi) & (n > 0), qi >= ki


def _attn_probs(raw_p, raw_c, sink, slope, masks):
    dist_p, dist_c, valid_p, valid_c = masks
    sp = jnp.where(valid_p, raw_p * ATTN_SCALE - slope * dist_p, NEG_BIG)
    sc = jnp.where(valid_c, raw_c * ATTN_SCALE - slope * dist_c, NEG_BIG)
    m = jnp.maximum(jnp.maximum(jnp.max(sp, axis=-1, keepdims=True), jnp.max(sc, axis=-1, keepdims=True)), sink)
    ep, ec, es = jnp.exp(sp - m), jnp.exp(sc - m), jnp.exp(sink - m)
    inv = 1.0 / (jnp.sum(ep, axis=-1, keepdims=True) + jnp.sum(ec, axis=-1, keepdims=True) + es)
    return ep * inv, ec * inv, es * inv


def _group_rows(ref, m):
    return jnp.concatenate([ref[:, HEAD_DIM * (Q_PER_KV * m + g):HEAD_DIM * (Q_PER_KV * m + g + 1)]
                            for g in range(Q_PER_KV)], axis=0)


def _head_rows(x, g):
    return x[ATTN_BLOCK * g:ATTN_BLOCK * (g + 1)]


def _attn_specs(nblk):
    last = nblk - 1
    kcol, vcol = K_COL // KV_COLS, V_COL // KV_COLS
    return [
        pl.BlockSpec((ATTN_BLOCK, K_COL), lambda n: (jnp.minimum(n, last), 0)),
        pl.BlockSpec((ATTN_BLOCK, KV_COLS), lambda n: (jnp.minimum(n, last), kcol)),
        pl.BlockSpec((ATTN_BLOCK, KV_COLS), lambda n: (jnp.maximum(jnp.minimum(n, last) - 1, 0), kcol)),
        pl.BlockSpec((ATTN_BLOCK, KV_COLS), lambda n: (jnp.minimum(n, last), vcol)),
        pl.BlockSpec((ATTN_BLOCK, KV_COLS), lambda n: (jnp.maximum(jnp.minimum(n, last) - 1, 0), vcol)),
    ]


P_COLS = 2 * ATTN_BLOCK


def _attn_fwd(name, proj, sinks):
    S = proj.shape[0]
    nblk = S // ATTN_BLOCK

    def body(q_ref, kc_ref, kp_ref, vc_ref, vp_ref, sk_ref, o_ref, p_ref, ps_ref):
        masks = _attn_masks(pl.program_id(0))
        lane = lax.broadcasted_iota(jnp.int32, (ATTN_BLOCK, 128), 1)
        sink_p = jnp.zeros((ATTN_BLOCK, 128), F32)
        for m in range(N_KV_HEADS):
            ks = slice(HEAD_DIM * m, HEAD_DIM * (m + 1))
            kp, kc, vp, vc = kp_ref[:, ks], kc_ref[:, ks], vp_ref[:, ks], vc_ref[:, ks]
            q4 = _group_rows(q_ref, m)
            raw_p, raw_c = _bdot(q4, kp, NT), _bdot(q4, kc, NT)
            pps, pcs = [], []
            for g in range(Q_PER_KV):
                hh = Q_PER_KV * m + g
                pp, pc, ps = _attn_probs(_head_rows(raw_p, g), _head_rows(raw_c, g), sk_ref[0, hh], ALIBI_SLOPES[hh], masks)
                pps.append(pp.astype(BF16))
                pcs.append(pc.astype(BF16))
                p_ref[:, P_COLS * hh:P_COLS * hh + ATTN_BLOCK] = pps[g]
                p_ref[:, P_COLS * hh + ATTN_BLOCK:P_COLS * (hh + 1)] = pcs[g]
                sink_p = jnp.where(lane == hh, ps, sink_p)
            o4 = _bdot(jnp.concatenate(pps, axis=0), vp, NN) + _bdot(jnp.concatenate(pcs, axis=0), vc, NN)
            for g in range(Q_PER_KV):
                hh = Q_PER_KV * m + g
                o_ref[:, HEAD_DIM * hh:HEAD_DIM * (hh + 1)] = _head_rows(o4, g).astype(o_ref.dtype)
        ps_ref[...] = sink_p

    row = lambda cols: pl.BlockSpec((ATTN_BLOCK, cols), lambda n: (n, 0))
    return pl.pallas_call(
        body, name=name, grid=(nblk,),
        in_specs=_attn_specs(nblk) + [pl.BlockSpec(memory_space=pltpu.SMEM)],
        out_specs=[row(K_COL), row(N_Q_HEADS * P_COLS), row(128)],
        out_shape=[_sds((S, K_COL), BF16), _sds((S, N_Q_HEADS * P_COLS), BF16), _sds((S, 128), F32)],
        compiler_params=_params(("parallel",)),
    )(proj, proj, proj, proj, proj, sinks)


def _attn_bwd(name, proj, probs, sink_probs, do):
    S = proj.shape[0]
    nblk = S // ATTN_BLOCK

    def body(q_ref, kc_ref, kp_ref, vc_ref, vp_ref, do_ref, p_ref, ps_ref, dz_ref, ds_ref, carry, cur, padd):
        n = pl.program_id(0)

        @pl.when(n == 0)
        def _():
            carry[...] = jnp.zeros_like(carry)
            ds_ref[...] = jnp.zeros_like(ds_ref)

        @pl.when(n < nblk)
        def _():
            lane = lax.broadcasted_iota(jnp.int32, (ATTN_BLOCK, 128), 1)
            sink_p = ps_ref[...]
            dsv = jnp.zeros((1, 128), F32)
            for m in range(N_KV_HEADS):
                ks = slice(HEAD_DIM * m, HEAD_DIM * (m + 1))
                kp, kc, vp, vc = kp_ref[:, ks], kc_ref[:, ks], vp_ref[:, ks], vc_ref[:, ks]
                q4, do4 = _group_rows(q_ref, m), _group_rows(do_ref, m)
                dpp4, dpc4 = _bdot(do4, vp, NT), _bdot(do4, vc, NT)
                pps, pcs, dsps, dscs = [], [], [], []
                for g in range(Q_PER_KV):
                    hh = Q_PER_KV * m + g
                    pps.append(p_ref[:, P_COLS * hh:P_COLS * hh + ATTN_BLOCK])
                    pcs.append(p_ref[:, P_COLS * hh + ATTN_BLOCK:P_COLS * (hh + 1)])
                    pp, pc = pps[g].astype(F32), pcs[g].astype(F32)
                    dpp, dpc = _head_rows(dpp4, g), _head_rows(dpc4, g)
                    delta = jnp.sum(pp * dpp, axis=-1, keepdims=True) + jnp.sum(pc * dpc, axis=-1, keepdims=True)
                    dsv = dsv - jnp.sum(jnp.where(lane == hh, sink_p, 0.0) * delta, axis=0, keepdims=True)
                    dsps.append((pp * (dpp - delta)).astype(BF16))
                    dscs.append((pc * (dpc - delta)).astype(BF16))
                pp4, pc4 = jnp.concatenate(pps, axis=0), jnp.concatenate(pcs, axis=0)
                dsp4, dsc4 = jnp.concatenate(dsps, axis=0), jnp.concatenate(dscs, axis=0)
                dq4 = (_bdot(dsp4, kp, NN) + _bdot(dsc4, kc, NN)) * ATTN_SCALE
                for g in range(Q_PER_KV):
                    hh = Q_PER_KV * m + g
                    cur[:, HEAD_DIM * hh:HEAD_DIM * (hh + 1)] = _head_rows(dq4, g)
                cur[:, K_COL + HEAD_DIM * m:K_COL + HEAD_DIM * (m + 1)] = _bdot(dsc4, q4, TN) * ATTN_SCALE
                cur[:, V_COL + HEAD_DIM * m:V_COL + HEAD_DIM * (m + 1)] = _bdot(pc4, do4, TN)
                padd[:, ks] = _bdot(dsp4, q4, TN) * ATTN_SCALE
                padd[:, KV_COLS + HEAD_DIM * m:KV_COLS + HEAD_DIM * (m + 1)] = _bdot(pp4, do4, TN)
            ds_ref[...] += dsv
            dz_ref[:, :K_COL] = carry[:, :K_COL].astype(dz_ref.dtype)
            dz_ref[:, K_COL:] = (carry[:, K_COL:] + padd[...]).astype(dz_ref.dtype)
            carry[...] = cur[...]

        @pl.when(n == nblk)
        def _():
            dz_ref[...] = carry[...].astype(dz_ref.dtype)

    return pl.pallas_call(
        body, name=name, grid=(nblk + 1,),
        in_specs=_attn_specs(nblk) + [
            pl.BlockSpec((ATTN_BLOCK, cols), lambda n: (jnp.minimum(n, nblk - 1), 0))
            for cols in (K_COL, N_Q_HEADS * P_COLS, 128)],
        out_specs=[pl.BlockSpec((ATTN_BLOCK, ATTN_IN), lambda n: (jnp.maximum(n - 1, 0), 0)),
                   pl.BlockSpec((1, 128), lambda n: (0, 0))],
        out_shape=[_sds((S, ATTN_IN), BF16), _sds((1, 128), F32)],
        scratch_shapes=[pltpu.VMEM((ATTN_BLOCK, ATTN_IN), F32), pltpu.VMEM((ATTN_BLOCK, ATTN_IN), F32),
                        pltpu.VMEM((ATTN_BLOCK, 2 * KV_COLS), F32)],
        compiler_params=_params(("arbitrary",)),
    )(proj, proj, proj, proj, proj, do, probs, sink_probs)


def _hg_consts():
    C = HG_CHUNK
    tri = np.tril(np.ones((C, C)))
    t = np.arange(C)
    rows, masks = [tri], []
    for lvl in range(HG_LEVELS):
        n = C >> (lvl + 1)
        sel = np.zeros((C, C))
        sel[t, (t // (2 * n)) * (2 * n) + n - 1] = 1.0
        rows.append(sel @ tri)
        tt, ss = t[:, None], t[None, :]
        masks.append((tt // (2 * n) == ss // (2 * n)) & ((tt // n) % 2 == 1) & ((ss // n) % 2 == 0))
    masks.append(np.eye(C, dtype=bool))
    stk = np.concatenate(rows, axis=0)
    return jnp.asarray(stk, BF16), jnp.asarray(np.stack(masks), F32)


def _sigmoid(x):
    return 1.0 / (1.0 + jnp.exp(-x))


def _silu_sigmoid(x):
    return 0.5 + 0.5 * jnp.tanh(0.5 * x)


def _split(x, parts):
    out, rest = [], x
    for _ in range(parts):
        out.append(rest.astype(BF16))
        rest = rest - out[-1].astype(F32)
    return out


def _dot01(m01, x, dn, parts=3):
    return sum(lax.dot_general(m01, p, dn, preferred_element_type=F32) for p in _split(x, parts))


def _ref_rows(b, n):
    C = b.shape[1]
    if 2 * n >= 8:
        b3 = b.reshape(HG_CHUNK // (2 * n), 2 * n, C)
        return jnp.broadcast_to(b3[:, n - 1:n, :], b3.shape).reshape(HG_CHUNK, C)
    pos = lax.broadcasted_iota(jnp.int32, b.shape, 0) % (2 * n)
    out = b
    for p in range(2 * n):
        if p != n - 1:
            out = jnp.where(pos == p, pltpu.roll(b, (p - (n - 1)) % HG_CHUNK, 0), out)
    return out


HG_STEP_CHUNKS = 8


def _chunk_rows(ci):
    return pl.ds(pl.multiple_of(ci * HG_CHUNK, HG_CHUNK), HG_CHUNK)


def _hg_common(z_ref, rows, lb_ref, stk_ref):
    qr, fr = z_ref[0, rows, :], z_ref[1, rows, :]
    lb = lb_ref[...]
    sq, sg, sgn = _silu_sigmoid(qr), _sigmoid(fr), _sigmoid(-fr)
    ft = lb + (1.0 - lb) * sg
    b = _dot01(stk_ref[0:HG_CHUNK, :], jnp.log(ft), NN)
    ws = [jnp.exp(-jnp.abs(b - _ref_rows(b, HG_CHUNK >> (l + 1)))) for l in range(HG_LEVELS)]
    blast = b[HG_CHUNK - 1:HG_CHUNK]
    return dict(qr=qr, fr=fr, lb=lb, sq=sq, sg=sg, sgn=sgn, ft=ft, q=qr * sq, kk=(1.0 - lb) * sgn, b=b,
                ws=ws, eb=jnp.exp(b), ed=jnp.exp(blast - b), elast=jnp.exp(blast))


def _hg_factors(qh, kh, ws, sl):
    return ([(qh * ws[l][:, sl]).astype(BF16) for l in range(HG_LEVELS)],
            [(kh * ws[l][:, sl]).astype(BF16) for l in range(HG_LEVELS)])


def _hg_intra(qh, kh, ws, msk_ref, sl):
    qls, kls = _hg_factors(qh, kh, ws, sl)
    a = msk_ref[HG_LEVELS] * _bdot(qh, kh, NT)
    for l in range(HG_LEVELS):
        a = a + msk_ref[l] * _bdot(qls[l], kls[l], NT)
    return a


def _hg_fwd(name, z, lb, ng):
    S = z.shape[2]
    nc = S // HG_CHUNK
    per = min(HG_STEP_CHUNKS, nc)
    stk, msk = _hg_consts()

    def body(z_ref, lb_ref, ng_ref, stk_ref, msk_ref, og_ref, st_ref, a_ref, o_ref, state):
        @pl.when(pl.program_id(1) == 0)
        def _():
            state[...] = jnp.zeros_like(state)

        def chunk(ci, _):
            rows = _chunk_rows(ci)
            cm = _hg_common(z_ref, rows, lb_ref, stk_ref)
            v, gt = z_ref[2, rows, :], z_ref[3, rows, :]
            kd = cm["kk"] * cm["ed"]
            for hh in range(4):
                sl = slice(HG_K * hh, HG_K * (hh + 1))
                st = state[hh]
                st_ref[ci, hh] = st
                qh, kh, vh = cm["q"][:, sl], cm["kk"][:, sl], v[:, sl]
                a = _hg_intra(qh, kh, cm["ws"], msk_ref, sl).astype(BF16)
                a_ref[ci, hh] = a
                o = _bdot(a, vh, NN) + _bdot(qh * cm["eb"][:, sl], st, NT)
                o_ref[rows, sl] = o
                state[hh] = cm["elast"][:, sl] * st + _bdot(vh, kd[:, sl], TN)
                gh = gt[:, sl]
                og_ref[rows, sl] = (o * _rms(o) * ng_ref[...] * (gh * _silu_sigmoid(gh))).astype(og_ref.dtype)
            return 0

        lax.fori_loop(0, per, chunk, 0, unroll=True)

    return pl.pallas_call(
        body, name=name, grid=(2, nc // per),
        in_specs=[pl.BlockSpec((4, None, per * HG_CHUNK, HG_SLOT), lambda g, c: (0, g, c, 0)),
                  pl.BlockSpec((1, HG_SLOT), lambda g, c: (0, g)),
                  pl.BlockSpec((1, HG_K), lambda g, c: (0, 0)),
                  pl.BlockSpec(stk.shape, lambda g, c: (0, 0)),
                  pl.BlockSpec(msk.shape, lambda g, c: (0, 0, 0))],
        out_specs=[pl.BlockSpec((per * HG_CHUNK, HG_SLOT), lambda g, c: (c, g)),
                   pl.BlockSpec((per, 4, HG_K, HG_K), lambda g, c: (c, g, 0, 0)),
                   pl.BlockSpec((per, 4, HG_CHUNK, HG_CHUNK), lambda g, c: (c, g, 0, 0)),
                   pl.BlockSpec((per * HG_CHUNK, HG_SLOT), lambda g, c: (c, g))],
        out_shape=[_sds((S, D_MODEL), BF16), _sds((nc, HG_HEADS, HG_K, HG_K), F32),
                   _sds((nc, HG_HEADS, HG_CHUNK, HG_CHUNK), BF16), _sds((S, D_MODEL), F32)],
        scratch_shapes=[pltpu.VMEM((4, HG_K, HG_K), F32)],
        compiler_params=_params(("parallel", "arbitrary")),
    )(z, lb, ng, stk, msk)


def _hg_bwd(name, z, lb, ng, states, intra, o_pre, dog):
    S = z.shape[2]
    nc = S // HG_CHUNK
    per = min(HG_STEP_CHUNKS, nc)
    stk, msk = _hg_consts()

    def body(z_ref, lb_ref, ng_ref, stk_ref, msk_ref, st_ref, a_ref, o_ref, dog_ref, dz_ref, dlb_ref, dng_ref, dstate):
        @pl.when(pl.program_id(1) == 0)
        def _():
            dstate[...] = jnp.zeros_like(dstate)
            dlb_ref[...] = jnp.zeros_like(dlb_ref)
            dng_ref[...] = jnp.zeros_like(dng_ref)

        def chunk(k, _):
            ci = per - 1 - k
            rows = _chunk_rows(ci)
            cm = _hg_common(z_ref, rows, lb_ref, stk_ref)
            v, gt = z_ref[2, rows, :], z_ref[3, rows, :]
            ng = ng_ref[...]
            kd = cm["kk"] * cm["ed"]
            row = lax.broadcasted_iota(jnp.int32, (HG_CHUNK, 1), 0)
            dng = jnp.zeros((1, HG_K), F32)
            dq_h, dkk_h, db_h, dv_h, dgt_h = [], [], [], [], []
            dr_h = [[] for _ in range(HG_LEVELS)]
            for hh in range(4):
                sl = slice(HG_K * hh, HG_K * (hh + 1))
                st, dst = st_ref[ci, hh], dstate[hh]
                qh, kh, vh, ebh, edh, kdh = cm["q"][:, sl], cm["kk"][:, sl], v[:, sl], cm["eb"][:, sl], cm["ed"][:, sl], kd[:, sl]
                elh = cm["elast"][:, sl]
                qls, kls = _hg_factors(qh, kh, cm["ws"], sl)
                a, o = a_ref[ci, hh], o_ref[rows, sl]
                qe = qh * ebh
                r = _rms(o)
                xh = o * r
                gh = gt[:, sl]
                sgg = _silu_sigmoid(gh)
                dog = dog_ref[rows, sl].astype(F32)
                dy = dog * (gh * sgg)
                dgt_h.append(dog * (xh * ng) * (sgg * (1.0 + gh * (1.0 - sgg))))
                dng = dng + jnp.sum(dy * xh, axis=0, keepdims=True)
                dyg = dy * ng
                do = r * (dyg - xh * jnp.mean(dyg * xh, axis=-1, keepdims=True))
                da = _bdot(do, vh, NT)
                dv_h.append(_bdot(a, do, TN) + _bdot(kdh, dst, NT))
                dkd = _bdot(vh, dst, NN)
                delast = jnp.sum(st * dst, axis=0, keepdims=True)
                dqe = _bdot(do, st, NN)
                dstate[hh] = elh * dst + _bdot(do, qe, TN)
                gk = dkd * kdh
                dblast = jnp.sum(gk, axis=0, keepdims=True) + delast * elh
                db = dqe * qe - gk + jnp.where(row == HG_CHUNK - 1, dblast, 0.0)
                dp = (msk_ref[HG_LEVELS] * da).astype(BF16)
                dq = dqe * ebh + _bdot(dp, kh, NN)
                dkk = dkd * edh + _bdot(dp, qh, TN)
                for l in range(HG_LEVELS):
                    dp = (msk_ref[l] * da).astype(BF16)
                    dql, dkl = _bdot(dp, kls[l], NN), _bdot(dp, qls[l], TN)
                    w = cm["ws"][l][:, sl]
                    dq = dq + dql * w
                    dkk = dkk + dkl * w
                    half = jnp.where(((row >> (HG_LEVELS - 1 - l)) & 1) == 1, 1.0, -1.0)
                    dd = half * w * (dql * qh + dkl * kh)
                    db = db + dd
                    dr_h[l].append(-dd)
                dq_h.append(dq)
                dkk_h.append(dkk)
                db_h.append(db)
            cat = lambda xs: jnp.concatenate(xs, axis=1)
            cot = jnp.concatenate([cat(db_h)] + [cat(dr_h[l]) for l in range(HG_LEVELS)], axis=0)
            dlf = _dot01(stk_ref[...], cot, TN, parts=2)
            dq, dkk = cat(dq_h), cat(dkk_h)
            dft = dlf / cm["ft"]
            one_lb = 1.0 - cm["lb"]
            dz_ref[0, rows, :] = (dq * (cm["sq"] * (1.0 + cm["qr"] * (1.0 - cm["sq"])))).astype(dz_ref.dtype)
            dz_ref[1, rows, :] = ((dft - dkk) * one_lb * cm["sg"] * cm["sgn"]).astype(dz_ref.dtype)
            dz_ref[2, rows, :] = cat(dv_h).astype(dz_ref.dtype)
            dz_ref[3, rows, :] = cat(dgt_h).astype(dz_ref.dtype)
            dlb_ref[...] += jnp.sum((dft - dkk) * cm["sgn"], axis=0, keepdims=True)
            dng_ref[...] += dng
            return 0

        lax.fori_loop(0, per, chunk, 0, unroll=True)

    rev = lambda c: nc // per - 1 - c
    rows_blk = pl.BlockSpec((per * HG_CHUNK, HG_SLOT), lambda g, c: (rev(c), g))
    return pl.pallas_call(
        body, name=name, grid=(2, nc // per),
        in_specs=[pl.BlockSpec((4, None, per * HG_CHUNK, HG_SLOT), lambda g, c: (0, g, rev(c), 0)),
                  pl.BlockSpec((1, HG_SLOT), lambda g, c: (0, g)),
                  pl.BlockSpec((1, HG_K), lambda g, c: (0, 0)),
                  pl.BlockSpec(stk.shape, lambda g, c: (0, 0)),
                  pl.BlockSpec(msk.shape, lambda g, c: (0, 0, 0)),
                  pl.BlockSpec((per, 4, HG_K, HG_K), lambda g, c: (rev(c), g, 0, 0)),
                  pl.BlockSpec((per, 4, HG_CHUNK, HG_CHUNK), lambda g, c: (rev(c), g, 0, 0)),
                  rows_blk, rows_blk],
        out_specs=[pl.BlockSpec((4, None, per * HG_CHUNK, HG_SLOT), lambda g, c: (0, g, rev(c), 0)),
                   pl.BlockSpec((1, HG_SLOT), lambda g, c: (0, g)),
                   pl.BlockSpec((None, 1, HG_K), lambda g, c: (g, 0, 0))],
        out_shape=[_sds(z.shape, BF16), _sds((1, 2 * HG_SLOT), F32), _sds((2, 1, HG_K), F32)],
        scratch_shapes=[pltpu.VMEM((4, HG_K, HG_K), F32)],
        compiler_params=_params(("parallel", "arbitrary")),
    )(z, lb, ng, stk, msk, states, intra, o_pre, dog)


def _lb_fwd(name, logits):
    def body(l_ref, o_ref):
        x = l_ref[...]
        e = jnp.exp(x - jnp.max(x, axis=0, keepdims=True))
        s = e / jnp.sum(e, axis=0, keepdims=True)
        o_ref[0:1, :] = s[1:2]
        o_ref[1:2, :] = s[1:2] + s[2:3] + s[3:4]

    return pl.pallas_call(body, name=name, out_shape=_sds((2, logits.shape[1]), F32))(logits)


def _lb_bwd(name, logits, dlb):
    def body(l_ref, d_ref, o_ref):
        x = l_ref[...]
        e = jnp.exp(x - jnp.max(x, axis=0, keepdims=True))
        s = e / jnp.sum(e, axis=0, keepdims=True)
        d1, d3 = d_ref[0:1, :], d_ref[1:2, :]
        ds = [jnp.zeros_like(d1), d1 + d3, d3, d3]
        dot = sum(ds[r] * s[r:r + 1] for r in range(1, DEPTH))
        for r in range(DEPTH):
            o_ref[r:r + 1, :] = s[r:r + 1] * (ds[r] - dot)

    return pl.pallas_call(body, name=name, out_shape=_sds(logits.shape, F32))(logits, dlb)


SUB = 8


def _rows_down(x, prev, k):
    row = lax.broadcasted_iota(jnp.int32, x.shape, 0)
    return jnp.where(row >= k, pltpu.roll(x, k, 0), pltpu.roll(prev, k, 0))


def _rows_up(x, nxt, k):
    row = lax.broadcasted_iota(jnp.int32, x.shape, 0)
    return jnp.where(row < SUB - k, pltpu.roll(x, SUB - k, 0), pltpu.roll(nxt, SUB - k, 0))


def _conv_block(w_ref, b_ref, p, x, prev):
    return (b_ref[p] + w_ref[p, 0:1, :] * _rows_down(x, prev, 2) + w_ref[p, 1:2, :] * _rows_down(x, prev, 1)
            + w_ref[p, 2:3, :] * x)


def _convgate_fwd(name, u, cw, cb):
    S = u.shape[2]
    tm = _tile(S, ROW_TILE)

    def body(u_ref, w_ref, b_ref, a_ref, c_ref, halo):
        @pl.when(pl.program_id(1) == 0)
        def _():
            halo[...] = jnp.zeros_like(halo)

        def step(r, prev):
            pg, pv = prev
            out, cgs, cvs = [], [], []
            rows = pl.ds(pl.multiple_of(r * 2 * SUB, 2 * SUB), 2 * SUB)
            ug16, uv16 = u_ref[0, rows, :].astype(F32), u_ref[1, rows, :].astype(F32)
            for s in range(2):
                xg, xv = ug16[s * SUB:(s + 1) * SUB], uv16[s * SUB:(s + 1) * SUB]
                cgs.append(_conv_block(w_ref, b_ref, 0, xg, pg))
                cvs.append(_conv_block(w_ref, b_ref, 1, xv, pv))
                out.append(cgs[s] * _silu_sigmoid(cgs[s]) * cvs[s])
                pg, pv = xg, xv
            a_ref[rows, :] = jnp.concatenate(out, axis=0).astype(a_ref.dtype)
            c_ref[0, rows, :] = jnp.concatenate(cgs, axis=0).astype(c_ref.dtype)
            c_ref[1, rows, :] = jnp.concatenate(cvs, axis=0).astype(c_ref.dtype)
            return pg, pv

        pg, pv = lax.fori_loop(0, tm // (2 * SUB), step, (halo[0], halo[1]), unroll=2)
        halo[0] = pg
        halo[1] = pv

    pair = pl.BlockSpec((2, None, tm, FF_SLOT), lambda j, t: (0, j, t, 0))
    return pl.pallas_call(
        body, name=name, grid=(4, S // tm),
        in_specs=[pair, pl.BlockSpec((2, None, 3, FF_SLOT), lambda j, t: (0, j, 0, 0)),
                  pl.BlockSpec((2, None, 1, FF_SLOT), lambda j, t: (0, j, 0, 0))],
        out_specs=[pl.BlockSpec((None, tm, FF_SLOT), lambda j, t: (j, t, 0)), pair],
        out_shape=[_sds((4, S, FF_SLOT), BF16), _sds(u.shape, BF16)],
        scratch_shapes=[pltpu.VMEM((2, SUB, FF_SLOT), F32)],
        compiler_params=_params(("parallel", "arbitrary")),
    )(u, cw, cb)


def _convgate_bwd(name, u, convs, cw, da):
    S = u.shape[2]
    tm = _tile(S, ROW_TILE)
    nt = S // tm

    def body(u_ref, c_ref, w_ref, da_ref, du_out, dw_ref, db_ref, after, first, acc, du_ref):
        @pl.when(pl.program_id(1) == 0)
        def _():
            after[...] = jnp.zeros_like(after)
            acc[...] = jnp.zeros_like(acc)

        def finish(p, x, d, nxt, rows):
            taps = (_rows_up(d, nxt, 2), _rows_up(d, nxt, 1), d)
            du_ref[p, rows, :] = w_ref[p, 0:1, :] * taps[0] + w_ref[p, 1:2, :] * taps[1] + w_ref[p, 2:3, :] * d
            for j in range(3):
                acc[p, j] += taps[j] * x
            acc[p, 3] += d

        def step(r, carry):
            xg_last, xv_last, dg_last, dv_last = carry
            rows16 = pl.ds(pl.multiple_of(r * 2 * SUB, 2 * SUB), 2 * SUB)
            dav = da_ref[rows16, :].astype(F32)
            cg16, cv16 = c_ref[0, rows16, :].astype(F32), c_ref[1, rows16, :].astype(F32)
            ug16, uv16 = u_ref[0, rows16, :].astype(F32), u_ref[1, rows16, :].astype(F32)
            for s in range(2):
                at = r * 2 * SUB + s * SUB
                part = slice(s * SUB, (s + 1) * SUB)
                cg, cv, dab = cg16[part], cv16[part], dav[part]
                sg = _silu_sigmoid(cg)
                dg = dab * cv * (sg * (1.0 + cg * (1.0 - sg)))
                dv = dab * cg * sg
                before = pl.ds(pl.multiple_of(at - SUB, SUB), SUB)
                if s == 0:
                    @pl.when(r == 0)
                    def _():
                        first[0] = dg
                        first[1] = dv

                    @pl.when(r > 0)
                    def _():
                        finish(0, xg_last, dg_last, dg, before)
                        finish(1, xv_last, dv_last, dv, before)
                else:
                    finish(0, xg_last, dg_last, dg, before)
                    finish(1, xv_last, dv_last, dv, before)
                xg_last, xv_last, dg_last, dv_last = ug16[part], uv16[part], dg, dv
            return xg_last, xv_last, dg_last, dv_last

        zero = jnp.zeros((SUB, FF_SLOT), F32)
        xg_last, xv_last, dg_last, dv_last = lax.fori_loop(0, tm // (2 * SUB), step, (zero, zero, zero, zero))
        finish(0, xg_last, dg_last, after[0], slice(tm - SUB, tm))
        finish(1, xv_last, dv_last, after[1], slice(tm - SUB, tm))
        du_out[...] = du_ref[...].astype(du_out.dtype)
        after[...] = first[...]
        for p in range(2):
            for j in range(3):
                dw_ref[p, j:j + 1, :] = jnp.sum(acc[p, j], axis=0, keepdims=True)
            db_ref[p] = jnp.sum(acc[p, 3], axis=0, keepdims=True)

    rev = lambda t: nt - 1 - t
    pair = pl.BlockSpec((2, None, tm, FF_SLOT), lambda j, t: (0, j, rev(t), 0))
    taps = pl.BlockSpec((2, None, 3, FF_SLOT), lambda j, t: (0, j, 0, 0))
    bias = pl.BlockSpec((2, None, 1, FF_SLOT), lambda j, t: (0, j, 0, 0))
    return pl.pallas_call(
        body, name=name, grid=(4, nt),
        in_specs=[pair, pair, taps, pl.BlockSpec((None, tm, FF_SLOT), lambda j, t: (j, rev(t), 0))],
        out_specs=[pair, taps, bias],
        out_shape=[_sds(u.shape, BF16), _sds(cw.shape, F32), _sds((2, 4, 1, FF_SLOT), F32)],
        scratch_shapes=[pltpu.VMEM((2, SUB, FF_SLOT), F32), pltpu.VMEM((2, SUB, FF_SLOT), F32),
                        pltpu.VMEM((2, 4, SUB, FF_SLOT), F32), pltpu.VMEM((2, tm, FF_SLOT), F32)],
        compiler_params=_params(("parallel", "arbitrary")),
    )(u, convs, cw, da)


def _row_tile(R):
    for t in range(256, 15, -16):
        if R % t == 0:
            return t
    return R


def _adamw(name, gsrcs, w, m, v, dep=None):
    L = len(gsrcs)
    n, A, C = gsrcs[0].shape
    tr = _row_tile(A)
    deps = () if dep is None else (dep,)

    def body(*refs):
        g_refs = refs[:L]
        w_ref, m_ref, v_ref = refs[L:L + 3]
        go_ref, d_ref, mo_ref, vo_ref = refs[L + 3 + len(deps):]
        for k in range(L):
            @pl.when(pl.program_id(0) == k)
            def _(k=k):
                g = g_refs[k][0].astype(F32)
                for s in range(1, n):
                    g = g + g_refs[k][s].astype(F32)
                m2 = ADAM_B1 * m_ref[...] + (1.0 - ADAM_B1) * g
                v2 = ADAM_B2 * v_ref[...] + (1.0 - ADAM_B2) * (g * g)
                m_hat = m2 / (1.0 - ADAM_B1 ** ADAM_STEP)
                v_hat = v2 / (1.0 - ADAM_B2 ** ADAM_STEP)
                go_ref[...] = g
                d_ref[...] = -ADAM_LR * (m_hat / (jnp.sqrt(v_hat) + ADAM_EPS) + ADAM_WD * w_ref[...])
                mo_ref[...] = m2
                vo_ref[...] = v2

    g_specs = [pl.BlockSpec((n, tr, C), lambda l, i, k=k: (0, jnp.where(l == k, i, 0), 0)) for k in range(L)]
    blk = pl.BlockSpec((None, tr, C), lambda l, i: (l, i, 0))
    return pl.pallas_call(
        body, name=name, grid=(L, A // tr), in_specs=g_specs + [blk, blk, blk] + [_dep_spec(2)] * len(deps),
        out_specs=[blk] * 4, out_shape=[_sds((L, A, C), F32)] * 4, compiler_params=_params(("parallel", "parallel")),
    )(*gsrcs, w, m, v, *deps)


MESH = pl.DeviceIdType.MESH
HBM_SPEC = pl.BlockSpec(memory_space=pltpu.HBM)
N_PEERS = N_DEV - 1


def _mesh_place():
    x, y, c = lax.axis_index("x"), lax.axis_index("y"), lax.axis_index("c")
    peers = []
    for p in range(1, N_DEV):
        px = 1 - x if p & 4 else x
        py = 1 - y if p & 2 else y
        pc = 1 - c if p & 1 else c
        peers.append(((px, py, pc), 4 * px + 2 * py + pc))
    return 4 * x + 2 * y + c, peers


SEM_SPEC = pl.BlockSpec(memory_space=pltpu.SEMAPHORE)
ANY_SPEC = pl.BlockSpec(memory_space=pl.ANY)
EFFECT = pltpu.SideEffectType.DATAFLOW_SIDE_EFFECTING


def _scatters(scatter, k):
    return scatter if isinstance(scatter, bool) else scatter[k]


def _exchange_refs(scatter, src, land, send, recv, k, p, dev, idx, me):
    return pltpu.make_async_remote_copy(src_ref=src[k].at[idx] if _scatters(scatter, k) else src[k], dst_ref=land[k].at[me],
                                        send_sem=send.at[k * N_PEERS + p], recv_sem=recv.at[k * N_PEERS + p], device_id=dev,
                                        device_id_type=MESH)


def _exchange_start(name, srcs, scatter, gate):
    n = len(srcs)
    lands = [lax.empty(s.shape if _scatters(scatter, k) else (N_DEV,) + s.shape, s.dtype) for k, s in enumerate(srcs)]

    def body(*refs):
        src, land = refs[:n], refs[n:2 * n]
        send, recv, own = refs[2 * n + 1:2 * n + 4]
        token = refs[-1]
        me, peers = _mesh_place()
        for k in range(n):
            pltpu.make_async_copy(src[k].at[me] if _scatters(scatter, k) else src[k], land[k].at[me], own.at[k]).start()
            for p, (dev, idx) in enumerate(peers):
                _exchange_refs(scatter, src, land, send, recv, k, p, dev, idx, me).start()
        token[...] = jnp.zeros_like(token)

    hbm = lambda a: pltpu.HBM(a.shape, a.dtype)
    outs = pl.pallas_call(
        body, name=name,
        out_shape=(pltpu.SemaphoreType.DMA((n * N_PEERS,)), pltpu.SemaphoreType.DMA((n * N_PEERS,)),
                   pltpu.SemaphoreType.DMA((n,)), *[hbm(s) for s in srcs], *[hbm(s) for s in lands], _sds(DEP_SHAPE, F32)),
        in_specs=[HBM_SPEC] * (2 * n) + [ANY_SPEC],
        out_specs=(SEM_SPEC, SEM_SPEC, SEM_SPEC, *[HBM_SPEC] * (2 * n), pl.BlockSpec(memory_space=pltpu.VMEM)),
        input_output_aliases={j: 3 + j for j in range(2 * n)},
        compiler_params=pltpu.CompilerParams(has_side_effects=EFFECT),
    )(*[pltpu.with_memory_space_constraint(s, pltpu.HBM) for s in srcs],
      *[pltpu.with_memory_space_constraint(s, pltpu.HBM) for s in lands], gate)
    return outs[:3], None, list(outs[3:3 + n]), list(outs[3 + n:3 + 2 * n]), outs[-1]


def _exchange_wait(name, started, scatter, after):
    (send, recv, own), _, srcs, lands, _ = started
    n = len(srcs)

    def body(*refs):
        src, land = refs[:n], refs[n:2 * n]
        send, recv, own = refs[2 * n:2 * n + 3]
        me, peers = _mesh_place()
        for k in range(n):
            pltpu.make_async_copy(src[k].at[me] if _scatters(scatter, k) else src[k], land[k].at[me], own.at[k]).wait()
            for p, (dev, idx) in enumerate(peers):
                cp = pltpu.make_async_remote_copy(src_ref=src[k].at[idx] if _scatters(scatter, k) else src[k], dst_ref=land[k].at[idx],
                                                  send_sem=send.at[k * N_PEERS + p], recv_sem=recv.at[k * N_PEERS + p], device_id=dev,
                                                  device_id_type=MESH)
                cp.wait_send()
                cp.wait_recv()

    hbm = lambda a: pltpu.HBM(a.shape, a.dtype)
    outs = pl.pallas_call(
        body, name=name, out_shape=(*[hbm(s) for s in srcs], *[hbm(s) for s in lands]),
        in_specs=[HBM_SPEC] * (2 * n) + [SEM_SPEC, SEM_SPEC, SEM_SPEC, ANY_SPEC], out_specs=tuple([HBM_SPEC] * (2 * n)),
        input_output_aliases={j: j for j in range(2 * n)},
        compiler_params=pltpu.CompilerParams(has_side_effects=EFFECT),
    )(*srcs, *lands, send, recv, own, after)
    return list(outs[n:])


def _sum_devices(name, parts):
    def body(p_ref, o_ref):
        tot = p_ref[0]
        for j in range(1, N_DEV):
            tot = tot + p_ref[j]
        o_ref[...] = tot

    return pl.pallas_call(body, name=name, out_shape=_sds(parts.shape[1:], F32),
                          compiler_params=pltpu.CompilerParams(vmem_limit_bytes=VMEM_LIMIT))(parts)


def _rows(a, width=D_MODEL):
    flat = a.reshape(-1)
    return jnp.pad(flat, (0, (-flat.shape[0]) % width)).reshape(-1, width)


def _pack_rows(parts):
    blocks = []
    for p in parts:
        r = _rows(p)
        blocks.append(jnp.pad(r, ((0, (-r.shape[0]) % 8), (0, 0))))
    return jnp.concatenate(blocks, axis=0)


def _unpack_rows(rows, shapes):
    out, at = [], 0
    for s in shapes:
        size = int(np.prod(s))
        n = -(-size // D_MODEL)
        out.append(rows[at:at + n].reshape(-1)[:size].reshape(s))
        at += -(-n // 8) * 8
    return out


def kernel(x, norm_mix, norm_ffn, norm_final, attn_w_in, attn_w_out, attn_sinks, hgrn_w_in, hgrn_w_out, hgrn_norm, hgrn_lb_logits, ffn_w_up, ffn_conv_w, ffn_conv_b, ffn_w_down, loss_target, m_norm_mix, m_norm_ffn, m_norm_final, m_attn_w_in, m_attn_w_out, m_attn_sinks, m_hgrn_w_in, m_hgrn_w_out, m_hgrn_norm, m_hgrn_lb_logits, m_ffn_w_up, m_ffn_conv_w, m_ffn_conv_b, m_ffn_w_down, v_norm_mix, v_norm_ffn, v_norm_final, v_attn_w_in, v_attn_w_out, v_attn_sinks, v_hgrn_w_in, v_hgrn_w_out, v_hgrn_norm, v_hgrn_lb_logits, v_ffn_w_up, v_ffn_conv_w, v_ffn_conv_b, v_ffn_w_down):
    S = x.shape[1]
    n_attn, n_hgrn = attn_w_in.shape[0], hgrn_w_in.shape[0]

    wa_in_t, wa_out_b = attn_w_in.transpose(0, 2, 1).astype(BF16), attn_w_out.astype(BF16)
    wh_in_b, wh_out_b = hgrn_w_in.astype(BF16), hgrn_w_out.astype(BF16)
    wf_up_b, wf_down_b = ffn_w_up.transpose(0, 2, 1).astype(BF16), ffn_w_down.astype(BF16)
    conv_b = ffn_conv_b.reshape(DEPTH, 2, 4, 1, FF_SLOT)
    lb = _lb_fwd("lb_fwd", hgrn_lb_logits)

    def unit_shards(l, part):
        if part == "ffn":
            return [wf_up_b[l], wf_down_b[l], ffn_conv_w[l]]
        return [wa_in_t[l // 2], wa_out_b[l // 2]] if l % 2 == 0 else [wh_in_b[l // 2], wh_out_b[l // 2]]

    def unit_weights(l, part, w):
        if part == "ffn":
            return w[0][None], w[1].reshape(1, 4, FF_SLOT, D_MODEL), w[2].reshape(2, 4, 3, FF_SLOT)
        if l % 2 == 0:
            return w[0].reshape(1, ATTN_IN, D_MODEL), w[1].reshape(1, D_MODEL, D_MODEL)
        return w[0][None], w[1].reshape(1, D_MODEL, D_MODEL)

    units = [(l, part) for l in range(DEPTH) for part in ("mix", "ffn")]
    gathers = [_exchange_start("gather_start0", unit_shards(*units[0]), False, norm_final)]
    gathers.append(_exchange_start("gather_start1", unit_shards(*units[1]), False, gathers[0][4]))
    arrived = _exchange_wait("gather_wait0", gathers[0], False, gathers[1][4])
    weights, saved = {}, [dict() for _ in range(DEPTH)]
    h = x[0]
    hn = _rmsnorm_fwd("norm_mix_fwd0", h, norm_mix[0:1])
    for n, (l, part) in enumerate(units):
        i, sv = l // 2, saved[l]
        weights[l, part] = w = unit_weights(l, part, arrived)
        dep = None
        if n + 2 < len(units):
            gathers.append(_exchange_start(f"gather_start{n + 2}", unit_shards(*units[n + 2]), False, arrived[0]))
            dep = gathers[n + 2][4]
        if part == "mix":
            sv["h"], sv["hn"] = h, hn
            if l % 2 == 0:
                sv["proj"] = _proj_rows(f"attn_proj{i}", hn, w[0], 0, BF16, dep)
                sv["o"], *sv["kept"] = _attn_fwd(f"attn_fwd{i}", sv["proj"], attn_sinks[i:i + 1])
                h, hn = _out_proj(f"attn_out{i}", sv["o"], w[1], 0, h, norm_ffn[l:l + 1])
            else:
                sv["z"] = _proj_slots(f"hgrn_proj{i}", hn, w[0], 0, dep=dep).reshape(4, 2, S, HG_SLOT)
                sv["o"], *sv["kept"] = _hg_fwd(f"hgrn_fwd{i}", sv["z"], lb[i:i + 1], hgrn_norm[i:i + 1])
                h, hn = _out_proj(f"hgrn_out{i}", sv["o"], w[1], 0, h, norm_ffn[l:l + 1])
        else:
            sv["h2"], sv["hn2"] = h, hn
            sv["u"] = _proj_slots(f"ffn_up{l}", hn, w[0], 0, True, dep, BF16).reshape(2, 4, S, FF_SLOT)
            sv["a"], sv["convs"] = _convgate_fwd(f"ffn_gate{l}", sv["u"], w[2], conv_b[l])
            if l + 1 < DEPTH:
                h, hn = _down_proj(f"ffn_down{l}", sv["a"], w[1], 0, h, norm_mix[l + 1:l + 2])
            else:
                h = _down_proj(f"ffn_down{l}", sv["a"], w[1], 0, h)
        if n + 1 < len(units):
            arrived = _exchange_wait(f"gather_wait{n + 1}", gathers[n + 1], False, h)
    dh, d_norm_final, loss_rows, dhb = _loss_head("loss_head", h, norm_final[None], loss_target[0])

    d_conv_w, d_conv_b, d_norm_mix, d_norm_ffn = [None] * DEPTH, [None] * DEPTH, [None] * DEPTH, [None] * DEPTH
    d_sinks, d_lb, d_hgrn_norm = [None] * n_attn, [None] * n_hgrn, [None] * n_hgrn
    received, pending = {}, []
    for l, part in reversed(units):
        i, sv, w = l // 2, saved[l], weights[l, part]
        dep = pending[-1][1][4] if pending else None
        if part == "ffn":
            da = _dgrad_down(f"ffn_down_dgrad{l}", dhb, w[1], 0, dep)
            g_down = _wgrad_down(f"ffn_down_wgrad{l}", sv["a"], dhb).reshape(N_DEV, D_FF // N_DEV, D_MODEL)
            du, d_conv_w[l], d_conv_b[l] = _convgate_bwd(f"ffn_gate_bwd{l}", sv["u"], sv["convs"], w[2], da)
            du = du.reshape(N_DEV, S, FF_SLOT)
            grads = [_wgrad_slots(f"ffn_up_wgrad{l}", sv["hn2"], du, True), g_down]
            dh, d_norm_ffn[l], dhb = _dgrad_slots(f"ffn_up_dgrad{l}", du, w[0], 0, (sv["h2"], norm_ffn[l:l + 1], dh), True)
        else:
            if l % 2 == 0:
                do = _dgrad_out(f"attn_out_dgrad{i}", dhb, w[1], 0, BF16, dep)
                g_out = _wgrad_rows(f"attn_out_wgrad{i}", sv["o"], dhb)
                dproj, d_sinks[i] = _attn_bwd(f"attn_bwd{i}", sv["proj"], *sv["kept"], do)
                g_in = _wgrad_rows(f"attn_proj_wgrad{i}", dproj, sv["hn"]).reshape(N_DEV, ATTN_IN // N_DEV, D_MODEL)
                dh_new = _dgrad_rows(f"attn_proj_dgrad{i}", dproj, w[0], 0, (sv["h"], norm_mix[l:l + 1], dh))
            else:
                dog = _dgrad_out(f"hgrn_out_dgrad{i}", dhb, w[1], 0, F32, dep)
                g_out = _wgrad_rows(f"hgrn_out_wgrad{i}", sv["o"], dhb)
                dz, d_lb[i], dng = _hg_bwd(f"hgrn_bwd{i}", sv["z"], lb[i:i + 1], hgrn_norm[i:i + 1], *sv["kept"], dog)
                d_hgrn_norm[i] = dng[0] + dng[1]
                dz = dz.reshape(N_DEV, S, HG_SLOT)
                g_in = _wgrad_slots(f"hgrn_proj_wgrad{i}", sv["hn"], dz)
                dh_new = _dgrad_slots(f"hgrn_proj_dgrad{i}", dz, w[0], 0, (sv["h"], norm_mix[l:l + 1], dh))
            grads = [g_in, g_out.reshape(N_DEV, D_MODEL // N_DEV, D_MODEL)]
            dh, d_norm_mix[l], dhb = dh_new
        gate = dh
        if len(pending) == 2:
            key, oldest = pending.pop(0)
            received[key] = _exchange_wait(f"scatter_wait_{key[1]}{key[0]}", oldest, True, dh)
            gate = received[key][0]
        pending.append(((l, part), _exchange_start(f"scatter_start_{part}{l}", grads, True, gate)))
    grad_x = dh[None]

    small_shapes = [(DEPTH, D_MODEL), (DEPTH, D_MODEL), (1, D_MODEL), (1, D_MODEL), (n_hgrn, D_MODEL), (n_attn, 128),
                    (n_hgrn, HG_K), (DEPTH, 2 * D_FF)]
    partial = _pack_rows([
        jnp.concatenate(d_norm_mix), jnp.concatenate(d_norm_ffn), d_norm_final, loss_rows, jnp.concatenate(d_lb),
        jnp.concatenate(d_sinks), jnp.concatenate(d_hgrn_norm), jnp.stack(d_conv_b)])
    d_taps = jnp.stack(d_conv_w).reshape(DEPTH, N_DEV, 3, FF_SLOT).transpose(1, 0, 2, 3).reshape(N_DEV, DEPTH * 3, FF_SLOT)
    small_started = _exchange_start("small_start", [partial, d_taps], (False, True), pending[-1][1][4])
    attn_layers, hgrn_layers = range(0, DEPTH, 2), range(1, DEPTH, 2)

    def transposed(ts):
        return [t.transpose(0, 2, 1) for t in ts]

    big = {"hgrn_w_in": _adamw("adamw_hgrn_in", [received[l, "mix"][0] for l in hgrn_layers], hgrn_w_in, m_hgrn_w_in,
                               v_hgrn_w_in, dep=small_started[4])}
    big["hgrn_w_out"] = _adamw("adamw_hgrn_out", [received[l, "mix"][1] for l in hgrn_layers], hgrn_w_out, m_hgrn_w_out, v_hgrn_w_out)
    key, oldest = pending.pop(0)
    received[key] = _exchange_wait(f"scatter_wait_{key[1]}{key[0]}", oldest, True, big["hgrn_w_in"][3])
    up_t = _adamw("adamw_ffn_up", [received[l, "ffn"][0] for l in range(DEPTH)], *transposed((ffn_w_up, m_ffn_w_up, v_ffn_w_up)))
    big["ffn_w_up"] = transposed(up_t)
    big["ffn_w_down"] = _adamw("adamw_ffn_down", [received[l, "ffn"][1] for l in range(DEPTH)], ffn_w_down, m_ffn_w_down, v_ffn_w_down)
    key, oldest = pending.pop(0)
    received[key] = _exchange_wait(f"scatter_wait_{key[1]}{key[0]}", oldest, True, up_t[3])
    small_parts, taps_parts = _exchange_wait("small_wait", small_started, (False, True), up_t[3])
    total = _sum_devices("sum_small", small_parts)
    (g_norm_mix, g_norm_ffn, g_norm_final, loss_sum, g_lb, g_sinks, g_hgrn_norm, g_conv_b) = _unpack_rows(total, small_shapes)

    loss = jnp.sum(loss_sum)
    g_norm_final = g_norm_final[0]
    g_sinks = g_sinks[:, :N_Q_HEADS]
    g_lb_logits = _lb_bwd("lb_bwd", hgrn_lb_logits, g_lb)

    big.update({
        "attn_w_in": transposed(_adamw("adamw_attn_in", [received[l, "mix"][0] for l in attn_layers],
                                       *transposed((attn_w_in, m_attn_w_in, v_attn_w_in)))),
        "attn_w_out": _adamw("adamw_attn_out", [received[l, "mix"][1] for l in attn_layers], attn_w_out, m_attn_w_out, v_attn_w_out),
        "ffn_conv_w": _adamw("adamw_conv_w", [taps_parts[:, 3 * l:3 * l + 3] for l in range(DEPTH)], ffn_conv_w, m_ffn_conv_w,
                             v_ffn_conv_w),
    })
    small_w = [norm_mix, norm_ffn, norm_final, attn_sinks, hgrn_norm, hgrn_lb_logits, ffn_conv_b]
    small_m = [m_norm_mix, m_norm_ffn, m_norm_final, m_attn_sinks, m_hgrn_norm, m_hgrn_lb_logits, m_ffn_conv_b]
    small_v = [v_norm_mix, v_norm_ffn, v_norm_final, v_attn_sinks, v_hgrn_norm, v_hgrn_lb_logits, v_ffn_conv_b]
    small_g = [g_norm_mix, g_norm_ffn, g_norm_final, g_sinks, g_hgrn_norm, g_lb_logits, g_conv_b]
    outs = _adamw("adamw_small", [_pack_rows(small_g)[None]], *[_pack_rows(t)[None] for t in (small_w, small_m, small_v)])
    outs = [o[0] for o in outs]
    shapes = [w.shape for w in small_w]
    small = {n: [t[j] for t in [_unpack_rows(o, shapes) for o in outs]]
             for j, n in enumerate(["norm_mix", "norm_ffn", "norm_final", "attn_sinks", "hgrn_norm", "hgrn_lb_logits", "ffn_conv_b"])}
    order = ["norm_mix", "norm_ffn", "norm_final", "attn_w_in", "attn_w_out", "attn_sinks", "hgrn_w_in", "hgrn_w_out",
             "hgrn_norm", "hgrn_lb_logits", "ffn_w_up", "ffn_conv_w", "ffn_conv_b", "ffn_w_down"]
    res = {**big, **small}
    return (loss, grad_x, *[res[n][0] for n in order], *[res[n][1] for n in order], *[res[n][2] for n in order],
            *[res[n][3] for n in order])
```

```python
import numpy as np
import jax
import jax.numpy as jnp
from jax import lax
from jax.experimental import pallas as pl
from jax.experimental.pallas import tpu as pltpu

F32 = jnp.float32
BF16 = jnp.bfloat16

D_MODEL = 1024
DEPTH = 4
HEAD_DIM = 64
N_Q_HEADS = 16
N_KV_HEADS = 4
Q_PER_KV = 4
ATTN_BLOCK = 128
ATTN_IN = 1536
HG_HEADS = 8
HG_K = 128
HG_CHUNK = 64
HG_IN = 4096
D_FF = 2816
EPS = 1e-6
N_DEV = 8
FF_SLOT = 2 * D_FF // N_DEV
HG_SLOT = HG_IN // N_DEV
HG_LEVELS = 6

ADAM_LR = 0.001
ADAM_B1 = 0.9
ADAM_B2 = 0.999
ADAM_EPS = 1e-08
ADAM_WD = 0.01
ADAM_STEP = 10

VMEM_LIMIT = 56 * 1024 * 1024
ROW_TILE = 1024
WIDE_ROW_TILE = 2048
SLOTS_PER_STEP = 2
NEG_BIG = -1e30

NN = (((1,), (0,)), ((), ()))
NT = (((1,), (1,)), ((), ()))
TN = (((0,), (0,)), ((), ()))


def _bdot(a, b, dn):
    return lax.dot_general(a.astype(BF16), b.astype(BF16), dn, preferred_element_type=F32)


def _sds(shape, dtype):
    return jax.ShapeDtypeStruct(tuple(shape), dtype)


def _params(sem):
    return pltpu.CompilerParams(dimension_semantics=sem, vmem_limit_bytes=VMEM_LIMIT)


DEP_SHAPE = (8, 128)


def _dep_spec(rank):
    return pl.BlockSpec(DEP_SHAPE, lambda *_: (0, 0))


def _matmul(name, a, b, *, dn, grid, a_spec, b_spec, o_spec, out_shape, acc_shape=None, extra=(), extra_specs=(),
            finish=None, dep=None, sem=("parallel", "parallel", "arbitrary")):
    nk = grid[2]
    many = isinstance(out_shape, (list, tuple))
    n_in = 2 + len(extra) + (dep is not None)
    n_out = len(out_shape) if many else 1

    def body(*refs):
        a_ref, b_ref = refs[0], refs[1]
        outs = refs[n_in:n_in + n_out]

        def prod():
            if len(a_ref.shape) == 3:
                return sum(_bdot(a_ref[s], b_ref[s], dn) for s in range(a_ref.shape[0]))
            return _bdot(a_ref[...], b_ref[...], dn)

        def done(v):
            if finish is None:
                outs[0][...] = v.astype(outs[0].dtype)
            else:
                finish(v, refs[2:2 + len(extra)], outs)

        if nk == 1:
            done(prod())
        else:
            acc = refs[-1]
            k = pl.program_id(2)

            @pl.when(k == 0)
            def _():
                acc[...] = prod()

            @pl.when(k > 0)
            def _():
                acc[...] += prod()

            @pl.when(k == nk - 1)
            def _():
                done(acc[...])

    in_specs = [a_spec, b_spec, *extra_specs] + ([_dep_spec(3)] if dep is not None else [])
    args = (a, b, *extra) + ((dep,) if dep is not None else ())
    scratch = [] if nk == 1 else [pltpu.VMEM(acc_shape, F32)]
    return pl.pallas_call(
        body, name=name, grid=grid, in_specs=in_specs, out_specs=o_spec, out_shape=out_shape,
        scratch_shapes=scratch, compiler_params=_params(sem),
    )(*args)


def _rms(x):
    return lax.rsqrt(jnp.mean(x * x, axis=-1, keepdims=True) + EPS)


def _residual_finish(v, ex, outs):
    h = v + ex[0][...]
    outs[0][...] = h
    if len(ex) > 1:
        outs[1][...] = (h * _rms(h) * ex[1][...]).astype(outs[1].dtype)


def _norm_bwd_finish(v, ex, outs):
    x = ex[0][...]
    r = _rms(x)
    xh = x * r
    dyg = v * ex[1][...]
    dh = ex[2][...] + r * (dyg - xh * jnp.mean(dyg * xh, axis=-1, keepdims=True))
    outs[0][...] = dh
    outs[2][...] = dh.astype(outs[2].dtype)
    part = jnp.sum(v * xh, axis=0, keepdims=True)

    @pl.when(pl.program_id(0) == 0)
    def _():
        outs[1][...] = part

    @pl.when(pl.program_id(0) > 0)
    def _():
        outs[1][...] += part


def _row_io(tm, norm_g):
    row = pl.BlockSpec((tm, D_MODEL), lambda i, j, k: (i, 0))
    vec = pl.BlockSpec((1, D_MODEL), lambda i, j, k: (0, 0))
    if norm_g is None:
        return (row,), row, lambda S: _sds((S, D_MODEL), F32)
    return (row, vec), [row, row], lambda S: [_sds((S, D_MODEL), F32), _sds((S, D_MODEL), BF16)]


def _tile(n, t):
    return min(n, t)


def _proj_rows(name, hn, wt, l, out_dtype, dep=None):
    S, N = hn.shape[0], wt.shape[1]
    tm, tn = _tile(S, WIDE_ROW_TILE), 512
    return _matmul(
        name, hn, wt, dn=NT, grid=(S // tm, N // tn, 1),
        a_spec=pl.BlockSpec((tm, D_MODEL), lambda i, j, k: (i, 0)),
        b_spec=pl.BlockSpec((None, tn, D_MODEL), lambda i, j, k: (l, j, 0)),
        o_spec=pl.BlockSpec((tm, tn), lambda i, j, k: (i, j)),
        out_shape=_sds((S, N), out_dtype), dep=dep)


def _slot_weight(w, transposed):
    if transposed:
        return w.shape[2], (None, None, w.shape[2], D_MODEL), NT, NN
    return w.shape[3], (None, None, D_MODEL, w.shape[3]), NN, NT


def _proj_slots(name, hn, w, l, transposed=False, dep=None, out_dtype=F32):
    S = hn.shape[0]
    r, blk, dn, _ = _slot_weight(w, transposed)
    tm = _tile(S, WIDE_ROW_TILE)
    return _matmul(
        name, hn, w, dn=dn, grid=(N_DEV, S // tm, 1),
        a_spec=pl.BlockSpec((tm, D_MODEL), lambda j, i, k: (i, 0)),
        b_spec=pl.BlockSpec(blk, lambda j, i, k: (l, j, 0, 0)),
        o_spec=pl.BlockSpec((None, tm, r), lambda j, i, k: (j, i, 0)),
        out_shape=_sds((N_DEV, S, r), out_dtype), dep=dep)


def _out_proj(name, o, w, l, h, norm_g=None):
    S, K = o.shape
    tm = _tile(S, ROW_TILE)
    extra_specs, o_spec, out_shape = _row_io(tm, norm_g)
    return _matmul(
        name, o, w, dn=NN, grid=(S // tm, 1, 1),
        a_spec=pl.BlockSpec((tm, K), lambda i, j, k: (i, 0)),
        b_spec=pl.BlockSpec((None, K, D_MODEL), lambda i, j, k: (l, 0, 0)),
        o_spec=o_spec, out_shape=out_shape(S), extra=(h,) if norm_g is None else (h, norm_g),
        extra_specs=extra_specs, finish=_residual_finish)


def _down_proj(name, a, w, l, h, norm_g=None):
    nj, S, r = a.shape
    tm = _tile(S, ROW_TILE)
    extra_specs, o_spec, out_shape = _row_io(tm, norm_g)
    return _matmul(
        name, a, w, dn=NN, grid=(S // tm, 1, nj // SLOTS_PER_STEP),
        a_spec=pl.BlockSpec((SLOTS_PER_STEP, tm, r), lambda i, j, k: (k, i, 0)),
        b_spec=pl.BlockSpec((None, SLOTS_PER_STEP, r, D_MODEL), lambda i, j, k: (l, k, 0, 0)),
        o_spec=o_spec, out_shape=out_shape(S), acc_shape=(tm, D_MODEL),
        extra=(h,) if norm_g is None else (h, norm_g), extra_specs=extra_specs, finish=_residual_finish)


def _dgrad_down(name, dh, w, l, dep=None):
    S = dh.shape[0]
    nj, r = w.shape[1], w.shape[2]
    tm = _tile(S, WIDE_ROW_TILE)
    return _matmul(
        name, dh, w, dn=NT, grid=(nj, S // tm, 1),
        a_spec=pl.BlockSpec((tm, D_MODEL), lambda j, i, k: (i, 0)),
        b_spec=pl.BlockSpec((None, None, r, D_MODEL), lambda j, i, k: (l, j, 0, 0)),
        o_spec=pl.BlockSpec((None, tm, r), lambda j, i, k: (j, i, 0)),
        out_shape=_sds((nj, S, r), BF16), dep=dep)


def _wgrad_down(name, a, dh):
    nj, S, r = a.shape
    tk = S
    return _matmul(
        name, a, dh, dn=TN, grid=(nj, 1, S // tk),
        a_spec=pl.BlockSpec((None, tk, r), lambda s, j, k: (s, k, 0)),
        b_spec=pl.BlockSpec((tk, D_MODEL), lambda s, j, k: (k, 0)),
        o_spec=pl.BlockSpec((None, r, D_MODEL), lambda s, j, k: (s, 0, 0)),
        out_shape=_sds((nj, r, D_MODEL), BF16), acc_shape=(r, D_MODEL))


def _norm_bwd_io(tm, S):
    row = pl.BlockSpec((tm, D_MODEL), lambda i, j, k: (i, 0))
    vec = pl.BlockSpec((1, D_MODEL), lambda i, j, k: (0, 0))
    return dict(extra_specs=(row, vec, row), o_spec=[row, vec, row],
                out_shape=[_sds((S, D_MODEL), F32), _sds((1, D_MODEL), F32), _sds((S, D_MODEL), BF16)],
                finish=_norm_bwd_finish, sem=("arbitrary", "arbitrary", "arbitrary"))


def _dgrad_slots(name, dz, w, l, norm, transposed=False):
    nj, S, r = dz.shape
    _, blk, _, dn = _slot_weight(w, transposed)
    tm = _tile(S, ROW_TILE)
    return _matmul(
        name, dz, w, dn=dn, grid=(S // tm, 1, nj // SLOTS_PER_STEP),
        a_spec=pl.BlockSpec((SLOTS_PER_STEP, tm, r), lambda i, j, k: (k, i, 0)),
        b_spec=pl.BlockSpec((None, SLOTS_PER_STEP) + blk[2:], lambda i, j, k: (l, k, 0, 0)),
        acc_shape=(tm, D_MODEL), extra=norm, **_norm_bwd_io(tm, S))


def _wgrad_slots(name, hn, dz, transposed=False):
    nj, S, r = dz.shape
    tk = S
    hn_spec = pl.BlockSpec((tk, D_MODEL), lambda s, j, k: (k, 0))
    dz_spec = pl.BlockSpec((None, tk, r), lambda s, j, k: (s, k, 0))
    if transposed:
        return _matmul(
            name, dz, hn, dn=TN, grid=(nj, 1, S // tk), a_spec=dz_spec, b_spec=hn_spec,
            o_spec=pl.BlockSpec((None, r, D_MODEL), lambda s, j, k: (s, 0, 0)),
            out_shape=_sds((nj, r, D_MODEL), BF16), acc_shape=(r, D_MODEL))
    return _matmul(
        name, hn, dz, dn=TN, grid=(nj, 1, S // tk), a_spec=hn_spec, b_spec=dz_spec,
        o_spec=pl.BlockSpec((None, D_MODEL, r), lambda s, j, k: (s, 0, 0)),
        out_shape=_sds((nj, D_MODEL, r), BF16), acc_shape=(D_MODEL, r))


def _dgrad_out(name, dh, w, l, out_dtype, dep=None):
    S, K = dh.shape[0], w.shape[1]
    tm = _tile(S, WIDE_ROW_TILE)
    return _matmul(
        name, dh, w, dn=NT, grid=(S // tm, 1, 1),
        a_spec=pl.BlockSpec((tm, D_MODEL), lambda i, j, k: (i, 0)),
        b_spec=pl.BlockSpec((None, K, D_MODEL), lambda i, j, k: (l, 0, 0)),
        o_spec=pl.BlockSpec((tm, K), lambda i, j, k: (i, 0)),
        out_shape=_sds((S, K), out_dtype), dep=dep)


def _wgrad_rows(name, a, b):
    S, K = a.shape
    tk = S
    return _matmul(
        name, a, b, dn=TN, grid=(1, 1, S // tk),
        a_spec=pl.BlockSpec((tk, K), lambda i, j, k: (k, 0)),
        b_spec=pl.BlockSpec((tk, D_MODEL), lambda i, j, k: (k, 0)),
        o_spec=pl.BlockSpec((K, D_MODEL), lambda i, j, k: (0, 0)),
        out_shape=_sds((K, D_MODEL), BF16), acc_shape=(K, D_MODEL))


def _dgrad_rows(name, dz, wt, l, norm):
    S, N = dz.shape
    tm = _tile(S, ROW_TILE)
    return _matmul(
        name, dz, wt, dn=NN, grid=(S // tm, 1, 1),
        a_spec=pl.BlockSpec((tm, N), lambda i, j, k: (i, 0)),
        b_spec=pl.BlockSpec((None, N, D_MODEL), lambda i, j, k: (l, 0, 0)),
        extra=norm, **_norm_bwd_io(tm, S))


def _rmsnorm_fwd(name, h, g):
    S = h.shape[0]
    tm = _tile(S, ROW_TILE)

    def body(h_ref, g_ref, o_ref):
        x = h_ref[...]
        o_ref[...] = (x * _rms(x) * g_ref[...]).astype(o_ref.dtype)

    row = pl.BlockSpec((tm, D_MODEL), lambda i: (i, 0))
    return pl.pallas_call(
        body, name=name, grid=(S // tm,), in_specs=[row, pl.BlockSpec((1, D_MODEL), lambda i: (0, 0))],
        out_specs=row, out_shape=_sds((S, D_MODEL), BF16), compiler_params=_params(("parallel",)),
    )(h, g)


def _loss_head(name, h, g, target):
    S = h.shape[0]
    tm = _tile(S, ROW_TILE)

    def body(h_ref, g_ref, t_ref, dh_ref, dg_ref, ls_ref, dhb_ref):
        x = h_ref[...]
        r = _rms(x)
        xh = x * r
        diff = xh * g_ref[...] - t_ref[...]
        dyf = diff * (1.0 / D_MODEL)
        dyg = dyf * g_ref[...]
        dh = r * (dyg - xh * jnp.mean(dyg * xh, axis=-1, keepdims=True))
        dh_ref[...] = dh
        dhb_ref[...] = dh.astype(dhb_ref.dtype)
        part = jnp.sum(dyf * xh, axis=0, keepdims=True)
        lpart = jnp.sum(diff * diff, axis=0, keepdims=True) * (0.5 / D_MODEL)

        @pl.when(pl.program_id(0) == 0)
        def _():
            dg_ref[...] = part
            ls_ref[...] = lpart

        @pl.when(pl.program_id(0) > 0)
        def _():
            dg_ref[...] += part
            ls_ref[...] += lpart

    row = pl.BlockSpec((tm, D_MODEL), lambda i: (i, 0))
    vec = pl.BlockSpec((1, D_MODEL), lambda i: (0, 0))
    return pl.pallas_call(
        body, name=name, grid=(S // tm,), in_specs=[row, vec, row], out_specs=[row, vec, vec, row],
        out_shape=[_sds((S, D_MODEL), F32), _sds((1, D_MODEL), F32), _sds((1, D_MODEL), F32), _sds((S, D_MODEL), BF16)],
        compiler_params=_params(("arbitrary",)),
    )(h, g, target)


ATTN_SCALE = HEAD_DIM ** -0.5
ALIBI_SLOPES = [2.0 ** (-8.0 * (h + 1) / N_Q_HEADS) for h in range(N_Q_HEADS)]
K_COL = N_Q_HEADS * HEAD_DIM
KV_COLS = N_KV_HEADS * HEAD_DIM
V_COL = K_COL + KV_COLS


def _attn_masks(n):
    qi = lax.broadcasted_iota(jnp.int32, (ATTN_BLOCK, ATTN_BLOCK), 0)
    ki = lax.broadcasted_iota(jnp.int32, (ATTN_BLOCK, ATTN_BLOCK), 1)
    dist_c = (qi - ki).astype(F32)
    return dist_c + float(ATTN_BLOCK), dist_c, (ki > qi) & (n > 0), qi >= ki


def _attn_probs(raw_p, raw_c, sink, slope, masks):
    dist_p, dist_c, valid_p, valid_c = masks
    sp = jnp.where(valid_p, raw_p * ATTN_SCALE - slope * dist_p, NEG_BIG)
    sc = jnp.where(valid_c, raw_c * ATTN_SCALE - slope * dist_c, NEG_BIG)
    m = jnp.maximum(jnp.maximum(jnp.max(sp, axis=-1, keepdims=True), jnp.max(sc, axis=-1, keepdims=True)), sink)
    ep, ec, es = jnp.exp(sp - m), jnp.exp(sc - m), jnp.exp(sink - m)
    inv = 1.0 / (jnp.sum(ep, axis=-1, keepdims=True) + jnp.sum(ec, axis=-1, keepdims=True) + es)
    return ep * inv, ec * inv, es * inv


def _group_rows(ref, m):
    return jnp.concatenate([ref[:, HEAD_DIM * (Q_PER_KV * m + g):HEAD_DIM * (Q_PER_KV * m + g + 1)]
                            for g in range(Q_PER_KV)], axis=0)


def _head_rows(x, g):
    return x[ATTN_BLOCK * g:ATTN_BLOCK * (g + 1)]


def _attn_specs(nblk):
    last = nblk - 1
    kcol, vcol = K_COL // KV_COLS, V_COL // KV_COLS
    return [
        pl.BlockSpec((ATTN_BLOCK, K_COL), lambda n: (jnp.minimum(n, last), 0)),
        pl.BlockSpec((ATTN_BLOCK, KV_COLS), lambda n: (jnp.minimum(n, last), kcol)),
        pl.BlockSpec((ATTN_BLOCK, KV_COLS), lambda n: (jnp.maximum(jnp.minimum(n, last) - 1, 0), kcol)),
        pl.BlockSpec((ATTN_BLOCK, KV_COLS), lambda n: (jnp.minimum(n, last), vcol)),
        pl.BlockSpec((ATTN_BLOCK, KV_COLS), lambda n: (jnp.maximum(jnp.minimum(n, last) - 1, 0), vcol)),
    ]


P_COLS = 2 * ATTN_BLOCK


def _attn_fwd(name, proj, sinks):
    S = proj.shape[0]
    nblk = S // ATTN_BLOCK

    def body(q_ref, kc_ref, kp_ref, vc_ref, vp_ref, sk_ref, o_ref, p_ref, ps_ref):
        masks = _attn_masks(pl.program_id(0))
        lane = lax.broadcasted_iota(jnp.int32, (ATTN_BLOCK, 128), 1)
        sink_p = jnp.zeros((ATTN_BLOCK, 128), F32)
        for m in range(N_KV_HEADS):
            ks = slice(HEAD_DIM * m, HEAD_DIM * (m + 1))
            kp, kc, vp, vc = kp_ref[:, ks], kc_ref[:, ks], vp_ref[:, ks], vc_ref[:, ks]
            q4 = _group_rows(q_ref, m)
            raw_p, raw_c = _bdot(q4, kp, NT), _bdot(q4, kc, NT)
            pps, pcs = [], []
            for g in range(Q_PER_KV):
                hh = Q_PER_KV * m + g
                pp, pc, ps = _attn_probs(_head_rows(raw_p, g), _head_rows(raw_c, g), sk_ref[0, hh], ALIBI_SLOPES[hh], masks)
                pps.append(pp.astype(BF16))
                pcs.append(pc.astype(BF16))
                p_ref[:, P_COLS * hh:P_COLS * hh + ATTN_BLOCK] = pps[g]
                p_ref[:, P_COLS * hh + ATTN_BLOCK:P_COLS * (hh + 1)] = pcs[g]
                sink_p = jnp.where(lane == hh, ps, sink_p)
            o4 = _bdot(jnp.concatenate(pps, axis=0), vp, NN) + _bdot(jnp.concatenate(pcs, axis=0), vc, NN)
            for g in range(Q_PER_KV):
                hh = Q_PER_KV * m + g
                o_ref[:, HEAD_DIM * hh:HEAD_DIM * (hh + 1)] = _head_rows(o4, g).astype(o_ref.dtype)
        ps_ref[...] = sink_p

    row = lambda cols: pl.BlockSpec((ATTN_BLOCK, cols), lambda n: (n, 0))
    return pl.pallas_call(
        body, name=name, grid=(nblk,),
        in_specs=_attn_specs(nblk) + [pl.BlockSpec(memory_space=pltpu.SMEM)],
        out_specs=[row(K_COL), row(N_Q_HEADS * P_COLS), row(128)],
        out_shape=[_sds((S, K_COL), BF16), _sds((S, N_Q_HEADS * P_COLS), BF16), _sds((S, 128), F32)],
        compiler_params=_params(("parallel",)),
    )(proj, proj, proj, proj, proj, sinks)


def _attn_bwd(name, proj, probs, sink_probs, do):
    S = proj.shape[0]
    nblk = S // ATTN_BLOCK

    def body(q_ref, kc_ref, kp_ref, vc_ref, vp_ref, do_ref, p_ref, ps_ref, dz_ref, ds_ref, carry, cur, padd):
        n = pl.program_id(0)

        @pl.when(n == 0)
        def _():
            carry[...] = jnp.zeros_like(carry)
            ds_ref[...] = jnp.zeros_like(ds_ref)

        @pl.when(n < nblk)
        def _():
            lane = lax.broadcasted_iota(jnp.int32, (ATTN_BLOCK, 128), 1)
            sink_p = ps_ref[...]
            dsv = jnp.zeros((1, 128), F32)
            for m in range(N_KV_HEADS):
                ks = slice(HEAD_DIM * m, HEAD_DIM * (m + 1))
                kp, kc, vp, vc = kp_ref[:, ks], kc_ref[:, ks], vp_ref[:, ks], vc_ref[:, ks]
                q4, do4 = _group_rows(q_ref, m), _group_rows(do_ref, m)
                dpp4, dpc4 = _bdot(do4, vp, NT), _bdot(do4, vc, NT)
                pps, pcs, dsps, dscs = [], [], [], []
                for g in range(Q_PER_KV):
                    hh = Q_PER_KV * m + g
                    pps.append(p_ref[:, P_COLS * hh:P_COLS * hh + ATTN_BLOCK])
                    pcs.append(p_ref[:, P_COLS * hh + ATTN_BLOCK:P_COLS * (hh + 1)])
                    pp, pc = pps[g].astype(F32), pcs[g].astype(F32)
                    dpp, dpc = _head_rows(dpp4, g), _head_rows(dpc4, g)
                    delta = jnp.sum(pp * dpp, axis=-1, keepdims=True) + jnp.sum(pc * dpc, axis=-1, keepdims=True)
                    dsv = dsv - jnp.sum(jnp.where(lane == hh, sink_p, 0.0) * delta, axis=0, keepdims=True)
                    dsps.append((pp * (dpp - delta)).astype(BF16))
                    dscs.append((pc * (dpc - delta)).astype(BF16))
                pp4, pc4 = jnp.concatenate(pps, axis=0), jnp.concatenate(pcs, axis=0)
                dsp4, dsc4 = jnp.concatenate(dsps, axis=0), jnp.concatenate(dscs, axis=0)
                dq4 = (_bdot(dsp4, kp, NN) + _bdot(dsc4, kc, NN)) * ATTN_SCALE
                for g in range(Q_PER_KV):
                    hh = Q_PER_KV * m + g
                    cur[:, HEAD_DIM * hh:HEAD_DIM * (hh + 1)] = _head_rows(dq4, g)
                cur[:, K_COL + HEAD_DIM * m:K_COL + HEAD_DIM * (m + 1)] = _bdot(dsc4, q4, TN) * ATTN_SCALE
                cur[:, V_COL + HEAD_DIM * m:V_COL + HEAD_DIM * (m + 1)] = _bdot(pc4, do4, TN)
                padd[:, ks] = _bdot(dsp4, q4, TN) * ATTN_SCALE
                padd[:, KV_COLS + HEAD_DIM * m:KV_COLS + HEAD_DIM * (m + 1)] = _bdot(pp4, do4, TN)
            ds_ref[...] += dsv
            dz_ref[:, :K_COL] = carry[:, :K_COL].astype(dz_ref.dtype)
            dz_ref[:, K_COL:] = (carry[:, K_COL:] + padd[...]).astype(dz_ref.dtype)
            carry[...] = cur[...]

        @pl.when(n == nblk)
        def _():
            dz_ref[...] = carry[...].astype(dz_ref.dtype)

    return pl.pallas_call(
        body, name=name, grid=(nblk + 1,),
        in_specs=_attn_specs(nblk) + [
            pl.BlockSpec((ATTN_BLOCK, cols), lambda n: (jnp.minimum(n, nblk - 1), 0))
            for cols in (K_COL, N_Q_HEADS * P_COLS, 128)],
        out_specs=[pl.BlockSpec((ATTN_BLOCK, ATTN_IN), lambda n: (jnp.maximum(n - 1, 0), 0)),
                   pl.BlockSpec((1, 128), lambda n: (0, 0))],
        out_shape=[_sds((S, ATTN_IN), BF16), _sds((1, 128), F32)],
        scratch_shapes=[pltpu.VMEM((ATTN_BLOCK, ATTN_IN), F32), pltpu.VMEM((ATTN_BLOCK, ATTN_IN), F32),
                        pltpu.VMEM((ATTN_BLOCK, 2 * KV_COLS), F32)],
        compiler_params=_params(("arbitrary",)),
    )(proj, proj, proj, proj, proj, do, probs, sink_probs)


def _hg_consts():
    C = HG_CHUNK
    tri = np.tril(np.ones((C, C)))
    t = np.arange(C)
    rows, masks = [tri], []
    for lvl in range(HG_LEVELS):
        n = C >> (lvl + 1)
        sel = np.zeros((C, C))
        sel[t, (t // (2 * n)) * (2 * n) + n - 1] = 1.0
        rows.append(sel @ tri)
        tt, ss = t[:, None], t[None, :]
        masks.append((tt // (2 * n) == ss // (2 * n)) & ((tt // n) % 2 == 1) & ((ss // n) % 2 == 0))
    masks.append(np.eye(C, dtype=bool))
    stk = np.concatenate(rows, axis=0)
    return jnp.asarray(stk, BF16), jnp.asarray(np.stack(masks), F32)


def _sigmoid(x):
    return 1.0 / (1.0 + jnp.exp(-x))


def _silu_sigmoid(x):
    return 0.5 + 0.5 * jnp.tanh(0.5 * x)


def _split(x, parts):
    out, rest = [], x
    for _ in range(parts):
        out.append(rest.astype(BF16))
        rest = rest - out[-1].astype(F32)
    return out


def _dot01(m01, x, dn, parts=3):
    return sum(lax.dot_general(m01, p, dn, preferred_element_type=F32) for p in _split(x, parts))


def _ref_rows(b, n):
    C = b.shape[1]
    if 2 * n >= 8:
        b3 = b.reshape(HG_CHUNK // (2 * n), 2 * n, C)
        return jnp.broadcast_to(b3[:, n - 1:n, :], b3.shape).reshape(HG_CHUNK, C)
    pos = lax.broadcasted_iota(jnp.int32, b.shape, 0) % (2 * n)
    out = b
    for p in range(2 * n):
        if p != n - 1:
            out = jnp.where(pos == p, pltpu.roll(b, (p - (n - 1)) % HG_CHUNK, 0), out)
    return out


HG_STEP_CHUNKS = 8


def _chunk_rows(ci):
    return pl.ds(pl.multiple_of(ci * HG_CHUNK, HG_CHUNK), HG_CHUNK)


def _hg_common(z_ref, rows, lb_ref, stk_ref):
    qr, fr = z_ref[0, rows, :], z_ref[1, rows, :]
    lb = lb_ref[...]
    sq, sg, sgn = _silu_sigmoid(qr), _sigmoid(fr), _sigmoid(-fr)
    ft = lb + (1.0 - lb) * sg
    b = _dot01(stk_ref[0:HG_CHUNK, :], jnp.log(ft), NN)
    ws = [jnp.exp(-jnp.abs(b - _ref_rows(b, HG_CHUNK >> (l + 1)))) for l in range(HG_LEVELS)]
    blast = b[HG_CHUNK - 1:HG_CHUNK]
    return dict(qr=qr, fr=fr, lb=lb, sq=sq, sg=sg, sgn=sgn, ft=ft, q=qr * sq, kk=(1.0 - lb) * sgn, b=b,
                ws=ws, eb=jnp.exp(b), ed=jnp.exp(blast - b), elast=jnp.exp(blast))


def _hg_factors(qh, kh, ws, sl):
    return ([(qh * ws[l][:, sl]).astype(BF16) for l in range(HG_LEVELS)],
            [(kh * ws[l][:, sl]).astype(BF16) for l in range(HG_LEVELS)])


def _hg_intra(qh, kh, ws, msk_ref, sl):
    qls, kls = _hg_factors(qh, kh, ws, sl)
    a = msk_ref[HG_LEVELS] * _bdot(qh, kh, NT)
    for l in range(HG_LEVELS):
        a = a + msk_ref[l] * _bdot(qls[l], kls[l], NT)
    return a


def _hg_fwd(name, z, lb, ng):
    S = z.shape[2]
    nc = S // HG_CHUNK
    per = min(HG_STEP_CHUNKS, nc)
    stk, msk = _hg_consts()

    def body(z_ref, lb_ref, ng_ref, stk_ref, msk_ref, og_ref, st_ref, a_ref, o_ref, state):
        @pl.when(pl.program_id(1) == 0)
        def _():
            state[...] = jnp.zeros_like(state)

        def chunk(ci, _):
            rows = _chunk_rows(ci)
            cm = _hg_common(z_ref, rows, lb_ref, stk_ref)
            v, gt = z_ref[2, rows, :], z_ref[3, rows, :]
            kd = cm["kk"] * cm["ed"]
            for hh in range(4):
                sl = slice(HG_K * hh, HG_K * (hh + 1))
                st = state[hh]
                st_ref[ci, hh] = st
                qh, kh, vh = cm["q"][:, sl], cm["kk"][:, sl], v[:, sl]
                a = _hg_intra(qh, kh, cm["ws"], msk_ref, sl).astype(BF16)
                a_ref[ci, hh] = a
                o = _bdot(a, vh, NN) + _bdot(qh * cm["eb"][:, sl], st, NT)
                o_ref[rows, sl] = o
                state[hh] = cm["elast"][:, sl] * st + _bdot(vh, kd[:, sl], TN)
                gh = gt[:, sl]
                og_ref[rows, sl] = (o * _rms(o) * ng_ref[...] * (gh * _silu_sigmoid(gh))).astype(og_ref.dtype)
            return 0

        lax.fori_loop(0, per, chunk, 0, unroll=True)

    return pl.pallas_call(
        body, name=name, grid=(2, nc // per),
        in_specs=[pl.BlockSpec((4, None, per * HG_CHUNK, HG_SLOT), lambda g, c: (0, g, c, 0)),
                  pl.BlockSpec((1, HG_SLOT), lambda g, c: (0, g)),
                  pl.BlockSpec((1, HG_K), lambda g, c: (0, 0)),
                  pl.BlockSpec(stk.shape, lambda g, c: (0, 0)),
                  pl.BlockSpec(msk.shape, lambda g, c: (0, 0, 0))],
        out_specs=[pl.BlockSpec((per * HG_CHUNK, HG_SLOT), lambda g, c: (c, g)),
                   pl.BlockSpec((per, 4, HG_K, HG_K), lambda g, c: (c, g, 0, 0)),
                   pl.BlockSpec((per, 4, HG_CHUNK, HG_CHUNK), lambda g, c: (c, g, 0, 0)),
                   pl.BlockSpec((per * HG_CHUNK, HG_SLOT), lambda g, c: (c, g))],
        out_shape=[_sds((S, D_MODEL), BF16), _sds((nc, HG_HEADS, HG_K, HG_K), F32),
                   _sds((nc, HG_HEADS, HG_CHUNK, HG_CHUNK), BF16), _sds((S, D_MODEL), F32)],
        scratch_shapes=[pltpu.VMEM((4, HG_K, HG_K), F32)],
        compiler_params=_params(("parallel", "arbitrary")),
    )(z, lb, ng, stk, msk)


def _hg_bwd(name, z, lb, ng, states, intra, o_pre, dog):
    S = z.shape[2]
    nc = S // HG_CHUNK
    per = min(HG_STEP_CHUNKS, nc)
    stk, msk = _hg_consts()

    def body(z_ref, lb_ref, ng_ref, stk_ref, msk_ref, st_ref, a_ref, o_ref, dog_ref, dz_ref, dlb_ref, dng_ref, dstate):
        @pl.when(pl.program_id(1) == 0)
        def _():
            dstate[...] = jnp.zeros_like(dstate)
            dlb_ref[...] = jnp.zeros_like(dlb_ref)
            dng_ref[...] = jnp.zeros_like(dng_ref)

        def chunk(k, _):
            ci = per - 1 - k
            rows = _chunk_rows(ci)
            cm = _hg_common(z_ref, rows, lb_ref, stk_ref)
            v, gt = z_ref[2, rows, :], z_ref[3, rows, :]
            ng = ng_ref[...]
            kd = cm["kk"] * cm["ed"]
            row = lax.broadcasted_iota(jnp.int32, (HG_CHUNK, 1), 0)
            dng = jnp.zeros((1, HG_K), F32)
            dq_h, dkk_h, db_h, dv_h, dgt_h = [], [], [], [], []
            dr_h = [[] for _ in range(HG_LEVELS)]
            for hh in range(4):
                sl = slice(HG_K * hh, HG_K * (hh + 1))
                st, dst = st_ref[ci, hh], dstate[hh]
                qh, kh, vh, ebh, edh, kdh = cm["q"][:, sl], cm["kk"][:, sl], v[:, sl], cm["eb"][:, sl], cm["ed"][:, sl], kd[:, sl]
                elh = cm["elast"][:, sl]
                qls, kls = _hg_factors(qh, kh, cm["ws"], sl)
                a, o = a_ref[ci, hh], o_ref[rows, sl]
                qe = qh * ebh
                r = _rms(o)
                xh = o * r
                gh = gt[:, sl]
                sgg = _silu_sigmoid(gh)
                dog = dog_ref[rows, sl].astype(F32)
                dy = dog * (gh * sgg)
                dgt_h.append(dog * (xh * ng) * (sgg * (1.0 + gh * (1.0 - sgg))))
                dng = dng + jnp.sum(dy * xh, axis=0, keepdims=True)
                dyg = dy * ng
                do = r * (dyg - xh * jnp.mean(dyg * xh, axis=-1, keepdims=True))
                da = _bdot(do, vh, NT)
                dv_h.append(_bdot(a, do, TN) + _bdot(kdh, dst, NT))
                dkd = _bdot(vh, dst, NN)
                delast = jnp.sum(st * dst, axis=0, keepdims=True)
                dqe = _bdot(do, st, NN)
                dstate[hh] = elh * dst + _bdot(do, qe, TN)
                gk = dkd * kdh
                dblast = jnp.sum(gk, axis=0, keepdims=True) + delast * elh
                db = dqe * qe - gk + jnp.where(row == HG_CHUNK - 1, dblast, 0.0)
                dp = (msk_ref[HG_LEVELS] * da).astype(BF16)
                dq = dqe * ebh + _bdot(dp, kh, NN)
                dkk = dkd * edh + _bdot(dp, qh, TN)
                for l in range(HG_LEVELS):
                    dp = (msk_ref[l] * da).astype(BF16)
                    dql, dkl = _bdot(dp, kls[l], NN), _bdot(dp, qls[l], TN)
                    w = cm["ws"][l][:, sl]
                    dq = dq + dql * w
                    dkk = dkk + dkl * w
                    half = jnp.where(((row >> (HG_LEVELS - 1 - l)) & 1) == 1, 1.0, -1.0)
                    dd = half * w * (dql * qh + dkl * kh)
                    db = db + dd
                    dr_h[l].append(-dd)
                dq_h.append(dq)
                dkk_h.append(dkk)
                db_h.append(db)
            cat = lambda xs: jnp.concatenate(xs, axis=1)
            cot = jnp.concatenate([cat(db_h)] + [cat(dr_h[l]) for l in range(HG_LEVELS)], axis=0)
            dlf = _dot01(stk_ref[...], cot, TN, parts=2)
            dq, dkk = cat(dq_h), cat(dkk_h)
            dft = dlf / cm["ft"]
            one_lb = 1.0 - cm["lb"]
            dz_ref[0, rows, :] = (dq * (cm["sq"] * (1.0 + cm["qr"] * (1.0 - cm["sq"])))).astype(dz_ref.dtype)
            dz_ref[1, rows, :] = ((dft - dkk) * one_lb * cm["sg"] * cm["sgn"]).astype(dz_ref.dtype)
            dz_ref[2, rows, :] = cat(dv_h).astype(dz_ref.dtype)
            dz_ref[3, rows, :] = cat(dgt_h).astype(dz_ref.dtype)
            dlb_ref[...] += jnp.sum((dft - dkk) * cm["sgn"], axis=0, keepdims=True)
            dng_ref[...] += dng
            return 0

        lax.fori_loop(0, per, chunk, 0, unroll=True)

    rev = lambda c: nc // per - 1 - c
    rows_blk = pl.BlockSpec((per * HG_CHUNK, HG_SLOT), lambda g, c: (rev(c), g))
    return pl.pallas_call(
        body, name=name, grid=(2, nc // per),
        in_specs=[pl.BlockSpec((4, None, per * HG_CHUNK, HG_SLOT), lambda g, c: (0, g, rev(c), 0)),
                  pl.BlockSpec((1, HG_SLOT), lambda g, c: (0, g)),
                  pl.BlockSpec((1, HG_K), lambda g, c: (0, 0)),
                  pl.BlockSpec(stk.shape, lambda g, c: (0, 0)),
                  pl.BlockSpec(msk.shape, lambda g, c: (0, 0, 0)),
                  pl.BlockSpec((per, 4, HG_K, HG_K), lambda g, c: (rev(c), g, 0, 0)),
                  pl.BlockSpec((per, 4, HG_CHUNK, HG_CHUNK), lambda g, c: (rev(c), g, 0, 0)),
                  rows_blk, rows_blk],
        out_specs=[pl.BlockSpec((4, None, per * HG_CHUNK, HG_SLOT), lambda g, c: (0, g, rev(c), 0)),
                   pl.BlockSpec((1, HG_SLOT), lambda g, c: (0, g)),
                   pl.BlockSpec((None, 1, HG_K), lambda g, c: (g, 0, 0))],
        out_shape=[_sds(z.shape, BF16), _sds((1, 2 * HG_SLOT), F32), _sds((2, 1, HG_K), F32)],
        scratch_shapes=[pltpu.VMEM((4, HG_K, HG_K), F32)],
        compiler_params=_params(("parallel", "arbitrary")),
    )(z, lb, ng, stk, msk, states, intra, o_pre, dog)


def _lb_fwd(name, logits):
    def body(l_ref, o_ref):
        x = l_ref[...]
        e = jnp.exp(x - jnp.max(x, axis=0, keepdims=True))
        s = e / jnp.sum(e, axis=0, keepdims=True)
        o_ref[0:1, :] = s[1:2]
        o_ref[1:2, :] = s[1:2] + s[2:3] + s[3:4]

    return pl.pallas_call(body, name=name, out_shape=_sds((2, logits.shape[1]), F32))(logits)


def _lb_bwd(name, logits, dlb):
    def body(l_ref, d_ref, o_ref):
        x = l_ref[...]
        e = jnp.exp(x - jnp.max(x, axis=0, keepdims=True))
        s = e / jnp.sum(e, axis=0, keepdims=True)
        d1, d3 = d_ref[0:1, :], d_ref[1:2, :]
        ds = [jnp.zeros_like(d1), d1 + d3, d3, d3]
        dot = sum(ds[r] * s[r:r + 1] for r in range(1, DEPTH))
        for r in range(DEPTH):
            o_ref[r:r + 1, :] = s[r:r + 1] * (ds[r] - dot)

    return pl.pallas_call(body, name=name, out_shape=_sds(logits.shape, F32))(logits, dlb)


SUB = 8


def _rows_down(x, prev, k):
    row = lax.broadcasted_iota(jnp.int32, x.shape, 0)
    return jnp.where(row >= k, pltpu.roll(x, k, 0), pltpu.roll(prev, k, 0))


def _rows_up(x, nxt, k):
    row = lax.broadcasted_iota(jnp.int32, x.shape, 0)
    return jnp.where(row < SUB - k, pltpu.roll(x, SUB - k, 0), pltpu.roll(nxt, SUB - k, 0))


def _conv_block(w_ref, b_ref, p, x, prev):
    return (b_ref[p] + w_ref[p, 0:1, :] * _rows_down(x, prev, 2) + w_ref[p, 1:2, :] * _rows_down(x, prev, 1)
            + w_ref[p, 2:3, :] * x)


def _convgate_fwd(name, u, cw, cb):
    S = u.shape[2]
    tm = _tile(S, ROW_TILE)

    def body(u_ref, w_ref, b_ref, a_ref, c_ref, halo):
        @pl.when(pl.program_id(1) == 0)
        def _():
            halo[...] = jnp.zeros_like(halo)

        def step(r, prev):
            pg, pv = prev
            out, cgs, cvs = [], [], []
            rows = pl.ds(pl.multiple_of(r * 2 * SUB, 2 * SUB), 2 * SUB)
            ug16, uv16 = u_ref[0, rows, :].astype(F32), u_ref[1, rows, :].astype(F32)
            for s in range(2):
                xg, xv = ug16[s * SUB:(s + 1) * SUB], uv16[s * SUB:(s + 1) * SUB]
                cgs.append(_conv_block(w_ref, b_ref, 0, xg, pg))
                cvs.append(_conv_block(w_ref, b_ref, 1, xv, pv))
                out.append(cgs[s] * _silu_sigmoid(cgs[s]) * cvs[s])
                pg, pv = xg, xv
            a_ref[rows, :] = jnp.concatenate(out, axis=0).astype(a_ref.dtype)
            c_ref[0, rows, :] = jnp.concatenate(cgs, axis=0).astype(c_ref.dtype)
            c_ref[1, rows, :] = jnp.concatenate(cvs, axis=0).astype(c_ref.dtype)
            return pg, pv

        pg, pv = lax.fori_loop(0, tm // (2 * SUB), step, (halo[0], halo[1]), unroll=2)
        halo[0] = pg
        halo[1] = pv

    pair = pl.BlockSpec((2, None, tm, FF_SLOT), lambda j, t: (0, j, t, 0))
    return pl.pallas_call(
        body, name=name, grid=(4, S // tm),
        in_specs=[pair, pl.BlockSpec((2, None, 3, FF_SLOT), lambda j, t: (0, j, 0, 0)),
                  pl.BlockSpec((2, None, 1, FF_SLOT), lambda j, t: (0, j, 0, 0))],
        out_specs=[pl.BlockSpec((None, tm, FF_SLOT), lambda j, t: (j, t, 0)), pair],
        out_shape=[_sds((4, S, FF_SLOT), BF16), _sds(u.shape, BF16)],
        scratch_shapes=[pltpu.VMEM((2, SUB, FF_SLOT), F32)],
        compiler_params=_params(("parallel", "arbitrary")),
    )(u, cw, cb)


def _convgate_bwd(name, u, convs, cw, da):
    S = u.shape[2]
    tm = _tile(S, ROW_TILE)
    nt = S // tm

    def body(u_ref, c_ref, w_ref, da_ref, du_out, dw_ref, db_ref, after, first, acc, du_ref):
        @pl.when(pl.program_id(1) == 0)
        def _():
            after[...] = jnp.zeros_like(after)
            acc[...] = jnp.zeros_like(acc)

        def finish(p, x, d, nxt, rows):
            taps = (_rows_up(d, nxt, 2), _rows_up(d, nxt, 1), d)
            du_ref[p, rows, :] = w_ref[p, 0:1, :] * taps[0] + w_ref[p, 1:2, :] * taps[1] + w_ref[p, 2:3, :] * d
            for j in range(3):
                acc[p, j] += taps[j] * x
            acc[p, 3] += d

        def step(r, carry):
            xg_last, xv_last, dg_last, dv_last = carry
            rows16 = pl.ds(pl.multiple_of(r * 2 * SUB, 2 * SUB), 2 * SUB)
            dav = da_ref[rows16, :].astype(F32)
            cg16, cv16 = c_ref[0, rows16, :].astype(F32), c_ref[1, rows16, :].astype(F32)
            ug16, uv16 = u_ref[0, rows16, :].astype(F32), u_ref[1, rows16, :].astype(F32)
            for s in range(2):
                at = r * 2 * SUB + s * SUB
                part = slice(s * SUB, (s + 1) * SUB)
                cg, cv, dab = cg16[part], cv16[part], dav[part]
                sg = _silu_sigmoid(cg)
                dg = dab * cv * (sg * (1.0 + cg * (1.0 - sg)))
                dv = dab * cg * sg
                before = pl.ds(pl.multiple_of(at - SUB, SUB), SUB)
                if s == 0:
                    @pl.when(r == 0)
                    def _():
                        first[0] = dg
                        first[1] = dv

                    @pl.when(r > 0)
                    def _():
                        finish(0, xg_last, dg_last, dg, before)
                        finish(1, xv_last, dv_last, dv, before)
                else:
                    finish(0, xg_last, dg_last, dg, before)
                    finish(1, xv_last, dv_last, dv, before)
                xg_last, xv_last, dg_last, dv_last = ug16[part], uv16[part], dg, dv
            return xg_last, xv_last, dg_last, dv_last

        zero = jnp.zeros((SUB, FF_SLOT), F32)
        xg_last, xv_last, dg_last, dv_last = lax.fori_loop(0, tm // (2 * SUB), step, (zero, zero, zero, zero))
        finish(0, xg_last, dg_last, after[0], slice(tm - SUB, tm))
        finish(1, xv_last, dv_last, after[1], slice(tm - SUB, tm))
        du_out[...] = du_ref[...].astype(du_out.dtype)
        after[...] = first[...]
        for p in range(2):
            for j in range(3):
                dw_ref[p, j:j + 1, :] = jnp.sum(acc[p, j], axis=0, keepdims=True)
            db_ref[p] = jnp.sum(acc[p, 3], axis=0, keepdims=True)

    rev = lambda t: nt - 1 - t
    pair = pl.BlockSpec((2, None, tm, FF_SLOT), lambda j, t: (0, j, rev(t), 0))
    taps = pl.BlockSpec((2, None, 3, FF_SLOT), lambda j, t: (0, j, 0, 0))
    bias = pl.BlockSpec((2, None, 1, FF_SLOT), lambda j, t: (0, j, 0, 0))
    return pl.pallas_call(
        body, name=name, grid=(4, nt),
        in_specs=[pair, pair, taps, pl.BlockSpec((None, tm, FF_SLOT), lambda j, t: (j, rev(t), 0))],
        out_specs=[pair, taps, bias],
        out_shape=[_sds(u.shape, BF16), _sds(cw.shape, F32), _sds((2, 4, 1, FF_SLOT), F32)],
        scratch_shapes=[pltpu.VMEM((2, SUB, FF_SLOT), F32), pltpu.VMEM((2, SUB, FF_SLOT), F32),
                        pltpu.VMEM((2, 4, SUB, FF_SLOT), F32), pltpu.VMEM((2, tm, FF_SLOT), F32)],
        compiler_params=_params(("parallel", "arbitrary")),
    )(u, convs, cw, da)


def _row_tile(R):
    for t in range(256, 15, -16):
        if R % t == 0:
            return t
    return R


def _adamw(name, gsrcs, w, m, v, dep=None):
    L = len(gsrcs)
    n, A, C = gsrcs[0].shape
    tr = _row_tile(A)
    deps = () if dep is None else (dep,)

    def body(*refs):
        g_refs = refs[:L]
        w_ref, m_ref, v_ref = refs[L:L + 3]
        go_ref, d_ref, mo_ref, vo_ref = refs[L + 3 + len(deps):]
        for k in range(L):
            @pl.when(pl.program_id(0) == k)
            def _(k=k):
                g = g_refs[k][0].astype(F32)
                for s in range(1, n):
                    g = g + g_refs[k][s].astype(F32)
                m2 = ADAM_B1 * m_ref[...] + (1.0 - ADAM_B1) * g
                v2 = ADAM_B2 * v_ref[...] + (1.0 - ADAM_B2) * (g * g)
                m_hat = m2 / (1.0 - ADAM_B1 ** ADAM_STEP)
                v_hat = v2 / (1.0 - ADAM_B2 ** ADAM_STEP)
                go_ref[...] = g
                d_ref[...] = -ADAM_LR * (m_hat / (jnp.sqrt(v_hat) + ADAM_EPS) + ADAM_WD * w_ref[...])
                mo_ref[...] = m2
                vo_ref[...] = v2

    g_specs = [pl.BlockSpec((n, tr, C), lambda l, i, k=k: (0, jnp.where(l == k, i, 0), 0)) for k in range(L)]
    blk = pl.BlockSpec((None, tr, C), lambda l, i: (l, i, 0))
    return pl.pallas_call(
        body, name=name, grid=(L, A // tr), in_specs=g_specs + [blk, blk, blk] + [_dep_spec(2)] * len(deps),
        out_specs=[blk] * 4, out_shape=[_sds((L, A, C), F32)] * 4, compiler_params=_params(("parallel", "parallel")),
    )(*gsrcs, w, m, v, *deps)


MESH = pl.DeviceIdType.MESH
HBM_SPEC = pl.BlockSpec(memory_space=pltpu.HBM)
N_PEERS = N_DEV - 1


def _mesh_place():
    x, y, c = lax.axis_index("x"), lax.axis_index("y"), lax.axis_index("c")
    peers = []
    for p in range(1, N_DEV):
        px = 1 - x if p & 4 else x
        py = 1 - y if p & 2 else y
        pc = 1 - c if p & 1 else c
        peers.append(((px, py, pc), 4 * px + 2 * py + pc))
    return 4 * x + 2 * y + c, peers


SEM_SPEC = pl.BlockSpec(memory_space=pltpu.SEMAPHORE)
ANY_SPEC = pl.BlockSpec(memory_space=pl.ANY)
EFFECT = pltpu.SideEffectType.DATAFLOW_SIDE_EFFECTING


def _scatters(scatter, k):
    return scatter if isinstance(scatter, bool) else scatter[k]


def _exchange_refs(scatter, src, land, send, recv, k, p, dev, idx, me):
    return pltpu.make_async_remote_copy(src_ref=src[k].at[idx] if _scatters(scatter, k) else src[k], dst_ref=land[k].at[me],
                                        send_sem=send.at[k * N_PEERS + p], recv_sem=recv.at[k * N_PEERS + p], device_id=dev,
                                        device_id_type=MESH)


def _exchange_start(name, srcs, scatter, gate):
    n = len(srcs)
    lands = [lax.empty(s.shape if _scatters(scatter, k) else (N_DEV,) + s.shape, s.dtype) for k, s in enumerate(srcs)]

    def body(*refs):
        src, land = refs[:n], refs[n:2 * n]
        send, recv, own = refs[2 * n + 1:2 * n + 4]
        token = refs[-1]
        me, peers = _mesh_place()
        for k in range(n):
            pltpu.make_async_copy(src[k].at[me] if _scatters(scatter, k) else src[k], land[k].at[me], own.at[k]).start()
            for p, (dev, idx) in enumerate(peers):
                _exchange_refs(scatter, src, land, send, recv, k, p, dev, idx, me).start()
        token[...] = jnp.zeros_like(token)

    hbm = lambda a: pltpu.HBM(a.shape, a.dtype)
    outs = pl.pallas_call(
        body, name=name,
        out_shape=(pltpu.SemaphoreType.DMA((n * N_PEERS,)), pltpu.SemaphoreType.DMA((n * N_PEERS,)),
                   pltpu.SemaphoreType.DMA((n,)), *[hbm(s) for s in srcs], *[hbm(s) for s in lands], _sds(DEP_SHAPE, F32)),
        in_specs=[HBM_SPEC] * (2 * n) + [ANY_SPEC],
        out_specs=(SEM_SPEC, SEM_SPEC, SEM_SPEC, *[HBM_SPEC] * (2 * n), pl.BlockSpec(memory_space=pltpu.VMEM)),
        input_output_aliases={j: 3 + j for j in range(2 * n)},
        compiler_params=pltpu.CompilerParams(has_side_effects=EFFECT),
    )(*[pltpu.with_memory_space_constraint(s, pltpu.HBM) for s in srcs],
      *[pltpu.with_memory_space_constraint(s, pltpu.HBM) for s in lands], gate)
    return outs[:3], None, list(outs[3:3 + n]), list(outs[3 + n:3 + 2 * n]), outs[-1]


def _exchange_wait(name, started, scatter, after):
    (send, recv, own), _, srcs, lands, _ = started
    n = len(srcs)

    def body(*refs):
        src, land = refs[:n], refs[n:2 * n]
        send, recv, own = refs[2 * n:2 * n + 3]
        me, peers = _mesh_place()
        for k in range(n):
            pltpu.make_async_copy(src[k].at[me] if _scatters(scatter, k) else src[k], land[k].at[me], own.at[k]).wait()
            for p, (dev, idx) in enumerate(peers):
                cp = pltpu.make_async_remote_copy(src_ref=src[k].at[idx] if _scatters(scatter, k) else src[k], dst_ref=land[k].at[idx],
                                                  send_sem=send.at[k * N_PEERS + p], recv_sem=recv.at[k * N_PEERS + p], device_id=dev,
                                                  device_id_type=MESH)
                cp.wait_send()
                cp.wait_recv()

    hbm = lambda a: pltpu.HBM(a.shape, a.dtype)
    outs = pl.pallas_call(
        body, name=name, out_shape=(*[hbm(s) for s in srcs], *[hbm(s) for s in lands]),
        in_specs=[HBM_SPEC] * (2 * n) + [SEM_SPEC, SEM_SPEC, SEM_SPEC, ANY_SPEC], out_specs=tuple([HBM_SPEC] * (2 * n)),
        input_output_aliases={j: j for j in range(2 * n)},
        compiler_params=pltpu.CompilerParams(has_side_effects=EFFECT),
    )(*srcs, *lands, send, recv, own, after)
    return list(outs[n:])


def _sum_devices(name, parts):
    def body(p_ref, o_ref):
        tot = p_ref[0]
        for j in range(1, N_DEV):
            tot = tot + p_ref[j]
        o_ref[...] = tot

    return pl.pallas_call(body, name=name, out_shape=_sds(parts.shape[1:], F32),
                          compiler_params=pltpu.CompilerParams(vmem_limit_bytes=VMEM_LIMIT))(parts)


def _rows(a, width=D_MODEL):
    flat = a.reshape(-1)
    return jnp.pad(flat, (0, (-flat.shape[0]) % width)).reshape(-1, width)


def _pack_rows(parts):
    blocks = []
    for p in parts:
        r = _rows(p)
        blocks.append(jnp.pad(r, ((0, (-r.shape[0]) % 8), (0, 0))))
    return jnp.concatenate(blocks, axis=0)


def _unpack_rows(rows, shapes):
    out, at = [], 0
    for s in shapes:
        size = int(np.prod(s))
        n = -(-size // D_MODEL)
        out.append(rows[at:at + n].reshape(-1)[:size].reshape(s))
        at += -(-n // 8) * 8
    return out


def kernel(x, norm_mix, norm_ffn, norm_final, attn_w_in, attn_w_out, attn_sinks, hgrn_w_in, hgrn_w_out, hgrn_norm, hgrn_lb_logits, ffn_w_up, ffn_conv_w, ffn_conv_b, ffn_w_down, loss_target, m_norm_mix, m_norm_ffn, m_norm_final, m_attn_w_in, m_attn_w_out, m_attn_sinks, m_hgrn_w_in, m_hgrn_w_out, m_hgrn_norm, m_hgrn_lb_logits, m_ffn_w_up, m_ffn_conv_w, m_ffn_conv_b, m_ffn_w_down, v_norm_mix, v_norm_ffn, v_norm_final, v_attn_w_in, v_attn_w_out, v_attn_sinks, v_hgrn_w_in, v_hgrn_w_out, v_hgrn_norm, v_hgrn_lb_logits, v_ffn_w_up, v_ffn_conv_w, v_ffn_conv_b, v_ffn_w_down):
    S = x.shape[1]
    n_attn, n_hgrn = attn_w_in.shape[0], hgrn_w_in.shape[0]

    wa_in_t, wa_out_b = attn_w_in.transpose(0, 2, 1).astype(BF16), attn_w_out.astype(BF16)
    wh_in_b, wh_out_b = hgrn_w_in.astype(BF16), hgrn_w_out.astype(BF16)
    wf_up_b, wf_down_b = ffn_w_up.transpose(0, 2, 1).astype(BF16), ffn_w_down.astype(BF16)
    conv_b = ffn_conv_b.reshape(DEPTH, 2, 4, 1, FF_SLOT)
    lb = _lb_fwd("lb_fwd", hgrn_lb_logits)

    def unit_shards(l, part):
        if part == "ffn":
            return [wf_up_b[l], wf_down_b[l], ffn_conv_w[l]]
        return [wa_in_t[l // 2], wa_out_b[l // 2]] if l % 2 == 0 else [wh_in_b[l // 2], wh_out_b[l // 2]]

    def unit_weights(l, part, w):
        if part == "ffn":
            return w[0][None], w[1].reshape(1, 4, FF_SLOT, D_MODEL), w[2].reshape(2, 4, 3, FF_SLOT)
        if l % 2 == 0:
            return w[0].reshape(1, ATTN_IN, D_MODEL), w[1].reshape(1, D_MODEL, D_MODEL)
        return w[0][None], w[1].reshape(1, D_MODEL, D_MODEL)

    units = [(l, part) for l in range(DEPTH) for part in ("mix", "ffn")]
    gathers = [_exchange_start("gather_start0", unit_shards(*units[0]), False, norm_final)]
    arrived = _exchange_wait("gather_wait0", gathers[0], False, gathers[0][4])
    gathers.append(_exchange_start("gather_start1", unit_shards(*units[1]), False, arrived[0]))
    weights, saved = {}, [dict() for _ in range(DEPTH)]
    h = x[0]
    hn = _rmsnorm_fwd("norm_mix_fwd0", h, norm_mix[0:1])
    for n, (l, part) in enumerate(units):
        i, sv = l // 2, saved[l]
        weights[l, part] = w = unit_weights(l, part, arrived)
        dep = None
        if n + 2 < len(units):
            gate = gathers[1][4] if n == 0 else arrived[0]
            gathers.append(_exchange_start(f"gather_start{n + 2}", unit_shards(*units[n + 2]), False, gate))
            dep = gathers[n + 2][4]
        if part == "mix":
            sv["h"], sv["hn"] = h, hn
            if l % 2 == 0:
                sv["proj"] = _proj_rows(f"attn_proj{i}", hn, w[0], 0, BF16, dep)
                sv["o"], *sv["kept"] = _attn_fwd(f"attn_fwd{i}", sv["proj"], attn_sinks[i:i + 1])
                h, hn = _out_proj(f"attn_out{i}", sv["o"], w[1], 0, h, norm_ffn[l:l + 1])
            else:
                sv["z"] = _proj_slots(f"hgrn_proj{i}", hn, w[0], 0, dep=dep).reshape(4, 2, S, HG_SLOT)
                sv["o"], *sv["kept"] = _hg_fwd(f"hgrn_fwd{i}", sv["z"], lb[i:i + 1], hgrn_norm[i:i + 1])
                h, hn = _out_proj(f"hgrn_out{i}", sv["o"], w[1], 0, h, norm_ffn[l:l + 1])
        else:
            sv["h2"], sv["hn2"] = h, hn
            sv["u"] = _proj_slots(f"ffn_up{l}", hn, w[0], 0, True, dep, BF16).reshape(2, 4, S, FF_SLOT)
            sv["a"], sv["convs"] = _convgate_fwd(f"ffn_gate{l}", sv["u"], w[2], conv_b[l])
            if l + 1 < DEPTH:
                h, hn = _down_proj(f"ffn_down{l}", sv["a"], w[1], 0, h, norm_mix[l + 1:l + 2])
            else:
                h = _down_proj(f"ffn_down{l}", sv["a"], w[1], 0, h)
        if n + 1 < len(units):
            arrived = _exchange_wait(f"gather_wait{n + 1}", gathers[n + 1], False, h)
    dh, d_norm_final, loss_rows, dhb = _loss_head("loss_head", h, norm_final[None], loss_target[0])

    d_conv_w, d_conv_b, d_norm_mix, d_norm_ffn = [None] * DEPTH, [None] * DEPTH, [None] * DEPTH, [None] * DEPTH
    d_sinks, d_lb, d_hgrn_norm = [None] * n_attn, [None] * n_hgrn, [None] * n_hgrn
    received, pending = {}, []
    for l, part in reversed(units):
        i, sv, w = l // 2, saved[l], weights[l, part]
        dep = pending[-1][1][4] if pending else None
        if part == "ffn":
            da = _dgrad_down(f"ffn_down_dgrad{l}", dhb, w[1], 0, dep)
            g_down = _wgrad_down(f"ffn_down_wgrad{l}", sv["a"], dhb).reshape(N_DEV, D_FF // N_DEV, D_MODEL)
            du, d_conv_w[l], d_conv_b[l] = _convgate_bwd(f"ffn_gate_bwd{l}", sv["u"], sv["convs"], w[2], da)
            du = du.reshape(N_DEV, S, FF_SLOT)
            grads = [_wgrad_slots(f"ffn_up_wgrad{l}", sv["hn2"], du, True), g_down]
            dh, d_norm_ffn[l], dhb = _dgrad_slots(f"ffn_up_dgrad{l}", du, w[0], 0, (sv["h2"], norm_ffn[l:l + 1], dh), True)
        else:
            if l % 2 == 0:
                do = _dgrad_out(f"attn_out_dgrad{i}", dhb, w[1], 0, BF16, dep)
                g_out = _wgrad_rows(f"attn_out_wgrad{i}", sv["o"], dhb)
                dproj, d_sinks[i] = _attn_bwd(f"attn_bwd{i}", sv["proj"], *sv["kept"], do)
                g_in = _wgrad_rows(f"attn_proj_wgrad{i}", dproj, sv["hn"]).reshape(N_DEV, ATTN_IN // N_DEV, D_MODEL)
                dh_new = _dgrad_rows(f"attn_proj_dgrad{i}", dproj, w[0], 0, (sv["h"], norm_mix[l:l + 1], dh))
            else:
                dog = _dgrad_out(f"hgrn_out_dgrad{i}", dhb, w[1], 0, F32, dep)
                g_out = _wgrad_rows(f"hgrn_out_wgrad{i}", sv["o"], dhb)
                dz, d_lb[i], dng = _hg_bwd(f"hgrn_bwd{i}", sv["z"], lb[i:i + 1], hgrn_norm[i:i + 1], *sv["kept"], dog)
                d_hgrn_norm[i] = dng[0] + dng[1]
                dz = dz.reshape(N_DEV, S, HG_SLOT)
                g_in = _wgrad_slots(f"hgrn_proj_wgrad{i}", sv["hn"], dz)
                dh_new = _dgrad_slots(f"hgrn_proj_dgrad{i}", dz, w[0], 0, (sv["h"], norm_mix[l:l + 1], dh))
            grads = [g_in, g_out.reshape(N_DEV, D_MODEL // N_DEV, D_MODEL)]
            dh, d_norm_mix[l], dhb = dh_new
        gate = dh
        if len(pending) == 2:
            key, oldest = pending.pop(0)
            received[key] = _exchange_wait(f"scatter_wait_{key[1]}{key[0]}", oldest, True, dh)
            gate = received[key][0]
        pending.append(((l, part), _exchange_start(f"scatter_start_{part}{l}", grads, True, gate)))
    grad_x = dh[None]

    small_shapes = [(DEPTH, D_MODEL), (DEPTH, D_MODEL), (1, D_MODEL), (1, D_MODEL), (n_hgrn, D_MODEL), (n_attn, 128),
                    (n_hgrn, HG_K), (DEPTH, 2 * D_FF)]
    partial = _pack_rows([
        jnp.concatenate(d_norm_mix), jnp.concatenate(d_norm_ffn), d_norm_final, loss_rows, jnp.concatenate(d_lb),
        jnp.concatenate(d_sinks), jnp.concatenate(d_hgrn_norm), jnp.stack(d_conv_b)])
    d_taps = jnp.stack(d_conv_w).reshape(DEPTH, N_DEV, 3, FF_SLOT).transpose(1, 0, 2, 3).reshape(N_DEV, DEPTH * 3, FF_SLOT)
    small_started = _exchange_start("small_start", [partial, d_taps], (False, True), pending[-1][1][4])
    attn_layers, hgrn_layers = range(0, DEPTH, 2), range(1, DEPTH, 2)

    def transposed(ts):
        return [t.transpose(0, 2, 1) for t in ts]

    big = {"hgrn_w_in": _adamw("adamw_hgrn_in", [received[l, "mix"][0] for l in hgrn_layers], hgrn_w_in, m_hgrn_w_in,
                               v_hgrn_w_in, dep=small_started[4])}
    big["hgrn_w_out"] = _adamw("adamw_hgrn_out", [received[l, "mix"][1] for l in hgrn_layers], hgrn_w_out, m_hgrn_w_out, v_hgrn_w_out)
    key, oldest = pending.pop(0)
    received[key] = _exchange_wait(f"scatter_wait_{key[1]}{key[0]}", oldest, True, big["hgrn_w_in"][3])
    up_t = _adamw("adamw_ffn_up", [received[l, "ffn"][0] for l in range(DEPTH)], *transposed((ffn_w_up, m_ffn_w_up, v_ffn_w_up)))
    big["ffn_w_up"] = transposed(up_t)
    big["ffn_w_down"] = _adamw("adamw_ffn_down", [received[l, "ffn"][1] for l in range(DEPTH)], ffn_w_down, m_ffn_w_down, v_ffn_w_down)
    key, oldest = pending.pop(0)
    received[key] = _exchange_wait(f"scatter_wait_{key[1]}{key[0]}", oldest, True, up_t[3])
    small_parts, taps_parts = _exchange_wait("small_wait", small_started, (False, True), up_t[3])
    total = _sum_devices("sum_small", small_parts)
    (g_norm_mix, g_norm_ffn, g_norm_final, loss_sum, g_lb, g_sinks, g_hgrn_norm, g_conv_b) = _unpack_rows(total, small_shapes)

    loss = jnp.sum(loss_sum)
    g_norm_final = g_norm_final[0]
    g_sinks = g_sinks[:, :N_Q_HEADS]
    g_lb_logits = _lb_bwd("lb_bwd", hgrn_lb_logits, g_lb)

    big.update({
        "attn_w_in": transposed(_adamw("adamw_attn_in", [received[l, "mix"][0] for l in attn_layers],
                                       *transposed((attn_w_in, m_attn_w_in, v_attn_w_in)))),
        "attn_w_out": _adamw("adamw_attn_out", [received[l, "mix"][1] for l in attn_layers], attn_w_out, m_attn_w_out, v_attn_w_out),
        "ffn_conv_w": _adamw("adamw_conv_w", [taps_parts[:, 3 * l:3 * l + 3] for l in range(DEPTH)], ffn_conv_w, m_ffn_conv_w,
                             v_ffn_conv_w),
    })
    small_w = [norm_mix, norm_ffn, norm_final, attn_sinks, hgrn_norm, hgrn_lb_logits, ffn_conv_b]
    small_m = [m_norm_mix, m_norm_ffn, m_norm_final, m_attn_sinks, m_hgrn_norm, m_hgrn_lb_logits, m_ffn_conv_b]
    small_v = [v_norm_mix, v_norm_ffn, v_norm_final, v_attn_sinks, v_hgrn_norm, v_hgrn_lb_logits, v_ffn_conv_b]
    small_g = [g_norm_mix, g_norm_ffn, g_norm_final, g_sinks, g_hgrn_norm, g_lb_logits, g_conv_b]
    outs = _adamw("adamw_small", [_pack_rows(small_g)[None]], *[_pack_rows(t)[None] for t in (small_w, small_m, small_v)])
    outs = [o[0] for o in outs]
    shapes = [w.shape for w in small_w]
    small = {n: [t[j] for t in [_unpack_rows(o, shapes) for o in outs]]
             for j, n in enumerate(["norm_mix", "norm_ffn", "norm_final", "attn_sinks", "hgrn_norm", "hgrn_lb_logits", "ffn_conv_b"])}
    order = ["norm_mix", "norm_ffn", "norm_final", "attn_w_in", "attn_w_out", "attn_sinks", "hgrn_w_in", "hgrn_w_out",
             "hgrn_norm", "hgrn_lb_logits", "ffn_w_up", "ffn_conv_w", "ffn_conv_b", "ffn_w_down"]
    res = {**big, **small}
    return (loss, grad_x, *[res[n][0] for n in order], *[res[n][1] for n in order], *[res[n][2] for n in order],
            *[res[n][3] for n in order])
```
